```python
import jax, jax.numpy as jnp
from jax import lax
import numpy as np

D_MODEL = 1024
BATCH = 16
SEQ = 2048
DEPTH = 1
DEC_BATCH = 32
DEC_SEQ = 64
PAST_LEN = 1024

CHUNK = 64
RET_HEADS = 4
RET_DK = 128
RET_DV = 256
RET_ROPE_BASE = 10000.0
DSA_HEADS = 8
DSA_KV_HEADS = 2
DSA_HEAD_DIM = 64
IDX_HEADS = 8
IDX_DIM = 64
DSA_TOPK = 256
ROPE_THETA = 500000.0
N_EXPERTS = 32
MOE_TOP_K = 4
D_FF = 1024
SWIGLU_ALPHA = 1.702
SWIGLU_LIMIT = 7.0
MOE_BLOCK = 128
LN_EPS = 1e-5
GN_EPS = 1e-6
DEEPNORM_ALPHA = (2.0 * DEPTH) ** 0.25
DEEPNORM_BETA = (8.0 * DEPTH) ** -0.25
PROJ_WIDTHS = (RET_HEADS * RET_DK, RET_HEADS * RET_DK, RET_HEADS * RET_DV, RET_HEADS * RET_DV,
               DSA_HEADS * DSA_HEAD_DIM, DSA_KV_HEADS * DSA_HEAD_DIM, DSA_KV_HEADS * DSA_HEAD_DIM,
               IDX_HEADS * IDX_DIM, IDX_DIM, IDX_HEADS, D_MODEL, D_MODEL)
PROJ_TOTAL = sum(PROJ_WIDTHS)

kernel_name = 'streaming_retention_dsa_moe_step'


def layer_norm(x, g, b):
    xf = x.astype(jnp.float32)
    mu = jnp.mean(xf, -1, keepdims=True)
    var = jnp.mean(jnp.square(xf - mu), -1, keepdims=True)
    return ((xf - mu) * lax.rsqrt(var + LN_EPS) * g + b).astype(x.dtype)


def apply_rotary(x, pos, inv_freq):
    half = inv_freq.shape[0]
    ang = pos.astype(jnp.float32)[:, None] * inv_freq[None, :]
    cos = jnp.cos(ang)[None, :, None, :]
    sin = jnp.sin(ang)[None, :, None, :]
    xf = x.astype(jnp.float32)
    x1, x2, rest = xf[..., :half], xf[..., half:2 * half], xf[..., 2 * half:]
    out = jnp.concatenate([x1 * cos - x2 * sin, x1 * sin + x2 * cos, rest], -1)
    return out.astype(x.dtype)


def retention_inv_freq():
    return RET_ROPE_BASE ** (-jnp.linspace(0.0, 1.0, RET_DK // 2, dtype=jnp.float32))


def partial_inv_freq(head_dim):
    n_rot = head_dim // 4
    return ROPE_THETA ** (-jnp.arange(0, n_rot, 2, dtype=jnp.float32) / n_rot)


def retention_gammas():
    return 1.0 - 2.0 ** (-5.0 - jnp.arange(RET_HEADS, dtype=jnp.float32))


def project_and_split(x, w_in, pos):
    b, t, _ = x.shape
    z = jnp.einsum('btd,de->bte', x, w_in)
    cuts = np.cumsum(PROJ_WIDTHS)[:-1].tolist()
    rq, rk, rv, rg, aq, ak, av, iq, ik, iw, gr, ga = jnp.split(z, cuts, axis=-1)
    ret_f = retention_inv_freq()
    rq = apply_rotary(rq.reshape(b, t, RET_HEADS, RET_DK), pos, ret_f)
    rk = apply_rotary(rk.reshape(b, t, RET_HEADS, RET_DK), pos, ret_f) * (RET_DK ** -0.5)
    rv = rv.reshape(b, t, RET_HEADS, RET_DV)
    att_f = partial_inv_freq(DSA_HEAD_DIM)
    aq = apply_rotary(aq.reshape(b, t, DSA_HEADS, DSA_HEAD_DIM), pos, att_f)
    ak = apply_rotary(ak.reshape(b, t, DSA_KV_HEADS, DSA_HEAD_DIM), pos, att_f)
    av = av.reshape(b, t, DSA_KV_HEADS, DSA_HEAD_DIM)
    idx_f = partial_inv_freq(IDX_DIM)
    iq = apply_rotary(iq.reshape(b, t, IDX_HEADS, IDX_DIM), pos, idx_f)
    ik = apply_rotary(ik.reshape(b, t, 1, IDX_DIM), pos, idx_f)[:, :, 0]
    iw = iw * (IDX_HEADS ** -0.5)
    return (rq, rk, rv, rg, aq, ak, av, iq, ik, iw, gr, ga)


def retention_chunk(q, k, v, s0):
    t = q.shape[1]
    log_g = jnp.log(retention_gammas())
    pos = jnp.arange(t, dtype=jnp.float32)
    diff = pos[:, None] - pos[None, :]
    decay = jnp.where(diff >= 0, jnp.exp(jnp.maximum(diff, 0.0)[None] * log_g[:, None, None]), 0.0)
    scores = jnp.einsum('bthd,bshd->bhts', q, k).astype(jnp.float32) * decay[None]
    o = jnp.einsum('bhts,bshe->bthe', scores, v.astype(jnp.float32))
    xi = jnp.exp((pos + 1.0)[:, None] * log_g[None, :])
    o = o + jnp.einsum('bthd,bhde->bthe', q.astype(jnp.float32), s0) * xi[None, :, :, None]
    zeta = jnp.exp((t - 1.0 - pos)[:, None] * log_g[None, :])
    s_new = (jnp.exp(t * log_g)[None, :, None, None] * s0
             + jnp.einsum('bthd,bthe->bhde', k.astype(jnp.float32) * zeta[None, :, :, None], v.astype(jnp.float32)))
    return o, s_new


def dsa_select_attend(q, iq, iw, k, v, ik, limit, n_sel):
    b, nq = q.shape[:2]
    n_keys = k.shape[1]
    idx = jnp.einsum('bqhd,bld->bqhl', iq, ik).astype(jnp.float32) * (IDX_DIM ** -0.5)
    idx = jnp.einsum('bqhl,bqh->bql', jax.nn.relu(idx), iw.astype(jnp.float32))
    admissible = jnp.arange(n_keys) < limit
    idx = jnp.where(admissible[None, None, :], idx, -jnp.inf)
    top_s, sel = lax.top_k(idx, n_sel)
    valid = jnp.isfinite(top_s)
    take = jax.vmap(lambda rows, ids: rows[ids])
    k_sel = take(k, sel)
    v_sel = take(v, sel)
    qg = q.reshape(b, nq, DSA_KV_HEADS, DSA_HEADS // DSA_KV_HEADS, DSA_HEAD_DIM)
    logits = jnp.einsum('bqgrd,bqngd->bqgrn', qg, k_sel).astype(jnp.float32) * (DSA_HEAD_DIM ** -0.5)
    logits = jnp.where(valid[:, :, None, None, :], logits, -jnp.inf)
    p = jax.nn.softmax(logits, axis=-1)
    out = jnp.einsum('bqgrn,bqngd->bqgrd', p, v_sel.astype(jnp.float32))
    return out.reshape(b, nq, DSA_HEADS * DSA_HEAD_DIM)


def merge_branches(x, o_ret, rg, o_dsa, gr, ga, w_ret_o, w_dsa_o, w_o):
    b, t = x.shape[:2]
    mu = jnp.mean(o_ret, -1, keepdims=True)
    var = jnp.mean(jnp.square(o_ret - mu), -1, keepdims=True)
    gn = (o_ret - mu) * lax.rsqrt(var + GN_EPS)
    ret = gn.reshape(b, t, RET_HEADS * RET_DV) * jax.nn.silu(rg.astype(jnp.float32))
    y_ret = jnp.einsum('bte,ed->btd', ret.astype(x.dtype), w_ret_o)
    y_dsa = jnp.einsum('bte,ed->btd', o_dsa.astype(x.dtype), w_dsa_o)
    merged = jax.nn.sigmoid(gr) * y_ret + jax.nn.sigmoid(ga) * y_dsa
    return jnp.einsum('btd,de->bte', merged, w_o)


def token_mixer_prompt(x, w_in, w_ret_o, w_dsa_o, w_o):
    b, s, _ = x.shape
    nc = s // CHUNK
    pos = jnp.arange(s)
    rq, rk, rv, rg, aq, ak, av, iq, ik, iw, gr, ga = project_and_split(x, w_in, pos)

    def chunks(a):
        return a.reshape(b, nc, CHUNK, *a.shape[2:]).swapaxes(0, 1)

    def unchunk(a):
        return a.swapaxes(0, 1).reshape(b, s, *a.shape[3:])

    def ret_step(state, blk):
        o, state = retention_chunk(blk[0], blk[1], blk[2], state)
        return state, o

    s0 = jnp.zeros((b, RET_HEADS, RET_DK, RET_DV), jnp.float32)
    s_fin, o_ret = lax.scan(ret_step, s0, (chunks(rq), chunks(rk), chunks(rv)))
    o_ret = unchunk(o_ret)
    n_sel = min(DSA_TOPK, s // 4)

    def dsa_block(blk):
        qb, iqb, iwb, ci = blk
        return dsa_select_attend(qb, iqb, iwb, ak, av, ik, (ci + 1) * CHUNK, n_sel)

    o_dsa = unchunk(lax.map(dsa_block, (chunks(aq), chunks(iq), chunks(iw), jnp.arange(nc))))
    y = merge_branches(x, o_ret, rg, o_dsa, gr, ga, w_ret_o, w_dsa_o, w_o)
    return y, s_fin, ak, av, ik


def token_mixer_sample(x, s_ret, past_k, past_v, past_ik, w_in, w_ret_o, w_dsa_o, w_o):
    t = x.shape[1]
    p = past_k.shape[1]
    pos = p + jnp.arange(t)
    rq, rk, rv, rg, aq, ak, av, iq, ik, iw, gr, ga = project_and_split(x, w_in, pos)
    o_ret, s_new = retention_chunk(rq, rk, rv, s_ret.astype(jnp.float32))
    k_all = jnp.concatenate([past_k, ak.astype(past_k.dtype)], axis=1)
    v_all = jnp.concatenate([past_v, av.astype(past_v.dtype)], axis=1)
    ik_all = jnp.concatenate([past_ik, ik.astype(past_ik.dtype)], axis=1)
    n_keys = p + t
    n_sel = min(DSA_TOPK, n_keys // 4)
    o_dsa = dsa_select_attend(aq, iq, iw, k_all, v_all, ik_all, n_keys, n_sel)
    y = merge_branches(x, o_ret, rg, o_dsa, gr, ga, w_ret_o, w_dsa_o, w_o)
    return y, s_new, ak, av, ik


def clamped_swiglu(h):
    h = h.astype(jnp.float32)
    glu = jnp.minimum(h[..., ::2], SWIGLU_LIMIT)
    lin = jnp.clip(h[..., 1::2], -SWIGLU_LIMIT, SWIGLU_LIMIT)
    return glu * jax.nn.sigmoid(SWIGLU_ALPHA * glu) * (lin + 1.0)


def moe_ffn(x, router_w, router_b, w_up, b_up, w_down, b_down):
    lead = x.shape[:-1]
    xf = x.reshape(-1, D_MODEL)
    n = xf.shape[0]
    logits = (xf @ router_w + router_b).astype(jnp.float32)
    top_v, top_e = lax.top_k(logits, MOE_TOP_K)
    gates = jax.nn.softmax(top_v, axis=-1)
    flat_e = top_e.reshape(-1)
    flat_t = jnp.repeat(jnp.arange(n, dtype=jnp.int32), MOE_TOP_K)
    flat_g = gates.reshape(-1)
    order = jnp.argsort(flat_e)
    e_sorted = flat_e[order]
    counts = jnp.zeros((N_EXPERTS,), jnp.int32).at[flat_e].add(1)
    padded = (counts + MOE_BLOCK - 1) // MOE_BLOCK * MOE_BLOCK
    pad_end = jnp.cumsum(padded)
    pad_start = pad_end - padded
    start = jnp.cumsum(counts) - counts
    dest = pad_start[e_sorted] + jnp.arange(n * MOE_TOP_K, dtype=jnp.int32) - start[e_sorted]
    n_blocks = -(-(n * MOE_TOP_K) // MOE_BLOCK) + N_EXPERTS
    rows = n_blocks * MOE_BLOCK
    row_tok = jnp.full((rows,), n, jnp.int32).at[dest].set(flat_t[order])
    row_gate = jnp.zeros((rows,), jnp.float32).at[dest].set(flat_g[order])
    block_e = jnp.minimum(jnp.searchsorted(pad_end, jnp.arange(n_blocks, dtype=jnp.int32) * MOE_BLOCK, side='right'),
                          N_EXPERTS - 1)
    x_pad = jnp.concatenate([xf, jnp.zeros((1, D_MODEL), xf.dtype)], axis=0)
    xb = x_pad[row_tok].reshape(n_blocks, MOE_BLOCK, D_MODEL)

    def expert_block(args):
        xblk, e = args
        h = xblk @ w_up[e] + b_up[e]
        a = clamped_swiglu(h).astype(xblk.dtype)
        return a @ w_down[e] + b_down[e]

    yb = lax.map(expert_block, (xb, block_e)).reshape(rows, D_MODEL)
    y = jnp.zeros((n + 1, D_MODEL), jnp.float32).at[row_tok].add(yb.astype(jnp.float32) * row_gate[:, None])
    return y[:n].astype(x.dtype).reshape(*lead, D_MODEL)


def setup_inputs(seed: int = 0) -> dict:
    key = jax.random.key(seed)
    ks = jax.random.split(key, 20)
    f32 = jnp.float32

    def nrm(k, shape, scale):
        return jax.random.normal(k, shape, f32) * scale

    return {
        'x_prompt': nrm(ks[0], (BATCH, SEQ, D_MODEL), 1.0),
        'x_sample': nrm(ks[1], (DEC_BATCH, DEC_SEQ, D_MODEL), 1.0),
        'state_ret': nrm(ks[2], (DEPTH, DEC_BATCH, RET_HEADS, RET_DK, RET_DV), 0.5),
        'cache_k': nrm(ks[3], (DEPTH, DEC_BATCH, PAST_LEN, DSA_KV_HEADS, DSA_HEAD_DIM), 1.0),
        'cache_v': nrm(ks[4], (DEPTH, DEC_BATCH, PAST_LEN, DSA_KV_HEADS, DSA_HEAD_DIM), 1.0),
        'cache_idx_k': nrm(ks[5], (DEPTH, DEC_BATCH, PAST_LEN, IDX_DIM), 1.0),
        'w_in': nrm(ks[6], (DEPTH, D_MODEL, PROJ_TOTAL), D_MODEL ** -0.5),
        'w_ret_o': nrm(ks[7], (DEPTH, RET_HEADS * RET_DV, D_MODEL), DEEPNORM_BETA * (RET_HEADS * RET_DV) ** -0.5),
        'w_dsa_o': nrm(ks[8], (DEPTH, DSA_HEADS * DSA_HEAD_DIM, D_MODEL), DEEPNORM_BETA * (DSA_HEADS * DSA_HEAD_DIM) ** -0.5),
        'w_o': nrm(ks[9], (DEPTH, D_MODEL, D_MODEL), DEEPNORM_BETA * D_MODEL ** -0.5),
        'ln1_g': 1.0 + nrm(ks[10], (DEPTH, D_MODEL), 0.01),
        'ln1_b': nrm(ks[11], (DEPTH, D_MODEL), 0.01),
        'router_w': nrm(ks[12], (DEPTH, D_MODEL, N_EXPERTS), D_MODEL ** -0.5),
        'router_b': nrm(ks[13], (DEPTH, N_EXPERTS), 0.01),
        'w_up': nrm(ks[14], (DEPTH, N_EXPERTS, D_MODEL, 2 * D_FF), D_MODEL ** -0.5),
        'b_up': nrm(ks[15], (DEPTH, N_EXPERTS, 2 * D_FF), 0.01),
        'w_down': nrm(ks[16], (DEPTH, N_EXPERTS, D_FF, D_MODEL), DEEPNORM_BETA * D_FF ** -0.5),
        'b_down': nrm(ks[17], (DEPTH, N_EXPERTS, D_MODEL), 0.01),
        'ln2_g': 1.0 + nrm(ks[18], (DEPTH, D_MODEL), 0.01),
        'ln2_b': nrm(ks[19], (DEPTH, D_MODEL), 0.01),
    }


def reference(x_prompt, x_sample, state_ret, cache_k, cache_v, cache_idx_k, w_in, w_ret_o, w_dsa_o, w_o,
              ln1_g, ln1_b, router_w, router_b, w_up, b_up, w_down, b_down, ln2_g, ln2_b):
    hp, hs = x_prompt, x_sample
    sp_l, kp_l, vp_l, ikp_l = [], [], [], []
    ss_l, ks_l, vs_l, iks_l = [], [], [], []
    for l in range(DEPTH):
        mix_p, s_p, k_p, v_p, ik_p = token_mixer_prompt(hp, w_in[l], w_ret_o[l], w_dsa_o[l], w_o[l])
        hp = layer_norm(DEEPNORM_ALPHA * hp + mix_p, ln1_g[l], ln1_b[l])
        hp = layer_norm(DEEPNORM_ALPHA * hp + moe_ffn(hp, router_w[l], router_b[l], w_up[l], b_up[l], w_down[l], b_down[l]),
                        ln2_g[l], ln2_b[l])
        mix_s, s_s, k_s, v_s, ik_s = token_mixer_sample(hs, state_ret[l], cache_k[l], cache_v[l], cache_idx_k[l],
                                                        w_in[l], w_ret_o[l], w_dsa_o[l], w_o[l])
        hs = layer_norm(DEEPNORM_ALPHA * hs + mix_s, ln1_g[l], ln1_b[l])
        hs = layer_norm(DEEPNORM_ALPHA * hs + moe_ffn(hs, router_w[l], router_b[l], w_up[l], b_up[l], w_down[l], b_down[l]),
                        ln2_g[l], ln2_b[l])
        sp_l.append(s_p); kp_l.append(k_p); vp_l.append(v_p); ikp_l.append(ik_p)
        ss_l.append(s_s); ks_l.append(k_s); vs_l.append(v_s); iks_l.append(ik_s)
    return (hp, hs, jnp.stack(sp_l), jnp.stack(kp_l), jnp.stack(vp_l), jnp.stack(ikp_l),
            jnp.stack(ss_l), jnp.stack(ks_l), jnp.stack(vs_l), jnp.stack(iks_l))
```

```python
import functools

import numpy as np
import jax
import jax.numpy as jnp
from jax import lax
from jax.experimental import pallas as pl
from jax.experimental.pallas import tpu as pltpu

F32 = jnp.float32
BF16 = jnp.bfloat16
I32 = jnp.int32

D_MODEL = 1024
CHUNK = 64
RET_HEADS = 4
RET_DK = 128
RET_DV = 256
RET_ROPE_BASE = 10000.0
DSA_HEADS = 8
DSA_KV_HEADS = 2
DSA_HEAD_DIM = 64
IDX_HEADS = 8
IDX_DIM = 64
DSA_TOPK = 256
ROPE_THETA = 500000.0
N_EXPERTS = 32
MOE_TOP_K = 4
D_FF = 1024
SWIGLU_ALPHA = 1.702
SWIGLU_LIMIT = 7.0
LN_EPS = 1e-5
GN_EPS = 1e-6
DEPTH = 1
DEEPNORM_ALPHA = (2.0 * DEPTH) ** 0.25
PROJ_WIDTHS = (RET_HEADS * RET_DK, RET_HEADS * RET_DK, RET_HEADS * RET_DV, RET_HEADS * RET_DV,
               DSA_HEADS * DSA_HEAD_DIM, DSA_KV_HEADS * DSA_HEAD_DIM, DSA_KV_HEADS * DSA_HEAD_DIM,
               IDX_HEADS * IDX_DIM, IDX_DIM, IDX_HEADS, D_MODEL, D_MODEL)

LANES = 128
MIB = 1024 * 1024

OFF_RQ, OFF_RK, OFF_RV, OFF_RG = 0, 512, 1024, 2048
OFF_AQ, OFF_AK, OFF_AV, OFF_IQ, OFF_IKW = 3072, 3584, 3712, 3840, 4352
OFF_GR, OFF_GA, PACKED_COLS = 4480, 5504, 6528
TAB_COLS = 8 * LANES

PROJ_TM = 256
RET_CHUNK = 128
DSA_TQ = 256
MERGE_TM = 512
MOE_T = 1024
MOE_CAP = 128


def _nt(a, b):
    return lax.dot_general(a, b, (((1,), (1,)), ((), ())), preferred_element_type=F32)


def _mm(a, b):
    return jnp.dot(a, b, preferred_element_type=F32)


def _proj_kernel(x_ref, w_ref, tab_ref, rq_ref, rk_ref, rv_ref, rg_ref, aq_ref, ak_ref, av_ref,
                 iq_ref, ikw_ref, gr_ref, ga_ref):
    xb = x_ref[...].astype(BF16)

    def mm(c0, n):
        return _mm(xb, w_ref[:, c0:c0 + n])

    def tab(i):
        return tab_ref[:, i * LANES:(i + 1) * LANES]

    def rot_full(z):
        return z * tab(0) + pltpu.roll(z, 64, 1) * tab(1)

    def rot_part(z, c):
        return z * tab(c) + pltpu.roll(z, LANES - 8, 1) * tab(c + 1) + pltpu.roll(z, 8, 1) * tab(c + 2)

    z = mm(OFF_RQ, 512)
    for h in range(4):
        sl = slice(h * LANES, (h + 1) * LANES)
        rq_ref[:, sl] = rot_full(z[:, sl]).astype(BF16)
    z = mm(OFF_RK, 512)
    for h in range(4):
        sl = slice(h * LANES, (h + 1) * LANES)
        rk_ref[:, sl] = (rot_full(z[:, sl]) * (RET_DK ** -0.5)).astype(BF16)
    for c in range(2):
        rv_ref[:, c * 512:(c + 1) * 512] = mm(OFF_RV + c * 512, 512).astype(BF16)
    for c in range(2):
        rg_ref[:, c * 512:(c + 1) * 512] = mm(OFF_RG + c * 512, 512)
    z = mm(OFF_AQ, 512)
    for h in range(4):
        sl = slice(h * LANES, (h + 1) * LANES)
        aq_ref[:, sl] = rot_part(z[:, sl], 2).astype(BF16)
    z = mm(OFF_AK, 256)
    ak_ref[...] = rot_part(z[:, :LANES], 2)
    av_ref[...] = z[:, LANES:]
    z = mm(OFF_IQ, 512)
    for h in range(4):
        sl = slice(h * LANES, (h + 1) * LANES)
        iq_ref[:, sl] = rot_part(z[:, sl], 2).astype(BF16)
    ikw_ref[...] = rot_part(mm(OFF_IKW, LANES), 5)
    for c in range(2):
        gr_ref[:, c * 512:(c + 1) * 512] = mm(OFF_GR + c * 512, 512)
    for c in range(2):
        ga_ref[:, c * 512:(c + 1) * 512] = mm(OFF_GA + c * 512, 512)


def _rot_tables(pos):
    p = pos.shape[0]
    posf = pos.astype(F32)[:, None]
    ret_f = RET_ROPE_BASE ** (-jnp.linspace(0.0, 1.0, RET_DK // 2, dtype=F32))
    ang = posf * ret_f[None, :]
    c, s = jnp.cos(ang), jnp.sin(ang)
    cos_r = jnp.concatenate([c, c], 1)
    sin_r = jnp.concatenate([-s, s], 1)
    n_rot = DSA_HEAD_DIM // 4
    att_f = ROPE_THETA ** (-jnp.arange(0, n_rot, 2, dtype=F32) / n_rot)
    ang2 = posf * att_f[None, :]
    c2, s2 = jnp.cos(ang2), jnp.sin(ang2)
    half = n_rot // 2
    rest = DSA_HEAD_DIM - 2 * half
    c64 = jnp.concatenate([c2, c2, jnp.ones((p, rest), F32)], 1)
    s1_64 = jnp.concatenate([-s2, jnp.zeros((p, DSA_HEAD_DIM - half), F32)], 1)
    s2_64 = jnp.concatenate([jnp.zeros((p, half), F32), s2, jnp.zeros((p, rest), F32)], 1)
    z64 = jnp.zeros((p, DSA_HEAD_DIM), F32)
    ci = jnp.concatenate([c64, jnp.full((p, IDX_HEADS), IDX_HEADS ** -0.5, F32),
                          jnp.zeros((p, DSA_HEAD_DIM - IDX_HEADS), F32)], 1)
    return jnp.concatenate([cos_r, sin_r,
                            jnp.concatenate([c64, c64], 1), jnp.concatenate([s1_64, s1_64], 1),
                            jnp.concatenate([s2_64, s2_64], 1),
                            ci, jnp.concatenate([s1_64, z64], 1), jnp.concatenate([s2_64, z64], 1)], 1)


def _pack_w_in(w):
    cuts = np.cumsum(PROJ_WIDTHS)[:-1].tolist()
    rq, rk, rv, rg, aq, ak, av, iq, ik, iw, gr, ga = jnp.split(w, cuts, axis=1)
    pad = jnp.zeros((w.shape[0], LANES - IDX_DIM - IDX_HEADS), w.dtype)
    return jnp.concatenate([rq, rk, rv, rg, aq, ak, av, iq, ik, iw, pad, gr, ga], axis=1).astype(BF16)


def _project(x, wp, tab, tab_period):
    n = x.shape[0]
    tm = PROJ_TM
    row = lambda w: pl.BlockSpec((tm, w), lambda i: (i, 0))
    out_shapes = [((n, 512), BF16), ((n, 512), BF16), ((n, 1024), BF16), ((n, 1024), F32),
                  ((n, 512), BF16), ((n, LANES), F32), ((n, LANES), F32), ((n, 512), BF16),
                  ((n, LANES), F32), ((n, 1024), F32), ((n, 1024), F32)]
    return pl.pallas_call(
        _proj_kernel,
        grid=(n // tm,),
        in_specs=[row(D_MODEL),
                  pl.BlockSpec((D_MODEL, PACKED_COLS), lambda i: (0, 0), pipeline_mode=pl.Buffered(1)),
                  pl.BlockSpec((tm, TAB_COLS), lambda i: (i % tab_period, 0))],
        out_specs=[row(s[1]) for s, _ in out_shapes],
        out_shape=[jax.ShapeDtypeStruct(s, d) for s, d in out_shapes],
        compiler_params=pltpu.CompilerParams(dimension_semantics=("parallel",), vmem_limit_bytes=44 * MIB),
        name="proj",
    )(x, wp, tab)


def _ret_kernel(dec_ref, xi_ref, zeta_ref, rq_ref, rk_ref, rv_ref, rg_ref, s0_ref, ret_ref, sout_ref, st_ref,
                *, rows, n_chunk, g_pow):
    j = pl.program_id(1)
    cpad = RET_CHUNK

    @pl.when(j == 0)
    def _():
        st_ref[...] = s0_ref[0]

    def padded(v):
        if rows == cpad:
            return v
        return jnp.concatenate([v, jnp.zeros((cpad - rows, v.shape[1]), v.dtype)], axis=0)

    for c in range(n_chunk):
        rs = slice(c * rows, (c + 1) * rows)
        for h in range(RET_HEADS):
            ks = slice(h * RET_DK, (h + 1) * RET_DK)
            vs = slice(h * RET_DV, (h + 1) * RET_DV)
            q = padded(rq_ref[rs, ks])
            kt = padded(rk_ref[rs, ks].astype(F32)).T
            v = padded(rv_ref[rs, vs])
            s = st_ref[h]
            sc = _mm(q, kt.astype(BF16)) * dec_ref[h]
            o = _mm(sc.astype(BF16), v) + _mm(q, s.astype(BF16)) * xi_ref[h]
            st_ref[h] = g_pow[h] * s + _mm((kt * zeta_ref[h]).astype(BF16), v)
            o = o[:rows]
            mu = jnp.mean(o, axis=-1, keepdims=True)
            d = o - mu
            var = jnp.mean(d * d, axis=-1, keepdims=True)
            gn = d * lax.rsqrt(var + GN_EPS)
            g = rg_ref[rs, vs]
            ret_ref[rs, vs] = (gn * (g * jax.nn.sigmoid(g))).astype(BF16)

    @pl.when(j == pl.num_programs(1) - 1)
    def _():
        sout_ref[0] = st_ref[...]


def _retention(rq, rk, rv, rg, s0, rows_per_stream, rows):
    n = rq.shape[0]
    n_streams = n // rows_per_stream
    n_chunk = min(rows_per_stream // rows, 2)
    blk = rows * n_chunk
    nb = rows_per_stream // blk
    gam = 1.0 - 2.0 ** (-5.0 - np.arange(RET_HEADS, dtype=np.float64))
    i = np.arange(RET_CHUNK, dtype=np.float64)
    diff = i[:, None] - i[None, :]
    dec = np.where(diff >= 0, gam[:, None, None] ** np.maximum(diff, 0.0)[None], 0.0)
    xi = gam[:, None, None] ** (i + 1.0)[None, :, None]
    zeta = np.where(i < rows, gam[:, None, None] ** (rows - 1.0 - i)[None, None, :], 0.0)
    g_pow = tuple(float(g ** rows) for g in gam)
    const = lambda shape: pl.BlockSpec(shape, lambda s, j: (0,) * len(shape))
    row = lambda w: pl.BlockSpec((blk, w), lambda s, j: (s * nb + j, 0))
    st = pl.BlockSpec((1, RET_HEADS, RET_DK, RET_DV), lambda s, j: (s, 0, 0, 0))
    return pl.pallas_call(
        functools.partial(_ret_kernel, rows=rows, n_chunk=n_chunk, g_pow=g_pow),
        grid=(n_streams, nb),
        in_specs=[const((RET_HEADS, RET_CHUNK, RET_CHUNK)), const((RET_HEADS, RET_CHUNK, 1)),
                  const((RET_HEADS, 1, RET_CHUNK)), row(512), row(512), row(1024), row(1024), st],
        out_specs=[row(1024), st],
        out_shape=[jax.ShapeDtypeStruct((n, RET_HEADS * RET_DV), BF16),
                   jax.ShapeDtypeStruct((n_streams, RET_HEADS, RET_DK, RET_DV), F32)],
        scratch_shapes=[pltpu.VMEM((RET_HEADS, RET_DK, RET_DV), F32)],
        compiler_params=pltpu.CompilerParams(dimension_semantics=("parallel", "arbitrary"),
                                             vmem_limit_bytes=32 * MIB),
        name="retention",
    )(jnp.asarray(dec, F32), jnp.asarray(xi, F32), jnp.asarray(zeta, F32), rq, rk, rv, rg, s0)


def _dsa_kernel(aq_ref, iq_ref, ikwq_ref, kk_ref, vv_ref, ikk_ref, o_ref, w_ref, bias_ref,
                *, tq, n_keys, chunk_causal, limit_const, n_sel):
    j = pl.program_id(1)
    nsel_f = float(n_sel)

    ikb = ikk_ref[...][:, :IDX_DIM].astype(BF16)
    iww = ikwq_ref[...][:, IDX_DIM:IDX_DIM + IDX_HEADS] * (IDX_DIM ** -0.5)
    acc = jnp.zeros((tq, n_keys), F32)
    for p in range(IDX_HEADS // 2):
        slab = iq_ref[:, p * LANES:(p + 1) * LANES]
        for hh in range(2):
            h = 2 * p + hh
            s = _nt(slab[:, hh * IDX_DIM:(hh + 1) * IDX_DIM], ikb)
            acc = acc + jnp.maximum(s, 0.0) * iww[:, h:h + 1]

    col = lax.broadcasted_iota(I32, (tq, n_keys), 1)
    if chunk_causal:
        row = lax.broadcasted_iota(I32, (tq, 1), 0) + j * tq
        limit = (jnp.right_shift(row, 6) + 1) * CHUNK
    else:
        limit = limit_const
    admissible = col < limit
    sc = jnp.where(admissible, acc, -jnp.inf)

    pos = jnp.sum(jnp.where(sc >= 0.0, 1.0, 0.0), axis=1, keepdims=True) >= nsel_f
    kk = jnp.where(pos, nsel_f, float(n_keys - n_sel + 1))
    w_ref[...] = jnp.where(pos, sc, -sc)

    def bit_step(i, u):
        cand_u = u | jnp.left_shift(jnp.int32(1), 30 - i)
        cand = pltpu.bitcast(cand_u, F32)
        cnt = jnp.sum(jnp.where(w_ref[...] >= cand, 1.0, 0.0), axis=1, keepdims=True)
        return jnp.where(cnt >= kk, cand_u, u)

    mag_u = lax.fori_loop(0, 31, bit_step, jnp.zeros((tq, 1), I32))
    mag = pltpu.bitcast(mag_u, F32)
    thr = jnp.where(pos, mag, -mag)

    sc = jnp.where(pos, w_ref[...], -w_ref[...])
    short = jnp.sum(jnp.where(sc >= thr, 1.0, 0.0), axis=1, keepdims=True) < nsel_f
    thr = jnp.where(jnp.logical_and(short, jnp.logical_not(pos)), -pltpu.bitcast(mag_u + 1, F32), thr)
    ge = sc >= thr
    cnt_gt = jnp.sum(jnp.where(sc > thr, 1.0, 0.0), axis=1, keepdims=True)
    cnt_ge = jnp.sum(jnp.where(ge, 1.0, 0.0), axis=1, keepdims=True)
    bias_ref[...] = jnp.where(jnp.logical_and(ge, admissible), 0.0, -jnp.inf)
    excess = jnp.logical_and(cnt_ge > nsel_f, thr > -jnp.inf)

    @pl.when(jnp.max(jnp.where(excess, 1.0, 0.0)) > 0.0)
    def _():
        need = nsel_f - cnt_gt
        tri = jnp.where(lax.broadcasted_iota(I32, (LANES, LANES), 0) < lax.broadcasted_iota(I32, (LANES, LANES), 1),
                        1.0, 0.0).astype(BF16)
        before = jnp.zeros((tq, 1), F32)
        for b in range(n_keys // LANES):
            sl = slice(b * LANES, (b + 1) * LANES)
            sblk = jnp.where(pos, w_ref[:, sl], -w_ref[:, sl])
            eq = jnp.where(sblk == thr, 1.0, 0.0)
            rank = _mm(eq.astype(BF16), tri) + before
            keep = jnp.logical_or(sblk > thr, jnp.logical_and(sblk == thr, rank < need))
            bias_ref[:, sl] = jnp.where(jnp.logical_and(keep, sblk > -jnp.inf), 0.0, -jnp.inf)
            before = before + jnp.sum(eq, axis=1, keepdims=True)

    kfull = kk_ref[...]
    vfull = vv_ref[...]
    for g in range(DSA_KV_HEADS):
        gs = slice(g * DSA_HEAD_DIM, (g + 1) * DSA_HEAD_DIM)
        kg = kfull[:, gs].astype(BF16)
        vg = vfull[:, gs].astype(BF16)
        for pp in range(2):
            p = 2 * g + pp
            slab = aq_ref[:, p * LANES:(p + 1) * LANES]
            outs = []
            for hh in range(2):
                lg = _nt(slab[:, hh * DSA_HEAD_DIM:(hh + 1) * DSA_HEAD_DIM], kg) * (DSA_HEAD_DIM ** -0.5) + bias_ref[...]
                m = jnp.max(lg, axis=1, keepdims=True)
                pr = jnp.exp(lg - m)
                den = jnp.sum(pr, axis=1, keepdims=True)
                outs.append(_mm(pr.astype(BF16), vg) / den)
            o_ref[:, p * LANES:(p + 1) * LANES] = jnp.concatenate(outs, axis=1).astype(BF16)


def _dsa(aq, iq, ikw, kk, vv, ikk, n_streams, tq, n_keys, chunk_causal, limit_const):
    n = aq.shape[0]
    nq = n // n_streams // tq
    n_sel = min(DSA_TOPK, limit_const // 4)
    qrow = lambda w: pl.BlockSpec((tq, w), lambda s, j: (s * nq + j, 0))
    krow = lambda w: pl.BlockSpec((n_keys, w), lambda s, j: (s, 0))
    return pl.pallas_call(
        functools.partial(_dsa_kernel, tq=tq, n_keys=n_keys, chunk_causal=chunk_causal,
                          limit_const=limit_const, n_sel=n_sel),
        grid=(n_streams, nq),
        in_specs=[qrow(512), qrow(512), qrow(LANES), krow(kk.shape[1]), krow(vv.shape[1]), krow(ikk.shape[1])],
        out_specs=qrow(512),
        out_shape=jax.ShapeDtypeStruct((n, DSA_HEADS * DSA_HEAD_DIM), BF16),
        scratch_shapes=[pltpu.VMEM((tq, n_keys), F32), pltpu.VMEM((tq, n_keys), F32)],
        compiler_params=pltpu.CompilerParams(dimension_semantics=("parallel", "arbitrary"),
                                             vmem_limit_bytes=48 * MIB),
        name="dsa",
    )(aq, iq, ikw, kk, vv, ikk)


def _layer_norm(v, g, b):
    mu = jnp.mean(v, axis=-1, keepdims=True)
    d = v - mu
    var = jnp.mean(d * d, axis=-1, keepdims=True)
    return d * lax.rsqrt(var + LN_EPS) * g + b


def _merge_kernel(x_ref, ret_ref, od_ref, gr_ref, ga_ref, wr_ref, wd_ref, wo_ref, g1_ref, b1_ref, rwt_ref, rb_ref,
                  tri_ref, h1_ref, gt_ref, slot_ref, cnt_ref, *, sub_tiles):
    i = pl.program_id(0)

    @pl.when(i % sub_tiles == 0)
    def _():
        cnt_ref[...] = jnp.zeros_like(cnt_ref)

    y_ret = _mm(ret_ref[...], wr_ref[...])
    y_dsa = _mm(od_ref[...], wd_ref[...])
    merged = jax.nn.sigmoid(gr_ref[...]) * y_ret + jax.nn.sigmoid(ga_ref[...]) * y_dsa
    mix = _mm(merged.astype(BF16), wo_ref[...])
    h1 = _layer_norm(DEEPNORM_ALPHA * x_ref[...] + mix, g1_ref[...], b1_ref[...])
    h1_ref[...] = h1

    logits = lax.dot_general(rwt_ref[...], h1, (((1,), (1,)), ((), ())), preferred_element_type=F32,
                             precision=lax.Precision.HIGHEST) + rb_ref[...]
    tm = logits.shape[1]
    e_iota = lax.broadcasted_iota(I32, (N_EXPERTS, tm), 0)
    tops, hots = [], []
    for _ in range(MOE_TOP_K):
        m = jnp.max(logits, axis=0, keepdims=True)
        first = jnp.min(jnp.where(logits == m, e_iota, N_EXPERTS), axis=0, keepdims=True)
        hot = e_iota == first
        tops.append(m)
        hots.append(hot)
        logits = jnp.where(hot, -jnp.inf, logits)
    exps = [jnp.exp(m - tops[0]) for m in tops]
    den = exps[0] + exps[1] + exps[2] + exps[3]
    gates = jnp.zeros((N_EXPERTS, tm), F32)
    sel = jnp.zeros((N_EXPERTS, tm), F32)
    for hot, e in zip(hots, exps):
        gates = gates + jnp.where(hot, e / den, 0.0)
        sel = sel + jnp.where(hot, 1.0, 0.0)
    gt_ref[...] = gates
    rank = _mm(sel.astype(BF16), tri_ref[...]) + cnt_ref[...]
    slot_ref[...] = jnp.where(sel > 0.0, rank, -1.0)
    cnt_ref[...] = cnt_ref[...] + jnp.sum(sel, axis=1, keepdims=True)


def _merge(x, ret, od, gr, ga, wr, wd, wo, g1, b1, rwt, rb):
    n = x.shape[0]
    tm = MERGE_TM
    row = lambda w: pl.BlockSpec((tm, w), lambda i: (i, 0))
    const = lambda a: pl.BlockSpec(a.shape, lambda i: (0,) * a.ndim)
    col = pl.BlockSpec((N_EXPERTS, tm), lambda i: (0, i))
    tri = jnp.asarray(np.triu(np.ones((tm, tm), np.float32), 1), BF16)
    return pl.pallas_call(
        functools.partial(_merge_kernel, sub_tiles=MOE_T // tm),
        grid=(n // tm,),
        in_specs=[row(D_MODEL), row(1024), row(512), row(1024), row(1024), const(wr), const(wd), const(wo),
                  const(g1), const(b1), const(rwt), const(rb), const(tri)],
        out_specs=[row(D_MODEL), col, col],
        out_shape=[jax.ShapeDtypeStruct((n, D_MODEL), F32), jax.ShapeDtypeStruct((N_EXPERTS, n), F32),
                   jax.ShapeDtypeStruct((N_EXPERTS, n), F32)],
        scratch_shapes=[pltpu.VMEM((N_EXPERTS, 1), F32)],
        compiler_params=pltpu.CompilerParams(dimension_semantics=("arbitrary",), vmem_limit_bytes=48 * MIB),
        name="merge",
    )(x, ret, od, gr, ga, wr, wd, wo, g1, b1, rwt, rb, tri)


def _moe_kernel(h1_ref, gt_ref, slot_ref, wup_ref, bup_ref, wdn_ref, bdn_ref, g2_ref, b2_ref, o_ref, xb_ref, y_ref):
    e = pl.program_id(1)
    t = h1_ref.shape[0]

    @pl.when(e == 0)
    def _():
        xb_ref[...] = h1_ref[...].astype(BF16)
        y_ref[...] = jnp.zeros_like(y_ref)

    slot_e = slot_ref[pl.ds(e, 1), :]
    gate_e = gt_ref[pl.ds(e, 1), :]
    count = jnp.sum(jnp.where(slot_e >= 0.0, 1.0, 0.0)).astype(I32)
    jrow = lax.broadcasted_iota(I32, (MOE_CAP, t), 0).astype(F32)

    def sub_block(sb, carry):
        base = (sb * MOE_CAP).astype(F32)
        pick = jnp.where(slot_e - base == jrow, 1.0, 0.0)
        xc = _mm(pick.astype(BF16), xb_ref[...]).astype(BF16)
        h = _mm(xc, wup_ref[0]) + bup_ref[0]
        glu = jnp.minimum(h[:, :D_FF], SWIGLU_LIMIT)
        lin = jnp.clip(h[:, D_FF:], -SWIGLU_LIMIT, SWIGLU_LIMIT)
        act = glu * jax.nn.sigmoid(SWIGLU_ALPHA * glu) * (lin + 1.0)
        out = _mm(act.astype(BF16), wdn_ref[0]) + bdn_ref[0]
        gate_c = jnp.sum(pick * gate_e, axis=1, keepdims=True)
        y_ref[...] += _mm(pick.T.astype(BF16), (out * gate_c).astype(BF16))
        return carry

    lax.fori_loop(0, (count + MOE_CAP - 1) // MOE_CAP, sub_block, 0)

    @pl.when(e == pl.num_programs(1) - 1)
    def _():
        o_ref[...] = _layer_norm(DEEPNORM_ALPHA * h1_ref[...] + y_ref[...], g2_ref[...], b2_ref[...])


def _moe(h1, gt, slot, wup, bup, wdn, bdn, g2, b2):
    n = h1.shape[0]
    t = MOE_T
    const = lambda a: pl.BlockSpec(a.shape, lambda i, e: (0,) * a.ndim)
    return pl.pallas_call(
        _moe_kernel,
        grid=(n // t, N_EXPERTS),
        in_specs=[pl.BlockSpec((t, D_MODEL), lambda i, e: (i, 0)),
                  pl.BlockSpec((N_EXPERTS, t), lambda i, e: (0, i)),
                  pl.BlockSpec((N_EXPERTS, t), lambda i, e: (0, i)),
                  pl.BlockSpec((1, D_MODEL, 2 * D_FF), lambda i, e: (e, 0, 0)),
                  pl.BlockSpec((1, 1, 2 * D_FF), lambda i, e: (e, 0, 0)),
                  pl.BlockSpec((1, D_FF, D_MODEL), lambda i, e: (e, 0, 0)),
                  pl.BlockSpec((1, 1, D_MODEL), lambda i, e: (e, 0, 0)),
                  const(g2), const(b2)],
        out_specs=pl.BlockSpec((t, D_MODEL), lambda i, e: (i, 0)),
        out_shape=jax.ShapeDtypeStruct((n, D_MODEL), F32),
        scratch_shapes=[pltpu.VMEM((t, D_MODEL), BF16), pltpu.VMEM((t, D_MODEL), F32)],
        compiler_params=pltpu.CompilerParams(dimension_semantics=("parallel", "arbitrary"),
                                             vmem_limit_bytes=52 * MIB),
        name="moe",
    )(h1, gt, slot, wup, bup, wdn, bdn, g2, b2)


def _layer(x, pos_tab, tab_period, s0, rows_per_stream, ret_rows, dsa_keys, weights):
    wp, wr, wd, wo, g1, b1, rwt, rb, wup, bup, wdn, bdn, g2, b2 = weights
    n = x.shape[0]
    n_streams = n // rows_per_stream
    rq, rk, rv, rg, aq, ak, av, iq, ikw, gr, ga = _project(x, wp, pos_tab, tab_period)
    ret, s_new = _retention(rq, rk, rv, rg, s0, rows_per_stream, ret_rows)
    if dsa_keys is None:
        od = _dsa(aq, iq, ikw, ak, av, ikw, n_streams, DSA_TQ, rows_per_stream, True, rows_per_stream)
    else:
        kk, vv, ikk, n_keys, limit = dsa_keys(ak, av, ikw)
        od = _dsa(aq, iq, ikw, kk, vv, ikk, n_streams, rows_per_stream, n_keys, False, limit)
    h1, gt, slot = _merge(x, ret, od, gr, ga, wr, wd, wo, g1, b1, rwt, rb)
    y = _moe(h1, gt, slot, wup, bup, wdn, bdn, g2, b2)
    return y, s_new, ak, av, ikw[:, :IDX_DIM]


def kernel(x_prompt, x_sample, state_ret, cache_k, cache_v, cache_idx_k, w_in, w_ret_o, w_dsa_o, w_o,
           ln1_g, ln1_b, router_w, router_b, w_up, b_up, w_down, b_down, ln2_g, ln2_b):
    assert w_in.shape[0] == DEPTH
    batch, seq, _ = x_prompt.shape
    dec_batch, dec_seq, _ = x_sample.shape
    past = cache_k.shape[2]
    assert seq % (PROJ_TM) == 0 and seq % DSA_TQ == 0 and PROJ_TM % dec_seq == 0

    l = 0
    weights = (
        _pack_w_in(w_in[l]), w_ret_o[l].astype(BF16), w_dsa_o[l].astype(BF16), w_o[l].astype(BF16),
        ln1_g[l][None, :], ln1_b[l][None, :], router_w[l].T, router_b[l][:, None],
        jnp.concatenate([w_up[l][:, :, 0::2], w_up[l][:, :, 1::2]], axis=2).astype(BF16),
        jnp.concatenate([b_up[l][:, 0::2], b_up[l][:, 1::2]], axis=1)[:, None, :],
        w_down[l].astype(BF16), b_down[l][:, None, :], ln2_g[l][None, :], ln2_b[l][None, :])

    tab_p = _rot_tables(jnp.arange(seq))
    zeros_state = jnp.zeros((batch, RET_HEADS, RET_DK, RET_DV), F32)
    y_p, s_p, k_p, v_p, ik_p = _layer(x_prompt.reshape(batch * seq, D_MODEL), tab_p, seq // PROJ_TM, zeros_state,
                                      seq, RET_CHUNK, None, weights)

    n_keys_real = past + dec_seq
    n_keys = -(-n_keys_real // LANES) * LANES
    tab_s = jnp.tile(_rot_tables(past + jnp.arange(dec_seq)), (PROJ_TM // dec_seq, 1))

    def sample_keys(ak, av, ikw):
        def cat(cache, new, width):
            padz = jnp.zeros((dec_batch, n_keys - n_keys_real, width), F32)
            return jnp.concatenate([cache.reshape(dec_batch, past, width), new.reshape(dec_batch, dec_seq, width), padz],
                                   axis=1).reshape(dec_batch * n_keys, width)
        return (cat(cache_k[l], ak, LANES), cat(cache_v[l], av, LANES),
                cat(cache_idx_k[l], ikw[:, :IDX_DIM], IDX_DIM), n_keys, n_keys_real)

    y_s, s_s, k_s, v_s, ik_s = _layer(x_sample.reshape(dec_batch * dec_seq, D_MODEL), tab_s, 1, state_ret[l],
                                      dec_seq, dec_seq, sample_keys, weights)

    kv = (DSA_KV_HEADS, DSA_HEAD_DIM)
    return (y_p.reshape(batch, seq, D_MODEL), y_s.reshape(dec_batch, dec_seq, D_MODEL),
            s_p[None], k_p.reshape(1, batch, seq, *kv), v_p.reshape(1, batch, seq, *kv),
            ik_p.reshape(1, batch, seq, IDX_DIM),
            s_s[None], k_s.reshape(1, dec_batch, dec_seq, *kv), v_s.reshape(1, dec_batch, dec_seq, *kv),
            ik_s.reshape(1, dec_batch, dec_seq, IDX_DIM))
```

```python
import functools

import numpy as np
import jax
import jax.numpy as jnp
from jax import lax
from jax.experimental import pallas as pl
from jax.experimental.pallas import tpu as pltpu

F32 = jnp.float32
BF16 = jnp.bfloat16
I32 = jnp.int32

D_MODEL = 1024
CHUNK = 64
RET_HEADS = 4
RET_DK = 128
RET_DV = 256
RET_ROPE_BASE = 10000.0
DSA_HEADS = 8
DSA_KV_HEADS = 2
DSA_HEAD_DIM = 64
IDX_HEADS = 8
IDX_DIM = 64
DSA_TOPK = 256
ROPE_THETA = 500000.0
N_EXPERTS = 32
MOE_TOP_K = 4
D_FF = 1024
SWIGLU_ALPHA = 1.702
SWIGLU_LIMIT = 7.0
LN_EPS = 1e-5
GN_EPS = 1e-6
DEPTH = 1
DEEPNORM_ALPHA = (2.0 * DEPTH) ** 0.25
PROJ_WIDTHS = (RET_HEADS * RET_DK, RET_HEADS * RET_DK, RET_HEADS * RET_DV, RET_HEADS * RET_DV,
               DSA_HEADS * DSA_HEAD_DIM, DSA_KV_HEADS * DSA_HEAD_DIM, DSA_KV_HEADS * DSA_HEAD_DIM,
               IDX_HEADS * IDX_DIM, IDX_DIM, IDX_HEADS, D_MODEL, D_MODEL)

LANES = 128
MIB = 1024 * 1024

OFF_RQ, OFF_RK, OFF_RV, OFF_RG = 0, 512, 1024, 2048
OFF_AQ, OFF_AK, OFF_AV, OFF_IQ, OFF_IKW = 3072, 3584, 3712, 3840, 4352
OFF_GR, OFF_GA, PACKED_COLS = 4480, 5504, 6528
TAB_COLS = 8 * LANES

PROJ_TM = 256
RET_CHUNK = 128
DSA_TQ = 256
MERGE_TM = 512
MOE_T = 1024
MOE_CAP = 128


def _nt(a, b):
    return lax.dot_general(a, b, (((1,), (1,)), ((), ())), preferred_element_type=F32)


def _mm(a, b):
    return jnp.dot(a, b, preferred_element_type=F32)


def _proj_kernel(x_ref, w_ref, tab_ref, rq_ref, rk_ref, rv_ref, rg_ref, aq_ref, ak_ref, av_ref,
                 iq_ref, ikw_ref, gr_ref, ga_ref):
    xb = x_ref[...].astype(BF16)

    def mm(c0, n):
        return _mm(xb, w_ref[:, c0:c0 + n])

    def tab(i):
        return tab_ref[:, i * LANES:(i + 1) * LANES]

    def rot_full(z):
        return z * tab(0) + pltpu.roll(z, 64, 1) * tab(1)

    def rot_part(z, c):
        return z * tab(c) + pltpu.roll(z, LANES - 8, 1) * tab(c + 1) + pltpu.roll(z, 8, 1) * tab(c + 2)

    z = mm(OFF_RQ, 512)
    for h in range(4):
        sl = slice(h * LANES, (h + 1) * LANES)
        rq_ref[:, sl] = rot_full(z[:, sl]).astype(BF16)
    z = mm(OFF_RK, 512)
    for h in range(4):
        sl = slice(h * LANES, (h + 1) * LANES)
        rk_ref[:, sl] = (rot_full(z[:, sl]) * (RET_DK ** -0.5)).astype(BF16)
    for c in range(2):
        rv_ref[:, c * 512:(c + 1) * 512] = mm(OFF_RV + c * 512, 512).astype(BF16)
    for c in range(2):
        rg_ref[:, c * 512:(c + 1) * 512] = mm(OFF_RG + c * 512, 512)
    z = mm(OFF_AQ, 512)
    for h in range(4):
        sl = slice(h * LANES, (h + 1) * LANES)
        aq_ref[:, sl] = rot_part(z[:, sl], 2).astype(BF16)
    z = mm(OFF_AK, 256)
    ak_ref[...] = rot_part(z[:, :LANES], 2)
    av_ref[...] = z[:, LANES:]
    z = mm(OFF_IQ, 512)
    for h in range(4):
        sl = slice(h * LANES, (h + 1) * LANES)
        iq_ref[:, sl] = rot_part(z[:, sl], 2).astype(BF16)
    ikw_ref[...] = rot_part(mm(OFF_IKW, LANES), 5)
    for c in range(2):
        gr_ref[:, c * 512:(c + 1) * 512] = mm(OFF_GR + c * 512, 512)
    for c in range(2):
        ga_ref[:, c * 512:(c + 1) * 512] = mm(OFF_GA + c * 512, 512)


def _rot_tables(pos):
    p = pos.shape[0]
    posf = pos.astype(F32)[:, None]
    ret_f = RET_ROPE_BASE ** (-jnp.linspace(0.0, 1.0, RET_DK // 2, dtype=F32))
    ang = posf * ret_f[None, :]
    c, s = jnp.cos(ang), jnp.sin(ang)
    cos_r = jnp.concatenate([c, c], 1)
    sin_r = jnp.concatenate([-s, s], 1)
    n_rot = DSA_HEAD_DIM // 4
    att_f = ROPE_THETA ** (-jnp.arange(0, n_rot, 2, dtype=F32) / n_rot)
    ang2 = posf * att_f[None, :]
    c2, s2 = jnp.cos(ang2), jnp.sin(ang2)
    half = n_rot // 2
    rest = DSA_HEAD_DIM - 2 * half
    c64 = jnp.concatenate([c2, c2, jnp.ones((p, rest), F32)], 1)
    s1_64 = jnp.concatenate([-s2, jnp.zeros((p, DSA_HEAD_DIM - half), F32)], 1)
    s2_64 = jnp.concatenate([jnp.zeros((p, half), F32), s2, jnp.zeros((p, rest), F32)], 1)
    z64 = jnp.zeros((p, DSA_HEAD_DIM), F32)
    ci = jnp.concatenate([c64, jnp.full((p, IDX_HEADS), IDX_HEADS ** -0.5, F32),
                          jnp.zeros((p, DSA_HEAD_DIM - IDX_HEADS), F32)], 1)
    return jnp.concatenate([cos_r, sin_r,
                            jnp.concatenate([c64, c64], 1), jnp.concatenate([s1_64, s1_64], 1),
                            jnp.concatenate([s2_64, s2_64], 1),
                            ci, jnp.concatenate([s1_64, z64], 1), jnp.concatenate([s2_64, z64], 1)], 1)


def _pack_w_in(w):
    cuts = np.cumsum(PROJ_WIDTHS)[:-1].tolist()
    rq, rk, rv, rg, aq, ak, av, iq, ik, iw, gr, ga = jnp.split(w, cuts, axis=1)
    pad = jnp.zeros((w.shape[0], LANES - IDX_DIM - IDX_HEADS), w.dtype)
    return jnp.concatenate([rq, rk, rv, rg, aq, ak, av, iq, ik, iw, pad, gr, ga], axis=1).astype(BF16)


def _project(x, wp, tab, tab_period):
    n = x.shape[0]
    tm = PROJ_TM
    row = lambda w: pl.BlockSpec((tm, w), lambda i: (i, 0))
    out_shapes = [((n, 512), BF16), ((n, 512), BF16), ((n, 1024), BF16), ((n, 1024), F32),
                  ((n, 512), BF16), ((n, LANES), F32), ((n, LANES), F32), ((n, 512), BF16),
                  ((n, LANES), F32), ((n, 1024), F32), ((n, 1024), F32)]
    return pl.pallas_call(
        _proj_kernel,
        grid=(n // tm,),
        in_specs=[row(D_MODEL),
                  pl.BlockSpec((D_MODEL, PACKED_COLS), lambda i: (0, 0), pipeline_mode=pl.Buffered(1)),
                  pl.BlockSpec((tm, TAB_COLS), lambda i: (i % tab_period, 0))],
        out_specs=[row(s[1]) for s, _ in out_shapes],
        out_shape=[jax.ShapeDtypeStruct(s, d) for s, d in out_shapes],
        compiler_params=pltpu.CompilerParams(dimension_semantics=("parallel",), vmem_limit_bytes=44 * MIB),
        name="proj",
    )(x, wp, tab)


def _ret_kernel(dec_ref, xi_ref, zeta_ref, rq_ref, rk_ref, rv_ref, rg_ref, s0_ref, ret_ref, sout_ref, st_ref,
                *, rows, n_chunk, g_pow):
    j = pl.program_id(1)
    cpad = RET_CHUNK

    @pl.when(j == 0)
    def _():
        st_ref[...] = s0_ref[0]

    def padded(v):
        if rows == cpad:
            return v
        return jnp.concatenate([v, jnp.zeros((cpad - rows, v.shape[1]), v.dtype)], axis=0)

    for c in range(n_chunk):
        rs = slice(c * rows, (c + 1) * rows)
        for h in range(RET_HEADS):
            ks = slice(h * RET_DK, (h + 1) * RET_DK)
            vs = slice(h * RET_DV, (h + 1) * RET_DV)
            q = padded(rq_ref[rs, ks])
            kt = padded(rk_ref[rs, ks].astype(F32)).T
            v = padded(rv_ref[rs, vs])
            s = st_ref[h]
            sc = _mm(q, kt.astype(BF16)) * dec_ref[h]
            o = _mm(sc.astype(BF16), v) + _mm(q, s.astype(BF16)) * xi_ref[h]
            st_ref[h] = g_pow[h] * s + _mm((kt * zeta_ref[h]).astype(BF16), v)
            o = o[:rows]
            mu = jnp.mean(o, axis=-1, keepdims=True)
            d = o - mu
            var = jnp.mean(d * d, axis=-1, keepdims=True)
            gn = d * lax.rsqrt(var + GN_EPS)
            g = rg_ref[rs, vs]
            ret_ref[rs, vs] = (gn * (g * jax.nn.sigmoid(g))).astype(BF16)

    @pl.when(j == pl.num_programs(1) - 1)
    def _():
        sout_ref[0] = st_ref[...]


def _retention(rq, rk, rv, rg, s0, rows_per_stream, rows):
    n = rq.shape[0]
    n_streams = n // rows_per_stream
    n_chunk = min(rows_per_stream // rows, 2)
    blk = rows * n_chunk
    nb = rows_per_stream // blk
    gam = 1.0 - 2.0 ** (-5.0 - np.arange(RET_HEADS, dtype=np.float64))
    i = np.arange(RET_CHUNK, dtype=np.float64)
    diff = i[:, None] - i[None, :]
    dec = np.where(diff >= 0, gam[:, None, None] ** np.maximum(diff, 0.0)[None], 0.0)
    xi = gam[:, None, None] ** (i + 1.0)[None, :, None]
    zeta = np.where(i < rows, gam[:, None, None] ** (rows - 1.0 - i)[None, None, :], 0.0)
    g_pow = tuple(float(g ** rows) for g in gam)
    const = lambda shape: pl.BlockSpec(shape, lambda s, j: (0,) * len(shape))
    row = lambda w: pl.BlockSpec((blk, w), lambda s, j: (s * nb + j, 0))
    st = pl.BlockSpec((1, RET_HEADS, RET_DK, RET_DV), lambda s, j: (s, 0, 0, 0))
    return pl.pallas_call(
        functools.partial(_ret_kernel, rows=rows, n_chunk=n_chunk, g_pow=g_pow),
        grid=(n_streams, nb),
        in_specs=[const((RET_HEADS, RET_CHUNK, RET_CHUNK)), const((RET_HEADS, RET_CHUNK, 1)),
                  const((RET_HEADS, 1, RET_CHUNK)), row(512), row(512), row(1024), row(1024), st],
        out_specs=[row(1024), st],
        out_shape=[jax.ShapeDtypeStruct((n, RET_HEADS * RET_DV), BF16),
                   jax.ShapeDtypeStruct((n_streams, RET_HEADS, RET_DK, RET_DV), F32)],
        scratch_shapes=[pltpu.VMEM((RET_HEADS, RET_DK, RET_DV), F32)],
        compiler_params=pltpu.CompilerParams(dimension_semantics=("parallel", "arbitrary"),
                                             vmem_limit_bytes=32 * MIB),
        name="retention",
    )(jnp.asarray(dec, F32), jnp.asarray(xi, F32), jnp.asarray(zeta, F32), rq, rk, rv, rg, s0)


def _dsa_kernel(aq_ref, iq_ref, ikwq_ref, kk_ref, vv_ref, ikk_ref, o_ref, w_ref, bias_ref,
                *, tq, n_keys, chunk_causal, limit_const, n_sel):
    j = pl.program_id(1)
    refs = (aq_ref, iq_ref, ikwq_ref, kk_ref, vv_ref, ikk_ref, o_ref, w_ref, bias_ref)
    if chunk_causal:
        for jj in range(n_keys // tq):
            pl.when(j == jj)(functools.partial(_dsa_body, *refs, tq=tq, n_keys=(jj + 1) * tq, row0=jj * tq,
                                               limit_const=None, n_sel=n_sel))
    else:
        _dsa_body(*refs, tq=tq, n_keys=n_keys, row0=0, limit_const=limit_const, n_sel=n_sel)


def _dsa_body(aq_ref, iq_ref, ikwq_ref, kk_ref, vv_ref, ikk_ref, o_ref, w_ref, bias_ref,
              *, tq, n_keys, row0, limit_const, n_sel):
    nsel_f = float(n_sel)
    ks = slice(0, n_keys)

    ikb = ikk_ref[ks, :][:, :IDX_DIM].astype(BF16)
    iww = ikwq_ref[...][:, IDX_DIM:IDX_DIM + IDX_HEADS] * (IDX_DIM ** -0.5)
    acc = jnp.zeros((tq, n_keys), F32)
    for p in range(IDX_HEADS // 2):
        slab = iq_ref[:, p * LANES:(p + 1) * LANES]
        for hh in range(2):
            h = 2 * p + hh
            s = _nt(slab[:, hh * IDX_DIM:(hh + 1) * IDX_DIM], ikb)
            acc = acc + jnp.maximum(s, 0.0) * iww[:, h:h + 1]

    col = lax.broadcasted_iota(I32, (tq, n_keys), 1)
    if limit_const is None:
        row = lax.broadcasted_iota(I32, (tq, 1), 0) + row0
        limit = (jnp.right_shift(row, 6) + 1) * CHUNK
    else:
        limit = limit_const
    admissible = col < limit
    sc = jnp.where(admissible, acc, -jnp.inf)

    pos = jnp.sum(jnp.where(sc >= 0.0, 1.0, 0.0), axis=1, keepdims=True) >= nsel_f
    kk = jnp.where(pos, nsel_f, float(n_keys - n_sel + 1))
    w_ref[:, ks] = jnp.where(pos, sc, -sc)

    def bit_step(i, u):
        cand_u = u | jnp.left_shift(jnp.int32(1), 30 - i)
        cand = pltpu.bitcast(cand_u, F32)
        cnt = jnp.sum(jnp.where(w_ref[:, ks] >= cand, 1.0, 0.0), axis=1, keepdims=True)
        return jnp.where(cnt >= kk, cand_u, u)

    mag_u = lax.fori_loop(0, 31, bit_step, jnp.zeros((tq, 1), I32))
    mag = pltpu.bitcast(mag_u, F32)
    thr = jnp.where(pos, mag, -mag)

    sc = jnp.where(pos, w_ref[:, ks], -w_ref[:, ks])
    short = jnp.sum(jnp.where(sc >= thr, 1.0, 0.0), axis=1, keepdims=True) < nsel_f
    thr = jnp.where(jnp.logical_and(short, jnp.logical_not(pos)), -pltpu.bitcast(mag_u + 1, F32), thr)
    ge = sc >= thr
    cnt_gt = jnp.sum(jnp.where(sc > thr, 1.0, 0.0), axis=1, keepdims=True)
    cnt_ge = jnp.sum(jnp.where(ge, 1.0, 0.0), axis=1, keepdims=True)
    bias_ref[:, ks] = jnp.where(jnp.logical_and(ge, admissible), 0.0, -jnp.inf)
    excess = jnp.logical_and(cnt_ge > nsel_f, thr > -jnp.inf)

    @pl.when(jnp.max(jnp.where(excess, 1.0, 0.0)) > 0.0)
    def _():
        need = nsel_f - cnt_gt
        tri = jnp.where(lax.broadcasted_iota(I32, (LANES, LANES), 0) < lax.broadcasted_iota(I32, (LANES, LANES), 1),
                        1.0, 0.0).astype(BF16)
        before = jnp.zeros((tq, 1), F32)
        for b in range(n_keys // LANES):
            sl = slice(b * LANES, (b + 1) * LANES)
            sblk = jnp.where(pos, w_ref[:, sl], -w_ref[:, sl])
            eq = jnp.where(sblk == thr, 1.0, 0.0)
            rank = _mm(eq.astype(BF16), tri) + before
            keep = jnp.logical_or(sblk > thr, jnp.logical_and(sblk == thr, rank < need))
            bias_ref[:, sl] = jnp.where(jnp.logical_and(keep, sblk > -jnp.inf), 0.0, -jnp.inf)
            before = before + jnp.sum(eq, axis=1, keepdims=True)

    kfull = kk_ref[ks, :]
    vfull = vv_ref[ks, :]
    for g in range(DSA_KV_HEADS):
        gs = slice(g * DSA_HEAD_DIM, (g + 1) * DSA_HEAD_DIM)
        kg = kfull[:, gs].astype(BF16)
        vg = vfull[:, gs].astype(BF16)
        for pp in range(2):
            p = 2 * g + pp
            slab = aq_ref[:, p * LANES:(p + 1) * LANES]
            outs = []
            for hh in range(2):
                lg = _nt(slab[:, hh * DSA_HEAD_DIM:(hh + 1) * DSA_HEAD_DIM], kg) * (DSA_HEAD_DIM ** -0.5) + bias_ref[:, ks]
                m = jnp.max(lg, axis=1, keepdims=True)
                pr = jnp.exp(lg - m)
                den = jnp.sum(pr, axis=1, keepdims=True)
                outs.append(_mm(pr.astype(BF16), vg) / den)
            o_ref[:, p * LANES:(p + 1) * LANES] = jnp.concatenate(outs, axis=1).astype(BF16)


def _dsa(aq, iq, ikw, kk, vv, ikk, n_streams, tq, n_keys, chunk_causal, limit_const):
    n = aq.shape[0]
    nq = n // n_streams // tq
    n_sel = min(DSA_TOPK, limit_const // 4)
    qrow = lambda w: pl.BlockSpec((tq, w), lambda s, j: (s * nq + j, 0))
    krow = lambda w: pl.BlockSpec((n_keys, w), lambda s, j: (s, 0))
    return pl.pallas_call(
        functools.partial(_dsa_kernel, tq=tq, n_keys=n_keys, chunk_causal=chunk_causal,
                          limit_const=limit_const, n_sel=n_sel),
        grid=(n_streams, nq),
        in_specs=[qrow(512), qrow(512), qrow(LANES), krow(kk.shape[1]), krow(vv.shape[1]), krow(ikk.shape[1])],
        out_specs=qrow(512),
        out_shape=jax.ShapeDtypeStruct((n, DSA_HEADS * DSA_HEAD_DIM), BF16),
        scratch_shapes=[pltpu.VMEM((tq, n_keys), F32), pltpu.VMEM((tq, n_keys), F32)],
        compiler_params=pltpu.CompilerParams(dimension_semantics=("parallel", "arbitrary"),
                                             vmem_limit_bytes=48 * MIB),
        name="dsa",
    )(aq, iq, ikw, kk, vv, ikk)


def _layer_norm(v, g, b):
    mu = jnp.mean(v, axis=-1, keepdims=True)
    d = v - mu
    var = jnp.mean(d * d, axis=-1, keepdims=True)
    return d * lax.rsqrt(var + LN_EPS) * g + b


def _merge_kernel(x_ref, ret_ref, od_ref, gr_ref, ga_ref, wr_ref, wd_ref, wo_ref, g1_ref, b1_ref, rwt_ref, rb_ref,
                  tri_ref, h1_ref, gt_ref, slot_ref, cnt_ref, *, sub_tiles):
    i = pl.program_id(0)

    @pl.when(i % sub_tiles == 0)
    def _():
        cnt_ref[...] = jnp.zeros_like(cnt_ref)

    y_ret = _mm(ret_ref[...], wr_ref[...])
    y_dsa = _mm(od_ref[...], wd_ref[...])
    merged = jax.nn.sigmoid(gr_ref[...]) * y_ret + jax.nn.sigmoid(ga_ref[...]) * y_dsa
    mix = _mm(merged.astype(BF16), wo_ref[...])
    h1 = _layer_norm(DEEPNORM_ALPHA * x_ref[...] + mix, g1_ref[...], b1_ref[...])
    h1_ref[...] = h1

    logits = lax.dot_general(rwt_ref[...], h1, (((1,), (1,)), ((), ())), preferred_element_type=F32,
                             precision=lax.Precision.HIGHEST) + rb_ref[...]
    tm = logits.shape[1]
    e_iota = lax.broadcasted_iota(I32, (N_EXPERTS, tm), 0)
    tops, hots = [], []
    for _ in range(MOE_TOP_K):
        m = jnp.max(logits, axis=0, keepdims=True)
        first = jnp.min(jnp.where(logits == m, e_iota, N_EXPERTS), axis=0, keepdims=True)
        hot = e_iota == first
        tops.append(m)
        hots.append(hot)
        logits = jnp.where(hot, -jnp.inf, logits)
    exps = [jnp.exp(m - tops[0]) for m in tops]
    den = exps[0] + exps[1] + exps[2] + exps[3]
    gates = jnp.zeros((N_EXPERTS, tm), F32)
    sel = jnp.zeros((N_EXPERTS, tm), F32)
    for hot, e in zip(hots, exps):
        gates = gates + jnp.where(hot, e / den, 0.0)
        sel = sel + jnp.where(hot, 1.0, 0.0)
    gt_ref[...] = gates
    rank = _mm(sel.astype(BF16), tri_ref[...]) + cnt_ref[...]
    slot_ref[...] = jnp.where(sel > 0.0, rank, -1.0)
    cnt_ref[...] = cnt_ref[...] + jnp.sum(sel, axis=1, keepdims=True)


def _merge(x, ret, od, gr, ga, wr, wd, wo, g1, b1, rwt, rb):
    n = x.shape[0]
    tm = MERGE_TM
    row = lambda w: pl.BlockSpec((tm, w), lambda i: (i, 0))
    const = lambda a: pl.BlockSpec(a.shape, lambda i: (0,) * a.ndim)
    col = pl.BlockSpec((N_EXPERTS, tm), lambda i: (0, i))
    tri = jnp.asarray(np.triu(np.ones((tm, tm), np.float32), 1), BF16)
    return pl.pallas_call(
        functools.partial(_merge_kernel, sub_tiles=MOE_T // tm),
        grid=(n // tm,),
        in_specs=[row(D_MODEL), row(1024), row(512), row(1024), row(1024), const(wr), const(wd), const(wo),
                  const(g1), const(b1), const(rwt), const(rb), const(tri)],
        out_specs=[row(D_MODEL), col, col],
        out_shape=[jax.ShapeDtypeStruct((n, D_MODEL), F32), jax.ShapeDtypeStruct((N_EXPERTS, n), F32),
                   jax.ShapeDtypeStruct((N_EXPERTS, n), F32)],
        scratch_shapes=[pltpu.VMEM((N_EXPERTS, 1), F32)],
        compiler_params=pltpu.CompilerParams(dimension_semantics=("arbitrary",), vmem_limit_bytes=48 * MIB),
        name="merge",
    )(x, ret, od, gr, ga, wr, wd, wo, g1, b1, rwt, rb, tri)


UP_BLOCK = 2 * LANES


def _deinterleave_kernel(w_ref, o_ref):
    r = lax.broadcasted_iota(I32, (UP_BLOCK, UP_BLOCK), 0)
    c = lax.broadcasted_iota(I32, (UP_BLOCK, UP_BLOCK), 1)
    src = jnp.where(c < LANES, 2 * c, 2 * (c - LANES) + 1)
    perm = jnp.where(r == src, 1.0, 0.0).astype(BF16)
    for b in range(w_ref.shape[2] // UP_BLOCK):
        sl = slice(b * UP_BLOCK, (b + 1) * UP_BLOCK)
        o_ref[0, :, sl] = _mm(w_ref[0, :, sl].astype(BF16), perm).astype(BF16)


def _deinterleave_w_up(w_up):
    n_e, d_in, d_out = w_up.shape
    cols = 512
    spec = pl.BlockSpec((1, d_in, cols), lambda e, c: (e, 0, c))
    return pl.pallas_call(
        _deinterleave_kernel,
        grid=(n_e, d_out // cols),
        in_specs=[spec],
        out_specs=spec,
        out_shape=jax.ShapeDtypeStruct(w_up.shape, BF16),
        compiler_params=pltpu.CompilerParams(dimension_semantics=("parallel", "parallel"),
                                             vmem_limit_bytes=24 * MIB),
        name="w_up_prep",
    )(w_up)

def _moe_kernel(h1_ref, gt_ref, slot_ref, wup_ref, bup_ref, wdn_ref, bdn_ref, g2_ref, b2_ref, o_ref, xb_ref, y_ref):
    e = pl.program_id(1)
    t = h1_ref.shape[0]

    @pl.when(e == 0)
    def _():
        xb_ref[...] = h1_ref[...].astype(BF16)
        y_ref[...] = jnp.zeros_like(y_ref)

    slot_e = slot_ref[pl.ds(e, 1), :]
    gate_e = gt_ref[pl.ds(e, 1), :]
    count = jnp.sum(jnp.where(slot_e >= 0.0, 1.0, 0.0)).astype(I32)
    jrow = lax.broadcasted_iota(I32, (MOE_CAP, t), 0).astype(F32)

    def sub_block(sb, carry):
        base = (sb * MOE_CAP).astype(F32)
        pick = jnp.where(slot_e - base == jrow, 1.0, 0.0)
        xc = _mm(pick.astype(BF16), xb_ref[...]).astype(BF16)
        h = _mm(xc, wup_ref[0]) + bup_ref[0]
        acts = []
        for b in range(2 * D_FF // UP_BLOCK):
            glu = jnp.minimum(h[:, b * UP_BLOCK:b * UP_BLOCK + LANES], SWIGLU_LIMIT)
            lin = jnp.clip(h[:, b * UP_BLOCK + LANES:(b + 1) * UP_BLOCK], -SWIGLU_LIMIT, SWIGLU_LIMIT)
            acts.append(glu * jax.nn.sigmoid(SWIGLU_ALPHA * glu) * (lin + 1.0))
        act = jnp.concatenate(acts, axis=1)
        out = _mm(act.astype(BF16), wdn_ref[0]) + bdn_ref[0]
        gate_c = jnp.sum(pick * gate_e, axis=1, keepdims=True)
        y_ref[...] += _mm(pick.T.astype(BF16), (out * gate_c).astype(BF16))
        return carry

    lax.fori_loop(0, (count + MOE_CAP - 1) // MOE_CAP, sub_block, 0)

    @pl.when(e == pl.num_programs(1) - 1)
    def _():
        o_ref[...] = _layer_norm(DEEPNORM_ALPHA * h1_ref[...] + y_ref[...], g2_ref[...], b2_ref[...])


def _moe(h1, gt, slot, wup, bup, wdn, bdn, g2, b2):
    n = h1.shape[0]
    t = MOE_T
    const = lambda a: pl.BlockSpec(a.shape, lambda i, e: (0,) * a.ndim)
    return pl.pallas_call(
        _moe_kernel,
        grid=(n // t, N_EXPERTS),
        in_specs=[pl.BlockSpec((t, D_MODEL), lambda i, e: (i, 0)),
                  pl.BlockSpec((N_EXPERTS, t), lambda i, e: (0, i)),
                  pl.BlockSpec((N_EXPERTS, t), lambda i, e: (0, i)),
                  pl.BlockSpec((1, D_MODEL, 2 * D_FF), lambda i, e: (e, 0, 0)),
                  pl.BlockSpec((1, 1, 2 * D_FF), lambda i, e: (e, 0, 0)),
                  pl.BlockSpec((1, D_FF, D_MODEL), lambda i, e: (e, 0, 0)),
                  pl.BlockSpec((1, 1, D_MODEL), lambda i, e: (e, 0, 0)),
                  const(g2), const(b2)],
        out_specs=pl.BlockSpec((t, D_MODEL), lambda i, e: (i, 0)),
        out_shape=jax.ShapeDtypeStruct((n, D_MODEL), F32),
        scratch_shapes=[pltpu.VMEM((t, D_MODEL), BF16), pltpu.VMEM((t, D_MODEL), F32)],
        compiler_params=pltpu.CompilerParams(dimension_semantics=("parallel", "arbitrary"),
                                             vmem_limit_bytes=52 * MIB),
        name="moe",
    )(h1, gt, slot, wup, bup, wdn, bdn, g2, b2)


def _layer(x, pos_tab, tab_period, s0, rows_per_stream, ret_rows, dsa_keys, weights):
    wp, wr, wd, wo, g1, b1, rwt, rb, wup, bup, wdn, bdn, g2, b2 = weights
    n = x.shape[0]
    n_streams = n // rows_per_stream
    rq, rk, rv, rg, aq, ak, av, iq, ikw, gr, ga = _project(x, wp, pos_tab, tab_period)
    ret, s_new = _retention(rq, rk, rv, rg, s0, rows_per_stream, ret_rows)
    if dsa_keys is None:
        od = _dsa(aq, iq, ikw, ak, av, ikw, n_streams, DSA_TQ, rows_per_stream, True, rows_per_stream)
    else:
        kk, vv, ikk, n_keys, limit = dsa_keys(ak, av, ikw)
        od = _dsa(aq, iq, ikw, kk, vv, ikk, n_streams, rows_per_stream, n_keys, False, limit)
    h1, gt, slot = _merge(x, ret, od, gr, ga, wr, wd, wo, g1, b1, rwt, rb)
    y = _moe(h1, gt, slot, wup, bup, wdn, bdn, g2, b2)
    return y, s_new, ak, av, ikw[:, :IDX_DIM]


def kernel(x_prompt, x_sample, state_ret, cache_k, cache_v, cache_idx_k, w_in, w_ret_o, w_dsa_o, w_o,
           ln1_g, ln1_b, router_w, router_b, w_up, b_up, w_down, b_down, ln2_g, ln2_b):
    assert w_in.shape[0] == DEPTH
    batch, seq, _ = x_prompt.shape
    dec_batch, dec_seq, _ = x_sample.shape
    past = cache_k.shape[2]
    assert seq % (PROJ_TM) == 0 and seq % DSA_TQ == 0 and PROJ_TM % dec_seq == 0

    l = 0
    weights = (
        _pack_w_in(w_in[l]), w_ret_o[l].astype(BF16), w_dsa_o[l].astype(BF16), w_o[l].astype(BF16),
        ln1_g[l][None, :], ln1_b[l][None, :], router_w[l].T, router_b[l][:, None],
        _deinterleave_w_up(w_up[l]),
        b_up[l].reshape(N_EXPERTS, 2 * D_FF // UP_BLOCK, LANES, 2).transpose(0, 1, 3, 2).reshape(N_EXPERTS, 1, 2 * D_FF),
        w_down[l].astype(BF16), b_down[l][:, None, :], ln2_g[l][None, :], ln2_b[l][None, :])

    tab_p = _rot_tables(jnp.arange(seq))
    zeros_state = jnp.zeros((batch, RET_HEADS, RET_DK, RET_DV), F32)
    y_p, s_p, k_p, v_p, ik_p = _layer(x_prompt.reshape(batch * seq, D_MODEL), tab_p, seq // PROJ_TM, zeros_state,
                                      seq, RET_CHUNK, None, weights)

    n_keys_real = past + dec_seq
    n_keys = -(-n_keys_real // LANES) * LANES
    tab_s = jnp.tile(_rot_tables(past + jnp.arange(dec_seq)), (PROJ_TM // dec_seq, 1))

    def sample_keys(ak, av, ikw):
        def cat(cache, new, width):
            padz = jnp.zeros((dec_batch, n_keys - n_keys_real, width), F32)
            return jnp.concatenate([cache.reshape(dec_batch, past, width), new.reshape(dec_batch, dec_seq, width), padz],
                                   axis=1).reshape(dec_batch * n_keys, width)
        return (cat(cache_k[l], ak, LANES), cat(cache_v[l], av, LANES),
                cat(cache_idx_k[l], ikw[:, :IDX_DIM], IDX_DIM), n_keys, n_keys_real)

    y_s, s_s, k_s, v_s, ik_s = _layer(x_sample.reshape(dec_batch * dec_seq, D_MODEL), tab_s, 1, state_ret[l],
                                      dec_seq, dec_seq, sample_keys, weights)

    kv = (DSA_KV_HEADS, DSA_HEAD_DIM)
    return (y_p.reshape(batch, seq, D_MODEL), y_s.reshape(dec_batch, dec_seq, D_MODEL),
            s_p[None], k_p.reshape(1, batch, seq, *kv), v_p.reshape(1, batch, seq, *kv),
            ik_p.reshape(1, batch, seq, IDX_DIM),
            s_s[None], k_s.reshape(1, dec_batch, dec_seq, *kv), v_s.reshape(1, dec_batch, dec_seq, *kv),
            ik_s.reshape(1, dec_batch, dec_seq, IDX_DIM))
```

```python
import functools

import numpy as np
import jax
import jax.numpy as jnp
from jax import lax
from jax.experimental import pallas as pl
from jax.experimental.pallas import tpu as pltpu

F32 = jnp.float32
BF16 = jnp.bfloat16
I32 = jnp.int32

D_MODEL = 1024
CHUNK = 64
RET_HEADS = 4
RET_DK = 128
RET_DV = 256
RET_ROPE_BASE = 10000.0
DSA_HEADS = 8
DSA_KV_HEADS = 2
DSA_HEAD_DIM = 64
IDX_HEADS = 8
IDX_DIM = 64
DSA_TOPK = 256
ROPE_THETA = 500000.0
N_EXPERTS = 32
MOE_TOP_K = 4
D_FF = 1024
SWIGLU_ALPHA = 1.702
SWIGLU_LIMIT = 7.0
LN_EPS = 1e-5
GN_EPS = 1e-6
DEPTH = 1
DEEPNORM_ALPHA = (2.0 * DEPTH) ** 0.25
PROJ_WIDTHS = (RET_HEADS * RET_DK, RET_HEADS * RET_DK, RET_HEADS * RET_DV, RET_HEADS * RET_DV,
               DSA_HEADS * DSA_HEAD_DIM, DSA_KV_HEADS * DSA_HEAD_DIM, DSA_KV_HEADS * DSA_HEAD_DIM,
               IDX_HEADS * IDX_DIM, IDX_DIM, IDX_HEADS, D_MODEL, D_MODEL)

LANES = 128
MIB = 1024 * 1024

OFF_RQ, OFF_RK, OFF_RV, OFF_RG = 0, 512, 1024, 2048
OFF_AQ, OFF_AK, OFF_AV, OFF_IQ, OFF_IKW = 3072, 3584, 3712, 3840, 4352
OFF_GR, OFF_GA, PACKED_COLS = 4480, 5504, 6528
TAB_COLS = 8 * LANES

PROJ_TM = 256
RET_CHUNK = 128
DSA_TQ = 256
DSA_KB = 256
MERGE_TM = 512
MOE_T = 1024
MOE_CAP = 128
UP_BLOCK = 2 * LANES


def _nt(a, b):
    return lax.dot_general(a, b, (((1,), (1,)), ((), ())), preferred_element_type=F32)


def _mm(a, b):
    return jnp.dot(a, b, preferred_element_type=F32)


def _proj_kernel(x_ref, w_ref, tab_ref, rq_ref, rk_ref, rv_ref, rg_ref, aq_ref, ak_ref, av_ref,
                 iq_ref, ikw_ref, gr_ref, ga_ref):
    xb = x_ref[...].astype(BF16)

    def mm(c0, n):
        return _mm(xb, w_ref[:, c0:c0 + n])

    def tab(i):
        return tab_ref[:, i * LANES:(i + 1) * LANES]

    def rot_full(z):
        return z * tab(0) + pltpu.roll(z, 64, 1) * tab(1)

    def rot_part(z, c):
        return z * tab(c) + pltpu.roll(z, LANES - 8, 1) * tab(c + 1) + pltpu.roll(z, 8, 1) * tab(c + 2)

    z = mm(OFF_RQ, 512)
    for h in range(4):
        sl = slice(h * LANES, (h + 1) * LANES)
        rq_ref[:, sl] = rot_full(z[:, sl]).astype(BF16)
    z = mm(OFF_RK, 512)
    for h in range(4):
        sl = slice(h * LANES, (h + 1) * LANES)
        rk_ref[:, sl] = (rot_full(z[:, sl]) * (RET_DK ** -0.5)).astype(BF16)
    for c in range(2):
        rv_ref[:, c * 512:(c + 1) * 512] = mm(OFF_RV + c * 512, 512).astype(BF16)
    for c in range(2):
        rg_ref[:, c * 512:(c + 1) * 512] = mm(OFF_RG + c * 512, 512)
    z = mm(OFF_AQ, 512)
    for h in range(4):
        sl = slice(h * LANES, (h + 1) * LANES)
        aq_ref[:, sl] = rot_part(z[:, sl], 2).astype(BF16)
    z = mm(OFF_AK, 256)
    ak_ref[...] = rot_part(z[:, :LANES], 2)
    av_ref[...] = z[:, LANES:]
    z = mm(OFF_IQ, 512)
    for h in range(4):
        sl = slice(h * LANES, (h + 1) * LANES)
        iq_ref[:, sl] = rot_part(z[:, sl], 2).astype(BF16)
    ikw_ref[...] = rot_part(mm(OFF_IKW, LANES), 5)
    for c in range(2):
        gr_ref[:, c * 512:(c + 1) * 512] = mm(OFF_GR + c * 512, 512)
    for c in range(2):
        ga_ref[:, c * 512:(c + 1) * 512] = mm(OFF_GA + c * 512, 512)


def _rot_tables(pos):
    p = pos.shape[0]
    posf = pos.astype(F32)[:, None]
    ret_f = RET_ROPE_BASE ** (-jnp.linspace(0.0, 1.0, RET_DK // 2, dtype=F32))
    ang = posf * ret_f[None, :]
    c, s = jnp.cos(ang), jnp.sin(ang)
    cos_r = jnp.concatenate([c, c], 1)
    sin_r = jnp.concatenate([-s, s], 1)
    n_rot = DSA_HEAD_DIM // 4
    att_f = ROPE_THETA ** (-jnp.arange(0, n_rot, 2, dtype=F32) / n_rot)
    ang2 = posf * att_f[None, :]
    c2, s2 = jnp.cos(ang2), jnp.sin(ang2)
    half = n_rot // 2
    rest = DSA_HEAD_DIM - 2 * half
    c64 = jnp.concatenate([c2, c2, jnp.ones((p, rest), F32)], 1)
    s1_64 = jnp.concatenate([-s2, jnp.zeros((p, DSA_HEAD_DIM - half), F32)], 1)
    s2_64 = jnp.concatenate([jnp.zeros((p, half), F32), s2, jnp.zeros((p, rest), F32)], 1)
    z64 = jnp.zeros((p, DSA_HEAD_DIM), F32)
    ci = jnp.concatenate([c64, jnp.full((p, IDX_HEADS), IDX_HEADS ** -0.5, F32),
                          jnp.zeros((p, DSA_HEAD_DIM - IDX_HEADS), F32)], 1)
    return jnp.concatenate([cos_r, sin_r,
                            jnp.concatenate([c64, c64], 1), jnp.concatenate([s1_64, s1_64], 1),
                            jnp.concatenate([s2_64, s2_64], 1),
                            ci, jnp.concatenate([s1_64, z64], 1), jnp.concatenate([s2_64, z64], 1)], 1)


def _pack_w_in(w):
    cuts = np.cumsum(PROJ_WIDTHS)[:-1].tolist()
    rq, rk, rv, rg, aq, ak, av, iq, ik, iw, gr, ga = jnp.split(w, cuts, axis=1)
    pad = jnp.zeros((w.shape[0], LANES - IDX_DIM - IDX_HEADS), w.dtype)
    return jnp.concatenate([rq, rk, rv, rg, aq, ak, av, iq, ik, iw, pad, gr, ga], axis=1).astype(BF16)


def _project(x, wp, tab, tab_period):
    n = x.shape[0]
    tm = PROJ_TM
    row = lambda w: pl.BlockSpec((tm, w), lambda i: (i, 0))
    out_shapes = [((n, 512), BF16), ((n, 512), BF16), ((n, 1024), BF16), ((n, 1024), F32),
                  ((n, 512), BF16), ((n, LANES), F32), ((n, LANES), F32), ((n, 512), BF16),
                  ((n, LANES), F32), ((n, 1024), F32), ((n, 1024), F32)]
    return pl.pallas_call(
        _proj_kernel,
        grid=(n // tm,),
        in_specs=[row(D_MODEL),
                  pl.BlockSpec((D_MODEL, PACKED_COLS), lambda i: (0, 0), pipeline_mode=pl.Buffered(1)),
                  pl.BlockSpec((tm, TAB_COLS), lambda i: (i % tab_period, 0))],
        out_specs=[row(s[1]) for s, _ in out_shapes],
        out_shape=[jax.ShapeDtypeStruct(s, d) for s, d in out_shapes],
        compiler_params=pltpu.CompilerParams(dimension_semantics=("parallel",), vmem_limit_bytes=44 * MIB),
        name="proj",
    )(x, wp, tab)


def _ret_kernel(dec_ref, xi_ref, zeta_ref, rq_ref, rk_ref, rv_ref, rg_ref, s0_ref, ret_ref, sout_ref, st_ref,
                *, rows, n_chunk, g_pow):
    j = pl.program_id(1)
    cpad = RET_CHUNK

    @pl.when(j == 0)
    def _():
        st_ref[...] = s0_ref[0]

    def padded(v):
        if rows == cpad:
            return v
        return jnp.concatenate([v, jnp.zeros((cpad - rows, v.shape[1]), v.dtype)], axis=0)

    for c in range(n_chunk):
        rs = slice(c * rows, (c + 1) * rows)
        for h in range(RET_HEADS):
            ks = slice(h * RET_DK, (h + 1) * RET_DK)
            vs = slice(h * RET_DV, (h + 1) * RET_DV)
            q = padded(rq_ref[rs, ks])
            kt = padded(rk_ref[rs, ks].astype(F32)).T
            v = padded(rv_ref[rs, vs])
            s = st_ref[h]
            sc = _mm(q, kt.astype(BF16)) * dec_ref[h]
            o = _mm(sc.astype(BF16), v) + _mm(q, s.astype(BF16)) * xi_ref[h]
            st_ref[h] = g_pow[h] * s + _mm((kt * zeta_ref[h]).astype(BF16), v)
            o = o[:rows]
            mu = jnp.mean(o, axis=-1, keepdims=True)
            d = o - mu
            var = jnp.mean(d * d, axis=-1, keepdims=True)
            gn = d * lax.rsqrt(var + GN_EPS)
            g = rg_ref[rs, vs]
            ret_ref[rs, vs] = (gn * (g * jax.nn.sigmoid(g))).astype(BF16)

    @pl.when(j == pl.num_programs(1) - 1)
    def _():
        sout_ref[0] = st_ref[...]


def _retention(rq, rk, rv, rg, s0, rows_per_stream, rows):
    n = rq.shape[0]
    n_streams = n // rows_per_stream
    n_chunk = min(rows_per_stream // rows, 2)
    blk = rows * n_chunk
    nb = rows_per_stream // blk
    gam = 1.0 - 2.0 ** (-5.0 - np.arange(RET_HEADS, dtype=np.float64))
    i = np.arange(RET_CHUNK, dtype=np.float64)
    diff = i[:, None] - i[None, :]
    dec = np.where(diff >= 0, gam[:, None, None] ** np.maximum(diff, 0.0)[None], 0.0)
    xi = gam[:, None, None] ** (i + 1.0)[None, :, None]
    zeta = np.where(i < rows, gam[:, None, None] ** (rows - 1.0 - i)[None, None, :], 0.0)
    g_pow = tuple(float(g ** rows) for g in gam)
    const = lambda shape: pl.BlockSpec(shape, lambda s, j: (0,) * len(shape))
    row = lambda w: pl.BlockSpec((blk, w), lambda s, j: (s * nb + j, 0))
    st = pl.BlockSpec((1, RET_HEADS, RET_DK, RET_DV), lambda s, j: (s, 0, 0, 0))
    return pl.pallas_call(
        functools.partial(_ret_kernel, rows=rows, n_chunk=n_chunk, g_pow=g_pow),
        grid=(n_streams, nb),
        in_specs=[const((RET_HEADS, RET_CHUNK, RET_CHUNK)), const((RET_HEADS, RET_CHUNK, 1)),
                  const((RET_HEADS, 1, RET_CHUNK)), row(512), row(512), row(1024), row(1024), st],
        out_specs=[row(1024), st],
        out_shape=[jax.ShapeDtypeStruct((n, RET_HEADS * RET_DV), BF16),
                   jax.ShapeDtypeStruct((n_streams, RET_HEADS, RET_DK, RET_DV), F32)],
        scratch_shapes=[pltpu.VMEM((RET_HEADS, RET_DK, RET_DV), F32)],
        compiler_params=pltpu.CompilerParams(dimension_semantics=("parallel", "arbitrary"),
                                             vmem_limit_bytes=32 * MIB),
        name="retention",
    )(jnp.asarray(dec, F32), jnp.asarray(xi, F32), jnp.asarray(zeta, F32), rq, rk, rv, rg, s0)


def _dsa_kernel(aq_ref, iq_ref, ikwq_ref, kk_ref, vv_ref, ikk_ref, o_ref,
                w_ref, bias_ref, ikb_ref, kg_ref, vg_ref, iqh_ref, aqh_ref, iwb_ref, mp_ref, lp_ref, acc_ref,
                *, tq, n_kb, chunk_causal, limit_const, n_sel):
    nsel_f = float(n_sel)
    kb_sz = DSA_KB

    @pl.when(pl.program_id(1) == 0)
    def _():
        ikb_ref[...] = ikk_ref[...][:, :IDX_DIM].astype(BF16)
        kfull = kk_ref[...]
        vfull = vv_ref[...]
        for g in range(DSA_KV_HEADS):
            gs = slice(g * DSA_HEAD_DIM, (g + 1) * DSA_HEAD_DIM)
            kg_ref[g] = kfull[:, gs].astype(BF16)
            vg_ref[g] = vfull[:, gs].astype(BF16)

    iww = ikwq_ref[...][:, IDX_DIM:IDX_DIM + IDX_HEADS] * (IDX_DIM ** -0.5)
    for p in range(DSA_HEADS // 2):
        islab = iq_ref[:, p * LANES:(p + 1) * LANES]
        aslab = (aq_ref[:, p * LANES:(p + 1) * LANES].astype(F32) * (DSA_HEAD_DIM ** -0.5)).astype(BF16)
        for hh in range(2):
            h = 2 * p + hh
            iqh_ref[h] = islab[:, hh * IDX_DIM:(hh + 1) * IDX_DIM]
            aqh_ref[h] = aslab[:, hh * DSA_HEAD_DIM:(hh + 1) * DSA_HEAD_DIM]
            iwb_ref[h] = jnp.broadcast_to(iww[:, h:h + 1], (tq, LANES))
    if chunk_causal:
        row0 = pl.program_id(1) * tq
        nkb = (row0 + tq + kb_sz - 1) // kb_sz
        n_cols = (nkb * kb_sz).astype(F32)
        limit = (jnp.right_shift(lax.broadcasted_iota(I32, (tq, 1), 0) + row0, 6) + 1) * CHUNK
    else:
        nkb = n_kb
        n_cols = float(n_kb * kb_sz)
        limit = limit_const
    col0 = lax.broadcasted_iota(I32, (tq, kb_sz), 1)
    neg_inf = -jnp.inf

    def fold(x):
        out = x[:, :LANES]
        for c in range(1, kb_sz // LANES):
            out = out + x[:, c * LANES:(c + 1) * LANES]
        return out

    def row_count(pred):
        part = lax.fori_loop(0, nkb, lambda kb, p: p + fold(jnp.where(pred(kb), 1.0, 0.0)),
                             jnp.zeros((tq, LANES), F32))
        return jnp.sum(part, axis=1, keepdims=True)

    def lanes_tiled(x):
        return jnp.concatenate([x] * (kb_sz // LANES), axis=1)

    def score_block(kb, carry):
        k0 = pl.multiple_of(kb * kb_sz, kb_sz)
        ikb = ikb_ref[pl.ds(k0, kb_sz), :]
        acc = jnp.zeros((tq, kb_sz), F32)
        for h in range(IDX_HEADS):
            acc = acc + jnp.maximum(_nt(iqh_ref[h], ikb), 0.0) * lanes_tiled(iwb_ref[h])
        w_ref[kb] = jnp.where(col0 + k0 < limit, acc, neg_inf)
        return carry

    lax.fori_loop(0, nkb, score_block, 0)

    pos = row_count(lambda kb: w_ref[kb] >= 0.0) >= nsel_f
    kk = jnp.where(pos, nsel_f, n_cols - nsel_f + 1.0)

    def flip_sign(kb, carry):
        w_ref[kb] = jnp.where(pos, w_ref[kb], -w_ref[kb])
        return carry

    lax.fori_loop(0, nkb, flip_sign, 0)

    def bit_step(i, u):
        cand_u = u | jnp.left_shift(jnp.int32(1), 30 - i)
        cand = pltpu.bitcast(cand_u, F32)
        return jnp.where(row_count(lambda kb: w_ref[kb] >= cand) >= kk, cand_u, u)

    mag_u = lax.fori_loop(0, 31, bit_step, jnp.zeros((tq, 1), I32))
    mag = pltpu.bitcast(mag_u, F32)
    thr = jnp.where(pos, mag, -mag)
    lax.fori_loop(0, nkb, flip_sign, 0)
    short = row_count(lambda kb: w_ref[kb] >= thr) < nsel_f
    thr = jnp.where(jnp.logical_and(short, jnp.logical_not(pos)), -pltpu.bitcast(mag_u + 1, F32), thr)

    def mask_block(kb, carry):
        c_gt, c_ge = carry
        sc = w_ref[kb]
        ge = sc >= thr
        bias_ref[kb] = jnp.where(jnp.logical_and(ge, sc > neg_inf), 0.0, neg_inf)
        return c_gt + fold(jnp.where(sc > thr, 1.0, 0.0)), c_ge + fold(jnp.where(ge, 1.0, 0.0))

    zero_part = jnp.zeros((tq, LANES), F32)
    c_gt, c_ge = lax.fori_loop(0, nkb, mask_block, (zero_part, zero_part))
    cnt_gt = jnp.sum(c_gt, axis=1, keepdims=True)
    cnt_ge = jnp.sum(c_ge, axis=1, keepdims=True)
    excess = jnp.logical_and(cnt_ge > nsel_f, thr > neg_inf)

    @pl.when(jnp.max(jnp.where(excess, 1.0, 0.0)) > 0.0)
    def _():
        need = nsel_f - cnt_gt
        tri = jnp.where(lax.broadcasted_iota(I32, (LANES, LANES), 0) < lax.broadcasted_iota(I32, (LANES, LANES), 1),
                        1.0, 0.0).astype(BF16)

        def tie_block(kb, before):
            sc = w_ref[kb]
            outs = []
            for c in range(kb_sz // LANES):
                sblk = sc[:, c * LANES:(c + 1) * LANES]
                eq = jnp.where(sblk == thr, 1.0, 0.0)
                rank = _mm(eq.astype(BF16), tri) + before
                keep = jnp.logical_or(sblk > thr, jnp.logical_and(sblk == thr, rank < need))
                outs.append(jnp.where(jnp.logical_and(keep, sblk > neg_inf), 0.0, neg_inf))
                before = before + jnp.sum(eq, axis=1, keepdims=True)
            bias_ref[kb] = jnp.concatenate(outs, axis=1)
            return before

        lax.fori_loop(0, nkb, tie_block, jnp.zeros((tq, 1), F32))

    heads_per_group = DSA_HEADS // DSA_KV_HEADS
    mp_ref[...] = jnp.full(mp_ref.shape, neg_inf, F32)
    lp_ref[...] = jnp.zeros(lp_ref.shape, F32)
    acc_ref[...] = jnp.zeros(acc_ref.shape, F32)

    def logits(h, kb, k0):
        kg = kg_ref[h // heads_per_group, pl.ds(k0, kb_sz), :]
        return _nt(aqh_ref[h], kg) + bias_ref[kb]

    def max_block(kb, carry):
        k0 = pl.multiple_of(kb * kb_sz, kb_sz)
        for h in range(DSA_HEADS):
            lg = logits(h, kb, k0)
            part = lg[:, :LANES]
            for c in range(1, kb_sz // LANES):
                part = jnp.maximum(part, lg[:, c * LANES:(c + 1) * LANES])
            mp_ref[h] = jnp.maximum(mp_ref[h], part)
        return carry

    lax.fori_loop(0, nkb, max_block, 0)
    for h in range(DSA_HEADS):
        m = jnp.max(mp_ref[h], axis=1, keepdims=True)
        mp_ref[h] = jnp.broadcast_to(jnp.where(m == neg_inf, 0.0, m), (tq, LANES))

    def value_block(kb, carry):
        k0 = pl.multiple_of(kb * kb_sz, kb_sz)
        for h in range(DSA_HEADS):
            pr = jnp.exp(logits(h, kb, k0) - lanes_tiled(mp_ref[h]))
            lp_ref[h] += fold(pr)
            acc_ref[h] += _mm(pr.astype(BF16), vg_ref[h // heads_per_group, pl.ds(k0, kb_sz), :])
        return carry

    lax.fori_loop(0, nkb, value_block, 0)
    for p in range(DSA_HEADS // 2):
        o_ref[:, p * LANES:(p + 1) * LANES] = jnp.concatenate(
            [acc_ref[h] / jnp.sum(lp_ref[h], axis=1, keepdims=True) for h in (2 * p, 2 * p + 1)],
            axis=1).astype(BF16)


def _dsa(aq, iq, ikw, kk, vv, ikk, n_streams, tq, n_keys, chunk_causal, limit_const):
    n = aq.shape[0]
    nq = n // n_streams // tq
    n_kb = n_keys // DSA_KB
    n_sel = min(DSA_TOPK, limit_const // 4)
    qrow = lambda w: pl.BlockSpec((tq, w), lambda s, j: (s * nq + j, 0))
    krow = lambda w: pl.BlockSpec((n_keys, w), lambda s, j: (s, 0))
    return pl.pallas_call(
        functools.partial(_dsa_kernel, tq=tq, n_kb=n_kb, chunk_causal=chunk_causal,
                          limit_const=limit_const, n_sel=n_sel),
        grid=(n_streams, nq),
        in_specs=[qrow(512), qrow(512), qrow(LANES), krow(kk.shape[1]), krow(vv.shape[1]), krow(ikk.shape[1])],
        out_specs=qrow(512),
        out_shape=jax.ShapeDtypeStruct((n, DSA_HEADS * DSA_HEAD_DIM), BF16),
        scratch_shapes=[pltpu.VMEM((n_kb, tq, DSA_KB), F32), pltpu.VMEM((n_kb, tq, DSA_KB), F32),
                        pltpu.VMEM((n_keys, IDX_DIM), BF16),
                        pltpu.VMEM((DSA_KV_HEADS, n_keys, DSA_HEAD_DIM), BF16),
                        pltpu.VMEM((DSA_KV_HEADS, n_keys, DSA_HEAD_DIM), BF16),
                        pltpu.VMEM((IDX_HEADS, tq, IDX_DIM), BF16), pltpu.VMEM((DSA_HEADS, tq, DSA_HEAD_DIM), BF16),
                        pltpu.VMEM((IDX_HEADS, tq, LANES), F32),
                        pltpu.VMEM((DSA_HEADS, tq, LANES), F32), pltpu.VMEM((DSA_HEADS, tq, LANES), F32),
                        pltpu.VMEM((DSA_HEADS, tq, DSA_HEAD_DIM), F32)],
        compiler_params=pltpu.CompilerParams(dimension_semantics=("parallel", "arbitrary"),
                                             vmem_limit_bytes=32 * MIB),
        name="dsa",
    )(aq, iq, ikw, kk, vv, ikk)


def _layer_norm(v, g, b):
    mu = jnp.mean(v, axis=-1, keepdims=True)
    d = v - mu
    var = jnp.mean(d * d, axis=-1, keepdims=True)
    return d * lax.rsqrt(var + LN_EPS) * g + b


def _merge_kernel(x_ref, ret_ref, od_ref, gr_ref, ga_ref, wr_ref, wd_ref, wo_ref, g1_ref, b1_ref, rwt_ref, rb_ref,
                  tri_ref, h1_ref, gt_ref, slot_ref, cnt_ref, *, sub_tiles):
    i = pl.program_id(0)

    @pl.when(i % sub_tiles == 0)
    def _():
        cnt_ref[...] = jnp.zeros_like(cnt_ref)

    y_ret = _mm(ret_ref[...], wr_ref[...])
    y_dsa = _mm(od_ref[...], wd_ref[...])
    merged = jax.nn.sigmoid(gr_ref[...]) * y_ret + jax.nn.sigmoid(ga_ref[...]) * y_dsa
    mix = _mm(merged.astype(BF16), wo_ref[...])
    h1 = _layer_norm(DEEPNORM_ALPHA * x_ref[...] + mix, g1_ref[...], b1_ref[...])
    h1_ref[...] = h1

    logits = lax.dot_general(rwt_ref[...], h1, (((1,), (1,)), ((), ())), preferred_element_type=F32,
                             precision=lax.Precision.HIGHEST) + rb_ref[...]
    tm = logits.shape[1]
    e_iota = lax.broadcasted_iota(I32, (N_EXPERTS, tm), 0)
    tops, hots = [], []
    for _ in range(MOE_TOP_K):
        m = jnp.max(logits, axis=0, keepdims=True)
        first = jnp.min(jnp.where(logits == m, e_iota, N_EXPERTS), axis=0, keepdims=True)
        hot = e_iota == first
        tops.append(m)
        hots.append(hot)
        logits = jnp.where(hot, -jnp.inf, logits)
    exps = [jnp.exp(m - tops[0]) for m in tops]
    den = exps[0] + exps[1] + exps[2] + exps[3]
    gates = jnp.zeros((N_EXPERTS, tm), F32)
    sel = jnp.zeros((N_EXPERTS, tm), F32)
    for hot, e in zip(hots, exps):
        gates = gates + jnp.where(hot, e / den, 0.0)
        sel = sel + jnp.where(hot, 1.0, 0.0)
    gt_ref[...] = gates
    rank = _mm(sel.astype(BF16), tri_ref[...]) + cnt_ref[...]
    slot_ref[...] = jnp.where(sel > 0.0, rank, -1.0)
    cnt_ref[...] = cnt_ref[...] + jnp.sum(sel, axis=1, keepdims=True)


def _merge(x, ret, od, gr, ga, wr, wd, wo, g1, b1, rwt, rb):
    n = x.shape[0]
    tm = MERGE_TM
    row = lambda w: pl.BlockSpec((tm, w), lambda i: (i, 0))
    const = lambda a: pl.BlockSpec(a.shape, lambda i: (0,) * a.ndim)
    col = pl.BlockSpec((N_EXPERTS, tm), lambda i: (0, i))
    tri = jnp.asarray(np.triu(np.ones((tm, tm), np.float32), 1), BF16)
    return pl.pallas_call(
        functools.partial(_merge_kernel, sub_tiles=MOE_T // tm),
        grid=(n // tm,),
        in_specs=[row(D_MODEL), row(1024), row(512), row(1024), row(1024), const(wr), const(wd), const(wo),
                  const(g1), const(b1), const(rwt), const(rb), const(tri)],
        out_specs=[row(D_MODEL), col, col],
        out_shape=[jax.ShapeDtypeStruct((n, D_MODEL), F32), jax.ShapeDtypeStruct((N_EXPERTS, n), F32),
                   jax.ShapeDtypeStruct((N_EXPERTS, n), F32)],
        scratch_shapes=[pltpu.VMEM((N_EXPERTS, 1), F32)],
        compiler_params=pltpu.CompilerParams(dimension_semantics=("arbitrary",), vmem_limit_bytes=48 * MIB),
        name="merge",
    )(x, ret, od, gr, ga, wr, wd, wo, g1, b1, rwt, rb, tri)


def _deinterleave_kernel(w_ref, o_ref):
    r = lax.broadcasted_iota(I32, (UP_BLOCK, UP_BLOCK), 0)
    c = lax.broadcasted_iota(I32, (UP_BLOCK, UP_BLOCK), 1)
    src = jnp.where(c < LANES, 2 * c, 2 * (c - LANES) + 1)
    perm = jnp.where(r == src, 1.0, 0.0).astype(BF16)
    for b in range(w_ref.shape[2] // UP_BLOCK):
        sl = slice(b * UP_BLOCK, (b + 1) * UP_BLOCK)
        o_ref[0, :, sl] = _mm(w_ref[0, :, sl].astype(BF16), perm).astype(BF16)


def _deinterleave_w_up(w_up):
    n_e, d_in, d_out = w_up.shape
    cols = 512
    spec = pl.BlockSpec((1, d_in, cols), lambda e, c: (e, 0, c))
    return pl.pallas_call(
        _deinterleave_kernel,
        grid=(n_e, d_out // cols),
        in_specs=[spec],
        out_specs=spec,
        out_shape=jax.ShapeDtypeStruct(w_up.shape, BF16),
        compiler_params=pltpu.CompilerParams(dimension_semantics=("parallel", "parallel"),
                                             vmem_limit_bytes=24 * MIB),
        name="w_up_prep",
    )(w_up)


def _moe_kernel(h1_ref, gt_ref, slot_ref, wup_ref, bup_ref, wdn_ref, bdn_ref, g2_ref, b2_ref, o_ref, xb_ref, y_ref):
    e = pl.program_id(1)
    t = h1_ref.shape[0]

    @pl.when(e == 0)
    def _():
        xb_ref[...] = h1_ref[...].astype(BF16)
        y_ref[...] = jnp.zeros_like(y_ref)

    slot_e = slot_ref[pl.ds(e, 1), :]
    gate_e = gt_ref[pl.ds(e, 1), :]
    count = jnp.sum(jnp.where(slot_e >= 0.0, 1.0, 0.0)).astype(I32)
    jrow = lax.broadcasted_iota(I32, (MOE_CAP, t), 0).astype(F32)

    def sub_block(sb, carry):
        base = (sb * MOE_CAP).astype(F32)
        pick = jnp.where(slot_e - base == jrow, 1.0, 0.0)
        xc = _mm(pick.astype(BF16), xb_ref[...]).astype(BF16)
        h = _mm(xc, wup_ref[0]) + bup_ref[0]
        acts = []
        for b in range(2 * D_FF // UP_BLOCK):
            glu = jnp.minimum(h[:, b * UP_BLOCK:b * UP_BLOCK + LANES], SWIGLU_LIMIT)
            lin = jnp.clip(h[:, b * UP_BLOCK + LANES:(b + 1) * UP_BLOCK], -SWIGLU_LIMIT, SWIGLU_LIMIT)
            acts.append(glu * jax.nn.sigmoid(SWIGLU_ALPHA * glu) * (lin + 1.0))
        act = jnp.concatenate(acts, axis=1)
        out = _mm(act.astype(BF16), wdn_ref[0]) + bdn_ref[0]
        gate_c = jnp.sum(pick * gate_e, axis=1, keepdims=True)
        y_ref[...] += _mm(pick.T.astype(BF16), (out * gate_c).astype(BF16))
        return carry

    lax.fori_loop(0, (count + MOE_CAP - 1) // MOE_CAP, sub_block, 0)

    @pl.when(e == pl.num_programs(1) - 1)
    def _():
        o_ref[...] = _layer_norm(DEEPNORM_ALPHA * h1_ref[...] + y_ref[...], g2_ref[...], b2_ref[...])


def _moe(h1, gt, slot, wup, bup, wdn, bdn, g2, b2):
    n = h1.shape[0]
    t = MOE_T
    const = lambda a: pl.BlockSpec(a.shape, lambda i, e: (0,) * a.ndim)
    return pl.pallas_call(
        _moe_kernel,
        grid=(n // t, N_EXPERTS),
        in_specs=[pl.BlockSpec((t, D_MODEL), lambda i, e: (i, 0)),
                  pl.BlockSpec((N_EXPERTS, t), lambda i, e: (0, i)),
                  pl.BlockSpec((N_EXPERTS, t), lambda i, e: (0, i)),
                  pl.BlockSpec((1, D_MODEL, 2 * D_FF), lambda i, e: (e, 0, 0)),
                  pl.BlockSpec((1, 1, 2 * D_FF), lambda i, e: (e, 0, 0)),
                  pl.BlockSpec((1, D_FF, D_MODEL), lambda i, e: (e, 0, 0)),
                  pl.BlockSpec((1, 1, D_MODEL), lambda i, e: (e, 0, 0)),
                  const(g2), const(b2)],
        out_specs=pl.BlockSpec((t, D_MODEL), lambda i, e: (i, 0)),
        out_shape=jax.ShapeDtypeStruct((n, D_MODEL), F32),
        scratch_shapes=[pltpu.VMEM((t, D_MODEL), BF16), pltpu.VMEM((t, D_MODEL), F32)],
        compiler_params=pltpu.CompilerParams(dimension_semantics=("parallel", "arbitrary"),
                                             vmem_limit_bytes=52 * MIB),
        name="moe",
    )(h1, gt, slot, wup, bup, wdn, bdn, g2, b2)


def _layer(x, pos_tab, tab_period, s0, rows_per_stream, ret_rows, dsa_keys, weights):
    wp, wr, wd, wo, g1, b1, rwt, rb, wup, bup, wdn, bdn, g2, b2 = weights
    n = x.shape[0]
    n_streams = n // rows_per_stream
    rq, rk, rv, rg, aq, ak, av, iq, ikw, gr, ga = _project(x, wp, pos_tab, tab_period)
    ret, s_new = _retention(rq, rk, rv, rg, s0, rows_per_stream, ret_rows)
    if dsa_keys is None:
        od = _dsa(aq, iq, ikw, ak, av, ikw, n_streams, DSA_TQ, rows_per_stream, True, rows_per_stream)
    else:
        kk, vv, ikk, n_keys, limit = dsa_keys(ak, av, ikw)
        od = _dsa(aq, iq, ikw, kk, vv, ikk, n_streams, rows_per_stream, n_keys, False, limit)
    h1, gt, slot = _merge(x, ret, od, gr, ga, wr, wd, wo, g1, b1, rwt, rb)
    y = _moe(h1, gt, slot, wup, bup, wdn, bdn, g2, b2)
    return y, s_new, ak, av, ikw[:, :IDX_DIM]


def kernel(x_prompt, x_sample, state_ret, cache_k, cache_v, cache_idx_k, w_in, w_ret_o, w_dsa_o, w_o,
           ln1_g, ln1_b, router_w, router_b, w_up, b_up, w_down, b_down, ln2_g, ln2_b):
    assert w_in.shape[0] == DEPTH
    batch, seq, _ = x_prompt.shape
    dec_batch, dec_seq, _ = x_sample.shape
    past = cache_k.shape[2]
    assert seq % PROJ_TM == 0 and seq % DSA_TQ == 0 and seq % DSA_KB == 0 and PROJ_TM % dec_seq == 0

    l = 0
    weights = (
        _pack_w_in(w_in[l]), w_ret_o[l].astype(BF16), w_dsa_o[l].astype(BF16), w_o[l].astype(BF16),
        ln1_g[l][None, :], ln1_b[l][None, :], router_w[l].T, router_b[l][:, None],
        _deinterleave_w_up(w_up[l]),
        b_up[l].reshape(N_EXPERTS, 2 * D_FF // UP_BLOCK, LANES, 2).transpose(0, 1, 3, 2).reshape(N_EXPERTS, 1, 2 * D_FF),
        w_down[l].astype(BF16), b_down[l][:, None, :], ln2_g[l][None, :], ln2_b[l][None, :])

    tab_p = _rot_tables(jnp.arange(seq))
    zeros_state = jnp.zeros((batch, RET_HEADS, RET_DK, RET_DV), F32)
    y_p, s_p, k_p, v_p, ik_p = _layer(x_prompt.reshape(batch * seq, D_MODEL), tab_p, seq // PROJ_TM, zeros_state,
                                      seq, RET_CHUNK, None, weights)

    n_keys_real = past + dec_seq
    n_keys = -(-n_keys_real // DSA_KB) * DSA_KB
    tab_s = jnp.tile(_rot_tables(past + jnp.arange(dec_seq)), (PROJ_TM // dec_seq, 1))

    def sample_keys(ak, av, ikw):
        def cat(cache, new, width):
            padz = jnp.zeros((dec_batch, n_keys - n_keys_real, width), F32)
            return jnp.concatenate([cache.reshape(dec_batch, past, width), new.reshape(dec_batch, dec_seq, width), padz],
                                   axis=1).reshape(dec_batch * n_keys, width)
        return (cat(cache_k[l], ak, LANES), cat(cache_v[l], av, LANES),
                cat(cache_idx_k[l], ikw[:, :IDX_DIM], IDX_DIM), n_keys, n_keys_real)

    y_s, s_s, k_s, v_s, ik_s = _layer(x_sample.reshape(dec_batch * dec_seq, D_MODEL), tab_s, 1, state_ret[l],
                                      dec_seq, dec_seq, sample_keys, weights)

    kv = (DSA_KV_HEADS, DSA_HEAD_DIM)
    return (y_p.reshape(batch, seq, D_MODEL), y_s.reshape(dec_batch, dec_seq, D_MODEL),
            s_p[None], k_p.reshape(1, batch, seq, *kv), v_p.reshape(1, batch, seq, *kv),
            ik_p.reshape(1, batch, seq, IDX_DIM),
            s_s[None], k_s.reshape(1, dec_batch, dec_seq, *kv), v_s.reshape(1, dec_batch, dec_seq, *kv),
            ik_s.reshape(1, dec_batch, dec_seq, IDX_DIM))
```

```python
import functools

import numpy as np
import jax
import jax.numpy as jnp
from jax import lax
from jax.experimental import pallas as pl
from jax.experimental.pallas import tpu as pltpu

F32 = jnp.float32
BF16 = jnp.bfloat16
I32 = jnp.int32

D_MODEL = 1024
CHUNK = 64
RET_HEADS = 4
RET_DK = 128
RET_DV = 256
RET_ROPE_BASE = 10000.0
DSA_HEADS = 8
DSA_KV_HEADS = 2
DSA_HEAD_DIM = 64
IDX_HEADS = 8
IDX_DIM = 64
DSA_TOPK = 256
ROPE_THETA = 500000.0
N_EXPERTS = 32
MOE_TOP_K = 4
D_FF = 1024
SWIGLU_ALPHA = 1.702
SWIGLU_LIMIT = 7.0
LN_EPS = 1e-5
GN_EPS = 1e-6
DEPTH = 1
DEEPNORM_ALPHA = (2.0 * DEPTH) ** 0.25
PROJ_WIDTHS = (RET_HEADS * RET_DK, RET_HEADS * RET_DK, RET_HEADS * RET_DV, RET_HEADS * RET_DV,
               DSA_HEADS * DSA_HEAD_DIM, DSA_KV_HEADS * DSA_HEAD_DIM, DSA_KV_HEADS * DSA_HEAD_DIM,
               IDX_HEADS * IDX_DIM, IDX_DIM, IDX_HEADS, D_MODEL, D_MODEL)

LANES = 128
MIB = 1024 * 1024

OFF_RQ, OFF_RK, OFF_RV, OFF_RG = 0, 512, 1024, 2048
OFF_AQ, OFF_AK, OFF_AV, OFF_IQ, OFF_IKW = 3072, 3584, 3712, 3840, 4352
OFF_GR, OFF_GA, PACKED_COLS = 4480, 5504, 6528
TAB_COLS = 8 * LANES

PROJ_TM = 256
RET_CHUNK = 128
DSA_TQ = 256
MERGE_TM = 512
MOE_T = 1024
MOE_CAP = 128
UP_BLOCK = 2 * LANES


def _nt(a, b):
    return lax.dot_general(a, b, (((1,), (1,)), ((), ())), preferred_element_type=F32)


def _mm(a, b):
    return jnp.dot(a, b, preferred_element_type=F32)


def _proj_kernel(x_ref, w_ref, tab_ref, rq_ref, rk_ref, rv_ref, rg_ref, aq_ref, ak_ref, av_ref,
                 iq_ref, ikw_ref, gr_ref, ga_ref):
    xb = x_ref[...].astype(BF16)

    def mm(c0, n):
        return _mm(xb, w_ref[:, c0:c0 + n])

    def tab(i):
        return tab_ref[:, i * LANES:(i + 1) * LANES]

    def rot_full(z):
        return z * tab(0) + pltpu.roll(z, 64, 1) * tab(1)

    def rot_part(z, c):
        return z * tab(c) + pltpu.roll(z, LANES - 8, 1) * tab(c + 1) + pltpu.roll(z, 8, 1) * tab(c + 2)

    z = mm(OFF_RQ, 512)
    for h in range(4):
        sl = slice(h * LANES, (h + 1) * LANES)
        rq_ref[:, sl] = rot_full(z[:, sl]).astype(BF16)
    z = mm(OFF_RK, 512)
    for h in range(4):
        sl = slice(h * LANES, (h + 1) * LANES)
        rk_ref[:, sl] = (rot_full(z[:, sl]) * (RET_DK ** -0.5)).astype(BF16)
    for c in range(2):
        rv_ref[:, c * 512:(c + 1) * 512] = mm(OFF_RV + c * 512, 512).astype(BF16)
    for c in range(2):
        rg_ref[:, c * 512:(c + 1) * 512] = mm(OFF_RG + c * 512, 512)
    z = mm(OFF_AQ, 512)
    for h in range(4):
        sl = slice(h * LANES, (h + 1) * LANES)
        aq_ref[:, sl] = rot_part(z[:, sl], 2).astype(BF16)
    z = mm(OFF_AK, 256)
    ak_ref[...] = rot_part(z[:, :LANES], 2)
    av_ref[...] = z[:, LANES:]
    z = mm(OFF_IQ, 512)
    for h in range(4):
        sl = slice(h * LANES, (h + 1) * LANES)
        iq_ref[:, sl] = rot_part(z[:, sl], 2).astype(BF16)
    ikw_ref[...] = rot_part(mm(OFF_IKW, LANES), 5)
    for c in range(2):
        gr_ref[:, c * 512:(c + 1) * 512] = mm(OFF_GR + c * 512, 512)
    for c in range(2):
        ga_ref[:, c * 512:(c + 1) * 512] = mm(OFF_GA + c * 512, 512)


def _rot_tables(pos):
    p = pos.shape[0]
    posf = pos.astype(F32)[:, None]
    ret_f = RET_ROPE_BASE ** (-jnp.linspace(0.0, 1.0, RET_DK // 2, dtype=F32))
    ang = posf * ret_f[None, :]
    c, s = jnp.cos(ang), jnp.sin(ang)
    cos_r = jnp.concatenate([c, c], 1)
    sin_r = jnp.concatenate([-s, s], 1)
    n_rot = DSA_HEAD_DIM // 4
    att_f = ROPE_THETA ** (-jnp.arange(0, n_rot, 2, dtype=F32) / n_rot)
    ang2 = posf * att_f[None, :]
    c2, s2 = jnp.cos(ang2), jnp.sin(ang2)
    half = n_rot // 2
    rest = DSA_HEAD_DIM - 2 * half
    c64 = jnp.concatenate([c2, c2, jnp.ones((p, rest), F32)], 1)
    s1_64 = jnp.concatenate([-s2, jnp.zeros((p, DSA_HEAD_DIM - half), F32)], 1)
    s2_64 = jnp.concatenate([jnp.zeros((p, half), F32), s2, jnp.zeros((p, rest), F32)], 1)
    z64 = jnp.zeros((p, DSA_HEAD_DIM), F32)
    ci = jnp.concatenate([c64, jnp.full((p, IDX_HEADS), IDX_HEADS ** -0.5, F32),
                          jnp.zeros((p, DSA_HEAD_DIM - IDX_HEADS), F32)], 1)
    return jnp.concatenate([cos_r, sin_r,
                            jnp.concatenate([c64, c64], 1), jnp.concatenate([s1_64, s1_64], 1),
                            jnp.concatenate([s2_64, s2_64], 1),
                            ci, jnp.concatenate([s1_64, z64], 1), jnp.concatenate([s2_64, z64], 1)], 1)


def _pack_w_in(w):
    cuts = np.cumsum(PROJ_WIDTHS)[:-1].tolist()
    rq, rk, rv, rg, aq, ak, av, iq, ik, iw, gr, ga = jnp.split(w, cuts, axis=1)
    pad = jnp.zeros((w.shape[0], LANES - IDX_DIM - IDX_HEADS), w.dtype)
    return jnp.concatenate([rq, rk, rv, rg, aq, ak, av, iq, ik, iw, pad, gr, ga], axis=1).astype(BF16)


def _project(x, wp, tab, tab_period):
    n = x.shape[0]
    tm = PROJ_TM
    row = lambda w: pl.BlockSpec((tm, w), lambda i: (i, 0))
    out_shapes = [((n, 512), BF16), ((n, 512), BF16), ((n, 1024), BF16), ((n, 1024), F32),
                  ((n, 512), BF16), ((n, LANES), F32), ((n, LANES), F32), ((n, 512), BF16),
                  ((n, LANES), F32), ((n, 1024), F32), ((n, 1024), F32)]
    return pl.pallas_call(
        _proj_kernel,
        grid=(n // tm,),
        in_specs=[row(D_MODEL),
                  pl.BlockSpec((D_MODEL, PACKED_COLS), lambda i: (0, 0), pipeline_mode=pl.Buffered(1)),
                  pl.BlockSpec((tm, TAB_COLS), lambda i: (i % tab_period, 0))],
        out_specs=[row(s[1]) for s, _ in out_shapes],
        out_shape=[jax.ShapeDtypeStruct(s, d) for s, d in out_shapes],
        compiler_params=pltpu.CompilerParams(dimension_semantics=("parallel",), vmem_limit_bytes=44 * MIB),
        name="proj",
    )(x, wp, tab)


def _ret_kernel(dec_ref, xi_ref, zeta_ref, rq_ref, rk_ref, rv_ref, rg_ref, s0_ref, ret_ref, sout_ref, st_ref,
                *, rows, n_chunk, g_pow):
    j = pl.program_id(1)
    cpad = RET_CHUNK

    @pl.when(j == 0)
    def _():
        st_ref[...] = s0_ref[0]

    def padded(v):
        if rows == cpad:
            return v
        return jnp.concatenate([v, jnp.zeros((cpad - rows, v.shape[1]), v.dtype)], axis=0)

    for c in range(n_chunk):
        rs = slice(c * rows, (c + 1) * rows)
        for h in range(RET_HEADS):
            ks = slice(h * RET_DK, (h + 1) * RET_DK)
            vs = slice(h * RET_DV, (h + 1) * RET_DV)
            q = padded(rq_ref[rs, ks])
            kt = padded(rk_ref[rs, ks].astype(F32)).T
            v = padded(rv_ref[rs, vs])
            s = st_ref[h]
            sc = _mm(q, kt.astype(BF16)) * dec_ref[h]
            o = _mm(sc.astype(BF16), v) + _mm(q, s.astype(BF16)) * xi_ref[h]
            st_ref[h] = g_pow[h] * s + _mm((kt * zeta_ref[h]).astype(BF16), v)
            o = o[:rows]
            mu = jnp.mean(o, axis=-1, keepdims=True)
            d = o - mu
            var = jnp.mean(d * d, axis=-1, keepdims=True)
            gn = d * lax.rsqrt(var + GN_EPS)
            g = rg_ref[rs, vs]
            ret_ref[rs, vs] = (gn * (g * jax.nn.sigmoid(g))).astype(BF16)

    @pl.when(j == pl.num_programs(1) - 1)
    def _():
        sout_ref[0] = st_ref[...]


def _retention(rq, rk, rv, rg, s0, rows_per_stream, rows):
    n = rq.shape[0]
    n_streams = n // rows_per_stream
    n_chunk = min(rows_per_stream // rows, 2)
    blk = rows * n_chunk
    nb = rows_per_stream // blk
    gam = 1.0 - 2.0 ** (-5.0 - np.arange(RET_HEADS, dtype=np.float64))
    i = np.arange(RET_CHUNK, dtype=np.float64)
    diff = i[:, None] - i[None, :]
    dec = np.where(diff >= 0, gam[:, None, None] ** np.maximum(diff, 0.0)[None], 0.0)
    xi = gam[:, None, None] ** (i + 1.0)[None, :, None]
    zeta = np.where(i < rows, gam[:, None, None] ** (rows - 1.0 - i)[None, None, :], 0.0)
    g_pow = tuple(float(g ** rows) for g in gam)
    const = lambda shape: pl.BlockSpec(shape, lambda s, j: (0,) * len(shape))
    row = lambda w: pl.BlockSpec((blk, w), lambda s, j: (s * nb + j, 0))
    st = pl.BlockSpec((1, RET_HEADS, RET_DK, RET_DV), lambda s, j: (s, 0, 0, 0))
    return pl.pallas_call(
        functools.partial(_ret_kernel, rows=rows, n_chunk=n_chunk, g_pow=g_pow),
        grid=(n_streams, nb),
        in_specs=[const((RET_HEADS, RET_CHUNK, RET_CHUNK)), const((RET_HEADS, RET_CHUNK, 1)),
                  const((RET_HEADS, 1, RET_CHUNK)), row(512), row(512), row(1024), row(1024), st],
        out_specs=[row(1024), st],
        out_shape=[jax.ShapeDtypeStruct((n, RET_HEADS * RET_DV), BF16),
                   jax.ShapeDtypeStruct((n_streams, RET_HEADS, RET_DK, RET_DV), F32)],
        scratch_shapes=[pltpu.VMEM((RET_HEADS, RET_DK, RET_DV), F32)],
        compiler_params=pltpu.CompilerParams(dimension_semantics=("parallel", "arbitrary"),
                                             vmem_limit_bytes=32 * MIB),
        name="retention",
    )(jnp.asarray(dec, F32), jnp.asarray(xi, F32), jnp.asarray(zeta, F32), rq, rk, rv, rg, s0)


def _dsa_kernel(aq_ref, iq_ref, ikwq_ref, kk_ref, vv_ref, ikk_ref, o_ref, w_ref, bias_ref,
                *, tq, n_keys, row0, limit_const, n_sel):
    nsel_f = float(n_sel)
    neg_inf = -jnp.inf

    ikb = ikk_ref[0][:, :IDX_DIM].astype(BF16)
    iww = ikwq_ref[...][:, IDX_DIM:IDX_DIM + IDX_HEADS] * (IDX_DIM ** -0.5)
    acc = jnp.zeros((tq, n_keys), F32)
    for p in range(IDX_HEADS // 2):
        slab = iq_ref[:, p * LANES:(p + 1) * LANES]
        for hh in range(2):
            h = 2 * p + hh
            s = _nt(slab[:, hh * IDX_DIM:(hh + 1) * IDX_DIM], ikb)
            acc = acc + jnp.maximum(s, 0.0) * iww[:, h:h + 1]

    col = lax.broadcasted_iota(I32, (tq, n_keys), 1)
    if limit_const is None:
        row = lax.broadcasted_iota(I32, (tq, 1), 0) + row0
        limit = (jnp.right_shift(row, 6) + 1) * CHUNK
    else:
        limit = limit_const
    admissible = col < limit
    sc = jnp.where(admissible, acc, neg_inf)

    pos = jnp.sum(jnp.where(sc >= 0.0, 1.0, 0.0), axis=1, keepdims=True) >= nsel_f
    kk = jnp.where(pos, nsel_f, float(n_keys - n_sel + 1))
    w_ref[...] = jnp.where(pos, sc, -sc)

    def bit_step(i, u):
        cand_u = u | jnp.left_shift(jnp.int32(1), 30 - i)
        cand = pltpu.bitcast(cand_u, F32)
        cnt = jnp.sum(jnp.where(w_ref[...] >= cand, 1.0, 0.0), axis=1, keepdims=True)
        return jnp.where(cnt >= kk, cand_u, u)

    mag_u = lax.fori_loop(0, 31, bit_step, jnp.zeros((tq, 1), I32))
    mag = pltpu.bitcast(mag_u, F32)
    thr = jnp.where(pos, mag, -mag)

    sc = jnp.where(pos, w_ref[...], -w_ref[...])
    short = jnp.sum(jnp.where(sc >= thr, 1.0, 0.0), axis=1, keepdims=True) < nsel_f
    thr = jnp.where(jnp.logical_and(short, jnp.logical_not(pos)), -pltpu.bitcast(mag_u + 1, F32), thr)
    ge = sc >= thr
    cnt_gt = jnp.sum(jnp.where(sc > thr, 1.0, 0.0), axis=1, keepdims=True)
    cnt_ge = jnp.sum(jnp.where(ge, 1.0, 0.0), axis=1, keepdims=True)
    bias_ref[...] = jnp.where(jnp.logical_and(ge, admissible), 0.0, neg_inf)
    excess = jnp.logical_and(cnt_ge > nsel_f, thr > neg_inf)

    @pl.when(jnp.max(jnp.where(excess, 1.0, 0.0)) > 0.0)
    def _():
        need = nsel_f - cnt_gt
        tri = jnp.where(lax.broadcasted_iota(I32, (LANES, LANES), 0) < lax.broadcasted_iota(I32, (LANES, LANES), 1),
                        1.0, 0.0).astype(BF16)
        before = jnp.zeros((tq, 1), F32)
        for b in range(n_keys // LANES):
            sl = slice(b * LANES, (b + 1) * LANES)
            sblk = jnp.where(pos, w_ref[:, sl], -w_ref[:, sl])
            eq = jnp.where(sblk == thr, 1.0, 0.0)
            rank = _mm(eq.astype(BF16), tri) + before
            keep = jnp.logical_or(sblk > thr, jnp.logical_and(sblk == thr, rank < need))
            bias_ref[:, sl] = jnp.where(jnp.logical_and(keep, sblk > neg_inf), 0.0, neg_inf)
            before = before + jnp.sum(eq, axis=1, keepdims=True)

    kfull = kk_ref[0]
    vfull = vv_ref[0]
    for g in range(DSA_KV_HEADS):
        gs = slice(g * DSA_HEAD_DIM, (g + 1) * DSA_HEAD_DIM)
        kg = kfull[:, gs].astype(BF16)
        vg = vfull[:, gs].astype(BF16)
        for pp in range(2):
            p = 2 * g + pp
            slab = (aq_ref[:, p * LANES:(p + 1) * LANES].astype(F32) * (DSA_HEAD_DIM ** -0.5)).astype(BF16)
            outs = []
            for hh in range(2):
                lg = _nt(slab[:, hh * DSA_HEAD_DIM:(hh + 1) * DSA_HEAD_DIM], kg) + bias_ref[...]
                m = jnp.max(lg, axis=1, keepdims=True)
                pr = jnp.exp(lg - m)
                den = jnp.sum(pr, axis=1, keepdims=True)
                outs.append(_mm(pr.astype(BF16), vg) / den)
            o_ref[:, p * LANES:(p + 1) * LANES] = jnp.concatenate(outs, axis=1).astype(BF16)


def _dsa_call(aq, iq, ikw, kk, vv, ikk, tq, nq, jq, n_keys, limit_const, n_sel, name):
    n_streams = kk.shape[0]
    qrow = lambda w: pl.BlockSpec((tq, w), lambda s: (s * nq + jq, 0))
    krow = lambda a: pl.BlockSpec((1, n_keys, a.shape[2]), lambda s: (s, 0, 0))
    return pl.pallas_call(
        functools.partial(_dsa_kernel, tq=tq, n_keys=n_keys, row0=jq * tq, limit_const=limit_const, n_sel=n_sel),
        grid=(n_streams,),
        in_specs=[qrow(512), qrow(512), qrow(LANES), krow(kk), krow(vv), krow(ikk)],
        out_specs=pl.BlockSpec((tq, DSA_HEADS * DSA_HEAD_DIM), lambda s: (s, 0)),
        out_shape=jax.ShapeDtypeStruct((n_streams * tq, DSA_HEADS * DSA_HEAD_DIM), BF16),
        scratch_shapes=[pltpu.VMEM((tq, n_keys), F32), pltpu.VMEM((tq, n_keys), F32)],
        compiler_params=pltpu.CompilerParams(dimension_semantics=("parallel",), vmem_limit_bytes=48 * MIB),
        name=name,
    )(aq, iq, ikw, kk, vv, ikk)


def _dsa(aq, iq, ikw, kk, vv, ikk, tq, chunk_causal, limit_const):
    n_streams, n_keys, _ = kk.shape
    n = aq.shape[0]
    nq = n // n_streams // tq
    n_sel = min(DSA_TOPK, limit_const // 4)
    if not chunk_causal:
        assert nq == 1
        return _dsa_call(aq, iq, ikw, kk, vv, ikk, tq, 1, 0, n_keys, limit_const, n_sel, "dsa_s")
    parts = [_dsa_call(aq, iq, ikw, kk, vv, ikk, tq, nq, jq, (jq + 1) * tq, None, n_sel, f"dsa_p{jq}")
             for jq in range(nq)]
    width = parts[0].shape[1]
    return jnp.stack(parts, 0).reshape(nq, n_streams, tq, width).transpose(1, 0, 2, 3).reshape(n, width)


def _layer_norm(v, g, b):
    mu = jnp.mean(v, axis=-1, keepdims=True)
    d = v - mu
    var = jnp.mean(d * d, axis=-1, keepdims=True)
    return d * lax.rsqrt(var + LN_EPS) * g + b


def _merge_kernel(x_ref, ret_ref, od_ref, gr_ref, ga_ref, wr_ref, wd_ref, wo_ref, g1_ref, b1_ref, rwt_ref, rb_ref,
                  tri_ref, h1_ref, gt_ref, slot_ref, cnt_ref, *, sub_tiles):
    i = pl.program_id(0)

    @pl.when(i % sub_tiles == 0)
    def _():
        cnt_ref[...] = jnp.zeros_like(cnt_ref)

    y_ret = _mm(ret_ref[...], wr_ref[...])
    y_dsa = _mm(od_ref[...], wd_ref[...])
    merged = jax.nn.sigmoid(gr_ref[...]) * y_ret + jax.nn.sigmoid(ga_ref[...]) * y_dsa
    mix = _mm(merged.astype(BF16), wo_ref[...])
    h1 = _layer_norm(DEEPNORM_ALPHA * x_ref[...] + mix, g1_ref[...], b1_ref[...])
    h1_ref[...] = h1

    logits = lax.dot_general(rwt_ref[...], h1, (((1,), (1,)), ((), ())), preferred_element_type=F32,
                             precision=lax.Precision.HIGHEST) + rb_ref[...]
    tm = logits.shape[1]
    e_iota = lax.broadcasted_iota(I32, (N_EXPERTS, tm), 0)
    tops, hots = [], []
    for _ in range(MOE_TOP_K):
        m = jnp.max(logits, axis=0, keepdims=True)
        first = jnp.min(jnp.where(logits == m, e_iota, N_EXPERTS), axis=0, keepdims=True)
        hot = e_iota == first
        tops.append(m)
        hots.append(hot)
        logits = jnp.where(hot, -jnp.inf, logits)
    exps = [jnp.exp(m - tops[0]) for m in tops]
    den = exps[0] + exps[1] + exps[2] + exps[3]
    gates = jnp.zeros((N_EXPERTS, tm), F32)
    sel = jnp.zeros((N_EXPERTS, tm), F32)
    for hot, e in zip(hots, exps):
        gates = gates + jnp.where(hot, e / den, 0.0)
        sel = sel + jnp.where(hot, 1.0, 0.0)
    gt_ref[...] = gates
    rank = _mm(sel.astype(BF16), tri_ref[...]) + cnt_ref[...]
    slot_ref[...] = jnp.where(sel > 0.0, rank, -1.0)
    cnt_ref[...] = cnt_ref[...] + jnp.sum(sel, axis=1, keepdims=True)


def _merge(x, ret, od, gr, ga, wr, wd, wo, g1, b1, rwt, rb):
    n = x.shape[0]
    tm = MERGE_TM
    row = lambda w: pl.BlockSpec((tm, w), lambda i: (i, 0))
    const = lambda a: pl.BlockSpec(a.shape, lambda i: (0,) * a.ndim)
    col = pl.BlockSpec((N_EXPERTS, tm), lambda i: (0, i))
    tri = jnp.asarray(np.triu(np.ones((tm, tm), np.float32), 1), BF16)
    return pl.pallas_call(
        functools.partial(_merge_kernel, sub_tiles=MOE_T // tm),
        grid=(n // tm,),
        in_specs=[row(D_MODEL), row(1024), row(512), row(1024), row(1024), const(wr), const(wd), const(wo),
                  const(g1), const(b1), const(rwt), const(rb), const(tri)],
        out_specs=[row(D_MODEL), col, col],
        out_shape=[jax.ShapeDtypeStruct((n, D_MODEL), F32), jax.ShapeDtypeStruct((N_EXPERTS, n), F32),
                   jax.ShapeDtypeStruct((N_EXPERTS, n), F32)],
        scratch_shapes=[pltpu.VMEM((N_EXPERTS, 1), F32)],
        compiler_params=pltpu.CompilerParams(dimension_semantics=("arbitrary",), vmem_limit_bytes=48 * MIB),
        name="merge",
    )(x, ret, od, gr, ga, wr, wd, wo, g1, b1, rwt, rb, tri)


def _deinterleave_kernel(w_ref, o_ref):
    r = lax.broadcasted_iota(I32, (UP_BLOCK, UP_BLOCK), 0)
    c = lax.broadcasted_iota(I32, (UP_BLOCK, UP_BLOCK), 1)
    src = jnp.where(c < LANES, 2 * c, 2 * (c - LANES) + 1)
    perm = jnp.where(r == src, 1.0, 0.0).astype(BF16)
    for b in range(w_ref.shape[2] // UP_BLOCK):
        sl = slice(b * UP_BLOCK, (b + 1) * UP_BLOCK)
        o_ref[0, :, sl] = _mm(w_ref[0, :, sl].astype(BF16), perm).astype(BF16)


def _deinterleave_w_up(w_up):
    n_e, d_in, d_out = w_up.shape
    cols = 512
    spec = pl.BlockSpec((1, d_in, cols), lambda e, c: (e, 0, c))
    return pl.pallas_call(
        _deinterleave_kernel,
        grid=(n_e, d_out // cols),
        in_specs=[spec],
        out_specs=spec,
        out_shape=jax.ShapeDtypeStruct(w_up.shape, BF16),
        compiler_params=pltpu.CompilerParams(dimension_semantics=("parallel", "parallel"),
                                             vmem_limit_bytes=24 * MIB),
        name="w_up_prep",
    )(w_up)


def _moe_kernel(h1_ref, gt_ref, slot_ref, wup_ref, bup_ref, wdn_ref, bdn_ref, g2_ref, b2_ref, o_ref, xb_ref, y_ref):
    e = pl.program_id(1)
    t = h1_ref.shape[0]

    @pl.when(e == 0)
    def _():
        xb_ref[...] = h1_ref[...].astype(BF16)
        y_ref[...] = jnp.zeros_like(y_ref)

    slot_e = slot_ref[pl.ds(e, 1), :]
    gate_e = gt_ref[pl.ds(e, 1), :]
    count = jnp.sum(jnp.where(slot_e >= 0.0, 1.0, 0.0)).astype(I32)
    jrow = lax.broadcasted_iota(I32, (MOE_CAP, t), 0).astype(F32)

    def sub_block(sb, carry):
        base = (sb * MOE_CAP).astype(F32)
        pick = jnp.where(slot_e - base == jrow, 1.0, 0.0)
        xc = _mm(pick.astype(BF16), xb_ref[...]).astype(BF16)
        h = _mm(xc, wup_ref[0]) + bup_ref[0]
        acts = []
        for b in range(2 * D_FF // UP_BLOCK):
            glu = jnp.minimum(h[:, b * UP_BLOCK:b * UP_BLOCK + LANES], SWIGLU_LIMIT)
            lin = jnp.clip(h[:, b * UP_BLOCK + LANES:(b + 1) * UP_BLOCK], -SWIGLU_LIMIT, SWIGLU_LIMIT)
            acts.append(glu * jax.nn.sigmoid(SWIGLU_ALPHA * glu) * (lin + 1.0))
        act = jnp.concatenate(acts, axis=1)
        out = _mm(act.astype(BF16), wdn_ref[0]) + bdn_ref[0]
        gate_c = jnp.sum(pick * gate_e, axis=1, keepdims=True)
        y_ref[...] += _mm(pick.T.astype(BF16), (out * gate_c).astype(BF16))
        return carry

    lax.fori_loop(0, (count + MOE_CAP - 1) // MOE_CAP, sub_block, 0)

    @pl.when(e == pl.num_programs(1) - 1)
    def _():
        o_ref[...] = _layer_norm(DEEPNORM_ALPHA * h1_ref[...] + y_ref[...], g2_ref[...], b2_ref[...])


def _moe(h1, gt, slot, wup, bup, wdn, bdn, g2, b2):
    n = h1.shape[0]
    t = MOE_T
    const = lambda a: pl.BlockSpec(a.shape, lambda i, e: (0,) * a.ndim)
    return pl.pallas_call(
        _moe_kernel,
        grid=(n // t, N_EXPERTS),
        in_specs=[pl.BlockSpec((t, D_MODEL), lambda i, e: (i, 0)),
                  pl.BlockSpec((N_EXPERTS, t), lambda i, e: (0, i)),
                  pl.BlockSpec((N_EXPERTS, t), lambda i, e: (0, i)),
                  pl.BlockSpec((1, D_MODEL, 2 * D_FF), lambda i, e: (e, 0, 0)),
                  pl.BlockSpec((1, 1, 2 * D_FF), lambda i, e: (e, 0, 0)),
                  pl.BlockSpec((1, D_FF, D_MODEL), lambda i, e: (e, 0, 0)),
                  pl.BlockSpec((1, 1, D_MODEL), lambda i, e: (e, 0, 0)),
                  const(g2), const(b2)],
        out_specs=pl.BlockSpec((t, D_MODEL), lambda i, e: (i, 0)),
        out_shape=jax.ShapeDtypeStruct((n, D_MODEL), F32),
        scratch_shapes=[pltpu.VMEM((t, D_MODEL), BF16), pltpu.VMEM((t, D_MODEL), F32)],
        compiler_params=pltpu.CompilerParams(dimension_semantics=("parallel", "arbitrary"),
                                             vmem_limit_bytes=52 * MIB),
        name="moe",
    )(h1, gt, slot, wup, bup, wdn, bdn, g2, b2)


def _layer(x, pos_tab, tab_period, s0, rows_per_stream, ret_rows, dsa_keys, weights):
    wp, wr, wd, wo, g1, b1, rwt, rb, wup, bup, wdn, bdn, g2, b2 = weights
    n = x.shape[0]
    n_streams = n // rows_per_stream
    rq, rk, rv, rg, aq, ak, av, iq, ikw, gr, ga = _project(x, wp, pos_tab, tab_period)
    ret, s_new = _retention(rq, rk, rv, rg, s0, rows_per_stream, ret_rows)
    if dsa_keys is None:
        per_stream = lambda a: a.reshape(n_streams, rows_per_stream, a.shape[1])
        od = _dsa(aq, iq, ikw, per_stream(ak), per_stream(av), per_stream(ikw), DSA_TQ, True, rows_per_stream)
    else:
        kk, vv, ikk, limit = dsa_keys(ak, av, ikw)
        od = _dsa(aq, iq, ikw, kk, vv, ikk, rows_per_stream, False, limit)
    h1, gt, slot = _merge(x, ret, od, gr, ga, wr, wd, wo, g1, b1, rwt, rb)
    y = _moe(h1, gt, slot, wup, bup, wdn, bdn, g2, b2)
    return y, s_new, ak, av, ikw[:, :IDX_DIM]


def kernel(x_prompt, x_sample, state_ret, cache_k, cache_v, cache_idx_k, w_in, w_ret_o, w_dsa_o, w_o,
           ln1_g, ln1_b, router_w, router_b, w_up, b_up, w_down, b_down, ln2_g, ln2_b):
    assert w_in.shape[0] == DEPTH
    batch, seq, _ = x_prompt.shape
    dec_batch, dec_seq, _ = x_sample.shape
    past = cache_k.shape[2]
    assert seq % PROJ_TM == 0 and seq % DSA_TQ == 0 and PROJ_TM % dec_seq == 0

    l = 0
    weights = (
        _pack_w_in(w_in[l]), w_ret_o[l].astype(BF16), w_dsa_o[l].astype(BF16), w_o[l].astype(BF16),
        ln1_g[l][None, :], ln1_b[l][None, :], router_w[l].T, router_b[l][:, None],
        _deinterleave_w_up(w_up[l]),
        b_up[l].reshape(N_EXPERTS, 2 * D_FF // UP_BLOCK, LANES, 2).transpose(0, 1, 3, 2).reshape(N_EXPERTS, 1, 2 * D_FF),
        w_down[l].astype(BF16), b_down[l][:, None, :], ln2_g[l][None, :], ln2_b[l][None, :])

    tab_p = _rot_tables(jnp.arange(seq))
    zeros_state = jnp.zeros((batch, RET_HEADS, RET_DK, RET_DV), F32)
    y_p, s_p, k_p, v_p, ik_p = _layer(x_prompt.reshape(batch * seq, D_MODEL), tab_p, seq // PROJ_TM, zeros_state,
                                      seq, RET_CHUNK, None, weights)

    n_keys_real = past + dec_seq
    n_keys = -(-n_keys_real // LANES) * LANES
    tab_s = jnp.tile(_rot_tables(past + jnp.arange(dec_seq)), (PROJ_TM // dec_seq, 1))

    def sample_keys(ak, av, ikw):
        def cat(cache, new, width):
            padz = jnp.zeros((dec_batch, n_keys - n_keys_real, width), F32)
            return jnp.concatenate([cache.reshape(dec_batch, past, width), new.reshape(dec_batch, dec_seq, width), padz],
                                   axis=1)
        return (cat(cache_k[l], ak, LANES), cat(cache_v[l], av, LANES),
                cat(cache_idx_k[l], ikw[:, :IDX_DIM], IDX_DIM), n_keys_real)

    y_s, s_s, k_s, v_s, ik_s = _layer(x_sample.reshape(dec_batch * dec_seq, D_MODEL), tab_s, 1, state_ret[l],
                                      dec_seq, dec_seq, sample_keys, weights)

    kv = (DSA_KV_HEADS, DSA_HEAD_DIM)
    return (y_p.reshape(batch, seq, D_MODEL), y_s.reshape(dec_batch, dec_seq, D_MODEL),
            s_p[None], k_p.reshape(1, batch, seq, *kv), v_p.reshape(1, batch, seq, *kv),
            ik_p.reshape(1, batch, seq, IDX_DIM),
            s_s[None], k_s.reshape(1, dec_batch, dec_seq, *kv), v_s.reshape(1, dec_batch, dec_seq, *kv),
            ik_s.reshape(1, dec_batch, dec_seq, IDX_DIM))
```

```python
import functools

import numpy as np
import jax
import jax.numpy as jnp
from jax import lax
from jax.experimental import pallas as pl
from jax.experimental.pallas import tpu as pltpu

F32 = jnp.float32
BF16 = jnp.bfloat16
I32 = jnp.int32

D_MODEL = 1024
CHUNK = 64
RET_HEADS = 4
RET_DK = 128
RET_DV = 256
RET_ROPE_BASE = 10000.0
DSA_HEADS = 8
DSA_KV_HEADS = 2
DSA_HEAD_DIM = 64
IDX_HEADS = 8
IDX_DIM = 64
DSA_TOPK = 256
ROPE_THETA = 500000.0
N_EXPERTS = 32
MOE_TOP_K = 4
D_FF = 1024
SWIGLU_ALPHA = 1.702
SWIGLU_LIMIT = 7.0
LN_EPS = 1e-5
GN_EPS = 1e-6
DEPTH = 1
DEEPNORM_ALPHA = (2.0 * DEPTH) ** 0.25
PROJ_WIDTHS = (RET_HEADS * RET_DK, RET_HEADS * RET_DK, RET_HEADS * RET_DV, RET_HEADS * RET_DV,
               DSA_HEADS * DSA_HEAD_DIM, DSA_KV_HEADS * DSA_HEAD_DIM, DSA_KV_HEADS * DSA_HEAD_DIM,
               IDX_HEADS * IDX_DIM, IDX_DIM, IDX_HEADS, D_MODEL, D_MODEL)

LANES = 128
MIB = 1024 * 1024

OFF_RQ, OFF_RK, OFF_RV, OFF_RG = 0, 512, 1024, 2048
OFF_AQ, OFF_AK, OFF_AV, OFF_IQ, OFF_IKW = 3072, 3584, 3712, 3840, 4352
OFF_GR, OFF_GA, PACKED_COLS = 4480, 5504, 6528
TAB_COLS = 8 * LANES

PROJ_TM = 256
RET_CHUNK = 128
DSA_TQ = 256
MERGE_TM = 512
MOE_T = 1024
MOE_CAP = 160
MOE_CAP_PAD = 256
UP_BLOCK = 2 * LANES


def _nt(a, b):
    return lax.dot_general(a, b, (((1,), (1,)), ((), ())), preferred_element_type=F32)


def _mm(a, b):
    return jnp.dot(a, b, preferred_element_type=F32)


def _proj_kernel(x_ref, w_ref, tab_ref, rq_ref, rk_ref, rv_ref, rg_ref, aq_ref, ak_ref, av_ref,
                 iq_ref, ikw_ref, gr_ref, ga_ref):
    xb = x_ref[...].astype(BF16)

    def mm(c0, n):
        return _mm(xb, w_ref[:, c0:c0 + n])

    def tab(i):
        return tab_ref[:, i * LANES:(i + 1) * LANES]

    def rot_full(z):
        return z * tab(0) + pltpu.roll(z, 64, 1) * tab(1)

    def rot_part(z, c):
        return z * tab(c) + pltpu.roll(z, LANES - 8, 1) * tab(c + 1) + pltpu.roll(z, 8, 1) * tab(c + 2)

    z = mm(OFF_RQ, 512)
    for h in range(4):
        sl = slice(h * LANES, (h + 1) * LANES)
        rq_ref[:, sl] = rot_full(z[:, sl]).astype(BF16)
    z = mm(OFF_RK, 512)
    for h in range(4):
        sl = slice(h * LANES, (h + 1) * LANES)
        rk_ref[:, sl] = (rot_full(z[:, sl]) * (RET_DK ** -0.5)).astype(BF16)
    for c in range(2):
        rv_ref[:, c * 512:(c + 1) * 512] = mm(OFF_RV + c * 512, 512).astype(BF16)
    for c in range(2):
        rg_ref[:, c * 512:(c + 1) * 512] = mm(OFF_RG + c * 512, 512)
    z = mm(OFF_AQ, 512)
    for h in range(4):
        sl = slice(h * LANES, (h + 1) * LANES)
        aq_ref[:, sl] = rot_part(z[:, sl], 2).astype(BF16)
    z = mm(OFF_AK, 256)
    ak_ref[...] = rot_part(z[:, :LANES], 2)
    av_ref[...] = z[:, LANES:]
    z = mm(OFF_IQ, 512)
    for h in range(4):
        sl = slice(h * LANES, (h + 1) * LANES)
        iq_ref[:, sl] = rot_part(z[:, sl], 2).astype(BF16)
    ikw_ref[...] = rot_part(mm(OFF_IKW, LANES), 5)
    for c in range(2):
        gr_ref[:, c * 512:(c + 1) * 512] = mm(OFF_GR + c * 512, 512)
    for c in range(2):
        ga_ref[:, c * 512:(c + 1) * 512] = mm(OFF_GA + c * 512, 512)


def _rot_tables(pos):
    p = pos.shape[0]
    posf = pos.astype(F32)[:, None]
    ret_f = RET_ROPE_BASE ** (-jnp.linspace(0.0, 1.0, RET_DK // 2, dtype=F32))
    ang = posf * ret_f[None, :]
    c, s = jnp.cos(ang), jnp.sin(ang)
    cos_r = jnp.concatenate([c, c], 1)
    sin_r = jnp.concatenate([-s, s], 1)
    n_rot = DSA_HEAD_DIM // 4
    att_f = ROPE_THETA ** (-jnp.arange(0, n_rot, 2, dtype=F32) / n_rot)
    ang2 = posf * att_f[None, :]
    c2, s2 = jnp.cos(ang2), jnp.sin(ang2)
    half = n_rot // 2
    rest = DSA_HEAD_DIM - 2 * half
    c64 = jnp.concatenate([c2, c2, jnp.ones((p, rest), F32)], 1)
    s1_64 = jnp.concatenate([-s2, jnp.zeros((p, DSA_HEAD_DIM - half), F32)], 1)
    s2_64 = jnp.concatenate([jnp.zeros((p, half), F32), s2, jnp.zeros((p, rest), F32)], 1)
    z64 = jnp.zeros((p, DSA_HEAD_DIM), F32)
    ci = jnp.concatenate([c64, jnp.full((p, IDX_HEADS), IDX_HEADS ** -0.5, F32),
                          jnp.zeros((p, DSA_HEAD_DIM - IDX_HEADS), F32)], 1)
    return jnp.concatenate([cos_r, sin_r,
                            jnp.concatenate([c64, c64], 1), jnp.concatenate([s1_64, s1_64], 1),
                            jnp.concatenate([s2_64, s2_64], 1),
                            ci, jnp.concatenate([s1_64, z64], 1), jnp.concatenate([s2_64, z64], 1)], 1)


def _pack_w_in(w):
    cuts = np.cumsum(PROJ_WIDTHS)[:-1].tolist()
    rq, rk, rv, rg, aq, ak, av, iq, ik, iw, gr, ga = jnp.split(w, cuts, axis=1)
    pad = jnp.zeros((w.shape[0], LANES - IDX_DIM - IDX_HEADS), w.dtype)
    return jnp.concatenate([rq, rk, rv, rg, aq, ak, av, iq, ik, iw, pad, gr, ga], axis=1).astype(BF16)


def _project(x, wp, tab, tab_period):
    n = x.shape[0]
    tm = PROJ_TM
    row = lambda w: pl.BlockSpec((tm, w), lambda i: (i, 0))
    out_shapes = [((n, 512), BF16), ((n, 512), BF16), ((n, 1024), BF16), ((n, 1024), F32),
                  ((n, 512), BF16), ((n, LANES), F32), ((n, LANES), F32), ((n, 512), BF16),
                  ((n, LANES), F32), ((n, 1024), F32), ((n, 1024), F32)]
    return pl.pallas_call(
        _proj_kernel,
        grid=(n // tm,),
        in_specs=[row(D_MODEL),
                  pl.BlockSpec((D_MODEL, PACKED_COLS), lambda i: (0, 0), pipeline_mode=pl.Buffered(1)),
                  pl.BlockSpec((tm, TAB_COLS), lambda i: (i % tab_period, 0))],
        out_specs=[row(s[1]) for s, _ in out_shapes],
        out_shape=[jax.ShapeDtypeStruct(s, d) for s, d in out_shapes],
        compiler_params=pltpu.CompilerParams(dimension_semantics=("parallel",), vmem_limit_bytes=44 * MIB),
        name="proj",
    )(x, wp, tab)


def _ret_kernel(dec_ref, xi_ref, zeta_ref, rq_ref, rk_ref, rv_ref, rg_ref, s0_ref, ret_ref, sout_ref, st_ref,
                *, rows, n_chunk, g_pow):
    j = pl.program_id(1)
    cpad = RET_CHUNK

    @pl.when(j == 0)
    def _():
        st_ref[...] = s0_ref[0]

    def padded(v):
        if rows == cpad:
            return v
        return jnp.concatenate([v, jnp.zeros((cpad - rows, v.shape[1]), v.dtype)], axis=0)

    for c in range(n_chunk):
        rs = slice(c * rows, (c + 1) * rows)
        for h in range(RET_HEADS):
            ks = slice(h * RET_DK, (h + 1) * RET_DK)
            vs = slice(h * RET_DV, (h + 1) * RET_DV)
            q = padded(rq_ref[rs, ks])
            kt = padded(rk_ref[rs, ks].astype(F32)).T
            v = padded(rv_ref[rs, vs])
            s = st_ref[h]
            sc = _mm(q, kt.astype(BF16)) * dec_ref[h]
            o = _mm(sc.astype(BF16), v) + _mm(q, s.astype(BF16)) * xi_ref[h]
            st_ref[h] = g_pow[h] * s + _mm((kt * zeta_ref[h]).astype(BF16), v)
            o = o[:rows]
            mu = jnp.mean(o, axis=-1, keepdims=True)
            d = o - mu
            var = jnp.mean(d * d, axis=-1, keepdims=True)
            gn = d * lax.rsqrt(var + GN_EPS)
            g = rg_ref[rs, vs]
            ret_ref[rs, vs] = (gn * (g * jax.nn.sigmoid(g))).astype(BF16)

    @pl.when(j == pl.num_programs(1) - 1)
    def _():
        sout_ref[0] = st_ref[...]


def _retention(rq, rk, rv, rg, s0, rows_per_stream, rows):
    n = rq.shape[0]
    n_streams = n // rows_per_stream
    n_chunk = min(rows_per_stream // rows, 2)
    blk = rows * n_chunk
    nb = rows_per_stream // blk
    gam = 1.0 - 2.0 ** (-5.0 - np.arange(RET_HEADS, dtype=np.float64))
    i = np.arange(RET_CHUNK, dtype=np.float64)
    diff = i[:, None] - i[None, :]
    dec = np.where(diff >= 0, gam[:, None, None] ** np.maximum(diff, 0.0)[None], 0.0)
    xi = gam[:, None, None] ** (i + 1.0)[None, :, None]
    zeta = np.where(i < rows, gam[:, None, None] ** (rows - 1.0 - i)[None, None, :], 0.0)
    g_pow = tuple(float(g ** rows) for g in gam)
    const = lambda shape: pl.BlockSpec(shape, lambda s, j: (0,) * len(shape))
    row = lambda w: pl.BlockSpec((blk, w), lambda s, j: (s * nb + j, 0))
    st = pl.BlockSpec((1, RET_HEADS, RET_DK, RET_DV), lambda s, j: (s, 0, 0, 0))
    return pl.pallas_call(
        functools.partial(_ret_kernel, rows=rows, n_chunk=n_chunk, g_pow=g_pow),
        grid=(n_streams, nb),
        in_specs=[const((RET_HEADS, RET_CHUNK, RET_CHUNK)), const((RET_HEADS, RET_CHUNK, 1)),
                  const((RET_HEADS, 1, RET_CHUNK)), row(512), row(512), row(1024), row(1024), st],
        out_specs=[row(1024), st],
        out_shape=[jax.ShapeDtypeStruct((n, RET_HEADS * RET_DV), BF16),
                   jax.ShapeDtypeStruct((n_streams, RET_HEADS, RET_DK, RET_DV), F32)],
        scratch_shapes=[pltpu.VMEM((RET_HEADS, RET_DK, RET_DV), F32)],
        compiler_params=pltpu.CompilerParams(dimension_semantics=("parallel", "arbitrary"),
                                             vmem_limit_bytes=32 * MIB),
        name="retention",
    )(jnp.asarray(dec, F32), jnp.asarray(xi, F32), jnp.asarray(zeta, F32), rq, rk, rv, rg, s0)


def _dsa_kernel(aq_ref, iq_ref, ikwq_ref, kk_ref, vv_ref, ikk_ref, o_ref, w_ref, bias_ref,
                *, tq, n_keys, row0, limit_const, n_sel):
    nsel_f = float(n_sel)
    neg_inf = -jnp.inf

    ikb = ikk_ref[0][:, :IDX_DIM].astype(BF16)
    iww = ikwq_ref[...][:, IDX_DIM:IDX_DIM + IDX_HEADS] * (IDX_DIM ** -0.5)
    acc = jnp.zeros((tq, n_keys), F32)
    for p in range(IDX_HEADS // 2):
        slab = iq_ref[:, p * LANES:(p + 1) * LANES]
        for hh in range(2):
            h = 2 * p + hh
            s = _nt(slab[:, hh * IDX_DIM:(hh + 1) * IDX_DIM], ikb)
            acc = acc + jnp.maximum(s, 0.0) * iww[:, h:h + 1]

    col = lax.broadcasted_iota(I32, (tq, n_keys), 1)
    if limit_const is None:
        row = lax.broadcasted_iota(I32, (tq, 1), 0) + row0
        limit = (jnp.right_shift(row, 6) + 1) * CHUNK
    else:
        limit = limit_const
    admissible = col < limit
    sc = jnp.where(admissible, acc, neg_inf)

    pos = jnp.sum(jnp.where(sc >= 0.0, 1.0, 0.0), axis=1, keepdims=True) >= nsel_f
    kk = jnp.where(pos, nsel_f, float(n_keys - n_sel + 1))
    w_ref[...] = jnp.where(pos, sc, -sc)

    def bit_step(i, u):
        cand_u = u | jnp.left_shift(jnp.int32(1), 30 - i)
        cand = pltpu.bitcast(cand_u, F32)
        cnt = jnp.sum(jnp.where(w_ref[...] >= cand, 1.0, 0.0), axis=1, keepdims=True)
        return jnp.where(cnt >= kk, cand_u, u)

    mag_u = lax.fori_loop(0, 31, bit_step, jnp.zeros((tq, 1), I32))
    mag = pltpu.bitcast(mag_u, F32)
    thr = jnp.where(pos, mag, -mag)

    sc = jnp.where(pos, w_ref[...], -w_ref[...])
    short = jnp.sum(jnp.where(sc >= thr, 1.0, 0.0), axis=1, keepdims=True) < nsel_f
    thr = jnp.where(jnp.logical_and(short, jnp.logical_not(pos)), -pltpu.bitcast(mag_u + 1, F32), thr)
    ge = sc >= thr
    cnt_gt = jnp.sum(jnp.where(sc > thr, 1.0, 0.0), axis=1, keepdims=True)
    cnt_ge = jnp.sum(jnp.where(ge, 1.0, 0.0), axis=1, keepdims=True)
    bias_ref[...] = jnp.where(jnp.logical_and(ge, admissible), 0.0, neg_inf)
    excess = jnp.logical_and(cnt_ge > nsel_f, thr > neg_inf)

    @pl.when(jnp.max(jnp.where(excess, 1.0, 0.0)) > 0.0)
    def _():
        need = nsel_f - cnt_gt
        tri = jnp.where(lax.broadcasted_iota(I32, (LANES, LANES), 0) < lax.broadcasted_iota(I32, (LANES, LANES), 1),
                        1.0, 0.0).astype(BF16)
        before = jnp.zeros((tq, 1), F32)
        for b in range(n_keys // LANES):
            sl = slice(b * LANES, (b + 1) * LANES)
            sblk = jnp.where(pos, w_ref[:, sl], -w_ref[:, sl])
            eq = jnp.where(sblk == thr, 1.0, 0.0)
            rank = _mm(eq.astype(BF16), tri) + before
            keep = jnp.logical_or(sblk > thr, jnp.logical_and(sblk == thr, rank < need))
            bias_ref[:, sl] = jnp.where(jnp.logical_and(keep, sblk > neg_inf), 0.0, neg_inf)
            before = before + jnp.sum(eq, axis=1, keepdims=True)

    kfull = kk_ref[0]
    vfull = vv_ref[0]
    for g in range(DSA_KV_HEADS):
        gs = slice(g * DSA_HEAD_DIM, (g + 1) * DSA_HEAD_DIM)
        kg = kfull[:, gs].astype(BF16)
        vg = vfull[:, gs].astype(BF16)
        for pp in range(2):
            p = 2 * g + pp
            slab = (aq_ref[:, p * LANES:(p + 1) * LANES].astype(F32) * (DSA_HEAD_DIM ** -0.5)).astype(BF16)
            outs = []
            for hh in range(2):
                lg = _nt(slab[:, hh * DSA_HEAD_DIM:(hh + 1) * DSA_HEAD_DIM], kg) + bias_ref[...]
                m = jnp.max(lg, axis=1, keepdims=True)
                pr = jnp.exp(lg - m)
                den = jnp.sum(pr, axis=1, keepdims=True)
                outs.append(_mm(pr.astype(BF16), vg) / den)
            o_ref[:, p * LANES:(p + 1) * LANES] = jnp.concatenate(outs, axis=1).astype(BF16)


def _dsa_call(aq, iq, ikw, kk, vv, ikk, tq, nq, jq, n_keys, limit_const, n_sel, name):
    n_streams = kk.shape[0]
    qrow = lambda w: pl.BlockSpec((tq, w), lambda s: (s * nq + jq, 0))
    krow = lambda a: pl.BlockSpec((1, n_keys, a.shape[2]), lambda s: (s, 0, 0))
    return pl.pallas_call(
        functools.partial(_dsa_kernel, tq=tq, n_keys=n_keys, row0=jq * tq, limit_const=limit_const, n_sel=n_sel),
        grid=(n_streams,),
        in_specs=[qrow(512), qrow(512), qrow(LANES), krow(kk), krow(vv), krow(ikk)],
        out_specs=pl.BlockSpec((tq, DSA_HEADS * DSA_HEAD_DIM), lambda s: (s, 0)),
        out_shape=jax.ShapeDtypeStruct((n_streams * tq, DSA_HEADS * DSA_HEAD_DIM), BF16),
        scratch_shapes=[pltpu.VMEM((tq, n_keys), F32), pltpu.VMEM((tq, n_keys), F32)],
        compiler_params=pltpu.CompilerParams(dimension_semantics=("parallel",), vmem_limit_bytes=48 * MIB),
        name=name,
    )(aq, iq, ikw, kk, vv, ikk)


def _dsa(aq, iq, ikw, kk, vv, ikk, tq, chunk_causal, limit_const):
    n_streams, n_keys, _ = kk.shape
    n = aq.shape[0]
    nq = n // n_streams // tq
    n_sel = min(DSA_TOPK, limit_const // 4)
    if not chunk_causal:
        assert nq == 1
        return _dsa_call(aq, iq, ikw, kk, vv, ikk, tq, 1, 0, n_keys, limit_const, n_sel, "dsa_s")
    parts = [_dsa_call(aq, iq, ikw, kk, vv, ikk, tq, nq, jq, (jq + 1) * tq, None, n_sel, f"dsa_p{jq}")
             for jq in range(nq)]
    width = parts[0].shape[1]
    return jnp.stack(parts, 0).reshape(nq, n_streams, tq, width).transpose(1, 0, 2, 3).reshape(n, width)


def _layer_norm(v, g, b):
    mu = jnp.mean(v, axis=-1, keepdims=True)
    d = v - mu
    var = jnp.mean(d * d, axis=-1, keepdims=True)
    return d * lax.rsqrt(var + LN_EPS) * g + b


def _merge_kernel(x_ref, ret_ref, od_ref, gr_ref, ga_ref, wr_ref, wd_ref, wo_ref, g1_ref, b1_ref, rwt_ref, rb_ref,
                  tri_ref, h1_ref, gt_ref, slot_ref, cnt_ref, *, sub_tiles):
    i = pl.program_id(0)

    @pl.when(i % sub_tiles == 0)
    def _():
        cnt_ref[...] = jnp.zeros_like(cnt_ref)

    y_ret = _mm(ret_ref[...], wr_ref[...])
    y_dsa = _mm(od_ref[...], wd_ref[...])
    merged = jax.nn.sigmoid(gr_ref[...]) * y_ret + jax.nn.sigmoid(ga_ref[...]) * y_dsa
    mix = _mm(merged.astype(BF16), wo_ref[...])
    h1 = _layer_norm(DEEPNORM_ALPHA * x_ref[...] + mix, g1_ref[...], b1_ref[...])
    h1_ref[...] = h1

    logits = lax.dot_general(rwt_ref[...], h1, (((1,), (1,)), ((), ())), preferred_element_type=F32,
                             precision=lax.Precision.HIGHEST) + rb_ref[...]
    tm = logits.shape[1]
    e_iota = lax.broadcasted_iota(I32, (N_EXPERTS, tm), 0)
    tops, hots = [], []
    for _ in range(MOE_TOP_K):
        m = jnp.max(logits, axis=0, keepdims=True)
        first = jnp.min(jnp.where(logits == m, e_iota, N_EXPERTS), axis=0, keepdims=True)
        hot = e_iota == first
        tops.append(m)
        hots.append(hot)
        logits = jnp.where(hot, -jnp.inf, logits)
    exps = [jnp.exp(m - tops[0]) for m in tops]
    den = exps[0] + exps[1] + exps[2] + exps[3]
    gates = jnp.zeros((N_EXPERTS, tm), F32)
    sel = jnp.zeros((N_EXPERTS, tm), F32)
    for hot, e in zip(hots, exps):
        gates = gates + jnp.where(hot, e / den, 0.0)
        sel = sel + jnp.where(hot, 1.0, 0.0)
    gt_ref[...] = gates
    rank = _mm(sel.astype(BF16), tri_ref[...]) + cnt_ref[...]
    slot_ref[...] = jnp.where(sel > 0.0, rank, -1.0)
    cnt_ref[...] = cnt_ref[...] + jnp.sum(sel, axis=1, keepdims=True)


def _merge(x, ret, od, gr, ga, wr, wd, wo, g1, b1, rwt, rb):
    n = x.shape[0]
    tm = MERGE_TM
    row = lambda w: pl.BlockSpec((tm, w), lambda i: (i, 0))
    const = lambda a: pl.BlockSpec(a.shape, lambda i: (0,) * a.ndim)
    col = pl.BlockSpec((N_EXPERTS, tm), lambda i: (0, i))
    tri = jnp.asarray(np.triu(np.ones((tm, tm), np.float32), 1), BF16)
    return pl.pallas_call(
        functools.partial(_merge_kernel, sub_tiles=MOE_T // tm),
        grid=(n // tm,),
        in_specs=[row(D_MODEL), row(1024), row(512), row(1024), row(1024), const(wr), const(wd), const(wo),
                  const(g1), const(b1), const(rwt), const(rb), const(tri)],
        out_specs=[row(D_MODEL), col, col],
        out_shape=[jax.ShapeDtypeStruct((n, D_MODEL), F32), jax.ShapeDtypeStruct((N_EXPERTS, n), F32),
                   jax.ShapeDtypeStruct((N_EXPERTS, n), F32)],
        scratch_shapes=[pltpu.VMEM((N_EXPERTS, 1), F32)],
        compiler_params=pltpu.CompilerParams(dimension_semantics=("arbitrary",), vmem_limit_bytes=48 * MIB),
        name="merge",
    )(x, ret, od, gr, ga, wr, wd, wo, g1, b1, rwt, rb, tri)


def _deinterleave_kernel(w_ref, o_ref):
    r = lax.broadcasted_iota(I32, (UP_BLOCK, UP_BLOCK), 0)
    c = lax.broadcasted_iota(I32, (UP_BLOCK, UP_BLOCK), 1)
    src = jnp.where(c < LANES, 2 * c, 2 * (c - LANES) + 1)
    perm = jnp.where(r == src, 1.0, 0.0).astype(BF16)
    for b in range(w_ref.shape[2] // UP_BLOCK):
        sl = slice(b * UP_BLOCK, (b + 1) * UP_BLOCK)
        o_ref[0, :, sl] = _mm(w_ref[0, :, sl].astype(BF16), perm).astype(BF16)


def _deinterleave_w_up(w_up):
    n_e, d_in, d_out = w_up.shape
    cols = 512
    spec = pl.BlockSpec((1, d_in, cols), lambda e, c: (e, 0, c))
    return pl.pallas_call(
        _deinterleave_kernel,
        grid=(n_e, d_out // cols),
        in_specs=[spec],
        out_specs=spec,
        out_shape=jax.ShapeDtypeStruct(w_up.shape, BF16),
        compiler_params=pltpu.CompilerParams(dimension_semantics=("parallel", "parallel"),
                                             vmem_limit_bytes=24 * MIB),
        name="w_up_prep",
    )(w_up)


def _moe_kernel(h1_ref, gt_ref, slot_ref, wup_ref, bup_ref, wdn_ref, bdn_ref, g2_ref, b2_ref, o_ref, xb_ref, y_ref):
    e = pl.program_id(1)
    t = h1_ref.shape[0]

    @pl.when(e == 0)
    def _():
        xb_ref[...] = h1_ref[...].astype(BF16)
        y_ref[...] = jnp.zeros_like(y_ref)

    slot_e = slot_ref[pl.ds(e, 1), :]
    gate_e = gt_ref[pl.ds(e, 1), :]
    count = jnp.sum(jnp.where(slot_e >= 0.0, 1.0, 0.0)).astype(I32)
    jrow = lax.broadcasted_iota(I32, (MOE_CAP_PAD, t), 0).astype(F32)

    def sub_block(sb, carry):
        rel = slot_e - (sb * MOE_CAP).astype(F32)
        rel = jnp.where(rel < float(MOE_CAP), rel, -1.0)
        pick_pad = jnp.where(rel == jrow, 1.0, 0.0)
        pick = pick_pad[:MOE_CAP]
        xc = _mm(pick.astype(BF16), xb_ref[...]).astype(BF16)
        h = _mm(xc, wup_ref[0]) + bup_ref[0]
        acts = []
        for b in range(2 * D_FF // UP_BLOCK):
            glu = jnp.minimum(h[:, b * UP_BLOCK:b * UP_BLOCK + LANES], SWIGLU_LIMIT)
            lin = jnp.clip(h[:, b * UP_BLOCK + LANES:(b + 1) * UP_BLOCK], -SWIGLU_LIMIT, SWIGLU_LIMIT)
            acts.append(glu * jax.nn.sigmoid(SWIGLU_ALPHA * glu) * (lin + 1.0))
        act = jnp.concatenate(acts, axis=1)
        out = _mm(act.astype(BF16), wdn_ref[0]) + bdn_ref[0]
        gate_c = jnp.sum(pick * gate_e, axis=1, keepdims=True)
        out_pad = jnp.concatenate([(out * gate_c).astype(BF16),
                                   jnp.zeros((MOE_CAP_PAD - MOE_CAP, D_MODEL), BF16)], axis=0)
        y_ref[...] += _mm(pick_pad.T.astype(BF16), out_pad)
        return carry

    lax.fori_loop(0, (count + MOE_CAP - 1) // MOE_CAP, sub_block, 0)

    @pl.when(e == pl.num_programs(1) - 1)
    def _():
        o_ref[...] = _layer_norm(DEEPNORM_ALPHA * h1_ref[...] + y_ref[...], g2_ref[...], b2_ref[...])


def _moe(h1, gt, slot, wup, bup, wdn, bdn, g2, b2):
    n = h1.shape[0]
    t = MOE_T
    const = lambda a: pl.BlockSpec(a.shape, lambda i, e: (0,) * a.ndim)
    return pl.pallas_call(
        _moe_kernel,
        grid=(n // t, N_EXPERTS),
        in_specs=[pl.BlockSpec((t, D_MODEL), lambda i, e: (i, 0)),
                  pl.BlockSpec((N_EXPERTS, t), lambda i, e: (0, i)),
                  pl.BlockSpec((N_EXPERTS, t), lambda i, e: (0, i)),
                  pl.BlockSpec((1, D_MODEL, 2 * D_FF), lambda i, e: (e, 0, 0)),
                  pl.BlockSpec((1, 1, 2 * D_FF), lambda i, e: (e, 0, 0)),
                  pl.BlockSpec((1, D_FF, D_MODEL), lambda i, e: (e, 0, 0)),
                  pl.BlockSpec((1, 1, D_MODEL), lambda i, e: (e, 0, 0)),
                  const(g2), const(b2)],
        out_specs=pl.BlockSpec((t, D_MODEL), lambda i, e: (i, 0)),
        out_shape=jax.ShapeDtypeStruct((n, D_MODEL), F32),
        scratch_shapes=[pltpu.VMEM((t, D_MODEL), BF16), pltpu.VMEM((t, D_MODEL), F32)],
        compiler_params=pltpu.CompilerParams(dimension_semantics=("parallel", "arbitrary"),
                                             vmem_limit_bytes=52 * MIB),
        name="moe",
    )(h1, gt, slot, wup, bup, wdn, bdn, g2, b2)


def _layer(x, pos_tab, tab_period, s0, rows_per_stream, ret_rows, dsa_keys, weights):
    wp, wr, wd, wo, g1, b1, rwt, rb, wup, bup, wdn, bdn, g2, b2 = weights
    n = x.shape[0]
    n_streams = n // rows_per_stream
    rq, rk, rv, rg, aq, ak, av, iq, ikw, gr, ga = _project(x, wp, pos_tab, tab_period)
    ret, s_new = _retention(rq, rk, rv, rg, s0, rows_per_stream, ret_rows)
    if dsa_keys is None:
        per_stream = lambda a: a.reshape(n_streams, rows_per_stream, a.shape[1])
        od = _dsa(aq, iq, ikw, per_stream(ak), per_stream(av), per_stream(ikw), DSA_TQ, True, rows_per_stream)
    else:
        kk, vv, ikk, limit = dsa_keys(ak, av, ikw)
        od = _dsa(aq, iq, ikw, kk, vv, ikk, rows_per_stream, False, limit)
    h1, gt, slot = _merge(x, ret, od, gr, ga, wr, wd, wo, g1, b1, rwt, rb)
    y = _moe(h1, gt, slot, wup, bup, wdn, bdn, g2, b2)
    return y, s_new, ak, av, ikw[:, :IDX_DIM]


def kernel(x_prompt, x_sample, state_ret, cache_k, cache_v, cache_idx_k, w_in, w_ret_o, w_dsa_o, w_o,
           ln1_g, ln1_b, router_w, router_b, w_up, b_up, w_down, b_down, ln2_g, ln2_b):
    assert w_in.shape[0] == DEPTH
    batch, seq, _ = x_prompt.shape
    dec_batch, dec_seq, _ = x_sample.shape
    past = cache_k.shape[2]
    assert seq % PROJ_TM == 0 and seq % DSA_TQ == 0 and PROJ_TM % dec_seq == 0

    l = 0
    weights = (
        _pack_w_in(w_in[l]), w_ret_o[l].astype(BF16), w_dsa_o[l].astype(BF16), w_o[l].astype(BF16),
        ln1_g[l][None, :], ln1_b[l][None, :], router_w[l].T, router_b[l][:, None],
        _deinterleave_w_up(w_up[l]),
        b_up[l].reshape(N_EXPERTS, 2 * D_FF // UP_BLOCK, LANES, 2).transpose(0, 1, 3, 2).reshape(N_EXPERTS, 1, 2 * D_FF),
        w_down[l].astype(BF16), b_down[l][:, None, :], ln2_g[l][None, :], ln2_b[l][None, :])

    tab_p = _rot_tables(jnp.arange(seq))
    zeros_state = jnp.zeros((batch, RET_HEADS, RET_DK, RET_DV), F32)
    y_p, s_p, k_p, v_p, ik_p = _layer(x_prompt.reshape(batch * seq, D_MODEL), tab_p, seq // PROJ_TM, zeros_state,
                                      seq, RET_CHUNK, None, weights)

    n_keys_real = past + dec_seq
    n_keys = -(-n_keys_real // LANES) * LANES
    tab_s = jnp.tile(_rot_tables(past + jnp.arange(dec_seq)), (PROJ_TM // dec_seq, 1))

    def sample_keys(ak, av, ikw):
        def cat(cache, new, width):
            padz = jnp.zeros((dec_batch, n_keys - n_keys_real, width), F32)
            return jnp.concatenate([cache.reshape(dec_batch, past, width), new.reshape(dec_batch, dec_seq, width), padz],
                                   axis=1)
        return (cat(cache_k[l], ak, LANES), cat(cache_v[l], av, LANES),
                cat(cache_idx_k[l], ikw[:, :IDX_DIM], IDX_DIM), n_keys_real)

    y_s, s_s, k_s, v_s, ik_s = _layer(x_sample.reshape(dec_batch * dec_seq, D_MODEL), tab_s, 1, state_ret[l],
                                      dec_seq, dec_seq, sample_keys, weights)

    kv = (DSA_KV_HEADS, DSA_HEAD_DIM)
    return (y_p.reshape(batch, seq, D_MODEL), y_s.reshape(dec_batch, dec_seq, D_MODEL),
            s_p[None], k_p.reshape(1, batch, seq, *kv), v_p.reshape(1, batch, seq, *kv),
            ik_p.reshape(1, batch, seq, IDX_DIM),
            s_s[None], k_s.reshape(1, dec_batch, dec_seq, *kv), v_s.reshape(1, dec_batch, dec_seq, *kv),
            ik_s.reshape(1, dec_batch, dec_seq, IDX_DIM))
```

```python
import functools

import numpy as np
import jax
import jax.numpy as jnp
from jax import lax
from jax.experimental import pallas as pl
from jax.experimental.pallas import tpu as pltpu
from jax.experimental.pallas import tpu_sc as plsc

F32 = jnp.float32
BF16 = jnp.bfloat16
I32 = jnp.int32

D_MODEL = 1024
CHUNK = 64
RET_HEADS = 4
RET_DK = 128
RET_DV = 256
RET_ROPE_BASE = 10000.0
DSA_HEADS = 8
DSA_KV_HEADS = 2
DSA_HEAD_DIM = 64
IDX_HEADS = 8
IDX_DIM = 64
DSA_TOPK = 256
ROPE_THETA = 500000.0
N_EXPERTS = 32
MOE_TOP_K = 4
D_FF = 1024
SWIGLU_ALPHA = 1.702
SWIGLU_LIMIT = 7.0
LN_EPS = 1e-5
GN_EPS = 1e-6
DEPTH = 1
DEEPNORM_ALPHA = (2.0 * DEPTH) ** 0.25
PROJ_WIDTHS = (RET_HEADS * RET_DK, RET_HEADS * RET_DK, RET_HEADS * RET_DV, RET_HEADS * RET_DV,
               DSA_HEADS * DSA_HEAD_DIM, DSA_KV_HEADS * DSA_HEAD_DIM, DSA_KV_HEADS * DSA_HEAD_DIM,
               IDX_HEADS * IDX_DIM, IDX_DIM, IDX_HEADS, D_MODEL, D_MODEL)

LANES = 128
MIB = 1024 * 1024

OFF_RQ, OFF_RK, OFF_RV, OFF_RG = 0, 512, 1024, 2048
OFF_AQ, OFF_AK, OFF_AV, OFF_IQ, OFF_IKW = 3072, 3584, 3712, 3840, 4352
OFF_GR, OFF_GA, PACKED_COLS = 4480, 5504, 6528
TAB_COLS = 8 * LANES

PROJ_TM = 256
RET_CHUNK = 128
DSA_TQ = 256
MERGE_TM = 512
FFN_BLOCK = 256
SC_WINDOW = 128
QUARTER = D_MODEL // 4
UP_BLOCK = 2 * LANES


def _nt(a, b):
    return lax.dot_general(a, b, (((1,), (1,)), ((), ())), preferred_element_type=F32)


def _mm(a, b):
    return jnp.dot(a, b, preferred_element_type=F32)


def _proj_kernel(x_ref, w_ref, tab_ref, rq_ref, rk_ref, rv_ref, rg_ref, aq_ref, ak_ref, av_ref,
                 iq_ref, ikw_ref, gr_ref, ga_ref):
    xb = x_ref[...].astype(BF16)

    def mm(c0, n):
        return _mm(xb, w_ref[:, c0:c0 + n])

    def tab(i):
        return tab_ref[:, i * LANES:(i + 1) * LANES]

    def rot_full(z):
        return z * tab(0) + pltpu.roll(z, 64, 1) * tab(1)

    def rot_part(z, c):
        return z * tab(c) + pltpu.roll(z, LANES - 8, 1) * tab(c + 1) + pltpu.roll(z, 8, 1) * tab(c + 2)

    z = mm(OFF_RQ, 512)
    for h in range(4):
        sl = slice(h * LANES, (h + 1) * LANES)
        rq_ref[:, sl] = rot_full(z[:, sl]).astype(BF16)
    z = mm(OFF_RK, 512)
    for h in range(4):
        sl = slice(h * LANES, (h + 1) * LANES)
        rk_ref[:, sl] = (rot_full(z[:, sl]) * (RET_DK ** -0.5)).astype(BF16)
    for c in range(2):
        rv_ref[:, c * 512:(c + 1) * 512] = mm(OFF_RV + c * 512, 512).astype(BF16)
    for c in range(2):
        rg_ref[:, c * 512:(c + 1) * 512] = mm(OFF_RG + c * 512, 512)
    z = mm(OFF_AQ, 512)
    for h in range(4):
        sl = slice(h * LANES, (h + 1) * LANES)
        aq_ref[:, sl] = rot_part(z[:, sl], 2).astype(BF16)
    z = mm(OFF_AK, 256)
    ak_ref[...] = rot_part(z[:, :LANES], 2)
    av_ref[...] = z[:, LANES:]
    z = mm(OFF_IQ, 512)
    for h in range(4):
        sl = slice(h * LANES, (h + 1) * LANES)
        iq_ref[:, sl] = rot_part(z[:, sl], 2).astype(BF16)
    ikw_ref[...] = rot_part(mm(OFF_IKW, LANES), 5)
    for c in range(2):
        gr_ref[:, c * 512:(c + 1) * 512] = mm(OFF_GR + c * 512, 512)
    for c in range(2):
        ga_ref[:, c * 512:(c + 1) * 512] = mm(OFF_GA + c * 512, 512)


def _rot_tables(pos):
    p = pos.shape[0]
    posf = pos.astype(F32)[:, None]
    ret_f = RET_ROPE_BASE ** (-jnp.linspace(0.0, 1.0, RET_DK // 2, dtype=F32))
    ang = posf * ret_f[None, :]
    c, s = jnp.cos(ang), jnp.sin(ang)
    cos_r = jnp.concatenate([c, c], 1)
    sin_r = jnp.concatenate([-s, s], 1)
    n_rot = DSA_HEAD_DIM // 4
    att_f = ROPE_THETA ** (-jnp.arange(0, n_rot, 2, dtype=F32) / n_rot)
    ang2 = posf * att_f[None, :]
    c2, s2 = jnp.cos(ang2), jnp.sin(ang2)
    half = n_rot // 2
    rest = DSA_HEAD_DIM - 2 * half
    c64 = jnp.concatenate([c2, c2, jnp.ones((p, rest), F32)], 1)
    s1_64 = jnp.concatenate([-s2, jnp.zeros((p, DSA_HEAD_DIM - half), F32)], 1)
    s2_64 = jnp.concatenate([jnp.zeros((p, half), F32), s2, jnp.zeros((p, rest), F32)], 1)
    z64 = jnp.zeros((p, DSA_HEAD_DIM), F32)
    ci = jnp.concatenate([c64, jnp.full((p, IDX_HEADS), IDX_HEADS ** -0.5, F32),
                          jnp.zeros((p, DSA_HEAD_DIM - IDX_HEADS), F32)], 1)
    return jnp.concatenate([cos_r, sin_r,
                            jnp.concatenate([c64, c64], 1), jnp.concatenate([s1_64, s1_64], 1),
                            jnp.concatenate([s2_64, s2_64], 1),
                            ci, jnp.concatenate([s1_64, z64], 1), jnp.concatenate([s2_64, z64], 1)], 1)


def _pack_w_in(w):
    cuts = np.cumsum(PROJ_WIDTHS)[:-1].tolist()
    rq, rk, rv, rg, aq, ak, av, iq, ik, iw, gr, ga = jnp.split(w, cuts, axis=1)
    pad = jnp.zeros((w.shape[0], LANES - IDX_DIM - IDX_HEADS), w.dtype)
    return jnp.concatenate([rq, rk, rv, rg, aq, ak, av, iq, ik, iw, pad, gr, ga], axis=1).astype(BF16)


def _project(x, wp, tab, tab_period):
    n = x.shape[0]
    tm = PROJ_TM
    row = lambda w: pl.BlockSpec((tm, w), lambda i: (i, 0))
    out_shapes = [((n, 512), BF16), ((n, 512), BF16), ((n, 1024), BF16), ((n, 1024), F32),
                  ((n, 512), BF16), ((n, LANES), F32), ((n, LANES), F32), ((n, 512), BF16),
                  ((n, LANES), F32), ((n, 1024), F32), ((n, 1024), F32)]
    return pl.pallas_call(
        _proj_kernel,
        grid=(n // tm,),
        in_specs=[row(D_MODEL),
                  pl.BlockSpec((D_MODEL, PACKED_COLS), lambda i: (0, 0), pipeline_mode=pl.Buffered(1)),
                  pl.BlockSpec((tm, TAB_COLS), lambda i: (i % tab_period, 0))],
        out_specs=[row(s[1]) for s, _ in out_shapes],
        out_shape=[jax.ShapeDtypeStruct(s, d) for s, d in out_shapes],
        compiler_params=pltpu.CompilerParams(dimension_semantics=("parallel",), vmem_limit_bytes=44 * MIB),
        name="proj",
    )(x, wp, tab)


def _ret_kernel(dec_ref, xi_ref, zeta_ref, rq_ref, rk_ref, rv_ref, rg_ref, s0_ref, ret_ref, sout_ref, st_ref,
                *, rows, n_chunk, g_pow):
    j = pl.program_id(1)
    cpad = RET_CHUNK

    @pl.when(j == 0)
    def _():
        st_ref[...] = s0_ref[0]

    def padded(v):
        if rows == cpad:
            return v
        return jnp.concatenate([v, jnp.zeros((cpad - rows, v.shape[1]), v.dtype)], axis=0)

    for c in range(n_chunk):
        rs = slice(c * rows, (c + 1) * rows)
        for h in range(RET_HEADS):
            ks = slice(h * RET_DK, (h + 1) * RET_DK)
            vs = slice(h * RET_DV, (h + 1) * RET_DV)
            q = padded(rq_ref[rs, ks])
            kt = padded(rk_ref[rs, ks].astype(F32)).T
            v = padded(rv_ref[rs, vs])
            s = st_ref[h]
            sc = _mm(q, kt.astype(BF16)) * dec_ref[h]
            o = _mm(sc.astype(BF16), v) + _mm(q, s.astype(BF16)) * xi_ref[h]
            st_ref[h] = g_pow[h] * s + _mm((kt * zeta_ref[h]).astype(BF16), v)
            o = o[:rows]
            mu = jnp.mean(o, axis=-1, keepdims=True)
            d = o - mu
            var = jnp.mean(d * d, axis=-1, keepdims=True)
            gn = d * lax.rsqrt(var + GN_EPS)
            g = rg_ref[rs, vs]
            ret_ref[rs, vs] = (gn * (g * jax.nn.sigmoid(g))).astype(BF16)

    @pl.when(j == pl.num_programs(1) - 1)
    def _():
        sout_ref[0] = st_ref[...]


def _retention(rq, rk, rv, rg, s0, rows_per_stream, rows):
    n = rq.shape[0]
    n_streams = n // rows_per_stream
    n_chunk = min(rows_per_stream // rows, 2)
    blk = rows * n_chunk
    nb = rows_per_stream // blk
    gam = 1.0 - 2.0 ** (-5.0 - np.arange(RET_HEADS, dtype=np.float64))
    i = np.arange(RET_CHUNK, dtype=np.float64)
    diff = i[:, None] - i[None, :]
    dec = np.where(diff >= 0, gam[:, None, None] ** np.maximum(diff, 0.0)[None], 0.0)
    xi = gam[:, None, None] ** (i + 1.0)[None, :, None]
    zeta = np.where(i < rows, gam[:, None, None] ** (rows - 1.0 - i)[None, None, :], 0.0)
    g_pow = tuple(float(g ** rows) for g in gam)
    const = lambda shape: pl.BlockSpec(shape, lambda s, j: (0,) * len(shape))
    row = lambda w: pl.BlockSpec((blk, w), lambda s, j: (s * nb + j, 0))
    st = pl.BlockSpec((1, RET_HEADS, RET_DK, RET_DV), lambda s, j: (s, 0, 0, 0))
    return pl.pallas_call(
        functools.partial(_ret_kernel, rows=rows, n_chunk=n_chunk, g_pow=g_pow),
        grid=(n_streams, nb),
        in_specs=[const((RET_HEADS, RET_CHUNK, RET_CHUNK)), const((RET_HEADS, RET_CHUNK, 1)),
                  const((RET_HEADS, 1, RET_CHUNK)), row(512), row(512), row(1024), row(1024), st],
        out_specs=[row(1024), st],
        out_shape=[jax.ShapeDtypeStruct((n, RET_HEADS * RET_DV), BF16),
                   jax.ShapeDtypeStruct((n_streams, RET_HEADS, RET_DK, RET_DV), F32)],
        scratch_shapes=[pltpu.VMEM((RET_HEADS, RET_DK, RET_DV), F32)],
        compiler_params=pltpu.CompilerParams(dimension_semantics=("parallel", "arbitrary"),
                                             vmem_limit_bytes=32 * MIB),
        name="retention",
    )(jnp.asarray(dec, F32), jnp.asarray(xi, F32), jnp.asarray(zeta, F32), rq, rk, rv, rg, s0)


def _dsa_kernel(aq_ref, iq_ref, ikwq_ref, kk_ref, vv_ref, ikk_ref, o_ref, w_ref, bias_ref,
                *, tq, n_keys, row0, limit_const, n_sel):
    nsel_f = float(n_sel)
    neg_inf = -jnp.inf

    ikb = ikk_ref[0][:, :IDX_DIM].astype(BF16)
    iww = ikwq_ref[...][:, IDX_DIM:IDX_DIM + IDX_HEADS] * (IDX_DIM ** -0.5)
    acc = jnp.zeros((tq, n_keys), F32)
    for p in range(IDX_HEADS // 2):
        slab = iq_ref[:, p * LANES:(p + 1) * LANES]
        for hh in range(2):
            h = 2 * p + hh
            s = _nt(slab[:, hh * IDX_DIM:(hh + 1) * IDX_DIM], ikb)
            acc = acc + jnp.maximum(s, 0.0) * iww[:, h:h + 1]

    col = lax.broadcasted_iota(I32, (tq, n_keys), 1)
    if limit_const is None:
        row = lax.broadcasted_iota(I32, (tq, 1), 0) + row0
        limit = (jnp.right_shift(row, 6) + 1) * CHUNK
    else:
        limit = limit_const
    admissible = col < limit
    sc = jnp.where(admissible, acc, neg_inf)

    pos = jnp.sum(jnp.where(sc >= 0.0, 1.0, 0.0), axis=1, keepdims=True) >= nsel_f
    kk = jnp.where(pos, nsel_f, float(n_keys - n_sel + 1))
    w_ref[...] = jnp.where(pos, sc, -sc)

    def bit_step(i, u):
        cand_u = u | jnp.left_shift(jnp.int32(1), 30 - i)
        cand = pltpu.bitcast(cand_u, F32)
        cnt = jnp.sum(jnp.where(w_ref[...] >= cand, 1.0, 0.0), axis=1, keepdims=True)
        return jnp.where(cnt >= kk, cand_u, u)

    mag_u = lax.fori_loop(0, 31, bit_step, jnp.zeros((tq, 1), I32))
    mag = pltpu.bitcast(mag_u, F32)
    thr = jnp.where(pos, mag, -mag)

    sc = jnp.where(pos, w_ref[...], -w_ref[...])
    short = jnp.sum(jnp.where(sc >= thr, 1.0, 0.0), axis=1, keepdims=True) < nsel_f
    thr = jnp.where(jnp.logical_and(short, jnp.logical_not(pos)), -pltpu.bitcast(mag_u + 1, F32), thr)
    ge = sc >= thr
    cnt_gt = jnp.sum(jnp.where(sc > thr, 1.0, 0.0), axis=1, keepdims=True)
    cnt_ge = jnp.sum(jnp.where(ge, 1.0, 0.0), axis=1, keepdims=True)
    bias_ref[...] = jnp.where(jnp.logical_and(ge, admissible), 0.0, neg_inf)
    excess = jnp.logical_and(cnt_ge > nsel_f, thr > neg_inf)

    @pl.when(jnp.max(jnp.where(excess, 1.0, 0.0)) > 0.0)
    def _():
        need = nsel_f - cnt_gt
        tri = jnp.where(lax.broadcasted_iota(I32, (LANES, LANES), 0) < lax.broadcasted_iota(I32, (LANES, LANES), 1),
                        1.0, 0.0).astype(BF16)
        before = jnp.zeros((tq, 1), F32)
        for b in range(n_keys // LANES):
            sl = slice(b * LANES, (b + 1) * LANES)
            sblk = jnp.where(pos, w_ref[:, sl], -w_ref[:, sl])
            eq = jnp.where(sblk == thr, 1.0, 0.0)
            rank = _mm(eq.astype(BF16), tri) + before
            keep = jnp.logical_or(sblk > thr, jnp.logical_and(sblk == thr, rank < need))
            bias_ref[:, sl] = jnp.where(jnp.logical_and(keep, sblk > neg_inf), 0.0, neg_inf)
            before = before + jnp.sum(eq, axis=1, keepdims=True)

    kfull = kk_ref[0]
    vfull = vv_ref[0]
    for g in range(DSA_KV_HEADS):
        gs = slice(g * DSA_HEAD_DIM, (g + 1) * DSA_HEAD_DIM)
        kg = kfull[:, gs].astype(BF16)
        vg = vfull[:, gs].astype(BF16)
        for pp in range(2):
            p = 2 * g + pp
            slab = (aq_ref[:, p * LANES:(p + 1) * LANES].astype(F32) * (DSA_HEAD_DIM ** -0.5)).astype(BF16)
            outs = []
            for hh in range(2):
                lg = _nt(slab[:, hh * DSA_HEAD_DIM:(hh + 1) * DSA_HEAD_DIM], kg) + bias_ref[...]
                m = jnp.max(lg, axis=1, keepdims=True)
                pr = jnp.exp(lg - m)
                den = jnp.sum(pr, axis=1, keepdims=True)
                outs.append(_mm(pr.astype(BF16), vg) / den)
            o_ref[:, p * LANES:(p + 1) * LANES] = jnp.concatenate(outs, axis=1).astype(BF16)


def _dsa_call(aq, iq, ikw, kk, vv, ikk, tq, nq, jq, n_keys, limit_const, n_sel, name):
    n_streams = kk.shape[0]
    qrow = lambda w: pl.BlockSpec((tq, w), lambda s: (s * nq + jq, 0))
    krow = lambda a: pl.BlockSpec((1, n_keys, a.shape[2]), lambda s: (s, 0, 0))
    return pl.pallas_call(
        functools.partial(_dsa_kernel, tq=tq, n_keys=n_keys, row0=jq * tq, limit_const=limit_const, n_sel=n_sel),
        grid=(n_streams,),
        in_specs=[qrow(512), qrow(512), qrow(LANES), krow(kk), krow(vv), krow(ikk)],
        out_specs=pl.BlockSpec((tq, DSA_HEADS * DSA_HEAD_DIM), lambda s: (s, 0)),
        out_shape=jax.ShapeDtypeStruct((n_streams * tq, DSA_HEADS * DSA_HEAD_DIM), BF16),
        scratch_shapes=[pltpu.VMEM((tq, n_keys), F32), pltpu.VMEM((tq, n_keys), F32)],
        compiler_params=pltpu.CompilerParams(dimension_semantics=("parallel",), vmem_limit_bytes=48 * MIB),
        name=name,
    )(aq, iq, ikw, kk, vv, ikk)


def _dsa(aq, iq, ikw, kk, vv, ikk, tq, chunk_causal, limit_const):
    n_streams, n_keys, _ = kk.shape
    n = aq.shape[0]
    nq = n // n_streams // tq
    n_sel = min(DSA_TOPK, limit_const // 4)
    if not chunk_causal:
        assert nq == 1
        return _dsa_call(aq, iq, ikw, kk, vv, ikk, tq, 1, 0, n_keys, limit_const, n_sel, "dsa_s")
    parts = [_dsa_call(aq, iq, ikw, kk, vv, ikk, tq, nq, jq, (jq + 1) * tq, None, n_sel, f"dsa_p{jq}")
             for jq in range(nq)]
    width = parts[0].shape[1]
    return jnp.stack(parts, 0).reshape(nq, n_streams, tq, width).transpose(1, 0, 2, 3).reshape(n, width)


def _pack_rows(v):
    q = QUARTER
    bits = lambda x: pltpu.bitcast(x.astype(BF16).astype(F32), I32)
    pair = lambda c: lax.shift_right_logical(bits(v[:, c * q:(c + 1) * q]), 16) | bits(v[:, (c + 1) * q:(c + 2) * q])
    return pair(0), pair(2)


def _unpack_rows(lo, hi):
    parts = []
    for w in (lo, hi):
        parts.append(pltpu.bitcast(lax.shift_left(w, 16), F32))
        parts.append(pltpu.bitcast(w & jnp.int32(-65536), F32))
    return jnp.concatenate(parts, axis=1)


def _layer_norm(v, g, b):
    mu = jnp.mean(v, axis=-1, keepdims=True)
    d = v - mu
    var = jnp.mean(d * d, axis=-1, keepdims=True)
    return d * lax.rsqrt(var + LN_EPS) * g + b


def _merge_kernel(x_ref, ret_ref, od_ref, gr_ref, ga_ref, wr_ref, wd_ref, wo_ref, g1_ref, b1_ref, rwt_ref, rb_ref,
                  tri_ref, h1_ref, hlo_ref, hhi_ref, gtm_ref, ek_ref, rk_ref, tot_ref, cnt_ref):
    @pl.when(pl.program_id(0) == 0)
    def _():
        cnt_ref[...] = jnp.zeros_like(cnt_ref)

    y_ret = _mm(ret_ref[...], wr_ref[...])
    y_dsa = _mm(od_ref[...], wd_ref[...])
    merged = jax.nn.sigmoid(gr_ref[...]) * y_ret + jax.nn.sigmoid(ga_ref[...]) * y_dsa
    mix = _mm(merged.astype(BF16), wo_ref[...])
    h1 = _layer_norm(DEEPNORM_ALPHA * x_ref[...] + mix, g1_ref[...], b1_ref[...])
    h1_ref[...] = h1
    hlo_ref[...], hhi_ref[...] = _pack_rows(h1)

    logits = lax.dot_general(rwt_ref[...], h1, (((1,), (1,)), ((), ())), preferred_element_type=F32,
                             precision=lax.Precision.HIGHEST) + rb_ref[...]
    tm = logits.shape[1]
    e_iota = lax.broadcasted_iota(I32, (N_EXPERTS, tm), 0)
    tops, hots, firsts = [], [], []
    for _ in range(MOE_TOP_K):
        m = jnp.max(logits, axis=0, keepdims=True)
        first = jnp.min(jnp.where(logits == m, e_iota, N_EXPERTS), axis=0, keepdims=True)
        hot = e_iota == first
        tops.append(m)
        hots.append(hot)
        firsts.append(first)
        logits = jnp.where(hot, -jnp.inf, logits)
    exps = [jnp.exp(m - tops[0]) for m in tops]
    den = exps[0] + exps[1] + exps[2] + exps[3]
    sel = jnp.zeros((N_EXPERTS, tm), F32)
    for hot in hots:
        sel = sel + jnp.where(hot, 1.0, 0.0)
    rank = _mm(sel.astype(BF16), tri_ref[...]) + cnt_ref[...]
    ranks = [jnp.sum(jnp.where(hot, rank, 0.0), axis=0, keepdims=True).astype(I32) for hot in hots]
    pad_i = jnp.zeros((8 - MOE_TOP_K, tm), I32)
    ek_ref[...] = jnp.concatenate(firsts + [pad_i], axis=0)
    rk_ref[...] = jnp.concatenate(ranks + [pad_i], axis=0)
    gates = jnp.concatenate([e / den for e in exps] + [jnp.zeros((LANES - MOE_TOP_K, tm), F32)], axis=0)
    gtm_ref[...] = gates.T
    cnt_ref[...] = cnt_ref[...] + jnp.sum(sel, axis=1, keepdims=True)
    tot_ref[...] = jnp.broadcast_to(cnt_ref[...], tot_ref.shape)


def _merge(x, ret, od, gr, ga, wr, wd, wo, g1, b1, rwt, rb):
    n = x.shape[0]
    tm = MERGE_TM
    row = lambda w: pl.BlockSpec((tm, w), lambda i: (i, 0))
    const = lambda a: pl.BlockSpec(a.shape, lambda i: (0,) * a.ndim)
    col = pl.BlockSpec((8, tm), lambda i: (0, i))
    tri = jnp.asarray(np.triu(np.ones((tm, tm), np.float32), 1), BF16)
    return pl.pallas_call(
        _merge_kernel,
        grid=(n // tm,),
        in_specs=[row(D_MODEL), row(1024), row(512), row(1024), row(1024), const(wr), const(wd), const(wo),
                  const(g1), const(b1), const(rwt), const(rb), const(tri)],
        out_specs=[row(D_MODEL), row(QUARTER), row(QUARTER), row(LANES), col, col,
                   pl.BlockSpec((N_EXPERTS, LANES), lambda i: (0, 0))],
        out_shape=[jax.ShapeDtypeStruct((n, D_MODEL), F32), jax.ShapeDtypeStruct((n, QUARTER), I32),
                   jax.ShapeDtypeStruct((n, QUARTER), I32), jax.ShapeDtypeStruct((n, LANES), F32),
                   jax.ShapeDtypeStruct((8, n), I32), jax.ShapeDtypeStruct((8, n), I32),
                   jax.ShapeDtypeStruct((N_EXPERTS, LANES), F32)],
        scratch_shapes=[pltpu.VMEM((N_EXPERTS, 1), F32)],
        compiler_params=pltpu.CompilerParams(dimension_semantics=("arbitrary",), vmem_limit_bytes=48 * MIB),
        name="merge",
    )(x, ret, od, gr, ga, wr, wd, wo, g1, b1, rwt, rb, tri)


def _deinterleave_kernel(w_ref, o_ref):
    r = lax.broadcasted_iota(I32, (UP_BLOCK, UP_BLOCK), 0)
    c = lax.broadcasted_iota(I32, (UP_BLOCK, UP_BLOCK), 1)
    src = jnp.where(c < LANES, 2 * c, 2 * (c - LANES) + 1)
    perm = jnp.where(r == src, 1.0, 0.0).astype(BF16)
    for b in range(w_ref.shape[2] // UP_BLOCK):
        sl = slice(b * UP_BLOCK, (b + 1) * UP_BLOCK)
        o_ref[0, :, sl] = _mm(w_ref[0, :, sl].astype(BF16), perm).astype(BF16)


def _deinterleave_w_up(w_up):
    n_e, d_in, d_out = w_up.shape
    cols = 512
    spec = pl.BlockSpec((1, d_in, cols), lambda e, c: (e, 0, c))
    return pl.pallas_call(
        _deinterleave_kernel,
        grid=(n_e, d_out // cols),
        in_specs=[spec],
        out_specs=spec,
        out_shape=jax.ShapeDtypeStruct(w_up.shape, BF16),
        compiler_params=pltpu.CompilerParams(dimension_semantics=("parallel", "parallel"),
                                             vmem_limit_bytes=24 * MIB),
        name="w_up_prep",
    )(w_up)


def _pos_kernel(off_ref, ek_ref, rk_ref, pos_ref):
    ek = ek_ref[...]
    pos = rk_ref[...]
    for e in range(N_EXPERTS):
        pos = pos + jnp.where(ek == e, off_ref[e], 0)
    pos_ref[...] = pos


def _positions(off, ek, rk):
    n = ek.shape[1]
    tn = min(n, 2048)
    spec = pl.BlockSpec((8, tn), lambda i, off: (0, i))
    return pl.pallas_call(
        _pos_kernel,
        grid_spec=pltpu.PrefetchScalarGridSpec(num_scalar_prefetch=1, grid=(n // tn,), in_specs=[spec, spec],
                                               out_specs=spec),
        out_shape=jax.ShapeDtypeStruct((8, n), I32),
        name="moe_pos",
    )(off, ek, rk)


def _sc_mesh():
    return plsc.VectorSubcoreMesh(core_axis_name="core", subcore_axis_name="subcore")


def _sc_scatter(x, pos_flat, n_rows):
    n = x.shape[0]
    nw = n // SC_WINDOW

    @functools.partial(pl.kernel, out_type=jax.ShapeDtypeStruct((n_rows, x.shape[1]), x.dtype), mesh=_sc_mesh())
    def scatter(x_hbm, p_hbm, o_hbm):
        def body(x_vmem, *p_vmem):
            for p in p_vmem:
                pltpu.sync_copy(x_vmem, o_hbm.at[p.at[0]])

        pltpu.emit_pipeline(
            body, grid=(nw,),
            in_specs=[pl.BlockSpec((SC_WINDOW, x.shape[1]), lambda i: (i, 0))] +
                     [pl.BlockSpec((1, SC_WINDOW), functools.partial(lambda k, i: (0, k * nw + i), k))
                      for k in range(MOE_TOP_K)],
            out_specs=[], core_axis_name=("core", "subcore"), dimension_semantics=(pltpu.PARALLEL,),
        )(x_hbm, *([p_hbm] * MOE_TOP_K))

    return scatter(x, pos_flat)


def _sc_gather(y, pos_flat):
    m = pos_flat.shape[1]

    @functools.partial(pl.kernel, out_type=jax.ShapeDtypeStruct((m, y.shape[1]), y.dtype), mesh=_sc_mesh())
    def gather(y_hbm, p_hbm, o_hbm):
        def body(p_vmem, o_vmem):
            pltpu.sync_copy(y_hbm.at[p_vmem.at[0]], o_vmem)

        pltpu.emit_pipeline(
            body, grid=(m // SC_WINDOW,),
            in_specs=[pl.BlockSpec((1, SC_WINDOW), lambda i: (0, i))],
            out_specs=[pl.BlockSpec((SC_WINDOW, y.shape[1]), lambda i: (i, 0))],
            core_axis_name=("core", "subcore"), dimension_semantics=(pltpu.PARALLEL,),
        )(p_hbm, o_hbm)

    return gather(y, pos_flat)


def _ffn_kernel(be_ref, nu_ref, xlo_ref, xhi_ref, wup_ref, bup_ref, wdn_ref, bdn_ref, ylo_ref, yhi_ref):
    @pl.when(pl.program_id(0) < nu_ref[0])
    def _():
        x = _unpack_rows(xlo_ref[...], xhi_ref[...]).astype(BF16)
        h = _mm(x, wup_ref[0]) + bup_ref[0]
        acts = []
        for b in range(2 * D_FF // UP_BLOCK):
            glu = jnp.minimum(h[:, b * UP_BLOCK:b * UP_BLOCK + LANES], SWIGLU_LIMIT)
            lin = jnp.clip(h[:, b * UP_BLOCK + LANES:(b + 1) * UP_BLOCK], -SWIGLU_LIMIT, SWIGLU_LIMIT)
            acts.append(glu * jax.nn.sigmoid(SWIGLU_ALPHA * glu) * (lin + 1.0))
        act = jnp.concatenate(acts, axis=1)
        ylo_ref[...], yhi_ref[...] = _pack_rows(_mm(act.astype(BF16), wdn_ref[0]) + bdn_ref[0])


def _ffn(block_expert, n_used, xs_lo, xs_hi, wup, bup, wdn, bdn):
    rows = xs_lo.shape[0]
    blk = FFN_BLOCK
    row = pl.BlockSpec((blk, QUARTER), lambda i, be, nu: (i, 0))
    per_expert = lambda a: pl.BlockSpec((1,) + a.shape[1:], lambda i, be, nu: (be[i], 0, 0))
    return pl.pallas_call(
        _ffn_kernel,
        grid_spec=pltpu.PrefetchScalarGridSpec(
            num_scalar_prefetch=2, grid=(rows // blk,),
            in_specs=[row, row, per_expert(wup), per_expert(bup), per_expert(wdn), per_expert(bdn)],
            out_specs=[row, row]),
        out_shape=[jax.ShapeDtypeStruct((rows, QUARTER), I32), jax.ShapeDtypeStruct((rows, QUARTER), I32)],
        compiler_params=pltpu.CompilerParams(dimension_semantics=("arbitrary",), vmem_limit_bytes=40 * MIB),
        name="moe_ffn",
    )(block_expert, n_used, xs_lo, xs_hi, wup, bup, wdn, bdn)


def _combine_kernel(h1_ref, olo_ref, ohi_ref, gtm_ref, g2_ref, b2_ref, o_ref):
    g = gtm_ref[...]
    y = jnp.zeros(h1_ref.shape, F32)
    for k in range(MOE_TOP_K):
        y = y + g[:, k:k + 1] * _unpack_rows(olo_ref[k], ohi_ref[k])
    o_ref[...] = _layer_norm(DEEPNORM_ALPHA * h1_ref[...] + y, g2_ref[...], b2_ref[...])


def _combine(h1, og_lo, og_hi, gtm, g2, b2):
    n = h1.shape[0]
    tm = MERGE_TM
    row = lambda w: pl.BlockSpec((tm, w), lambda i: (i, 0))
    const = lambda a: pl.BlockSpec(a.shape, lambda i: (0,) * a.ndim)
    picked = pl.BlockSpec((MOE_TOP_K, tm, QUARTER), lambda i: (0, i, 0))
    return pl.pallas_call(
        _combine_kernel,
        grid=(n // tm,),
        in_specs=[row(D_MODEL), picked, picked, row(LANES), const(g2), const(b2)],
        out_specs=row(D_MODEL),
        out_shape=jax.ShapeDtypeStruct((n, D_MODEL), F32),
        compiler_params=pltpu.CompilerParams(dimension_semantics=("parallel",), vmem_limit_bytes=40 * MIB),
        name="moe_combine",
    )(h1, og_lo.reshape(MOE_TOP_K, n, QUARTER), og_hi.reshape(MOE_TOP_K, n, QUARTER), gtm, g2, b2)


def _moe(h1, h_lo, h_hi, gtm, ek, rk, totals, wup, bup, wdn, bdn, g2, b2):
    n = h1.shape[0]
    blk = FFN_BLOCK
    n_rows = MOE_TOP_K * n + N_EXPERTS * blk
    counts = totals[:, 0].astype(I32)
    padded = (counts + blk - 1) // blk * blk
    ends = jnp.cumsum(padded)
    block_expert = jnp.minimum(
        jnp.searchsorted(ends, jnp.arange(n_rows // blk, dtype=I32) * blk, side='right'), N_EXPERTS - 1).astype(I32)
    n_used = (ends[-1:] // blk).astype(I32)
    pos = _positions((ends - padded).astype(I32), ek, rk)
    pos_flat = pos[:MOE_TOP_K].reshape(1, MOE_TOP_K * n)
    xs_lo = _sc_scatter(h_lo, pos_flat, n_rows)
    xs_hi = _sc_scatter(h_hi, pos_flat, n_rows)
    ys_lo, ys_hi = _ffn(block_expert, n_used, xs_lo, xs_hi, wup, bup, wdn, bdn)
    return _combine(h1, _sc_gather(ys_lo, pos_flat), _sc_gather(ys_hi, pos_flat), gtm, g2, b2)


def _layer(x, pos_tab, tab_period, s0, rows_per_stream, ret_rows, dsa_keys, weights):
    wp, wr, wd, wo, g1, b1, rwt, rb, wup, bup, wdn, bdn, g2, b2 = weights
    n = x.shape[0]
    n_streams = n // rows_per_stream
    rq, rk, rv, rg, aq, ak, av, iq, ikw, gr, ga = _project(x, wp, pos_tab, tab_period)
    ret, s_new = _retention(rq, rk, rv, rg, s0, rows_per_stream, ret_rows)
    if dsa_keys is None:
        per_stream = lambda a: a.reshape(n_streams, rows_per_stream, a.shape[1])
        od = _dsa(aq, iq, ikw, per_stream(ak), per_stream(av), per_stream(ikw), DSA_TQ, True, rows_per_stream)
    else:
        kk, vv, ikk, limit = dsa_keys(ak, av, ikw)
        od = _dsa(aq, iq, ikw, kk, vv, ikk, rows_per_stream, False, limit)
    h1, h_lo, h_hi, gtm, ek, rk, totals = _merge(x, ret, od, gr, ga, wr, wd, wo, g1, b1, rwt, rb)
    y = _moe(h1, h_lo, h_hi, gtm, ek, rk, totals, wup, bup, wdn, bdn, g2, b2)
    return y, s_new, ak, av, ikw[:, :IDX_DIM]


def kernel(x_prompt, x_sample, state_ret, cache_k, cache_v, cache_idx_k, w_in, w_ret_o, w_dsa_o, w_o,
           ln1_g, ln1_b, router_w, router_b, w_up, b_up, w_down, b_down, ln2_g, ln2_b):
    assert w_in.shape[0] == DEPTH
    batch, seq, _ = x_prompt.shape
    dec_batch, dec_seq, _ = x_sample.shape
    past = cache_k.shape[2]
    assert seq % PROJ_TM == 0 and seq % DSA_TQ == 0 and PROJ_TM % dec_seq == 0

    l = 0
    weights = (
        _pack_w_in(w_in[l]), w_ret_o[l].astype(BF16), w_dsa_o[l].astype(BF16), w_o[l].astype(BF16),
        ln1_g[l][None, :], ln1_b[l][None, :], router_w[l].T, router_b[l][:, None],
        _deinterleave_w_up(w_up[l]),
        b_up[l].reshape(N_EXPERTS, 2 * D_FF // UP_BLOCK, LANES, 2).transpose(0, 1, 3, 2).reshape(N_EXPERTS, 1, 2 * D_FF),
        w_down[l].astype(BF16), b_down[l][:, None, :], ln2_g[l][None, :], ln2_b[l][None, :])

    tab_p = _rot_tables(jnp.arange(seq))
    zeros_state = jnp.zeros((batch, RET_HEADS, RET_DK, RET_DV), F32)
    y_p, s_p, k_p, v_p, ik_p = _layer(x_prompt.reshape(batch * seq, D_MODEL), tab_p, seq // PROJ_TM, zeros_state,
                                      seq, RET_CHUNK, None, weights)

    n_keys_real = past + dec_seq
    n_keys = -(-n_keys_real // LANES) * LANES
    tab_s = jnp.tile(_rot_tables(past + jnp.arange(dec_seq)), (PROJ_TM // dec_seq, 1))

    def sample_keys(ak, av, ikw):
        def cat(cache, new, width):
            padz = jnp.zeros((dec_batch, n_keys - n_keys_real, width), F32)
            return jnp.concatenate([cache.reshape(dec_batch, past, width), new.reshape(dec_batch, dec_seq, width), padz],
                                   axis=1)
        return (cat(cache_k[l], ak, LANES), cat(cache_v[l], av, LANES),
                cat(cache_idx_k[l], ikw[:, :IDX_DIM], IDX_DIM), n_keys_real)

    y_s, s_s, k_s, v_s, ik_s = _layer(x_sample.reshape(dec_batch * dec_seq, D_MODEL), tab_s, 1, state_ret[l],
                                      dec_seq, dec_seq, sample_keys, weights)

    kv = (DSA_KV_HEADS, DSA_HEAD_DIM)
    return (y_p.reshape(batch, seq, D_MODEL), y_s.reshape(dec_batch, dec_seq, D_MODEL),
            s_p[None], k_p.reshape(1, batch, seq, *kv), v_p.reshape(1, batch, seq, *kv),
            ik_p.reshape(1, batch, seq, IDX_DIM),
            s_s[None], k_s.reshape(1, dec_batch, dec_seq, *kv), v_s.reshape(1, dec_batch, dec_seq, *kv),
            ik_s.reshape(1, dec_batch, dec_seq, IDX_DIM))
```

```python
import functools

import numpy as np
import jax
import jax.numpy as jnp
from jax import lax
from jax.experimental import pallas as pl
from jax.experimental.pallas import tpu as pltpu
from jax.experimental.pallas import tpu_sc as plsc

F32 = jnp.float32
BF16 = jnp.bfloat16
I32 = jnp.int32

D_MODEL = 1024
CHUNK = 64
RET_HEADS = 4
RET_DK = 128
RET_DV = 256
RET_ROPE_BASE = 10000.0
DSA_HEADS = 8
DSA_KV_HEADS = 2
DSA_HEAD_DIM = 64
IDX_HEADS = 8
IDX_DIM = 64
DSA_TOPK = 256
ROPE_THETA = 500000.0
N_EXPERTS = 32
MOE_TOP_K = 4
D_FF = 1024
SWIGLU_ALPHA = 1.702
SWIGLU_LIMIT = 7.0
LN_EPS = 1e-5
GN_EPS = 1e-6
DEPTH = 1
DEEPNORM_ALPHA = (2.0 * DEPTH) ** 0.25
PROJ_WIDTHS = (RET_HEADS * RET_DK, RET_HEADS * RET_DK, RET_HEADS * RET_DV, RET_HEADS * RET_DV,
               DSA_HEADS * DSA_HEAD_DIM, DSA_KV_HEADS * DSA_HEAD_DIM, DSA_KV_HEADS * DSA_HEAD_DIM,
               IDX_HEADS * IDX_DIM, IDX_DIM, IDX_HEADS, D_MODEL, D_MODEL)

LANES = 128
MIB = 1024 * 1024

OFF_RQ, OFF_RK, OFF_RV, OFF_RG = 0, 512, 1024, 2048
OFF_AQ, OFF_AK, OFF_AV, OFF_IQ, OFF_IKW = 3072, 3584, 3712, 3840, 4352
OFF_GR, OFF_GA, PACKED_COLS = 4480, 5504, 6528
TAB_COLS = 8 * LANES

PROJ_TM = 256
RET_CHUNK = 128
DSA_TQ = 256
MERGE_TM = 512
FFN_BLOCK = 256
SC_WINDOW = 128
QUARTER = D_MODEL // 4
UP_BLOCK = 2 * LANES


def _nt(a, b):
    return lax.dot_general(a, b, (((1,), (1,)), ((), ())), preferred_element_type=F32)


def _mm(a, b):
    return jnp.dot(a, b, preferred_element_type=F32)


def _proj_kernel(x_ref, w_ref, tab_ref, rq_ref, rk_ref, rv_ref, rg_ref, aq_ref, ak_ref, av_ref,
                 iq_ref, ikw_ref, gr_ref, ga_ref):
    xb = x_ref[...].astype(BF16)

    def mm(c0, n):
        return _mm(xb, w_ref[:, c0:c0 + n])

    def tab(i):
        return tab_ref[:, i * LANES:(i + 1) * LANES]

    def rot_full(z):
        return z * tab(0) + pltpu.roll(z, 64, 1) * tab(1)

    def rot_part(z, c):
        return z * tab(c) + pltpu.roll(z, LANES - 8, 1) * tab(c + 1) + pltpu.roll(z, 8, 1) * tab(c + 2)

    z = mm(OFF_RQ, 512)
    for h in range(4):
        sl = slice(h * LANES, (h + 1) * LANES)
        rq_ref[:, sl] = rot_full(z[:, sl]).astype(BF16)
    z = mm(OFF_RK, 512)
    for h in range(4):
        sl = slice(h * LANES, (h + 1) * LANES)
        rk_ref[:, sl] = (rot_full(z[:, sl]) * (RET_DK ** -0.5)).astype(BF16)
    for c in range(2):
        rv_ref[:, c * 512:(c + 1) * 512] = mm(OFF_RV + c * 512, 512).astype(BF16)
    for c in range(2):
        rg_ref[:, c * 512:(c + 1) * 512] = mm(OFF_RG + c * 512, 512)
    z = mm(OFF_AQ, 512)
    for h in range(4):
        sl = slice(h * LANES, (h + 1) * LANES)
        aq_ref[:, sl] = rot_part(z[:, sl], 2).astype(BF16)
    z = mm(OFF_AK, 256)
    ak_ref[...] = rot_part(z[:, :LANES], 2)
    av_ref[...] = z[:, LANES:]
    z = mm(OFF_IQ, 512)
    for h in range(4):
        sl = slice(h * LANES, (h + 1) * LANES)
        iq_ref[:, sl] = rot_part(z[:, sl], 2).astype(BF16)
    ikw_ref[...] = rot_part(mm(OFF_IKW, LANES), 5)
    for c in range(2):
        gr_ref[:, c * 512:(c + 1) * 512] = mm(OFF_GR + c * 512, 512)
    for c in range(2):
        ga_ref[:, c * 512:(c + 1) * 512] = mm(OFF_GA + c * 512, 512)


def _rot_tables(pos):
    p = pos.shape[0]
    posf = pos.astype(F32)[:, None]
    ret_f = RET_ROPE_BASE ** (-jnp.linspace(0.0, 1.0, RET_DK // 2, dtype=F32))
    ang = posf * ret_f[None, :]
    c, s = jnp.cos(ang), jnp.sin(ang)
    cos_r = jnp.concatenate([c, c], 1)
    sin_r = jnp.concatenate([-s, s], 1)
    n_rot = DSA_HEAD_DIM // 4
    att_f = ROPE_THETA ** (-jnp.arange(0, n_rot, 2, dtype=F32) / n_rot)
    ang2 = posf * att_f[None, :]
    c2, s2 = jnp.cos(ang2), jnp.sin(ang2)
    half = n_rot // 2
    rest = DSA_HEAD_DIM - 2 * half
    c64 = jnp.concatenate([c2, c2, jnp.ones((p, rest), F32)], 1)
    s1_64 = jnp.concatenate([-s2, jnp.zeros((p, DSA_HEAD_DIM - half), F32)], 1)
    s2_64 = jnp.concatenate([jnp.zeros((p, half), F32), s2, jnp.zeros((p, rest), F32)], 1)
    z64 = jnp.zeros((p, DSA_HEAD_DIM), F32)
    ci = jnp.concatenate([c64, jnp.full((p, IDX_HEADS), IDX_HEADS ** -0.5, F32),
                          jnp.zeros((p, DSA_HEAD_DIM - IDX_HEADS), F32)], 1)
    return jnp.concatenate([cos_r, sin_r,
                            jnp.concatenate([c64, c64], 1), jnp.concatenate([s1_64, s1_64], 1),
                            jnp.concatenate([s2_64, s2_64], 1),
                            ci, jnp.concatenate([s1_64, z64], 1), jnp.concatenate([s2_64, z64], 1)], 1)


def _pack_w_in(w):
    cuts = np.cumsum(PROJ_WIDTHS)[:-1].tolist()
    rq, rk, rv, rg, aq, ak, av, iq, ik, iw, gr, ga = jnp.split(w, cuts, axis=1)
    pad = jnp.zeros((w.shape[0], LANES - IDX_DIM - IDX_HEADS), w.dtype)
    return jnp.concatenate([rq, rk, rv, rg, aq, ak, av, iq, ik, iw, pad, gr, ga], axis=1).astype(BF16)


def _project(x, wp, tab, tab_period):
    n = x.shape[0]
    tm = PROJ_TM
    row = lambda w: pl.BlockSpec((tm, w), lambda i: (i, 0))
    out_shapes = [((n, 512), BF16), ((n, 512), BF16), ((n, 1024), BF16), ((n, 1024), F32),
                  ((n, 512), BF16), ((n, LANES), F32), ((n, LANES), F32), ((n, 512), BF16),
                  ((n, LANES), F32), ((n, 1024), F32), ((n, 1024), F32)]
    return pl.pallas_call(
        _proj_kernel,
        grid=(n // tm,),
        in_specs=[row(D_MODEL),
                  pl.BlockSpec((D_MODEL, PACKED_COLS), lambda i: (0, 0), pipeline_mode=pl.Buffered(1)),
                  pl.BlockSpec((tm, TAB_COLS), lambda i: (i % tab_period, 0))],
        out_specs=[row(s[1]) for s, _ in out_shapes],
        out_shape=[jax.ShapeDtypeStruct(s, d) for s, d in out_shapes],
        compiler_params=pltpu.CompilerParams(dimension_semantics=("parallel",), vmem_limit_bytes=44 * MIB),
        name="proj",
    )(x, wp, tab)


def _ret_kernel(dec_ref, xi_ref, zeta_ref, rq_ref, rk_ref, rv_ref, rg_ref, s0_ref, ret_ref, sout_ref, st_ref,
                *, rows, n_chunk, g_pow):
    j = pl.program_id(1)
    cpad = RET_CHUNK

    @pl.when(j == 0)
    def _():
        st_ref[...] = s0_ref[0]

    def padded(v):
        if rows == cpad:
            return v
        return jnp.concatenate([v, jnp.zeros((cpad - rows, v.shape[1]), v.dtype)], axis=0)

    for c in range(n_chunk):
        rs = slice(c * rows, (c + 1) * rows)
        for h in range(RET_HEADS):
            ks = slice(h * RET_DK, (h + 1) * RET_DK)
            vs = slice(h * RET_DV, (h + 1) * RET_DV)
            q = padded(rq_ref[rs, ks])
            kt = padded(rk_ref[rs, ks].astype(F32)).T
            v = padded(rv_ref[rs, vs])
            s = st_ref[h]
            sc = _mm(q, kt.astype(BF16)) * dec_ref[h]
            o = _mm(sc.astype(BF16), v) + _mm(q, s.astype(BF16)) * xi_ref[h]
            st_ref[h] = g_pow[h] * s + _mm((kt * zeta_ref[h]).astype(BF16), v)
            o = o[:rows]
            mu = jnp.mean(o, axis=-1, keepdims=True)
            d = o - mu
            var = jnp.mean(d * d, axis=-1, keepdims=True)
            gn = d * lax.rsqrt(var + GN_EPS)
            g = rg_ref[rs, vs]
            ret_ref[rs, vs] = (gn * (g * jax.nn.sigmoid(g))).astype(BF16)

    @pl.when(j == pl.num_programs(1) - 1)
    def _():
        sout_ref[0] = st_ref[...]


def _retention(rq, rk, rv, rg, s0, rows_per_stream, rows):
    n = rq.shape[0]
    n_streams = n // rows_per_stream
    n_chunk = min(rows_per_stream // rows, 2)
    blk = rows * n_chunk
    nb = rows_per_stream // blk
    gam = 1.0 - 2.0 ** (-5.0 - np.arange(RET_HEADS, dtype=np.float64))
    i = np.arange(RET_CHUNK, dtype=np.float64)
    diff = i[:, None] - i[None, :]
    dec = np.where(diff >= 0, gam[:, None, None] ** np.maximum(diff, 0.0)[None], 0.0)
    xi = gam[:, None, None] ** (i + 1.0)[None, :, None]
    zeta = np.where(i < rows, gam[:, None, None] ** (rows - 1.0 - i)[None, None, :], 0.0)
    g_pow = tuple(float(g ** rows) for g in gam)
    const = lambda shape: pl.BlockSpec(shape, lambda s, j: (0,) * len(shape))
    row = lambda w: pl.BlockSpec((blk, w), lambda s, j: (s * nb + j, 0))
    st = pl.BlockSpec((1, RET_HEADS, RET_DK, RET_DV), lambda s, j: (s, 0, 0, 0))
    return pl.pallas_call(
        functools.partial(_ret_kernel, rows=rows, n_chunk=n_chunk, g_pow=g_pow),
        grid=(n_streams, nb),
        in_specs=[const((RET_HEADS, RET_CHUNK, RET_CHUNK)), const((RET_HEADS, RET_CHUNK, 1)),
                  const((RET_HEADS, 1, RET_CHUNK)), row(512), row(512), row(1024), row(1024), st],
        out_specs=[row(1024), st],
        out_shape=[jax.ShapeDtypeStruct((n, RET_HEADS * RET_DV), BF16),
                   jax.ShapeDtypeStruct((n_streams, RET_HEADS, RET_DK, RET_DV), F32)],
        scratch_shapes=[pltpu.VMEM((RET_HEADS, RET_DK, RET_DV), F32)],
        compiler_params=pltpu.CompilerParams(dimension_semantics=("parallel", "arbitrary"),
                                             vmem_limit_bytes=32 * MIB),
        name="retention",
    )(jnp.asarray(dec, F32), jnp.asarray(xi, F32), jnp.asarray(zeta, F32), rq, rk, rv, rg, s0)


def _dsa_kernel(aq_ref, iq_ref, ikwq_ref, kk_ref, vv_ref, ikk_ref, o_ref, w_ref, bias_ref,
                *, tq, n_keys, row0, limit_const, n_sel):
    nsel_f = float(n_sel)
    neg_inf = -jnp.inf

    ikb = ikk_ref[0][:, :IDX_DIM].astype(BF16)
    iww = ikwq_ref[...][:, IDX_DIM:IDX_DIM + IDX_HEADS] * (IDX_DIM ** -0.5)
    acc = jnp.zeros((tq, n_keys), F32)
    for p in range(IDX_HEADS // 2):
        slab = iq_ref[:, p * LANES:(p + 1) * LANES]
        for hh in range(2):
            h = 2 * p + hh
            s = _nt(slab[:, hh * IDX_DIM:(hh + 1) * IDX_DIM], ikb)
            acc = acc + jnp.maximum(s, 0.0) * iww[:, h:h + 1]

    col = lax.broadcasted_iota(I32, (tq, n_keys), 1)
    if limit_const is None:
        row = lax.broadcasted_iota(I32, (tq, 1), 0) + row0
        limit = (jnp.right_shift(row, 6) + 1) * CHUNK
    else:
        limit = limit_const
    admissible = col < limit
    sc = jnp.where(admissible, acc, neg_inf)

    pos = jnp.sum(jnp.where(sc >= 0.0, 1.0, 0.0), axis=1, keepdims=True) >= nsel_f
    kk = jnp.where(pos, nsel_f, float(n_keys - n_sel + 1))
    w_ref[...] = jnp.where(pos, sc, -sc)

    def bit_step(i, u):
        cand_u = u | jnp.left_shift(jnp.int32(1), 30 - i)
        cand = pltpu.bitcast(cand_u, F32)
        cnt = jnp.sum(jnp.where(w_ref[...] >= cand, 1.0, 0.0), axis=1, keepdims=True)
        return jnp.where(cnt >= kk, cand_u, u)

    mag_u = lax.fori_loop(0, 31, bit_step, jnp.zeros((tq, 1), I32))
    mag = pltpu.bitcast(mag_u, F32)
    thr = jnp.where(pos, mag, -mag)

    sc = jnp.where(pos, w_ref[...], -w_ref[...])
    short = jnp.sum(jnp.where(sc >= thr, 1.0, 0.0), axis=1, keepdims=True) < nsel_f
    thr = jnp.where(jnp.logical_and(short, jnp.logical_not(pos)), -pltpu.bitcast(mag_u + 1, F32), thr)
    ge = sc >= thr
    cnt_gt = jnp.sum(jnp.where(sc > thr, 1.0, 0.0), axis=1, keepdims=True)
    cnt_ge = jnp.sum(jnp.where(ge, 1.0, 0.0), axis=1, keepdims=True)
    bias_ref[...] = jnp.where(jnp.logical_and(ge, admissible), 0.0, neg_inf)
    excess = jnp.logical_and(cnt_ge > nsel_f, thr > neg_inf)

    @pl.when(jnp.max(jnp.where(excess, 1.0, 0.0)) > 0.0)
    def _():
        need = nsel_f - cnt_gt
        tri = jnp.where(lax.broadcasted_iota(I32, (LANES, LANES), 0) < lax.broadcasted_iota(I32, (LANES, LANES), 1),
                        1.0, 0.0).astype(BF16)
        before = jnp.zeros((tq, 1), F32)
        for b in range(n_keys // LANES):
            sl = slice(b * LANES, (b + 1) * LANES)
            sblk = jnp.where(pos, w_ref[:, sl], -w_ref[:, sl])
            eq = jnp.where(sblk == thr, 1.0, 0.0)
            rank = _mm(eq.astype(BF16), tri) + before
            keep = jnp.logical_or(sblk > thr, jnp.logical_and(sblk == thr, rank < need))
            bias_ref[:, sl] = jnp.where(jnp.logical_and(keep, sblk > neg_inf), 0.0, neg_inf)
            before = before + jnp.sum(eq, axis=1, keepdims=True)

    kfull = kk_ref[0]
    vfull = vv_ref[0]
    for g in range(DSA_KV_HEADS):
        gs = slice(g * DSA_HEAD_DIM, (g + 1) * DSA_HEAD_DIM)
        kg = kfull[:, gs].astype(BF16)
        vg = vfull[:, gs].astype(BF16)
        for pp in range(2):
            p = 2 * g + pp
            slab = (aq_ref[:, p * LANES:(p + 1) * LANES].astype(F32) * (DSA_HEAD_DIM ** -0.5)).astype(BF16)
            outs = []
            for hh in range(2):
                lg = _nt(slab[:, hh * DSA_HEAD_DIM:(hh + 1) * DSA_HEAD_DIM], kg) + bias_ref[...]
                m = jnp.max(lg, axis=1, keepdims=True)
                pr = jnp.exp(lg - m)
                den = jnp.sum(pr, axis=1, keepdims=True)
                outs.append(_mm(pr.astype(BF16), vg) / den)
            o_ref[:, p * LANES:(p + 1) * LANES] = jnp.concatenate(outs, axis=1).astype(BF16)


def _dsa_call(aq, iq, ikw, kk, vv, ikk, tq, nq, jq, n_keys, limit_const, n_sel, name):
    n_streams = kk.shape[0]
    qrow = lambda w: pl.BlockSpec((tq, w), lambda s: (s * nq + jq, 0))
    krow = lambda a: pl.BlockSpec((1, n_keys, a.shape[2]), lambda s: (s, 0, 0))
    return pl.pallas_call(
        functools.partial(_dsa_kernel, tq=tq, n_keys=n_keys, row0=jq * tq, limit_const=limit_const, n_sel=n_sel),
        grid=(n_streams,),
        in_specs=[qrow(512), qrow(512), qrow(LANES), krow(kk), krow(vv), krow(ikk)],
        out_specs=pl.BlockSpec((tq, DSA_HEADS * DSA_HEAD_DIM), lambda s: (s, 0)),
        out_shape=jax.ShapeDtypeStruct((n_streams * tq, DSA_HEADS * DSA_HEAD_DIM), BF16),
        scratch_shapes=[pltpu.VMEM((tq, n_keys), F32), pltpu.VMEM((tq, n_keys), F32)],
        compiler_params=pltpu.CompilerParams(dimension_semantics=("parallel",), vmem_limit_bytes=48 * MIB),
        name=name,
    )(aq, iq, ikw, kk, vv, ikk)


def _dsa(aq, iq, ikw, kk, vv, ikk, tq, chunk_causal, limit_const):
    n_streams, n_keys, _ = kk.shape
    n = aq.shape[0]
    nq = n // n_streams // tq
    n_sel = min(DSA_TOPK, limit_const // 4)
    if not chunk_causal:
        assert nq == 1
        return _dsa_call(aq, iq, ikw, kk, vv, ikk, tq, 1, 0, n_keys, limit_const, n_sel, "dsa_s")
    parts = [_dsa_call(aq, iq, ikw, kk, vv, ikk, tq, nq, jq, (jq + 1) * tq, None, n_sel, f"dsa_p{jq}")
             for jq in range(nq)]
    width = parts[0].shape[1]
    return jnp.stack(parts, 0).reshape(nq, n_streams, tq, width).transpose(1, 0, 2, 3).reshape(n, width)


def _pack_rows(v):
    q = QUARTER
    bits = lambda x: pltpu.bitcast(x.astype(BF16).astype(F32), I32)
    pair = lambda c: lax.shift_right_logical(bits(v[:, c * q:(c + 1) * q]), 16) | bits(v[:, (c + 1) * q:(c + 2) * q])
    return pair(0), pair(2)


def _unpack_rows(lo, hi):
    parts = []
    for w in (lo, hi):
        parts.append(pltpu.bitcast(lax.shift_left(w, 16), F32))
        parts.append(pltpu.bitcast(w & jnp.int32(-65536), F32))
    return jnp.concatenate(parts, axis=1)


def _layer_norm(v, g, b):
    mu = jnp.mean(v, axis=-1, keepdims=True)
    d = v - mu
    var = jnp.mean(d * d, axis=-1, keepdims=True)
    return d * lax.rsqrt(var + LN_EPS) * g + b


def _merge_kernel(x_ref, ret_ref, od_ref, gr_ref, ga_ref, wr_ref, wd_ref, wo_ref, g1_ref, b1_ref, rwt_ref, rb_ref,
                  tri_ref, cnt0_ref, h1_ref, hlo_ref, hhi_ref, gtm_ref, ek_ref, rk_ref, tot_ref, cnt_ref):
    @pl.when(pl.program_id(0) == 0)
    def _():
        cnt_ref[...] = cnt0_ref[:, 0:1]

    y_ret = _mm(ret_ref[...], wr_ref[...])
    y_dsa = _mm(od_ref[...], wd_ref[...])
    merged = jax.nn.sigmoid(gr_ref[...]) * y_ret + jax.nn.sigmoid(ga_ref[...]) * y_dsa
    mix = _mm(merged.astype(BF16), wo_ref[...])
    h1 = _layer_norm(DEEPNORM_ALPHA * x_ref[...] + mix, g1_ref[...], b1_ref[...])
    h1_ref[...] = h1
    hlo_ref[...], hhi_ref[...] = _pack_rows(h1)

    logits = lax.dot_general(rwt_ref[...], h1, (((1,), (1,)), ((), ())), preferred_element_type=F32,
                             precision=lax.Precision.HIGHEST) + rb_ref[...]
    tm = logits.shape[1]
    e_iota = lax.broadcasted_iota(I32, (N_EXPERTS, tm), 0)
    tops, hots, firsts = [], [], []
    for _ in range(MOE_TOP_K):
        m = jnp.max(logits, axis=0, keepdims=True)
        first = jnp.min(jnp.where(logits == m, e_iota, N_EXPERTS), axis=0, keepdims=True)
        hot = e_iota == first
        tops.append(m)
        hots.append(hot)
        firsts.append(first)
        logits = jnp.where(hot, -jnp.inf, logits)
    exps = [jnp.exp(m - tops[0]) for m in tops]
    den = exps[0] + exps[1] + exps[2] + exps[3]
    sel = jnp.zeros((N_EXPERTS, tm), F32)
    for hot in hots:
        sel = sel + jnp.where(hot, 1.0, 0.0)
    rank = _mm(sel.astype(BF16), tri_ref[...]) + cnt_ref[...]
    ranks = [jnp.sum(jnp.where(hot, rank, 0.0), axis=0, keepdims=True).astype(I32) for hot in hots]
    pad_i = jnp.zeros((8 - MOE_TOP_K, tm), I32)
    ek_ref[...] = jnp.concatenate(firsts + [pad_i], axis=0)
    rk_ref[...] = jnp.concatenate(ranks + [pad_i], axis=0)
    gates = jnp.concatenate([e / den for e in exps] + [jnp.zeros((LANES - MOE_TOP_K, tm), F32)], axis=0)
    gtm_ref[...] = gates.T
    cnt_ref[...] = cnt_ref[...] + jnp.sum(sel, axis=1, keepdims=True)
    tot_ref[...] = jnp.broadcast_to(cnt_ref[...], tot_ref.shape)


def _merge(x, ret, od, gr, ga, wr, wd, wo, g1, b1, rwt, rb, cnt0):
    n = x.shape[0]
    tm = MERGE_TM
    row = lambda w: pl.BlockSpec((tm, w), lambda i: (i, 0))
    const = lambda a: pl.BlockSpec(a.shape, lambda i: (0,) * a.ndim)
    col = pl.BlockSpec((8, tm), lambda i: (0, i))
    tri = jnp.asarray(np.triu(np.ones((tm, tm), np.float32), 1), BF16)
    return pl.pallas_call(
        _merge_kernel,
        grid=(n // tm,),
        in_specs=[row(D_MODEL), row(1024), row(512), row(1024), row(1024), const(wr), const(wd), const(wo),
                  const(g1), const(b1), const(rwt), const(rb), const(tri), const(cnt0)],
        out_specs=[row(D_MODEL), row(QUARTER), row(QUARTER), row(LANES), col, col,
                   pl.BlockSpec((N_EXPERTS, LANES), lambda i: (0, 0))],
        out_shape=[jax.ShapeDtypeStruct((n, D_MODEL), F32), jax.ShapeDtypeStruct((n, QUARTER), I32),
                   jax.ShapeDtypeStruct((n, QUARTER), I32), jax.ShapeDtypeStruct((n, LANES), F32),
                   jax.ShapeDtypeStruct((8, n), I32), jax.ShapeDtypeStruct((8, n), I32),
                   jax.ShapeDtypeStruct((N_EXPERTS, LANES), F32)],
        scratch_shapes=[pltpu.VMEM((N_EXPERTS, 1), F32)],
        compiler_params=pltpu.CompilerParams(dimension_semantics=("arbitrary",), vmem_limit_bytes=48 * MIB),
        name="merge",
    )(x, ret, od, gr, ga, wr, wd, wo, g1, b1, rwt, rb, tri, cnt0)


def _deinterleave_kernel(w_ref, o_ref):
    r = lax.broadcasted_iota(I32, (UP_BLOCK, UP_BLOCK), 0)
    c = lax.broadcasted_iota(I32, (UP_BLOCK, UP_BLOCK), 1)
    src = jnp.where(c < LANES, 2 * c, 2 * (c - LANES) + 1)
    perm = jnp.where(r == src, 1.0, 0.0).astype(BF16)
    for b in range(w_ref.shape[2] // UP_BLOCK):
        sl = slice(b * UP_BLOCK, (b + 1) * UP_BLOCK)
        o_ref[0, :, sl] = _mm(w_ref[0, :, sl].astype(BF16), perm).astype(BF16)


def _deinterleave_w_up(w_up):
    n_e, d_in, d_out = w_up.shape
    cols = 512
    spec = pl.BlockSpec((1, d_in, cols), lambda e, c: (e, 0, c))
    return pl.pallas_call(
        _deinterleave_kernel,
        grid=(n_e, d_out // cols),
        in_specs=[spec],
        out_specs=spec,
        out_shape=jax.ShapeDtypeStruct(w_up.shape, BF16),
        compiler_params=pltpu.CompilerParams(dimension_semantics=("parallel", "parallel"),
                                             vmem_limit_bytes=24 * MIB),
        name="w_up_prep",
    )(w_up)


def _pos_kernel(off_ref, ek_ref, rk_ref, pos_ref):
    ek = ek_ref[...]
    pos = rk_ref[...]
    for e in range(N_EXPERTS):
        pos = pos + jnp.where(ek == e, off_ref[e], 0)
    pos_ref[...] = pos


def _positions(off, ek, rk):
    n = ek.shape[1]
    tn = min(n, 2048)
    spec = pl.BlockSpec((8, tn), lambda i, off: (0, i))
    return pl.pallas_call(
        _pos_kernel,
        grid_spec=pltpu.PrefetchScalarGridSpec(num_scalar_prefetch=1, grid=(n // tn,), in_specs=[spec, spec],
                                               out_specs=spec),
        out_shape=jax.ShapeDtypeStruct((8, n), I32),
        name="moe_pos",
    )(off, ek, rk)


def _sc_mesh():
    return plsc.VectorSubcoreMesh(core_axis_name="core", subcore_axis_name="subcore")


def _sc_scatter(x, pos_flat, n_rows):
    n = x.shape[0]
    nw = n // SC_WINDOW

    @functools.partial(pl.kernel, out_type=jax.ShapeDtypeStruct((n_rows, x.shape[1]), x.dtype), mesh=_sc_mesh())
    def scatter(x_hbm, p_hbm, o_hbm):
        def body(x_vmem, *p_vmem):
            for p in p_vmem:
                pltpu.sync_copy(x_vmem, o_hbm.at[p.at[0]])

        pltpu.emit_pipeline(
            body, grid=(nw,),
            in_specs=[pl.BlockSpec((SC_WINDOW, x.shape[1]), lambda i: (i, 0))] +
                     [pl.BlockSpec((1, SC_WINDOW), functools.partial(lambda k, i: (0, k * nw + i), k))
                      for k in range(MOE_TOP_K)],
            out_specs=[], core_axis_name=("core", "subcore"), dimension_semantics=(pltpu.PARALLEL,),
        )(x_hbm, *([p_hbm] * MOE_TOP_K))

    return scatter(x, pos_flat)


def _sc_gather(y, pos_flat):
    m = pos_flat.shape[1]

    @functools.partial(pl.kernel, out_type=jax.ShapeDtypeStruct((m, y.shape[1]), y.dtype), mesh=_sc_mesh())
    def gather(y_hbm, p_hbm, o_hbm):
        def body(p_vmem, o_vmem):
            pltpu.sync_copy(y_hbm.at[p_vmem.at[0]], o_vmem)

        pltpu.emit_pipeline(
            body, grid=(m // SC_WINDOW,),
            in_specs=[pl.BlockSpec((1, SC_WINDOW), lambda i: (0, i))],
            out_specs=[pl.BlockSpec((SC_WINDOW, y.shape[1]), lambda i: (i, 0))],
            core_axis_name=("core", "subcore"), dimension_semantics=(pltpu.PARALLEL,),
        )(p_hbm, o_hbm)

    return gather(y, pos_flat)


def _ffn_kernel(be_ref, nu_ref, xlo_ref, xhi_ref, wup_ref, bup_ref, wdn_ref, bdn_ref, ylo_ref, yhi_ref):
    @pl.when(pl.program_id(0) < nu_ref[0])
    def _():
        x = _unpack_rows(xlo_ref[...], xhi_ref[...]).astype(BF16)
        h = _mm(x, wup_ref[0]) + bup_ref[0]
        acts = []
        for b in range(2 * D_FF // UP_BLOCK):
            glu = jnp.minimum(h[:, b * UP_BLOCK:b * UP_BLOCK + LANES], SWIGLU_LIMIT)
            lin = jnp.clip(h[:, b * UP_BLOCK + LANES:(b + 1) * UP_BLOCK], -SWIGLU_LIMIT, SWIGLU_LIMIT)
            acts.append(glu * jax.nn.sigmoid(SWIGLU_ALPHA * glu) * (lin + 1.0))
        act = jnp.concatenate(acts, axis=1)
        ylo_ref[...], yhi_ref[...] = _pack_rows(_mm(act.astype(BF16), wdn_ref[0]) + bdn_ref[0])


def _ffn(block_expert, n_used, xs_lo, xs_hi, wup, bup, wdn, bdn):
    rows = xs_lo.shape[0]
    blk = FFN_BLOCK
    row = pl.BlockSpec((blk, QUARTER), lambda i, be, nu: (i, 0))
    per_expert = lambda a: pl.BlockSpec((1,) + a.shape[1:], lambda i, be, nu: (be[i], 0, 0))
    return pl.pallas_call(
        _ffn_kernel,
        grid_spec=pltpu.PrefetchScalarGridSpec(
            num_scalar_prefetch=2, grid=(rows // blk,),
            in_specs=[row, row, per_expert(wup), per_expert(bup), per_expert(wdn), per_expert(bdn)],
            out_specs=[row, row]),
        out_shape=[jax.ShapeDtypeStruct((rows, QUARTER), I32), jax.ShapeDtypeStruct((rows, QUARTER), I32)],
        compiler_params=pltpu.CompilerParams(dimension_semantics=("arbitrary",), vmem_limit_bytes=40 * MIB),
        name="moe_ffn",
    )(block_expert, n_used, xs_lo, xs_hi, wup, bup, wdn, bdn)


def _combine_kernel(h1_ref, olo_ref, ohi_ref, gtm_ref, g2_ref, b2_ref, o_ref):
    g = gtm_ref[...]
    y = jnp.zeros(h1_ref.shape, F32)
    for k in range(MOE_TOP_K):
        y = y + g[:, k:k + 1] * _unpack_rows(olo_ref[k], ohi_ref[k])
    o_ref[...] = _layer_norm(DEEPNORM_ALPHA * h1_ref[...] + y, g2_ref[...], b2_ref[...])


def _combine(h1, og_lo, og_hi, gtm, g2, b2, row0):
    n = h1.shape[0]
    n_all = og_lo.shape[0] // MOE_TOP_K
    tm = MERGE_TM
    row = lambda w: pl.BlockSpec((tm, w), lambda i: (i, 0))
    const = lambda a: pl.BlockSpec(a.shape, lambda i: (0,) * a.ndim)
    picked = pl.BlockSpec((MOE_TOP_K, tm, QUARTER), lambda i: (0, i + row0 // tm, 0))
    return pl.pallas_call(
        _combine_kernel,
        grid=(n // tm,),
        in_specs=[row(D_MODEL), picked, picked, row(LANES), const(g2), const(b2)],
        out_specs=row(D_MODEL),
        out_shape=jax.ShapeDtypeStruct((n, D_MODEL), F32),
        compiler_params=pltpu.CompilerParams(dimension_semantics=("parallel",), vmem_limit_bytes=40 * MIB),
        name="moe_combine",
    )(h1, og_lo.reshape(MOE_TOP_K, n_all, QUARTER), og_hi.reshape(MOE_TOP_K, n_all, QUARTER), gtm, g2, b2)


def _moe(groups, totals, wup, bup, wdn, bdn, g2, b2):
    blk = FFN_BLOCK
    sizes = [g[0].shape[0] for g in groups]
    n_all = sum(sizes)
    n_rows = MOE_TOP_K * n_all + N_EXPERTS * blk
    counts = totals[:, 0].astype(I32)
    padded = (counts + blk - 1) // blk * blk
    ends = jnp.cumsum(padded)
    block_start = jnp.arange(n_rows // blk, dtype=I32) * blk
    block_expert = jnp.minimum(jnp.sum((ends[None, :] <= block_start[:, None]).astype(I32), axis=1), N_EXPERTS - 1)
    n_used = (ends[-1:] // blk).astype(I32)
    cat = lambda i, axis: jnp.concatenate([g[i] for g in groups], axis=axis)
    pos = _positions((ends - padded).astype(I32), cat(4, 1), cat(5, 1))
    pos_flat = pos[:MOE_TOP_K].reshape(1, MOE_TOP_K * n_all)
    xs_lo = _sc_scatter(cat(1, 0), pos_flat, n_rows)
    xs_hi = _sc_scatter(cat(2, 0), pos_flat, n_rows)
    ys_lo, ys_hi = _ffn(block_expert, n_used, xs_lo, xs_hi, wup, bup, wdn, bdn)
    og_lo, og_hi = _sc_gather(ys_lo, pos_flat), _sc_gather(ys_hi, pos_flat)
    starts = np.cumsum([0] + sizes[:-1]).tolist()
    return [_combine(g[0], og_lo, og_hi, g[3], g2, b2, r0) for g, r0 in zip(groups, starts)]


def _mixer(x, pos_tab, tab_period, s0, rows_per_stream, ret_rows, dsa_keys, weights, cnt0):
    wp, wr, wd, wo, g1, b1, rwt, rb = weights
    n = x.shape[0]
    n_streams = n // rows_per_stream
    rq, rk, rv, rg, aq, ak, av, iq, ikw, gr, ga = _project(x, wp, pos_tab, tab_period)
    ret, s_new = _retention(rq, rk, rv, rg, s0, rows_per_stream, ret_rows)
    if dsa_keys is None:
        per_stream = lambda a: a.reshape(n_streams, rows_per_stream, a.shape[1])
        od = _dsa(aq, iq, ikw, per_stream(ak), per_stream(av), per_stream(ikw), DSA_TQ, True, rows_per_stream)
    else:
        kk, vv, ikk, limit = dsa_keys(ak, av, ikw)
        od = _dsa(aq, iq, ikw, kk, vv, ikk, rows_per_stream, False, limit)
    h1, h_lo, h_hi, gtm, ek, rk_, totals = _merge(x, ret, od, gr, ga, wr, wd, wo, g1, b1, rwt, rb, cnt0)
    return (h1, h_lo, h_hi, gtm, ek, rk_), totals, (s_new, ak, av, ikw[:, :IDX_DIM])


def kernel(x_prompt, x_sample, state_ret, cache_k, cache_v, cache_idx_k, w_in, w_ret_o, w_dsa_o, w_o,
           ln1_g, ln1_b, router_w, router_b, w_up, b_up, w_down, b_down, ln2_g, ln2_b):
    assert w_in.shape[0] == DEPTH
    batch, seq, _ = x_prompt.shape
    dec_batch, dec_seq, _ = x_sample.shape
    past = cache_k.shape[2]
    assert seq % PROJ_TM == 0 and seq % DSA_TQ == 0 and PROJ_TM % dec_seq == 0

    l = 0
    mixer_w = (_pack_w_in(w_in[l]), w_ret_o[l].astype(BF16), w_dsa_o[l].astype(BF16), w_o[l].astype(BF16),
               ln1_g[l][None, :], ln1_b[l][None, :], router_w[l].T, router_b[l][:, None])
    moe_w = (
        _deinterleave_w_up(w_up[l]),
        b_up[l].reshape(N_EXPERTS, 2 * D_FF // UP_BLOCK, LANES, 2).transpose(0, 1, 3, 2).reshape(N_EXPERTS, 1, 2 * D_FF),
        w_down[l].astype(BF16), b_down[l][:, None, :], ln2_g[l][None, :], ln2_b[l][None, :])

    tab_p = _rot_tables(jnp.arange(seq))
    zeros_state = jnp.zeros((batch, RET_HEADS, RET_DK, RET_DV), F32)
    moe_p, totals_p, (s_p, k_p, v_p, ik_p) = _mixer(
        x_prompt.reshape(batch * seq, D_MODEL), tab_p, seq // PROJ_TM, zeros_state, seq, RET_CHUNK, None, mixer_w,
        jnp.zeros((N_EXPERTS, LANES), F32))

    n_keys_real = past + dec_seq
    n_keys = -(-n_keys_real // LANES) * LANES
    tab_s = jnp.tile(_rot_tables(past + jnp.arange(dec_seq)), (PROJ_TM // dec_seq, 1))

    def sample_keys(ak, av, ikw):
        def cat(cache, new, width):
            padz = jnp.zeros((dec_batch, n_keys - n_keys_real, width), F32)
            return jnp.concatenate([cache.reshape(dec_batch, past, width), new.reshape(dec_batch, dec_seq, width), padz],
                                   axis=1)
        return (cat(cache_k[l], ak, LANES), cat(cache_v[l], av, LANES),
                cat(cache_idx_k[l], ikw[:, :IDX_DIM], IDX_DIM), n_keys_real)

    moe_s, totals, (s_s, k_s, v_s, ik_s) = _mixer(
        x_sample.reshape(dec_batch * dec_seq, D_MODEL), tab_s, 1, state_ret[l], dec_seq, dec_seq, sample_keys, mixer_w,
        totals_p)

    y_p, y_s = _moe([moe_p, moe_s], totals, *moe_w)

    kv = (DSA_KV_HEADS, DSA_HEAD_DIM)
    return (y_p.reshape(batch, seq, D_MODEL), y_s.reshape(dec_batch, dec_seq, D_MODEL),
            s_p[None], k_p.reshape(1, batch, seq, *kv), v_p.reshape(1, batch, seq, *kv),
            ik_p.reshape(1, batch, seq, IDX_DIM),
            s_s[None], k_s.reshape(1, dec_batch, dec_seq, *kv), v_s.reshape(1, dec_batch, dec_seq, *kv),
            ik_s.reshape(1, dec_batch, dec_seq, IDX_DIM))
```

```python
import functools

import numpy as np
import jax
import jax.numpy as jnp
from jax import lax
from jax.experimental import pallas as pl
from jax.experimental.pallas import tpu as pltpu
from jax.experimental.pallas import tpu_sc as plsc

F32 = jnp.float32
BF16 = jnp.bfloat16
I32 = jnp.int32

D_MODEL = 1024
CHUNK = 64
RET_HEADS = 4
RET_DK = 128
RET_DV = 256
RET_ROPE_BASE = 10000.0
DSA_HEADS = 8
DSA_KV_HEADS = 2
DSA_HEAD_DIM = 64
IDX_HEADS = 8
IDX_DIM = 64
DSA_TOPK = 256
ROPE_THETA = 500000.0
N_EXPERTS = 32
MOE_TOP_K = 4
D_FF = 1024
SWIGLU_ALPHA = 1.702
SWIGLU_LIMIT = 7.0
LN_EPS = 1e-5
GN_EPS = 1e-6
DEPTH = 1
DEEPNORM_ALPHA = (2.0 * DEPTH) ** 0.25
PROJ_WIDTHS = (RET_HEADS * RET_DK, RET_HEADS * RET_DK, RET_HEADS * RET_DV, RET_HEADS * RET_DV,
               DSA_HEADS * DSA_HEAD_DIM, DSA_KV_HEADS * DSA_HEAD_DIM, DSA_KV_HEADS * DSA_HEAD_DIM,
               IDX_HEADS * IDX_DIM, IDX_DIM, IDX_HEADS, D_MODEL, D_MODEL)

LANES = 128
MIB = 1024 * 1024

OFF_RQ, OFF_RK, OFF_RV, OFF_RG = 0, 512, 1024, 2048
OFF_AQ, OFF_AK, OFF_AV, OFF_IQ, OFF_IKW = 3072, 3584, 3712, 3840, 4352
OFF_GR, OFF_GA, PACKED_COLS = 4480, 5504, 6528
TAB_COLS = 8 * LANES

PROJ_TM = 256
RET_CHUNK = 128
DSA_TQ = 256
SEARCH_UNROLL = 4
MERGE_TM = 512
FFN_BLOCK = 256
SC_WINDOW = 128
QUARTER = D_MODEL // 4
UP_BLOCK = 2 * LANES


def _nt(a, b):
    return lax.dot_general(a, b, (((1,), (1,)), ((), ())), preferred_element_type=F32)


def _mm(a, b):
    return jnp.dot(a, b, preferred_element_type=F32)


def _proj_kernel(x_ref, w_ref, tab_ref, rq_ref, rk_ref, rv_ref, rg_ref, aq_ref, ak_ref, av_ref,
                 iq_ref, ikw_ref, gr_ref, ga_ref):
    xb = x_ref[...].astype(BF16)

    def mm(c0, n):
        return _mm(xb, w_ref[:, c0:c0 + n])

    def tab(i):
        return tab_ref[:, i * LANES:(i + 1) * LANES]

    def rot_full(z):
        return z * tab(0) + pltpu.roll(z, 64, 1) * tab(1)

    def rot_part(z, c):
        return z * tab(c) + pltpu.roll(z, LANES - 8, 1) * tab(c + 1) + pltpu.roll(z, 8, 1) * tab(c + 2)

    z = mm(OFF_RQ, 512)
    for h in range(4):
        sl = slice(h * LANES, (h + 1) * LANES)
        rq_ref[:, sl] = rot_full(z[:, sl]).astype(BF16)
    z = mm(OFF_RK, 512)
    for h in range(4):
        sl = slice(h * LANES, (h + 1) * LANES)
        rk_ref[:, sl] = (rot_full(z[:, sl]) * (RET_DK ** -0.5)).astype(BF16)
    for c in range(2):
        rv_ref[:, c * 512:(c + 1) * 512] = mm(OFF_RV + c * 512, 512).astype(BF16)
    for c in range(2):
        rg_ref[:, c * 512:(c + 1) * 512] = mm(OFF_RG + c * 512, 512)
    z = mm(OFF_AQ, 512)
    for h in range(4):
        sl = slice(h * LANES, (h + 1) * LANES)
        aq_ref[:, sl] = rot_part(z[:, sl], 2).astype(BF16)
    z = mm(OFF_AK, 256)
    ak_ref[...] = rot_part(z[:, :LANES], 2)
    av_ref[...] = z[:, LANES:]
    z = mm(OFF_IQ, 512)
    for h in range(4):
        sl = slice(h * LANES, (h + 1) * LANES)
        iq_ref[:, sl] = rot_part(z[:, sl], 2).astype(BF16)
    ikw_ref[...] = rot_part(mm(OFF_IKW, LANES), 5)
    for c in range(2):
        gr_ref[:, c * 512:(c + 1) * 512] = mm(OFF_GR + c * 512, 512)
    for c in range(2):
        ga_ref[:, c * 512:(c + 1) * 512] = mm(OFF_GA + c * 512, 512)


def _rot_tables(pos):
    p = pos.shape[0]
    posf = pos.astype(F32)[:, None]
    ret_f = RET_ROPE_BASE ** (-jnp.linspace(0.0, 1.0, RET_DK // 2, dtype=F32))
    ang = posf * ret_f[None, :]
    c, s = jnp.cos(ang), jnp.sin(ang)
    cos_r = jnp.concatenate([c, c], 1)
    sin_r = jnp.concatenate([-s, s], 1)
    n_rot = DSA_HEAD_DIM // 4
    att_f = ROPE_THETA ** (-jnp.arange(0, n_rot, 2, dtype=F32) / n_rot)
    ang2 = posf * att_f[None, :]
    c2, s2 = jnp.cos(ang2), jnp.sin(ang2)
    half = n_rot // 2
    rest = DSA_HEAD_DIM - 2 * half
    c64 = jnp.concatenate([c2, c2, jnp.ones((p, rest), F32)], 1)
    s1_64 = jnp.concatenate([-s2, jnp.zeros((p, DSA_HEAD_DIM - half), F32)], 1)
    s2_64 = jnp.concatenate([jnp.zeros((p, half), F32), s2, jnp.zeros((p, rest), F32)], 1)
    z64 = jnp.zeros((p, DSA_HEAD_DIM), F32)
    ci = jnp.concatenate([c64, jnp.full((p, IDX_HEADS), IDX_HEADS ** -0.5, F32),
                          jnp.zeros((p, DSA_HEAD_DIM - IDX_HEADS), F32)], 1)
    return jnp.concatenate([cos_r, sin_r,
                            jnp.concatenate([c64, c64], 1), jnp.concatenate([s1_64, s1_64], 1),
                            jnp.concatenate([s2_64, s2_64], 1),
                            ci, jnp.concatenate([s1_64, z64], 1), jnp.concatenate([s2_64, z64], 1)], 1)


def _pack_w_in(w):
    cuts = np.cumsum(PROJ_WIDTHS)[:-1].tolist()
    rq, rk, rv, rg, aq, ak, av, iq, ik, iw, gr, ga = jnp.split(w, cuts, axis=1)
    pad = jnp.zeros((w.shape[0], LANES - IDX_DIM - IDX_HEADS), w.dtype)
    return jnp.concatenate([rq, rk, rv, rg, aq, ak, av, iq, ik, iw, pad, gr, ga], axis=1).astype(BF16)


def _project(x, wp, tab, tab_period):
    n = x.shape[0]
    tm = PROJ_TM
    row = lambda w: pl.BlockSpec((tm, w), lambda i: (i, 0))
    out_shapes = [((n, 512), BF16), ((n, 512), BF16), ((n, 1024), BF16), ((n, 1024), F32),
                  ((n, 512), BF16), ((n, LANES), F32), ((n, LANES), F32), ((n, 512), BF16),
                  ((n, LANES), F32), ((n, 1024), F32), ((n, 1024), F32)]
    return pl.pallas_call(
        _proj_kernel,
        grid=(n // tm,),
        in_specs=[row(D_MODEL),
                  pl.BlockSpec((D_MODEL, PACKED_COLS), lambda i: (0, 0), pipeline_mode=pl.Buffered(1)),
                  pl.BlockSpec((tm, TAB_COLS), lambda i: (i % tab_period, 0))],
        out_specs=[row(s[1]) for s, _ in out_shapes],
        out_shape=[jax.ShapeDtypeStruct(s, d) for s, d in out_shapes],
        compiler_params=pltpu.CompilerParams(dimension_semantics=("parallel",), vmem_limit_bytes=44 * MIB),
        name="proj",
    )(x, wp, tab)


def _ret_kernel(dec_ref, xi_ref, zeta_ref, rq_ref, rk_ref, rv_ref, rg_ref, s0_ref, ret_ref, sout_ref, st_ref,
                *, rows, n_chunk, g_pow):
    j = pl.program_id(1)
    cpad = RET_CHUNK

    @pl.when(j == 0)
    def _():
        st_ref[...] = s0_ref[0]

    def padded(v):
        if rows == cpad:
            return v
        return jnp.concatenate([v, jnp.zeros((cpad - rows, v.shape[1]), v.dtype)], axis=0)

    for c in range(n_chunk):
        rs = slice(c * rows, (c + 1) * rows)
        for h in range(RET_HEADS):
            ks = slice(h * RET_DK, (h + 1) * RET_DK)
            vs = slice(h * RET_DV, (h + 1) * RET_DV)
            q = padded(rq_ref[rs, ks])
            kt = padded(rk_ref[rs, ks].astype(F32)).T
            v = padded(rv_ref[rs, vs])
            s = st_ref[h]
            sc = _mm(q, kt.astype(BF16)) * dec_ref[h]
            o = _mm(sc.astype(BF16), v) + _mm(q, s.astype(BF16)) * xi_ref[h]
            st_ref[h] = g_pow[h] * s + _mm((kt * zeta_ref[h]).astype(BF16), v)
            o = o[:rows]
            mu = jnp.mean(o, axis=-1, keepdims=True)
            d = o - mu
            var = jnp.mean(d * d, axis=-1, keepdims=True)
            gn = d * lax.rsqrt(var + GN_EPS)
            g = rg_ref[rs, vs]
            ret_ref[rs, vs] = (gn * (g * jax.nn.sigmoid(g))).astype(BF16)

    @pl.when(j == pl.num_programs(1) - 1)
    def _():
        sout_ref[0] = st_ref[...]


def _retention(rq, rk, rv, rg, s0, rows_per_stream, rows):
    n = rq.shape[0]
    n_streams = n // rows_per_stream
    n_chunk = min(rows_per_stream // rows, 2)
    blk = rows * n_chunk
    nb = rows_per_stream // blk
    gam = 1.0 - 2.0 ** (-5.0 - np.arange(RET_HEADS, dtype=np.float64))
    i = np.arange(RET_CHUNK, dtype=np.float64)
    diff = i[:, None] - i[None, :]
    dec = np.where(diff >= 0, gam[:, None, None] ** np.maximum(diff, 0.0)[None], 0.0)
    xi = gam[:, None, None] ** (i + 1.0)[None, :, None]
    zeta = np.where(i < rows, gam[:, None, None] ** (rows - 1.0 - i)[None, None, :], 0.0)
    g_pow = tuple(float(g ** rows) for g in gam)
    const = lambda shape: pl.BlockSpec(shape, lambda s, j: (0,) * len(shape))
    row = lambda w: pl.BlockSpec((blk, w), lambda s, j: (s * nb + j, 0))
    st = pl.BlockSpec((1, RET_HEADS, RET_DK, RET_DV), lambda s, j: (s, 0, 0, 0))
    return pl.pallas_call(
        functools.partial(_ret_kernel, rows=rows, n_chunk=n_chunk, g_pow=g_pow),
        grid=(n_streams, nb),
        in_specs=[const((RET_HEADS, RET_CHUNK, RET_CHUNK)), const((RET_HEADS, RET_CHUNK, 1)),
                  const((RET_HEADS, 1, RET_CHUNK)), row(512), row(512), row(1024), row(1024), st],
        out_specs=[row(1024), st],
        out_shape=[jax.ShapeDtypeStruct((n, RET_HEADS * RET_DV), BF16),
                   jax.ShapeDtypeStruct((n_streams, RET_HEADS, RET_DK, RET_DV), F32)],
        scratch_shapes=[pltpu.VMEM((RET_HEADS, RET_DK, RET_DV), F32)],
        compiler_params=pltpu.CompilerParams(dimension_semantics=("parallel", "arbitrary"),
                                             vmem_limit_bytes=32 * MIB),
        name="retention",
    )(jnp.asarray(dec, F32), jnp.asarray(xi, F32), jnp.asarray(zeta, F32), rq, rk, rv, rg, s0)


def _dsa_kernel(aq_ref, iq_ref, ikwq_ref, kk_ref, vv_ref, ikk_ref, o_ref, w_ref, bias_ref,
                *, tq, n_keys, row0, limit_const, n_sel):
    nsel_f = float(n_sel)
    neg_inf = -jnp.inf

    ikb = ikk_ref[0][:, :IDX_DIM].astype(BF16)
    iww = ikwq_ref[...][:, IDX_DIM:IDX_DIM + IDX_HEADS] * (IDX_DIM ** -0.5)
    acc = jnp.zeros((tq, n_keys), F32)
    for p in range(IDX_HEADS // 2):
        slab = iq_ref[:, p * LANES:(p + 1) * LANES]
        for hh in range(2):
            h = 2 * p + hh
            s = _nt(slab[:, hh * IDX_DIM:(hh + 1) * IDX_DIM], ikb)
            acc = acc + jnp.maximum(s, 0.0) * iww[:, h:h + 1]

    col = lax.broadcasted_iota(I32, (tq, n_keys), 1)
    if limit_const is None:
        row = lax.broadcasted_iota(I32, (tq, 1), 0) + row0
        limit = (jnp.right_shift(row, 6) + 1) * CHUNK
    else:
        limit = limit_const
    admissible = col < limit
    sc = jnp.where(admissible, acc, neg_inf)

    pos = jnp.sum(jnp.where(sc >= 0.0, 1.0, 0.0), axis=1, keepdims=True) >= nsel_f
    kk = jnp.where(pos, nsel_f, float(n_keys - n_sel + 1))
    w_ref[...] = jnp.where(pos, sc, -sc)

    def bit_step(i, u):
        cand_u = u | jnp.left_shift(jnp.int32(1), 30 - i)
        cand = pltpu.bitcast(cand_u, F32)
        cnt = jnp.sum(jnp.where(w_ref[...] >= cand, 1.0, 0.0), axis=1, keepdims=True)
        return jnp.where(cnt >= kk, cand_u, u)

    mag_u = lax.fori_loop(0, 31, bit_step, jnp.zeros((tq, 1), I32), unroll=SEARCH_UNROLL)
    mag = pltpu.bitcast(mag_u, F32)
    thr = jnp.where(pos, mag, -mag)

    sc = jnp.where(pos, w_ref[...], -w_ref[...])
    short = jnp.sum(jnp.where(sc >= thr, 1.0, 0.0), axis=1, keepdims=True) < nsel_f
    thr = jnp.where(jnp.logical_and(short, jnp.logical_not(pos)), -pltpu.bitcast(mag_u + 1, F32), thr)
    ge = sc >= thr
    cnt_gt = jnp.sum(jnp.where(sc > thr, 1.0, 0.0), axis=1, keepdims=True)
    cnt_ge = jnp.sum(jnp.where(ge, 1.0, 0.0), axis=1, keepdims=True)
    bias_ref[...] = jnp.where(jnp.logical_and(ge, admissible), 0.0, neg_inf)
    excess = jnp.logical_and(cnt_ge > nsel_f, thr > neg_inf)

    @pl.when(jnp.max(jnp.where(excess, 1.0, 0.0)) > 0.0)
    def _():
        need = nsel_f - cnt_gt
        tri = jnp.where(lax.broadcasted_iota(I32, (LANES, LANES), 0) < lax.broadcasted_iota(I32, (LANES, LANES), 1),
                        1.0, 0.0).astype(BF16)
        before = jnp.zeros((tq, 1), F32)
        for b in range(n_keys // LANES):
            sl = slice(b * LANES, (b + 1) * LANES)
            sblk = jnp.where(pos, w_ref[:, sl], -w_ref[:, sl])
            eq = jnp.where(sblk == thr, 1.0, 0.0)
            rank = _mm(eq.astype(BF16), tri) + before
            keep = jnp.logical_or(sblk > thr, jnp.logical_and(sblk == thr, rank < need))
            bias_ref[:, sl] = jnp.where(jnp.logical_and(keep, sblk > neg_inf), 0.0, neg_inf)
            before = before + jnp.sum(eq, axis=1, keepdims=True)

    kfull = kk_ref[0]
    vfull = vv_ref[0]
    for g in range(DSA_KV_HEADS):
        gs = slice(g * DSA_HEAD_DIM, (g + 1) * DSA_HEAD_DIM)
        kg = kfull[:, gs].astype(BF16)
        vg = vfull[:, gs].astype(BF16)
        for pp in range(2):
            p = 2 * g + pp
            slab = (aq_ref[:, p * LANES:(p + 1) * LANES].astype(F32) * (DSA_HEAD_DIM ** -0.5)).astype(BF16)
            outs = []
            for hh in range(2):
                lg = _nt(slab[:, hh * DSA_HEAD_DIM:(hh + 1) * DSA_HEAD_DIM], kg) + bias_ref[...]
                m = jnp.max(lg, axis=1, keepdims=True)
                pr = jnp.exp(lg - m)
                den = jnp.sum(pr, axis=1, keepdims=True)
                outs.append(_mm(pr.astype(BF16), vg) / den)
            o_ref[:, p * LANES:(p + 1) * LANES] = jnp.concatenate(outs, axis=1).astype(BF16)


def _dsa_call(aq, iq, ikw, kk, vv, ikk, tq, nq, jq, n_keys, limit_const, n_sel, name):
    n_streams = kk.shape[0]
    qrow = lambda w: pl.BlockSpec((tq, w), lambda s: (s * nq + jq, 0))
    krow = lambda a: pl.BlockSpec((1, n_keys, a.shape[2]), lambda s: (s, 0, 0))
    return pl.pallas_call(
        functools.partial(_dsa_kernel, tq=tq, n_keys=n_keys, row0=jq * tq, limit_const=limit_const, n_sel=n_sel),
        grid=(n_streams,),
        in_specs=[qrow(512), qrow(512), qrow(LANES), krow(kk), krow(vv), krow(ikk)],
        out_specs=pl.BlockSpec((tq, DSA_HEADS * DSA_HEAD_DIM), lambda s: (s, 0)),
        out_shape=jax.ShapeDtypeStruct((n_streams * tq, DSA_HEADS * DSA_HEAD_DIM), BF16),
        scratch_shapes=[pltpu.VMEM((tq, n_keys), F32), pltpu.VMEM((tq, n_keys), F32)],
        compiler_params=pltpu.CompilerParams(dimension_semantics=("parallel",), vmem_limit_bytes=48 * MIB),
        name=name,
    )(aq, iq, ikw, kk, vv, ikk)


def _dsa(aq, iq, ikw, kk, vv, ikk, tq, chunk_causal, limit_const):
    n_streams, n_keys, _ = kk.shape
    n = aq.shape[0]
    nq = n // n_streams // tq
    n_sel = min(DSA_TOPK, limit_const // 4)
    if not chunk_causal:
        assert nq == 1
        return _dsa_call(aq, iq, ikw, kk, vv, ikk, tq, 1, 0, n_keys, limit_const, n_sel, "dsa_s")
    parts = [_dsa_call(aq, iq, ikw, kk, vv, ikk, tq, nq, jq, (jq + 1) * tq, None, n_sel, f"dsa_p{jq}")
             for jq in range(nq)]
    width = parts[0].shape[1]
    return jnp.stack(parts, 0).reshape(nq, n_streams, tq, width).transpose(1, 0, 2, 3).reshape(n, width)


def _pack_rows(v):
    q = QUARTER
    bits = lambda x: pltpu.bitcast(x.astype(BF16).astype(F32), I32)
    pair = lambda c: lax.shift_right_logical(bits(v[:, c * q:(c + 1) * q]), 16) | bits(v[:, (c + 1) * q:(c + 2) * q])
    return pair(0), pair(2)


def _unpack_rows(lo, hi):
    parts = []
    for w in (lo, hi):
        parts.append(pltpu.bitcast(lax.shift_left(w, 16), F32))
        parts.append(pltpu.bitcast(w & jnp.int32(-65536), F32))
    return jnp.concatenate(parts, axis=1)


def _layer_norm(v, g, b):
    mu = jnp.mean(v, axis=-1, keepdims=True)
    d = v - mu
    var = jnp.mean(d * d, axis=-1, keepdims=True)
    return d * lax.rsqrt(var + LN_EPS) * g + b


def _merge_kernel(x_ref, ret_ref, od_ref, gr_ref, ga_ref, wr_ref, wd_ref, wo_ref, g1_ref, b1_ref, rwt_ref, rb_ref,
                  tri_ref, cnt0_ref, h1_ref, hlo_ref, hhi_ref, gtm_ref, ek_ref, rk_ref, tot_ref, cnt_ref):
    @pl.when(pl.program_id(0) == 0)
    def _():
        cnt_ref[...] = cnt0_ref[:, 0:1]

    y_ret = _mm(ret_ref[...], wr_ref[...])
    y_dsa = _mm(od_ref[...], wd_ref[...])
    merged = jax.nn.sigmoid(gr_ref[...]) * y_ret + jax.nn.sigmoid(ga_ref[...]) * y_dsa
    mix = _mm(merged.astype(BF16), wo_ref[...])
    h1 = _layer_norm(DEEPNORM_ALPHA * x_ref[...] + mix, g1_ref[...], b1_ref[...])
    h1_ref[...] = h1
    hlo_ref[...], hhi_ref[...] = _pack_rows(h1)

    logits = lax.dot_general(rwt_ref[...], h1, (((1,), (1,)), ((), ())), preferred_element_type=F32,
                             precision=lax.Precision.HIGHEST) + rb_ref[...]
    tm = logits.shape[1]
    e_iota = lax.broadcasted_iota(I32, (N_EXPERTS, tm), 0)
    tops, hots, firsts = [], [], []
    for _ in range(MOE_TOP_K):
        m = jnp.max(logits, axis=0, keepdims=True)
        first = jnp.min(jnp.where(logits == m, e_iota, N_EXPERTS), axis=0, keepdims=True)
        hot = e_iota == first
        tops.append(m)
        hots.append(hot)
        firsts.append(first)
        logits = jnp.where(hot, -jnp.inf, logits)
    exps = [jnp.exp(m - tops[0]) for m in tops]
    den = exps[0] + exps[1] + exps[2] + exps[3]
    sel = jnp.zeros((N_EXPERTS, tm), F32)
    for hot in hots:
        sel = sel + jnp.where(hot, 1.0, 0.0)
    rank = _mm(sel.astype(BF16), tri_ref[...]) + cnt_ref[...]
    ranks = [jnp.sum(jnp.where(hot, rank, 0.0), axis=0, keepdims=True).astype(I32) for hot in hots]
    pad_i = jnp.zeros((8 - MOE_TOP_K, tm), I32)
    ek_ref[...] = jnp.concatenate(firsts + [pad_i], axis=0)
    rk_ref[...] = jnp.concatenate(ranks + [pad_i], axis=0)
    gates = jnp.concatenate([e / den for e in exps] + [jnp.zeros((LANES - MOE_TOP_K, tm), F32)], axis=0)
    gtm_ref[...] = gates.T
    cnt_ref[...] = cnt_ref[...] + jnp.sum(sel, axis=1, keepdims=True)
    tot_ref[...] = jnp.broadcast_to(cnt_ref[...], tot_ref.shape)


def _merge(x, ret, od, gr, ga, wr, wd, wo, g1, b1, rwt, rb, cnt0):
    n = x.shape[0]
    tm = MERGE_TM
    row = lambda w: pl.BlockSpec((tm, w), lambda i: (i, 0))
    const = lambda a: pl.BlockSpec(a.shape, lambda i: (0,) * a.ndim)
    col = pl.BlockSpec((8, tm), lambda i: (0, i))
    tri = jnp.asarray(np.triu(np.ones((tm, tm), np.float32), 1), BF16)
    return pl.pallas_call(
        _merge_kernel,
        grid=(n // tm,),
        in_specs=[row(D_MODEL), row(1024), row(512), row(1024), row(1024), const(wr), const(wd), const(wo),
                  const(g1), const(b1), const(rwt), const(rb), const(tri), const(cnt0)],
        out_specs=[row(D_MODEL), row(QUARTER), row(QUARTER), row(LANES), col, col,
                   pl.BlockSpec((N_EXPERTS, LANES), lambda i: (0, 0))],
        out_shape=[jax.ShapeDtypeStruct((n, D_MODEL), F32), jax.ShapeDtypeStruct((n, QUARTER), I32),
                   jax.ShapeDtypeStruct((n, QUARTER), I32), jax.ShapeDtypeStruct((n, LANES), F32),
                   jax.ShapeDtypeStruct((8, n), I32), jax.ShapeDtypeStruct((8, n), I32),
                   jax.ShapeDtypeStruct((N_EXPERTS, LANES), F32)],
        scratch_shapes=[pltpu.VMEM((N_EXPERTS, 1), F32)],
        compiler_params=pltpu.CompilerParams(dimension_semantics=("arbitrary",), vmem_limit_bytes=48 * MIB),
        name="merge",
    )(x, ret, od, gr, ga, wr, wd, wo, g1, b1, rwt, rb, tri, cnt0)


def _deinterleave_kernel(w_ref, o_ref):
    r = lax.broadcasted_iota(I32, (UP_BLOCK, UP_BLOCK), 0)
    c = lax.broadcasted_iota(I32, (UP_BLOCK, UP_BLOCK), 1)
    src = jnp.where(c < LANES, 2 * c, 2 * (c - LANES) + 1)
    perm = jnp.where(r == src, 1.0, 0.0).astype(BF16)
    for b in range(w_ref.shape[2] // UP_BLOCK):
        sl = slice(b * UP_BLOCK, (b + 1) * UP_BLOCK)
        o_ref[0, :, sl] = _mm(w_ref[0, :, sl].astype(BF16), perm).astype(BF16)


def _deinterleave_w_up(w_up):
    n_e, d_in, d_out = w_up.shape
    cols = 512
    spec = pl.BlockSpec((1, d_in, cols), lambda e, c: (e, 0, c))
    return pl.pallas_call(
        _deinterleave_kernel,
        grid=(n_e, d_out // cols),
        in_specs=[spec],
        out_specs=spec,
        out_shape=jax.ShapeDtypeStruct(w_up.shape, BF16),
        compiler_params=pltpu.CompilerParams(dimension_semantics=("parallel", "parallel"),
                                             vmem_limit_bytes=24 * MIB),
        name="w_up_prep",
    )(w_up)


def _pos_kernel(off_ref, ek_ref, rk_ref, pos_ref):
    ek = ek_ref[...]
    pos = rk_ref[...]
    for e in range(N_EXPERTS):
        pos = pos + jnp.where(ek == e, off_ref[e], 0)
    pos_ref[...] = pos


def _positions(off, ek, rk):
    n = ek.shape[1]
    tn = min(n, 2048)
    spec = pl.BlockSpec((8, tn), lambda i, off: (0, i))
    return pl.pallas_call(
        _pos_kernel,
        grid_spec=pltpu.PrefetchScalarGridSpec(num_scalar_prefetch=1, grid=(n // tn,), in_specs=[spec, spec],
                                               out_specs=spec),
        out_shape=jax.ShapeDtypeStruct((8, n), I32),
        name="moe_pos",
    )(off, ek, rk)


def _sc_mesh():
    return plsc.VectorSubcoreMesh(core_axis_name="core", subcore_axis_name="subcore")


def _sc_scatter(x, pos_flat, n_rows):
    n = x.shape[0]
    nw = n // SC_WINDOW

    @functools.partial(pl.kernel, out_type=jax.ShapeDtypeStruct((n_rows, x.shape[1]), x.dtype), mesh=_sc_mesh())
    def scatter(x_hbm, p_hbm, o_hbm):
        def body(x_vmem, *p_vmem):
            for p in p_vmem:
                pltpu.sync_copy(x_vmem, o_hbm.at[p.at[0]])

        pltpu.emit_pipeline(
            body, grid=(nw,),
            in_specs=[pl.BlockSpec((SC_WINDOW, x.shape[1]), lambda i: (i, 0))] +
                     [pl.BlockSpec((1, SC_WINDOW), functools.partial(lambda k, i: (0, k * nw + i), k))
                      for k in range(MOE_TOP_K)],
            out_specs=[], core_axis_name=("core", "subcore"), dimension_semantics=(pltpu.PARALLEL,),
        )(x_hbm, *([p_hbm] * MOE_TOP_K))

    return scatter(x, pos_flat)


def _sc_gather(y, pos_flat):
    m = pos_flat.shape[1]

    @functools.partial(pl.kernel, out_type=jax.ShapeDtypeStruct((m, y.shape[1]), y.dtype), mesh=_sc_mesh())
    def gather(y_hbm, p_hbm, o_hbm):
        def body(p_vmem, o_vmem):
            pltpu.sync_copy(y_hbm.at[p_vmem.at[0]], o_vmem)

        pltpu.emit_pipeline(
            body, grid=(m // SC_WINDOW,),
            in_specs=[pl.BlockSpec((1, SC_WINDOW), lambda i: (0, i))],
            out_specs=[pl.BlockSpec((SC_WINDOW, y.shape[1]), lambda i: (i, 0))],
            core_axis_name=("core", "subcore"), dimension_semantics=(pltpu.PARALLEL,),
        )(p_hbm, o_hbm)

    return gather(y, pos_flat)


def _ffn_kernel(be_ref, nu_ref, xlo_ref, xhi_ref, wup_ref, bup_ref, wdn_ref, bdn_ref, ylo_ref, yhi_ref):
    @pl.when(pl.program_id(0) < nu_ref[0])
    def _():
        x = _unpack_rows(xlo_ref[...], xhi_ref[...]).astype(BF16)
        h = _mm(x, wup_ref[0]) + bup_ref[0]
        acts = []
        for b in range(2 * D_FF // UP_BLOCK):
            glu = jnp.minimum(h[:, b * UP_BLOCK:b * UP_BLOCK + LANES], SWIGLU_LIMIT)
            lin = jnp.clip(h[:, b * UP_BLOCK + LANES:(b + 1) * UP_BLOCK], -SWIGLU_LIMIT, SWIGLU_LIMIT)
            acts.append(glu * jax.nn.sigmoid(SWIGLU_ALPHA * glu) * (lin + 1.0))
        act = jnp.concatenate(acts, axis=1)
        ylo_ref[...], yhi_ref[...] = _pack_rows(_mm(act.astype(BF16), wdn_ref[0]) + bdn_ref[0])


def _ffn(block_expert, n_used, xs_lo, xs_hi, wup, bup, wdn, bdn):
    rows = xs_lo.shape[0]
    blk = FFN_BLOCK
    row = pl.BlockSpec((blk, QUARTER), lambda i, be, nu: (i, 0))
    per_expert = lambda a: pl.BlockSpec((1,) + a.shape[1:], lambda i, be, nu: (be[i], 0, 0))
    return pl.pallas_call(
        _ffn_kernel,
        grid_spec=pltpu.PrefetchScalarGridSpec(
            num_scalar_prefetch=2, grid=(rows // blk,),
            in_specs=[row, row, per_expert(wup), per_expert(bup), per_expert(wdn), per_expert(bdn)],
            out_specs=[row, row]),
        out_shape=[jax.ShapeDtypeStruct((rows, QUARTER), I32), jax.ShapeDtypeStruct((rows, QUARTER), I32)],
        compiler_params=pltpu.CompilerParams(dimension_semantics=("arbitrary",), vmem_limit_bytes=40 * MIB),
        name="moe_ffn",
    )(block_expert, n_used, xs_lo, xs_hi, wup, bup, wdn, bdn)


def _combine_kernel(h1_ref, olo_ref, ohi_ref, gtm_ref, g2_ref, b2_ref, o_ref):
    g = gtm_ref[...]
    y = jnp.zeros(h1_ref.shape, F32)
    for k in range(MOE_TOP_K):
        y = y + g[:, k:k + 1] * _unpack_rows(olo_ref[k], ohi_ref[k])
    o_ref[...] = _layer_norm(DEEPNORM_ALPHA * h1_ref[...] + y, g2_ref[...], b2_ref[...])


def _combine(h1, og_lo, og_hi, gtm, g2, b2, row0):
    n = h1.shape[0]
    n_all = og_lo.shape[0] // MOE_TOP_K
    tm = MERGE_TM
    row = lambda w: pl.BlockSpec((tm, w), lambda i: (i, 0))
    const = lambda a: pl.BlockSpec(a.shape, lambda i: (0,) * a.ndim)
    picked = pl.BlockSpec((MOE_TOP_K, tm, QUARTER), lambda i: (0, i + row0 // tm, 0))
    return pl.pallas_call(
        _combine_kernel,
        grid=(n // tm,),
        in_specs=[row(D_MODEL), picked, picked, row(LANES), const(g2), const(b2)],
        out_specs=row(D_MODEL),
        out_shape=jax.ShapeDtypeStruct((n, D_MODEL), F32),
        compiler_params=pltpu.CompilerParams(dimension_semantics=("parallel",), vmem_limit_bytes=40 * MIB),
        name="moe_combine",
    )(h1, og_lo.reshape(MOE_TOP_K, n_all, QUARTER), og_hi.reshape(MOE_TOP_K, n_all, QUARTER), gtm, g2, b2)


def _moe(groups, totals, wup, bup, wdn, bdn, g2, b2):
    blk = FFN_BLOCK
    sizes = [g[0].shape[0] for g in groups]
    n_all = sum(sizes)
    n_rows = MOE_TOP_K * n_all + N_EXPERTS * blk
    counts = totals[:, 0].astype(I32)
    padded = (counts + blk - 1) // blk * blk
    ends = jnp.cumsum(padded)
    block_start = jnp.arange(n_rows // blk, dtype=I32) * blk
    block_expert = jnp.minimum(jnp.sum((ends[None, :] <= block_start[:, None]).astype(I32), axis=1), N_EXPERTS - 1)
    n_used = (ends[-1:] // blk).astype(I32)
    cat = lambda i, axis: jnp.concatenate([g[i] for g in groups], axis=axis)
    pos = _positions((ends - padded).astype(I32), cat(4, 1), cat(5, 1))
    pos_flat = pos[:MOE_TOP_K].reshape(1, MOE_TOP_K * n_all)
    xs_lo = _sc_scatter(cat(1, 0), pos_flat, n_rows)
    xs_hi = _sc_scatter(cat(2, 0), pos_flat, n_rows)
    ys_lo, ys_hi = _ffn(block_expert, n_used, xs_lo, xs_hi, wup, bup, wdn, bdn)
    og_lo, og_hi = _sc_gather(ys_lo, pos_flat), _sc_gather(ys_hi, pos_flat)
    starts = np.cumsum([0] + sizes[:-1]).tolist()
    return [_combine(g[0], og_lo, og_hi, g[3], g2, b2, r0) for g, r0 in zip(groups, starts)]


def _mixer(x, pos_tab, tab_period, s0, rows_per_stream, ret_rows, dsa_keys, weights, cnt0):
    wp, wr, wd, wo, g1, b1, rwt, rb = weights
    n = x.shape[0]
    n_streams = n // rows_per_stream
    rq, rk, rv, rg, aq, ak, av, iq, ikw, gr, ga = _project(x, wp, pos_tab, tab_period)
    ret, s_new = _retention(rq, rk, rv, rg, s0, rows_per_stream, ret_rows)
    if dsa_keys is None:
        per_stream = lambda a: a.reshape(n_streams, rows_per_stream, a.shape[1])
        od = _dsa(aq, iq, ikw, per_stream(ak), per_stream(av), per_stream(ikw), DSA_TQ, True, rows_per_stream)
    else:
        kk, vv, ikk, limit = dsa_keys(ak, av, ikw)
        od = _dsa(aq, iq, ikw, kk, vv, ikk, rows_per_stream, False, limit)
    h1, h_lo, h_hi, gtm, ek, rk_, totals = _merge(x, ret, od, gr, ga, wr, wd, wo, g1, b1, rwt, rb, cnt0)
    return (h1, h_lo, h_hi, gtm, ek, rk_), totals, (s_new, ak, av, ikw[:, :IDX_DIM])


def kernel(x_prompt, x_sample, state_ret, cache_k, cache_v, cache_idx_k, w_in, w_ret_o, w_dsa_o, w_o,
           ln1_g, ln1_b, router_w, router_b, w_up, b_up, w_down, b_down, ln2_g, ln2_b):
    assert w_in.shape[0] == DEPTH
    batch, seq, _ = x_prompt.shape
    dec_batch, dec_seq, _ = x_sample.shape
    past = cache_k.shape[2]
    assert seq % PROJ_TM == 0 and seq % DSA_TQ == 0 and PROJ_TM % dec_seq == 0

    l = 0
    mixer_w = (_pack_w_in(w_in[l]), w_ret_o[l].astype(BF16), w_dsa_o[l].astype(BF16), w_o[l].astype(BF16),
               ln1_g[l][None, :], ln1_b[l][None, :], router_w[l].T, router_b[l][:, None])
    moe_w = (
        _deinterleave_w_up(w_up[l]),
        b_up[l].reshape(N_EXPERTS, 2 * D_FF // UP_BLOCK, LANES, 2).transpose(0, 1, 3, 2).reshape(N_EXPERTS, 1, 2 * D_FF),
        w_down[l].astype(BF16), b_down[l][:, None, :], ln2_g[l][None, :], ln2_b[l][None, :])

    tab_p = _rot_tables(jnp.arange(seq))
    zeros_state = jnp.zeros((batch, RET_HEADS, RET_DK, RET_DV), F32)
    moe_p, totals_p, (s_p, k_p, v_p, ik_p) = _mixer(
        x_prompt.reshape(batch * seq, D_MODEL), tab_p, seq // PROJ_TM, zeros_state, seq, RET_CHUNK, None, mixer_w,
        jnp.zeros((N_EXPERTS, LANES), F32))

    n_keys_real = past + dec_seq
    n_keys = -(-n_keys_real // LANES) * LANES
    tab_s = jnp.tile(_rot_tables(past + jnp.arange(dec_seq)), (PROJ_TM // dec_seq, 1))

    def sample_keys(ak, av, ikw):
        def cat(cache, new, width):
            padz = jnp.zeros((dec_batch, n_keys - n_keys_real, width), F32)
            return jnp.concatenate([cache.reshape(dec_batch, past, width), new.reshape(dec_batch, dec_seq, width), padz],
                                   axis=1)
        return (cat(cache_k[l], ak, LANES), cat(cache_v[l], av, LANES),
                cat(cache_idx_k[l], ikw[:, :IDX_DIM], IDX_DIM), n_keys_real)

    moe_s, totals, (s_s, k_s, v_s, ik_s) = _mixer(
        x_sample.reshape(dec_batch * dec_seq, D_MODEL), tab_s, 1, state_ret[l], dec_seq, dec_seq, sample_keys, mixer_w,
        totals_p)

    y_p, y_s = _moe([moe_p, moe_s], totals, *moe_w)

    kv = (DSA_KV_HEADS, DSA_HEAD_DIM)
    return (y_p.reshape(batch, seq, D_MODEL), y_s.reshape(dec_batch, dec_seq, D_MODEL),
            s_p[None], k_p.reshape(1, batch, seq, *kv), v_p.reshape(1, batch, seq, *kv),
            ik_p.reshape(1, batch, seq, IDX_DIM),
            s_s[None], k_s.reshape(1, dec_batch, dec_seq, *kv), v_s.reshape(1, dec_batch, dec_seq, *kv),
            ik_s.reshape(1, dec_batch, dec_seq, IDX_DIM))
```

```python
import functools

import numpy as np
import jax
import jax.numpy as jnp
from jax import lax
from jax.experimental import pallas as pl
from jax.experimental.pallas import tpu as pltpu
from jax.experimental.pallas import tpu_sc as plsc

F32 = jnp.float32
BF16 = jnp.bfloat16
I32 = jnp.int32

D_MODEL = 1024
CHUNK = 64
RET_HEADS = 4
RET_DK = 128
RET_DV = 256
RET_ROPE_BASE = 10000.0
DSA_HEADS = 8
DSA_KV_HEADS = 2
DSA_HEAD_DIM = 64
IDX_HEADS = 8
IDX_DIM = 64
DSA_TOPK = 256
ROPE_THETA = 500000.0
N_EXPERTS = 32
MOE_TOP_K = 4
D_FF = 1024
SWIGLU_ALPHA = 1.702
SWIGLU_LIMIT = 7.0
LN_EPS = 1e-5
GN_EPS = 1e-6
DEPTH = 1
DEEPNORM_ALPHA = (2.0 * DEPTH) ** 0.25
PROJ_WIDTHS = (RET_HEADS * RET_DK, RET_HEADS * RET_DK, RET_HEADS * RET_DV, RET_HEADS * RET_DV,
               DSA_HEADS * DSA_HEAD_DIM, DSA_KV_HEADS * DSA_HEAD_DIM, DSA_KV_HEADS * DSA_HEAD_DIM,
               IDX_HEADS * IDX_DIM, IDX_DIM, IDX_HEADS, D_MODEL, D_MODEL)

LANES = 128
MIB = 1024 * 1024

OFF_RQ, OFF_RK, OFF_RV, OFF_RG = 0, 512, 1024, 2048
OFF_AQ, OFF_AK, OFF_AV, OFF_IQ, OFF_IKW = 3072, 3584, 3712, 3840, 4352
OFF_GR, OFF_GA, PACKED_COLS = 4480, 5504, 6528
TAB_COLS = 8 * LANES

PROJ_TM = 256
RET_CHUNK = 128
DSA_TQ = 256
DSA_ROWS = 512
DSA_GROUP_MAX = 4
SEARCH_UNROLL = 4
MERGE_TM = 512
FFN_BLOCK = 256
SC_WINDOW = 128
QUARTER = D_MODEL // 4
UP_BLOCK = 2 * LANES


def _nt(a, b):
    return lax.dot_general(a, b, (((1,), (1,)), ((), ())), preferred_element_type=F32)


def _mm(a, b):
    return jnp.dot(a, b, preferred_element_type=F32)


def _proj_kernel(x_ref, w_ref, tab_ref, rq_ref, rk_ref, rv_ref, rg_ref, aq_ref, ak_ref, av_ref,
                 iq_ref, ikw_ref, gr_ref, ga_ref):
    xb = x_ref[...].astype(BF16)

    def mm(c0, n):
        return _mm(xb, w_ref[:, c0:c0 + n])

    def tab(i):
        return tab_ref[:, i * LANES:(i + 1) * LANES]

    def rot_full(z):
        return z * tab(0) + pltpu.roll(z, 64, 1) * tab(1)

    def rot_part(z, c):
        return z * tab(c) + pltpu.roll(z, LANES - 8, 1) * tab(c + 1) + pltpu.roll(z, 8, 1) * tab(c + 2)

    z = mm(OFF_RQ, 512)
    for h in range(4):
        sl = slice(h * LANES, (h + 1) * LANES)
        rq_ref[:, sl] = rot_full(z[:, sl]).astype(BF16)
    z = mm(OFF_RK, 512)
    for h in range(4):
        sl = slice(h * LANES, (h + 1) * LANES)
        rk_ref[:, sl] = (rot_full(z[:, sl]) * (RET_DK ** -0.5)).astype(BF16)
    for c in range(2):
        rv_ref[:, c * 512:(c + 1) * 512] = mm(OFF_RV + c * 512, 512).astype(BF16)
    for c in range(2):
        rg_ref[:, c * 512:(c + 1) * 512] = mm(OFF_RG + c * 512, 512)
    z = mm(OFF_AQ, 512)
    for h in range(4):
        sl = slice(h * LANES, (h + 1) * LANES)
        aq_ref[:, sl] = rot_part(z[:, sl], 2).astype(BF16)
    z = mm(OFF_AK, 256)
    ak_ref[...] = rot_part(z[:, :LANES], 2)
    av_ref[...] = z[:, LANES:]
    z = mm(OFF_IQ, 512)
    for h in range(4):
        sl = slice(h * LANES, (h + 1) * LANES)
        iq_ref[:, sl] = rot_part(z[:, sl], 2).astype(BF16)
    ikw_ref[...] = rot_part(mm(OFF_IKW, LANES), 5)
    for c in range(2):
        gr_ref[:, c * 512:(c + 1) * 512] = mm(OFF_GR + c * 512, 512)
    for c in range(2):
        ga_ref[:, c * 512:(c + 1) * 512] = mm(OFF_GA + c * 512, 512)


def _rot_tables(pos):
    p = pos.shape[0]
    posf = pos.astype(F32)[:, None]
    ret_f = RET_ROPE_BASE ** (-jnp.linspace(0.0, 1.0, RET_DK // 2, dtype=F32))
    ang = posf * ret_f[None, :]
    c, s = jnp.cos(ang), jnp.sin(ang)
    cos_r = jnp.concatenate([c, c], 1)
    sin_r = jnp.concatenate([-s, s], 1)
    n_rot = DSA_HEAD_DIM // 4
    att_f = ROPE_THETA ** (-jnp.arange(0, n_rot, 2, dtype=F32) / n_rot)
    ang2 = posf * att_f[None, :]
    c2, s2 = jnp.cos(ang2), jnp.sin(ang2)
    half = n_rot // 2
    rest = DSA_HEAD_DIM - 2 * half
    c64 = jnp.concatenate([c2, c2, jnp.ones((p, rest), F32)], 1)
    s1_64 = jnp.concatenate([-s2, jnp.zeros((p, DSA_HEAD_DIM - half), F32)], 1)
    s2_64 = jnp.concatenate([jnp.zeros((p, half), F32), s2, jnp.zeros((p, rest), F32)], 1)
    z64 = jnp.zeros((p, DSA_HEAD_DIM), F32)
    ci = jnp.concatenate([c64, jnp.full((p, IDX_HEADS), IDX_HEADS ** -0.5, F32),
                          jnp.zeros((p, DSA_HEAD_DIM - IDX_HEADS), F32)], 1)
    return jnp.concatenate([cos_r, sin_r,
                            jnp.concatenate([c64, c64], 1), jnp.concatenate([s1_64, s1_64], 1),
                            jnp.concatenate([s2_64, s2_64], 1),
                            ci, jnp.concatenate([s1_64, z64], 1), jnp.concatenate([s2_64, z64], 1)], 1)


def _pack_w_in(w):
    cuts = np.cumsum(PROJ_WIDTHS)[:-1].tolist()
    rq, rk, rv, rg, aq, ak, av, iq, ik, iw, gr, ga = jnp.split(w, cuts, axis=1)
    pad = jnp.zeros((w.shape[0], LANES - IDX_DIM - IDX_HEADS), w.dtype)
    return jnp.concatenate([rq, rk, rv, rg, aq, ak, av, iq, ik, iw, pad, gr, ga], axis=1).astype(BF16)


def _project(x, wp, tab, tab_period):
    n = x.shape[0]
    tm = PROJ_TM
    row = lambda w: pl.BlockSpec((tm, w), lambda i: (i, 0))
    out_shapes = [((n, 512), BF16), ((n, 512), BF16), ((n, 1024), BF16), ((n, 1024), F32),
                  ((n, 512), BF16), ((n, LANES), F32), ((n, LANES), F32), ((n, 512), BF16),
                  ((n, LANES), F32), ((n, 1024), F32), ((n, 1024), F32)]
    return pl.pallas_call(
        _proj_kernel,
        grid=(n // tm,),
        in_specs=[row(D_MODEL),
                  pl.BlockSpec((D_MODEL, PACKED_COLS), lambda i: (0, 0), pipeline_mode=pl.Buffered(1)),
                  pl.BlockSpec((tm, TAB_COLS), lambda i: (i % tab_period, 0))],
        out_specs=[row(s[1]) for s, _ in out_shapes],
        out_shape=[jax.ShapeDtypeStruct(s, d) for s, d in out_shapes],
        compiler_params=pltpu.CompilerParams(dimension_semantics=("parallel",), vmem_limit_bytes=44 * MIB),
        name="proj",
    )(x, wp, tab)


def _ret_kernel(dec_ref, xi_ref, zeta_ref, rq_ref, rk_ref, rv_ref, rg_ref, s0_ref, ret_ref, sout_ref, st_ref,
                *, rows, n_chunk, g_pow):
    j = pl.program_id(1)
    cpad = RET_CHUNK

    @pl.when(j == 0)
    def _():
        st_ref[...] = s0_ref[0]

    def padded(v):
        if rows == cpad:
            return v
        return jnp.concatenate([v, jnp.zeros((cpad - rows, v.shape[1]), v.dtype)], axis=0)

    for c in range(n_chunk):
        rs = slice(c * rows, (c + 1) * rows)
        for h in range(RET_HEADS):
            ks = slice(h * RET_DK, (h + 1) * RET_DK)
            vs = slice(h * RET_DV, (h + 1) * RET_DV)
            q = padded(rq_ref[rs, ks])
            kt = padded(rk_ref[rs, ks].astype(F32)).T
            v = padded(rv_ref[rs, vs])
            s = st_ref[h]
            sc = _mm(q, kt.astype(BF16)) * dec_ref[h]
            o = _mm(sc.astype(BF16), v) + _mm(q, s.astype(BF16)) * xi_ref[h]
            st_ref[h] = g_pow[h] * s + _mm((kt * zeta_ref[h]).astype(BF16), v)
            o = o[:rows]
            mu = jnp.mean(o, axis=-1, keepdims=True)
            d = o - mu
            var = jnp.mean(d * d, axis=-1, keepdims=True)
            gn = d * lax.rsqrt(var + GN_EPS)
            g = rg_ref[rs, vs]
            ret_ref[rs, vs] = (gn * (g * jax.nn.sigmoid(g))).astype(BF16)

    @pl.when(j == pl.num_programs(1) - 1)
    def _():
        sout_ref[0] = st_ref[...]


def _retention(rq, rk, rv, rg, s0, rows_per_stream, rows):
    n = rq.shape[0]
    n_streams = n // rows_per_stream
    n_chunk = min(rows_per_stream // rows, 2)
    blk = rows * n_chunk
    nb = rows_per_stream // blk
    gam = 1.0 - 2.0 ** (-5.0 - np.arange(RET_HEADS, dtype=np.float64))
    i = np.arange(RET_CHUNK, dtype=np.float64)
    diff = i[:, None] - i[None, :]
    dec = np.where(diff >= 0, gam[:, None, None] ** np.maximum(diff, 0.0)[None], 0.0)
    xi = gam[:, None, None] ** (i + 1.0)[None, :, None]
    zeta = np.where(i < rows, gam[:, None, None] ** (rows - 1.0 - i)[None, None, :], 0.0)
    g_pow = tuple(float(g ** rows) for g in gam)
    const = lambda shape: pl.BlockSpec(shape, lambda s, j: (0,) * len(shape))
    row = lambda w: pl.BlockSpec((blk, w), lambda s, j: (s * nb + j, 0))
    st = pl.BlockSpec((1, RET_HEADS, RET_DK, RET_DV), lambda s, j: (s, 0, 0, 0))
    return pl.pallas_call(
        functools.partial(_ret_kernel, rows=rows, n_chunk=n_chunk, g_pow=g_pow),
        grid=(n_streams, nb),
        in_specs=[const((RET_HEADS, RET_CHUNK, RET_CHUNK)), const((RET_HEADS, RET_CHUNK, 1)),
                  const((RET_HEADS, 1, RET_CHUNK)), row(512), row(512), row(1024), row(1024), st],
        out_specs=[row(1024), st],
        out_shape=[jax.ShapeDtypeStruct((n, RET_HEADS * RET_DV), BF16),
                   jax.ShapeDtypeStruct((n_streams, RET_HEADS, RET_DK, RET_DV), F32)],
        scratch_shapes=[pltpu.VMEM((RET_HEADS, RET_DK, RET_DV), F32)],
        compiler_params=pltpu.CompilerParams(dimension_semantics=("parallel", "arbitrary"),
                                             vmem_limit_bytes=32 * MIB),
        name="retention",
    )(jnp.asarray(dec, F32), jnp.asarray(xi, F32), jnp.asarray(zeta, F32), rq, rk, rv, rg, s0)


def _dsa_kernel(aq_ref, iq_ref, ikwq_ref, kk_ref, vv_ref, ikk_ref, o_ref, w_ref, bias_ref,
                *, group, tq, n_keys, row0, limit_const, n_sel):
    rows = group * tq
    nsel_f = float(n_sel)
    neg_inf = -jnp.inf

    col = lax.broadcasted_iota(I32, (tq, n_keys), 1)
    if limit_const is None:
        limit = (jnp.right_shift(lax.broadcasted_iota(I32, (tq, 1), 0) + row0, 6) + 1) * CHUNK
    else:
        limit = limit_const

    def score_stream(s, carry):
        ikb = ikk_ref[s][:, :IDX_DIM].astype(BF16)
        iww = ikwq_ref[s][:, IDX_DIM:IDX_DIM + IDX_HEADS] * (IDX_DIM ** -0.5)
        acc = jnp.zeros((tq, n_keys), F32)
        for p in range(IDX_HEADS // 2):
            slab = iq_ref[s, :, p * LANES:(p + 1) * LANES]
            for hh in range(2):
                h = 2 * p + hh
                acc = acc + jnp.maximum(_nt(slab[:, hh * IDX_DIM:(hh + 1) * IDX_DIM], ikb), 0.0) * iww[:, h:h + 1]
        w_ref[pl.ds(pl.multiple_of(s * tq, tq), tq), :] = jnp.where(col < limit, acc, neg_inf)
        return carry

    lax.fori_loop(0, group, score_stream, 0)

    sc = w_ref[...]
    pos = jnp.sum(jnp.where(sc >= 0.0, 1.0, 0.0), axis=1, keepdims=True) >= nsel_f
    kk = jnp.where(pos, nsel_f, float(n_keys - n_sel + 1))
    w_ref[...] = jnp.where(pos, sc, -sc)

    def bit_step(i, u):
        cand_u = u | jnp.left_shift(jnp.int32(1), 30 - i)
        cand = pltpu.bitcast(cand_u, F32)
        cnt = jnp.sum(jnp.where(w_ref[...] >= cand, 1.0, 0.0), axis=1, keepdims=True)
        return jnp.where(cnt >= kk, cand_u, u)

    mag_u = lax.fori_loop(0, 31, bit_step, jnp.zeros((rows, 1), I32), unroll=SEARCH_UNROLL)
    mag = pltpu.bitcast(mag_u, F32)
    thr = jnp.where(pos, mag, -mag)

    sc = jnp.where(pos, w_ref[...], -w_ref[...])
    short = jnp.sum(jnp.where(sc >= thr, 1.0, 0.0), axis=1, keepdims=True) < nsel_f
    thr = jnp.where(jnp.logical_and(short, jnp.logical_not(pos)), -pltpu.bitcast(mag_u + 1, F32), thr)
    ge = sc >= thr
    cnt_gt = jnp.sum(jnp.where(sc > thr, 1.0, 0.0), axis=1, keepdims=True)
    cnt_ge = jnp.sum(jnp.where(ge, 1.0, 0.0), axis=1, keepdims=True)
    bias_ref[...] = jnp.where(jnp.logical_and(ge, sc > neg_inf), 0.0, neg_inf)
    excess = jnp.logical_and(cnt_ge > nsel_f, thr > neg_inf)

    @pl.when(jnp.max(jnp.where(excess, 1.0, 0.0)) > 0.0)
    def _():
        need = nsel_f - cnt_gt
        tri = jnp.where(lax.broadcasted_iota(I32, (LANES, LANES), 0) < lax.broadcasted_iota(I32, (LANES, LANES), 1),
                        1.0, 0.0).astype(BF16)
        before = jnp.zeros((rows, 1), F32)
        for b in range(n_keys // LANES):
            sl = slice(b * LANES, (b + 1) * LANES)
            sblk = jnp.where(pos, w_ref[:, sl], -w_ref[:, sl])
            eq = jnp.where(sblk == thr, 1.0, 0.0)
            rank = _mm(eq.astype(BF16), tri) + before
            keep = jnp.logical_or(sblk > thr, jnp.logical_and(sblk == thr, rank < need))
            bias_ref[:, sl] = jnp.where(jnp.logical_and(keep, sblk > neg_inf), 0.0, neg_inf)
            before = before + jnp.sum(eq, axis=1, keepdims=True)

    def attend_stream(s, carry):
        bias = bias_ref[pl.ds(pl.multiple_of(s * tq, tq), tq), :]
        kfull = kk_ref[s]
        vfull = vv_ref[s]
        for g in range(DSA_KV_HEADS):
            gs = slice(g * DSA_HEAD_DIM, (g + 1) * DSA_HEAD_DIM)
            kg = kfull[:, gs].astype(BF16)
            vg = vfull[:, gs].astype(BF16)
            for pp in range(2):
                p = 2 * g + pp
                slab = (aq_ref[s, :, p * LANES:(p + 1) * LANES].astype(F32) * (DSA_HEAD_DIM ** -0.5)).astype(BF16)
                outs = []
                for hh in range(2):
                    lg = _nt(slab[:, hh * DSA_HEAD_DIM:(hh + 1) * DSA_HEAD_DIM], kg) + bias
                    m = jnp.max(lg, axis=1, keepdims=True)
                    pr = jnp.exp(lg - m)
                    den = jnp.sum(pr, axis=1, keepdims=True)
                    outs.append(_mm(pr.astype(BF16), vg) / den)
                o_ref[s, :, p * LANES:(p + 1) * LANES] = jnp.concatenate(outs, axis=1).astype(BF16)
        return carry

    lax.fori_loop(0, group, attend_stream, 0)


def _dsa_call(aq, iq, ikw, kk, vv, ikk, group, tq, jq, n_keys, limit_const, n_sel, name):
    n_streams = kk.shape[0]
    qrow = lambda a: pl.BlockSpec((group, tq, a.shape[2]), lambda s: (s, jq, 0))
    krow = lambda a: pl.BlockSpec((group, n_keys, a.shape[2]), lambda s: (s, 0, 0))
    width = DSA_HEADS * DSA_HEAD_DIM
    return pl.pallas_call(
        functools.partial(_dsa_kernel, group=group, tq=tq, n_keys=n_keys, row0=jq * tq, limit_const=limit_const,
                          n_sel=n_sel),
        grid=(n_streams // group,),
        in_specs=[qrow(aq), qrow(iq), qrow(ikw), krow(kk), krow(vv), krow(ikk)],
        out_specs=pl.BlockSpec((group, tq, width), lambda s: (s, 0, 0)),
        out_shape=jax.ShapeDtypeStruct((n_streams, tq, width), BF16),
        scratch_shapes=[pltpu.VMEM((group * tq, n_keys), F32), pltpu.VMEM((group * tq, n_keys), F32)],
        compiler_params=pltpu.CompilerParams(dimension_semantics=("parallel",), vmem_limit_bytes=56 * MIB),
        name=name,
    )(aq, iq, ikw, kk, vv, ikk)


def _dsa(aq, iq, ikw, kk, vv, ikk, tq, chunk_causal, limit_const):
    n_streams, n_keys, _ = kk.shape
    n = aq.shape[0]
    per_stream = lambda a: a.reshape(n_streams, n // n_streams, a.shape[1])
    aq, iq, ikw = per_stream(aq), per_stream(iq), per_stream(ikw)
    nq = n // n_streams // tq
    n_sel = min(DSA_TOPK, limit_const // 4)
    group = max(1, min(n_streams, DSA_ROWS // tq, DSA_GROUP_MAX))
    if not chunk_causal:
        assert nq == 1
        return _dsa_call(aq, iq, ikw, kk, vv, ikk, group, tq, 0, n_keys, limit_const, n_sel, "dsa_s").reshape(n, -1)
    parts = [_dsa_call(aq, iq, ikw, kk, vv, ikk, group, tq, jq, (jq + 1) * tq, None, n_sel, f"dsa_p{jq}")
             for jq in range(nq)]
    return jnp.stack(parts, 1).reshape(n, -1)


def _pack_rows(v):
    q = QUARTER
    bits = lambda x: pltpu.bitcast(x.astype(BF16).astype(F32), I32)
    pair = lambda c: lax.shift_right_logical(bits(v[:, c * q:(c + 1) * q]), 16) | bits(v[:, (c + 1) * q:(c + 2) * q])
    return pair(0), pair(2)


def _unpack_rows(lo, hi):
    parts = []
    for w in (lo, hi):
        parts.append(pltpu.bitcast(lax.shift_left(w, 16), F32))
        parts.append(pltpu.bitcast(w & jnp.int32(-65536), F32))
    return jnp.concatenate(parts, axis=1)


def _layer_norm(v, g, b):
    mu = jnp.mean(v, axis=-1, keepdims=True)
    d = v - mu
    var = jnp.mean(d * d, axis=-1, keepdims=True)
    return d * lax.rsqrt(var + LN_EPS) * g + b


def _merge_kernel(x_ref, ret_ref, od_ref, gr_ref, ga_ref, wr_ref, wd_ref, wo_ref, g1_ref, b1_ref, rwt_ref, rb_ref,
                  tri_ref, cnt0_ref, h1_ref, hlo_ref, hhi_ref, gtm_ref, ek_ref, rk_ref, tot_ref, cnt_ref):
    @pl.when(pl.program_id(0) == 0)
    def _():
        cnt_ref[...] = cnt0_ref[:, 0:1]

    y_ret = _mm(ret_ref[...], wr_ref[...])
    y_dsa = _mm(od_ref[...], wd_ref[...])
    merged = jax.nn.sigmoid(gr_ref[...]) * y_ret + jax.nn.sigmoid(ga_ref[...]) * y_dsa
    mix = _mm(merged.astype(BF16), wo_ref[...])
    h1 = _layer_norm(DEEPNORM_ALPHA * x_ref[...] + mix, g1_ref[...], b1_ref[...])
    h1_ref[...] = h1
    hlo_ref[...], hhi_ref[...] = _pack_rows(h1)

    logits = lax.dot_general(rwt_ref[...], h1, (((1,), (1,)), ((), ())), preferred_element_type=F32,
                             precision=lax.Precision.HIGHEST) + rb_ref[...]
    tm = logits.shape[1]
    e_iota = lax.broadcasted_iota(I32, (N_EXPERTS, tm), 0)
    tops, hots, firsts = [], [], []
    for _ in range(MOE_TOP_K):
        m = jnp.max(logits, axis=0, keepdims=True)
        first = jnp.min(jnp.where(logits == m, e_iota, N_EXPERTS), axis=0, keepdims=True)
        hot = e_iota == first
        tops.append(m)
        hots.append(hot)
        firsts.append(first)
        logits = jnp.where(hot, -jnp.inf, logits)
    exps = [jnp.exp(m - tops[0]) for m in tops]
    den = exps[0] + exps[1] + exps[2] + exps[3]
    sel = jnp.zeros((N_EXPERTS, tm), F32)
    for hot in hots:
        sel = sel + jnp.where(hot, 1.0, 0.0)
    rank = _mm(sel.astype(BF16), tri_ref[...]) + cnt_ref[...]
    ranks = [jnp.sum(jnp.where(hot, rank, 0.0), axis=0, keepdims=True).astype(I32) for hot in hots]
    pad_i = jnp.zeros((8 - MOE_TOP_K, tm), I32)
    ek_ref[...] = jnp.concatenate(firsts + [pad_i], axis=0)
    rk_ref[...] = jnp.concatenate(ranks + [pad_i], axis=0)
    gates = jnp.concatenate([e / den for e in exps] + [jnp.zeros((LANES - MOE_TOP_K, tm), F32)], axis=0)
    gtm_ref[...] = gates.T
    cnt_ref[...] = cnt_ref[...] + jnp.sum(sel, axis=1, keepdims=True)
    tot_ref[...] = jnp.broadcast_to(cnt_ref[...], tot_ref.shape)


def _merge(x, ret, od, gr, ga, wr, wd, wo, g1, b1, rwt, rb, cnt0):
    n = x.shape[0]
    tm = MERGE_TM
    row = lambda w: pl.BlockSpec((tm, w), lambda i: (i, 0))
    const = lambda a: pl.BlockSpec(a.shape, lambda i: (0,) * a.ndim)
    col = pl.BlockSpec((8, tm), lambda i: (0, i))
    tri = jnp.asarray(np.triu(np.ones((tm, tm), np.float32), 1), BF16)
    return pl.pallas_call(
        _merge_kernel,
        grid=(n // tm,),
        in_specs=[row(D_MODEL), row(1024), row(512), row(1024), row(1024), const(wr), const(wd), const(wo),
                  const(g1), const(b1), const(rwt), const(rb), const(tri), const(cnt0)],
        out_specs=[row(D_MODEL), row(QUARTER), row(QUARTER), row(LANES), col, col,
                   pl.BlockSpec((N_EXPERTS, LANES), lambda i: (0, 0))],
        out_shape=[jax.ShapeDtypeStruct((n, D_MODEL), F32), jax.ShapeDtypeStruct((n, QUARTER), I32),
                   jax.ShapeDtypeStruct((n, QUARTER), I32), jax.ShapeDtypeStruct((n, LANES), F32),
                   jax.ShapeDtypeStruct((8, n), I32), jax.ShapeDtypeStruct((8, n), I32),
                   jax.ShapeDtypeStruct((N_EXPERTS, LANES), F32)],
        scratch_shapes=[pltpu.VMEM((N_EXPERTS, 1), F32)],
        compiler_params=pltpu.CompilerParams(dimension_semantics=("arbitrary",), vmem_limit_bytes=48 * MIB),
        name="merge",
    )(x, ret, od, gr, ga, wr, wd, wo, g1, b1, rwt, rb, tri, cnt0)


def _deinterleave_kernel(w_ref, o_ref):
    r = lax.broadcasted_iota(I32, (UP_BLOCK, UP_BLOCK), 0)
    c = lax.broadcasted_iota(I32, (UP_BLOCK, UP_BLOCK), 1)
    src = jnp.where(c < LANES, 2 * c, 2 * (c - LANES) + 1)
    perm = jnp.where(r == src, 1.0, 0.0).astype(BF16)
    for b in range(w_ref.shape[2] // UP_BLOCK):
        sl = slice(b * UP_BLOCK, (b + 1) * UP_BLOCK)
        o_ref[0, :, sl] = _mm(w_ref[0, :, sl].astype(BF16), perm).astype(BF16)


def _deinterleave_w_up(w_up):
    n_e, d_in, d_out = w_up.shape
    cols = 512
    spec = pl.BlockSpec((1, d_in, cols), lambda e, c: (e, 0, c))
    return pl.pallas_call(
        _deinterleave_kernel,
        grid=(n_e, d_out // cols),
        in_specs=[spec],
        out_specs=spec,
        out_shape=jax.ShapeDtypeStruct(w_up.shape, BF16),
        compiler_params=pltpu.CompilerParams(dimension_semantics=("parallel", "parallel"),
                                             vmem_limit_bytes=24 * MIB),
        name="w_up_prep",
    )(w_up)


def _pos_kernel(off_ref, ek_ref, rk_ref, pos_ref):
    ek = ek_ref[...]
    pos = rk_ref[...]
    for e in range(N_EXPERTS):
        pos = pos + jnp.where(ek == e, off_ref[e], 0)
    pos_ref[...] = pos


def _positions(off, ek, rk):
    n = ek.shape[1]
    tn = min(n, 2048)
    spec = pl.BlockSpec((8, tn), lambda i, off: (0, i))
    return pl.pallas_call(
        _pos_kernel,
        grid_spec=pltpu.PrefetchScalarGridSpec(num_scalar_prefetch=1, grid=(n // tn,), in_specs=[spec, spec],
                                               out_specs=spec),
        out_shape=jax.ShapeDtypeStruct((8, n), I32),
        name="moe_pos",
    )(off, ek, rk)


def _sc_mesh():
    return plsc.VectorSubcoreMesh(core_axis_name="core", subcore_axis_name="subcore")


def _sc_scatter(x, pos_flat, n_rows):
    n = x.shape[0]
    nw = n // SC_WINDOW

    @functools.partial(pl.kernel, out_type=jax.ShapeDtypeStruct((n_rows, x.shape[1]), x.dtype), mesh=_sc_mesh())
    def scatter(x_hbm, p_hbm, o_hbm):
        def body(x_vmem, *p_vmem):
            for p in p_vmem:
                pltpu.sync_copy(x_vmem, o_hbm.at[p.at[0]])

        pltpu.emit_pipeline(
            body, grid=(nw,),
            in_specs=[pl.BlockSpec((SC_WINDOW, x.shape[1]), lambda i: (i, 0))] +
                     [pl.BlockSpec((1, SC_WINDOW), functools.partial(lambda k, i: (0, k * nw + i), k))
                      for k in range(MOE_TOP_K)],
            out_specs=[], core_axis_name=("core", "subcore"), dimension_semantics=(pltpu.PARALLEL,),
        )(x_hbm, *([p_hbm] * MOE_TOP_K))

    return scatter(x, pos_flat)


def _sc_gather(y, pos_flat):
    m = pos_flat.shape[1]

    @functools.partial(pl.kernel, out_type=jax.ShapeDtypeStruct((m, y.shape[1]), y.dtype), mesh=_sc_mesh())
    def gather(y_hbm, p_hbm, o_hbm):
        def body(p_vmem, o_vmem):
            pltpu.sync_copy(y_hbm.at[p_vmem.at[0]], o_vmem)

        pltpu.emit_pipeline(
            body, grid=(m // SC_WINDOW,),
            in_specs=[pl.BlockSpec((1, SC_WINDOW), lambda i: (0, i))],
            out_specs=[pl.BlockSpec((SC_WINDOW, y.shape[1]), lambda i: (i, 0))],
            core_axis_name=("core", "subcore"), dimension_semantics=(pltpu.PARALLEL,),
        )(p_hbm, o_hbm)

    return gather(y, pos_flat)


def _ffn_kernel(be_ref, nu_ref, xlo_ref, xhi_ref, wup_ref, bup_ref, wdn_ref, bdn_ref, ylo_ref, yhi_ref):
    @pl.when(pl.program_id(0) < nu_ref[0])
    def _():
        x = _unpack_rows(xlo_ref[...], xhi_ref[...]).astype(BF16)
        h = _mm(x, wup_ref[0]) + bup_ref[0]
        acts = []
        for b in range(2 * D_FF // UP_BLOCK):
            glu = jnp.minimum(h[:, b * UP_BLOCK:b * UP_BLOCK + LANES], SWIGLU_LIMIT)
            lin = jnp.clip(h[:, b * UP_BLOCK + LANES:(b + 1) * UP_BLOCK], -SWIGLU_LIMIT, SWIGLU_LIMIT)
            acts.append(glu * jax.nn.sigmoid(SWIGLU_ALPHA * glu) * (lin + 1.0))
        act = jnp.concatenate(acts, axis=1)
        ylo_ref[...], yhi_ref[...] = _pack_rows(_mm(act.astype(BF16), wdn_ref[0]) + bdn_ref[0])


def _ffn(block_expert, n_used, xs_lo, xs_hi, wup, bup, wdn, bdn):
    rows = xs_lo.shape[0]
    blk = FFN_BLOCK
    row = pl.BlockSpec((blk, QUARTER), lambda i, be, nu: (i, 0))
    per_expert = lambda a: pl.BlockSpec((1,) + a.shape[1:], lambda i, be, nu: (be[i], 0, 0))
    return pl.pallas_call(
        _ffn_kernel,
        grid_spec=pltpu.PrefetchScalarGridSpec(
            num_scalar_prefetch=2, grid=(rows // blk,),
            in_specs=[row, row, per_expert(wup), per_expert(bup), per_expert(wdn), per_expert(bdn)],
            out_specs=[row, row]),
        out_shape=[jax.ShapeDtypeStruct((rows, QUARTER), I32), jax.ShapeDtypeStruct((rows, QUARTER), I32)],
        compiler_params=pltpu.CompilerParams(dimension_semantics=("arbitrary",), vmem_limit_bytes=40 * MIB),
        name="moe_ffn",
    )(block_expert, n_used, xs_lo, xs_hi, wup, bup, wdn, bdn)


def _combine_kernel(h1_ref, olo_ref, ohi_ref, gtm_ref, g2_ref, b2_ref, o_ref):
    g = gtm_ref[...]
    y = jnp.zeros(h1_ref.shape, F32)
    for k in range(MOE_TOP_K):
        y = y + g[:, k:k + 1] * _unpack_rows(olo_ref[k], ohi_ref[k])
    o_ref[...] = _layer_norm(DEEPNORM_ALPHA * h1_ref[...] + y, g2_ref[...], b2_ref[...])


def _combine(h1, og_lo, og_hi, gtm, g2, b2, row0):
    n = h1.shape[0]
    n_all = og_lo.shape[0] // MOE_TOP_K
    tm = MERGE_TM
    row = lambda w: pl.BlockSpec((tm, w), lambda i: (i, 0))
    const = lambda a: pl.BlockSpec(a.shape, lambda i: (0,) * a.ndim)
    picked = pl.BlockSpec((MOE_TOP_K, tm, QUARTER), lambda i: (0, i + row0 // tm, 0))
    return pl.pallas_call(
        _combine_kernel,
        grid=(n // tm,),
        in_specs=[row(D_MODEL), picked, picked, row(LANES), const(g2), const(b2)],
        out_specs=row(D_MODEL),
        out_shape=jax.ShapeDtypeStruct((n, D_MODEL), F32),
        compiler_params=pltpu.CompilerParams(dimension_semantics=("parallel",), vmem_limit_bytes=40 * MIB),
        name="moe_combine",
    )(h1, og_lo.reshape(MOE_TOP_K, n_all, QUARTER), og_hi.reshape(MOE_TOP_K, n_all, QUARTER), gtm, g2, b2)


def _moe(groups, totals, wup, bup, wdn, bdn, g2, b2):
    blk = FFN_BLOCK
    sizes = [g[0].shape[0] for g in groups]
    n_all = sum(sizes)
    n_rows = MOE_TOP_K * n_all + N_EXPERTS * blk
    counts = totals[:, 0].astype(I32)
    padded = (counts + blk - 1) // blk * blk
    ends = jnp.cumsum(padded)
    block_start = jnp.arange(n_rows // blk, dtype=I32) * blk
    block_expert = jnp.minimum(jnp.sum((ends[None, :] <= block_start[:, None]).astype(I32), axis=1), N_EXPERTS - 1)
    n_used = (ends[-1:] // blk).astype(I32)
    cat = lambda i, axis: jnp.concatenate([g[i] for g in groups], axis=axis)
    pos = _positions((ends - padded).astype(I32), cat(4, 1), cat(5, 1))
    pos_flat = pos[:MOE_TOP_K].reshape(1, MOE_TOP_K * n_all)
    xs_lo = _sc_scatter(cat(1, 0), pos_flat, n_rows)
    xs_hi = _sc_scatter(cat(2, 0), pos_flat, n_rows)
    ys_lo, ys_hi = _ffn(block_expert, n_used, xs_lo, xs_hi, wup, bup, wdn, bdn)
    og_lo, og_hi = _sc_gather(ys_lo, pos_flat), _sc_gather(ys_hi, pos_flat)
    starts = np.cumsum([0] + sizes[:-1]).tolist()
    return [_combine(g[0], og_lo, og_hi, g[3], g2, b2, r0) for g, r0 in zip(groups, starts)]


def _mixer(x, pos_tab, tab_period, s0, rows_per_stream, ret_rows, dsa_keys, weights, cnt0):
    wp, wr, wd, wo, g1, b1, rwt, rb = weights
    n = x.shape[0]
    n_streams = n // rows_per_stream
    rq, rk, rv, rg, aq, ak, av, iq, ikw, gr, ga = _project(x, wp, pos_tab, tab_period)
    ret, s_new = _retention(rq, rk, rv, rg, s0, rows_per_stream, ret_rows)
    if dsa_keys is None:
        per_stream = lambda a: a.reshape(n_streams, rows_per_stream, a.shape[1])
        od = _dsa(aq, iq, ikw, per_stream(ak), per_stream(av), per_stream(ikw), DSA_TQ, True, rows_per_stream)
    else:
        kk, vv, ikk, limit = dsa_keys(ak, av, ikw)
        od = _dsa(aq, iq, ikw, kk, vv, ikk, rows_per_stream, False, limit)
    h1, h_lo, h_hi, gtm, ek, rk_, totals = _merge(x, ret, od, gr, ga, wr, wd, wo, g1, b1, rwt, rb, cnt0)
    return (h1, h_lo, h_hi, gtm, ek, rk_), totals, (s_new, ak, av, ikw[:, :IDX_DIM])


def kernel(x_prompt, x_sample, state_ret, cache_k, cache_v, cache_idx_k, w_in, w_ret_o, w_dsa_o, w_o,
           ln1_g, ln1_b, router_w, router_b, w_up, b_up, w_down, b_down, ln2_g, ln2_b):
    assert w_in.shape[0] == DEPTH
    batch, seq, _ = x_prompt.shape
    dec_batch, dec_seq, _ = x_sample.shape
    past = cache_k.shape[2]
    assert seq % PROJ_TM == 0 and seq % DSA_TQ == 0 and PROJ_TM % dec_seq == 0

    l = 0
    mixer_w = (_pack_w_in(w_in[l]), w_ret_o[l].astype(BF16), w_dsa_o[l].astype(BF16), w_o[l].astype(BF16),
               ln1_g[l][None, :], ln1_b[l][None, :], router_w[l].T, router_b[l][:, None])
    moe_w = (
        _deinterleave_w_up(w_up[l]),
        b_up[l].reshape(N_EXPERTS, 2 * D_FF // UP_BLOCK, LANES, 2).transpose(0, 1, 3, 2).reshape(N_EXPERTS, 1, 2 * D_FF),
        w_down[l].astype(BF16), b_down[l][:, None, :], ln2_g[l][None, :], ln2_b[l][None, :])

    tab_p = _rot_tables(jnp.arange(seq))
    zeros_state = jnp.zeros((batch, RET_HEADS, RET_DK, RET_DV), F32)
    moe_p, totals_p, (s_p, k_p, v_p, ik_p) = _mixer(
        x_prompt.reshape(batch * seq, D_MODEL), tab_p, seq // PROJ_TM, zeros_state, seq, RET_CHUNK, None, mixer_w,
        jnp.zeros((N_EXPERTS, LANES), F32))

    n_keys_real = past + dec_seq
    n_keys = -(-n_keys_real // LANES) * LANES
    tab_s = jnp.tile(_rot_tables(past + jnp.arange(dec_seq)), (PROJ_TM // dec_seq, 1))

    def sample_keys(ak, av, ikw):
        def cat(cache, new, width):
            padz = jnp.zeros((dec_batch, n_keys - n_keys_real, width), F32)
            return jnp.concatenate([cache.reshape(dec_batch, past, width), new.reshape(dec_batch, dec_seq, width), padz],
                                   axis=1)
        return (cat(cache_k[l], ak, LANES), cat(cache_v[l], av, LANES),
                cat(cache_idx_k[l], ikw[:, :IDX_DIM], IDX_DIM), n_keys_real)

    moe_s, totals, (s_s, k_s, v_s, ik_s) = _mixer(
        x_sample.reshape(dec_batch * dec_seq, D_MODEL), tab_s, 1, state_ret[l], dec_seq, dec_seq, sample_keys, mixer_w,
        totals_p)

    y_p, y_s = _moe([moe_p, moe_s], totals, *moe_w)

    kv = (DSA_KV_HEADS, DSA_HEAD_DIM)
    return (y_p.reshape(batch, seq, D_MODEL), y_s.reshape(dec_batch, dec_seq, D_MODEL),
            s_p[None], k_p.reshape(1, batch, seq, *kv), v_p.reshape(1, batch, seq, *kv),
            ik_p.reshape(1, batch, seq, IDX_DIM),
            s_s[None], k_s.reshape(1, dec_batch, dec_seq, *kv), v_s.reshape(1, dec_batch, dec_seq, *kv),
            ik_s.reshape(1, dec_batch, dec_seq, IDX_DIM))
```

```python
import functools

import numpy as np
import jax
import jax.numpy as jnp
from jax import lax
from jax.experimental import pallas as pl
from jax.experimental.pallas import tpu as pltpu
from jax.experimental.pallas import tpu_sc as plsc

F32 = jnp.float32
BF16 = jnp.bfloat16
I32 = jnp.int32

D_MODEL = 1024
CHUNK = 64
RET_HEADS = 4
RET_DK = 128
RET_DV = 256
RET_ROPE_BASE = 10000.0
DSA_HEADS = 8
DSA_KV_HEADS = 2
DSA_HEAD_DIM = 64
IDX_HEADS = 8
IDX_DIM = 64
DSA_TOPK = 256
ROPE_THETA = 500000.0
N_EXPERTS = 32
MOE_TOP_K = 4
D_FF = 1024
SWIGLU_ALPHA = 1.702
SWIGLU_LIMIT = 7.0
LN_EPS = 1e-5
GN_EPS = 1e-6
DEPTH = 1
DEEPNORM_ALPHA = (2.0 * DEPTH) ** 0.25
PROJ_WIDTHS = (RET_HEADS * RET_DK, RET_HEADS * RET_DK, RET_HEADS * RET_DV, RET_HEADS * RET_DV,
               DSA_HEADS * DSA_HEAD_DIM, DSA_KV_HEADS * DSA_HEAD_DIM, DSA_KV_HEADS * DSA_HEAD_DIM,
               IDX_HEADS * IDX_DIM, IDX_DIM, IDX_HEADS, D_MODEL, D_MODEL)

LANES = 128
MIB = 1024 * 1024

OFF_RQ, OFF_RK, OFF_RV, OFF_RG = 0, 512, 1024, 2048
OFF_AQ, OFF_AK, OFF_AV, OFF_IQ, OFF_IKW = 3072, 3584, 3712, 3840, 4352
OFF_GR, OFF_GA, PACKED_COLS = 4480, 5504, 6528
TAB_COLS = 8 * LANES

PROJ_TM = 512
RET_CHUNK = 128
RET_CHUNKS_PER_STEP = 4
DSA_TQ = 256
DSA_ROWS = 512
DSA_GROUP_MAX = 4
SEARCH_UNROLL = 8
MERGE_TM = 512
FFN_BLOCK = 256
SC_WINDOW = 128
QUARTER = D_MODEL // 4
UP_BLOCK = 2 * LANES


def _nt(a, b):
    return lax.dot_general(a, b, (((1,), (1,)), ((), ())), preferred_element_type=F32)


def _mm(a, b):
    return jnp.dot(a, b, preferred_element_type=F32)


def _proj_kernel(x_ref, w_ref, tab_ref, rq_ref, rk_ref, rv_ref, rg_ref, aq_ref, ak_ref, av_ref,
                 iq_ref, ikw_ref, gr_ref, ga_ref):
    xb = x_ref[...].astype(BF16)

    def mm(c0, n):
        return _mm(xb, w_ref[:, c0:c0 + n])

    def tab(i):
        return tab_ref[:, i * LANES:(i + 1) * LANES]

    def rot_full(z):
        return z * tab(0) + pltpu.roll(z, 64, 1) * tab(1)

    def rot_part(z, c):
        return z * tab(c) + pltpu.roll(z, LANES - 8, 1) * tab(c + 1) + pltpu.roll(z, 8, 1) * tab(c + 2)

    z = mm(OFF_RQ, 512)
    for h in range(4):
        sl = slice(h * LANES, (h + 1) * LANES)
        rq_ref[:, sl] = rot_full(z[:, sl]).astype(BF16)
    z = mm(OFF_RK, 512)
    for h in range(4):
        sl = slice(h * LANES, (h + 1) * LANES)
        rk_ref[:, sl] = (rot_full(z[:, sl]) * (RET_DK ** -0.5)).astype(BF16)
    for c in range(2):
        rv_ref[:, c * 512:(c + 1) * 512] = mm(OFF_RV + c * 512, 512).astype(BF16)
    for c in range(2):
        rg_ref[:, c * 512:(c + 1) * 512] = mm(OFF_RG + c * 512, 512)
    z = mm(OFF_AQ, 512)
    for h in range(4):
        sl = slice(h * LANES, (h + 1) * LANES)
        aq_ref[:, sl] = rot_part(z[:, sl], 2).astype(BF16)
    z = mm(OFF_AK, 256)
    ak_ref[...] = rot_part(z[:, :LANES], 2)
    av_ref[...] = z[:, LANES:]
    z = mm(OFF_IQ, 512)
    for h in range(4):
        sl = slice(h * LANES, (h + 1) * LANES)
        iq_ref[:, sl] = rot_part(z[:, sl], 2).astype(BF16)
    ikw_ref[...] = rot_part(mm(OFF_IKW, LANES), 5)
    for c in range(2):
        gr_ref[:, c * 512:(c + 1) * 512] = mm(OFF_GR + c * 512, 512)
    for c in range(2):
        ga_ref[:, c * 512:(c + 1) * 512] = mm(OFF_GA + c * 512, 512)


def _rot_tables(pos):
    p = pos.shape[0]
    posf = pos.astype(F32)[:, None]
    ret_f = RET_ROPE_BASE ** (-jnp.linspace(0.0, 1.0, RET_DK // 2, dtype=F32))
    ang = posf * ret_f[None, :]
    c, s = jnp.cos(ang), jnp.sin(ang)
    cos_r = jnp.concatenate([c, c], 1)
    sin_r = jnp.concatenate([-s, s], 1)
    n_rot = DSA_HEAD_DIM // 4
    att_f = ROPE_THETA ** (-jnp.arange(0, n_rot, 2, dtype=F32) / n_rot)
    ang2 = posf * att_f[None, :]
    c2, s2 = jnp.cos(ang2), jnp.sin(ang2)
    half = n_rot // 2
    rest = DSA_HEAD_DIM - 2 * half
    c64 = jnp.concatenate([c2, c2, jnp.ones((p, rest), F32)], 1)
    s1_64 = jnp.concatenate([-s2, jnp.zeros((p, DSA_HEAD_DIM - half), F32)], 1)
    s2_64 = jnp.concatenate([jnp.zeros((p, half), F32), s2, jnp.zeros((p, rest), F32)], 1)
    z64 = jnp.zeros((p, DSA_HEAD_DIM), F32)
    ci = jnp.concatenate([c64, jnp.full((p, IDX_HEADS), IDX_HEADS ** -0.5, F32),
                          jnp.zeros((p, DSA_HEAD_DIM - IDX_HEADS), F32)], 1)
    return jnp.concatenate([cos_r, sin_r,
                            jnp.concatenate([c64, c64], 1), jnp.concatenate([s1_64, s1_64], 1),
                            jnp.concatenate([s2_64, s2_64], 1),
                            ci, jnp.concatenate([s1_64, z64], 1), jnp.concatenate([s2_64, z64], 1)], 1)


def _pack_w_in(w):
    cuts = np.cumsum(PROJ_WIDTHS)[:-1].tolist()
    rq, rk, rv, rg, aq, ak, av, iq, ik, iw, gr, ga = jnp.split(w, cuts, axis=1)
    pad = jnp.zeros((w.shape[0], LANES - IDX_DIM - IDX_HEADS), w.dtype)
    return jnp.concatenate([rq, rk, rv, rg, aq, ak, av, iq, ik, iw, pad, gr, ga], axis=1).astype(BF16)


def _project(x, wp, tab, tab_period):
    n = x.shape[0]
    tm = PROJ_TM
    row = lambda w: pl.BlockSpec((tm, w), lambda i: (i, 0))
    out_shapes = [((n, 512), BF16), ((n, 512), BF16), ((n, 1024), BF16), ((n, 1024), F32),
                  ((n, 512), BF16), ((n, LANES), F32), ((n, LANES), F32), ((n, 512), BF16),
                  ((n, LANES), F32), ((n, 1024), F32), ((n, 1024), F32)]
    return pl.pallas_call(
        _proj_kernel,
        grid=(n // tm,),
        in_specs=[row(D_MODEL),
                  pl.BlockSpec((D_MODEL, PACKED_COLS), lambda i: (0, 0), pipeline_mode=pl.Buffered(1)),
                  pl.BlockSpec((tm, TAB_COLS), lambda i: (i % tab_period, 0))],
        out_specs=[row(s[1]) for s, _ in out_shapes],
        out_shape=[jax.ShapeDtypeStruct(s, d) for s, d in out_shapes],
        compiler_params=pltpu.CompilerParams(dimension_semantics=("parallel",), vmem_limit_bytes=52 * MIB),
        name="proj",
    )(x, wp, tab)


def _ret_kernel(dec_ref, xi_ref, zeta_ref, rq_ref, rk_ref, rv_ref, rg_ref, s0_ref, ret_ref, sout_ref, st_ref,
                *, rows, n_chunk, g_pow):
    j = pl.program_id(1)
    cpad = RET_CHUNK

    @pl.when(j == 0)
    def _():
        st_ref[...] = s0_ref[0]

    def padded(v):
        if rows == cpad:
            return v
        return jnp.concatenate([v, jnp.zeros((cpad - rows, v.shape[1]), v.dtype)], axis=0)

    for c in range(n_chunk):
        rs = slice(c * rows, (c + 1) * rows)
        for h in range(RET_HEADS):
            ks = slice(h * RET_DK, (h + 1) * RET_DK)
            vs = slice(h * RET_DV, (h + 1) * RET_DV)
            q = padded(rq_ref[rs, ks])
            kt = padded(rk_ref[rs, ks].astype(F32)).T
            v = padded(rv_ref[rs, vs])
            s = st_ref[h]
            sc = _mm(q, kt.astype(BF16)) * dec_ref[h]
            o = _mm(sc.astype(BF16), v) + _mm(q, s.astype(BF16)) * xi_ref[h]
            st_ref[h] = g_pow[h] * s + _mm((kt * zeta_ref[h]).astype(BF16), v)
            o = o[:rows]
            mu = jnp.mean(o, axis=-1, keepdims=True)
            d = o - mu
            var = jnp.mean(d * d, axis=-1, keepdims=True)
            gn = d * lax.rsqrt(var + GN_EPS)
            g = rg_ref[rs, vs]
            ret_ref[rs, vs] = (gn * (g * jax.nn.sigmoid(g))).astype(BF16)

    @pl.when(j == pl.num_programs(1) - 1)
    def _():
        sout_ref[0] = st_ref[...]


def _retention(rq, rk, rv, rg, s0, rows_per_stream, rows):
    n = rq.shape[0]
    n_streams = n // rows_per_stream
    n_chunk = min(rows_per_stream // rows, RET_CHUNKS_PER_STEP)
    blk = rows * n_chunk
    nb = rows_per_stream // blk
    gam = 1.0 - 2.0 ** (-5.0 - np.arange(RET_HEADS, dtype=np.float64))
    i = np.arange(RET_CHUNK, dtype=np.float64)
    diff = i[:, None] - i[None, :]
    dec = np.where(diff >= 0, gam[:, None, None] ** np.maximum(diff, 0.0)[None], 0.0)
    xi = gam[:, None, None] ** (i + 1.0)[None, :, None]
    zeta = np.where(i < rows, gam[:, None, None] ** (rows - 1.0 - i)[None, None, :], 0.0)
    g_pow = tuple(float(g ** rows) for g in gam)
    const = lambda shape: pl.BlockSpec(shape, lambda s, j: (0,) * len(shape))
    row = lambda w: pl.BlockSpec((blk, w), lambda s, j: (s * nb + j, 0))
    st = pl.BlockSpec((1, RET_HEADS, RET_DK, RET_DV), lambda s, j: (s, 0, 0, 0))
    return pl.pallas_call(
        functools.partial(_ret_kernel, rows=rows, n_chunk=n_chunk, g_pow=g_pow),
        grid=(n_streams, nb),
        in_specs=[const((RET_HEADS, RET_CHUNK, RET_CHUNK)), const((RET_HEADS, RET_CHUNK, 1)),
                  const((RET_HEADS, 1, RET_CHUNK)), row(512), row(512), row(1024), row(1024), st],
        out_specs=[row(1024), st],
        out_shape=[jax.ShapeDtypeStruct((n, RET_HEADS * RET_DV), BF16),
                   jax.ShapeDtypeStruct((n_streams, RET_HEADS, RET_DK, RET_DV), F32)],
        scratch_shapes=[pltpu.VMEM((RET_HEADS, RET_DK, RET_DV), F32)],
        compiler_params=pltpu.CompilerParams(dimension_semantics=("parallel", "arbitrary"),
                                             vmem_limit_bytes=32 * MIB),
        name="retention",
    )(jnp.asarray(dec, F32), jnp.asarray(xi, F32), jnp.asarray(zeta, F32), rq, rk, rv, rg, s0)


def _dsa_kernel(aq_ref, iq_ref, ikwq_ref, kk_ref, vv_ref, ikk_ref, o_ref, w_ref, bias_ref,
                *, group, tq, n_keys, row0, limit_const, n_sel):
    rows = group * tq
    nsel_f = float(n_sel)
    neg_inf = -jnp.inf

    col = lax.broadcasted_iota(I32, (tq, n_keys), 1)
    if limit_const is None:
        limit = (jnp.right_shift(lax.broadcasted_iota(I32, (tq, 1), 0) + row0, 6) + 1) * CHUNK
    else:
        limit = limit_const

    def score_stream(s, carry):
        ikb = ikk_ref[s][:, :IDX_DIM].astype(BF16)
        iww = ikwq_ref[s][:, IDX_DIM:IDX_DIM + IDX_HEADS] * (IDX_DIM ** -0.5)
        acc = jnp.zeros((tq, n_keys), F32)
        for p in range(IDX_HEADS // 2):
            slab = iq_ref[s, :, p * LANES:(p + 1) * LANES]
            for hh in range(2):
                h = 2 * p + hh
                acc = acc + jnp.maximum(_nt(slab[:, hh * IDX_DIM:(hh + 1) * IDX_DIM], ikb), 0.0) * iww[:, h:h + 1]
        w_ref[pl.ds(pl.multiple_of(s * tq, tq), tq), :] = jnp.where(col < limit, acc, neg_inf)
        return carry

    lax.fori_loop(0, group, score_stream, 0)

    sc = w_ref[...]
    pos = jnp.sum(jnp.where(sc >= 0.0, 1.0, 0.0), axis=1, keepdims=True) >= nsel_f
    kk = jnp.where(pos, nsel_f, float(n_keys - n_sel + 1))
    w_ref[...] = jnp.where(pos, sc, -sc)

    def bit_step(i, u):
        cand_u = u | jnp.left_shift(jnp.int32(1), 30 - i)
        cand = pltpu.bitcast(cand_u, F32)
        cnt = jnp.sum(jnp.where(w_ref[...] >= cand, 1.0, 0.0), axis=1, keepdims=True)
        return jnp.where(cnt >= kk, cand_u, u)

    mag_u = lax.fori_loop(0, 31, bit_step, jnp.zeros((rows, 1), I32), unroll=SEARCH_UNROLL)
    mag = pltpu.bitcast(mag_u, F32)
    thr = jnp.where(pos, mag, -mag)

    sc = jnp.where(pos, w_ref[...], -w_ref[...])
    short = jnp.sum(jnp.where(sc >= thr, 1.0, 0.0), axis=1, keepdims=True) < nsel_f
    thr = jnp.where(jnp.logical_and(short, jnp.logical_not(pos)), -pltpu.bitcast(mag_u + 1, F32), thr)
    ge = sc >= thr
    cnt_gt = jnp.sum(jnp.where(sc > thr, 1.0, 0.0), axis=1, keepdims=True)
    cnt_ge = jnp.sum(jnp.where(ge, 1.0, 0.0), axis=1, keepdims=True)
    bias_ref[...] = jnp.where(jnp.logical_and(ge, sc > neg_inf), 0.0, neg_inf)
    excess = jnp.logical_and(cnt_ge > nsel_f, thr > neg_inf)

    @pl.when(jnp.max(jnp.where(excess, 1.0, 0.0)) > 0.0)
    def _():
        need = nsel_f - cnt_gt
        tri = jnp.where(lax.broadcasted_iota(I32, (LANES, LANES), 0) < lax.broadcasted_iota(I32, (LANES, LANES), 1),
                        1.0, 0.0).astype(BF16)
        before = jnp.zeros((rows, 1), F32)
        for b in range(n_keys // LANES):
            sl = slice(b * LANES, (b + 1) * LANES)
            sblk = jnp.where(pos, w_ref[:, sl], -w_ref[:, sl])
            eq = jnp.where(sblk == thr, 1.0, 0.0)
            rank = _mm(eq.astype(BF16), tri) + before
            keep = jnp.logical_or(sblk > thr, jnp.logical_and(sblk == thr, rank < need))
            bias_ref[:, sl] = jnp.where(jnp.logical_and(keep, sblk > neg_inf), 0.0, neg_inf)
            before = before + jnp.sum(eq, axis=1, keepdims=True)

    def attend_stream(s, carry):
        bias = bias_ref[pl.ds(pl.multiple_of(s * tq, tq), tq), :]
        kfull = kk_ref[s]
        vfull = vv_ref[s]
        for g in range(DSA_KV_HEADS):
            gs = slice(g * DSA_HEAD_DIM, (g + 1) * DSA_HEAD_DIM)
            kg = kfull[:, gs].astype(BF16)
            vg = vfull[:, gs].astype(BF16)
            for pp in range(2):
                p = 2 * g + pp
                slab = (aq_ref[s, :, p * LANES:(p + 1) * LANES].astype(F32) * (DSA_HEAD_DIM ** -0.5)).astype(BF16)
                outs = []
                for hh in range(2):
                    lg = _nt(slab[:, hh * DSA_HEAD_DIM:(hh + 1) * DSA_HEAD_DIM], kg) + bias
                    m = jnp.max(lg, axis=1, keepdims=True)
                    pr = jnp.exp(lg - m)
                    den = jnp.sum(pr, axis=1, keepdims=True)
                    outs.append(_mm(pr.astype(BF16), vg) / den)
                o_ref[s, :, p * LANES:(p + 1) * LANES] = jnp.concatenate(outs, axis=1).astype(BF16)
        return carry

    lax.fori_loop(0, group, attend_stream, 0)


def _dsa_call(aq, iq, ikw, kk, vv, ikk, group, tq, jq, n_keys, limit_const, n_sel, name):
    n_streams = kk.shape[0]
    qrow = lambda a: pl.BlockSpec((group, tq, a.shape[2]), lambda s: (s, jq, 0))
    krow = lambda a: pl.BlockSpec((group, n_keys, a.shape[2]), lambda s: (s, 0, 0))
    width = DSA_HEADS * DSA_HEAD_DIM
    return pl.pallas_call(
        functools.partial(_dsa_kernel, group=group, tq=tq, n_keys=n_keys, row0=jq * tq, limit_const=limit_const,
                          n_sel=n_sel),
        grid=(n_streams // group,),
        in_specs=[qrow(aq), qrow(iq), qrow(ikw), krow(kk), krow(vv), krow(ikk)],
        out_specs=pl.BlockSpec((group, tq, width), lambda s: (s, 0, 0)),
        out_shape=jax.ShapeDtypeStruct((n_streams, tq, width), BF16),
        scratch_shapes=[pltpu.VMEM((group * tq, n_keys), F32), pltpu.VMEM((group * tq, n_keys), F32)],
        compiler_params=pltpu.CompilerParams(dimension_semantics=("parallel",), vmem_limit_bytes=56 * MIB),
        name=name,
    )(aq, iq, ikw, kk, vv, ikk)


def _dsa(aq, iq, ikw, kk, vv, ikk, tq, chunk_causal, limit_const):
    n_streams, n_keys, _ = kk.shape
    n = aq.shape[0]
    per_stream = lambda a: a.reshape(n_streams, n // n_streams, a.shape[1])
    aq, iq, ikw = per_stream(aq), per_stream(iq), per_stream(ikw)
    nq = n // n_streams // tq
    n_sel = min(DSA_TOPK, limit_const // 4)
    group = max(1, min(n_streams, DSA_ROWS // tq, DSA_GROUP_MAX))
    if not chunk_causal:
        assert nq == 1
        return _dsa_call(aq, iq, ikw, kk, vv, ikk, group, tq, 0, n_keys, limit_const, n_sel, "dsa_s").reshape(n, -1)
    parts = [_dsa_call(aq, iq, ikw, kk, vv, ikk, group, tq, jq, (jq + 1) * tq, None, n_sel, f"dsa_p{jq}")
             for jq in range(nq)]
    return jnp.stack(parts, 1).reshape(n, -1)


def _pack_rows(v):
    q = QUARTER
    bits = lambda x: pltpu.bitcast(x.astype(BF16).astype(F32), I32)
    pair = lambda c: lax.shift_right_logical(bits(v[:, c * q:(c + 1) * q]), 16) | bits(v[:, (c + 1) * q:(c + 2) * q])
    return pair(0), pair(2)


def _unpack_rows(lo, hi):
    parts = []
    for w in (lo, hi):
        parts.append(pltpu.bitcast(lax.shift_left(w, 16), F32))
        parts.append(pltpu.bitcast(w & jnp.int32(-65536), F32))
    return jnp.concatenate(parts, axis=1)


def _layer_norm(v, g, b):
    mu = jnp.mean(v, axis=-1, keepdims=True)
    d = v - mu
    var = jnp.mean(d * d, axis=-1, keepdims=True)
    return d * lax.rsqrt(var + LN_EPS) * g + b


def _merge_kernel(x_ref, ret_ref, od_ref, gr_ref, ga_ref, wr_ref, wd_ref, wo_ref, g1_ref, b1_ref, rwt_ref, rb_ref,
                  tri_ref, cnt0_ref, h1_ref, hlo_ref, hhi_ref, gtm_ref, ek_ref, rk_ref, tot_ref, cnt_ref):
    @pl.when(pl.program_id(0) == 0)
    def _():
        cnt_ref[...] = cnt0_ref[:, 0:1]

    y_ret = _mm(ret_ref[...], wr_ref[...])
    y_dsa = _mm(od_ref[...], wd_ref[...])
    merged = jax.nn.sigmoid(gr_ref[...]) * y_ret + jax.nn.sigmoid(ga_ref[...]) * y_dsa
    mix = _mm(merged.astype(BF16), wo_ref[...])
    h1 = _layer_norm(DEEPNORM_ALPHA * x_ref[...] + mix, g1_ref[...], b1_ref[...])
    h1_ref[...] = h1
    hlo_ref[...], hhi_ref[...] = _pack_rows(h1)

    logits = lax.dot_general(rwt_ref[...], h1, (((1,), (1,)), ((), ())), preferred_element_type=F32,
                             precision=lax.Precision.HIGHEST) + rb_ref[...]
    tm = logits.shape[1]
    e_iota = lax.broadcasted_iota(I32, (N_EXPERTS, tm), 0)
    tops, hots, firsts = [], [], []
    for _ in range(MOE_TOP_K):
        m = jnp.max(logits, axis=0, keepdims=True)
        first = jnp.min(jnp.where(logits == m, e_iota, N_EXPERTS), axis=0, keepdims=True)
        hot = e_iota == first
        tops.append(m)
        hots.append(hot)
        firsts.append(first)
        logits = jnp.where(hot, -jnp.inf, logits)
    exps = [jnp.exp(m - tops[0]) for m in tops]
    den = exps[0] + exps[1] + exps[2] + exps[3]
    sel = jnp.zeros((N_EXPERTS, tm), F32)
    for hot in hots:
        sel = sel + jnp.where(hot, 1.0, 0.0)
    rank = _mm(sel.astype(BF16), tri_ref[...]) + cnt_ref[...]
    ranks = [jnp.sum(jnp.where(hot, rank, 0.0), axis=0, keepdims=True).astype(I32) for hot in hots]
    pad_i = jnp.zeros((8 - MOE_TOP_K, tm), I32)
    ek_ref[...] = jnp.concatenate(firsts + [pad_i], axis=0)
    rk_ref[...] = jnp.concatenate(ranks + [pad_i], axis=0)
    gates = jnp.concatenate([e / den for e in exps] + [jnp.zeros((LANES - MOE_TOP_K, tm), F32)], axis=0)
    gtm_ref[...] = gates.T
    cnt_ref[...] = cnt_ref[...] + jnp.sum(sel, axis=1, keepdims=True)
    tot_ref[...] = jnp.broadcast_to(cnt_ref[...], tot_ref.shape)


def _merge(x, ret, od, gr, ga, wr, wd, wo, g1, b1, rwt, rb, cnt0):
    n = x.shape[0]
    tm = MERGE_TM
    row = lambda w: pl.BlockSpec((tm, w), lambda i: (i, 0))
    const = lambda a: pl.BlockSpec(a.shape, lambda i: (0,) * a.ndim)
    col = pl.BlockSpec((8, tm), lambda i: (0, i))
    tri = jnp.asarray(np.triu(np.ones((tm, tm), np.float32), 1), BF16)
    return pl.pallas_call(
        _merge_kernel,
        grid=(n // tm,),
        in_specs=[row(D_MODEL), row(1024), row(512), row(1024), row(1024), const(wr), const(wd), const(wo),
                  const(g1), const(b1), const(rwt), const(rb), const(tri), const(cnt0)],
        out_specs=[row(D_MODEL), row(QUARTER), row(QUARTER), row(LANES), col, col,
                   pl.BlockSpec((N_EXPERTS, LANES), lambda i: (0, 0))],
        out_shape=[jax.ShapeDtypeStruct((n, D_MODEL), F32), jax.ShapeDtypeStruct((n, QUARTER), I32),
                   jax.ShapeDtypeStruct((n, QUARTER), I32), jax.ShapeDtypeStruct((n, LANES), F32),
                   jax.ShapeDtypeStruct((8, n), I32), jax.ShapeDtypeStruct((8, n), I32),
                   jax.ShapeDtypeStruct((N_EXPERTS, LANES), F32)],
        scratch_shapes=[pltpu.VMEM((N_EXPERTS, 1), F32)],
        compiler_params=pltpu.CompilerParams(dimension_semantics=("arbitrary",), vmem_limit_bytes=48 * MIB),
        name="merge",
    )(x, ret, od, gr, ga, wr, wd, wo, g1, b1, rwt, rb, tri, cnt0)


def _deinterleave_kernel(w_ref, o_ref):
    r = lax.broadcasted_iota(I32, (UP_BLOCK, UP_BLOCK), 0)
    c = lax.broadcasted_iota(I32, (UP_BLOCK, UP_BLOCK), 1)
    src = jnp.where(c < LANES, 2 * c, 2 * (c - LANES) + 1)
    perm = jnp.where(r == src, 1.0, 0.0).astype(BF16)
    for b in range(w_ref.shape[2] // UP_BLOCK):
        sl = slice(b * UP_BLOCK, (b + 1) * UP_BLOCK)
        o_ref[0, :, sl] = _mm(w_ref[0, :, sl].astype(BF16), perm).astype(BF16)


def _deinterleave_w_up(w_up):
    n_e, d_in, d_out = w_up.shape
    cols = 512
    spec = pl.BlockSpec((1, d_in, cols), lambda e, c: (e, 0, c))
    return pl.pallas_call(
        _deinterleave_kernel,
        grid=(n_e, d_out // cols),
        in_specs=[spec],
        out_specs=spec,
        out_shape=jax.ShapeDtypeStruct(w_up.shape, BF16),
        compiler_params=pltpu.CompilerParams(dimension_semantics=("parallel", "parallel"),
                                             vmem_limit_bytes=24 * MIB),
        name="w_up_prep",
    )(w_up)


def _pos_kernel(off_ref, ek_ref, rk_ref, pos_ref):
    ek = ek_ref[...]
    pos = rk_ref[...]
    for e in range(N_EXPERTS):
        pos = pos + jnp.where(ek == e, off_ref[e], 0)
    pos_ref[...] = pos


def _positions(off, ek, rk):
    n = ek.shape[1]
    tn = min(n, 2048)
    spec = pl.BlockSpec((8, tn), lambda i, off: (0, i))
    return pl.pallas_call(
        _pos_kernel,
        grid_spec=pltpu.PrefetchScalarGridSpec(num_scalar_prefetch=1, grid=(n // tn,), in_specs=[spec, spec],
                                               out_specs=spec),
        out_shape=jax.ShapeDtypeStruct((8, n), I32),
        name="moe_pos",
    )(off, ek, rk)


def _sc_mesh():
    return plsc.VectorSubcoreMesh(core_axis_name="core", subcore_axis_name="subcore")


def _sc_scatter(x, pos_flat, n_rows):
    n = x.shape[0]
    nw = n // SC_WINDOW

    @functools.partial(pl.kernel, out_type=jax.ShapeDtypeStruct((n_rows, x.shape[1]), x.dtype), mesh=_sc_mesh())
    def scatter(x_hbm, p_hbm, o_hbm):
        def body(x_vmem, *p_vmem):
            for p in p_vmem:
                pltpu.sync_copy(x_vmem, o_hbm.at[p.at[0]])

        pltpu.emit_pipeline(
            body, grid=(nw,),
            in_specs=[pl.BlockSpec((SC_WINDOW, x.shape[1]), lambda i: (i, 0))] +
                     [pl.BlockSpec((1, SC_WINDOW), functools.partial(lambda k, i: (0, k * nw + i), k))
                      for k in range(MOE_TOP_K)],
            out_specs=[], core_axis_name=("core", "subcore"), dimension_semantics=(pltpu.PARALLEL,),
        )(x_hbm, *([p_hbm] * MOE_TOP_K))

    return scatter(x, pos_flat)


def _sc_gather(y, pos_flat):
    m = pos_flat.shape[1]

    @functools.partial(pl.kernel, out_type=jax.ShapeDtypeStruct((m, y.shape[1]), y.dtype), mesh=_sc_mesh())
    def gather(y_hbm, p_hbm, o_hbm):
        def body(p_vmem, o_vmem):
            pltpu.sync_copy(y_hbm.at[p_vmem.at[0]], o_vmem)

        pltpu.emit_pipeline(
            body, grid=(m // SC_WINDOW,),
            in_specs=[pl.BlockSpec((1, SC_WINDOW), lambda i: (0, i))],
            out_specs=[pl.BlockSpec((SC_WINDOW, y.shape[1]), lambda i: (i, 0))],
            core_axis_name=("core", "subcore"), dimension_semantics=(pltpu.PARALLEL,),
        )(p_hbm, o_hbm)

    return gather(y, pos_flat)


def _ffn_kernel(be_ref, nu_ref, xlo_ref, xhi_ref, wup_ref, bup_ref, wdn_ref, bdn_ref, ylo_ref, yhi_ref):
    @pl.when(pl.program_id(0) < nu_ref[0])
    def _():
        x = _unpack_rows(xlo_ref[...], xhi_ref[...]).astype(BF16)
        h = _mm(x, wup_ref[0]) + bup_ref[0]
        acts = []
        for b in range(2 * D_FF // UP_BLOCK):
            glu = jnp.minimum(h[:, b * UP_BLOCK:b * UP_BLOCK + LANES], SWIGLU_LIMIT)
            lin = jnp.clip(h[:, b * UP_BLOCK + LANES:(b + 1) * UP_BLOCK], -SWIGLU_LIMIT, SWIGLU_LIMIT)
            acts.append(glu * jax.nn.sigmoid(SWIGLU_ALPHA * glu) * (lin + 1.0))
        act = jnp.concatenate(acts, axis=1)
        ylo_ref[...], yhi_ref[...] = _pack_rows(_mm(act.astype(BF16), wdn_ref[0]) + bdn_ref[0])


def _ffn(block_expert, n_used, xs_lo, xs_hi, wup, bup, wdn, bdn):
    rows = xs_lo.shape[0]
    blk = FFN_BLOCK
    row = pl.BlockSpec((blk, QUARTER), lambda i, be, nu: (i, 0))
    per_expert = lambda a: pl.BlockSpec((1,) + a.shape[1:], lambda i, be, nu: (be[i], 0, 0))
    return pl.pallas_call(
        _ffn_kernel,
        grid_spec=pltpu.PrefetchScalarGridSpec(
            num_scalar_prefetch=2, grid=(rows // blk,),
            in_specs=[row, row, per_expert(wup), per_expert(bup), per_expert(wdn), per_expert(bdn)],
            out_specs=[row, row]),
        out_shape=[jax.ShapeDtypeStruct((rows, QUARTER), I32), jax.ShapeDtypeStruct((rows, QUARTER), I32)],
        compiler_params=pltpu.CompilerParams(dimension_semantics=("arbitrary",), vmem_limit_bytes=40 * MIB),
        name="moe_ffn",
    )(block_expert, n_used, xs_lo, xs_hi, wup, bup, wdn, bdn)


def _combine_kernel(h1_ref, olo_ref, ohi_ref, gtm_ref, g2_ref, b2_ref, o_ref):
    g = gtm_ref[...]
    y = jnp.zeros(h1_ref.shape, F32)
    for k in range(MOE_TOP_K):
        y = y + g[:, k:k + 1] * _unpack_rows(olo_ref[k], ohi_ref[k])
    o_ref[...] = _layer_norm(DEEPNORM_ALPHA * h1_ref[...] + y, g2_ref[...], b2_ref[...])


def _combine(h1, og_lo, og_hi, gtm, g2, b2, row0):
    n = h1.shape[0]
    n_all = og_lo.shape[0] // MOE_TOP_K
    tm = MERGE_TM
    row = lambda w: pl.BlockSpec((tm, w), lambda i: (i, 0))
    const = lambda a: pl.BlockSpec(a.shape, lambda i: (0,) * a.ndim)
    picked = pl.BlockSpec((MOE_TOP_K, tm, QUARTER), lambda i: (0, i + row0 // tm, 0))
    return pl.pallas_call(
        _combine_kernel,
        grid=(n // tm,),
        in_specs=[row(D_MODEL), picked, picked, row(LANES), const(g2), const(b2)],
        out_specs=row(D_MODEL),
        out_shape=jax.ShapeDtypeStruct((n, D_MODEL), F32),
        compiler_params=pltpu.CompilerParams(dimension_semantics=("parallel",), vmem_limit_bytes=40 * MIB),
        name="moe_combine",
    )(h1, og_lo.reshape(MOE_TOP_K, n_all, QUARTER), og_hi.reshape(MOE_TOP_K, n_all, QUARTER), gtm, g2, b2)


def _moe(groups, totals, wup, bup, wdn, bdn, g2, b2):
    blk = FFN_BLOCK
    sizes = [g[0].shape[0] for g in groups]
    n_all = sum(sizes)
    n_rows = MOE_TOP_K * n_all + N_EXPERTS * blk
    counts = totals[:, 0].astype(I32)
    padded = (counts + blk - 1) // blk * blk
    ends = jnp.cumsum(padded)
    block_start = jnp.arange(n_rows // blk, dtype=I32) * blk
    block_expert = jnp.minimum(jnp.sum((ends[None, :] <= block_start[:, None]).astype(I32), axis=1), N_EXPERTS - 1)
    n_used = (ends[-1:] // blk).astype(I32)
    cat = lambda i, axis: jnp.concatenate([g[i] for g in groups], axis=axis)
    pos = _positions((ends - padded).astype(I32), cat(4, 1), cat(5, 1))
    pos_flat = pos[:MOE_TOP_K].reshape(1, MOE_TOP_K * n_all)
    xs_lo = _sc_scatter(cat(1, 0), pos_flat, n_rows)
    xs_hi = _sc_scatter(cat(2, 0), pos_flat, n_rows)
    ys_lo, ys_hi = _ffn(block_expert, n_used, xs_lo, xs_hi, wup, bup, wdn, bdn)
    og_lo, og_hi = _sc_gather(ys_lo, pos_flat), _sc_gather(ys_hi, pos_flat)
    starts = np.cumsum([0] + sizes[:-1]).tolist()
    return [_combine(g[0], og_lo, og_hi, g[3], g2, b2, r0) for g, r0 in zip(groups, starts)]


def _mixer(x, pos_tab, tab_period, s0, rows_per_stream, ret_rows, dsa_keys, weights, cnt0):
    wp, wr, wd, wo, g1, b1, rwt, rb = weights
    n = x.shape[0]
    n_streams = n // rows_per_stream
    rq, rk, rv, rg, aq, ak, av, iq, ikw, gr, ga = _project(x, wp, pos_tab, tab_period)
    ret, s_new = _retention(rq, rk, rv, rg, s0, rows_per_stream, ret_rows)
    if dsa_keys is None:
        per_stream = lambda a: a.reshape(n_streams, rows_per_stream, a.shape[1])
        od = _dsa(aq, iq, ikw, per_stream(ak), per_stream(av), per_stream(ikw), DSA_TQ, True, rows_per_stream)
    else:
        kk, vv, ikk, limit = dsa_keys(ak, av, ikw)
        od = _dsa(aq, iq, ikw, kk, vv, ikk, rows_per_stream, False, limit)
    h1, h_lo, h_hi, gtm, ek, rk_, totals = _merge(x, ret, od, gr, ga, wr, wd, wo, g1, b1, rwt, rb, cnt0)
    return (h1, h_lo, h_hi, gtm, ek, rk_), totals, (s_new, ak, av, ikw[:, :IDX_DIM])


def kernel(x_prompt, x_sample, state_ret, cache_k, cache_v, cache_idx_k, w_in, w_ret_o, w_dsa_o, w_o,
           ln1_g, ln1_b, router_w, router_b, w_up, b_up, w_down, b_down, ln2_g, ln2_b):
    assert w_in.shape[0] == DEPTH
    batch, seq, _ = x_prompt.shape
    dec_batch, dec_seq, _ = x_sample.shape
    past = cache_k.shape[2]
    assert seq % PROJ_TM == 0 and seq % DSA_TQ == 0 and PROJ_TM % dec_seq == 0

    l = 0
    mixer_w = (_pack_w_in(w_in[l]), w_ret_o[l].astype(BF16), w_dsa_o[l].astype(BF16), w_o[l].astype(BF16),
               ln1_g[l][None, :], ln1_b[l][None, :], router_w[l].T, router_b[l][:, None])
    moe_w = (
        _deinterleave_w_up(w_up[l]),
        b_up[l].reshape(N_EXPERTS, 2 * D_FF // UP_BLOCK, LANES, 2).transpose(0, 1, 3, 2).reshape(N_EXPERTS, 1, 2 * D_FF),
        w_down[l].astype(BF16), b_down[l][:, None, :], ln2_g[l][None, :], ln2_b[l][None, :])

    tab_p = _rot_tables(jnp.arange(seq))
    zeros_state = jnp.zeros((batch, RET_HEADS, RET_DK, RET_DV), F32)
    moe_p, totals_p, (s_p, k_p, v_p, ik_p) = _mixer(
        x_prompt.reshape(batch * seq, D_MODEL), tab_p, seq // PROJ_TM, zeros_state, seq, RET_CHUNK, None, mixer_w,
        jnp.zeros((N_EXPERTS, LANES), F32))

    n_keys_real = past + dec_seq
    n_keys = -(-n_keys_real // LANES) * LANES
    tab_s = jnp.tile(_rot_tables(past + jnp.arange(dec_seq)), (PROJ_TM // dec_seq, 1))

    def sample_keys(ak, av, ikw):
        def cat(cache, new, width):
            padz = jnp.zeros((dec_batch, n_keys - n_keys_real, width), F32)
            return jnp.concatenate([cache.reshape(dec_batch, past, width), new.reshape(dec_batch, dec_seq, width), padz],
                                   axis=1)
        return (cat(cache_k[l], ak, LANES), cat(cache_v[l], av, LANES),
                cat(cache_idx_k[l], ikw[:, :IDX_DIM], IDX_DIM), n_keys_real)

    moe_s, totals, (s_s, k_s, v_s, ik_s) = _mixer(
        x_sample.reshape(dec_batch * dec_seq, D_MODEL), tab_s, 1, state_ret[l], dec_seq, dec_seq, sample_keys, mixer_w,
        totals_p)

    y_p, y_s = _moe([moe_p, moe_s], totals, *moe_w)

    kv = (DSA_KV_HEADS, DSA_HEAD_DIM)
    return (y_p.reshape(batch, seq, D_MODEL), y_s.reshape(dec_batch, dec_seq, D_MODEL),
            s_p[None], k_p.reshape(1, batch, seq, *kv), v_p.reshape(1, batch, seq, *kv),
            ik_p.reshape(1, batch, seq, IDX_DIM),
            s_s[None], k_s.reshape(1, dec_batch, dec_seq, *kv), v_s.reshape(1, dec_batch, dec_seq, *kv),
            ik_s.reshape(1, dec_batch, dec_seq, IDX_DIM))
```

```python
import functools

import numpy as np
import jax
import jax.numpy as jnp
from jax import lax
from jax.experimental import pallas as pl
from jax.experimental.pallas import tpu as pltpu
from jax.experimental.pallas import tpu_sc as plsc

F32 = jnp.float32
BF16 = jnp.bfloat16
I32 = jnp.int32

D_MODEL = 1024
CHUNK = 64
RET_HEADS = 4
RET_DK = 128
RET_DV = 256
RET_ROPE_BASE = 10000.0
DSA_HEADS = 8
DSA_KV_HEADS = 2
DSA_HEAD_DIM = 64
IDX_HEADS = 8
IDX_DIM = 64
DSA_TOPK = 256
ROPE_THETA = 500000.0
N_EXPERTS = 32
MOE_TOP_K = 4
D_FF = 1024
SWIGLU_ALPHA = 1.702
SWIGLU_LIMIT = 7.0
LN_EPS = 1e-5
GN_EPS = 1e-6
DEPTH = 1
DEEPNORM_ALPHA = (2.0 * DEPTH) ** 0.25
PROJ_WIDTHS = (RET_HEADS * RET_DK, RET_HEADS * RET_DK, RET_HEADS * RET_DV, RET_HEADS * RET_DV,
               DSA_HEADS * DSA_HEAD_DIM, DSA_KV_HEADS * DSA_HEAD_DIM, DSA_KV_HEADS * DSA_HEAD_DIM,
               IDX_HEADS * IDX_DIM, IDX_DIM, IDX_HEADS, D_MODEL, D_MODEL)

LANES = 128
MIB = 1024 * 1024

OFF_RQ, OFF_RK, OFF_RV, OFF_RG = 0, 512, 1024, 2048
OFF_AQ, OFF_AK, OFF_AV, OFF_IQ, OFF_IKW = 3072, 3584, 3712, 3840, 4352
OFF_GR, OFF_GA, PACKED_COLS = 4480, 5504, 6528
TAB_COLS = 8 * LANES

PROJ_TM = 512
RET_CHUNK = 128
RET_CHUNKS_PER_STEP = 8
DSA_TQ = 256
DSA_ROWS = 512
DSA_GROUP_MAX = 4
SEARCH_UNROLL = 4
MERGE_TM = 512
FFN_BLOCK = 512
SC_WINDOW = 128
QUARTER = D_MODEL // 4
UP_BLOCK = 2 * LANES


def _nt(a, b):
    return lax.dot_general(a, b, (((1,), (1,)), ((), ())), preferred_element_type=F32)


def _mm(a, b):
    return jnp.dot(a, b, preferred_element_type=F32)


def _proj_kernel(x_ref, w_ref, tab_ref, rq_ref, rk_ref, rv_ref, rg_ref, aq_ref, ak_ref, av_ref,
                 iq_ref, ikw_ref, gr_ref, ga_ref):
    xb = x_ref[...].astype(BF16)

    def mm(c0, n):
        return _mm(xb, w_ref[:, c0:c0 + n])

    def tab(i):
        return tab_ref[:, i * LANES:(i + 1) * LANES]

    def rot_full(z):
        return z * tab(0) + pltpu.roll(z, 64, 1) * tab(1)

    def rot_part(z, c):
        return z * tab(c) + pltpu.roll(z, LANES - 8, 1) * tab(c + 1) + pltpu.roll(z, 8, 1) * tab(c + 2)

    z = mm(OFF_RQ, 512)
    for h in range(4):
        sl = slice(h * LANES, (h + 1) * LANES)
        rq_ref[:, sl] = rot_full(z[:, sl]).astype(BF16)
    z = mm(OFF_RK, 512)
    for h in range(4):
        sl = slice(h * LANES, (h + 1) * LANES)
        rk_ref[:, sl] = (rot_full(z[:, sl]) * (RET_DK ** -0.5)).astype(BF16)
    for c in range(2):
        rv_ref[:, c * 512:(c + 1) * 512] = mm(OFF_RV + c * 512, 512).astype(BF16)
    for c in range(2):
        rg_ref[:, c * 512:(c + 1) * 512] = mm(OFF_RG + c * 512, 512)
    z = mm(OFF_AQ, 512)
    for h in range(4):
        sl = slice(h * LANES, (h + 1) * LANES)
        aq_ref[:, sl] = rot_part(z[:, sl], 2).astype(BF16)
    z = mm(OFF_AK, 256)
    ak_ref[...] = rot_part(z[:, :LANES], 2)
    av_ref[...] = z[:, LANES:]
    z = mm(OFF_IQ, 512)
    for h in range(4):
        sl = slice(h * LANES, (h + 1) * LANES)
        iq_ref[:, sl] = rot_part(z[:, sl], 2).astype(BF16)
    ikw_ref[...] = rot_part(mm(OFF_IKW, LANES), 5)
    for c in range(2):
        gr_ref[:, c * 512:(c + 1) * 512] = mm(OFF_GR + c * 512, 512)
    for c in range(2):
        ga_ref[:, c * 512:(c + 1) * 512] = mm(OFF_GA + c * 512, 512)


def _rot_tables(pos):
    p = pos.shape[0]
    posf = pos.astype(F32)[:, None]
    ret_f = RET_ROPE_BASE ** (-jnp.linspace(0.0, 1.0, RET_DK // 2, dtype=F32))
    ang = posf * ret_f[None, :]
    c, s = jnp.cos(ang), jnp.sin(ang)
    cos_r = jnp.concatenate([c, c], 1)
    sin_r = jnp.concatenate([-s, s], 1)
    n_rot = DSA_HEAD_DIM // 4
    att_f = ROPE_THETA ** (-jnp.arange(0, n_rot, 2, dtype=F32) / n_rot)
    ang2 = posf * att_f[None, :]
    c2, s2 = jnp.cos(ang2), jnp.sin(ang2)
    half = n_rot // 2
    rest = DSA_HEAD_DIM - 2 * half
    c64 = jnp.concatenate([c2, c2, jnp.ones((p, rest), F32)], 1)
    s1_64 = jnp.concatenate([-s2, jnp.zeros((p, DSA_HEAD_DIM - half), F32)], 1)
    s2_64 = jnp.concatenate([jnp.zeros((p, half), F32), s2, jnp.zeros((p, rest), F32)], 1)
    z64 = jnp.zeros((p, DSA_HEAD_DIM), F32)
    ci = jnp.concatenate([c64, jnp.full((p, IDX_HEADS), IDX_HEADS ** -0.5, F32),
                          jnp.zeros((p, DSA_HEAD_DIM - IDX_HEADS), F32)], 1)
    return jnp.concatenate([cos_r, sin_r,
                            jnp.concatenate([c64, c64], 1), jnp.concatenate([s1_64, s1_64], 1),
                            jnp.concatenate([s2_64, s2_64], 1),
                            ci, jnp.concatenate([s1_64, z64], 1), jnp.concatenate([s2_64, z64], 1)], 1)


def _pack_w_in(w):
    cuts = np.cumsum(PROJ_WIDTHS)[:-1].tolist()
    rq, rk, rv, rg, aq, ak, av, iq, ik, iw, gr, ga = jnp.split(w, cuts, axis=1)
    pad = jnp.zeros((w.shape[0], LANES - IDX_DIM - IDX_HEADS), w.dtype)
    return jnp.concatenate([rq, rk, rv, rg, aq, ak, av, iq, ik, iw, pad, gr, ga], axis=1).astype(BF16)


def _project(x, wp, tab, tab_period):
    n = x.shape[0]
    tm = PROJ_TM
    row = lambda w: pl.BlockSpec((tm, w), lambda i: (i, 0))
    out_shapes = [((n, 512), BF16), ((n, 512), BF16), ((n, 1024), BF16), ((n, 1024), F32),
                  ((n, 512), BF16), ((n, LANES), F32), ((n, LANES), F32), ((n, 512), BF16),
                  ((n, LANES), F32), ((n, 1024), F32), ((n, 1024), F32)]
    return pl.pallas_call(
        _proj_kernel,
        grid=(n // tm,),
        in_specs=[row(D_MODEL),
                  pl.BlockSpec((D_MODEL, PACKED_COLS), lambda i: (0, 0), pipeline_mode=pl.Buffered(1)),
                  pl.BlockSpec((tm, TAB_COLS), lambda i: (i % tab_period, 0))],
        out_specs=[row(s[1]) for s, _ in out_shapes],
        out_shape=[jax.ShapeDtypeStruct(s, d) for s, d in out_shapes],
        compiler_params=pltpu.CompilerParams(dimension_semantics=("parallel",), vmem_limit_bytes=52 * MIB),
        name="proj",
    )(x, wp, tab)


def _ret_kernel(dec_ref, xi_ref, zeta_ref, rq_ref, rk_ref, rv_ref, rg_ref, s0_ref, ret_ref, sout_ref, st_ref,
                *, rows, n_chunk, g_pow):
    j = pl.program_id(1)
    cpad = RET_CHUNK

    @pl.when(j == 0)
    def _():
        st_ref[...] = s0_ref[0]

    def padded(v):
        if rows == cpad:
            return v
        return jnp.concatenate([v, jnp.zeros((cpad - rows, v.shape[1]), v.dtype)], axis=0)

    for c in range(n_chunk):
        rs = slice(c * rows, (c + 1) * rows)
        for h in range(RET_HEADS):
            ks = slice(h * RET_DK, (h + 1) * RET_DK)
            vs = slice(h * RET_DV, (h + 1) * RET_DV)
            q = padded(rq_ref[rs, ks])
            kt = padded(rk_ref[rs, ks].astype(F32)).T
            v = padded(rv_ref[rs, vs])
            s = st_ref[h]
            sc = _mm(q, kt.astype(BF16)) * dec_ref[h]
            o = _mm(sc.astype(BF16), v) + _mm(q, s.astype(BF16)) * xi_ref[h]
            st_ref[h] = g_pow[h] * s + _mm((kt * zeta_ref[h]).astype(BF16), v)
            o = o[:rows]
            mu = jnp.mean(o, axis=-1, keepdims=True)
            d = o - mu
            var = jnp.mean(d * d, axis=-1, keepdims=True)
            gn = d * lax.rsqrt(var + GN_EPS)
            g = rg_ref[rs, vs]
            ret_ref[rs, vs] = (gn * (g * jax.nn.sigmoid(g))).astype(BF16)

    @pl.when(j == pl.num_programs(1) - 1)
    def _():
        sout_ref[0] = st_ref[...]


def _retention(rq, rk, rv, rg, s0, rows_per_stream, rows):
    n = rq.shape[0]
    n_streams = n // rows_per_stream
    n_chunk = min(rows_per_stream // rows, RET_CHUNKS_PER_STEP)
    blk = rows * n_chunk
    nb = rows_per_stream // blk
    gam = 1.0 - 2.0 ** (-5.0 - np.arange(RET_HEADS, dtype=np.float64))
    i = np.arange(RET_CHUNK, dtype=np.float64)
    diff = i[:, None] - i[None, :]
    dec = np.where(diff >= 0, gam[:, None, None] ** np.maximum(diff, 0.0)[None], 0.0)
    xi = gam[:, None, None] ** (i + 1.0)[None, :, None]
    zeta = np.where(i < rows, gam[:, None, None] ** (rows - 1.0 - i)[None, None, :], 0.0)
    g_pow = tuple(float(g ** rows) for g in gam)
    const = lambda shape: pl.BlockSpec(shape, lambda s, j: (0,) * len(shape))
    row = lambda w: pl.BlockSpec((blk, w), lambda s, j: (s * nb + j, 0))
    st = pl.BlockSpec((1, RET_HEADS, RET_DK, RET_DV), lambda s, j: (s, 0, 0, 0))
    return pl.pallas_call(
        functools.partial(_ret_kernel, rows=rows, n_chunk=n_chunk, g_pow=g_pow),
        grid=(n_streams, nb),
        in_specs=[const((RET_HEADS, RET_CHUNK, RET_CHUNK)), const((RET_HEADS, RET_CHUNK, 1)),
                  const((RET_HEADS, 1, RET_CHUNK)), row(512), row(512), row(1024), row(1024), st],
        out_specs=[row(1024), st],
        out_shape=[jax.ShapeDtypeStruct((n, RET_HEADS * RET_DV), BF16),
                   jax.ShapeDtypeStruct((n_streams, RET_HEADS, RET_DK, RET_DV), F32)],
        scratch_shapes=[pltpu.VMEM((RET_HEADS, RET_DK, RET_DV), F32)],
        compiler_params=pltpu.CompilerParams(dimension_semantics=("parallel", "arbitrary"),
                                             vmem_limit_bytes=32 * MIB),
        name="retention",
    )(jnp.asarray(dec, F32), jnp.asarray(xi, F32), jnp.asarray(zeta, F32), rq, rk, rv, rg, s0)


def _dsa_kernel(aq_ref, iq_ref, ikwq_ref, kk_ref, vv_ref, ikk_ref, o_ref, w_ref, bias_ref,
                *, group, tq, n_keys, row0, limit_const, n_sel):
    rows = group * tq
    nsel_f = float(n_sel)
    neg_inf = -jnp.inf

    col = lax.broadcasted_iota(I32, (tq, n_keys), 1)
    if limit_const is None:
        limit = (jnp.right_shift(lax.broadcasted_iota(I32, (tq, 1), 0) + row0, 6) + 1) * CHUNK
    else:
        limit = limit_const

    def score_stream(s, carry):
        ikb = ikk_ref[s][:, :IDX_DIM].astype(BF16)
        iww = ikwq_ref[s][:, IDX_DIM:IDX_DIM + IDX_HEADS] * (IDX_DIM ** -0.5)
        acc = jnp.zeros((tq, n_keys), F32)
        for p in range(IDX_HEADS // 2):
            slab = iq_ref[s, :, p * LANES:(p + 1) * LANES]
            for hh in range(2):
                h = 2 * p + hh
                acc = acc + jnp.maximum(_nt(slab[:, hh * IDX_DIM:(hh + 1) * IDX_DIM], ikb), 0.0) * iww[:, h:h + 1]
        w_ref[pl.ds(pl.multiple_of(s * tq, tq), tq), :] = jnp.where(col < limit, acc, neg_inf)
        return carry

    lax.fori_loop(0, group, score_stream, 0)

    sc = w_ref[...]
    pos = jnp.sum(jnp.where(sc >= 0.0, 1.0, 0.0), axis=1, keepdims=True) >= nsel_f
    kk = jnp.where(pos, nsel_f, float(n_keys - n_sel + 1))
    w_ref[...] = jnp.where(pos, sc, -sc)

    def bit_step(i, u):
        cand_u = u | jnp.left_shift(jnp.int32(1), 30 - i)
        cand = pltpu.bitcast(cand_u, F32)
        cnt = jnp.sum(jnp.where(w_ref[...] >= cand, 1.0, 0.0), axis=1, keepdims=True)
        return jnp.where(cnt >= kk, cand_u, u)

    mag_u = lax.fori_loop(0, 31, bit_step, jnp.zeros((rows, 1), I32), unroll=SEARCH_UNROLL)
    mag = pltpu.bitcast(mag_u, F32)
    thr = jnp.where(pos, mag, -mag)

    sc = jnp.where(pos, w_ref[...], -w_ref[...])
    short = jnp.sum(jnp.where(sc >= thr, 1.0, 0.0), axis=1, keepdims=True) < nsel_f
    thr = jnp.where(jnp.logical_and(short, jnp.logical_not(pos)), -pltpu.bitcast(mag_u + 1, F32), thr)
    ge = sc >= thr
    cnt_gt = jnp.sum(jnp.where(sc > thr, 1.0, 0.0), axis=1, keepdims=True)
    cnt_ge = jnp.sum(jnp.where(ge, 1.0, 0.0), axis=1, keepdims=True)
    bias_ref[...] = jnp.where(jnp.logical_and(ge, sc > neg_inf), 0.0, neg_inf)
    excess = jnp.logical_and(cnt_ge > nsel_f, thr > neg_inf)

    @pl.when(jnp.max(jnp.where(excess, 1.0, 0.0)) > 0.0)
    def _():
        need = nsel_f - cnt_gt
        tri = jnp.where(lax.broadcasted_iota(I32, (LANES, LANES), 0) < lax.broadcasted_iota(I32, (LANES, LANES), 1),
                        1.0, 0.0).astype(BF16)
        before = jnp.zeros((rows, 1), F32)
        for b in range(n_keys // LANES):
            sl = slice(b * LANES, (b + 1) * LANES)
            sblk = jnp.where(pos, w_ref[:, sl], -w_ref[:, sl])
            eq = jnp.where(sblk == thr, 1.0, 0.0)
            rank = _mm(eq.astype(BF16), tri) + before
            keep = jnp.logical_or(sblk > thr, jnp.logical_and(sblk == thr, rank < need))
            bias_ref[:, sl] = jnp.where(jnp.logical_and(keep, sblk > neg_inf), 0.0, neg_inf)
            before = before + jnp.sum(eq, axis=1, keepdims=True)

    def attend_stream(s, carry):
        bias = bias_ref[pl.ds(pl.multiple_of(s * tq, tq), tq), :]
        kfull = kk_ref[s]
        vfull = vv_ref[s]
        for g in range(DSA_KV_HEADS):
            gs = slice(g * DSA_HEAD_DIM, (g + 1) * DSA_HEAD_DIM)
            kg = kfull[:, gs].astype(BF16)
            vg = vfull[:, gs].astype(BF16)
            for pp in range(2):
                p = 2 * g + pp
                slab = (aq_ref[s, :, p * LANES:(p + 1) * LANES].astype(F32) * (DSA_HEAD_DIM ** -0.5)).astype(BF16)
                outs = []
                for hh in range(2):
                    lg = _nt(slab[:, hh * DSA_HEAD_DIM:(hh + 1) * DSA_HEAD_DIM], kg) + bias
                    m = jnp.max(lg, axis=1, keepdims=True)
                    pr = jnp.exp(lg - m)
                    den = jnp.sum(pr, axis=1, keepdims=True)
                    outs.append(_mm(pr.astype(BF16), vg) / den)
                o_ref[s, :, p * LANES:(p + 1) * LANES] = jnp.concatenate(outs, axis=1).astype(BF16)
        return carry

    lax.fori_loop(0, group, attend_stream, 0)


def _dsa_call(aq, iq, ikw, kk, vv, ikk, group, tq, jq, n_keys, limit_const, n_sel, name):
    n_streams = kk.shape[0]
    qrow = lambda a: pl.BlockSpec((group, tq, a.shape[2]), lambda s: (s, jq, 0))
    krow = lambda a: pl.BlockSpec((group, n_keys, a.shape[2]), lambda s: (s, 0, 0))
    width = DSA_HEADS * DSA_HEAD_DIM
    return pl.pallas_call(
        functools.partial(_dsa_kernel, group=group, tq=tq, n_keys=n_keys, row0=jq * tq, limit_const=limit_const,
                          n_sel=n_sel),
        grid=(n_streams // group,),
        in_specs=[qrow(aq), qrow(iq), qrow(ikw), krow(kk), krow(vv), krow(ikk)],
        out_specs=pl.BlockSpec((group, tq, width), lambda s: (s, 0, 0)),
        out_shape=jax.ShapeDtypeStruct((n_streams, tq, width), BF16),
        scratch_shapes=[pltpu.VMEM((group * tq, n_keys), F32), pltpu.VMEM((group * tq, n_keys), F32)],
        compiler_params=pltpu.CompilerParams(dimension_semantics=("parallel",), vmem_limit_bytes=56 * MIB),
        name=name,
    )(aq, iq, ikw, kk, vv, ikk)


def _dsa(aq, iq, ikw, kk, vv, ikk, tq, chunk_causal, limit_const):
    n_streams, n_keys, _ = kk.shape
    n = aq.shape[0]
    per_stream = lambda a: a.reshape(n_streams, n // n_streams, a.shape[1])
    aq, iq, ikw = per_stream(aq), per_stream(iq), per_stream(ikw)
    nq = n // n_streams // tq
    n_sel = min(DSA_TOPK, limit_const // 4)
    group = max(1, min(n_streams, DSA_ROWS // tq, DSA_GROUP_MAX))
    if not chunk_causal:
        assert nq == 1
        return _dsa_call(aq, iq, ikw, kk, vv, ikk, group, tq, 0, n_keys, limit_const, n_sel, "dsa_s").reshape(n, -1)
    parts = [_dsa_call(aq, iq, ikw, kk, vv, ikk, group, tq, jq, (jq + 1) * tq, None, n_sel, f"dsa_p{jq}")
             for jq in range(nq)]
    return jnp.stack(parts, 1).reshape(n, -1)


def _pack_rows(v):
    q = QUARTER
    bits = lambda x: pltpu.bitcast(x.astype(BF16).astype(F32), I32)
    pair = lambda c: lax.shift_right_logical(bits(v[:, c * q:(c + 1) * q]), 16) | bits(v[:, (c + 1) * q:(c + 2) * q])
    return pair(0), pair(2)


def _unpack_rows(lo, hi):
    parts = []
    for w in (lo, hi):
        parts.append(pltpu.bitcast(lax.shift_left(w, 16), F32))
        parts.append(pltpu.bitcast(w & jnp.int32(-65536), F32))
    return jnp.concatenate(parts, axis=1)


def _layer_norm(v, g, b):
    mu = jnp.mean(v, axis=-1, keepdims=True)
    d = v - mu
    var = jnp.mean(d * d, axis=-1, keepdims=True)
    return d * lax.rsqrt(var + LN_EPS) * g + b


def _merge_kernel(x_ref, ret_ref, od_ref, gr_ref, ga_ref, wr_ref, wd_ref, wo_ref, g1_ref, b1_ref, rwt_ref, rb_ref,
                  tri_ref, cnt0_ref, h1_ref, hlo_ref, hhi_ref, gtm_ref, ek_ref, rk_ref, tot_ref, cnt_ref):
    @pl.when(pl.program_id(0) == 0)
    def _():
        cnt_ref[...] = cnt0_ref[:, 0:1]

    y_ret = _mm(ret_ref[...], wr_ref[...])
    y_dsa = _mm(od_ref[...], wd_ref[...])
    merged = jax.nn.sigmoid(gr_ref[...]) * y_ret + jax.nn.sigmoid(ga_ref[...]) * y_dsa
    mix = _mm(merged.astype(BF16), wo_ref[...])
    h1 = _layer_norm(DEEPNORM_ALPHA * x_ref[...] + mix, g1_ref[...], b1_ref[...])
    h1_ref[...] = h1
    hlo_ref[...], hhi_ref[...] = _pack_rows(h1)

    logits = lax.dot_general(rwt_ref[...], h1, (((1,), (1,)), ((), ())), preferred_element_type=F32,
                             precision=lax.Precision.HIGHEST) + rb_ref[...]
    tm = logits.shape[1]
    e_iota = lax.broadcasted_iota(I32, (N_EXPERTS, tm), 0)
    tops, hots, firsts = [], [], []
    for _ in range(MOE_TOP_K):
        m = jnp.max(logits, axis=0, keepdims=True)
        first = jnp.min(jnp.where(logits == m, e_iota, N_EXPERTS), axis=0, keepdims=True)
        hot = e_iota == first
        tops.append(m)
        hots.append(hot)
        firsts.append(first)
        logits = jnp.where(hot, -jnp.inf, logits)
    exps = [jnp.exp(m - tops[0]) for m in tops]
    den = exps[0] + exps[1] + exps[2] + exps[3]
    sel = jnp.zeros((N_EXPERTS, tm), F32)
    for hot in hots:
        sel = sel + jnp.where(hot, 1.0, 0.0)
    rank = _mm(sel.astype(BF16), tri_ref[...]) + cnt_ref[...]
    ranks = [jnp.sum(jnp.where(hot, rank, 0.0), axis=0, keepdims=True).astype(I32) for hot in hots]
    pad_i = jnp.zeros((8 - MOE_TOP_K, tm), I32)
    ek_ref[...] = jnp.concatenate(firsts + [pad_i], axis=0)
    rk_ref[...] = jnp.concatenate(ranks + [pad_i], axis=0)
    gates = jnp.concatenate([e / den for e in exps] + [jnp.zeros((LANES - MOE_TOP_K, tm), F32)], axis=0)
    gtm_ref[...] = gates.T
    cnt_ref[...] = cnt_ref[...] + jnp.sum(sel, axis=1, keepdims=True)
    tot_ref[...] = jnp.broadcast_to(cnt_ref[...], tot_ref.shape)


def _merge(x, ret, od, gr, ga, wr, wd, wo, g1, b1, rwt, rb, cnt0):
    n = x.shape[0]
    tm = MERGE_TM
    row = lambda w: pl.BlockSpec((tm, w), lambda i: (i, 0))
    const = lambda a: pl.BlockSpec(a.shape, lambda i: (0,) * a.ndim)
    col = pl.BlockSpec((8, tm), lambda i: (0, i))
    tri = jnp.asarray(np.triu(np.ones((tm, tm), np.float32), 1), BF16)
    return pl.pallas_call(
        _merge_kernel,
        grid=(n // tm,),
        in_specs=[row(D_MODEL), row(1024), row(512), row(1024), row(1024), const(wr), const(wd), const(wo),
                  const(g1), const(b1), const(rwt), const(rb), const(tri), const(cnt0)],
        out_specs=[row(D_MODEL), row(QUARTER), row(QUARTER), row(LANES), col, col,
                   pl.BlockSpec((N_EXPERTS, LANES), lambda i: (0, 0))],
        out_shape=[jax.ShapeDtypeStruct((n, D_MODEL), F32), jax.ShapeDtypeStruct((n, QUARTER), I32),
                   jax.ShapeDtypeStruct((n, QUARTER), I32), jax.ShapeDtypeStruct((n, LANES), F32),
                   jax.ShapeDtypeStruct((8, n), I32), jax.ShapeDtypeStruct((8, n), I32),
                   jax.ShapeDtypeStruct((N_EXPERTS, LANES), F32)],
        scratch_shapes=[pltpu.VMEM((N_EXPERTS, 1), F32)],
        compiler_params=pltpu.CompilerParams(dimension_semantics=("arbitrary",), vmem_limit_bytes=48 * MIB),
        name="merge",
    )(x, ret, od, gr, ga, wr, wd, wo, g1, b1, rwt, rb, tri, cnt0)


def _deinterleave_kernel(w_ref, o_ref):
    r = lax.broadcasted_iota(I32, (UP_BLOCK, UP_BLOCK), 0)
    c = lax.broadcasted_iota(I32, (UP_BLOCK, UP_BLOCK), 1)
    src = jnp.where(c < LANES, 2 * c, 2 * (c - LANES) + 1)
    perm = jnp.where(r == src, 1.0, 0.0).astype(BF16)
    for b in range(w_ref.shape[2] // UP_BLOCK):
        sl = slice(b * UP_BLOCK, (b + 1) * UP_BLOCK)
        o_ref[0, :, sl] = _mm(w_ref[0, :, sl].astype(BF16), perm).astype(BF16)


def _deinterleave_w_up(w_up):
    n_e, d_in, d_out = w_up.shape
    cols = 512
    spec = pl.BlockSpec((1, d_in, cols), lambda e, c: (e, 0, c))
    return pl.pallas_call(
        _deinterleave_kernel,
        grid=(n_e, d_out // cols),
        in_specs=[spec],
        out_specs=spec,
        out_shape=jax.ShapeDtypeStruct(w_up.shape, BF16),
        compiler_params=pltpu.CompilerParams(dimension_semantics=("parallel", "parallel"),
                                             vmem_limit_bytes=24 * MIB),
        name="w_up_prep",
    )(w_up)


def _pos_kernel(off_ref, ek_ref, rk_ref, pos_ref):
    ek = ek_ref[...]
    pos = rk_ref[...]
    for e in range(N_EXPERTS):
        pos = pos + jnp.where(ek == e, off_ref[e], 0)
    pos_ref[...] = pos


def _positions(off, ek, rk):
    n = ek.shape[1]
    tn = min(n, 2048)
    spec = pl.BlockSpec((8, tn), lambda i, off: (0, i))
    return pl.pallas_call(
        _pos_kernel,
        grid_spec=pltpu.PrefetchScalarGridSpec(num_scalar_prefetch=1, grid=(n // tn,), in_specs=[spec, spec],
                                               out_specs=spec),
        out_shape=jax.ShapeDtypeStruct((8, n), I32),
        name="moe_pos",
    )(off, ek, rk)


def _sc_mesh():
    return plsc.VectorSubcoreMesh(core_axis_name="core", subcore_axis_name="subcore")


def _sc_scatter(x, pos_flat, n_rows):
    n = x.shape[0]
    nw = n // SC_WINDOW

    @functools.partial(pl.kernel, out_type=jax.ShapeDtypeStruct((n_rows, x.shape[1]), x.dtype), mesh=_sc_mesh())
    def scatter(x_hbm, p_hbm, o_hbm):
        def body(x_vmem, *p_vmem):
            for p in p_vmem:
                pltpu.sync_copy(x_vmem, o_hbm.at[p.at[0]])

        pltpu.emit_pipeline(
            body, grid=(nw,),
            in_specs=[pl.BlockSpec((SC_WINDOW, x.shape[1]), lambda i: (i, 0))] +
                     [pl.BlockSpec((1, SC_WINDOW), functools.partial(lambda k, i: (0, k * nw + i), k))
                      for k in range(MOE_TOP_K)],
            out_specs=[], core_axis_name=("core", "subcore"), dimension_semantics=(pltpu.PARALLEL,),
        )(x_hbm, *([p_hbm] * MOE_TOP_K))

    return scatter(x, pos_flat)


def _sc_gather(y, pos_flat):
    m = pos_flat.shape[1]

    @functools.partial(pl.kernel, out_type=jax.ShapeDtypeStruct((m, y.shape[1]), y.dtype), mesh=_sc_mesh())
    def gather(y_hbm, p_hbm, o_hbm):
        def body(p_vmem, o_vmem):
            pltpu.sync_copy(y_hbm.at[p_vmem.at[0]], o_vmem)

        pltpu.emit_pipeline(
            body, grid=(m // SC_WINDOW,),
            in_specs=[pl.BlockSpec((1, SC_WINDOW), lambda i: (0, i))],
            out_specs=[pl.BlockSpec((SC_WINDOW, y.shape[1]), lambda i: (i, 0))],
            core_axis_name=("core", "subcore"), dimension_semantics=(pltpu.PARALLEL,),
        )(p_hbm, o_hbm)

    return gather(y, pos_flat)


def _ffn_kernel(be_ref, nu_ref, xlo_ref, xhi_ref, wup_ref, bup_ref, wdn_ref, bdn_ref, ylo_ref, yhi_ref):
    @pl.when(pl.program_id(0) < nu_ref[0])
    def _():
        x = _unpack_rows(xlo_ref[...], xhi_ref[...]).astype(BF16)
        h = _mm(x, wup_ref[0]) + bup_ref[0]
        acts = []
        for b in range(2 * D_FF // UP_BLOCK):
            glu = jnp.minimum(h[:, b * UP_BLOCK:b * UP_BLOCK + LANES], SWIGLU_LIMIT)
            lin = jnp.clip(h[:, b * UP_BLOCK + LANES:(b + 1) * UP_BLOCK], -SWIGLU_LIMIT, SWIGLU_LIMIT)
            acts.append(glu * jax.nn.sigmoid(SWIGLU_ALPHA * glu) * (lin + 1.0))
        act = jnp.concatenate(acts, axis=1)
        ylo_ref[...], yhi_ref[...] = _pack_rows(_mm(act.astype(BF16), wdn_ref[0]) + bdn_ref[0])


def _ffn(block_expert, n_used, xs_lo, xs_hi, wup, bup, wdn, bdn):
    rows = xs_lo.shape[0]
    blk = FFN_BLOCK
    row = pl.BlockSpec((blk, QUARTER), lambda i, be, nu: (i, 0))
    per_expert = lambda a: pl.BlockSpec((1,) + a.shape[1:], lambda i, be, nu: (be[i], 0, 0))
    return pl.pallas_call(
        _ffn_kernel,
        grid_spec=pltpu.PrefetchScalarGridSpec(
            num_scalar_prefetch=2, grid=(rows // blk,),
            in_specs=[row, row, per_expert(wup), per_expert(bup), per_expert(wdn), per_expert(bdn)],
            out_specs=[row, row]),
        out_shape=[jax.ShapeDtypeStruct((rows, QUARTER), I32), jax.ShapeDtypeStruct((rows, QUARTER), I32)],
        compiler_params=pltpu.CompilerParams(dimension_semantics=("arbitrary",), vmem_limit_bytes=40 * MIB),
        name="moe_ffn",
    )(block_expert, n_used, xs_lo, xs_hi, wup, bup, wdn, bdn)


def _combine_kernel(h1_ref, olo_ref, ohi_ref, gtm_ref, g2_ref, b2_ref, o_ref):
    g = gtm_ref[...]
    y = jnp.zeros(h1_ref.shape, F32)
    for k in range(MOE_TOP_K):
        y = y + g[:, k:k + 1] * _unpack_rows(olo_ref[k], ohi_ref[k])
    o_ref[...] = _layer_norm(DEEPNORM_ALPHA * h1_ref[...] + y, g2_ref[...], b2_ref[...])


def _combine(h1, og_lo, og_hi, gtm, g2, b2, row0):
    n = h1.shape[0]
    n_all = og_lo.shape[0] // MOE_TOP_K
    tm = MERGE_TM
    row = lambda w: pl.BlockSpec((tm, w), lambda i: (i, 0))
    const = lambda a: pl.BlockSpec(a.shape, lambda i: (0,) * a.ndim)
    picked = pl.BlockSpec((MOE_TOP_K, tm, QUARTER), lambda i: (0, i + row0 // tm, 0))
    return pl.pallas_call(
        _combine_kernel,
        grid=(n // tm,),
        in_specs=[row(D_MODEL), picked, picked, row(LANES), const(g2), const(b2)],
        out_specs=row(D_MODEL),
        out_shape=jax.ShapeDtypeStruct((n, D_MODEL), F32),
        compiler_params=pltpu.CompilerParams(dimension_semantics=("parallel",), vmem_limit_bytes=40 * MIB),
        name="moe_combine",
    )(h1, og_lo.reshape(MOE_TOP_K, n_all, QUARTER), og_hi.reshape(MOE_TOP_K, n_all, QUARTER), gtm, g2, b2)


def _moe(groups, totals, wup, bup, wdn, bdn, g2, b2):
    blk = FFN_BLOCK
    sizes = [g[0].shape[0] for g in groups]
    n_all = sum(sizes)
    n_rows = MOE_TOP_K * n_all + N_EXPERTS * blk
    counts = totals[:, 0].astype(I32)
    padded = (counts + blk - 1) // blk * blk
    ends = jnp.cumsum(padded)
    block_start = jnp.arange(n_rows // blk, dtype=I32) * blk
    block_expert = jnp.minimum(jnp.sum((ends[None, :] <= block_start[:, None]).astype(I32), axis=1), N_EXPERTS - 1)
    n_used = (ends[-1:] // blk).astype(I32)
    cat = lambda i, axis: jnp.concatenate([g[i] for g in groups], axis=axis)
    pos = _positions((ends - padded).astype(I32), cat(4, 1), cat(5, 1))
    pos_flat = pos[:MOE_TOP_K].reshape(1, MOE_TOP_K * n_all)
    xs_lo = _sc_scatter(cat(1, 0), pos_flat, n_rows)
    xs_hi = _sc_scatter(cat(2, 0), pos_flat, n_rows)
    ys_lo, ys_hi = _ffn(block_expert, n_used, xs_lo, xs_hi, wup, bup, wdn, bdn)
    og_lo, og_hi = _sc_gather(ys_lo, pos_flat), _sc_gather(ys_hi, pos_flat)
    starts = np.cumsum([0] + sizes[:-1]).tolist()
    return [_combine(g[0], og_lo, og_hi, g[3], g2, b2, r0) for g, r0 in zip(groups, starts)]


def _mixer(x, pos_tab, tab_period, s0, rows_per_stream, ret_rows, dsa_keys, weights, cnt0):
    wp, wr, wd, wo, g1, b1, rwt, rb = weights
    n = x.shape[0]
    n_streams = n // rows_per_stream
    rq, rk, rv, rg, aq, ak, av, iq, ikw, gr, ga = _project(x, wp, pos_tab, tab_period)
    ret, s_new = _retention(rq, rk, rv, rg, s0, rows_per_stream, ret_rows)
    if dsa_keys is None:
        per_stream = lambda a: a.reshape(n_streams, rows_per_stream, a.shape[1])
        od = _dsa(aq, iq, ikw, per_stream(ak), per_stream(av), per_stream(ikw), DSA_TQ, True, rows_per_stream)
    else:
        kk, vv, ikk, limit = dsa_keys(ak, av, ikw)
        od = _dsa(aq, iq, ikw, kk, vv, ikk, rows_per_stream, False, limit)
    h1, h_lo, h_hi, gtm, ek, rk_, totals = _merge(x, ret, od, gr, ga, wr, wd, wo, g1, b1, rwt, rb, cnt0)
    return (h1, h_lo, h_hi, gtm, ek, rk_), totals, (s_new, ak, av, ikw[:, :IDX_DIM])


def kernel(x_prompt, x_sample, state_ret, cache_k, cache_v, cache_idx_k, w_in, w_ret_o, w_dsa_o, w_o,
           ln1_g, ln1_b, router_w, router_b, w_up, b_up, w_down, b_down, ln2_g, ln2_b):
    assert w_in.shape[0] == DEPTH
    batch, seq, _ = x_prompt.shape
    dec_batch, dec_seq, _ = x_sample.shape
    past = cache_k.shape[2]
    assert seq % PROJ_TM == 0 and seq % DSA_TQ == 0 and PROJ_TM % dec_seq == 0

    l = 0
    mixer_w = (_pack_w_in(w_in[l]), w_ret_o[l].astype(BF16), w_dsa_o[l].astype(BF16), w_o[l].astype(BF16),
               ln1_g[l][None, :], ln1_b[l][None, :], router_w[l].T, router_b[l][:, None])
    moe_w = (
        _deinterleave_w_up(w_up[l]),
        b_up[l].reshape(N_EXPERTS, 2 * D_FF // UP_BLOCK, LANES, 2).transpose(0, 1, 3, 2).reshape(N_EXPERTS, 1, 2 * D_FF),
        w_down[l].astype(BF16), b_down[l][:, None, :], ln2_g[l][None, :], ln2_b[l][None, :])

    tab_p = _rot_tables(jnp.arange(seq))
    zeros_state = jnp.zeros((batch, RET_HEADS, RET_DK, RET_DV), F32)
    moe_p, totals_p, (s_p, k_p, v_p, ik_p) = _mixer(
        x_prompt.reshape(batch * seq, D_MODEL), tab_p, seq // PROJ_TM, zeros_state, seq, RET_CHUNK, None, mixer_w,
        jnp.zeros((N_EXPERTS, LANES), F32))

    n_keys_real = past + dec_seq
    n_keys = -(-n_keys_real // LANES) * LANES
    tab_s = jnp.tile(_rot_tables(past + jnp.arange(dec_seq)), (PROJ_TM // dec_seq, 1))

    def sample_keys(ak, av, ikw):
        def cat(cache, new, width):
            padz = jnp.zeros((dec_batch, n_keys - n_keys_real, width), F32)
            return jnp.concatenate([cache.reshape(dec_batch, past, width), new.reshape(dec_batch, dec_seq, width), padz],
                                   axis=1)
        return (cat(cache_k[l], ak, LANES), cat(cache_v[l], av, LANES),
                cat(cache_idx_k[l], ikw[:, :IDX_DIM], IDX_DIM), n_keys_real)

    moe_s, totals, (s_s, k_s, v_s, ik_s) = _mixer(
        x_sample.reshape(dec_batch * dec_seq, D_MODEL), tab_s, 1, state_ret[l], dec_seq, dec_seq, sample_keys, mixer_w,
        totals_p)

    y_p, y_s = _moe([moe_p, moe_s], totals, *moe_w)

    kv = (DSA_KV_HEADS, DSA_HEAD_DIM)
    return (y_p.reshape(batch, seq, D_MODEL), y_s.reshape(dec_batch, dec_seq, D_MODEL),
            s_p[None], k_p.reshape(1, batch, seq, *kv), v_p.reshape(1, batch, seq, *kv),
            ik_p.reshape(1, batch, seq, IDX_DIM),
            s_s[None], k_s.reshape(1, dec_batch, dec_seq, *kv), v_s.reshape(1, dec_batch, dec_seq, *kv),
            ik_s.reshape(1, dec_batch, dec_seq, IDX_DIM))
```

```python
import functools

import numpy as np
import jax
import jax.numpy as jnp
from jax import lax
from jax.experimental import pallas as pl
from jax.experimental.pallas import tpu as pltpu
from jax.experimental.pallas import tpu_sc as plsc

F32 = jnp.float32
BF16 = jnp.bfloat16
I32 = jnp.int32

D_MODEL = 1024
CHUNK = 64
RET_HEADS = 4
RET_DK = 128
RET_DV = 256
RET_ROPE_BASE = 10000.0
DSA_HEADS = 8
DSA_KV_HEADS = 2
DSA_HEAD_DIM = 64
IDX_HEADS = 8
IDX_DIM = 64
DSA_TOPK = 256
ROPE_THETA = 500000.0
N_EXPERTS = 32
MOE_TOP_K = 4
D_FF = 1024
SWIGLU_ALPHA = 1.702
SWIGLU_LIMIT = 7.0
LN_EPS = 1e-5
GN_EPS = 1e-6
DEPTH = 1
DEEPNORM_ALPHA = (2.0 * DEPTH) ** 0.25
PROJ_WIDTHS = (RET_HEADS * RET_DK, RET_HEADS * RET_DK, RET_HEADS * RET_DV, RET_HEADS * RET_DV,
               DSA_HEADS * DSA_HEAD_DIM, DSA_KV_HEADS * DSA_HEAD_DIM, DSA_KV_HEADS * DSA_HEAD_DIM,
               IDX_HEADS * IDX_DIM, IDX_DIM, IDX_HEADS, D_MODEL, D_MODEL)

LANES = 128
MIB = 1024 * 1024

OFF_RQ, OFF_RK, OFF_RV, OFF_RG = 0, 512, 1024, 2048
OFF_AQ, OFF_AK, OFF_AV, OFF_IQ, OFF_IKW = 3072, 3584, 3712, 3840, 4352
OFF_GR, OFF_GA, PACKED_COLS = 4480, 5504, 6528
TAB_COLS = 8 * LANES

PROJ_TM = 512
RET_CHUNK = 128
RET_CHUNKS_PER_STEP = 8
DSA_TQ = 256
DSA_ROWS = 512
DSA_GROUP_MAX = 4
SEARCH_UNROLL = 4
MERGE_TM = 512
FFN_BLOCK = 512
SC_WINDOW = 128
QUARTER = D_MODEL // 4
UP_BLOCK = 2 * LANES


def _nt(a, b):
    return lax.dot_general(a, b, (((1,), (1,)), ((), ())), preferred_element_type=F32)


def _mm(a, b):
    return jnp.dot(a, b, preferred_element_type=F32)


def _proj_kernel(x_ref, w_ref, tab_ref, rq_ref, rk_ref, rv_ref, rg_ref, aq_ref, ak_ref, av_ref,
                 iq_ref, ikw_ref, gr_ref, ga_ref):
    xb = x_ref[...].astype(BF16)

    def mm(c0, n):
        return _mm(xb, w_ref[:, c0:c0 + n])

    def tab(i):
        return tab_ref[:, i * LANES:(i + 1) * LANES]

    def rot_full(z):
        return z * tab(0) + pltpu.roll(z, 64, 1) * tab(1)

    def rot_part(z, c):
        return z * tab(c) + pltpu.roll(z, LANES - 8, 1) * tab(c + 1) + pltpu.roll(z, 8, 1) * tab(c + 2)

    z = mm(OFF_RQ, 512)
    for h in range(4):
        sl = slice(h * LANES, (h + 1) * LANES)
        rq_ref[:, sl] = rot_full(z[:, sl]).astype(BF16)
    z = mm(OFF_RK, 512)
    for h in range(4):
        sl = slice(h * LANES, (h + 1) * LANES)
        rk_ref[:, sl] = (rot_full(z[:, sl]) * (RET_DK ** -0.5)).astype(BF16)
    for c in range(2):
        rv_ref[:, c * 512:(c + 1) * 512] = mm(OFF_RV + c * 512, 512).astype(BF16)
    for c in range(2):
        rg_ref[:, c * 512:(c + 1) * 512] = mm(OFF_RG + c * 512, 512)
    z = mm(OFF_AQ, 512)
    for h in range(4):
        sl = slice(h * LANES, (h + 1) * LANES)
        aq_ref[:, sl] = rot_part(z[:, sl], 2).astype(BF16)
    z = mm(OFF_AK, 256)
    ak_ref[...] = rot_part(z[:, :LANES], 2)
    av_ref[...] = z[:, LANES:]
    z = mm(OFF_IQ, 512)
    for h in range(4):
        sl = slice(h * LANES, (h + 1) * LANES)
        iq_ref[:, sl] = rot_part(z[:, sl], 2).astype(BF16)
    ikw_ref[...] = rot_part(mm(OFF_IKW, LANES), 5)
    for c in range(2):
        gr_ref[:, c * 512:(c + 1) * 512] = mm(OFF_GR + c * 512, 512)
    for c in range(2):
        ga_ref[:, c * 512:(c + 1) * 512] = mm(OFF_GA + c * 512, 512)


def _rot_tables(pos):
    p = pos.shape[0]
    posf = pos.astype(F32)[:, None]
    ret_f = RET_ROPE_BASE ** (-jnp.linspace(0.0, 1.0, RET_DK // 2, dtype=F32))
    ang = posf * ret_f[None, :]
    c, s = jnp.cos(ang), jnp.sin(ang)
    cos_r = jnp.concatenate([c, c], 1)
    sin_r = jnp.concatenate([-s, s], 1)
    n_rot = DSA_HEAD_DIM // 4
    att_f = ROPE_THETA ** (-jnp.arange(0, n_rot, 2, dtype=F32) / n_rot)
    ang2 = posf * att_f[None, :]
    c2, s2 = jnp.cos(ang2), jnp.sin(ang2)
    half = n_rot // 2
    rest = DSA_HEAD_DIM - 2 * half
    c64 = jnp.concatenate([c2, c2, jnp.ones((p, rest), F32)], 1)
    s1_64 = jnp.concatenate([-s2, jnp.zeros((p, DSA_HEAD_DIM - half), F32)], 1)
    s2_64 = jnp.concatenate([jnp.zeros((p, half), F32), s2, jnp.zeros((p, rest), F32)], 1)
    z64 = jnp.zeros((p, DSA_HEAD_DIM), F32)
    ci = jnp.concatenate([c64, jnp.full((p, IDX_HEADS), IDX_HEADS ** -0.5, F32),
                          jnp.zeros((p, DSA_HEAD_DIM - IDX_HEADS), F32)], 1)
    return jnp.concatenate([cos_r, sin_r,
                            jnp.concatenate([c64, c64], 1), jnp.concatenate([s1_64, s1_64], 1),
                            jnp.concatenate([s2_64, s2_64], 1),
                            ci, jnp.concatenate([s1_64, z64], 1), jnp.concatenate([s2_64, z64], 1)], 1)


def _pack_w_in(w):
    cuts = np.cumsum(PROJ_WIDTHS)[:-1].tolist()
    rq, rk, rv, rg, aq, ak, av, iq, ik, iw, gr, ga = jnp.split(w, cuts, axis=1)
    pad = jnp.zeros((w.shape[0], LANES - IDX_DIM - IDX_HEADS), w.dtype)
    return jnp.concatenate([rq, rk, rv, rg, aq, ak, av, iq, ik, iw, pad, gr, ga], axis=1).astype(BF16)


def _project(x, wp, tab, tab_period):
    n = x.shape[0]
    tm = PROJ_TM
    row = lambda w: pl.BlockSpec((tm, w), lambda i: (i, 0))
    out_shapes = [((n, 512), BF16), ((n, 512), BF16), ((n, 1024), BF16), ((n, 1024), F32),
                  ((n, 512), BF16), ((n, LANES), F32), ((n, LANES), F32), ((n, 512), BF16),
                  ((n, LANES), F32), ((n, 1024), F32), ((n, 1024), F32)]
    return pl.pallas_call(
        _proj_kernel,
        grid=(n // tm,),
        in_specs=[row(D_MODEL),
                  pl.BlockSpec((D_MODEL, PACKED_COLS), lambda i: (0, 0), pipeline_mode=pl.Buffered(1)),
                  pl.BlockSpec((tm, TAB_COLS), lambda i: (i % tab_period, 0))],
        out_specs=[row(s[1]) for s, _ in out_shapes],
        out_shape=[jax.ShapeDtypeStruct(s, d) for s, d in out_shapes],
        compiler_params=pltpu.CompilerParams(dimension_semantics=("parallel",), vmem_limit_bytes=52 * MIB),
        name="proj",
    )(x, wp, tab)


def _ret_kernel(dec_ref, xi_ref, zeta_ref, rq_ref, rk_ref, rv_ref, rg_ref, s0_ref, ret_ref, sout_ref, st_ref,
                *, rows, n_chunk, g_pow):
    j = pl.program_id(1)
    cpad = RET_CHUNK

    @pl.when(j == 0)
    def _():
        st_ref[...] = s0_ref[0]

    def padded(v):
        if rows == cpad:
            return v
        return jnp.concatenate([v, jnp.zeros((cpad - rows, v.shape[1]), v.dtype)], axis=0)

    for c in range(n_chunk):
        rs = slice(c * rows, (c + 1) * rows)
        for h in range(RET_HEADS):
            ks = slice(h * RET_DK, (h + 1) * RET_DK)
            vs = slice(h * RET_DV, (h + 1) * RET_DV)
            q = padded(rq_ref[rs, ks])
            kt = padded(rk_ref[rs, ks].astype(F32)).T
            v = padded(rv_ref[rs, vs])
            s = st_ref[h]
            sc = _mm(q, kt.astype(BF16)) * dec_ref[h]
            o = _mm(sc.astype(BF16), v) + _mm(q, s.astype(BF16)) * xi_ref[h]
            st_ref[h] = g_pow[h] * s + _mm((kt * zeta_ref[h]).astype(BF16), v)
            o = o[:rows]
            mu = jnp.mean(o, axis=-1, keepdims=True)
            d = o - mu
            var = jnp.mean(d * d, axis=-1, keepdims=True)
            gn = d * lax.rsqrt(var + GN_EPS)
            g = rg_ref[rs, vs]
            ret_ref[rs, vs] = (gn * (g * jax.nn.sigmoid(g))).astype(BF16)

    @pl.when(j == pl.num_programs(1) - 1)
    def _():
        sout_ref[0] = st_ref[...]


def _retention(rq, rk, rv, rg, s0, rows_per_stream, rows):
    n = rq.shape[0]
    n_streams = n // rows_per_stream
    n_chunk = min(rows_per_stream // rows, RET_CHUNKS_PER_STEP)
    blk = rows * n_chunk
    nb = rows_per_stream // blk
    gam = 1.0 - 2.0 ** (-5.0 - np.arange(RET_HEADS, dtype=np.float64))
    i = np.arange(RET_CHUNK, dtype=np.float64)
    diff = i[:, None] - i[None, :]
    dec = np.where(diff >= 0, gam[:, None, None] ** np.maximum(diff, 0.0)[None], 0.0)
    xi = gam[:, None, None] ** (i + 1.0)[None, :, None]
    zeta = np.where(i < rows, gam[:, None, None] ** (rows - 1.0 - i)[None, None, :], 0.0)
    g_pow = tuple(float(g ** rows) for g in gam)
    const = lambda shape: pl.BlockSpec(shape, lambda s, j: (0,) * len(shape))
    row = lambda w: pl.BlockSpec((blk, w), lambda s, j: (s * nb + j, 0))
    st = pl.BlockSpec((1, RET_HEADS, RET_DK, RET_DV), lambda s, j: (s, 0, 0, 0))
    return pl.pallas_call(
        functools.partial(_ret_kernel, rows=rows, n_chunk=n_chunk, g_pow=g_pow),
        grid=(n_streams, nb),
        in_specs=[const((RET_HEADS, RET_CHUNK, RET_CHUNK)), const((RET_HEADS, RET_CHUNK, 1)),
                  const((RET_HEADS, 1, RET_CHUNK)), row(512), row(512), row(1024), row(1024), st],
        out_specs=[row(1024), st],
        out_shape=[jax.ShapeDtypeStruct((n, RET_HEADS * RET_DV), BF16),
                   jax.ShapeDtypeStruct((n_streams, RET_HEADS, RET_DK, RET_DV), F32)],
        scratch_shapes=[pltpu.VMEM((RET_HEADS, RET_DK, RET_DV), F32)],
        compiler_params=pltpu.CompilerParams(dimension_semantics=("parallel", "arbitrary"),
                                             vmem_limit_bytes=32 * MIB),
        name="retention",
    )(jnp.asarray(dec, F32), jnp.asarray(xi, F32), jnp.asarray(zeta, F32), rq, rk, rv, rg, s0)


def _dsa_kernel(aq_ref, iq_ref, ikwq_ref, kk_ref, vv_ref, ikk_ref, *rest,
                group, tq, n_keys, row0, limit_const, n_sel):
    o_ref, w_ref, bias_ref = rest[-3:]
    rows = group * tq
    nsel_f = float(n_sel)
    neg_inf = -jnp.inf

    col = lax.broadcasted_iota(I32, (tq, n_keys), 1)
    if limit_const is None:
        limit = (jnp.right_shift(lax.broadcasted_iota(I32, (tq, 1), 0) + row0, 6) + 1) * CHUNK
    else:
        limit = limit_const

    def score_stream(s, carry):
        ikb = ikk_ref[s][:, :IDX_DIM].astype(BF16)
        iww = ikwq_ref[s][:, IDX_DIM:IDX_DIM + IDX_HEADS] * (IDX_DIM ** -0.5)
        acc = jnp.zeros((tq, n_keys), F32)
        for p in range(IDX_HEADS // 2):
            slab = iq_ref[s, :, p * LANES:(p + 1) * LANES]
            for hh in range(2):
                h = 2 * p + hh
                acc = acc + jnp.maximum(_nt(slab[:, hh * IDX_DIM:(hh + 1) * IDX_DIM], ikb), 0.0) * iww[:, h:h + 1]
        w_ref[pl.ds(pl.multiple_of(s * tq, tq), tq), :] = jnp.where(col < limit, acc, neg_inf)
        return carry

    lax.fori_loop(0, group, score_stream, 0)

    sc = w_ref[...]
    pos = jnp.sum(jnp.where(sc >= 0.0, 1.0, 0.0), axis=1, keepdims=True) >= nsel_f
    kk = jnp.where(pos, nsel_f, float(n_keys - n_sel + 1))
    w_ref[...] = jnp.where(pos, sc, -sc)

    def bit_step(i, u):
        cand_u = u | jnp.left_shift(jnp.int32(1), 30 - i)
        cand = pltpu.bitcast(cand_u, F32)
        cnt = jnp.sum(jnp.where(w_ref[...] >= cand, 1.0, 0.0), axis=1, keepdims=True)
        return jnp.where(cnt >= kk, cand_u, u)

    mag_u = lax.fori_loop(0, 31, bit_step, jnp.zeros((rows, 1), I32), unroll=SEARCH_UNROLL)
    mag = pltpu.bitcast(mag_u, F32)
    thr = jnp.where(pos, mag, -mag)

    sc = jnp.where(pos, w_ref[...], -w_ref[...])
    short = jnp.sum(jnp.where(sc >= thr, 1.0, 0.0), axis=1, keepdims=True) < nsel_f
    thr = jnp.where(jnp.logical_and(short, jnp.logical_not(pos)), -pltpu.bitcast(mag_u + 1, F32), thr)
    ge = sc >= thr
    cnt_gt = jnp.sum(jnp.where(sc > thr, 1.0, 0.0), axis=1, keepdims=True)
    cnt_ge = jnp.sum(jnp.where(ge, 1.0, 0.0), axis=1, keepdims=True)
    bias_ref[...] = jnp.where(jnp.logical_and(ge, sc > neg_inf), 0.0, neg_inf)
    excess = jnp.logical_and(cnt_ge > nsel_f, thr > neg_inf)

    @pl.when(jnp.max(jnp.where(excess, 1.0, 0.0)) > 0.0)
    def _():
        need = nsel_f - cnt_gt
        tri = jnp.where(lax.broadcasted_iota(I32, (LANES, LANES), 0) < lax.broadcasted_iota(I32, (LANES, LANES), 1),
                        1.0, 0.0).astype(BF16)
        before = jnp.zeros((rows, 1), F32)
        for b in range(n_keys // LANES):
            sl = slice(b * LANES, (b + 1) * LANES)
            sblk = jnp.where(pos, w_ref[:, sl], -w_ref[:, sl])
            eq = jnp.where(sblk == thr, 1.0, 0.0)
            rank = _mm(eq.astype(BF16), tri) + before
            keep = jnp.logical_or(sblk > thr, jnp.logical_and(sblk == thr, rank < need))
            bias_ref[:, sl] = jnp.where(jnp.logical_and(keep, sblk > neg_inf), 0.0, neg_inf)
            before = before + jnp.sum(eq, axis=1, keepdims=True)

    def attend_stream(s, carry):
        bias = bias_ref[pl.ds(pl.multiple_of(s * tq, tq), tq), :]
        kfull = kk_ref[s]
        vfull = vv_ref[s]
        for g in range(DSA_KV_HEADS):
            gs = slice(g * DSA_HEAD_DIM, (g + 1) * DSA_HEAD_DIM)
            kg = kfull[:, gs].astype(BF16)
            vg = vfull[:, gs].astype(BF16)
            for pp in range(2):
                p = 2 * g + pp
                slab = (aq_ref[s, :, p * LANES:(p + 1) * LANES].astype(F32) * (DSA_HEAD_DIM ** -0.5)).astype(BF16)
                outs = []
                for hh in range(2):
                    lg = _nt(slab[:, hh * DSA_HEAD_DIM:(hh + 1) * DSA_HEAD_DIM], kg) + bias
                    m = jnp.max(lg, axis=1, keepdims=True)
                    pr = jnp.exp(lg - m)
                    den = jnp.sum(pr, axis=1, keepdims=True)
                    outs.append(_mm(pr.astype(BF16), vg) / den)
                o_ref[s, :, p * LANES:(p + 1) * LANES] = jnp.concatenate(outs, axis=1).astype(BF16)
        return carry

    lax.fori_loop(0, group, attend_stream, 0)


def _dsa_call(aq, iq, ikw, kk, vv, ikk, group, tq, jq, n_keys, limit_const, n_sel, name, prev=None):
    n_streams, rows_per_stream = aq.shape[0], aq.shape[1]
    extra_specs, extra_args, aliases = [], (), {}
    if prev is not None:
        extra_specs, extra_args, aliases = [pl.BlockSpec(memory_space=pl.ANY)], (prev,), {6: 0}
    qrow = lambda a: pl.BlockSpec((group, tq, a.shape[2]), lambda s: (s, jq, 0))
    krow = lambda a: pl.BlockSpec((group, n_keys, a.shape[2]), lambda s: (s, 0, 0))
    width = DSA_HEADS * DSA_HEAD_DIM
    return pl.pallas_call(
        functools.partial(_dsa_kernel, group=group, tq=tq, n_keys=n_keys, row0=jq * tq, limit_const=limit_const,
                          n_sel=n_sel),
        grid=(n_streams // group,),
        in_specs=[qrow(aq), qrow(iq), qrow(ikw), krow(kk), krow(vv), krow(ikk)] + extra_specs,
        out_specs=pl.BlockSpec((group, tq, width), lambda s: (s, jq, 0)),
        out_shape=jax.ShapeDtypeStruct((n_streams, rows_per_stream, width), BF16),
        input_output_aliases=aliases,
        scratch_shapes=[pltpu.VMEM((group * tq, n_keys), F32), pltpu.VMEM((group * tq, n_keys), F32)],
        compiler_params=pltpu.CompilerParams(dimension_semantics=("parallel",), vmem_limit_bytes=56 * MIB),
        name=name,
    )(aq, iq, ikw, kk, vv, ikk, *extra_args)


def _dsa(aq, iq, ikw, kk, vv, ikk, tq, chunk_causal, limit_const):
    n_streams, n_keys, _ = kk.shape
    n = aq.shape[0]
    per_stream = lambda a: a.reshape(n_streams, n // n_streams, a.shape[1])
    aq, iq, ikw = per_stream(aq), per_stream(iq), per_stream(ikw)
    nq = n // n_streams // tq
    n_sel = min(DSA_TOPK, limit_const // 4)
    group = max(1, min(n_streams, DSA_ROWS // tq, DSA_GROUP_MAX))
    if not chunk_causal:
        assert nq == 1
        return _dsa_call(aq, iq, ikw, kk, vv, ikk, group, tq, 0, n_keys, limit_const, n_sel, "dsa_s").reshape(n, -1)
    out = None
    for jq in range(nq):
        out = _dsa_call(aq, iq, ikw, kk, vv, ikk, group, tq, jq, (jq + 1) * tq, None, n_sel, f"dsa_p{jq}", prev=out)
    return out.reshape(n, -1)


def _pack_rows(v):
    q = QUARTER
    bits = lambda x: pltpu.bitcast(x.astype(BF16).astype(F32), I32)
    pair = lambda c: lax.shift_right_logical(bits(v[:, c * q:(c + 1) * q]), 16) | bits(v[:, (c + 1) * q:(c + 2) * q])
    return pair(0), pair(2)


def _unpack_rows(lo, hi):
    parts = []
    for w in (lo, hi):
        parts.append(pltpu.bitcast(lax.shift_left(w, 16), F32))
        parts.append(pltpu.bitcast(w & jnp.int32(-65536), F32))
    return jnp.concatenate(parts, axis=1)


def _layer_norm(v, g, b):
    mu = jnp.mean(v, axis=-1, keepdims=True)
    d = v - mu
    var = jnp.mean(d * d, axis=-1, keepdims=True)
    return d * lax.rsqrt(var + LN_EPS) * g + b


def _merge_kernel(x_ref, ret_ref, od_ref, gr_ref, ga_ref, wr_ref, wd_ref, wo_ref, g1_ref, b1_ref, rwt_ref, rb_ref,
                  tri_ref, cnt0_ref, h1_ref, hlo_ref, hhi_ref, gtm_ref, ek_ref, rk_ref, tot_ref, cnt_ref):
    @pl.when(pl.program_id(0) == 0)
    def _():
        cnt_ref[...] = cnt0_ref[:, 0:1]

    y_ret = _mm(ret_ref[...], wr_ref[...])
    y_dsa = _mm(od_ref[...], wd_ref[...])
    merged = jax.nn.sigmoid(gr_ref[...]) * y_ret + jax.nn.sigmoid(ga_ref[...]) * y_dsa
    mix = _mm(merged.astype(BF16), wo_ref[...])
    h1 = _layer_norm(DEEPNORM_ALPHA * x_ref[...] + mix, g1_ref[...], b1_ref[...])
    h1_ref[...] = h1
    hlo_ref[...], hhi_ref[...] = _pack_rows(h1)

    logits = lax.dot_general(rwt_ref[...], h1, (((1,), (1,)), ((), ())), preferred_element_type=F32,
                             precision=lax.Precision.HIGHEST) + rb_ref[...]
    tm = logits.shape[1]
    e_iota = lax.broadcasted_iota(I32, (N_EXPERTS, tm), 0)
    tops, hots, firsts = [], [], []
    for _ in range(MOE_TOP_K):
        m = jnp.max(logits, axis=0, keepdims=True)
        first = jnp.min(jnp.where(logits == m, e_iota, N_EXPERTS), axis=0, keepdims=True)
        hot = e_iota == first
        tops.append(m)
        hots.append(hot)
        firsts.append(first)
        logits = jnp.where(hot, -jnp.inf, logits)
    exps = [jnp.exp(m - tops[0]) for m in tops]
    den = exps[0] + exps[1] + exps[2] + exps[3]
    sel = jnp.zeros((N_EXPERTS, tm), F32)
    for hot in hots:
        sel = sel + jnp.where(hot, 1.0, 0.0)
    rank = _mm(sel.astype(BF16), tri_ref[...]) + cnt_ref[...]
    ranks = [jnp.sum(jnp.where(hot, rank, 0.0), axis=0, keepdims=True).astype(I32) for hot in hots]
    pad_i = jnp.zeros((8 - MOE_TOP_K, tm), I32)
    ek_ref[...] = jnp.concatenate(firsts + [pad_i], axis=0)
    rk_ref[...] = jnp.concatenate(ranks + [pad_i], axis=0)
    gates = jnp.concatenate([e / den for e in exps] + [jnp.zeros((LANES - MOE_TOP_K, tm), F32)], axis=0)
    gtm_ref[...] = gates.T
    cnt_ref[...] = cnt_ref[...] + jnp.sum(sel, axis=1, keepdims=True)
    tot_ref[...] = jnp.broadcast_to(cnt_ref[...], tot_ref.shape)


def _merge(x, ret, od, gr, ga, wr, wd, wo, g1, b1, rwt, rb, cnt0):
    n = x.shape[0]
    tm = MERGE_TM
    row = lambda w: pl.BlockSpec((tm, w), lambda i: (i, 0))
    const = lambda a: pl.BlockSpec(a.shape, lambda i: (0,) * a.ndim)
    col = pl.BlockSpec((8, tm), lambda i: (0, i))
    tri = jnp.asarray(np.triu(np.ones((tm, tm), np.float32), 1), BF16)
    return pl.pallas_call(
        _merge_kernel,
        grid=(n // tm,),
        in_specs=[row(D_MODEL), row(1024), row(512), row(1024), row(1024), const(wr), const(wd), const(wo),
                  const(g1), const(b1), const(rwt), const(rb), const(tri), const(cnt0)],
        out_specs=[row(D_MODEL), row(QUARTER), row(QUARTER), row(LANES), col, col,
                   pl.BlockSpec((N_EXPERTS, LANES), lambda i: (0, 0))],
        out_shape=[jax.ShapeDtypeStruct((n, D_MODEL), F32), jax.ShapeDtypeStruct((n, QUARTER), I32),
                   jax.ShapeDtypeStruct((n, QUARTER), I32), jax.ShapeDtypeStruct((n, LANES), F32),
                   jax.ShapeDtypeStruct((8, n), I32), jax.ShapeDtypeStruct((8, n), I32),
                   jax.ShapeDtypeStruct((N_EXPERTS, LANES), F32)],
        scratch_shapes=[pltpu.VMEM((N_EXPERTS, 1), F32)],
        compiler_params=pltpu.CompilerParams(dimension_semantics=("arbitrary",), vmem_limit_bytes=48 * MIB),
        name="merge",
    )(x, ret, od, gr, ga, wr, wd, wo, g1, b1, rwt, rb, tri, cnt0)


def _deinterleave_kernel(w_ref, o_ref):
    r = lax.broadcasted_iota(I32, (UP_BLOCK, UP_BLOCK), 0)
    c = lax.broadcasted_iota(I32, (UP_BLOCK, UP_BLOCK), 1)
    src = jnp.where(c < LANES, 2 * c, 2 * (c - LANES) + 1)
    perm = jnp.where(r == src, 1.0, 0.0).astype(BF16)
    for b in range(w_ref.shape[2] // UP_BLOCK):
        sl = slice(b * UP_BLOCK, (b + 1) * UP_BLOCK)
        o_ref[0, :, sl] = _mm(w_ref[0, :, sl].astype(BF16), perm).astype(BF16)


def _deinterleave_w_up(w_up):
    n_e, d_in, d_out = w_up.shape
    cols = d_out
    spec = pl.BlockSpec((1, d_in, cols), lambda e, c: (e, 0, c))
    return pl.pallas_call(
        _deinterleave_kernel,
        grid=(n_e, d_out // cols),
        in_specs=[spec],
        out_specs=spec,
        out_shape=jax.ShapeDtypeStruct(w_up.shape, BF16),
        compiler_params=pltpu.CompilerParams(dimension_semantics=("parallel", "parallel"),
                                             vmem_limit_bytes=40 * MIB),
        name="w_up_prep",
    )(w_up)


def _pos_kernel(off_ref, ek_ref, rk_ref, pos_ref):
    ek = ek_ref[...]
    pos = rk_ref[...]
    for e in range(N_EXPERTS):
        pos = pos + jnp.where(ek == e, off_ref[e], 0)
    pos_ref[...] = pos


def _positions(off, ek, rk):
    n = ek.shape[1]
    tn = min(n, 2048)
    spec = pl.BlockSpec((8, tn), lambda i, off: (0, i))
    return pl.pallas_call(
        _pos_kernel,
        grid_spec=pltpu.PrefetchScalarGridSpec(num_scalar_prefetch=1, grid=(n // tn,), in_specs=[spec, spec],
                                               out_specs=spec),
        out_shape=jax.ShapeDtypeStruct((8, n), I32),
        name="moe_pos",
    )(off, ek, rk)


def _sc_mesh():
    return plsc.VectorSubcoreMesh(core_axis_name="core", subcore_axis_name="subcore")


def _sc_scatter(x, pos_flat, n_rows):
    n = x.shape[0]
    nw = n // SC_WINDOW

    @functools.partial(pl.kernel, out_type=jax.ShapeDtypeStruct((n_rows, x.shape[1]), x.dtype), mesh=_sc_mesh())
    def scatter(x_hbm, p_hbm, o_hbm):
        def body(x_vmem, *p_vmem):
            for p in p_vmem:
                pltpu.sync_copy(x_vmem, o_hbm.at[p.at[0]])

        pltpu.emit_pipeline(
            body, grid=(nw,),
            in_specs=[pl.BlockSpec((SC_WINDOW, x.shape[1]), lambda i: (i, 0))] +
                     [pl.BlockSpec((1, SC_WINDOW), functools.partial(lambda k, i: (0, k * nw + i), k))
                      for k in range(MOE_TOP_K)],
            out_specs=[], core_axis_name=("core", "subcore"), dimension_semantics=(pltpu.PARALLEL,),
        )(x_hbm, *([p_hbm] * MOE_TOP_K))

    return scatter(x, pos_flat)


def _sc_gather(y, pos_flat):
    m = pos_flat.shape[1]

    @functools.partial(pl.kernel, out_type=jax.ShapeDtypeStruct((m, y.shape[1]), y.dtype), mesh=_sc_mesh())
    def gather(y_hbm, p_hbm, o_hbm):
        def body(p_vmem, o_vmem):
            pltpu.sync_copy(y_hbm.at[p_vmem.at[0]], o_vmem)

        pltpu.emit_pipeline(
            body, grid=(m // SC_WINDOW,),
            in_specs=[pl.BlockSpec((1, SC_WINDOW), lambda i: (0, i))],
            out_specs=[pl.BlockSpec((SC_WINDOW, y.shape[1]), lambda i: (i, 0))],
            core_axis_name=("core", "subcore"), dimension_semantics=(pltpu.PARALLEL,),
        )(p_hbm, o_hbm)

    return gather(y, pos_flat)


def _ffn_kernel(be_ref, nu_ref, xlo_ref, xhi_ref, wup_ref, bup_ref, wdn_ref, bdn_ref, ylo_ref, yhi_ref):
    @pl.when(pl.program_id(0) < nu_ref[0])
    def _():
        x = _unpack_rows(xlo_ref[...], xhi_ref[...]).astype(BF16)
        h = _mm(x, wup_ref[0]) + bup_ref[0]
        acts = []
        for b in range(2 * D_FF // UP_BLOCK):
            glu = jnp.minimum(h[:, b * UP_BLOCK:b * UP_BLOCK + LANES], SWIGLU_LIMIT)
            lin = jnp.clip(h[:, b * UP_BLOCK + LANES:(b + 1) * UP_BLOCK], -SWIGLU_LIMIT, SWIGLU_LIMIT)
            acts.append(glu * jax.nn.sigmoid(SWIGLU_ALPHA * glu) * (lin + 1.0))
        act = jnp.concatenate(acts, axis=1)
        ylo_ref[...], yhi_ref[...] = _pack_rows(_mm(act.astype(BF16), wdn_ref[0]) + bdn_ref[0])


def _ffn(block_expert, n_used, xs_lo, xs_hi, wup, bup, wdn, bdn):
    rows = xs_lo.shape[0]
    blk = FFN_BLOCK
    row = pl.BlockSpec((blk, QUARTER), lambda i, be, nu: (i, 0))
    per_expert = lambda a: pl.BlockSpec((1,) + a.shape[1:], lambda i, be, nu: (be[i], 0, 0))
    return pl.pallas_call(
        _ffn_kernel,
        grid_spec=pltpu.PrefetchScalarGridSpec(
            num_scalar_prefetch=2, grid=(rows // blk,),
            in_specs=[row, row, per_expert(wup), per_expert(bup), per_expert(wdn), per_expert(bdn)],
            out_specs=[row, row]),
        out_shape=[jax.ShapeDtypeStruct((rows, QUARTER), I32), jax.ShapeDtypeStruct((rows, QUARTER), I32)],
        compiler_params=pltpu.CompilerParams(dimension_semantics=("arbitrary",), vmem_limit_bytes=40 * MIB),
        name="moe_ffn",
    )(block_expert, n_used, xs_lo, xs_hi, wup, bup, wdn, bdn)


def _combine_kernel(h1_ref, olo_ref, ohi_ref, gtm_ref, g2_ref, b2_ref, o_ref):
    g = gtm_ref[...]
    y = jnp.zeros(h1_ref.shape, F32)
    for k in range(MOE_TOP_K):
        y = y + g[:, k:k + 1] * _unpack_rows(olo_ref[k], ohi_ref[k])
    o_ref[...] = _layer_norm(DEEPNORM_ALPHA * h1_ref[...] + y, g2_ref[...], b2_ref[...])


def _combine(h1, og_lo, og_hi, gtm, g2, b2, row0):
    n = h1.shape[0]
    n_all = og_lo.shape[0] // MOE_TOP_K
    tm = MERGE_TM
    row = lambda w: pl.BlockSpec((tm, w), lambda i: (i, 0))
    const = lambda a: pl.BlockSpec(a.shape, lambda i: (0,) * a.ndim)
    picked = pl.BlockSpec((MOE_TOP_K, tm, QUARTER), lambda i: (0, i + row0 // tm, 0))
    return pl.pallas_call(
        _combine_kernel,
        grid=(n // tm,),
        in_specs=[row(D_MODEL), picked, picked, row(LANES), const(g2), const(b2)],
        out_specs=row(D_MODEL),
        out_shape=jax.ShapeDtypeStruct((n, D_MODEL), F32),
        compiler_params=pltpu.CompilerParams(dimension_semantics=("parallel",), vmem_limit_bytes=40 * MIB),
        name="moe_combine",
    )(h1, og_lo.reshape(MOE_TOP_K, n_all, QUARTER), og_hi.reshape(MOE_TOP_K, n_all, QUARTER), gtm, g2, b2)


def _moe(groups, totals, wup, bup, wdn, bdn, g2, b2):
    blk = FFN_BLOCK
    sizes = [g[0].shape[0] for g in groups]
    n_all = sum(sizes)
    n_rows = MOE_TOP_K * n_all + N_EXPERTS * blk
    counts = totals[:, 0].astype(I32)
    padded = (counts + blk - 1) // blk * blk
    ends = jnp.cumsum(padded)
    block_start = jnp.arange(n_rows // blk, dtype=I32) * blk
    block_expert = jnp.minimum(jnp.sum((ends[None, :] <= block_start[:, None]).astype(I32), axis=1), N_EXPERTS - 1)
    n_used = (ends[-1:] // blk).astype(I32)
    cat = lambda i, axis: jnp.concatenate([g[i] for g in groups], axis=axis)
    pos = _positions((ends - padded).astype(I32), cat(4, 1), cat(5, 1))
    pos_flat = pos[:MOE_TOP_K].reshape(1, MOE_TOP_K * n_all)
    xs_lo = _sc_scatter(cat(1, 0), pos_flat, n_rows)
    xs_hi = _sc_scatter(cat(2, 0), pos_flat, n_rows)
    ys_lo, ys_hi = _ffn(block_expert, n_used, xs_lo, xs_hi, wup, bup, wdn, bdn)
    og_lo, og_hi = _sc_gather(ys_lo, pos_flat), _sc_gather(ys_hi, pos_flat)
    starts = np.cumsum([0] + sizes[:-1]).tolist()
    return [_combine(g[0], og_lo, og_hi, g[3], g2, b2, r0) for g, r0 in zip(groups, starts)]


def _mixer(x, pos_tab, tab_period, s0, rows_per_stream, ret_rows, dsa_keys, weights, cnt0):
    wp, wr, wd, wo, g1, b1, rwt, rb = weights
    n = x.shape[0]
    n_streams = n // rows_per_stream
    rq, rk, rv, rg, aq, ak, av, iq, ikw, gr, ga = _project(x, wp, pos_tab, tab_period)
    ret, s_new = _retention(rq, rk, rv, rg, s0, rows_per_stream, ret_rows)
    if dsa_keys is None:
        per_stream = lambda a: a.reshape(n_streams, rows_per_stream, a.shape[1])
        od = _dsa(aq, iq, ikw, per_stream(ak), per_stream(av), per_stream(ikw), DSA_TQ, True, rows_per_stream)
    else:
        kk, vv, ikk, limit = dsa_keys(ak, av, ikw)
        od = _dsa(aq, iq, ikw, kk, vv, ikk, rows_per_stream, False, limit)
    h1, h_lo, h_hi, gtm, ek, rk_, totals = _merge(x, ret, od, gr, ga, wr, wd, wo, g1, b1, rwt, rb, cnt0)
    return (h1, h_lo, h_hi, gtm, ek, rk_), totals, (s_new, ak, av, ikw[:, :IDX_DIM])


def kernel(x_prompt, x_sample, state_ret, cache_k, cache_v, cache_idx_k, w_in, w_ret_o, w_dsa_o, w_o,
           ln1_g, ln1_b, router_w, router_b, w_up, b_up, w_down, b_down, ln2_g, ln2_b):
    assert w_in.shape[0] == DEPTH
    batch, seq, _ = x_prompt.shape
    dec_batch, dec_seq, _ = x_sample.shape
    past = cache_k.shape[2]
    assert seq % PROJ_TM == 0 and seq % DSA_TQ == 0 and PROJ_TM % dec_seq == 0

    l = 0
    mixer_w = (_pack_w_in(w_in[l]), w_ret_o[l].astype(BF16), w_dsa_o[l].astype(BF16), w_o[l].astype(BF16),
               ln1_g[l][None, :], ln1_b[l][None, :], router_w[l].T, router_b[l][:, None])
    moe_w = (
        _deinterleave_w_up(w_up[l]),
        b_up[l].reshape(N_EXPERTS, 2 * D_FF // UP_BLOCK, LANES, 2).transpose(0, 1, 3, 2).reshape(N_EXPERTS, 1, 2 * D_FF),
        w_down[l].astype(BF16), b_down[l][:, None, :], ln2_g[l][None, :], ln2_b[l][None, :])

    tab_p = _rot_tables(jnp.arange(seq))
    zeros_state = jnp.zeros((batch, RET_HEADS, RET_DK, RET_DV), F32)
    moe_p, totals_p, (s_p, k_p, v_p, ik_p) = _mixer(
        x_prompt.reshape(batch * seq, D_MODEL), tab_p, seq // PROJ_TM, zeros_state, seq, RET_CHUNK, None, mixer_w,
        jnp.zeros((N_EXPERTS, LANES), F32))

    n_keys_real = past + dec_seq
    n_keys = -(-n_keys_real // LANES) * LANES
    tab_s = jnp.tile(_rot_tables(past + jnp.arange(dec_seq)), (PROJ_TM // dec_seq, 1))

    def sample_keys(ak, av, ikw):
        def cat(cache, new, width):
            padz = jnp.zeros((dec_batch, n_keys - n_keys_real, width), F32)
            return jnp.concatenate([cache.reshape(dec_batch, past, width), new.reshape(dec_batch, dec_seq, width), padz],
                                   axis=1)
        return (cat(cache_k[l], ak, LANES), cat(cache_v[l], av, LANES),
                cat(cache_idx_k[l], ikw[:, :IDX_DIM], IDX_DIM), n_keys_real)

    moe_s, totals, (s_s, k_s, v_s, ik_s) = _mixer(
        x_sample.reshape(dec_batch * dec_seq, D_MODEL), tab_s, 1, state_ret[l], dec_seq, dec_seq, sample_keys, mixer_w,
        totals_p)

    y_p, y_s = _moe([moe_p, moe_s], totals, *moe_w)

    kv = (DSA_KV_HEADS, DSA_HEAD_DIM)
    return (y_p.reshape(batch, seq, D_MODEL), y_s.reshape(dec_batch, dec_seq, D_MODEL),
            s_p[None], k_p.reshape(1, batch, seq, *kv), v_p.reshape(1, batch, seq, *kv),
            ik_p.reshape(1, batch, seq, IDX_DIM),
            s_s[None], k_s.reshape(1, dec_batch, dec_seq, *kv), v_s.reshape(1, dec_batch, dec_seq, *kv),
            ik_s.reshape(1, dec_batch, dec_seq, IDX_DIM))
```

```python
import functools

import numpy as np
import jax
import jax.numpy as jnp
from jax import lax
from jax.experimental import pallas as pl
from jax.experimental.pallas import tpu as pltpu
from jax.experimental.pallas import tpu_sc as plsc

F32 = jnp.float32
BF16 = jnp.bfloat16
I32 = jnp.int32

D_MODEL = 1024
CHUNK = 64
RET_HEADS = 4
RET_DK = 128
RET_DV = 256
RET_ROPE_BASE = 10000.0
DSA_HEADS = 8
DSA_KV_HEADS = 2
DSA_HEAD_DIM = 64
IDX_HEADS = 8
IDX_DIM = 64
DSA_TOPK = 256
ROPE_THETA = 500000.0
N_EXPERTS = 32
MOE_TOP_K = 4
D_FF = 1024
SWIGLU_ALPHA = 1.702
SWIGLU_LIMIT = 7.0
LN_EPS = 1e-5
GN_EPS = 1e-6
DEPTH = 1
DEEPNORM_ALPHA = (2.0 * DEPTH) ** 0.25
PROJ_WIDTHS = (RET_HEADS * RET_DK, RET_HEADS * RET_DK, RET_HEADS * RET_DV, RET_HEADS * RET_DV,
               DSA_HEADS * DSA_HEAD_DIM, DSA_KV_HEADS * DSA_HEAD_DIM, DSA_KV_HEADS * DSA_HEAD_DIM,
               IDX_HEADS * IDX_DIM, IDX_DIM, IDX_HEADS, D_MODEL, D_MODEL)

LANES = 128
MIB = 1024 * 1024

OFF_RQ, OFF_RK, OFF_RV, OFF_RG = 0, 512, 1024, 2048
OFF_AQ, OFF_AK, OFF_AV, OFF_IQ, OFF_IKW = 3072, 3584, 3712, 3840, 4352
OFF_GR, OFF_GA, PACKED_COLS = 4480, 5504, 6528
TAB_COLS = 8 * LANES

PROJ_TM = 512
RET_CHUNK = 128
RET_CHUNKS_PER_STEP = 8
DSA_TQ = 256
DSA_ROWS = 512
DSA_GROUP_MAX = 4
SEARCH_UNROLL = 4
MERGE_TM = 512
FFN_BLOCK = 512
SC_WINDOW = 128
QUARTER = D_MODEL // 4
UP_BLOCK = 2 * LANES


def _nt(a, b):
    return lax.dot_general(a, b, (((1,), (1,)), ((), ())), preferred_element_type=F32)


def _mm(a, b):
    return jnp.dot(a, b, preferred_element_type=F32)


def _proj_kernel(x_ref, w_ref, tab_ref, rq_ref, rk_ref, rv_ref, rg_ref, aq_ref, ak_ref, av_ref,
                 iq_ref, ikw_ref, gr_ref, ga_ref):
    xb = x_ref[...].astype(BF16)

    def mm(c0, n):
        return _mm(xb, w_ref[:, c0:c0 + n])

    def tab(i):
        return tab_ref[:, i * LANES:(i + 1) * LANES]

    def rot_full(z):
        return z * tab(0) + pltpu.roll(z, 64, 1) * tab(1)

    def rot_part(z, c):
        return z * tab(c) + pltpu.roll(z, LANES - 8, 1) * tab(c + 1) + pltpu.roll(z, 8, 1) * tab(c + 2)

    z = mm(OFF_RQ, 512)
    for h in range(4):
        sl = slice(h * LANES, (h + 1) * LANES)
        rq_ref[:, sl] = rot_full(z[:, sl]).astype(BF16)
    z = mm(OFF_RK, 512)
    for h in range(4):
        sl = slice(h * LANES, (h + 1) * LANES)
        rk_ref[:, sl] = (rot_full(z[:, sl]) * (RET_DK ** -0.5)).astype(BF16)
    for c in range(2):
        rv_ref[:, c * 512:(c + 1) * 512] = mm(OFF_RV + c * 512, 512).astype(BF16)
    for c in range(2):
        rg_ref[:, c * 512:(c + 1) * 512] = mm(OFF_RG + c * 512, 512)
    z = mm(OFF_AQ, 512)
    for h in range(4):
        sl = slice(h * LANES, (h + 1) * LANES)
        aq_ref[:, sl] = rot_part(z[:, sl], 2).astype(BF16)
    z = mm(OFF_AK, 256)
    ak_ref[...] = rot_part(z[:, :LANES], 2)
    av_ref[...] = z[:, LANES:]
    z = mm(OFF_IQ, 512)
    for h in range(4):
        sl = slice(h * LANES, (h + 1) * LANES)
        iq_ref[:, sl] = rot_part(z[:, sl], 2).astype(BF16)
    ikw_ref[...] = rot_part(mm(OFF_IKW, LANES), 5)
    for c in range(2):
        gr_ref[:, c * 512:(c + 1) * 512] = mm(OFF_GR + c * 512, 512)
    for c in range(2):
        ga_ref[:, c * 512:(c + 1) * 512] = mm(OFF_GA + c * 512, 512)


def _rot_tables(pos):
    p = pos.shape[0]
    posf = pos.astype(F32)[:, None]
    ret_f = RET_ROPE_BASE ** (-jnp.linspace(0.0, 1.0, RET_DK // 2, dtype=F32))
    ang = posf * ret_f[None, :]
    c, s = jnp.cos(ang), jnp.sin(ang)
    cos_r = jnp.concatenate([c, c], 1)
    sin_r = jnp.concatenate([-s, s], 1)
    n_rot = DSA_HEAD_DIM // 4
    att_f = ROPE_THETA ** (-jnp.arange(0, n_rot, 2, dtype=F32) / n_rot)
    ang2 = posf * att_f[None, :]
    c2, s2 = jnp.cos(ang2), jnp.sin(ang2)
    half = n_rot // 2
    rest = DSA_HEAD_DIM - 2 * half
    c64 = jnp.concatenate([c2, c2, jnp.ones((p, rest), F32)], 1)
    s1_64 = jnp.concatenate([-s2, jnp.zeros((p, DSA_HEAD_DIM - half), F32)], 1)
    s2_64 = jnp.concatenate([jnp.zeros((p, half), F32), s2, jnp.zeros((p, rest), F32)], 1)
    z64 = jnp.zeros((p, DSA_HEAD_DIM), F32)
    ci = jnp.concatenate([c64, jnp.full((p, IDX_HEADS), IDX_HEADS ** -0.5, F32),
                          jnp.zeros((p, DSA_HEAD_DIM - IDX_HEADS), F32)], 1)
    return jnp.concatenate([cos_r, sin_r,
                            jnp.concatenate([c64, c64], 1), jnp.concatenate([s1_64, s1_64], 1),
                            jnp.concatenate([s2_64, s2_64], 1),
                            ci, jnp.concatenate([s1_64, z64], 1), jnp.concatenate([s2_64, z64], 1)], 1)


def _pack_w_in(w):
    cuts = np.cumsum(PROJ_WIDTHS)[:-1].tolist()
    rq, rk, rv, rg, aq, ak, av, iq, ik, iw, gr, ga = jnp.split(w, cuts, axis=1)
    pad = jnp.zeros((w.shape[0], LANES - IDX_DIM - IDX_HEADS), w.dtype)
    return jnp.concatenate([rq, rk, rv, rg, aq, ak, av, iq, ik, iw, pad, gr, ga], axis=1).astype(BF16)


def _project(x, wp, tab, tab_period):
    n = x.shape[0]
    tm = PROJ_TM
    row = lambda w: pl.BlockSpec((tm, w), lambda i: (i, 0))
    out_shapes = [((n, 512), BF16), ((n, 512), BF16), ((n, 1024), BF16), ((n, 1024), F32),
                  ((n, 512), BF16), ((n, LANES), F32), ((n, LANES), F32), ((n, 512), BF16),
                  ((n, LANES), F32), ((n, 1024), F32), ((n, 1024), F32)]
    return pl.pallas_call(
        _proj_kernel,
        grid=(n // tm,),
        in_specs=[row(D_MODEL),
                  pl.BlockSpec((D_MODEL, PACKED_COLS), lambda i: (0, 0), pipeline_mode=pl.Buffered(1)),
                  pl.BlockSpec((tm, TAB_COLS), lambda i: (i % tab_period, 0))],
        out_specs=[row(s[1]) for s, _ in out_shapes],
        out_shape=[jax.ShapeDtypeStruct(s, d) for s, d in out_shapes],
        compiler_params=pltpu.CompilerParams(dimension_semantics=("parallel",), vmem_limit_bytes=52 * MIB),
        name="proj",
    )(x, wp, tab)


def _ret_kernel(dec_ref, xi_ref, zeta_ref, rq_ref, rk_ref, rv_ref, rg_ref, s0_ref, ret_ref, sout_ref, st_ref,
                *, rows, n_chunk, g_pow):
    j = pl.program_id(1)
    cpad = RET_CHUNK

    @pl.when(j == 0)
    def _():
        st_ref[...] = s0_ref[0]

    def padded(v):
        if rows == cpad:
            return v
        return jnp.concatenate([v, jnp.zeros((cpad - rows, v.shape[1]), v.dtype)], axis=0)

    for c in range(n_chunk):
        rs = slice(c * rows, (c + 1) * rows)
        for h in range(RET_HEADS):
            ks = slice(h * RET_DK, (h + 1) * RET_DK)
            vs = slice(h * RET_DV, (h + 1) * RET_DV)
            q = padded(rq_ref[rs, ks])
            kt = padded(rk_ref[rs, ks].astype(F32)).T
            v = padded(rv_ref[rs, vs])
            s = st_ref[h]
            sc = _mm(q, kt.astype(BF16)) * dec_ref[h]
            o = _mm(sc.astype(BF16), v) + _mm(q, s.astype(BF16)) * xi_ref[h]
            st_ref[h] = g_pow[h] * s + _mm((kt * zeta_ref[h]).astype(BF16), v)
            o = o[:rows]
            mu = jnp.mean(o, axis=-1, keepdims=True)
            d = o - mu
            var = jnp.mean(d * d, axis=-1, keepdims=True)
            gn = d * lax.rsqrt(var + GN_EPS)
            g = rg_ref[rs, vs]
            ret_ref[rs, vs] = (gn * (g * jax.nn.sigmoid(g))).astype(BF16)

    @pl.when(j == pl.num_programs(1) - 1)
    def _():
        sout_ref[0] = st_ref[...]


def _retention(rq, rk, rv, rg, s0, rows_per_stream, rows):
    n = rq.shape[0]
    n_streams = n // rows_per_stream
    n_chunk = min(rows_per_stream // rows, RET_CHUNKS_PER_STEP)
    blk = rows * n_chunk
    nb = rows_per_stream // blk
    gam = 1.0 - 2.0 ** (-5.0 - np.arange(RET_HEADS, dtype=np.float64))
    i = np.arange(RET_CHUNK, dtype=np.float64)
    diff = i[:, None] - i[None, :]
    dec = np.where(diff >= 0, gam[:, None, None] ** np.maximum(diff, 0.0)[None], 0.0)
    xi = gam[:, None, None] ** (i + 1.0)[None, :, None]
    zeta = np.where(i < rows, gam[:, None, None] ** (rows - 1.0 - i)[None, None, :], 0.0)
    g_pow = tuple(float(g ** rows) for g in gam)
    const = lambda shape: pl.BlockSpec(shape, lambda s, j: (0,) * len(shape))
    row = lambda w: pl.BlockSpec((blk, w), lambda s, j: (s * nb + j, 0))
    st = pl.BlockSpec((1, RET_HEADS, RET_DK, RET_DV), lambda s, j: (s, 0, 0, 0))
    return pl.pallas_call(
        functools.partial(_ret_kernel, rows=rows, n_chunk=n_chunk, g_pow=g_pow),
        grid=(n_streams, nb),
        in_specs=[const((RET_HEADS, RET_CHUNK, RET_CHUNK)), const((RET_HEADS, RET_CHUNK, 1)),
                  const((RET_HEADS, 1, RET_CHUNK)), row(512), row(512), row(1024), row(1024), st],
        out_specs=[row(1024), st],
        out_shape=[jax.ShapeDtypeStruct((n, RET_HEADS * RET_DV), BF16),
                   jax.ShapeDtypeStruct((n_streams, RET_HEADS, RET_DK, RET_DV), F32)],
        scratch_shapes=[pltpu.VMEM((RET_HEADS, RET_DK, RET_DV), F32)],
        compiler_params=pltpu.CompilerParams(dimension_semantics=("parallel", "arbitrary"),
                                             vmem_limit_bytes=32 * MIB),
        name="retention",
    )(jnp.asarray(dec, F32), jnp.asarray(xi, F32), jnp.asarray(zeta, F32), rq, rk, rv, rg, s0)


def _dsa_kernel(aq_ref, iq_ref, ikwq_ref, kk_ref, vv_ref, ikk_ref, *rest,
                group, tq, n_keys, row0, limit_const, n_sel):
    o_ref, w_ref, bias_ref = rest[-3:]
    rows = group * tq
    nsel_f = float(n_sel)
    neg_inf = -jnp.inf

    col = lax.broadcasted_iota(I32, (tq, n_keys), 1)
    if limit_const is None:
        limit = (jnp.right_shift(lax.broadcasted_iota(I32, (tq, 1), 0) + row0, 6) + 1) * CHUNK
    else:
        limit = limit_const

    def score_stream(s, carry):
        ikb = ikk_ref[s][:, :IDX_DIM].astype(BF16)
        iww = ikwq_ref[s][:, IDX_DIM:IDX_DIM + IDX_HEADS] * (IDX_DIM ** -0.5)
        acc = jnp.zeros((tq, n_keys), F32)
        for p in range(IDX_HEADS // 2):
            slab = iq_ref[s, :, p * LANES:(p + 1) * LANES]
            for hh in range(2):
                h = 2 * p + hh
                acc = acc + jnp.maximum(_nt(slab[:, hh * IDX_DIM:(hh + 1) * IDX_DIM], ikb), 0.0) * iww[:, h:h + 1]
        w_ref[pl.ds(pl.multiple_of(s * tq, tq), tq), :] = jnp.where(col < limit, acc, neg_inf)
        return carry

    lax.fori_loop(0, group, score_stream, 0)

    sc = w_ref[...]
    pos = jnp.sum(jnp.where(sc >= 0.0, 1.0, 0.0), axis=1, keepdims=True) >= nsel_f
    kk = jnp.where(pos, nsel_f, float(n_keys - n_sel + 1))
    w_ref[...] = jnp.where(pos, sc, -sc)

    def bit_step(i, u):
        cand_u = u | jnp.left_shift(jnp.int32(1), 30 - i)
        cand = pltpu.bitcast(cand_u, F32)
        cnt = jnp.sum(jnp.where(w_ref[...] >= cand, 1.0, 0.0), axis=1, keepdims=True)
        return jnp.where(cnt >= kk, cand_u, u)

    mag_u = lax.fori_loop(0, 31, bit_step, jnp.zeros((rows, 1), I32), unroll=SEARCH_UNROLL)
    mag = pltpu.bitcast(mag_u, F32)
    thr = jnp.where(pos, mag, -mag)

    sc = jnp.where(pos, w_ref[...], -w_ref[...])
    short = jnp.sum(jnp.where(sc >= thr, 1.0, 0.0), axis=1, keepdims=True) < nsel_f
    thr = jnp.where(jnp.logical_and(short, jnp.logical_not(pos)), -pltpu.bitcast(mag_u + 1, F32), thr)
    ge = sc >= thr
    cnt_gt = jnp.sum(jnp.where(sc > thr, 1.0, 0.0), axis=1, keepdims=True)
    cnt_ge = jnp.sum(jnp.where(ge, 1.0, 0.0), axis=1, keepdims=True)
    bias_ref[...] = jnp.where(jnp.logical_and(ge, sc > neg_inf), 0.0, neg_inf)
    excess = jnp.logical_and(cnt_ge > nsel_f, thr > neg_inf)

    @pl.when(jnp.max(jnp.where(excess, 1.0, 0.0)) > 0.0)
    def _():
        need = nsel_f - cnt_gt
        tri = jnp.where(lax.broadcasted_iota(I32, (LANES, LANES), 0) < lax.broadcasted_iota(I32, (LANES, LANES), 1),
                        1.0, 0.0).astype(BF16)
        before = jnp.zeros((rows, 1), F32)
        for b in range(n_keys // LANES):
            sl = slice(b * LANES, (b + 1) * LANES)
            sblk = jnp.where(pos, w_ref[:, sl], -w_ref[:, sl])
            eq = jnp.where(sblk == thr, 1.0, 0.0)
            rank = _mm(eq.astype(BF16), tri) + before
            keep = jnp.logical_or(sblk > thr, jnp.logical_and(sblk == thr, rank < need))
            bias_ref[:, sl] = jnp.where(jnp.logical_and(keep, sblk > neg_inf), 0.0, neg_inf)
            before = before + jnp.sum(eq, axis=1, keepdims=True)

    def attend_stream(s, carry):
        bias = bias_ref[pl.ds(pl.multiple_of(s * tq, tq), tq), :]
        kfull = kk_ref[s]
        vfull = vv_ref[s]
        for g in range(DSA_KV_HEADS):
            gs = slice(g * DSA_HEAD_DIM, (g + 1) * DSA_HEAD_DIM)
            kg = kfull[:, gs].astype(BF16)
            vg = vfull[:, gs].astype(BF16)
            for pp in range(2):
                p = 2 * g + pp
                slab = (aq_ref[s, :, p * LANES:(p + 1) * LANES].astype(F32) * (DSA_HEAD_DIM ** -0.5)).astype(BF16)
                outs = []
                for hh in range(2):
                    lg = _nt(slab[:, hh * DSA_HEAD_DIM:(hh + 1) * DSA_HEAD_DIM], kg) + bias
                    m = jnp.max(lg, axis=1, keepdims=True)
                    pr = jnp.exp(lg - m)
                    den = jnp.sum(pr, axis=1, keepdims=True)
                    outs.append(_mm(pr.astype(BF16), vg) / den)
                o_ref[s, :, p * LANES:(p + 1) * LANES] = jnp.concatenate(outs, axis=1).astype(BF16)
        return carry

    lax.fori_loop(0, group, attend_stream, 0)


def _dsa_call(aq, iq, ikw, kk, vv, ikk, group, tq, jq, n_keys, limit_const, n_sel, name, prev=None):
    n_streams, rows_per_stream = aq.shape[0], aq.shape[1]
    extra_specs, extra_args, aliases = [], (), {}
    if prev is not None:
        extra_specs, extra_args, aliases = [pl.BlockSpec(memory_space=pl.ANY)], (prev,), {6: 0}
    qrow = lambda a: pl.BlockSpec((group, tq, a.shape[2]), lambda s: (s, jq, 0))
    krow = lambda a: pl.BlockSpec((group, n_keys, a.shape[2]), lambda s: (s, 0, 0))
    width = DSA_HEADS * DSA_HEAD_DIM
    return pl.pallas_call(
        functools.partial(_dsa_kernel, group=group, tq=tq, n_keys=n_keys, row0=jq * tq, limit_const=limit_const,
                          n_sel=n_sel),
        grid=(n_streams // group,),
        in_specs=[qrow(aq), qrow(iq), qrow(ikw), krow(kk), krow(vv), krow(ikk)] + extra_specs,
        out_specs=pl.BlockSpec((group, tq, width), lambda s: (s, jq, 0)),
        out_shape=jax.ShapeDtypeStruct((n_streams, rows_per_stream, width), BF16),
        input_output_aliases=aliases,
        scratch_shapes=[pltpu.VMEM((group * tq, n_keys), F32), pltpu.VMEM((group * tq, n_keys), F32)],
        compiler_params=pltpu.CompilerParams(dimension_semantics=("parallel",), vmem_limit_bytes=56 * MIB),
        name=name,
    )(aq, iq, ikw, kk, vv, ikk, *extra_args)


def _dsa_cached_kernel(aq_ref, iq_ref, ikwq_ref, ck_ref, cv_ref, cik_ref, nk_ref, nv_ref, o_ref,
                       w_ref, bias_ref, kk_ref, vv_ref, ikk_ref, **static):
    past, new = ck_ref.shape[1], nk_ref.shape[1]
    for dst, cache, fresh in ((kk_ref, ck_ref, nk_ref), (vv_ref, cv_ref, nv_ref), (ikk_ref, cik_ref, ikwq_ref)):
        dst[:, :past, :cache.shape[2]] = cache[...]
        dst[:, past:past + new, :] = fresh[...]
        dst[:, past + new:, :] = jnp.zeros((dst.shape[0], dst.shape[1] - past - new, dst.shape[2]), F32)
    _dsa_kernel(aq_ref, iq_ref, ikwq_ref, kk_ref, vv_ref, ikk_ref, o_ref, w_ref, bias_ref, **static)


def _dsa_cached(aq, iq, ikw, cache_k, cache_v, cache_ik, new_k, new_v, tq):
    n_streams, past, _ = cache_k.shape
    n = aq.shape[0]
    per_stream = lambda a: a.reshape(n_streams, tq, a.shape[1])
    aq, iq, ikw, new_k, new_v = per_stream(aq), per_stream(iq), per_stream(ikw), per_stream(new_k), per_stream(new_v)
    limit = past + tq
    n_keys = -(-limit // LANES) * LANES
    group = max(1, min(n_streams, DSA_ROWS // tq, DSA_GROUP_MAX))
    blk = lambda a: pl.BlockSpec((group,) + a.shape[1:], lambda s: (s, 0, 0))
    width = DSA_HEADS * DSA_HEAD_DIM
    return pl.pallas_call(
        functools.partial(_dsa_cached_kernel, group=group, tq=tq, n_keys=n_keys, row0=0, limit_const=limit,
                          n_sel=min(DSA_TOPK, limit // 4)),
        grid=(n_streams // group,),
        in_specs=[blk(a) for a in (aq, iq, ikw, cache_k, cache_v, cache_ik, new_k, new_v)],
        out_specs=pl.BlockSpec((group, tq, width), lambda s: (s, 0, 0)),
        out_shape=jax.ShapeDtypeStruct((n_streams, tq, width), BF16),
        scratch_shapes=[pltpu.VMEM((group * tq, n_keys), F32), pltpu.VMEM((group * tq, n_keys), F32)] +
                       [pltpu.VMEM((group, n_keys, LANES), F32)] * 3,
        compiler_params=pltpu.CompilerParams(dimension_semantics=("parallel",), vmem_limit_bytes=48 * MIB),
        name="dsa_s",
    )(aq, iq, ikw, cache_k, cache_v, cache_ik, new_k, new_v).reshape(n, width)


def _dsa(aq, iq, ikw, kk, vv, ikk, tq):
    n_streams, n_keys, _ = kk.shape
    n = aq.shape[0]
    per_stream = lambda a: a.reshape(n_streams, n // n_streams, a.shape[1])
    aq, iq, ikw = per_stream(aq), per_stream(iq), per_stream(ikw)
    nq = n // n_streams // tq
    n_sel = min(DSA_TOPK, n_keys // 4)
    group = max(1, min(n_streams, DSA_ROWS // tq, DSA_GROUP_MAX))
    out = None
    for jq in range(nq):
        out = _dsa_call(aq, iq, ikw, kk, vv, ikk, group, tq, jq, (jq + 1) * tq, None, n_sel, f"dsa_p{jq}", prev=out)
    return out.reshape(n, -1)


def _pack_rows(v):
    q = QUARTER
    bits = lambda x: pltpu.bitcast(x.astype(BF16).astype(F32), I32)
    pair = lambda c: lax.shift_right_logical(bits(v[:, c * q:(c + 1) * q]), 16) | bits(v[:, (c + 1) * q:(c + 2) * q])
    return pair(0), pair(2)


def _unpack_rows(lo, hi):
    parts = []
    for w in (lo, hi):
        parts.append(pltpu.bitcast(lax.shift_left(w, 16), F32))
        parts.append(pltpu.bitcast(w & jnp.int32(-65536), F32))
    return jnp.concatenate(parts, axis=1)


def _layer_norm(v, g, b):
    mu = jnp.mean(v, axis=-1, keepdims=True)
    d = v - mu
    var = jnp.mean(d * d, axis=-1, keepdims=True)
    return d * lax.rsqrt(var + LN_EPS) * g + b


def _merge_kernel(x_ref, ret_ref, od_ref, gr_ref, ga_ref, wr_ref, wd_ref, wo_ref, g1_ref, b1_ref, rwt_ref, rb_ref,
                  tri_ref, cnt0_ref, h1_ref, hlo_ref, hhi_ref, gtm_ref, ek_ref, rk_ref, tot_ref, cnt_ref):
    @pl.when(pl.program_id(0) == 0)
    def _():
        cnt_ref[...] = cnt0_ref[:, 0:1]

    y_ret = _mm(ret_ref[...], wr_ref[...])
    y_dsa = _mm(od_ref[...], wd_ref[...])
    merged = jax.nn.sigmoid(gr_ref[...]) * y_ret + jax.nn.sigmoid(ga_ref[...]) * y_dsa
    mix = _mm(merged.astype(BF16), wo_ref[...])
    h1 = _layer_norm(DEEPNORM_ALPHA * x_ref[...] + mix, g1_ref[...], b1_ref[...])
    h1_ref[...] = h1
    hlo_ref[...], hhi_ref[...] = _pack_rows(h1)

    logits = lax.dot_general(rwt_ref[...], h1, (((1,), (1,)), ((), ())), preferred_element_type=F32,
                             precision=lax.Precision.HIGHEST) + rb_ref[...]
    tm = logits.shape[1]
    e_iota = lax.broadcasted_iota(I32, (N_EXPERTS, tm), 0)
    tops, hots, firsts = [], [], []
    for _ in range(MOE_TOP_K):
        m = jnp.max(logits, axis=0, keepdims=True)
        first = jnp.min(jnp.where(logits == m, e_iota, N_EXPERTS), axis=0, keepdims=True)
        hot = e_iota == first
        tops.append(m)
        hots.append(hot)
        firsts.append(first)
        logits = jnp.where(hot, -jnp.inf, logits)
    exps = [jnp.exp(m - tops[0]) for m in tops]
    den = exps[0] + exps[1] + exps[2] + exps[3]
    sel = jnp.zeros((N_EXPERTS, tm), F32)
    for hot in hots:
        sel = sel + jnp.where(hot, 1.0, 0.0)
    rank = _mm(sel.astype(BF16), tri_ref[...]) + cnt_ref[...]
    ranks = [jnp.sum(jnp.where(hot, rank, 0.0), axis=0, keepdims=True).astype(I32) for hot in hots]
    pad_i = jnp.zeros((8 - MOE_TOP_K, tm), I32)
    ek_ref[...] = jnp.concatenate(firsts + [pad_i], axis=0)
    rk_ref[...] = jnp.concatenate(ranks + [pad_i], axis=0)
    gates = jnp.concatenate([e / den for e in exps] + [jnp.zeros((LANES - MOE_TOP_K, tm), F32)], axis=0)
    gtm_ref[...] = gates.T
    cnt_ref[...] = cnt_ref[...] + jnp.sum(sel, axis=1, keepdims=True)
    tot_ref[...] = jnp.broadcast_to(cnt_ref[...], tot_ref.shape)


def _merge(x, ret, od, gr, ga, wr, wd, wo, g1, b1, rwt, rb, cnt0):
    n = x.shape[0]
    tm = MERGE_TM
    row = lambda w: pl.BlockSpec((tm, w), lambda i: (i, 0))
    const = lambda a: pl.BlockSpec(a.shape, lambda i: (0,) * a.ndim)
    col = pl.BlockSpec((8, tm), lambda i: (0, i))
    tri = jnp.asarray(np.triu(np.ones((tm, tm), np.float32), 1), BF16)
    return pl.pallas_call(
        _merge_kernel,
        grid=(n // tm,),
        in_specs=[row(D_MODEL), row(1024), row(512), row(1024), row(1024), const(wr), const(wd), const(wo),
                  const(g1), const(b1), const(rwt), const(rb), const(tri), const(cnt0)],
        out_specs=[row(D_MODEL), row(QUARTER), row(QUARTER), row(LANES), col, col,
                   pl.BlockSpec((N_EXPERTS, LANES), lambda i: (0, 0))],
        out_shape=[jax.ShapeDtypeStruct((n, D_MODEL), F32), jax.ShapeDtypeStruct((n, QUARTER), I32),
                   jax.ShapeDtypeStruct((n, QUARTER), I32), jax.ShapeDtypeStruct((n, LANES), F32),
                   jax.ShapeDtypeStruct((8, n), I32), jax.ShapeDtypeStruct((8, n), I32),
                   jax.ShapeDtypeStruct((N_EXPERTS, LANES), F32)],
        scratch_shapes=[pltpu.VMEM((N_EXPERTS, 1), F32)],
        compiler_params=pltpu.CompilerParams(dimension_semantics=("arbitrary",), vmem_limit_bytes=48 * MIB),
        name="merge",
    )(x, ret, od, gr, ga, wr, wd, wo, g1, b1, rwt, rb, tri, cnt0)


def _deinterleave_kernel(w_ref, o_ref):
    r = lax.broadcasted_iota(I32, (UP_BLOCK, UP_BLOCK), 0)
    c = lax.broadcasted_iota(I32, (UP_BLOCK, UP_BLOCK), 1)
    src = jnp.where(c < LANES, 2 * c, 2 * (c - LANES) + 1)
    perm = jnp.where(r == src, 1.0, 0.0).astype(BF16)
    for b in range(w_ref.shape[2] // UP_BLOCK):
        sl = slice(b * UP_BLOCK, (b + 1) * UP_BLOCK)
        o_ref[0, :, sl] = _mm(w_ref[0, :, sl].astype(BF16), perm).astype(BF16)


def _deinterleave_w_up(w_up):
    n_e, d_in, d_out = w_up.shape
    cols = d_out
    spec = pl.BlockSpec((1, d_in, cols), lambda e, c: (e, 0, c))
    return pl.pallas_call(
        _deinterleave_kernel,
        grid=(n_e, d_out // cols),
        in_specs=[spec],
        out_specs=spec,
        out_shape=jax.ShapeDtypeStruct(w_up.shape, BF16),
        compiler_params=pltpu.CompilerParams(dimension_semantics=("parallel", "parallel"),
                                             vmem_limit_bytes=40 * MIB),
        name="w_up_prep",
    )(w_up)


def _pos_kernel(off_ref, ek_ref, rk_ref, pos_ref):
    ek = ek_ref[...]
    pos = rk_ref[...]
    for e in range(N_EXPERTS):
        pos = pos + jnp.where(ek == e, off_ref[e], 0)
    pos_ref[...] = pos


def _positions(off, ek, rk):
    n = ek.shape[1]
    tn = min(n, 2048)
    spec = pl.BlockSpec((8, tn), lambda i, off: (0, i))
    return pl.pallas_call(
        _pos_kernel,
        grid_spec=pltpu.PrefetchScalarGridSpec(num_scalar_prefetch=1, grid=(n // tn,), in_specs=[spec, spec],
                                               out_specs=spec),
        out_shape=jax.ShapeDtypeStruct((8, n), I32),
        name="moe_pos",
    )(off, ek, rk)


def _sc_mesh():
    return plsc.VectorSubcoreMesh(core_axis_name="core", subcore_axis_name="subcore")


def _sc_scatter(x, pos_flat, n_rows):
    n = x.shape[0]
    nw = n // SC_WINDOW

    @functools.partial(pl.kernel, out_type=jax.ShapeDtypeStruct((n_rows, x.shape[1]), x.dtype), mesh=_sc_mesh())
    def scatter(x_hbm, p_hbm, o_hbm):
        def body(x_vmem, *p_vmem):
            for p in p_vmem:
                pltpu.sync_copy(x_vmem, o_hbm.at[p.at[0]])

        pltpu.emit_pipeline(
            body, grid=(nw,),
            in_specs=[pl.BlockSpec((SC_WINDOW, x.shape[1]), lambda i: (i, 0))] +
                     [pl.BlockSpec((1, SC_WINDOW), functools.partial(lambda k, i: (0, k * nw + i), k))
                      for k in range(MOE_TOP_K)],
            out_specs=[], core_axis_name=("core", "subcore"), dimension_semantics=(pltpu.PARALLEL,),
        )(x_hbm, *([p_hbm] * MOE_TOP_K))

    return scatter(x, pos_flat)


def _sc_gather(y, pos_flat):
    m = pos_flat.shape[1]

    @functools.partial(pl.kernel, out_type=jax.ShapeDtypeStruct((m, y.shape[1]), y.dtype), mesh=_sc_mesh())
    def gather(y_hbm, p_hbm, o_hbm):
        def body(p_vmem, o_vmem):
            pltpu.sync_copy(y_hbm.at[p_vmem.at[0]], o_vmem)

        pltpu.emit_pipeline(
            body, grid=(m // SC_WINDOW,),
            in_specs=[pl.BlockSpec((1, SC_WINDOW), lambda i: (0, i))],
            out_specs=[pl.BlockSpec((SC_WINDOW, y.shape[1]), lambda i: (i, 0))],
            core_axis_name=("core", "subcore"), dimension_semantics=(pltpu.PARALLEL,),
        )(p_hbm, o_hbm)

    return gather(y, pos_flat)


def _ffn_kernel(be_ref, nu_ref, xlo_ref, xhi_ref, wup_ref, bup_ref, wdn_ref, bdn_ref, ylo_ref, yhi_ref):
    @pl.when(pl.program_id(0) < nu_ref[0])
    def _():
        x = _unpack_rows(xlo_ref[...], xhi_ref[...]).astype(BF16)
        h = _mm(x, wup_ref[0]) + bup_ref[0]
        acts = []
        for b in range(2 * D_FF // UP_BLOCK):
            glu = jnp.minimum(h[:, b * UP_BLOCK:b * UP_BLOCK + LANES], SWIGLU_LIMIT)
            lin = jnp.clip(h[:, b * UP_BLOCK + LANES:(b + 1) * UP_BLOCK], -SWIGLU_LIMIT, SWIGLU_LIMIT)
            acts.append(glu * jax.nn.sigmoid(SWIGLU_ALPHA * glu) * (lin + 1.0))
        act = jnp.concatenate(acts, axis=1)
        ylo_ref[...], yhi_ref[...] = _pack_rows(_mm(act.astype(BF16), wdn_ref[0]) + bdn_ref[0])


def _ffn(block_expert, n_used, xs_lo, xs_hi, wup, bup, wdn, bdn):
    rows = xs_lo.shape[0]
    blk = FFN_BLOCK
    row = pl.BlockSpec((blk, QUARTER), lambda i, be, nu: (i, 0))
    per_expert = lambda a: pl.BlockSpec((1,) + a.shape[1:], lambda i, be, nu: (be[i], 0, 0))
    return pl.pallas_call(
        _ffn_kernel,
        grid_spec=pltpu.PrefetchScalarGridSpec(
            num_scalar_prefetch=2, grid=(rows // blk,),
            in_specs=[row, row, per_expert(wup), per_expert(bup), per_expert(wdn), per_expert(bdn)],
            out_specs=[row, row]),
        out_shape=[jax.ShapeDtypeStruct((rows, QUARTER), I32), jax.ShapeDtypeStruct((rows, QUARTER), I32)],
        compiler_params=pltpu.CompilerParams(dimension_semantics=("arbitrary",), vmem_limit_bytes=40 * MIB),
        name="moe_ffn",
    )(block_expert, n_used, xs_lo, xs_hi, wup, bup, wdn, bdn)


def _combine_kernel(h1_ref, olo_ref, ohi_ref, gtm_ref, g2_ref, b2_ref, o_ref):
    g = gtm_ref[...]
    y = jnp.zeros(h1_ref.shape, F32)
    for k in range(MOE_TOP_K):
        y = y + g[:, k:k + 1] * _unpack_rows(olo_ref[k], ohi_ref[k])
    o_ref[...] = _layer_norm(DEEPNORM_ALPHA * h1_ref[...] + y, g2_ref[...], b2_ref[...])


def _combine(h1, og_lo, og_hi, gtm, g2, b2, row0):
    n = h1.shape[0]
    n_all = og_lo.shape[0] // MOE_TOP_K
    tm = MERGE_TM
    row = lambda w: pl.BlockSpec((tm, w), lambda i: (i, 0))
    const = lambda a: pl.BlockSpec(a.shape, lambda i: (0,) * a.ndim)
    picked = pl.BlockSpec((MOE_TOP_K, tm, QUARTER), lambda i: (0, i + row0 // tm, 0))
    return pl.pallas_call(
        _combine_kernel,
        grid=(n // tm,),
        in_specs=[row(D_MODEL), picked, picked, row(LANES), const(g2), const(b2)],
        out_specs=row(D_MODEL),
        out_shape=jax.ShapeDtypeStruct((n, D_MODEL), F32),
        compiler_params=pltpu.CompilerParams(dimension_semantics=("parallel",), vmem_limit_bytes=40 * MIB),
        name="moe_combine",
    )(h1, og_lo.reshape(MOE_TOP_K, n_all, QUARTER), og_hi.reshape(MOE_TOP_K, n_all, QUARTER), gtm, g2, b2)


def _moe(groups, totals, wup, bup, wdn, bdn, g2, b2):
    blk = FFN_BLOCK
    sizes = [g[0].shape[0] for g in groups]
    n_all = sum(sizes)
    n_rows = MOE_TOP_K * n_all + N_EXPERTS * blk
    counts = totals[:, 0].astype(I32)
    padded = (counts + blk - 1) // blk * blk
    ends = jnp.cumsum(padded)
    block_start = jnp.arange(n_rows // blk, dtype=I32) * blk
    block_expert = jnp.minimum(jnp.sum((ends[None, :] <= block_start[:, None]).astype(I32), axis=1), N_EXPERTS - 1)
    n_used = (ends[-1:] // blk).astype(I32)
    cat = lambda i, axis: jnp.concatenate([g[i] for g in groups], axis=axis)
    pos = _positions((ends - padded).astype(I32), cat(4, 1), cat(5, 1))
    pos_flat = pos[:MOE_TOP_K].reshape(1, MOE_TOP_K * n_all)
    xs_lo = _sc_scatter(cat(1, 0), pos_flat, n_rows)
    xs_hi = _sc_scatter(cat(2, 0), pos_flat, n_rows)
    ys_lo, ys_hi = _ffn(block_expert, n_used, xs_lo, xs_hi, wup, bup, wdn, bdn)
    og_lo, og_hi = _sc_gather(ys_lo, pos_flat), _sc_gather(ys_hi, pos_flat)
    starts = np.cumsum([0] + sizes[:-1]).tolist()
    return [_combine(g[0], og_lo, og_hi, g[3], g2, b2, r0) for g, r0 in zip(groups, starts)]


def _mixer(x, pos_tab, tab_period, s0, rows_per_stream, ret_rows, caches, weights, cnt0):
    wp, wr, wd, wo, g1, b1, rwt, rb = weights
    n = x.shape[0]
    n_streams = n // rows_per_stream
    rq, rk, rv, rg, aq, ak, av, iq, ikw, gr, ga = _project(x, wp, pos_tab, tab_period)
    ret, s_new = _retention(rq, rk, rv, rg, s0, rows_per_stream, ret_rows)
    if caches is None:
        per_stream = lambda a: a.reshape(n_streams, rows_per_stream, a.shape[1])
        od = _dsa(aq, iq, ikw, per_stream(ak), per_stream(av), per_stream(ikw), DSA_TQ)
    else:
        od = _dsa_cached(aq, iq, ikw, *caches, ak, av, rows_per_stream)
    h1, h_lo, h_hi, gtm, ek, rk_, totals = _merge(x, ret, od, gr, ga, wr, wd, wo, g1, b1, rwt, rb, cnt0)
    return (h1, h_lo, h_hi, gtm, ek, rk_), totals, (s_new, ak, av, ikw[:, :IDX_DIM])


def kernel(x_prompt, x_sample, state_ret, cache_k, cache_v, cache_idx_k, w_in, w_ret_o, w_dsa_o, w_o,
           ln1_g, ln1_b, router_w, router_b, w_up, b_up, w_down, b_down, ln2_g, ln2_b):
    assert w_in.shape[0] == DEPTH
    batch, seq, _ = x_prompt.shape
    dec_batch, dec_seq, _ = x_sample.shape
    past = cache_k.shape[2]
    assert seq % PROJ_TM == 0 and seq % DSA_TQ == 0 and PROJ_TM % dec_seq == 0

    l = 0
    mixer_w = (_pack_w_in(w_in[l]), w_ret_o[l].astype(BF16), w_dsa_o[l].astype(BF16), w_o[l].astype(BF16),
               ln1_g[l][None, :], ln1_b[l][None, :], router_w[l].T, router_b[l][:, None])
    moe_w = (
        _deinterleave_w_up(w_up[l]),
        b_up[l].reshape(N_EXPERTS, 2 * D_FF // UP_BLOCK, LANES, 2).transpose(0, 1, 3, 2).reshape(N_EXPERTS, 1, 2 * D_FF),
        w_down[l].astype(BF16), b_down[l][:, None, :], ln2_g[l][None, :], ln2_b[l][None, :])

    tab_p = _rot_tables(jnp.arange(seq))
    zeros_state = jnp.zeros((batch, RET_HEADS, RET_DK, RET_DV), F32)
    moe_p, totals_p, (s_p, k_p, v_p, ik_p) = _mixer(
        x_prompt.reshape(batch * seq, D_MODEL), tab_p, seq // PROJ_TM, zeros_state, seq, RET_CHUNK, None, mixer_w,
        jnp.zeros((N_EXPERTS, LANES), F32))

    tab_s = jnp.tile(_rot_tables(past + jnp.arange(dec_seq)), (PROJ_TM // dec_seq, 1))
    caches = (cache_k[l].reshape(dec_batch, past, LANES), cache_v[l].reshape(dec_batch, past, LANES), cache_idx_k[l])
    moe_s, totals, (s_s, k_s, v_s, ik_s) = _mixer(
        x_sample.reshape(dec_batch * dec_seq, D_MODEL), tab_s, 1, state_ret[l], dec_seq, dec_seq, caches, mixer_w,
        totals_p)

    y_p, y_s = _moe([moe_p, moe_s], totals, *moe_w)

    kv = (DSA_KV_HEADS, DSA_HEAD_DIM)
    return (y_p.reshape(batch, seq, D_MODEL), y_s.reshape(dec_batch, dec_seq, D_MODEL),
            s_p[None], k_p.reshape(1, batch, seq, *kv), v_p.reshape(1, batch, seq, *kv),
            ik_p.reshape(1, batch, seq, IDX_DIM),
            s_s[None], k_s.reshape(1, dec_batch, dec_seq, *kv), v_s.reshape(1, dec_batch, dec_seq, *kv),
            ik_s.reshape(1, dec_batch, dec_seq, IDX_DIM))
```

```python
import functools

import numpy as np
import jax
import jax.numpy as jnp
from jax import lax
from jax.experimental import pallas as pl
from jax.experimental.pallas import tpu as pltpu
from jax.experimental.pallas import tpu_sc as plsc

F32 = jnp.float32
BF16 = jnp.bfloat16
I32 = jnp.int32

D_MODEL = 1024
CHUNK = 64
RET_HEADS = 4
RET_DK = 128
RET_DV = 256
RET_ROPE_BASE = 10000.0
DSA_HEADS = 8
DSA_KV_HEADS = 2
DSA_HEAD_DIM = 64
IDX_HEADS = 8
IDX_DIM = 64
DSA_TOPK = 256
ROPE_THETA = 500000.0
N_EXPERTS = 32
MOE_TOP_K = 4
D_FF = 1024
SWIGLU_ALPHA = 1.702
SWIGLU_LIMIT = 7.0
LN_EPS = 1e-5
GN_EPS = 1e-6
DEPTH = 1
DEEPNORM_ALPHA = (2.0 * DEPTH) ** 0.25
PROJ_WIDTHS = (RET_HEADS * RET_DK, RET_HEADS * RET_DK, RET_HEADS * RET_DV, RET_HEADS * RET_DV,
               DSA_HEADS * DSA_HEAD_DIM, DSA_KV_HEADS * DSA_HEAD_DIM, DSA_KV_HEADS * DSA_HEAD_DIM,
               IDX_HEADS * IDX_DIM, IDX_DIM, IDX_HEADS, D_MODEL, D_MODEL)

LANES = 128
MIB = 1024 * 1024

OFF_RQ, OFF_RK, OFF_RV, OFF_RG = 0, 512, 1024, 2048
OFF_AQ, OFF_AK, OFF_AV, OFF_IQ, OFF_IKW = 3072, 3584, 3712, 3840, 4352
OFF_GR, OFF_GA, PACKED_COLS = 4480, 5504, 6528
TAB_COLS = 8 * LANES

PROJ_TM = 512
RET_CHUNK = 128
RET_CHUNKS_PER_STEP = 8
DSA_TQ = 256
DSA_ROWS = 512
DSA_GROUP_MAX = 4
SEARCH_UNROLL = 4
MERGE_TM = 512
FFN_BLOCK = 512
SC_WINDOW = 128
QUARTER = D_MODEL // 4
UP_BLOCK = 2 * LANES


def _nt(a, b):
    return lax.dot_general(a, b, (((1,), (1,)), ((), ())), preferred_element_type=F32)


def _mm(a, b):
    return jnp.dot(a, b, preferred_element_type=F32)


def _proj_kernel(x_ref, w_ref, tab_ref, rq_ref, rk_ref, rv_ref, rg_ref, aq_ref, ak_ref, av_ref,
                 iq_ref, ikw_ref, gr_ref, ga_ref):
    xb = x_ref[...].astype(BF16)

    def mm(c0, n):
        return _mm(xb, w_ref[:, c0:c0 + n])

    def tab(i):
        return tab_ref[:, i * LANES:(i + 1) * LANES]

    def rot_full(z):
        return z * tab(0) + pltpu.roll(z, 64, 1) * tab(1)

    def rot_part(z, c):
        return z * tab(c) + pltpu.roll(z, LANES - 8, 1) * tab(c + 1) + pltpu.roll(z, 8, 1) * tab(c + 2)

    z = mm(OFF_RQ, 512)
    for h in range(4):
        sl = slice(h * LANES, (h + 1) * LANES)
        rq_ref[:, sl] = rot_full(z[:, sl]).astype(BF16)
    z = mm(OFF_RK, 512)
    for h in range(4):
        sl = slice(h * LANES, (h + 1) * LANES)
        rk_ref[:, sl] = (rot_full(z[:, sl]) * (RET_DK ** -0.5)).astype(BF16)
    for c in range(2):
        rv_ref[:, c * 512:(c + 1) * 512] = mm(OFF_RV + c * 512, 512).astype(BF16)
    for c in range(2):
        z = mm(OFF_RG + c * 512, 512)
        rg_ref[:, c * 512:(c + 1) * 512] = z * jax.nn.sigmoid(z)
    z = mm(OFF_AQ, 512)
    for h in range(4):
        sl = slice(h * LANES, (h + 1) * LANES)
        aq_ref[:, sl] = rot_part(z[:, sl], 2).astype(BF16)
    z = mm(OFF_AK, 256)
    ak_ref[...] = rot_part(z[:, :LANES], 2)
    av_ref[...] = z[:, LANES:]
    z = mm(OFF_IQ, 512)
    for h in range(4):
        sl = slice(h * LANES, (h + 1) * LANES)
        iq_ref[:, sl] = rot_part(z[:, sl], 2).astype(BF16)
    ikw_ref[...] = rot_part(mm(OFF_IKW, LANES), 5)
    for c in range(2):
        gr_ref[:, c * 512:(c + 1) * 512] = jax.nn.sigmoid(mm(OFF_GR + c * 512, 512))
    for c in range(2):
        ga_ref[:, c * 512:(c + 1) * 512] = jax.nn.sigmoid(mm(OFF_GA + c * 512, 512))


def _rot_tables(pos):
    p = pos.shape[0]
    posf = pos.astype(F32)[:, None]
    ret_f = RET_ROPE_BASE ** (-jnp.linspace(0.0, 1.0, RET_DK // 2, dtype=F32))
    ang = posf * ret_f[None, :]
    c, s = jnp.cos(ang), jnp.sin(ang)
    cos_r = jnp.concatenate([c, c], 1)
    sin_r = jnp.concatenate([-s, s], 1)
    n_rot = DSA_HEAD_DIM // 4
    att_f = ROPE_THETA ** (-jnp.arange(0, n_rot, 2, dtype=F32) / n_rot)
    ang2 = posf * att_f[None, :]
    c2, s2 = jnp.cos(ang2), jnp.sin(ang2)
    half = n_rot // 2
    rest = DSA_HEAD_DIM - 2 * half
    c64 = jnp.concatenate([c2, c2, jnp.ones((p, rest), F32)], 1)
    s1_64 = jnp.concatenate([-s2, jnp.zeros((p, DSA_HEAD_DIM - half), F32)], 1)
    s2_64 = jnp.concatenate([jnp.zeros((p, half), F32), s2, jnp.zeros((p, rest), F32)], 1)
    z64 = jnp.zeros((p, DSA_HEAD_DIM), F32)
    ci = jnp.concatenate([c64, jnp.full((p, IDX_HEADS), IDX_HEADS ** -0.5, F32),
                          jnp.zeros((p, DSA_HEAD_DIM - IDX_HEADS), F32)], 1)
    return jnp.concatenate([cos_r, sin_r,
                            jnp.concatenate([c64, c64], 1), jnp.concatenate([s1_64, s1_64], 1),
                            jnp.concatenate([s2_64, s2_64], 1),
                            ci, jnp.concatenate([s1_64, z64], 1), jnp.concatenate([s2_64, z64], 1)], 1)


def _pack_w_in(w):
    cuts = np.cumsum(PROJ_WIDTHS)[:-1].tolist()
    rq, rk, rv, rg, aq, ak, av, iq, ik, iw, gr, ga = jnp.split(w, cuts, axis=1)
    pad = jnp.zeros((w.shape[0], LANES - IDX_DIM - IDX_HEADS), w.dtype)
    return jnp.concatenate([rq, rk, rv, rg, aq, ak, av, iq, ik, iw, pad, gr, ga], axis=1).astype(BF16)


def _project(x, wp, tab, tab_period):
    n = x.shape[0]
    tm = PROJ_TM
    row = lambda w: pl.BlockSpec((tm, w), lambda i: (i, 0))
    out_shapes = [((n, 512), BF16), ((n, 512), BF16), ((n, 1024), BF16), ((n, 1024), F32),
                  ((n, 512), BF16), ((n, LANES), F32), ((n, LANES), F32), ((n, 512), BF16),
                  ((n, LANES), F32), ((n, 1024), F32), ((n, 1024), F32)]
    return pl.pallas_call(
        _proj_kernel,
        grid=(n // tm,),
        in_specs=[row(D_MODEL),
                  pl.BlockSpec((D_MODEL, PACKED_COLS), lambda i: (0, 0), pipeline_mode=pl.Buffered(1)),
                  pl.BlockSpec((tm, TAB_COLS), lambda i: (i % tab_period, 0))],
        out_specs=[row(s[1]) for s, _ in out_shapes],
        out_shape=[jax.ShapeDtypeStruct(s, d) for s, d in out_shapes],
        compiler_params=pltpu.CompilerParams(dimension_semantics=("parallel",), vmem_limit_bytes=52 * MIB),
        name="proj",
    )(x, wp, tab)


def _ret_kernel(dec_ref, xi_ref, zeta_ref, rq_ref, rk_ref, rv_ref, rg_ref, s0_ref, ret_ref, sout_ref, st_ref,
                *, rows, n_chunk, g_pow):
    j = pl.program_id(1)
    cpad = RET_CHUNK

    @pl.when(j == 0)
    def _():
        st_ref[...] = s0_ref[0]

    def padded(v):
        if rows == cpad:
            return v
        return jnp.concatenate([v, jnp.zeros((cpad - rows, v.shape[1]), v.dtype)], axis=0)

    for c in range(n_chunk):
        rs = slice(c * rows, (c + 1) * rows)
        for h in range(RET_HEADS):
            ks = slice(h * RET_DK, (h + 1) * RET_DK)
            vs = slice(h * RET_DV, (h + 1) * RET_DV)
            q = padded(rq_ref[rs, ks])
            kt = padded(rk_ref[rs, ks].astype(F32)).T
            v = padded(rv_ref[rs, vs])
            s = st_ref[h]
            sc = _mm(q, kt.astype(BF16)) * dec_ref[h]
            o = _mm(sc.astype(BF16), v) + _mm(q, s.astype(BF16)) * xi_ref[h]
            st_ref[h] = g_pow[h] * s + _mm((kt * zeta_ref[h]).astype(BF16), v)
            o = o[:rows]
            mu = jnp.mean(o, axis=-1, keepdims=True)
            d = o - mu
            var = jnp.mean(d * d, axis=-1, keepdims=True)
            gn = d * lax.rsqrt(var + GN_EPS)
            ret_ref[rs, vs] = (gn * rg_ref[rs, vs]).astype(BF16)

    @pl.when(j == pl.num_programs(1) - 1)
    def _():
        sout_ref[0] = st_ref[...]


def _retention(rq, rk, rv, rg, s0, rows_per_stream, rows):
    n = rq.shape[0]
    n_streams = n // rows_per_stream
    n_chunk = min(rows_per_stream // rows, RET_CHUNKS_PER_STEP)
    blk = rows * n_chunk
    nb = rows_per_stream // blk
    gam = 1.0 - 2.0 ** (-5.0 - np.arange(RET_HEADS, dtype=np.float64))
    i = np.arange(RET_CHUNK, dtype=np.float64)
    diff = i[:, None] - i[None, :]
    dec = np.where(diff >= 0, gam[:, None, None] ** np.maximum(diff, 0.0)[None], 0.0)
    xi = gam[:, None, None] ** (i + 1.0)[None, :, None]
    zeta = np.where(i < rows, gam[:, None, None] ** (rows - 1.0 - i)[None, None, :], 0.0)
    g_pow = tuple(float(g ** rows) for g in gam)
    const = lambda shape: pl.BlockSpec(shape, lambda s, j: (0,) * len(shape))
    row = lambda w: pl.BlockSpec((blk, w), lambda s, j: (s * nb + j, 0))
    st = pl.BlockSpec((1, RET_HEADS, RET_DK, RET_DV), lambda s, j: (s, 0, 0, 0))
    return pl.pallas_call(
        functools.partial(_ret_kernel, rows=rows, n_chunk=n_chunk, g_pow=g_pow),
        grid=(n_streams, nb),
        in_specs=[const((RET_HEADS, RET_CHUNK, RET_CHUNK)), const((RET_HEADS, RET_CHUNK, 1)),
                  const((RET_HEADS, 1, RET_CHUNK)), row(512), row(512), row(1024), row(1024), st],
        out_specs=[row(1024), st],
        out_shape=[jax.ShapeDtypeStruct((n, RET_HEADS * RET_DV), BF16),
                   jax.ShapeDtypeStruct((n_streams, RET_HEADS, RET_DK, RET_DV), F32)],
        scratch_shapes=[pltpu.VMEM((RET_HEADS, RET_DK, RET_DV), F32)],
        compiler_params=pltpu.CompilerParams(dimension_semantics=("parallel", "arbitrary"),
                                             vmem_limit_bytes=32 * MIB),
        name="retention",
    )(jnp.asarray(dec, F32), jnp.asarray(xi, F32), jnp.asarray(zeta, F32), rq, rk, rv, rg, s0)


def _dsa_kernel(aq_ref, iq_ref, ikwq_ref, kk_ref, vv_ref, ikk_ref, *rest,
                group, tq, n_keys, row0, limit_const, n_sel):
    o_ref, w_ref, bias_ref = rest[-3:]
    rows = group * tq
    nsel_f = float(n_sel)
    neg_inf = -jnp.inf

    col = lax.broadcasted_iota(I32, (tq, n_keys), 1)
    if limit_const is None:
        limit = (jnp.right_shift(lax.broadcasted_iota(I32, (tq, 1), 0) + row0, 6) + 1) * CHUNK
    else:
        limit = limit_const

    def score_stream(s, carry):
        ikb = ikk_ref[s][:, :IDX_DIM].astype(BF16)
        iww = ikwq_ref[s][:, IDX_DIM:IDX_DIM + IDX_HEADS] * (IDX_DIM ** -0.5)
        acc = jnp.zeros((tq, n_keys), F32)
        for p in range(IDX_HEADS // 2):
            slab = iq_ref[s, :, p * LANES:(p + 1) * LANES]
            for hh in range(2):
                h = 2 * p + hh
                acc = acc + jnp.maximum(_nt(slab[:, hh * IDX_DIM:(hh + 1) * IDX_DIM], ikb), 0.0) * iww[:, h:h + 1]
        w_ref[pl.ds(pl.multiple_of(s * tq, tq), tq), :] = jnp.where(col < limit, acc, neg_inf)
        return carry

    lax.fori_loop(0, group, score_stream, 0)

    sc = w_ref[...]
    pos = jnp.sum(jnp.where(sc >= 0.0, 1.0, 0.0), axis=1, keepdims=True) >= nsel_f
    kk = jnp.where(pos, nsel_f, float(n_keys - n_sel + 1))
    w_ref[...] = jnp.where(pos, sc, -sc)

    def bit_step(i, u):
        cand_u = u | jnp.left_shift(jnp.int32(1), 30 - i)
        cand = pltpu.bitcast(cand_u, F32)
        cnt = jnp.sum(jnp.where(w_ref[...] >= cand, 1.0, 0.0), axis=1, keepdims=True)
        return jnp.where(cnt >= kk, cand_u, u)

    mag_u = lax.fori_loop(0, 31, bit_step, jnp.zeros((rows, 1), I32), unroll=SEARCH_UNROLL)
    mag = pltpu.bitcast(mag_u, F32)
    thr = jnp.where(pos, mag, -mag)

    sc = jnp.where(pos, w_ref[...], -w_ref[...])
    short = jnp.sum(jnp.where(sc >= thr, 1.0, 0.0), axis=1, keepdims=True) < nsel_f
    thr = jnp.where(jnp.logical_and(short, jnp.logical_not(pos)), -pltpu.bitcast(mag_u + 1, F32), thr)
    ge = sc >= thr
    cnt_gt = jnp.sum(jnp.where(sc > thr, 1.0, 0.0), axis=1, keepdims=True)
    cnt_ge = jnp.sum(jnp.where(ge, 1.0, 0.0), axis=1, keepdims=True)
    bias_ref[...] = jnp.where(jnp.logical_and(ge, sc > neg_inf), 0.0, neg_inf)
    excess = jnp.logical_and(cnt_ge > nsel_f, thr > neg_inf)

    @pl.when(jnp.max(jnp.where(excess, 1.0, 0.0)) > 0.0)
    def _():
        need = nsel_f - cnt_gt
        tri = jnp.where(lax.broadcasted_iota(I32, (LANES, LANES), 0) < lax.broadcasted_iota(I32, (LANES, LANES), 1),
                        1.0, 0.0).astype(BF16)
        before = jnp.zeros((rows, 1), F32)
        for b in range(n_keys // LANES):
            sl = slice(b * LANES, (b + 1) * LANES)
            sblk = jnp.where(pos, w_ref[:, sl], -w_ref[:, sl])
            eq = jnp.where(sblk == thr, 1.0, 0.0)
            rank = _mm(eq.astype(BF16), tri) + before
            keep = jnp.logical_or(sblk > thr, jnp.logical_and(sblk == thr, rank < need))
            bias_ref[:, sl] = jnp.where(jnp.logical_and(keep, sblk > neg_inf), 0.0, neg_inf)
            before = before + jnp.sum(eq, axis=1, keepdims=True)

    def attend_stream(s, carry):
        bias = bias_ref[pl.ds(pl.multiple_of(s * tq, tq), tq), :]
        kfull = kk_ref[s]
        vfull = vv_ref[s]
        for g in range(DSA_KV_HEADS):
            gs = slice(g * DSA_HEAD_DIM, (g + 1) * DSA_HEAD_DIM)
            kg = kfull[:, gs].astype(BF16)
            vg = vfull[:, gs].astype(BF16)
            for pp in range(2):
                p = 2 * g + pp
                slab = (aq_ref[s, :, p * LANES:(p + 1) * LANES].astype(F32) * (DSA_HEAD_DIM ** -0.5)).astype(BF16)
                outs = []
                for hh in range(2):
                    lg = _nt(slab[:, hh * DSA_HEAD_DIM:(hh + 1) * DSA_HEAD_DIM], kg) + bias
                    m = jnp.max(lg, axis=1, keepdims=True)
                    pr = jnp.exp(lg - m)
                    den = jnp.sum(pr, axis=1, keepdims=True)
                    outs.append(_mm(pr.astype(BF16), vg) / den)
                o_ref[s, :, p * LANES:(p + 1) * LANES] = jnp.concatenate(outs, axis=1).astype(BF16)
        return carry

    lax.fori_loop(0, group, attend_stream, 0)


def _dsa_call(aq, iq, ikw, kk, vv, ikk, group, tq, jq, n_keys, limit_const, n_sel, name, prev=None):
    n_streams, rows_per_stream = aq.shape[0], aq.shape[1]
    extra_specs, extra_args, aliases = [], (), {}
    if prev is not None:
        extra_specs, extra_args, aliases = [pl.BlockSpec(memory_space=pl.ANY)], (prev,), {6: 0}
    qrow = lambda a: pl.BlockSpec((group, tq, a.shape[2]), lambda s: (s, jq, 0))
    krow = lambda a: pl.BlockSpec((group, n_keys, a.shape[2]), lambda s: (s, 0, 0))
    width = DSA_HEADS * DSA_HEAD_DIM
    return pl.pallas_call(
        functools.partial(_dsa_kernel, group=group, tq=tq, n_keys=n_keys, row0=jq * tq, limit_const=limit_const,
                          n_sel=n_sel),
        grid=(n_streams // group,),
        in_specs=[qrow(aq), qrow(iq), qrow(ikw), krow(kk), krow(vv), krow(ikk)] + extra_specs,
        out_specs=pl.BlockSpec((group, tq, width), lambda s: (s, jq, 0)),
        out_shape=jax.ShapeDtypeStruct((n_streams, rows_per_stream, width), BF16),
        input_output_aliases=aliases,
        scratch_shapes=[pltpu.VMEM((group * tq, n_keys), F32), pltpu.VMEM((group * tq, n_keys), F32)],
        compiler_params=pltpu.CompilerParams(dimension_semantics=("parallel",), vmem_limit_bytes=56 * MIB),
        name=name,
    )(aq, iq, ikw, kk, vv, ikk, *extra_args)


def _dsa_cached_kernel(aq_ref, iq_ref, ikwq_ref, ck_ref, cv_ref, cik_ref, nk_ref, nv_ref, o_ref,
                       w_ref, bias_ref, kk_ref, vv_ref, ikk_ref, **static):
    past, new = ck_ref.shape[1], nk_ref.shape[1]
    for dst, cache, fresh in ((kk_ref, ck_ref, nk_ref), (vv_ref, cv_ref, nv_ref), (ikk_ref, cik_ref, ikwq_ref)):
        dst[:, :past, :cache.shape[2]] = cache[...]
        dst[:, past:past + new, :] = fresh[...]
        dst[:, past + new:, :] = jnp.zeros((dst.shape[0], dst.shape[1] - past - new, dst.shape[2]), F32)
    _dsa_kernel(aq_ref, iq_ref, ikwq_ref, kk_ref, vv_ref, ikk_ref, o_ref, w_ref, bias_ref, **static)


def _dsa_cached(aq, iq, ikw, cache_k, cache_v, cache_ik, new_k, new_v, tq):
    n_streams, past, _ = cache_k.shape
    n = aq.shape[0]
    per_stream = lambda a: a.reshape(n_streams, tq, a.shape[1])
    aq, iq, ikw, new_k, new_v = per_stream(aq), per_stream(iq), per_stream(ikw), per_stream(new_k), per_stream(new_v)
    limit = past + tq
    n_keys = -(-limit // LANES) * LANES
    group = max(1, min(n_streams, DSA_ROWS // tq, DSA_GROUP_MAX))
    blk = lambda a: pl.BlockSpec((group,) + a.shape[1:], lambda s: (s, 0, 0))
    width = DSA_HEADS * DSA_HEAD_DIM
    return pl.pallas_call(
        functools.partial(_dsa_cached_kernel, group=group, tq=tq, n_keys=n_keys, row0=0, limit_const=limit,
                          n_sel=min(DSA_TOPK, limit // 4)),
        grid=(n_streams // group,),
        in_specs=[blk(a) for a in (aq, iq, ikw, cache_k, cache_v, cache_ik, new_k, new_v)],
        out_specs=pl.BlockSpec((group, tq, width), lambda s: (s, 0, 0)),
        out_shape=jax.ShapeDtypeStruct((n_streams, tq, width), BF16),
        scratch_shapes=[pltpu.VMEM((group * tq, n_keys), F32), pltpu.VMEM((group * tq, n_keys), F32)] +
                       [pltpu.VMEM((group, n_keys, LANES), F32)] * 3,
        compiler_params=pltpu.CompilerParams(dimension_semantics=("parallel",), vmem_limit_bytes=48 * MIB),
        name="dsa_s",
    )(aq, iq, ikw, cache_k, cache_v, cache_ik, new_k, new_v).reshape(n, width)


def _dsa(aq, iq, ikw, kk, vv, ikk, tq):
    n_streams, n_keys, _ = kk.shape
    n = aq.shape[0]
    per_stream = lambda a: a.reshape(n_streams, n // n_streams, a.shape[1])
    aq, iq, ikw = per_stream(aq), per_stream(iq), per_stream(ikw)
    nq = n // n_streams // tq
    n_sel = min(DSA_TOPK, n_keys // 4)
    group = max(1, min(n_streams, DSA_ROWS // tq, DSA_GROUP_MAX))
    out = None
    for jq in range(nq):
        out = _dsa_call(aq, iq, ikw, kk, vv, ikk, group, tq, jq, (jq + 1) * tq, None, n_sel, f"dsa_p{jq}", prev=out)
    return out.reshape(n, -1)


def _pack_rows(v):
    q = QUARTER
    bits = lambda x: pltpu.bitcast(x.astype(BF16).astype(F32), I32)
    pair = lambda c: lax.shift_right_logical(bits(v[:, c * q:(c + 1) * q]), 16) | bits(v[:, (c + 1) * q:(c + 2) * q])
    return pair(0), pair(2)


def _unpack_rows(lo, hi):
    parts = []
    for w in (lo, hi):
        parts.append(pltpu.bitcast(lax.shift_left(w, 16), F32))
        parts.append(pltpu.bitcast(w & jnp.int32(-65536), F32))
    return jnp.concatenate(parts, axis=1)


def _layer_norm(v, g, b):
    mu = jnp.mean(v, axis=-1, keepdims=True)
    d = v - mu
    var = jnp.mean(d * d, axis=-1, keepdims=True)
    return d * lax.rsqrt(var + LN_EPS) * g + b


def _merge_kernel(x_ref, ret_ref, od_ref, gr_ref, ga_ref, wr_ref, wd_ref, wo_ref, g1_ref, b1_ref, rwt_ref, rb_ref,
                  tri_ref, cnt0_ref, h1_ref, hlo_ref, hhi_ref, gtm_ref, ek_ref, rk_ref, tot_ref, cnt_ref):
    @pl.when(pl.program_id(0) == 0)
    def _():
        cnt_ref[...] = cnt0_ref[:, 0:1]

    y_ret = _mm(ret_ref[...], wr_ref[...])
    y_dsa = _mm(od_ref[...], wd_ref[...])
    merged = gr_ref[...] * y_ret + ga_ref[...] * y_dsa
    mix = _mm(merged.astype(BF16), wo_ref[...])
    h1 = _layer_norm(DEEPNORM_ALPHA * x_ref[...] + mix, g1_ref[...], b1_ref[...])
    h1_ref[...] = h1
    hlo_ref[...], hhi_ref[...] = _pack_rows(h1)

    logits = lax.dot_general(rwt_ref[...], h1, (((1,), (1,)), ((), ())), preferred_element_type=F32,
                             precision=lax.Precision.HIGHEST) + rb_ref[...]
    tm = logits.shape[1]
    e_iota = lax.broadcasted_iota(I32, (N_EXPERTS, tm), 0)
    tops, hots, firsts = [], [], []
    for _ in range(MOE_TOP_K):
        m = jnp.max(logits, axis=0, keepdims=True)
        first = jnp.min(jnp.where(logits == m, e_iota, N_EXPERTS), axis=0, keepdims=True)
        hot = e_iota == first
        tops.append(m)
        hots.append(hot)
        firsts.append(first)
        logits = jnp.where(hot, -jnp.inf, logits)
    exps = [jnp.exp(m - tops[0]) for m in tops]
    den = exps[0] + exps[1] + exps[2] + exps[3]
    sel = jnp.zeros((N_EXPERTS, tm), F32)
    for hot in hots:
        sel = sel + jnp.where(hot, 1.0, 0.0)
    rank = _mm(sel.astype(BF16), tri_ref[...]) + cnt_ref[...]
    ranks = [jnp.sum(jnp.where(hot, rank, 0.0), axis=0, keepdims=True).astype(I32) for hot in hots]
    pad_i = jnp.zeros((8 - MOE_TOP_K, tm), I32)
    ek_ref[...] = jnp.concatenate(firsts + [pad_i], axis=0)
    rk_ref[...] = jnp.concatenate(ranks + [pad_i], axis=0)
    gates = jnp.concatenate([e / den for e in exps] + [jnp.zeros((LANES - MOE_TOP_K, tm), F32)], axis=0)
    gtm_ref[...] = gates.T
    cnt_ref[...] = cnt_ref[...] + jnp.sum(sel, axis=1, keepdims=True)
    tot_ref[...] = jnp.broadcast_to(cnt_ref[...], tot_ref.shape)


def _merge(x, ret, od, gr, ga, wr, wd, wo, g1, b1, rwt, rb, cnt0):
    n = x.shape[0]
    tm = MERGE_TM
    row = lambda w: pl.BlockSpec((tm, w), lambda i: (i, 0))
    const = lambda a: pl.BlockSpec(a.shape, lambda i: (0,) * a.ndim)
    col = pl.BlockSpec((8, tm), lambda i: (0, i))
    tri = jnp.asarray(np.triu(np.ones((tm, tm), np.float32), 1), BF16)
    return pl.pallas_call(
        _merge_kernel,
        grid=(n // tm,),
        in_specs=[row(D_MODEL), row(1024), row(512), row(1024), row(1024), const(wr), const(wd), const(wo),
                  const(g1), const(b1), const(rwt), const(rb), const(tri), const(cnt0)],
        out_specs=[row(D_MODEL), row(QUARTER), row(QUARTER), row(LANES), col, col,
                   pl.BlockSpec((N_EXPERTS, LANES), lambda i: (0, 0))],
        out_shape=[jax.ShapeDtypeStruct((n, D_MODEL), F32), jax.ShapeDtypeStruct((n, QUARTER), I32),
                   jax.ShapeDtypeStruct((n, QUARTER), I32), jax.ShapeDtypeStruct((n, LANES), F32),
                   jax.ShapeDtypeStruct((8, n), I32), jax.ShapeDtypeStruct((8, n), I32),
                   jax.ShapeDtypeStruct((N_EXPERTS, LANES), F32)],
        scratch_shapes=[pltpu.VMEM((N_EXPERTS, 1), F32)],
        compiler_params=pltpu.CompilerParams(dimension_semantics=("arbitrary",), vmem_limit_bytes=48 * MIB),
        name="merge",
    )(x, ret, od, gr, ga, wr, wd, wo, g1, b1, rwt, rb, tri, cnt0)


def _deinterleave_kernel(w_ref, o_ref):
    r = lax.broadcasted_iota(I32, (UP_BLOCK, UP_BLOCK), 0)
    c = lax.broadcasted_iota(I32, (UP_BLOCK, UP_BLOCK), 1)
    src = jnp.where(c < LANES, 2 * c, 2 * (c - LANES) + 1)
    perm = jnp.where(r == src, 1.0, 0.0).astype(BF16)
    for b in range(w_ref.shape[2] // UP_BLOCK):
        sl = slice(b * UP_BLOCK, (b + 1) * UP_BLOCK)
        o_ref[0, :, sl] = _mm(w_ref[0, :, sl].astype(BF16), perm).astype(BF16)


def _deinterleave_w_up(w_up):
    n_e, d_in, d_out = w_up.shape
    cols = d_out
    spec = pl.BlockSpec((1, d_in, cols), lambda e, c: (e, 0, c))
    return pl.pallas_call(
        _deinterleave_kernel,
        grid=(n_e, d_out // cols),
        in_specs=[spec],
        out_specs=spec,
        out_shape=jax.ShapeDtypeStruct(w_up.shape, BF16),
        compiler_params=pltpu.CompilerParams(dimension_semantics=("parallel", "parallel"),
                                             vmem_limit_bytes=40 * MIB),
        name="w_up_prep",
    )(w_up)


def _pos_kernel(off_ref, ek_ref, rk_ref, pos_ref):
    ek = ek_ref[...]
    pos = rk_ref[...]
    for e in range(N_EXPERTS):
        pos = pos + jnp.where(ek == e, off_ref[e], 0)
    pos_ref[...] = pos


def _positions(off, ek, rk):
    n = ek.shape[1]
    tn = min(n, 2048)
    spec = pl.BlockSpec((8, tn), lambda i, off: (0, i))
    return pl.pallas_call(
        _pos_kernel,
        grid_spec=pltpu.PrefetchScalarGridSpec(num_scalar_prefetch=1, grid=(n // tn,), in_specs=[spec, spec],
                                               out_specs=spec),
        out_shape=jax.ShapeDtypeStruct((8, n), I32),
        name="moe_pos",
    )(off, ek, rk)


def _sc_mesh():
    return plsc.VectorSubcoreMesh(core_axis_name="core", subcore_axis_name="subcore")


def _sc_scatter(x, pos_flat, n_rows):
    n = x.shape[0]
    nw = n // SC_WINDOW

    @functools.partial(pl.kernel, out_type=jax.ShapeDtypeStruct((n_rows, x.shape[1]), x.dtype), mesh=_sc_mesh())
    def scatter(x_hbm, p_hbm, o_hbm):
        def body(x_vmem, *p_vmem):
            for p in p_vmem:
                pltpu.sync_copy(x_vmem, o_hbm.at[p.at[0]])

        pltpu.emit_pipeline(
            body, grid=(nw,),
            in_specs=[pl.BlockSpec((SC_WINDOW, x.shape[1]), lambda i: (i, 0))] +
                     [pl.BlockSpec((1, SC_WINDOW), functools.partial(lambda k, i: (0, k * nw + i), k))
                      for k in range(MOE_TOP_K)],
            out_specs=[], core_axis_name=("core", "subcore"), dimension_semantics=(pltpu.PARALLEL,),
        )(x_hbm, *([p_hbm] * MOE_TOP_K))

    return scatter(x, pos_flat)


def _sc_gather(y, pos_flat):
    m = pos_flat.shape[1]

    @functools.partial(pl.kernel, out_type=jax.ShapeDtypeStruct((m, y.shape[1]), y.dtype), mesh=_sc_mesh())
    def gather(y_hbm, p_hbm, o_hbm):
        def body(p_vmem, o_vmem):
            pltpu.sync_copy(y_hbm.at[p_vmem.at[0]], o_vmem)

        pltpu.emit_pipeline(
            body, grid=(m // SC_WINDOW,),
            in_specs=[pl.BlockSpec((1, SC_WINDOW), lambda i: (0, i))],
            out_specs=[pl.BlockSpec((SC_WINDOW, y.shape[1]), lambda i: (i, 0))],
            core_axis_name=("core", "subcore"), dimension_semantics=(pltpu.PARALLEL,),
        )(p_hbm, o_hbm)

    return gather(y, pos_flat)


def _ffn_kernel(be_ref, nu_ref, xlo_ref, xhi_ref, wup_ref, bup_ref, wdn_ref, bdn_ref, ylo_ref, yhi_ref):
    @pl.when(pl.program_id(0) < nu_ref[0])
    def _():
        x = _unpack_rows(xlo_ref[...], xhi_ref[...]).astype(BF16)
        h = _mm(x, wup_ref[0]) + bup_ref[0]
        acts = []
        for b in range(2 * D_FF // UP_BLOCK):
            glu = jnp.minimum(h[:, b * UP_BLOCK:b * UP_BLOCK + LANES], SWIGLU_LIMIT)
            lin = jnp.clip(h[:, b * UP_BLOCK + LANES:(b + 1) * UP_BLOCK], -SWIGLU_LIMIT, SWIGLU_LIMIT)
            acts.append(glu * jax.nn.sigmoid(SWIGLU_ALPHA * glu) * (lin + 1.0))
        act = jnp.concatenate(acts, axis=1)
        ylo_ref[...], yhi_ref[...] = _pack_rows(_mm(act.astype(BF16), wdn_ref[0]) + bdn_ref[0])


def _ffn(block_expert, n_used, xs_lo, xs_hi, wup, bup, wdn, bdn):
    rows = xs_lo.shape[0]
    blk = FFN_BLOCK
    row = pl.BlockSpec((blk, QUARTER), lambda i, be, nu: (i, 0))
    per_expert = lambda a: pl.BlockSpec((1,) + a.shape[1:], lambda i, be, nu: (be[i], 0, 0))
    return pl.pallas_call(
        _ffn_kernel,
        grid_spec=pltpu.PrefetchScalarGridSpec(
            num_scalar_prefetch=2, grid=(rows // blk,),
            in_specs=[row, row, per_expert(wup), per_expert(bup), per_expert(wdn), per_expert(bdn)],
            out_specs=[row, row]),
        out_shape=[jax.ShapeDtypeStruct((rows, QUARTER), I32), jax.ShapeDtypeStruct((rows, QUARTER), I32)],
        compiler_params=pltpu.CompilerParams(dimension_semantics=("arbitrary",), vmem_limit_bytes=40 * MIB),
        name="moe_ffn",
    )(block_expert, n_used, xs_lo, xs_hi, wup, bup, wdn, bdn)


def _combine_kernel(h1_ref, olo_ref, ohi_ref, gtm_ref, g2_ref, b2_ref, o_ref):
    g = gtm_ref[...]
    y = jnp.zeros(h1_ref.shape, F32)
    for k in range(MOE_TOP_K):
        y = y + g[:, k:k + 1] * _unpack_rows(olo_ref[k], ohi_ref[k])
    o_ref[...] = _layer_norm(DEEPNORM_ALPHA * h1_ref[...] + y, g2_ref[...], b2_ref[...])


def _combine(h1, og_lo, og_hi, gtm, g2, b2, row0):
    n = h1.shape[0]
    n_all = og_lo.shape[0] // MOE_TOP_K
    tm = MERGE_TM
    row = lambda w: pl.BlockSpec((tm, w), lambda i: (i, 0))
    const = lambda a: pl.BlockSpec(a.shape, lambda i: (0,) * a.ndim)
    picked = pl.BlockSpec((MOE_TOP_K, tm, QUARTER), lambda i: (0, i + row0 // tm, 0))
    return pl.pallas_call(
        _combine_kernel,
        grid=(n // tm,),
        in_specs=[row(D_MODEL), picked, picked, row(LANES), const(g2), const(b2)],
        out_specs=row(D_MODEL),
        out_shape=jax.ShapeDtypeStruct((n, D_MODEL), F32),
        compiler_params=pltpu.CompilerParams(dimension_semantics=("parallel",), vmem_limit_bytes=40 * MIB),
        name="moe_combine",
    )(h1, og_lo.reshape(MOE_TOP_K, n_all, QUARTER), og_hi.reshape(MOE_TOP_K, n_all, QUARTER), gtm, g2, b2)


def _moe(groups, totals, wup, bup, wdn, bdn, g2, b2):
    blk = FFN_BLOCK
    sizes = [g[0].shape[0] for g in groups]
    n_all = sum(sizes)
    n_rows = MOE_TOP_K * n_all + N_EXPERTS * blk
    counts = totals[:, 0].astype(I32)
    padded = (counts + blk - 1) // blk * blk
    ends = jnp.cumsum(padded)
    block_start = jnp.arange(n_rows // blk, dtype=I32) * blk
    block_expert = jnp.minimum(jnp.sum((ends[None, :] <= block_start[:, None]).astype(I32), axis=1), N_EXPERTS - 1)
    n_used = (ends[-1:] // blk).astype(I32)
    cat = lambda i, axis: jnp.concatenate([g[i] for g in groups], axis=axis)
    pos = _positions((ends - padded).astype(I32), cat(4, 1), cat(5, 1))
    pos_flat = pos[:MOE_TOP_K].reshape(1, MOE_TOP_K * n_all)
    xs_lo = _sc_scatter(cat(1, 0), pos_flat, n_rows)
    xs_hi = _sc_scatter(cat(2, 0), pos_flat, n_rows)
    ys_lo, ys_hi = _ffn(block_expert, n_used, xs_lo, xs_hi, wup, bup, wdn, bdn)
    og_lo, og_hi = _sc_gather(ys_lo, pos_flat), _sc_gather(ys_hi, pos_flat)
    starts = np.cumsum([0] + sizes[:-1]).tolist()
    return [_combine(g[0], og_lo, og_hi, g[3], g2, b2, r0) for g, r0 in zip(groups, starts)]


def _mixer(x, pos_tab, tab_period, s0, rows_per_stream, ret_rows, caches, weights, cnt0):
    wp, wr, wd, wo, g1, b1, rwt, rb = weights
    n = x.shape[0]
    n_streams = n // rows_per_stream
    rq, rk, rv, rg, aq, ak, av, iq, ikw, gr, ga = _project(x, wp, pos_tab, tab_period)
    ret, s_new = _retention(rq, rk, rv, rg, s0, rows_per_stream, ret_rows)
    if caches is None:
        per_stream = lambda a: a.reshape(n_streams, rows_per_stream, a.shape[1])
        od = _dsa(aq, iq, ikw, per_stream(ak), per_stream(av), per_stream(ikw), DSA_TQ)
    else:
        od = _dsa_cached(aq, iq, ikw, *caches, ak, av, rows_per_stream)
    h1, h_lo, h_hi, gtm, ek, rk_, totals = _merge(x, ret, od, gr, ga, wr, wd, wo, g1, b1, rwt, rb, cnt0)
    return (h1, h_lo, h_hi, gtm, ek, rk_), totals, (s_new, ak, av, ikw[:, :IDX_DIM])


def kernel(x_prompt, x_sample, state_ret, cache_k, cache_v, cache_idx_k, w_in, w_ret_o, w_dsa_o, w_o,
           ln1_g, ln1_b, router_w, router_b, w_up, b_up, w_down, b_down, ln2_g, ln2_b):
    assert w_in.shape[0] == DEPTH
    batch, seq, _ = x_prompt.shape
    dec_batch, dec_seq, _ = x_sample.shape
    past = cache_k.shape[2]
    assert seq % PROJ_TM == 0 and seq % DSA_TQ == 0 and PROJ_TM % dec_seq == 0

    l = 0
    mixer_w = (_pack_w_in(w_in[l]), w_ret_o[l].astype(BF16), w_dsa_o[l].astype(BF16), w_o[l].astype(BF16),
               ln1_g[l][None, :], ln1_b[l][None, :], router_w[l].T, router_b[l][:, None])
    moe_w = (
        _deinterleave_w_up(w_up[l]),
        b_up[l].reshape(N_EXPERTS, 2 * D_FF // UP_BLOCK, LANES, 2).transpose(0, 1, 3, 2).reshape(N_EXPERTS, 1, 2 * D_FF),
        w_down[l].astype(BF16), b_down[l][:, None, :], ln2_g[l][None, :], ln2_b[l][None, :])

    tab_p = _rot_tables(jnp.arange(seq))
    zeros_state = jnp.zeros((batch, RET_HEADS, RET_DK, RET_DV), F32)
    moe_p, totals_p, (s_p, k_p, v_p, ik_p) = _mixer(
        x_prompt.reshape(batch * seq, D_MODEL), tab_p, seq // PROJ_TM, zeros_state, seq, RET_CHUNK, None, mixer_w,
        jnp.zeros((N_EXPERTS, LANES), F32))

    tab_s = jnp.tile(_rot_tables(past + jnp.arange(dec_seq)), (PROJ_TM // dec_seq, 1))
    caches = (cache_k[l].reshape(dec_batch, past, LANES), cache_v[l].reshape(dec_batch, past, LANES), cache_idx_k[l])
    moe_s, totals, (s_s, k_s, v_s, ik_s) = _mixer(
        x_sample.reshape(dec_batch * dec_seq, D_MODEL), tab_s, 1, state_ret[l], dec_seq, dec_seq, caches, mixer_w,
        totals_p)

    y_p, y_s = _moe([moe_p, moe_s], totals, *moe_w)

    kv = (DSA_KV_HEADS, DSA_HEAD_DIM)
    return (y_p.reshape(batch, seq, D_MODEL), y_s.reshape(dec_batch, dec_seq, D_MODEL),
            s_p[None], k_p.reshape(1, batch, seq, *kv), v_p.reshape(1, batch, seq, *kv),
            ik_p.reshape(1, batch, seq, IDX_DIM),
            s_s[None], k_s.reshape(1, dec_batch, dec_seq, *kv), v_s.reshape(1, dec_batch, dec_seq, *kv),
            ik_s.reshape(1, dec_batch, dec_seq, IDX_DIM))
```

```python
import functools

import numpy as np
import jax
import jax.numpy as jnp
from jax import lax
from jax.experimental import pallas as pl
from jax.experimental.pallas import tpu as pltpu
from jax.experimental.pallas import tpu_sc as plsc

F32 = jnp.float32
BF16 = jnp.bfloat16
I32 = jnp.int32

D_MODEL = 1024
CHUNK = 64
RET_HEADS = 4
RET_DK = 128
RET_DV = 256
RET_ROPE_BASE = 10000.0
DSA_HEADS = 8
DSA_KV_HEADS = 2
DSA_HEAD_DIM = 64
IDX_HEADS = 8
IDX_DIM = 64
DSA_TOPK = 256
ROPE_THETA = 500000.0
N_EXPERTS = 32
MOE_TOP_K = 4
D_FF = 1024
SWIGLU_ALPHA = 1.702
SWIGLU_LIMIT = 7.0
LN_EPS = 1e-5
GN_EPS = 1e-6
DEPTH = 1
DEEPNORM_ALPHA = (2.0 * DEPTH) ** 0.25
PROJ_WIDTHS = (RET_HEADS * RET_DK, RET_HEADS * RET_DK, RET_HEADS * RET_DV, RET_HEADS * RET_DV,
               DSA_HEADS * DSA_HEAD_DIM, DSA_KV_HEADS * DSA_HEAD_DIM, DSA_KV_HEADS * DSA_HEAD_DIM,
               IDX_HEADS * IDX_DIM, IDX_DIM, IDX_HEADS, D_MODEL, D_MODEL)

LANES = 128
MIB = 1024 * 1024

OFF_RQ, OFF_RK, OFF_RV, OFF_RG = 0, 512, 1024, 2048
OFF_AQ, OFF_AK, OFF_AV, OFF_IQ, OFF_IKW = 3072, 3584, 3712, 3840, 4352
OFF_GR, OFF_GA, PACKED_COLS = 4480, 5504, 6528
TAB_COLS = 8 * LANES

PROJ_TM = 512
RET_CHUNK = 256
RET_CHUNKS_PER_STEP = 4
DSA_TQ = 256
DSA_ROWS = 512
DSA_GROUP_MAX = 4
SEARCH_UNROLL = 4
MERGE_TM = 512
FFN_BLOCK = 512
SC_WINDOW = 128
QUARTER = D_MODEL // 4
UP_BLOCK = 2 * LANES


def _nt(a, b):
    return lax.dot_general(a, b, (((1,), (1,)), ((), ())), preferred_element_type=F32)


def _mm(a, b):
    return jnp.dot(a, b, preferred_element_type=F32)


def _proj_kernel(x_ref, w_ref, tab_ref, rq_ref, rk_ref, rv_ref, rg_ref, aq_ref, ak_ref, av_ref,
                 iq_ref, ikw_ref, gr_ref, ga_ref):
    xb = x_ref[...].astype(BF16)

    def mm(c0, n):
        return _mm(xb, w_ref[:, c0:c0 + n])

    def tab(i):
        return tab_ref[:, i * LANES:(i + 1) * LANES]

    def rot_full(z):
        return z * tab(0) + pltpu.roll(z, 64, 1) * tab(1)

    def rot_part(z, c):
        return z * tab(c) + pltpu.roll(z, LANES - 8, 1) * tab(c + 1) + pltpu.roll(z, 8, 1) * tab(c + 2)

    z = mm(OFF_RQ, 512)
    for h in range(4):
        sl = slice(h * LANES, (h + 1) * LANES)
        rq_ref[:, sl] = rot_full(z[:, sl]).astype(BF16)
    z = mm(OFF_RK, 512)
    for h in range(4):
        sl = slice(h * LANES, (h + 1) * LANES)
        rk_ref[:, sl] = (rot_full(z[:, sl]) * (RET_DK ** -0.5)).astype(BF16)
    for c in range(2):
        rv_ref[:, c * 512:(c + 1) * 512] = mm(OFF_RV + c * 512, 512).astype(BF16)
    for c in range(2):
        rg_ref[:, c * 512:(c + 1) * 512] = mm(OFF_RG + c * 512, 512)
    z = mm(OFF_AQ, 512)
    for h in range(4):
        sl = slice(h * LANES, (h + 1) * LANES)
        aq_ref[:, sl] = rot_part(z[:, sl], 2).astype(BF16)
    z = mm(OFF_AK, 256)
    ak_ref[...] = rot_part(z[:, :LANES], 2)
    av_ref[...] = z[:, LANES:]
    z = mm(OFF_IQ, 512)
    for h in range(4):
        sl = slice(h * LANES, (h + 1) * LANES)
        iq_ref[:, sl] = rot_part(z[:, sl], 2).astype(BF16)
    ikw_ref[...] = rot_part(mm(OFF_IKW, LANES), 5)
    for c in range(2):
        gr_ref[:, c * 512:(c + 1) * 512] = mm(OFF_GR + c * 512, 512)
    for c in range(2):
        ga_ref[:, c * 512:(c + 1) * 512] = mm(OFF_GA + c * 512, 512)


def _rot_tables(pos):
    p = pos.shape[0]
    posf = pos.astype(F32)[:, None]
    ret_f = RET_ROPE_BASE ** (-jnp.linspace(0.0, 1.0, RET_DK // 2, dtype=F32))
    ang = posf * ret_f[None, :]
    c, s = jnp.cos(ang), jnp.sin(ang)
    cos_r = jnp.concatenate([c, c], 1)
    sin_r = jnp.concatenate([-s, s], 1)
    n_rot = DSA_HEAD_DIM // 4
    att_f = ROPE_THETA ** (-jnp.arange(0, n_rot, 2, dtype=F32) / n_rot)
    ang2 = posf * att_f[None, :]
    c2, s2 = jnp.cos(ang2), jnp.sin(ang2)
    half = n_rot // 2
    rest = DSA_HEAD_DIM - 2 * half
    c64 = jnp.concatenate([c2, c2, jnp.ones((p, rest), F32)], 1)
    s1_64 = jnp.concatenate([-s2, jnp.zeros((p, DSA_HEAD_DIM - half), F32)], 1)
    s2_64 = jnp.concatenate([jnp.zeros((p, half), F32), s2, jnp.zeros((p, rest), F32)], 1)
    z64 = jnp.zeros((p, DSA_HEAD_DIM), F32)
    ci = jnp.concatenate([c64, jnp.full((p, IDX_HEADS), IDX_HEADS ** -0.5, F32),
                          jnp.zeros((p, DSA_HEAD_DIM - IDX_HEADS), F32)], 1)
    return jnp.concatenate([cos_r, sin_r,
                            jnp.concatenate([c64, c64], 1), jnp.concatenate([s1_64, s1_64], 1),
                            jnp.concatenate([s2_64, s2_64], 1),
                            ci, jnp.concatenate([s1_64, z64], 1), jnp.concatenate([s2_64, z64], 1)], 1)


def _pack_w_in(w):
    cuts = np.cumsum(PROJ_WIDTHS)[:-1].tolist()
    rq, rk, rv, rg, aq, ak, av, iq, ik, iw, gr, ga = jnp.split(w, cuts, axis=1)
    pad = jnp.zeros((w.shape[0], LANES - IDX_DIM - IDX_HEADS), w.dtype)
    return jnp.concatenate([rq, rk, rv, rg, aq, ak, av, iq, ik, iw, pad, gr, ga], axis=1).astype(BF16)


def _project(x, wp, tab, tab_period):
    n = x.shape[0]
    tm = PROJ_TM
    row = lambda w: pl.BlockSpec((tm, w), lambda i: (i, 0))
    out_shapes = [((n, 512), BF16), ((n, 512), BF16), ((n, 1024), BF16), ((n, 1024), F32),
                  ((n, 512), BF16), ((n, LANES), F32), ((n, LANES), F32), ((n, 512), BF16),
                  ((n, LANES), F32), ((n, 1024), F32), ((n, 1024), F32)]
    return pl.pallas_call(
        _proj_kernel,
        grid=(n // tm,),
        in_specs=[row(D_MODEL),
                  pl.BlockSpec((D_MODEL, PACKED_COLS), lambda i: (0, 0), pipeline_mode=pl.Buffered(1)),
                  pl.BlockSpec((tm, TAB_COLS), lambda i: (i % tab_period, 0))],
        out_specs=[row(s[1]) for s, _ in out_shapes],
        out_shape=[jax.ShapeDtypeStruct(s, d) for s, d in out_shapes],
        compiler_params=pltpu.CompilerParams(dimension_semantics=("parallel",), vmem_limit_bytes=52 * MIB),
        name="proj",
    )(x, wp, tab)


def _ret_kernel(dec_ref, xi_ref, zeta_ref, rq_ref, rk_ref, rv_ref, rg_ref, s0_ref, ret_ref, sout_ref, st_ref,
                *, rows, n_chunk, g_pow):
    j = pl.program_id(1)
    cpad = dec_ref.shape[1]

    @pl.when(j == 0)
    def _():
        st_ref[...] = s0_ref[0]

    def padded(v):
        if rows == cpad:
            return v
        return jnp.concatenate([v, jnp.zeros((cpad - rows, v.shape[1]), v.dtype)], axis=0)

    for c in range(n_chunk):
        rs = slice(c * rows, (c + 1) * rows)
        for h in range(RET_HEADS):
            ks = slice(h * RET_DK, (h + 1) * RET_DK)
            vs = slice(h * RET_DV, (h + 1) * RET_DV)
            q = padded(rq_ref[rs, ks])
            kt = padded(rk_ref[rs, ks].astype(F32)).T
            v = padded(rv_ref[rs, vs])
            s = st_ref[h]
            sc = _mm(q, kt.astype(BF16)) * dec_ref[h]
            o = _mm(sc.astype(BF16), v) + _mm(q, s.astype(BF16)) * xi_ref[h]
            st_ref[h] = g_pow[h] * s + _mm((kt * zeta_ref[h]).astype(BF16), v)
            o = o[:rows]
            mu = jnp.mean(o, axis=-1, keepdims=True)
            d = o - mu
            var = jnp.mean(d * d, axis=-1, keepdims=True)
            gn = d * lax.rsqrt(var + GN_EPS)
            g = rg_ref[rs, vs]
            ret_ref[rs, vs] = (gn * (g * jax.nn.sigmoid(g))).astype(BF16)

    @pl.when(j == pl.num_programs(1) - 1)
    def _():
        sout_ref[0] = st_ref[...]


def _retention(rq, rk, rv, rg, s0, rows_per_stream, rows):
    n = rq.shape[0]
    n_streams = n // rows_per_stream
    n_chunk = min(rows_per_stream // rows, RET_CHUNKS_PER_STEP)
    blk = rows * n_chunk
    nb = rows_per_stream // blk
    gam = 1.0 - 2.0 ** (-5.0 - np.arange(RET_HEADS, dtype=np.float64))
    cpad = max(rows, LANES)
    i = np.arange(cpad, dtype=np.float64)
    diff = i[:, None] - i[None, :]
    dec = np.where(diff >= 0, gam[:, None, None] ** np.maximum(diff, 0.0)[None], 0.0)
    xi = gam[:, None, None] ** (i + 1.0)[None, :, None]
    zeta = np.where(i < rows, gam[:, None, None] ** (rows - 1.0 - i)[None, None, :], 0.0)
    g_pow = tuple(float(g ** rows) for g in gam)
    const = lambda shape: pl.BlockSpec(shape, lambda s, j: (0,) * len(shape))
    row = lambda w: pl.BlockSpec((blk, w), lambda s, j: (s * nb + j, 0))
    st = pl.BlockSpec((1, RET_HEADS, RET_DK, RET_DV), lambda s, j: (s, 0, 0, 0))
    return pl.pallas_call(
        functools.partial(_ret_kernel, rows=rows, n_chunk=n_chunk, g_pow=g_pow),
        grid=(n_streams, nb),
        in_specs=[const((RET_HEADS, cpad, cpad)), const((RET_HEADS, cpad, 1)),
                  const((RET_HEADS, 1, cpad)), row(512), row(512), row(1024), row(1024), st],
        out_specs=[row(1024), st],
        out_shape=[jax.ShapeDtypeStruct((n, RET_HEADS * RET_DV), BF16),
                   jax.ShapeDtypeStruct((n_streams, RET_HEADS, RET_DK, RET_DV), F32)],
        scratch_shapes=[pltpu.VMEM((RET_HEADS, RET_DK, RET_DV), F32)],
        compiler_params=pltpu.CompilerParams(dimension_semantics=("parallel", "arbitrary"),
                                             vmem_limit_bytes=32 * MIB),
        name="retention",
    )(jnp.asarray(dec, F32), jnp.asarray(xi, F32), jnp.asarray(zeta, F32), rq, rk, rv, rg, s0)


def _dsa_kernel(aq_ref, iq_ref, ikwq_ref, kk_ref, vv_ref, ikk_ref, *rest,
                group, tq, n_keys, row0, limit_const, n_sel):
    o_ref, w_ref, bias_ref = rest[-3:]
    rows = group * tq
    nsel_f = float(n_sel)
    neg_inf = -jnp.inf

    col = lax.broadcasted_iota(I32, (tq, n_keys), 1)
    if limit_const is None:
        limit = (jnp.right_shift(lax.broadcasted_iota(I32, (tq, 1), 0) + row0, 6) + 1) * CHUNK
    else:
        limit = limit_const

    def score_stream(s, carry):
        ikb = ikk_ref[s][:, :IDX_DIM].astype(BF16)
        iww = ikwq_ref[s][:, IDX_DIM:IDX_DIM + IDX_HEADS] * (IDX_DIM ** -0.5)
        acc = jnp.zeros((tq, n_keys), F32)
        for p in range(IDX_HEADS // 2):
            slab = iq_ref[s, :, p * LANES:(p + 1) * LANES]
            for hh in range(2):
                h = 2 * p + hh
                acc = acc + jnp.maximum(_nt(slab[:, hh * IDX_DIM:(hh + 1) * IDX_DIM], ikb), 0.0) * iww[:, h:h + 1]
        w_ref[pl.ds(pl.multiple_of(s * tq, tq), tq), :] = jnp.where(col < limit, acc, neg_inf)
        return carry

    lax.fori_loop(0, group, score_stream, 0)

    sc = w_ref[...]
    pos = jnp.sum(jnp.where(sc >= 0.0, 1.0, 0.0), axis=1, keepdims=True) >= nsel_f
    kk = jnp.where(pos, nsel_f, float(n_keys - n_sel + 1))
    w_ref[...] = jnp.where(pos, sc, -sc)

    def bit_step(i, u):
        cand_u = u | jnp.left_shift(jnp.int32(1), 30 - i)
        cand = pltpu.bitcast(cand_u, F32)
        cnt = jnp.sum(jnp.where(w_ref[...] >= cand, 1.0, 0.0), axis=1, keepdims=True)
        return jnp.where(cnt >= kk, cand_u, u)

    mag_u = lax.fori_loop(0, 31, bit_step, jnp.zeros((rows, 1), I32), unroll=SEARCH_UNROLL)
    mag = pltpu.bitcast(mag_u, F32)
    thr = jnp.where(pos, mag, -mag)

    sc = jnp.where(pos, w_ref[...], -w_ref[...])
    short = jnp.sum(jnp.where(sc >= thr, 1.0, 0.0), axis=1, keepdims=True) < nsel_f
    thr = jnp.where(jnp.logical_and(short, jnp.logical_not(pos)), -pltpu.bitcast(mag_u + 1, F32), thr)
    ge = sc >= thr
    cnt_gt = jnp.sum(jnp.where(sc > thr, 1.0, 0.0), axis=1, keepdims=True)
    cnt_ge = jnp.sum(jnp.where(ge, 1.0, 0.0), axis=1, keepdims=True)
    bias_ref[...] = jnp.where(jnp.logical_and(ge, sc > neg_inf), 0.0, neg_inf)
    excess = jnp.logical_and(cnt_ge > nsel_f, thr > neg_inf)

    @pl.when(jnp.max(jnp.where(excess, 1.0, 0.0)) > 0.0)
    def _():
        need = nsel_f - cnt_gt
        tri = jnp.where(lax.broadcasted_iota(I32, (LANES, LANES), 0) < lax.broadcasted_iota(I32, (LANES, LANES), 1),
                        1.0, 0.0).astype(BF16)
        before = jnp.zeros((rows, 1), F32)
        for b in range(n_keys // LANES):
            sl = slice(b * LANES, (b + 1) * LANES)
            sblk = jnp.where(pos, w_ref[:, sl], -w_ref[:, sl])
            eq = jnp.where(sblk == thr, 1.0, 0.0)
            rank = _mm(eq.astype(BF16), tri) + before
            keep = jnp.logical_or(sblk > thr, jnp.logical_and(sblk == thr, rank < need))
            bias_ref[:, sl] = jnp.where(jnp.logical_and(keep, sblk > neg_inf), 0.0, neg_inf)
            before = before + jnp.sum(eq, axis=1, keepdims=True)

    def attend_stream(s, carry):
        bias = bias_ref[pl.ds(pl.multiple_of(s * tq, tq), tq), :]
        kfull = kk_ref[s]
        vfull = vv_ref[s]
        for g in range(DSA_KV_HEADS):
            gs = slice(g * DSA_HEAD_DIM, (g + 1) * DSA_HEAD_DIM)
            kg = kfull[:, gs].astype(BF16)
            vg = vfull[:, gs].astype(BF16)
            for pp in range(2):
                p = 2 * g + pp
                slab = (aq_ref[s, :, p * LANES:(p + 1) * LANES].astype(F32) * (DSA_HEAD_DIM ** -0.5)).astype(BF16)
                outs = []
                for hh in range(2):
                    lg = _nt(slab[:, hh * DSA_HEAD_DIM:(hh + 1) * DSA_HEAD_DIM], kg) + bias
                    m = jnp.max(lg, axis=1, keepdims=True)
                    pr = jnp.exp(lg - m)
                    den = jnp.sum(pr, axis=1, keepdims=True)
                    outs.append(_mm(pr.astype(BF16), vg) / den)
                o_ref[s, :, p * LANES:(p + 1) * LANES] = jnp.concatenate(outs, axis=1).astype(BF16)
        return carry

    lax.fori_loop(0, group, attend_stream, 0)


def _dsa_call(aq, iq, ikw, kk, vv, ikk, group, tq, jq, n_keys, limit_const, n_sel, name, prev=None):
    n_streams, rows_per_stream = aq.shape[0], aq.shape[1]
    extra_specs, extra_args, aliases = [], (), {}
    if prev is not None:
        extra_specs, extra_args, aliases = [pl.BlockSpec(memory_space=pl.ANY)], (prev,), {6: 0}
    qrow = lambda a: pl.BlockSpec((group, tq, a.shape[2]), lambda s: (s, jq, 0))
    krow = lambda a: pl.BlockSpec((group, n_keys, a.shape[2]), lambda s: (s, 0, 0))
    width = DSA_HEADS * DSA_HEAD_DIM
    return pl.pallas_call(
        functools.partial(_dsa_kernel, group=group, tq=tq, n_keys=n_keys, row0=jq * tq, limit_const=limit_const,
                          n_sel=n_sel),
        grid=(n_streams // group,),
        in_specs=[qrow(aq), qrow(iq), qrow(ikw), krow(kk), krow(vv), krow(ikk)] + extra_specs,
        out_specs=pl.BlockSpec((group, tq, width), lambda s: (s, jq, 0)),
        out_shape=jax.ShapeDtypeStruct((n_streams, rows_per_stream, width), BF16),
        input_output_aliases=aliases,
        scratch_shapes=[pltpu.VMEM((group * tq, n_keys), F32), pltpu.VMEM((group * tq, n_keys), F32)],
        compiler_params=pltpu.CompilerParams(dimension_semantics=("parallel",), vmem_limit_bytes=56 * MIB),
        name=name,
    )(aq, iq, ikw, kk, vv, ikk, *extra_args)


def _dsa_cached_kernel(aq_ref, iq_ref, ikwq_ref, ck_ref, cv_ref, cik_ref, nk_ref, nv_ref, o_ref,
                       w_ref, bias_ref, kk_ref, vv_ref, ikk_ref, **static):
    past, new = ck_ref.shape[1], nk_ref.shape[1]
    for dst, cache, fresh in ((kk_ref, ck_ref, nk_ref), (vv_ref, cv_ref, nv_ref), (ikk_ref, cik_ref, ikwq_ref)):
        dst[:, :past, :cache.shape[2]] = cache[...]
        dst[:, past:past + new, :] = fresh[...]
        dst[:, past + new:, :] = jnp.zeros((dst.shape[0], dst.shape[1] - past - new, dst.shape[2]), F32)
    _dsa_kernel(aq_ref, iq_ref, ikwq_ref, kk_ref, vv_ref, ikk_ref, o_ref, w_ref, bias_ref, **static)


def _dsa_cached(aq, iq, ikw, cache_k, cache_v, cache_ik, new_k, new_v, tq):
    n_streams, past, _ = cache_k.shape
    n = aq.shape[0]
    per_stream = lambda a: a.reshape(n_streams, tq, a.shape[1])
    aq, iq, ikw, new_k, new_v = per_stream(aq), per_stream(iq), per_stream(ikw), per_stream(new_k), per_stream(new_v)
    limit = past + tq
    n_keys = -(-limit // LANES) * LANES
    group = max(1, min(n_streams, DSA_ROWS // tq, DSA_GROUP_MAX))
    blk = lambda a: pl.BlockSpec((group,) + a.shape[1:], lambda s: (s, 0, 0))
    width = DSA_HEADS * DSA_HEAD_DIM
    return pl.pallas_call(
        functools.partial(_dsa_cached_kernel, group=group, tq=tq, n_keys=n_keys, row0=0, limit_const=limit,
                          n_sel=min(DSA_TOPK, limit // 4)),
        grid=(n_streams // group,),
        in_specs=[blk(a) for a in (aq, iq, ikw, cache_k, cache_v, cache_ik, new_k, new_v)],
        out_specs=pl.BlockSpec((group, tq, width), lambda s: (s, 0, 0)),
        out_shape=jax.ShapeDtypeStruct((n_streams, tq, width), BF16),
        scratch_shapes=[pltpu.VMEM((group * tq, n_keys), F32), pltpu.VMEM((group * tq, n_keys), F32)] +
                       [pltpu.VMEM((group, n_keys, LANES), F32)] * 3,
        compiler_params=pltpu.CompilerParams(dimension_semantics=("parallel",), vmem_limit_bytes=48 * MIB),
        name="dsa_s",
    )(aq, iq, ikw, cache_k, cache_v, cache_ik, new_k, new_v).reshape(n, width)


def _dsa(aq, iq, ikw, kk, vv, ikk, tq):
    n_streams, n_keys, _ = kk.shape
    n = aq.shape[0]
    per_stream = lambda a: a.reshape(n_streams, n // n_streams, a.shape[1])
    aq, iq, ikw = per_stream(aq), per_stream(iq), per_stream(ikw)
    nq = n // n_streams // tq
    n_sel = min(DSA_TOPK, n_keys // 4)
    group = max(1, min(n_streams, DSA_ROWS // tq, DSA_GROUP_MAX))
    out = None
    for jq in range(nq):
        out = _dsa_call(aq, iq, ikw, kk, vv, ikk, group, tq, jq, (jq + 1) * tq, None, n_sel, f"dsa_p{jq}", prev=out)
    return out.reshape(n, -1)


def _pack_rows(v):
    q = QUARTER
    bits = lambda x: pltpu.bitcast(x.astype(BF16).astype(F32), I32)
    pair = lambda c: lax.shift_right_logical(bits(v[:, c * q:(c + 1) * q]), 16) | bits(v[:, (c + 1) * q:(c + 2) * q])
    return pair(0), pair(2)


def _unpack_rows(lo, hi):
    parts = []
    for w in (lo, hi):
        parts.append(pltpu.bitcast(lax.shift_left(w, 16), F32))
        parts.append(pltpu.bitcast(w & jnp.int32(-65536), F32))
    return jnp.concatenate(parts, axis=1)


def _layer_norm(v, g, b):
    mu = jnp.mean(v, axis=-1, keepdims=True)
    d = v - mu
    var = jnp.mean(d * d, axis=-1, keepdims=True)
    return d * lax.rsqrt(var + LN_EPS) * g + b


def _merge_kernel(x_ref, ret_ref, od_ref, gr_ref, ga_ref, wr_ref, wd_ref, wo_ref, g1_ref, b1_ref, rwt_ref, rb_ref,
                  tri_ref, cnt0_ref, h1_ref, hlo_ref, hhi_ref, gtm_ref, ek_ref, rk_ref, tot_ref, cnt_ref):
    @pl.when(pl.program_id(0) == 0)
    def _():
        cnt_ref[...] = cnt0_ref[:, 0:1]

    y_ret = _mm(ret_ref[...], wr_ref[...])
    y_dsa = _mm(od_ref[...], wd_ref[...])
    merged = jax.nn.sigmoid(gr_ref[...]) * y_ret + jax.nn.sigmoid(ga_ref[...]) * y_dsa
    mix = _mm(merged.astype(BF16), wo_ref[...])
    h1 = _layer_norm(DEEPNORM_ALPHA * x_ref[...] + mix, g1_ref[...], b1_ref[...])
    h1_ref[...] = h1
    hlo_ref[...], hhi_ref[...] = _pack_rows(h1)

    logits = lax.dot_general(rwt_ref[...], h1, (((1,), (1,)), ((), ())), preferred_element_type=F32,
                             precision=lax.Precision.HIGHEST) + rb_ref[...]
    tm = logits.shape[1]
    e_iota = lax.broadcasted_iota(I32, (N_EXPERTS, tm), 0)
    tops, hots, firsts = [], [], []
    for _ in range(MOE_TOP_K):
        m = jnp.max(logits, axis=0, keepdims=True)
        first = jnp.min(jnp.where(logits == m, e_iota, N_EXPERTS), axis=0, keepdims=True)
        hot = e_iota == first
        tops.append(m)
        hots.append(hot)
        firsts.append(first)
        logits = jnp.where(hot, -jnp.inf, logits)
    exps = [jnp.exp(m - tops[0]) for m in tops]
    den = exps[0] + exps[1] + exps[2] + exps[3]
    sel = jnp.zeros((N_EXPERTS, tm), F32)
    for hot in hots:
        sel = sel + jnp.where(hot, 1.0, 0.0)
    rank = _mm(sel.astype(BF16), tri_ref[...]) + cnt_ref[...]
    ranks = [jnp.sum(jnp.where(hot, rank, 0.0), axis=0, keepdims=True).astype(I32) for hot in hots]
    pad_i = jnp.zeros((8 - MOE_TOP_K, tm), I32)
    ek_ref[...] = jnp.concatenate(firsts + [pad_i], axis=0)
    rk_ref[...] = jnp.concatenate(ranks + [pad_i], axis=0)
    gates = jnp.concatenate([e / den for e in exps] + [jnp.zeros((LANES - MOE_TOP_K, tm), F32)], axis=0)
    gtm_ref[...] = gates.T
    cnt_ref[...] = cnt_ref[...] + jnp.sum(sel, axis=1, keepdims=True)
    tot_ref[...] = jnp.broadcast_to(cnt_ref[...], tot_ref.shape)


def _merge(x, ret, od, gr, ga, wr, wd, wo, g1, b1, rwt, rb, cnt0):
    n = x.shape[0]
    tm = MERGE_TM
    row = lambda w: pl.BlockSpec((tm, w), lambda i: (i, 0))
    const = lambda a: pl.BlockSpec(a.shape, lambda i: (0,) * a.ndim)
    col = pl.BlockSpec((8, tm), lambda i: (0, i))
    tri = jnp.asarray(np.triu(np.ones((tm, tm), np.float32), 1), BF16)
    return pl.pallas_call(
        _merge_kernel,
        grid=(n // tm,),
        in_specs=[row(D_MODEL), row(1024), row(512), row(1024), row(1024), const(wr), const(wd), const(wo),
                  const(g1), const(b1), const(rwt), const(rb), const(tri), const(cnt0)],
        out_specs=[row(D_MODEL), row(QUARTER), row(QUARTER), row(LANES), col, col,
                   pl.BlockSpec((N_EXPERTS, LANES), lambda i: (0, 0))],
        out_shape=[jax.ShapeDtypeStruct((n, D_MODEL), F32), jax.ShapeDtypeStruct((n, QUARTER), I32),
                   jax.ShapeDtypeStruct((n, QUARTER), I32), jax.ShapeDtypeStruct((n, LANES), F32),
                   jax.ShapeDtypeStruct((8, n), I32), jax.ShapeDtypeStruct((8, n), I32),
                   jax.ShapeDtypeStruct((N_EXPERTS, LANES), F32)],
        scratch_shapes=[pltpu.VMEM((N_EXPERTS, 1), F32)],
        compiler_params=pltpu.CompilerParams(dimension_semantics=("arbitrary",), vmem_limit_bytes=48 * MIB),
        name="merge",
    )(x, ret, od, gr, ga, wr, wd, wo, g1, b1, rwt, rb, tri, cnt0)


def _deinterleave_kernel(w_ref, o_ref):
    r = lax.broadcasted_iota(I32, (UP_BLOCK, UP_BLOCK), 0)
    c = lax.broadcasted_iota(I32, (UP_BLOCK, UP_BLOCK), 1)
    src = jnp.where(c < LANES, 2 * c, 2 * (c - LANES) + 1)
    perm = jnp.where(r == src, 1.0, 0.0).astype(BF16)
    for b in range(w_ref.shape[2] // UP_BLOCK):
        sl = slice(b * UP_BLOCK, (b + 1) * UP_BLOCK)
        o_ref[0, :, sl] = _mm(w_ref[0, :, sl].astype(BF16), perm).astype(BF16)


def _deinterleave_w_up(w_up):
    n_e, d_in, d_out = w_up.shape
    cols = d_out
    spec = pl.BlockSpec((1, d_in, cols), lambda e, c: (e, 0, c))
    return pl.pallas_call(
        _deinterleave_kernel,
        grid=(n_e, d_out // cols),
        in_specs=[spec],
        out_specs=spec,
        out_shape=jax.ShapeDtypeStruct(w_up.shape, BF16),
        compiler_params=pltpu.CompilerParams(dimension_semantics=("parallel", "parallel"),
                                             vmem_limit_bytes=40 * MIB),
        name="w_up_prep",
    )(w_up)


def _pos_kernel(off_ref, ek_ref, rk_ref, pos_ref):
    ek = ek_ref[...]
    pos = rk_ref[...]
    for e in range(N_EXPERTS):
        pos = pos + jnp.where(ek == e, off_ref[e], 0)
    pos_ref[...] = pos


def _positions(off, ek, rk):
    n = ek.shape[1]
    tn = min(n, 2048)
    spec = pl.BlockSpec((8, tn), lambda i, off: (0, i))
    return pl.pallas_call(
        _pos_kernel,
        grid_spec=pltpu.PrefetchScalarGridSpec(num_scalar_prefetch=1, grid=(n // tn,), in_specs=[spec, spec],
                                               out_specs=spec),
        out_shape=jax.ShapeDtypeStruct((8, n), I32),
        name="moe_pos",
    )(off, ek, rk)


def _sc_mesh():
    return plsc.VectorSubcoreMesh(core_axis_name="core", subcore_axis_name="subcore")


def _sc_scatter(x, pos_flat, n_rows):
    n = x.shape[0]
    nw = n // SC_WINDOW

    @functools.partial(pl.kernel, out_type=jax.ShapeDtypeStruct((n_rows, x.shape[1]), x.dtype), mesh=_sc_mesh())
    def scatter(x_hbm, p_hbm, o_hbm):
        def body(x_vmem, *p_vmem):
            for p in p_vmem:
                pltpu.sync_copy(x_vmem, o_hbm.at[p.at[0]])

        pltpu.emit_pipeline(
            body, grid=(nw,),
            in_specs=[pl.BlockSpec((SC_WINDOW, x.shape[1]), lambda i: (i, 0))] +
                     [pl.BlockSpec((1, SC_WINDOW), functools.partial(lambda k, i: (0, k * nw + i), k))
                      for k in range(MOE_TOP_K)],
            out_specs=[], core_axis_name=("core", "subcore"), dimension_semantics=(pltpu.PARALLEL,),
        )(x_hbm, *([p_hbm] * MOE_TOP_K))

    return scatter(x, pos_flat)


def _sc_gather(y, pos_flat):
    m = pos_flat.shape[1]

    @functools.partial(pl.kernel, out_type=jax.ShapeDtypeStruct((m, y.shape[1]), y.dtype), mesh=_sc_mesh())
    def gather(y_hbm, p_hbm, o_hbm):
        def body(p_vmem, o_vmem):
            pltpu.sync_copy(y_hbm.at[p_vmem.at[0]], o_vmem)

        pltpu.emit_pipeline(
            body, grid=(m // SC_WINDOW,),
            in_specs=[pl.BlockSpec((1, SC_WINDOW), lambda i: (0, i))],
            out_specs=[pl.BlockSpec((SC_WINDOW, y.shape[1]), lambda i: (i, 0))],
            core_axis_name=("core", "subcore"), dimension_semantics=(pltpu.PARALLEL,),
        )(p_hbm, o_hbm)

    return gather(y, pos_flat)


def _ffn_kernel(be_ref, nu_ref, xlo_ref, xhi_ref, wup_ref, bup_ref, wdn_ref, bdn_ref, ylo_ref, yhi_ref):
    @pl.when(pl.program_id(0) < nu_ref[0])
    def _():
        x = _unpack_rows(xlo_ref[...], xhi_ref[...]).astype(BF16)
        h = _mm(x, wup_ref[0]) + bup_ref[0]
        acts = []
        for b in range(2 * D_FF // UP_BLOCK):
            glu = jnp.minimum(h[:, b * UP_BLOCK:b * UP_BLOCK + LANES], SWIGLU_LIMIT)
            lin = jnp.clip(h[:, b * UP_BLOCK + LANES:(b + 1) * UP_BLOCK], -SWIGLU_LIMIT, SWIGLU_LIMIT)
            acts.append(glu * jax.nn.sigmoid(SWIGLU_ALPHA * glu) * (lin + 1.0))
        act = jnp.concatenate(acts, axis=1)
        ylo_ref[...], yhi_ref[...] = _pack_rows(_mm(act.astype(BF16), wdn_ref[0]) + bdn_ref[0])


def _ffn(block_expert, n_used, xs_lo, xs_hi, wup, bup, wdn, bdn):
    rows = xs_lo.shape[0]
    blk = FFN_BLOCK
    row = pl.BlockSpec((blk, QUARTER), lambda i, be, nu: (i, 0))
    per_expert = lambda a: pl.BlockSpec((1,) + a.shape[1:], lambda i, be, nu: (be[i], 0, 0))
    return pl.pallas_call(
        _ffn_kernel,
        grid_spec=pltpu.PrefetchScalarGridSpec(
            num_scalar_prefetch=2, grid=(rows // blk,),
            in_specs=[row, row, per_expert(wup), per_expert(bup), per_expert(wdn), per_expert(bdn)],
            out_specs=[row, row]),
        out_shape=[jax.ShapeDtypeStruct((rows, QUARTER), I32), jax.ShapeDtypeStruct((rows, QUARTER), I32)],
        compiler_params=pltpu.CompilerParams(dimension_semantics=("arbitrary",), vmem_limit_bytes=40 * MIB),
        name="moe_ffn",
    )(block_expert, n_used, xs_lo, xs_hi, wup, bup, wdn, bdn)


def _combine_kernel(h1_ref, olo_ref, ohi_ref, gtm_ref, g2_ref, b2_ref, o_ref):
    g = gtm_ref[...]
    y = jnp.zeros(h1_ref.shape, F32)
    for k in range(MOE_TOP_K):
        y = y + g[:, k:k + 1] * _unpack_rows(olo_ref[k], ohi_ref[k])
    o_ref[...] = _layer_norm(DEEPNORM_ALPHA * h1_ref[...] + y, g2_ref[...], b2_ref[...])


def _combine(h1, og_lo, og_hi, gtm, g2, b2, row0):
    n = h1.shape[0]
    n_all = og_lo.shape[0] // MOE_TOP_K
    tm = MERGE_TM
    row = lambda w: pl.BlockSpec((tm, w), lambda i: (i, 0))
    const = lambda a: pl.BlockSpec(a.shape, lambda i: (0,) * a.ndim)
    picked = pl.BlockSpec((MOE_TOP_K, tm, QUARTER), lambda i: (0, i + row0 // tm, 0))
    return pl.pallas_call(
        _combine_kernel,
        grid=(n // tm,),
        in_specs=[row(D_MODEL), picked, picked, row(LANES), const(g2), const(b2)],
        out_specs=row(D_MODEL),
        out_shape=jax.ShapeDtypeStruct((n, D_MODEL), F32),
        compiler_params=pltpu.CompilerParams(dimension_semantics=("parallel",), vmem_limit_bytes=40 * MIB),
        name="moe_combine",
    )(h1, og_lo.reshape(MOE_TOP_K, n_all, QUARTER), og_hi.reshape(MOE_TOP_K, n_all, QUARTER), gtm, g2, b2)


def _moe(groups, totals, wup, bup, wdn, bdn, g2, b2):
    blk = FFN_BLOCK
    sizes = [g[0].shape[0] for g in groups]
    n_all = sum(sizes)
    n_rows = MOE_TOP_K * n_all + N_EXPERTS * blk
    counts = totals[:, 0].astype(I32)
    padded = (counts + blk - 1) // blk * blk
    ends = jnp.cumsum(padded)
    block_start = jnp.arange(n_rows // blk, dtype=I32) * blk
    block_expert = jnp.minimum(jnp.sum((ends[None, :] <= block_start[:, None]).astype(I32), axis=1), N_EXPERTS - 1)
    n_used = (ends[-1:] // blk).astype(I32)
    cat = lambda i, axis: jnp.concatenate([g[i] for g in groups], axis=axis)
    pos = _positions((ends - padded).astype(I32), cat(4, 1), cat(5, 1))
    pos_flat = pos[:MOE_TOP_K].reshape(1, MOE_TOP_K * n_all)
    xs_lo = _sc_scatter(cat(1, 0), pos_flat, n_rows)
    xs_hi = _sc_scatter(cat(2, 0), pos_flat, n_rows)
    ys_lo, ys_hi = _ffn(block_expert, n_used, xs_lo, xs_hi, wup, bup, wdn, bdn)
    og_lo, og_hi = _sc_gather(ys_lo, pos_flat), _sc_gather(ys_hi, pos_flat)
    starts = np.cumsum([0] + sizes[:-1]).tolist()
    return [_combine(g[0], og_lo, og_hi, g[3], g2, b2, r0) for g, r0 in zip(groups, starts)]


def _mixer(x, pos_tab, tab_period, s0, rows_per_stream, ret_rows, caches, weights, cnt0):
    wp, wr, wd, wo, g1, b1, rwt, rb = weights
    n = x.shape[0]
    n_streams = n // rows_per_stream
    rq, rk, rv, rg, aq, ak, av, iq, ikw, gr, ga = _project(x, wp, pos_tab, tab_period)
    ret, s_new = _retention(rq, rk, rv, rg, s0, rows_per_stream, ret_rows)
    if caches is None:
        per_stream = lambda a: a.reshape(n_streams, rows_per_stream, a.shape[1])
        od = _dsa(aq, iq, ikw, per_stream(ak), per_stream(av), per_stream(ikw), DSA_TQ)
    else:
        od = _dsa_cached(aq, iq, ikw, *caches, ak, av, rows_per_stream)
    h1, h_lo, h_hi, gtm, ek, rk_, totals = _merge(x, ret, od, gr, ga, wr, wd, wo, g1, b1, rwt, rb, cnt0)
    return (h1, h_lo, h_hi, gtm, ek, rk_), totals, (s_new, ak, av, ikw[:, :IDX_DIM])


def kernel(x_prompt, x_sample, state_ret, cache_k, cache_v, cache_idx_k, w_in, w_ret_o, w_dsa_o, w_o,
           ln1_g, ln1_b, router_w, router_b, w_up, b_up, w_down, b_down, ln2_g, ln2_b):
    assert w_in.shape[0] == DEPTH
    batch, seq, _ = x_prompt.shape
    dec_batch, dec_seq, _ = x_sample.shape
    past = cache_k.shape[2]
    assert seq % PROJ_TM == 0 and seq % DSA_TQ == 0 and PROJ_TM % dec_seq == 0

    l = 0
    mixer_w = (_pack_w_in(w_in[l]), w_ret_o[l].astype(BF16), w_dsa_o[l].astype(BF16), w_o[l].astype(BF16),
               ln1_g[l][None, :], ln1_b[l][None, :], router_w[l].T, router_b[l][:, None])
    moe_w = (
        _deinterleave_w_up(w_up[l]),
        b_up[l].reshape(N_EXPERTS, 2 * D_FF // UP_BLOCK, LANES, 2).transpose(0, 1, 3, 2).reshape(N_EXPERTS, 1, 2 * D_FF),
        w_down[l].astype(BF16), b_down[l][:, None, :], ln2_g[l][None, :], ln2_b[l][None, :])

    tab_p = _rot_tables(jnp.arange(seq))
    zeros_state = jnp.zeros((batch, RET_HEADS, RET_DK, RET_DV), F32)
    moe_p, totals_p, (s_p, k_p, v_p, ik_p) = _mixer(
        x_prompt.reshape(batch * seq, D_MODEL), tab_p, seq // PROJ_TM, zeros_state, seq, RET_CHUNK, None, mixer_w,
        jnp.zeros((N_EXPERTS, LANES), F32))

    tab_s = jnp.tile(_rot_tables(past + jnp.arange(dec_seq)), (PROJ_TM // dec_seq, 1))
    caches = (cache_k[l].reshape(dec_batch, past, LANES), cache_v[l].reshape(dec_batch, past, LANES), cache_idx_k[l])
    moe_s, totals, (s_s, k_s, v_s, ik_s) = _mixer(
        x_sample.reshape(dec_batch * dec_seq, D_MODEL), tab_s, 1, state_ret[l], dec_seq, dec_seq, caches, mixer_w,
        totals_p)

    y_p, y_s = _moe([moe_p, moe_s], totals, *moe_w)

    kv = (DSA_KV_HEADS, DSA_HEAD_DIM)
    return (y_p.reshape(batch, seq, D_MODEL), y_s.reshape(dec_batch, dec_seq, D_MODEL),
            s_p[None], k_p.reshape(1, batch, seq, *kv), v_p.reshape(1, batch, seq, *kv),
            ik_p.reshape(1, batch, seq, IDX_DIM),
            s_s[None], k_s.reshape(1, dec_batch, dec_seq, *kv), v_s.reshape(1, dec_batch, dec_seq, *kv),
            ik_s.reshape(1, dec_batch, dec_seq, IDX_DIM))
```

```python
import functools

import numpy as np
import jax
import jax.numpy as jnp
from jax import lax
from jax.experimental import pallas as pl
from jax.experimental.pallas import tpu as pltpu
from jax.experimental.pallas import tpu_sc as plsc

F32 = jnp.float32
BF16 = jnp.bfloat16
I32 = jnp.int32

D_MODEL = 1024
CHUNK = 64
RET_HEADS = 4
RET_DK = 128
RET_DV = 256
RET_ROPE_BASE = 10000.0
DSA_HEADS = 8
DSA_KV_HEADS = 2
DSA_HEAD_DIM = 64
IDX_HEADS = 8
IDX_DIM = 64
DSA_TOPK = 256
ROPE_THETA = 500000.0
N_EXPERTS = 32
MOE_TOP_K = 4
D_FF = 1024
SWIGLU_ALPHA = 1.702
SWIGLU_LIMIT = 7.0
LN_EPS = 1e-5
GN_EPS = 1e-6
DEPTH = 1
DEEPNORM_ALPHA = (2.0 * DEPTH) ** 0.25
PROJ_WIDTHS = (RET_HEADS * RET_DK, RET_HEADS * RET_DK, RET_HEADS * RET_DV, RET_HEADS * RET_DV,
               DSA_HEADS * DSA_HEAD_DIM, DSA_KV_HEADS * DSA_HEAD_DIM, DSA_KV_HEADS * DSA_HEAD_DIM,
               IDX_HEADS * IDX_DIM, IDX_DIM, IDX_HEADS, D_MODEL, D_MODEL)

LANES = 128
MIB = 1024 * 1024

OFF_RQ, OFF_RK, OFF_RV, OFF_RG = 0, 512, 1024, 2048
OFF_AQ, OFF_AK, OFF_AV, OFF_IQ, OFF_IKW = 3072, 3584, 3712, 3840, 4352
OFF_GR, OFF_GA, PACKED_COLS = 4480, 5504, 6528
TAB_COLS = 8 * LANES

PROJ_TM = 512
RET_CHUNK = 128
RET_CHUNKS_PER_STEP = 8
DSA_TQ = 256
DSA_ROWS = 512
DSA_GROUP_MAX = 4
SEARCH_UNROLL = 4
MERGE_TM = 512
FFN_BLOCK = 512
SC_WINDOW = 128
QUARTER = D_MODEL // 4
UP_BLOCK = 2 * LANES


def _nt(a, b):
    return lax.dot_general(a, b, (((1,), (1,)), ((), ())), preferred_element_type=F32)


def _mm(a, b):
    return jnp.dot(a, b, preferred_element_type=F32)


def _proj_kernel(x_ref, w_ref, tab_ref, rq_ref, rk_ref, rv_ref, rg_ref, aq_ref, ak_ref, av_ref,
                 iq_ref, ikw_ref, gr_ref, ga_ref, od0_ref):
    xb = x_ref[...].astype(BF16)
    od0_ref[...] = jnp.zeros(od0_ref.shape, od0_ref.dtype)

    def mm(c0, n):
        return _mm(xb, w_ref[:, c0:c0 + n])

    def tab(i):
        return tab_ref[:, i * LANES:(i + 1) * LANES]

    def rot_full(z):
        return z * tab(0) + pltpu.roll(z, 64, 1) * tab(1)

    def rot_part(z, c):
        return z * tab(c) + pltpu.roll(z, LANES - 8, 1) * tab(c + 1) + pltpu.roll(z, 8, 1) * tab(c + 2)

    z = mm(OFF_RQ, 512)
    for h in range(4):
        sl = slice(h * LANES, (h + 1) * LANES)
        rq_ref[:, sl] = rot_full(z[:, sl]).astype(BF16)
    z = mm(OFF_RK, 512)
    for h in range(4):
        sl = slice(h * LANES, (h + 1) * LANES)
        rk_ref[:, sl] = (rot_full(z[:, sl]) * (RET_DK ** -0.5)).astype(BF16)
    for c in range(2):
        rv_ref[:, c * 512:(c + 1) * 512] = mm(OFF_RV + c * 512, 512).astype(BF16)
    for c in range(2):
        rg_ref[:, c * 512:(c + 1) * 512] = mm(OFF_RG + c * 512, 512)
    z = mm(OFF_AQ, 512)
    for h in range(4):
        sl = slice(h * LANES, (h + 1) * LANES)
        aq_ref[:, sl] = rot_part(z[:, sl], 2).astype(BF16)
    z = mm(OFF_AK, 256)
    ak_ref[...] = rot_part(z[:, :LANES], 2)
    av_ref[...] = z[:, LANES:]
    z = mm(OFF_IQ, 512)
    for h in range(4):
        sl = slice(h * LANES, (h + 1) * LANES)
        iq_ref[:, sl] = rot_part(z[:, sl], 2).astype(BF16)
    ikw_ref[...] = rot_part(mm(OFF_IKW, LANES), 5)
    for c in range(2):
        gr_ref[:, c * 512:(c + 1) * 512] = mm(OFF_GR + c * 512, 512)
    for c in range(2):
        ga_ref[:, c * 512:(c + 1) * 512] = mm(OFF_GA + c * 512, 512)


def _rot_tables(pos):
    p = pos.shape[0]
    posf = pos.astype(F32)[:, None]
    ret_f = RET_ROPE_BASE ** (-jnp.linspace(0.0, 1.0, RET_DK // 2, dtype=F32))
    ang = posf * ret_f[None, :]
    c, s = jnp.cos(ang), jnp.sin(ang)
    cos_r = jnp.concatenate([c, c], 1)
    sin_r = jnp.concatenate([-s, s], 1)
    n_rot = DSA_HEAD_DIM // 4
    att_f = ROPE_THETA ** (-jnp.arange(0, n_rot, 2, dtype=F32) / n_rot)
    ang2 = posf * att_f[None, :]
    c2, s2 = jnp.cos(ang2), jnp.sin(ang2)
    half = n_rot // 2
    rest = DSA_HEAD_DIM - 2 * half
    c64 = jnp.concatenate([c2, c2, jnp.ones((p, rest), F32)], 1)
    s1_64 = jnp.concatenate([-s2, jnp.zeros((p, DSA_HEAD_DIM - half), F32)], 1)
    s2_64 = jnp.concatenate([jnp.zeros((p, half), F32), s2, jnp.zeros((p, rest), F32)], 1)
    z64 = jnp.zeros((p, DSA_HEAD_DIM), F32)
    ci = jnp.concatenate([c64, jnp.full((p, IDX_HEADS), IDX_HEADS ** -0.5, F32),
                          jnp.zeros((p, DSA_HEAD_DIM - IDX_HEADS), F32)], 1)
    return jnp.concatenate([cos_r, sin_r,
                            jnp.concatenate([c64, c64], 1), jnp.concatenate([s1_64, s1_64], 1),
                            jnp.concatenate([s2_64, s2_64], 1),
                            ci, jnp.concatenate([s1_64, z64], 1), jnp.concatenate([s2_64, z64], 1)], 1)


def _pack_w_in(w):
    cuts = np.cumsum(PROJ_WIDTHS)[:-1].tolist()
    rq, rk, rv, rg, aq, ak, av, iq, ik, iw, gr, ga = jnp.split(w, cuts, axis=1)
    pad = jnp.zeros((w.shape[0], LANES - IDX_DIM - IDX_HEADS), w.dtype)
    return jnp.concatenate([rq, rk, rv, rg, aq, ak, av, iq, ik, iw, pad, gr, ga], axis=1).astype(BF16)


def _project(x, wp, tab, tab_period):
    n = x.shape[0]
    tm = PROJ_TM
    row = lambda w: pl.BlockSpec((tm, w), lambda i: (i, 0))
    out_shapes = [((n, 512), BF16), ((n, 512), BF16), ((n, 1024), BF16), ((n, 1024), F32),
                  ((n, 512), BF16), ((n, LANES), F32), ((n, LANES), F32), ((n, 512), BF16),
                  ((n, LANES), F32), ((n, 1024), F32), ((n, 1024), F32), ((n, DSA_HEADS * DSA_HEAD_DIM), BF16)]
    return pl.pallas_call(
        _proj_kernel,
        grid=(n // tm,),
        in_specs=[row(D_MODEL),
                  pl.BlockSpec((D_MODEL, PACKED_COLS), lambda i: (0, 0), pipeline_mode=pl.Buffered(1)),
                  pl.BlockSpec((tm, TAB_COLS), lambda i: (i % tab_period, 0))],
        out_specs=[row(s[1]) for s, _ in out_shapes],
        out_shape=[jax.ShapeDtypeStruct(s, d) for s, d in out_shapes],
        compiler_params=pltpu.CompilerParams(dimension_semantics=("parallel",), vmem_limit_bytes=52 * MIB),
        name="proj",
    )(x, wp, tab)


def _ret_kernel(dec_ref, xi_ref, zeta_ref, rq_ref, rk_ref, rv_ref, rg_ref, s0_ref, ret_ref, sout_ref, st_ref,
                *, rows, n_chunk, g_pow):
    j = pl.program_id(1)
    cpad = RET_CHUNK

    @pl.when(j == 0)
    def _():
        st_ref[...] = s0_ref[0]

    def padded(v):
        if rows == cpad:
            return v
        return jnp.concatenate([v, jnp.zeros((cpad - rows, v.shape[1]), v.dtype)], axis=0)

    for c in range(n_chunk):
        rs = slice(c * rows, (c + 1) * rows)
        for h in range(RET_HEADS):
            ks = slice(h * RET_DK, (h + 1) * RET_DK)
            vs = slice(h * RET_DV, (h + 1) * RET_DV)
            q = padded(rq_ref[rs, ks])
            kt = padded(rk_ref[rs, ks].astype(F32)).T
            v = padded(rv_ref[rs, vs])
            s = st_ref[h]
            sc = _mm(q, kt.astype(BF16)) * dec_ref[h]
            o = _mm(sc.astype(BF16), v) + _mm(q, s.astype(BF16)) * xi_ref[h]
            st_ref[h] = g_pow[h] * s + _mm((kt * zeta_ref[h]).astype(BF16), v)
            o = o[:rows]
            mu = jnp.mean(o, axis=-1, keepdims=True)
            d = o - mu
            var = jnp.mean(d * d, axis=-1, keepdims=True)
            gn = d * lax.rsqrt(var + GN_EPS)
            g = rg_ref[rs, vs]
            ret_ref[rs, vs] = (gn * (g * jax.nn.sigmoid(g))).astype(BF16)

    @pl.when(j == pl.num_programs(1) - 1)
    def _():
        sout_ref[0] = st_ref[...]


def _retention(rq, rk, rv, rg, s0, rows_per_stream, rows):
    n = rq.shape[0]
    n_streams = n // rows_per_stream
    n_chunk = min(rows_per_stream // rows, RET_CHUNKS_PER_STEP)
    blk = rows * n_chunk
    nb = rows_per_stream // blk
    gam = 1.0 - 2.0 ** (-5.0 - np.arange(RET_HEADS, dtype=np.float64))
    i = np.arange(RET_CHUNK, dtype=np.float64)
    diff = i[:, None] - i[None, :]
    dec = np.where(diff >= 0, gam[:, None, None] ** np.maximum(diff, 0.0)[None], 0.0)
    xi = gam[:, None, None] ** (i + 1.0)[None, :, None]
    zeta = np.where(i < rows, gam[:, None, None] ** (rows - 1.0 - i)[None, None, :], 0.0)
    g_pow = tuple(float(g ** rows) for g in gam)
    const = lambda shape: pl.BlockSpec(shape, lambda s, j: (0,) * len(shape))
    row = lambda w: pl.BlockSpec((blk, w), lambda s, j: (s * nb + j, 0))
    st = pl.BlockSpec((1, RET_HEADS, RET_DK, RET_DV), lambda s, j: (s, 0, 0, 0))
    return pl.pallas_call(
        functools.partial(_ret_kernel, rows=rows, n_chunk=n_chunk, g_pow=g_pow),
        grid=(n_streams, nb),
        in_specs=[const((RET_HEADS, RET_CHUNK, RET_CHUNK)), const((RET_HEADS, RET_CHUNK, 1)),
                  const((RET_HEADS, 1, RET_CHUNK)), row(512), row(512), row(1024), row(1024), st],
        out_specs=[row(1024), st],
        out_shape=[jax.ShapeDtypeStruct((n, RET_HEADS * RET_DV), BF16),
                   jax.ShapeDtypeStruct((n_streams, RET_HEADS, RET_DK, RET_DV), F32)],
        scratch_shapes=[pltpu.VMEM((RET_HEADS, RET_DK, RET_DV), F32)],
        compiler_params=pltpu.CompilerParams(dimension_semantics=("parallel", "arbitrary"),
                                             vmem_limit_bytes=32 * MIB),
        name="retention",
    )(jnp.asarray(dec, F32), jnp.asarray(xi, F32), jnp.asarray(zeta, F32), rq, rk, rv, rg, s0)


def _dsa_kernel(aq_ref, iq_ref, ikwq_ref, kk_ref, vv_ref, ikk_ref, *rest,
                group, tq, n_keys, row0, limit_const, n_sel):
    o_ref, w_ref, bias_ref = rest[-3:]
    rows = group * tq
    nsel_f = float(n_sel)
    neg_inf = -jnp.inf

    col = lax.broadcasted_iota(I32, (tq, n_keys), 1)
    if limit_const is None:
        limit = (jnp.right_shift(lax.broadcasted_iota(I32, (tq, 1), 0) + row0, 6) + 1) * CHUNK
    else:
        limit = limit_const

    def score_stream(s, carry):
        ikb = ikk_ref[s][:, :IDX_DIM].astype(BF16)
        iww = ikwq_ref[s][:, IDX_DIM:IDX_DIM + IDX_HEADS] * (IDX_DIM ** -0.5)
        acc = jnp.zeros((tq, n_keys), F32)
        for p in range(IDX_HEADS // 2):
            slab = iq_ref[s, :, p * LANES:(p + 1) * LANES]
            for hh in range(2):
                h = 2 * p + hh
                acc = acc + jnp.maximum(_nt(slab[:, hh * IDX_DIM:(hh + 1) * IDX_DIM], ikb), 0.0) * iww[:, h:h + 1]
        w_ref[pl.ds(pl.multiple_of(s * tq, tq), tq), :] = jnp.where(col < limit, acc, neg_inf)
        return carry

    lax.fori_loop(0, group, score_stream, 0)

    sc = w_ref[...]
    pos = jnp.sum(jnp.where(sc >= 0.0, 1.0, 0.0), axis=1, keepdims=True) >= nsel_f
    kk = jnp.where(pos, nsel_f, float(n_keys - n_sel + 1))
    w_ref[...] = jnp.where(pos, sc, -sc)

    def bit_step(i, u):
        cand_u = u | jnp.left_shift(jnp.int32(1), 30 - i)
        cand = pltpu.bitcast(cand_u, F32)
        cnt = jnp.sum(jnp.where(w_ref[...] >= cand, 1.0, 0.0), axis=1, keepdims=True)
        return jnp.where(cnt >= kk, cand_u, u)

    mag_u = lax.fori_loop(0, 31, bit_step, jnp.zeros((rows, 1), I32), unroll=SEARCH_UNROLL)
    mag = pltpu.bitcast(mag_u, F32)
    thr = jnp.where(pos, mag, -mag)

    sc = jnp.where(pos, w_ref[...], -w_ref[...])
    short = jnp.sum(jnp.where(sc >= thr, 1.0, 0.0), axis=1, keepdims=True) < nsel_f
    thr = jnp.where(jnp.logical_and(short, jnp.logical_not(pos)), -pltpu.bitcast(mag_u + 1, F32), thr)
    ge = sc >= thr
    cnt_gt = jnp.sum(jnp.where(sc > thr, 1.0, 0.0), axis=1, keepdims=True)
    cnt_ge = jnp.sum(jnp.where(ge, 1.0, 0.0), axis=1, keepdims=True)
    bias_ref[...] = jnp.where(jnp.logical_and(ge, sc > neg_inf), 0.0, neg_inf)
    excess = jnp.logical_and(cnt_ge > nsel_f, thr > neg_inf)

    @pl.when(jnp.max(jnp.where(excess, 1.0, 0.0)) > 0.0)
    def _():
        need = nsel_f - cnt_gt
        tri = jnp.where(lax.broadcasted_iota(I32, (LANES, LANES), 0) < lax.broadcasted_iota(I32, (LANES, LANES), 1),
                        1.0, 0.0).astype(BF16)
        before = jnp.zeros((rows, 1), F32)
        for b in range(n_keys // LANES):
            sl = slice(b * LANES, (b + 1) * LANES)
            sblk = jnp.where(pos, w_ref[:, sl], -w_ref[:, sl])
            eq = jnp.where(sblk == thr, 1.0, 0.0)
            rank = _mm(eq.astype(BF16), tri) + before
            keep = jnp.logical_or(sblk > thr, jnp.logical_and(sblk == thr, rank < need))
            bias_ref[:, sl] = jnp.where(jnp.logical_and(keep, sblk > neg_inf), 0.0, neg_inf)
            before = before + jnp.sum(eq, axis=1, keepdims=True)

    def attend_stream(s, carry):
        bias = bias_ref[pl.ds(pl.multiple_of(s * tq, tq), tq), :]
        kfull = kk_ref[s]
        vfull = vv_ref[s]
        for g in range(DSA_KV_HEADS):
            gs = slice(g * DSA_HEAD_DIM, (g + 1) * DSA_HEAD_DIM)
            kg = kfull[:, gs].astype(BF16)
            vg = vfull[:, gs].astype(BF16)
            for pp in range(2):
                p = 2 * g + pp
                slab = (aq_ref[s, :, p * LANES:(p + 1) * LANES].astype(F32) * (DSA_HEAD_DIM ** -0.5)).astype(BF16)
                outs = []
                for hh in range(2):
                    lg = _nt(slab[:, hh * DSA_HEAD_DIM:(hh + 1) * DSA_HEAD_DIM], kg) + bias
                    m = jnp.max(lg, axis=1, keepdims=True)
                    pr = jnp.exp(lg - m)
                    den = jnp.sum(pr, axis=1, keepdims=True)
                    outs.append(_mm(pr.astype(BF16), vg) / den)
                o_ref[s, :, p * LANES:(p + 1) * LANES] = jnp.concatenate(outs, axis=1).astype(BF16)
        return carry

    lax.fori_loop(0, group, attend_stream, 0)


def _dsa_call(aq, iq, ikw, kk, vv, ikk, group, tq, jq, n_keys, limit_const, n_sel, name, prev=None):
    n_streams, rows_per_stream = aq.shape[0], aq.shape[1]
    extra_specs, extra_args, aliases = [], (), {}
    if prev is not None:
        extra_specs, extra_args, aliases = [pl.BlockSpec(memory_space=pl.ANY)], (prev,), {6: 0}
    qrow = lambda a: pl.BlockSpec((group, tq, a.shape[2]), lambda s: (s, jq, 0))
    krow = lambda a: pl.BlockSpec((group, n_keys, a.shape[2]), lambda s: (s, 0, 0))
    width = DSA_HEADS * DSA_HEAD_DIM
    return pl.pallas_call(
        functools.partial(_dsa_kernel, group=group, tq=tq, n_keys=n_keys, row0=jq * tq, limit_const=limit_const,
                          n_sel=n_sel),
        grid=(n_streams // group,),
        in_specs=[qrow(aq), qrow(iq), qrow(ikw), krow(kk), krow(vv), krow(ikk)] + extra_specs,
        out_specs=pl.BlockSpec((group, tq, width), lambda s: (s, jq, 0)),
        out_shape=jax.ShapeDtypeStruct((n_streams, rows_per_stream, width), BF16),
        input_output_aliases=aliases,
        scratch_shapes=[pltpu.VMEM((group * tq, n_keys), F32), pltpu.VMEM((group * tq, n_keys), F32)],
        compiler_params=pltpu.CompilerParams(dimension_semantics=("parallel",), vmem_limit_bytes=56 * MIB),
        name=name,
    )(aq, iq, ikw, kk, vv, ikk, *extra_args)


def _dsa_cached_kernel(aq_ref, iq_ref, ikwq_ref, ck_ref, cv_ref, cik_ref, nk_ref, nv_ref, o_ref,
                       w_ref, bias_ref, kk_ref, vv_ref, ikk_ref, **static):
    past, new = ck_ref.shape[1], nk_ref.shape[1]
    for dst, cache, fresh in ((kk_ref, ck_ref, nk_ref), (vv_ref, cv_ref, nv_ref), (ikk_ref, cik_ref, ikwq_ref)):
        dst[:, :past, :cache.shape[2]] = cache[...]
        dst[:, past:past + new, :] = fresh[...]
        dst[:, past + new:, :] = jnp.zeros((dst.shape[0], dst.shape[1] - past - new, dst.shape[2]), F32)
    _dsa_kernel(aq_ref, iq_ref, ikwq_ref, kk_ref, vv_ref, ikk_ref, o_ref, w_ref, bias_ref, **static)


def _dsa_cached(aq, iq, ikw, cache_k, cache_v, cache_ik, new_k, new_v, tq):
    n_streams, past, _ = cache_k.shape
    n = aq.shape[0]
    per_stream = lambda a: a.reshape(n_streams, tq, a.shape[1])
    aq, iq, ikw, new_k, new_v = per_stream(aq), per_stream(iq), per_stream(ikw), per_stream(new_k), per_stream(new_v)
    limit = past + tq
    n_keys = -(-limit // LANES) * LANES
    group = max(1, min(n_streams, DSA_ROWS // tq, DSA_GROUP_MAX))
    blk = lambda a: pl.BlockSpec((group,) + a.shape[1:], lambda s: (s, 0, 0))
    width = DSA_HEADS * DSA_HEAD_DIM
    return pl.pallas_call(
        functools.partial(_dsa_cached_kernel, group=group, tq=tq, n_keys=n_keys, row0=0, limit_const=limit,
                          n_sel=min(DSA_TOPK, limit // 4)),
        grid=(n_streams // group,),
        in_specs=[blk(a) for a in (aq, iq, ikw, cache_k, cache_v, cache_ik, new_k, new_v)],
        out_specs=pl.BlockSpec((group, tq, width), lambda s: (s, 0, 0)),
        out_shape=jax.ShapeDtypeStruct((n_streams, tq, width), BF16),
        scratch_shapes=[pltpu.VMEM((group * tq, n_keys), F32), pltpu.VMEM((group * tq, n_keys), F32)] +
                       [pltpu.VMEM((group, n_keys, LANES), F32)] * 3,
        compiler_params=pltpu.CompilerParams(dimension_semantics=("parallel",), vmem_limit_bytes=48 * MIB),
        name="dsa_s",
    )(aq, iq, ikw, cache_k, cache_v, cache_ik, new_k, new_v).reshape(n, width)


def _dsa(aq, iq, ikw, kk, vv, ikk, tq, out):
    n_streams, n_keys, _ = kk.shape
    n = aq.shape[0]
    per_stream = lambda a: a.reshape(n_streams, n // n_streams, a.shape[1])
    aq, iq, ikw = per_stream(aq), per_stream(iq), per_stream(ikw)
    nq = n // n_streams // tq
    n_sel = min(DSA_TOPK, n_keys // 4)
    group = max(1, min(n_streams, DSA_ROWS // tq, DSA_GROUP_MAX))
    out = per_stream(out)
    for jq in range(nq):
        out = _dsa_call(aq, iq, ikw, kk, vv, ikk, group, tq, jq, (jq + 1) * tq, None, n_sel, f"dsa_p{jq}", prev=out)
    return out.reshape(n, -1)


def _pack_rows(v):
    q = QUARTER
    bits = lambda x: pltpu.bitcast(x.astype(BF16).astype(F32), I32)
    pair = lambda c: lax.shift_right_logical(bits(v[:, c * q:(c + 1) * q]), 16) | bits(v[:, (c + 1) * q:(c + 2) * q])
    return pair(0), pair(2)


def _unpack_rows(lo, hi):
    parts = []
    for w in (lo, hi):
        parts.append(pltpu.bitcast(lax.shift_left(w, 16), F32))
        parts.append(pltpu.bitcast(w & jnp.int32(-65536), F32))
    return jnp.concatenate(parts, axis=1)


def _layer_norm(v, g, b):
    mu = jnp.mean(v, axis=-1, keepdims=True)
    d = v - mu
    var = jnp.mean(d * d, axis=-1, keepdims=True)
    return d * lax.rsqrt(var + LN_EPS) * g + b


def _merge_kernel(x_ref, ret_ref, od_ref, gr_ref, ga_ref, wr_ref, wd_ref, wo_ref, g1_ref, b1_ref, rwt_ref, rb_ref,
                  tri_ref, cnt0_ref, h1_ref, hlo_ref, hhi_ref, gtm_ref, ek_ref, rk_ref, tot_ref, cnt_ref):
    @pl.when(pl.program_id(0) == 0)
    def _():
        cnt_ref[...] = cnt0_ref[:, 0:1]

    y_ret = _mm(ret_ref[...], wr_ref[...])
    y_dsa = _mm(od_ref[...], wd_ref[...])
    merged = jax.nn.sigmoid(gr_ref[...]) * y_ret + jax.nn.sigmoid(ga_ref[...]) * y_dsa
    mix = _mm(merged.astype(BF16), wo_ref[...])
    h1 = _layer_norm(DEEPNORM_ALPHA * x_ref[...] + mix, g1_ref[...], b1_ref[...])
    h1_ref[...] = h1
    hlo_ref[...], hhi_ref[...] = _pack_rows(h1)

    logits = lax.dot_general(rwt_ref[...], h1, (((1,), (1,)), ((), ())), preferred_element_type=F32,
                             precision=lax.Precision.HIGHEST) + rb_ref[...]
    tm = logits.shape[1]
    e_iota = lax.broadcasted_iota(I32, (N_EXPERTS, tm), 0)
    tops, hots, firsts = [], [], []
    for _ in range(MOE_TOP_K):
        m = jnp.max(logits, axis=0, keepdims=True)
        first = jnp.min(jnp.where(logits == m, e_iota, N_EXPERTS), axis=0, keepdims=True)
        hot = e_iota == first
        tops.append(m)
        hots.append(hot)
        firsts.append(first)
        logits = jnp.where(hot, -jnp.inf, logits)
    exps = [jnp.exp(m - tops[0]) for m in tops]
    den = exps[0] + exps[1] + exps[2] + exps[3]
    sel = jnp.zeros((N_EXPERTS, tm), F32)
    for hot in hots:
        sel = sel + jnp.where(hot, 1.0, 0.0)
    rank = _mm(sel.astype(BF16), tri_ref[...]) + cnt_ref[...]
    ranks = [jnp.sum(jnp.where(hot, rank, 0.0), axis=0, keepdims=True).astype(I32) for hot in hots]
    pad_i = jnp.zeros((8 - MOE_TOP_K, tm), I32)
    ek_ref[...] = jnp.concatenate(firsts + [pad_i], axis=0)
    rk_ref[...] = jnp.concatenate(ranks + [pad_i], axis=0)
    gates = jnp.concatenate([e / den for e in exps] + [jnp.zeros((LANES - MOE_TOP_K, tm), F32)], axis=0)
    gtm_ref[...] = gates.T
    cnt_ref[...] = cnt_ref[...] + jnp.sum(sel, axis=1, keepdims=True)
    tot_ref[...] = jnp.broadcast_to(cnt_ref[...], tot_ref.shape)


def _merge(x, ret, od, gr, ga, wr, wd, wo, g1, b1, rwt, rb, cnt0):
    n = x.shape[0]
    tm = MERGE_TM
    row = lambda w: pl.BlockSpec((tm, w), lambda i: (i, 0))
    const = lambda a: pl.BlockSpec(a.shape, lambda i: (0,) * a.ndim)
    col = pl.BlockSpec((8, tm), lambda i: (0, i))
    tri = jnp.asarray(np.triu(np.ones((tm, tm), np.float32), 1), BF16)
    return pl.pallas_call(
        _merge_kernel,
        grid=(n // tm,),
        in_specs=[row(D_MODEL), row(1024), row(512), row(1024), row(1024), const(wr), const(wd), const(wo),
                  const(g1), const(b1), const(rwt), const(rb), const(tri), const(cnt0)],
        out_specs=[row(D_MODEL), row(QUARTER), row(QUARTER), row(LANES), col, col,
                   pl.BlockSpec((N_EXPERTS, LANES), lambda i: (0, 0))],
        out_shape=[jax.ShapeDtypeStruct((n, D_MODEL), F32), jax.ShapeDtypeStruct((n, QUARTER), I32),
                   jax.ShapeDtypeStruct((n, QUARTER), I32), jax.ShapeDtypeStruct((n, LANES), F32),
                   jax.ShapeDtypeStruct((8, n), I32), jax.ShapeDtypeStruct((8, n), I32),
                   jax.ShapeDtypeStruct((N_EXPERTS, LANES), F32)],
        scratch_shapes=[pltpu.VMEM((N_EXPERTS, 1), F32)],
        compiler_params=pltpu.CompilerParams(dimension_semantics=("arbitrary",), vmem_limit_bytes=48 * MIB),
        name="merge",
    )(x, ret, od, gr, ga, wr, wd, wo, g1, b1, rwt, rb, tri, cnt0)


def _deinterleave_kernel(w_ref, o_ref):
    r = lax.broadcasted_iota(I32, (UP_BLOCK, UP_BLOCK), 0)
    c = lax.broadcasted_iota(I32, (UP_BLOCK, UP_BLOCK), 1)
    src = jnp.where(c < LANES, 2 * c, 2 * (c - LANES) + 1)
    perm = jnp.where(r == src, 1.0, 0.0).astype(BF16)
    for b in range(w_ref.shape[2] // UP_BLOCK):
        sl = slice(b * UP_BLOCK, (b + 1) * UP_BLOCK)
        o_ref[0, :, sl] = _mm(w_ref[0, :, sl].astype(BF16), perm).astype(BF16)


def _deinterleave_w_up(w_up):
    n_e, d_in, d_out = w_up.shape
    cols = d_out
    spec = pl.BlockSpec((1, d_in, cols), lambda e, c: (e, 0, c))
    return pl.pallas_call(
        _deinterleave_kernel,
        grid=(n_e, d_out // cols),
        in_specs=[spec],
        out_specs=spec,
        out_shape=jax.ShapeDtypeStruct(w_up.shape, BF16),
        compiler_params=pltpu.CompilerParams(dimension_semantics=("parallel", "parallel"),
                                             vmem_limit_bytes=40 * MIB),
        name="w_up_prep",
    )(w_up)


def _pos_kernel(off_ref, ek_ref, rk_ref, pos_ref):
    ek = ek_ref[...]
    pos = rk_ref[...]
    for e in range(N_EXPERTS):
        pos = pos + jnp.where(ek == e, off_ref[e], 0)
    pos_ref[...] = pos


def _positions(off, ek, rk):
    n = ek.shape[1]
    tn = min(n, 2048)
    spec = pl.BlockSpec((8, tn), lambda i, off: (0, i))
    return pl.pallas_call(
        _pos_kernel,
        grid_spec=pltpu.PrefetchScalarGridSpec(num_scalar_prefetch=1, grid=(n // tn,), in_specs=[spec, spec],
                                               out_specs=spec),
        out_shape=jax.ShapeDtypeStruct((8, n), I32),
        name="moe_pos",
    )(off, ek, rk)


def _sc_mesh():
    return plsc.VectorSubcoreMesh(core_axis_name="core", subcore_axis_name="subcore")


def _sc_scatter(x, pos_flat, n_rows):
    n = x.shape[0]
    nw = n // SC_WINDOW

    @functools.partial(pl.kernel, out_type=jax.ShapeDtypeStruct((n_rows, x.shape[1]), x.dtype), mesh=_sc_mesh())
    def scatter(x_hbm, p_hbm, o_hbm):
        def body(x_vmem, *p_vmem):
            for p in p_vmem:
                pltpu.sync_copy(x_vmem, o_hbm.at[p.at[0]])

        pltpu.emit_pipeline(
            body, grid=(nw,),
            in_specs=[pl.BlockSpec((SC_WINDOW, x.shape[1]), lambda i: (i, 0))] +
                     [pl.BlockSpec((1, SC_WINDOW), functools.partial(lambda k, i: (0, k * nw + i), k))
                      for k in range(MOE_TOP_K)],
            out_specs=[], core_axis_name=("core", "subcore"), dimension_semantics=(pltpu.PARALLEL,),
        )(x_hbm, *([p_hbm] * MOE_TOP_K))

    return scatter(x, pos_flat)


def _sc_gather(y, pos_flat):
    m = pos_flat.shape[1]

    @functools.partial(pl.kernel, out_type=jax.ShapeDtypeStruct((m, y.shape[1]), y.dtype), mesh=_sc_mesh())
    def gather(y_hbm, p_hbm, o_hbm):
        def body(p_vmem, o_vmem):
            pltpu.sync_copy(y_hbm.at[p_vmem.at[0]], o_vmem)

        pltpu.emit_pipeline(
            body, grid=(m // SC_WINDOW,),
            in_specs=[pl.BlockSpec((1, SC_WINDOW), lambda i: (0, i))],
            out_specs=[pl.BlockSpec((SC_WINDOW, y.shape[1]), lambda i: (i, 0))],
            core_axis_name=("core", "subcore"), dimension_semantics=(pltpu.PARALLEL,),
        )(p_hbm, o_hbm)

    return gather(y, pos_flat)


def _ffn_kernel(be_ref, nu_ref, xlo_ref, xhi_ref, wup_ref, bup_ref, wdn_ref, bdn_ref, ylo_ref, yhi_ref):
    @pl.when(pl.program_id(0) < nu_ref[0])
    def _():
        x = _unpack_rows(xlo_ref[...], xhi_ref[...]).astype(BF16)
        h = _mm(x, wup_ref[0]) + bup_ref[0]
        acts = []
        for b in range(2 * D_FF // UP_BLOCK):
            glu = jnp.minimum(h[:, b * UP_BLOCK:b * UP_BLOCK + LANES], SWIGLU_LIMIT)
            lin = jnp.clip(h[:, b * UP_BLOCK + LANES:(b + 1) * UP_BLOCK], -SWIGLU_LIMIT, SWIGLU_LIMIT)
            acts.append(glu * jax.nn.sigmoid(SWIGLU_ALPHA * glu) * (lin + 1.0))
        act = jnp.concatenate(acts, axis=1)
        ylo_ref[...], yhi_ref[...] = _pack_rows(_mm(act.astype(BF16), wdn_ref[0]) + bdn_ref[0])


def _ffn(block_expert, n_used, xs_lo, xs_hi, wup, bup, wdn, bdn):
    rows = xs_lo.shape[0]
    blk = FFN_BLOCK
    row = pl.BlockSpec((blk, QUARTER), lambda i, be, nu: (i, 0))
    per_expert = lambda a: pl.BlockSpec((1,) + a.shape[1:], lambda i, be, nu: (be[i], 0, 0))
    return pl.pallas_call(
        _ffn_kernel,
        grid_spec=pltpu.PrefetchScalarGridSpec(
            num_scalar_prefetch=2, grid=(rows // blk,),
            in_specs=[row, row, per_expert(wup), per_expert(bup), per_expert(wdn), per_expert(bdn)],
            out_specs=[row, row]),
        out_shape=[jax.ShapeDtypeStruct((rows, QUARTER), I32), jax.ShapeDtypeStruct((rows, QUARTER), I32)],
        compiler_params=pltpu.CompilerParams(dimension_semantics=("arbitrary",), vmem_limit_bytes=40 * MIB),
        name="moe_ffn",
    )(block_expert, n_used, xs_lo, xs_hi, wup, bup, wdn, bdn)


def _combine_kernel(h1_ref, olo_ref, ohi_ref, gtm_ref, g2_ref, b2_ref, o_ref):
    g = gtm_ref[...]
    y = jnp.zeros(h1_ref.shape, F32)
    for k in range(MOE_TOP_K):
        y = y + g[:, k:k + 1] * _unpack_rows(olo_ref[k], ohi_ref[k])
    o_ref[...] = _layer_norm(DEEPNORM_ALPHA * h1_ref[...] + y, g2_ref[...], b2_ref[...])


def _combine(h1, og_lo, og_hi, gtm, g2, b2, row0):
    n = h1.shape[0]
    n_all = og_lo.shape[0] // MOE_TOP_K
    tm = MERGE_TM
    row = lambda w: pl.BlockSpec((tm, w), lambda i: (i, 0))
    const = lambda a: pl.BlockSpec(a.shape, lambda i: (0,) * a.ndim)
    picked = pl.BlockSpec((MOE_TOP_K, tm, QUARTER), lambda i: (0, i + row0 // tm, 0))
    return pl.pallas_call(
        _combine_kernel,
        grid=(n // tm,),
        in_specs=[row(D_MODEL), picked, picked, row(LANES), const(g2), const(b2)],
        out_specs=row(D_MODEL),
        out_shape=jax.ShapeDtypeStruct((n, D_MODEL), F32),
        compiler_params=pltpu.CompilerParams(dimension_semantics=("parallel",), vmem_limit_bytes=40 * MIB),
        name="moe_combine",
    )(h1, og_lo.reshape(MOE_TOP_K, n_all, QUARTER), og_hi.reshape(MOE_TOP_K, n_all, QUARTER), gtm, g2, b2)


def _moe(groups, totals, wup, bup, wdn, bdn, g2, b2):
    blk = FFN_BLOCK
    sizes = [g[0].shape[0] for g in groups]
    n_all = sum(sizes)
    n_rows = MOE_TOP_K * n_all + N_EXPERTS * blk
    counts = totals[:, 0].astype(I32)
    padded = (counts + blk - 1) // blk * blk
    ends = jnp.cumsum(padded)
    block_start = jnp.arange(n_rows // blk, dtype=I32) * blk
    block_expert = jnp.minimum(jnp.sum((ends[None, :] <= block_start[:, None]).astype(I32), axis=1), N_EXPERTS - 1)
    n_used = (ends[-1:] // blk).astype(I32)
    cat = lambda i, axis: jnp.concatenate([g[i] for g in groups], axis=axis)
    pos = _positions((ends - padded).astype(I32), cat(4, 1), cat(5, 1))
    pos_flat = pos[:MOE_TOP_K].reshape(1, MOE_TOP_K * n_all)
    xs_lo = _sc_scatter(cat(1, 0), pos_flat, n_rows)
    xs_hi = _sc_scatter(cat(2, 0), pos_flat, n_rows)
    ys_lo, ys_hi = _ffn(block_expert, n_used, xs_lo, xs_hi, wup, bup, wdn, bdn)
    og_lo, og_hi = _sc_gather(ys_lo, pos_flat), _sc_gather(ys_hi, pos_flat)
    starts = np.cumsum([0] + sizes[:-1]).tolist()
    return [_combine(g[0], og_lo, og_hi, g[3], g2, b2, r0) for g, r0 in zip(groups, starts)]


def _mixer(x, pos_tab, tab_period, s0, rows_per_stream, ret_rows, caches, weights, cnt0):
    wp, wr, wd, wo, g1, b1, rwt, rb = weights
    n = x.shape[0]
    n_streams = n // rows_per_stream
    rq, rk, rv, rg, aq, ak, av, iq, ikw, gr, ga, od0 = _project(x, wp, pos_tab, tab_period)
    ret, s_new = _retention(rq, rk, rv, rg, s0, rows_per_stream, ret_rows)
    if caches is None:
        per_stream = lambda a: a.reshape(n_streams, rows_per_stream, a.shape[1])
        od = _dsa(aq, iq, ikw, per_stream(ak), per_stream(av), per_stream(ikw), DSA_TQ, od0)
    else:
        od = _dsa_cached(aq, iq, ikw, *caches, ak, av, rows_per_stream)
    h1, h_lo, h_hi, gtm, ek, rk_, totals = _merge(x, ret, od, gr, ga, wr, wd, wo, g1, b1, rwt, rb, cnt0)
    return (h1, h_lo, h_hi, gtm, ek, rk_), totals, (s_new, ak, av, ikw[:, :IDX_DIM])


def kernel(x_prompt, x_sample, state_ret, cache_k, cache_v, cache_idx_k, w_in, w_ret_o, w_dsa_o, w_o,
           ln1_g, ln1_b, router_w, router_b, w_up, b_up, w_down, b_down, ln2_g, ln2_b):
    assert w_in.shape[0] == DEPTH
    batch, seq, _ = x_prompt.shape
    dec_batch, dec_seq, _ = x_sample.shape
    past = cache_k.shape[2]
    assert seq % PROJ_TM == 0 and seq % DSA_TQ == 0 and PROJ_TM % dec_seq == 0

    l = 0
    mixer_w = (_pack_w_in(w_in[l]), w_ret_o[l].astype(BF16), w_dsa_o[l].astype(BF16), w_o[l].astype(BF16),
               ln1_g[l][None, :], ln1_b[l][None, :], router_w[l].T, router_b[l][:, None])
    moe_w = (
        _deinterleave_w_up(w_up[l]),
        b_up[l].reshape(N_EXPERTS, 2 * D_FF // UP_BLOCK, LANES, 2).transpose(0, 1, 3, 2).reshape(N_EXPERTS, 1, 2 * D_FF),
        w_down[l].astype(BF16), b_down[l][:, None, :], ln2_g[l][None, :], ln2_b[l][None, :])

    tab_p = _rot_tables(jnp.arange(seq))
    zeros_state = jnp.zeros((batch, RET_HEADS, RET_DK, RET_DV), F32)
    moe_p, totals_p, (s_p, k_p, v_p, ik_p) = _mixer(
        x_prompt.reshape(batch * seq, D_MODEL), tab_p, seq // PROJ_TM, zeros_state, seq, RET_CHUNK, None, mixer_w,
        jnp.zeros((N_EXPERTS, LANES), F32))

    tab_s = jnp.tile(_rot_tables(past + jnp.arange(dec_seq)), (PROJ_TM // dec_seq, 1))
    caches = (cache_k[l].reshape(dec_batch, past, LANES), cache_v[l].reshape(dec_batch, past, LANES), cache_idx_k[l])
    moe_s, totals, (s_s, k_s, v_s, ik_s) = _mixer(
        x_sample.reshape(dec_batch * dec_seq, D_MODEL), tab_s, 1, state_ret[l], dec_seq, dec_seq, caches, mixer_w,
        totals_p)

    y_p, y_s = _moe([moe_p, moe_s], totals, *moe_w)

    kv = (DSA_KV_HEADS, DSA_HEAD_DIM)
    return (y_p.reshape(batch, seq, D_MODEL), y_s.reshape(dec_batch, dec_seq, D_MODEL),
            s_p[None], k_p.reshape(1, batch, seq, *kv), v_p.reshape(1, batch, seq, *kv),
            ik_p.reshape(1, batch, seq, IDX_DIM),
            s_s[None], k_s.reshape(1, dec_batch, dec_seq, *kv), v_s.reshape(1, dec_batch, dec_seq, *kv),
            ik_s.reshape(1, dec_batch, dec_seq, IDX_DIM))
```

```python
import functools

import numpy as np
import jax
import jax.numpy as jnp
from jax import lax
from jax.experimental import pallas as pl
from jax.experimental.pallas import tpu as pltpu
from jax.experimental.pallas import tpu_sc as plsc

F32 = jnp.float32
BF16 = jnp.bfloat16
I32 = jnp.int32

D_MODEL = 1024
CHUNK = 64
RET_HEADS = 4
RET_DK = 128
RET_DV = 256
RET_ROPE_BASE = 10000.0
DSA_HEADS = 8
DSA_KV_HEADS = 2
DSA_HEAD_DIM = 64
IDX_HEADS = 8
IDX_DIM = 64
DSA_TOPK = 256
ROPE_THETA = 500000.0
N_EXPERTS = 32
MOE_TOP_K = 4
D_FF = 1024
SWIGLU_ALPHA = 1.702
SWIGLU_LIMIT = 7.0
LN_EPS = 1e-5
GN_EPS = 1e-6
DEPTH = 1
DEEPNORM_ALPHA = (2.0 * DEPTH) ** 0.25
PROJ_WIDTHS = (RET_HEADS * RET_DK, RET_HEADS * RET_DK, RET_HEADS * RET_DV, RET_HEADS * RET_DV,
               DSA_HEADS * DSA_HEAD_DIM, DSA_KV_HEADS * DSA_HEAD_DIM, DSA_KV_HEADS * DSA_HEAD_DIM,
               IDX_HEADS * IDX_DIM, IDX_DIM, IDX_HEADS, D_MODEL, D_MODEL)

LANES = 128
MIB = 1024 * 1024

OFF_RQ, OFF_RK, OFF_RV, OFF_RG = 0, 512, 1024, 2048
OFF_AQ, OFF_AK, OFF_AV, OFF_IQ, OFF_IKW = 3072, 3584, 3712, 3840, 4352
OFF_GR, OFF_GA, PACKED_COLS = 4480, 5504, 6528
TAB_COLS = 8 * LANES

PROJ_TM = 512
RET_CHUNK = 128
RET_CHUNKS_PER_STEP = 8
DSA_TQ = 256
DSA_ROWS = 512
DSA_GROUP_MAX = 4
SEARCH_UNROLL = 4
MERGE_TM = 512
FFN_BLOCK = 512
SC_WINDOW = 128
QUARTER = D_MODEL // 4
UP_BLOCK = 2 * LANES


def _nt(a, b):
    return lax.dot_general(a, b, (((1,), (1,)), ((), ())), preferred_element_type=F32)


def _mm(a, b):
    return jnp.dot(a, b, preferred_element_type=F32)


def _proj_kernel(x_ref, w_ref, tab_ref, rq_ref, rk_ref, rv_ref, rg_ref, aq_ref, ak_ref, av_ref,
                 iq_ref, ikw_ref, gr_ref, ga_ref, od0_ref):
    xb = x_ref[...].astype(BF16)
    od0_ref[...] = jnp.zeros(od0_ref.shape, od0_ref.dtype)

    def mm(c0, n):
        return _mm(xb, w_ref[:, c0:c0 + n])

    def tab(i):
        return tab_ref[:, i * LANES:(i + 1) * LANES]

    def rot_full(z):
        return z * tab(0) + pltpu.roll(z, 64, 1) * tab(1)

    def rot_part(z, c):
        return z * tab(c) + pltpu.roll(z, LANES - 8, 1) * tab(c + 1) + pltpu.roll(z, 8, 1) * tab(c + 2)

    z = mm(OFF_RQ, 512)
    for h in range(4):
        sl = slice(h * LANES, (h + 1) * LANES)
        rq_ref[:, sl] = rot_full(z[:, sl]).astype(BF16)
    z = mm(OFF_RK, 512)
    for h in range(4):
        sl = slice(h * LANES, (h + 1) * LANES)
        rk_ref[:, sl] = (rot_full(z[:, sl]) * (RET_DK ** -0.5)).astype(BF16)
    for c in range(2):
        rv_ref[:, c * 512:(c + 1) * 512] = mm(OFF_RV + c * 512, 512).astype(BF16)
    for c in range(2):
        rg_ref[:, c * 512:(c + 1) * 512] = mm(OFF_RG + c * 512, 512)
    z = mm(OFF_AQ, 512)
    for h in range(4):
        sl = slice(h * LANES, (h + 1) * LANES)
        aq_ref[:, sl] = rot_part(z[:, sl], 2).astype(BF16)
    z = mm(OFF_AK, 256)
    ak_ref[...] = rot_part(z[:, :LANES], 2)
    av_ref[...] = z[:, LANES:]
    z = mm(OFF_IQ, 512)
    for h in range(4):
        sl = slice(h * LANES, (h + 1) * LANES)
        iq_ref[:, sl] = rot_part(z[:, sl], 2).astype(BF16)
    ikw_ref[...] = rot_part(mm(OFF_IKW, LANES), 5)
    for c in range(2):
        gr_ref[:, c * 512:(c + 1) * 512] = mm(OFF_GR + c * 512, 512)
    for c in range(2):
        ga_ref[:, c * 512:(c + 1) * 512] = mm(OFF_GA + c * 512, 512)


def _rot_tables(pos):
    p = pos.shape[0]
    posf = pos.astype(F32)[:, None]
    ret_f = RET_ROPE_BASE ** (-jnp.linspace(0.0, 1.0, RET_DK // 2, dtype=F32))
    ang = posf * ret_f[None, :]
    c, s = jnp.cos(ang), jnp.sin(ang)
    cos_r = jnp.concatenate([c, c], 1)
    sin_r = jnp.concatenate([-s, s], 1)
    n_rot = DSA_HEAD_DIM // 4
    att_f = ROPE_THETA ** (-jnp.arange(0, n_rot, 2, dtype=F32) / n_rot)
    ang2 = posf * att_f[None, :]
    c2, s2 = jnp.cos(ang2), jnp.sin(ang2)
    half = n_rot // 2
    rest = DSA_HEAD_DIM - 2 * half
    c64 = jnp.concatenate([c2, c2, jnp.ones((p, rest), F32)], 1)
    s1_64 = jnp.concatenate([-s2, jnp.zeros((p, DSA_HEAD_DIM - half), F32)], 1)
    s2_64 = jnp.concatenate([jnp.zeros((p, half), F32), s2, jnp.zeros((p, rest), F32)], 1)
    z64 = jnp.zeros((p, DSA_HEAD_DIM), F32)
    ci = jnp.concatenate([c64, jnp.full((p, IDX_HEADS), IDX_HEADS ** -0.5, F32),
                          jnp.zeros((p, DSA_HEAD_DIM - IDX_HEADS), F32)], 1)
    return jnp.concatenate([cos_r, sin_r,
                            jnp.concatenate([c64, c64], 1), jnp.concatenate([s1_64, s1_64], 1),
                            jnp.concatenate([s2_64, s2_64], 1),
                            ci, jnp.concatenate([s1_64, z64], 1), jnp.concatenate([s2_64, z64], 1)], 1)


def _pack_w_in(w):
    cuts = np.cumsum(PROJ_WIDTHS)[:-1].tolist()
    rq, rk, rv, rg, aq, ak, av, iq, ik, iw, gr, ga = jnp.split(w, cuts, axis=1)
    pad = jnp.zeros((w.shape[0], LANES - IDX_DIM - IDX_HEADS), w.dtype)
    return jnp.concatenate([rq, rk, rv, rg, aq, ak, av, iq, ik, iw, pad, gr, ga], axis=1).astype(BF16)


def _project(x, wp, tab, tab_period):
    n = x.shape[0]
    tm = PROJ_TM
    row = lambda w: pl.BlockSpec((tm, w), lambda i: (i, 0))
    out_shapes = [((n, 512), BF16), ((n, 512), BF16), ((n, 1024), BF16), ((n, 1024), F32),
                  ((n, 512), BF16), ((n, LANES), F32), ((n, LANES), F32), ((n, 512), BF16),
                  ((n, LANES), F32), ((n, 1024), F32), ((n, 1024), F32), ((n, DSA_HEADS * DSA_HEAD_DIM), BF16)]
    return pl.pallas_call(
        _proj_kernel,
        grid=(n // tm,),
        in_specs=[row(D_MODEL),
                  pl.BlockSpec((D_MODEL, PACKED_COLS), lambda i: (0, 0), pipeline_mode=pl.Buffered(1)),
                  pl.BlockSpec((tm, TAB_COLS), lambda i: (i % tab_period, 0))],
        out_specs=[row(s[1]) for s, _ in out_shapes],
        out_shape=[jax.ShapeDtypeStruct(s, d) for s, d in out_shapes],
        compiler_params=pltpu.CompilerParams(dimension_semantics=("parallel",), vmem_limit_bytes=52 * MIB),
        name="proj",
    )(x, wp, tab)


def _ret_kernel(dec_ref, xi_ref, zeta_ref, rq_ref, rk_ref, rv_ref, rg_ref, s0_ref, ret_ref, sout_ref, st_ref,
                *, rows, n_chunk, g_pow):
    j = pl.program_id(1)
    cpad = RET_CHUNK

    @pl.when(j == 0)
    def _():
        st_ref[...] = s0_ref[0]

    def padded(v):
        if rows == cpad:
            return v
        return jnp.concatenate([v, jnp.zeros((cpad - rows, v.shape[1]), v.dtype)], axis=0)

    for c in range(n_chunk):
        rs = slice(c * rows, (c + 1) * rows)
        for h in range(RET_HEADS):
            ks = slice(h * RET_DK, (h + 1) * RET_DK)
            vs = slice(h * RET_DV, (h + 1) * RET_DV)
            q = padded(rq_ref[rs, ks])
            kt = padded(rk_ref[rs, ks].astype(F32)).T
            v = padded(rv_ref[rs, vs])
            s = st_ref[h]
            sc = _mm(q, kt.astype(BF16)) * dec_ref[h]
            o = _mm(sc.astype(BF16), v) + _mm(q, s.astype(BF16)) * xi_ref[h]
            st_ref[h] = g_pow[h] * s + _mm((kt * zeta_ref[h]).astype(BF16), v)
            o = o[:rows]
            mu = jnp.mean(o, axis=-1, keepdims=True)
            d = o - mu
            var = jnp.mean(d * d, axis=-1, keepdims=True)
            gn = d * lax.rsqrt(var + GN_EPS)
            g = rg_ref[rs, vs]
            ret_ref[rs, vs] = (gn * (g * jax.nn.sigmoid(g))).astype(BF16)

    @pl.when(j == pl.num_programs(1) - 1)
    def _():
        sout_ref[0] = st_ref[...]


def _retention(rq, rk, rv, rg, s0, rows_per_stream, rows):
    n = rq.shape[0]
    n_streams = n // rows_per_stream
    n_chunk = min(rows_per_stream // rows, RET_CHUNKS_PER_STEP)
    blk = rows * n_chunk
    nb = rows_per_stream // blk
    gam = 1.0 - 2.0 ** (-5.0 - np.arange(RET_HEADS, dtype=np.float64))
    i = np.arange(RET_CHUNK, dtype=np.float64)
    diff = i[:, None] - i[None, :]
    dec = np.where(diff >= 0, gam[:, None, None] ** np.maximum(diff, 0.0)[None], 0.0)
    xi = gam[:, None, None] ** (i + 1.0)[None, :, None]
    zeta = np.where(i < rows, gam[:, None, None] ** (rows - 1.0 - i)[None, None, :], 0.0)
    g_pow = tuple(float(g ** rows) for g in gam)
    const = lambda shape: pl.BlockSpec(shape, lambda s, j: (0,) * len(shape))
    row = lambda w: pl.BlockSpec((blk, w), lambda s, j: (s * nb + j, 0))
    st = pl.BlockSpec((1, RET_HEADS, RET_DK, RET_DV), lambda s, j: (s, 0, 0, 0))
    return pl.pallas_call(
        functools.partial(_ret_kernel, rows=rows, n_chunk=n_chunk, g_pow=g_pow),
        grid=(n_streams, nb),
        in_specs=[const((RET_HEADS, RET_CHUNK, RET_CHUNK)), const((RET_HEADS, RET_CHUNK, 1)),
                  const((RET_HEADS, 1, RET_CHUNK)), row(512), row(512), row(1024), row(1024), st],
        out_specs=[row(1024), st],
        out_shape=[jax.ShapeDtypeStruct((n, RET_HEADS * RET_DV), BF16),
                   jax.ShapeDtypeStruct((n_streams, RET_HEADS, RET_DK, RET_DV), F32)],
        scratch_shapes=[pltpu.VMEM((RET_HEADS, RET_DK, RET_DV), F32)],
        compiler_params=pltpu.CompilerParams(dimension_semantics=("parallel", "arbitrary"),
                                             vmem_limit_bytes=32 * MIB),
        name="retention",
    )(jnp.asarray(dec, F32), jnp.asarray(xi, F32), jnp.asarray(zeta, F32), rq, rk, rv, rg, s0)


def _dsa_kernel(aq_ref, iq_ref, ikwq_ref, kk_ref, vv_ref, ikk_ref, *rest,
                group, tq, n_keys, row0, limit_const, n_sel):
    o_ref, w_ref, bias_ref = rest[-3:]
    rows = group * tq
    nsel_f = float(n_sel)
    neg_inf = -jnp.inf

    col = lax.broadcasted_iota(I32, (tq, n_keys), 1)
    if limit_const is None:
        limit = (jnp.right_shift(lax.broadcasted_iota(I32, (tq, 1), 0) + row0, 6) + 1) * CHUNK
    else:
        limit = limit_const

    def score_stream(s, carry):
        ikb = ikk_ref[s][:, :IDX_DIM].astype(BF16)
        iww = ikwq_ref[s][:, IDX_DIM:IDX_DIM + IDX_HEADS] * (IDX_DIM ** -0.5)
        acc = jnp.zeros((tq, n_keys), F32)
        for p in range(IDX_HEADS // 2):
            slab = iq_ref[s, :, p * LANES:(p + 1) * LANES]
            for hh in range(2):
                h = 2 * p + hh
                acc = acc + jnp.maximum(_nt(slab[:, hh * IDX_DIM:(hh + 1) * IDX_DIM], ikb), 0.0) * iww[:, h:h + 1]
        w_ref[pl.ds(pl.multiple_of(s * tq, tq), tq), :] = jnp.where(col < limit, acc, neg_inf)
        return carry

    lax.fori_loop(0, group, score_stream, 0)

    sc = w_ref[...]
    pos = jnp.sum(jnp.where(sc >= 0.0, 1.0, 0.0), axis=1, keepdims=True) >= nsel_f
    kk = jnp.where(pos, nsel_f, float(n_keys - n_sel + 1))
    w_ref[...] = jnp.where(pos, sc, -sc)

    def bit_step(i, u):
        cand_u = u | jnp.left_shift(jnp.int32(1), 30 - i)
        cand = pltpu.bitcast(cand_u, F32)
        cnt = jnp.sum(jnp.where(w_ref[...] >= cand, 1.0, 0.0), axis=1, keepdims=True)
        return jnp.where(cnt >= kk, cand_u, u)

    mag_u = lax.fori_loop(0, 31, bit_step, jnp.zeros((rows, 1), I32), unroll=SEARCH_UNROLL)
    mag = pltpu.bitcast(mag_u, F32)
    thr = jnp.where(pos, mag, -mag)

    sc = jnp.where(pos, w_ref[...], -w_ref[...])
    short = jnp.sum(jnp.where(sc >= thr, 1.0, 0.0), axis=1, keepdims=True) < nsel_f
    thr = jnp.where(jnp.logical_and(short, jnp.logical_not(pos)), -pltpu.bitcast(mag_u + 1, F32), thr)
    ge = sc >= thr
    cnt_gt = jnp.sum(jnp.where(sc > thr, 1.0, 0.0), axis=1, keepdims=True)
    cnt_ge = jnp.sum(jnp.where(ge, 1.0, 0.0), axis=1, keepdims=True)
    bias_ref[...] = jnp.where(jnp.logical_and(ge, sc > neg_inf), 0.0, neg_inf)
    excess = jnp.logical_and(cnt_ge > nsel_f, thr > neg_inf)

    @pl.when(jnp.max(jnp.where(excess, 1.0, 0.0)) > 0.0)
    def _():
        need = nsel_f - cnt_gt
        tri = jnp.where(lax.broadcasted_iota(I32, (LANES, LANES), 0) < lax.broadcasted_iota(I32, (LANES, LANES), 1),
                        1.0, 0.0).astype(BF16)
        before = jnp.zeros((rows, 1), F32)
        for b in range(n_keys // LANES):
            sl = slice(b * LANES, (b + 1) * LANES)
            sblk = jnp.where(pos, w_ref[:, sl], -w_ref[:, sl])
            eq = jnp.where(sblk == thr, 1.0, 0.0)
            rank = _mm(eq.astype(BF16), tri) + before
            keep = jnp.logical_or(sblk > thr, jnp.logical_and(sblk == thr, rank < need))
            bias_ref[:, sl] = jnp.where(jnp.logical_and(keep, sblk > neg_inf), 0.0, neg_inf)
            before = before + jnp.sum(eq, axis=1, keepdims=True)

    def attend_stream(s, carry):
        bias = bias_ref[pl.ds(pl.multiple_of(s * tq, tq), tq), :]
        kfull = kk_ref[s]
        vfull = vv_ref[s]
        for g in range(DSA_KV_HEADS):
            gs = slice(g * DSA_HEAD_DIM, (g + 1) * DSA_HEAD_DIM)
            kg = kfull[:, gs].astype(BF16)
            vg = vfull[:, gs].astype(BF16)
            for pp in range(2):
                p = 2 * g + pp
                slab = (aq_ref[s, :, p * LANES:(p + 1) * LANES].astype(F32) * (DSA_HEAD_DIM ** -0.5)).astype(BF16)
                outs = []
                for hh in range(2):
                    lg = _nt(slab[:, hh * DSA_HEAD_DIM:(hh + 1) * DSA_HEAD_DIM], kg) + bias
                    m = jnp.max(lg, axis=1, keepdims=True)
                    pr = jnp.exp(lg - m)
                    den = jnp.sum(pr, axis=1, keepdims=True)
                    outs.append(_mm(pr.astype(BF16), vg) / den)
                o_ref[s, :, p * LANES:(p + 1) * LANES] = jnp.concatenate(outs, axis=1).astype(BF16)
        return carry

    lax.fori_loop(0, group, attend_stream, 0)


def _dsa_call(aq, iq, ikw, kk, vv, ikk, group, tq, jq, n_keys, limit_const, n_sel, name, prev=None):
    n_streams, rows_per_stream = aq.shape[0], aq.shape[1]
    extra_specs, extra_args, aliases = [], (), {}
    if prev is not None:
        extra_specs, extra_args, aliases = [pl.BlockSpec(memory_space=pl.ANY)], (prev,), {6: 0}
    qrow = lambda a: pl.BlockSpec((group, tq, a.shape[2]), lambda s: (s, jq, 0))
    krow = lambda a: pl.BlockSpec((group, n_keys, a.shape[2]), lambda s: (s, 0, 0))
    width = DSA_HEADS * DSA_HEAD_DIM
    return pl.pallas_call(
        functools.partial(_dsa_kernel, group=group, tq=tq, n_keys=n_keys, row0=jq * tq, limit_const=limit_const,
                          n_sel=n_sel),
        grid=(n_streams // group,),
        in_specs=[qrow(aq), qrow(iq), qrow(ikw), krow(kk), krow(vv), krow(ikk)] + extra_specs,
        out_specs=pl.BlockSpec((group, tq, width), lambda s: (s, jq, 0)),
        out_shape=jax.ShapeDtypeStruct((n_streams, rows_per_stream, width), BF16),
        input_output_aliases=aliases,
        scratch_shapes=[pltpu.VMEM((group * tq, n_keys), F32), pltpu.VMEM((group * tq, n_keys), F32)],
        compiler_params=pltpu.CompilerParams(dimension_semantics=("parallel",), vmem_limit_bytes=56 * MIB),
        name=name,
    )(aq, iq, ikw, kk, vv, ikk, *extra_args)


def _dsa_cached_kernel(aq_ref, iq_ref, ikwq_ref, ck_ref, cv_ref, cik_ref, nk_ref, nv_ref, o_ref,
                       w_ref, bias_ref, kk_ref, vv_ref, ikk_ref, **static):
    past, new = ck_ref.shape[1], nk_ref.shape[1]
    for dst, cache, fresh in ((kk_ref, ck_ref, nk_ref), (vv_ref, cv_ref, nv_ref), (ikk_ref, cik_ref, ikwq_ref)):
        dst[:, :past, :cache.shape[2]] = cache[...]
        dst[:, past:past + new, :] = fresh[...]
        dst[:, past + new:, :] = jnp.zeros((dst.shape[0], dst.shape[1] - past - new, dst.shape[2]), F32)
    _dsa_kernel(aq_ref, iq_ref, ikwq_ref, kk_ref, vv_ref, ikk_ref, o_ref, w_ref, bias_ref, **static)


def _dsa_cached(aq, iq, ikw, cache_k, cache_v, cache_ik, new_k, new_v, tq):
    n_streams, past, _ = cache_k.shape
    n = aq.shape[0]
    per_stream = lambda a: a.reshape(n_streams, tq, a.shape[1])
    aq, iq, ikw, new_k, new_v = per_stream(aq), per_stream(iq), per_stream(ikw), per_stream(new_k), per_stream(new_v)
    limit = past + tq
    n_keys = -(-limit // LANES) * LANES
    group = max(1, min(n_streams, DSA_ROWS // tq, DSA_GROUP_MAX))
    blk = lambda a: pl.BlockSpec((group,) + a.shape[1:], lambda s: (s, 0, 0))
    width = DSA_HEADS * DSA_HEAD_DIM
    return pl.pallas_call(
        functools.partial(_dsa_cached_kernel, group=group, tq=tq, n_keys=n_keys, row0=0, limit_const=limit,
                          n_sel=min(DSA_TOPK, limit // 4)),
        grid=(n_streams // group,),
        in_specs=[blk(a) for a in (aq, iq, ikw, cache_k, cache_v, cache_ik, new_k, new_v)],
        out_specs=pl.BlockSpec((group, tq, width), lambda s: (s, 0, 0)),
        out_shape=jax.ShapeDtypeStruct((n_streams, tq, width), BF16),
        scratch_shapes=[pltpu.VMEM((group * tq, n_keys), F32), pltpu.VMEM((group * tq, n_keys), F32)] +
                       [pltpu.VMEM((group, n_keys, LANES), F32)] * 3,
        compiler_params=pltpu.CompilerParams(dimension_semantics=("parallel",), vmem_limit_bytes=48 * MIB),
        name="dsa_s",
    )(aq, iq, ikw, cache_k, cache_v, cache_ik, new_k, new_v).reshape(n, width)


def _dsa(aq, iq, ikw, kk, vv, ikk, tq, out):
    n_streams, n_keys, _ = kk.shape
    n = aq.shape[0]
    per_stream = lambda a: a.reshape(n_streams, n // n_streams, a.shape[1])
    aq, iq, ikw = per_stream(aq), per_stream(iq), per_stream(ikw)
    nq = n // n_streams // tq
    n_sel = min(DSA_TOPK, n_keys // 4)
    group = max(1, min(n_streams, DSA_ROWS // tq, DSA_GROUP_MAX))
    out = per_stream(out)
    for jq in range(nq):
        out = _dsa_call(aq, iq, ikw, kk, vv, ikk, group, tq, jq, (jq + 1) * tq, None, n_sel, f"dsa_p{jq}", prev=out)
    return out.reshape(n, -1)


def _pack_rows(v):
    q = QUARTER
    bits = lambda x: pltpu.bitcast(x.astype(BF16).astype(F32), I32)
    pair = lambda c: lax.shift_right_logical(bits(v[:, c * q:(c + 1) * q]), 16) | bits(v[:, (c + 1) * q:(c + 2) * q])
    return pair(0), pair(2)


def _unpack_rows(lo, hi):
    parts = []
    for w in (lo, hi):
        parts.append(pltpu.bitcast(lax.shift_left(w, 16), F32))
        parts.append(pltpu.bitcast(w & jnp.int32(-65536), F32))
    return jnp.concatenate(parts, axis=1)


def _layer_norm(v, g, b):
    mu = jnp.mean(v, axis=-1, keepdims=True)
    d = v - mu
    var = jnp.mean(d * d, axis=-1, keepdims=True)
    return d * lax.rsqrt(var + LN_EPS) * g + b


def _merge_kernel(x_ref, ret_ref, od_ref, gr_ref, ga_ref, wr_ref, wd_ref, wo_ref, g1_ref, b1_ref, rwt_ref, rb_ref,
                  tri_ref, cnt0_ref, h1_ref, hlo_ref, hhi_ref, gtm_ref, ek_ref, rk_ref, tot_ref, cnt_ref):
    @pl.when(pl.program_id(0) == 0)
    def _():
        cnt_ref[...] = cnt0_ref[:, 0:1]

    y_ret = _mm(ret_ref[...], wr_ref[...])
    y_dsa = _mm(od_ref[...], wd_ref[...])
    merged = jax.nn.sigmoid(gr_ref[...]) * y_ret + jax.nn.sigmoid(ga_ref[...]) * y_dsa
    mix = _mm(merged.astype(BF16), wo_ref[...])
    h1 = _layer_norm(DEEPNORM_ALPHA * x_ref[...] + mix, g1_ref[...], b1_ref[...])
    h1_ref[...] = h1
    hlo_ref[...], hhi_ref[...] = _pack_rows(h1)

    logits = lax.dot_general(rwt_ref[...], h1, (((1,), (1,)), ((), ())), preferred_element_type=F32,
                             precision=lax.Precision.HIGHEST) + rb_ref[...]
    tm = logits.shape[1]
    e_iota = lax.broadcasted_iota(I32, (N_EXPERTS, tm), 0)
    tops, hots, firsts = [], [], []
    for _ in range(MOE_TOP_K):
        m = jnp.max(logits, axis=0, keepdims=True)
        first = jnp.min(jnp.where(logits == m, e_iota, N_EXPERTS), axis=0, keepdims=True)
        hot = e_iota == first
        tops.append(m)
        hots.append(hot)
        firsts.append(first)
        logits = jnp.where(hot, -jnp.inf, logits)
    exps = [jnp.exp(m - tops[0]) for m in tops]
    den = exps[0] + exps[1] + exps[2] + exps[3]
    sel = jnp.zeros((N_EXPERTS, tm), F32)
    for hot in hots:
        sel = sel + jnp.where(hot, 1.0, 0.0)
    rank = _mm(sel.astype(BF16), tri_ref[...]) + cnt_ref[...]
    ranks = [jnp.sum(jnp.where(hot, rank, 0.0), axis=0, keepdims=True).astype(I32) for hot in hots]
    pad_i = jnp.zeros((8 - MOE_TOP_K, tm), I32)
    ek_ref[...] = jnp.concatenate(firsts + [pad_i], axis=0)
    rk_ref[...] = jnp.concatenate(ranks + [pad_i], axis=0)
    gates = jnp.concatenate([e / den for e in exps] + [jnp.zeros((LANES - MOE_TOP_K, tm), F32)], axis=0)
    gtm_ref[...] = gates.T
    cnt_ref[...] = cnt_ref[...] + jnp.sum(sel, axis=1, keepdims=True)
    tot_ref[...] = jnp.broadcast_to(cnt_ref[...], tot_ref.shape)


def _merge(x, ret, od, gr, ga, wr, wd, wo, g1, b1, rwt, rb, cnt0):
    n = x.shape[0]
    tm = MERGE_TM
    row = lambda w: pl.BlockSpec((tm, w), lambda i: (i, 0))
    const = lambda a: pl.BlockSpec(a.shape, lambda i: (0,) * a.ndim)
    col = pl.BlockSpec((8, tm), lambda i: (0, i))
    tri = jnp.asarray(np.triu(np.ones((tm, tm), np.float32), 1), BF16)
    return pl.pallas_call(
        _merge_kernel,
        grid=(n // tm,),
        in_specs=[row(D_MODEL), row(1024), row(512), row(1024), row(1024), const(wr), const(wd), const(wo),
                  const(g1), const(b1), const(rwt), const(rb), const(tri), const(cnt0)],
        out_specs=[row(D_MODEL), row(QUARTER), row(QUARTER), row(LANES), col, col,
                   pl.BlockSpec((N_EXPERTS, LANES), lambda i: (0, 0))],
        out_shape=[jax.ShapeDtypeStruct((n, D_MODEL), F32), jax.ShapeDtypeStruct((n, QUARTER), I32),
                   jax.ShapeDtypeStruct((n, QUARTER), I32), jax.ShapeDtypeStruct((n, LANES), F32),
                   jax.ShapeDtypeStruct((8, n), I32), jax.ShapeDtypeStruct((8, n), I32),
                   jax.ShapeDtypeStruct((N_EXPERTS, LANES), F32)],
        scratch_shapes=[pltpu.VMEM((N_EXPERTS, 1), F32)],
        compiler_params=pltpu.CompilerParams(dimension_semantics=("arbitrary",), vmem_limit_bytes=48 * MIB),
        name="merge",
    )(x, ret, od, gr, ga, wr, wd, wo, g1, b1, rwt, rb, tri, cnt0)


def _deinterleave_kernel(w_ref, o_ref):
    r = lax.broadcasted_iota(I32, (UP_BLOCK, UP_BLOCK), 0)
    c = lax.broadcasted_iota(I32, (UP_BLOCK, UP_BLOCK), 1)
    src = jnp.where(c < LANES, 2 * c, 2 * (c - LANES) + 1)
    perm = jnp.where(r == src, 1.0, 0.0).astype(BF16)
    for b in range(w_ref.shape[2] // UP_BLOCK):
        sl = slice(b * UP_BLOCK, (b + 1) * UP_BLOCK)
        o_ref[0, :, sl] = _mm(w_ref[0, :, sl].astype(BF16), perm).astype(BF16)


def _deinterleave_w_up(w_up):
    n_e, d_in, d_out = w_up.shape
    cols = d_out
    spec = pl.BlockSpec((1, d_in, cols), lambda e, c: (e, 0, c))
    return pl.pallas_call(
        _deinterleave_kernel,
        grid=(n_e, d_out // cols),
        in_specs=[spec],
        out_specs=spec,
        out_shape=jax.ShapeDtypeStruct(w_up.shape, BF16),
        compiler_params=pltpu.CompilerParams(dimension_semantics=("parallel", "parallel"),
                                             vmem_limit_bytes=40 * MIB),
        name="w_up_prep",
    )(w_up)


def _pos_kernel(off_ref, ek_ref, rk_ref, pos_ref):
    ek = ek_ref[...]
    pos = rk_ref[...]
    for e in range(N_EXPERTS):
        pos = pos + jnp.where(ek == e, off_ref[e], 0)
    pos_ref[...] = pos


def _positions(off, ek, rk):
    n = ek.shape[1]
    tn = min(n, 2048)
    spec = pl.BlockSpec((8, tn), lambda i, off: (0, i))
    return pl.pallas_call(
        _pos_kernel,
        grid_spec=pltpu.PrefetchScalarGridSpec(num_scalar_prefetch=1, grid=(n // tn,), in_specs=[spec, spec],
                                               out_specs=spec),
        out_shape=jax.ShapeDtypeStruct((8, n), I32),
        name="moe_pos",
    )(off, ek, rk)


def _sc_mesh():
    return plsc.VectorSubcoreMesh(core_axis_name="core", subcore_axis_name="subcore")


def _sc_scatter(xs, pos_flat, n_rows):
    width, dtype = xs[0].shape[1], xs[0].dtype
    windows = [x.shape[0] // SC_WINDOW for x in xs]
    nw = sum(windows)

    @functools.partial(pl.kernel, out_type=jax.ShapeDtypeStruct((n_rows, width), dtype), mesh=_sc_mesh())
    def scatter(*refs):
        p_hbm, o_hbm = refs[len(xs)], refs[len(xs) + 1]

        def body(x_vmem, *p_vmem):
            for p in p_vmem:
                pltpu.sync_copy(x_vmem, o_hbm.at[p.at[0]])

        first = 0
        for x_hbm, nw_g in zip(refs[:len(xs)], windows):
            pltpu.emit_pipeline(
                body, grid=(nw_g,),
                in_specs=[pl.BlockSpec((SC_WINDOW, width), lambda i: (i, 0))] +
                         [pl.BlockSpec((1, SC_WINDOW), functools.partial(lambda w0, i: (0, w0 + i), k * nw + first))
                          for k in range(MOE_TOP_K)],
                out_specs=[], core_axis_name=("core", "subcore"), dimension_semantics=(pltpu.PARALLEL,),
            )(x_hbm, *([p_hbm] * MOE_TOP_K))
            first += nw_g

    return scatter(*xs, pos_flat)


def _sc_gather(y, pos_flat):
    m = pos_flat.shape[1]

    @functools.partial(pl.kernel, out_type=jax.ShapeDtypeStruct((m, y.shape[1]), y.dtype), mesh=_sc_mesh())
    def gather(y_hbm, p_hbm, o_hbm):
        def body(p_vmem, o_vmem):
            pltpu.sync_copy(y_hbm.at[p_vmem.at[0]], o_vmem)

        pltpu.emit_pipeline(
            body, grid=(m // SC_WINDOW,),
            in_specs=[pl.BlockSpec((1, SC_WINDOW), lambda i: (0, i))],
            out_specs=[pl.BlockSpec((SC_WINDOW, y.shape[1]), lambda i: (i, 0))],
            core_axis_name=("core", "subcore"), dimension_semantics=(pltpu.PARALLEL,),
        )(p_hbm, o_hbm)

    return gather(y, pos_flat)


def _ffn_kernel(be_ref, nu_ref, xlo_ref, xhi_ref, wup_ref, bup_ref, wdn_ref, bdn_ref, ylo_ref, yhi_ref):
    @pl.when(pl.program_id(0) < nu_ref[0])
    def _():
        x = _unpack_rows(xlo_ref[...], xhi_ref[...]).astype(BF16)
        h = _mm(x, wup_ref[0]) + bup_ref[0]
        acts = []
        for b in range(2 * D_FF // UP_BLOCK):
            glu = jnp.minimum(h[:, b * UP_BLOCK:b * UP_BLOCK + LANES], SWIGLU_LIMIT)
            lin = jnp.clip(h[:, b * UP_BLOCK + LANES:(b + 1) * UP_BLOCK], -SWIGLU_LIMIT, SWIGLU_LIMIT)
            acts.append(glu * jax.nn.sigmoid(SWIGLU_ALPHA * glu) * (lin + 1.0))
        act = jnp.concatenate(acts, axis=1)
        ylo_ref[...], yhi_ref[...] = _pack_rows(_mm(act.astype(BF16), wdn_ref[0]) + bdn_ref[0])


def _ffn(block_expert, n_used, xs_lo, xs_hi, wup, bup, wdn, bdn):
    rows = xs_lo.shape[0]
    blk = FFN_BLOCK
    row = pl.BlockSpec((blk, QUARTER), lambda i, be, nu: (i, 0))
    per_expert = lambda a: pl.BlockSpec((1,) + a.shape[1:], lambda i, be, nu: (be[i], 0, 0))
    return pl.pallas_call(
        _ffn_kernel,
        grid_spec=pltpu.PrefetchScalarGridSpec(
            num_scalar_prefetch=2, grid=(rows // blk,),
            in_specs=[row, row, per_expert(wup), per_expert(bup), per_expert(wdn), per_expert(bdn)],
            out_specs=[row, row]),
        out_shape=[jax.ShapeDtypeStruct((rows, QUARTER), I32), jax.ShapeDtypeStruct((rows, QUARTER), I32)],
        compiler_params=pltpu.CompilerParams(dimension_semantics=("arbitrary",), vmem_limit_bytes=40 * MIB),
        name="moe_ffn",
    )(block_expert, n_used, xs_lo, xs_hi, wup, bup, wdn, bdn)


def _combine_kernel(h1_ref, olo_ref, ohi_ref, gtm_ref, g2_ref, b2_ref, o_ref):
    g = gtm_ref[...]
    y = jnp.zeros(h1_ref.shape, F32)
    for k in range(MOE_TOP_K):
        y = y + g[:, k:k + 1] * _unpack_rows(olo_ref[k], ohi_ref[k])
    o_ref[...] = _layer_norm(DEEPNORM_ALPHA * h1_ref[...] + y, g2_ref[...], b2_ref[...])


def _combine(h1, og_lo, og_hi, gtm, g2, b2, row0):
    n = h1.shape[0]
    n_all = og_lo.shape[0] // MOE_TOP_K
    tm = MERGE_TM
    row = lambda w: pl.BlockSpec((tm, w), lambda i: (i, 0))
    const = lambda a: pl.BlockSpec(a.shape, lambda i: (0,) * a.ndim)
    picked = pl.BlockSpec((MOE_TOP_K, tm, QUARTER), lambda i: (0, i + row0 // tm, 0))
    return pl.pallas_call(
        _combine_kernel,
        grid=(n // tm,),
        in_specs=[row(D_MODEL), picked, picked, row(LANES), const(g2), const(b2)],
        out_specs=row(D_MODEL),
        out_shape=jax.ShapeDtypeStruct((n, D_MODEL), F32),
        compiler_params=pltpu.CompilerParams(dimension_semantics=("parallel",), vmem_limit_bytes=40 * MIB),
        name="moe_combine",
    )(h1, og_lo.reshape(MOE_TOP_K, n_all, QUARTER), og_hi.reshape(MOE_TOP_K, n_all, QUARTER), gtm, g2, b2)


def _moe(groups, totals, wup, bup, wdn, bdn, g2, b2):
    blk = FFN_BLOCK
    sizes = [g[0].shape[0] for g in groups]
    n_all = sum(sizes)
    n_rows = MOE_TOP_K * n_all + N_EXPERTS * blk
    counts = totals[:, 0].astype(I32)
    padded = (counts + blk - 1) // blk * blk
    ends = jnp.cumsum(padded)
    block_start = jnp.arange(n_rows // blk, dtype=I32) * blk
    block_expert = jnp.minimum(jnp.sum((ends[None, :] <= block_start[:, None]).astype(I32), axis=1), N_EXPERTS - 1)
    n_used = (ends[-1:] // blk).astype(I32)
    cat = lambda i, axis: jnp.concatenate([g[i] for g in groups], axis=axis)
    pos = _positions((ends - padded).astype(I32), cat(4, 1), cat(5, 1))
    pos_flat = pos[:MOE_TOP_K].reshape(1, MOE_TOP_K * n_all)
    xs_lo = _sc_scatter([g[1] for g in groups], pos_flat, n_rows)
    xs_hi = _sc_scatter([g[2] for g in groups], pos_flat, n_rows)
    ys_lo, ys_hi = _ffn(block_expert, n_used, xs_lo, xs_hi, wup, bup, wdn, bdn)
    og_lo, og_hi = _sc_gather(ys_lo, pos_flat), _sc_gather(ys_hi, pos_flat)
    starts = np.cumsum([0] + sizes[:-1]).tolist()
    return [_combine(g[0], og_lo, og_hi, g[3], g2, b2, r0) for g, r0 in zip(groups, starts)]


def _mixer(x, pos_tab, tab_period, s0, rows_per_stream, ret_rows, caches, weights, cnt0):
    wp, wr, wd, wo, g1, b1, rwt, rb = weights
    n = x.shape[0]
    n_streams = n // rows_per_stream
    rq, rk, rv, rg, aq, ak, av, iq, ikw, gr, ga, od0 = _project(x, wp, pos_tab, tab_period)
    ret, s_new = _retention(rq, rk, rv, rg, s0, rows_per_stream, ret_rows)
    if caches is None:
        per_stream = lambda a: a.reshape(n_streams, rows_per_stream, a.shape[1])
        od = _dsa(aq, iq, ikw, per_stream(ak), per_stream(av), per_stream(ikw), DSA_TQ, od0)
    else:
        od = _dsa_cached(aq, iq, ikw, *caches, ak, av, rows_per_stream)
    h1, h_lo, h_hi, gtm, ek, rk_, totals = _merge(x, ret, od, gr, ga, wr, wd, wo, g1, b1, rwt, rb, cnt0)
    return (h1, h_lo, h_hi, gtm, ek, rk_), totals, (s_new, ak, av, ikw[:, :IDX_DIM])


def kernel(x_prompt, x_sample, state_ret, cache_k, cache_v, cache_idx_k, w_in, w_ret_o, w_dsa_o, w_o,
           ln1_g, ln1_b, router_w, router_b, w_up, b_up, w_down, b_down, ln2_g, ln2_b):
    assert w_in.shape[0] == DEPTH
    batch, seq, _ = x_prompt.shape
    dec_batch, dec_seq, _ = x_sample.shape
    past = cache_k.shape[2]
    assert seq % PROJ_TM == 0 and seq % DSA_TQ == 0 and PROJ_TM % dec_seq == 0

    l = 0
    mixer_w = (_pack_w_in(w_in[l]), w_ret_o[l].astype(BF16), w_dsa_o[l].astype(BF16), w_o[l].astype(BF16),
               ln1_g[l][None, :], ln1_b[l][None, :], router_w[l].T, router_b[l][:, None])
    moe_w = (
        _deinterleave_w_up(w_up[l]),
        b_up[l].reshape(N_EXPERTS, 2 * D_FF // UP_BLOCK, LANES, 2).transpose(0, 1, 3, 2).reshape(N_EXPERTS, 1, 2 * D_FF),
        w_down[l].astype(BF16), b_down[l][:, None, :], ln2_g[l][None, :], ln2_b[l][None, :])

    tab_p = _rot_tables(jnp.arange(seq))
    zeros_state = jnp.zeros((batch, RET_HEADS, RET_DK, RET_DV), F32)
    moe_p, totals_p, (s_p, k_p, v_p, ik_p) = _mixer(
        x_prompt.reshape(batch * seq, D_MODEL), tab_p, seq // PROJ_TM, zeros_state, seq, RET_CHUNK, None, mixer_w,
        jnp.zeros((N_EXPERTS, LANES), F32))

    tab_s = jnp.tile(_rot_tables(past + jnp.arange(dec_seq)), (PROJ_TM // dec_seq, 1))
    caches = (cache_k[l].reshape(dec_batch, past, LANES), cache_v[l].reshape(dec_batch, past, LANES), cache_idx_k[l])
    moe_s, totals, (s_s, k_s, v_s, ik_s) = _mixer(
        x_sample.reshape(dec_batch * dec_seq, D_MODEL), tab_s, 1, state_ret[l], dec_seq, dec_seq, caches, mixer_w,
        totals_p)

    y_p, y_s = _moe([moe_p, moe_s], totals, *moe_w)

    kv = (DSA_KV_HEADS, DSA_HEAD_DIM)
    return (y_p.reshape(batch, seq, D_MODEL), y_s.reshape(dec_batch, dec_seq, D_MODEL),
            s_p[None], k_p.reshape(1, batch, seq, *kv), v_p.reshape(1, batch, seq, *kv),
            ik_p.reshape(1, batch, seq, IDX_DIM),
            s_s[None], k_s.reshape(1, dec_batch, dec_seq, *kv), v_s.reshape(1, dec_batch, dec_seq, *kv),
            ik_s.reshape(1, dec_batch, dec_seq, IDX_DIM))
```

```python
import functools

import numpy as np
import jax
import jax.numpy as jnp
from jax import lax
from jax.experimental import pallas as pl
from jax.experimental.pallas import tpu as pltpu
from jax.experimental.pallas import tpu_sc as plsc

F32 = jnp.float32
BF16 = jnp.bfloat16
I32 = jnp.int32

D_MODEL = 1024
CHUNK = 64
RET_HEADS = 4
RET_DK = 128
RET_DV = 256
RET_ROPE_BASE = 10000.0
DSA_HEADS = 8
DSA_KV_HEADS = 2
DSA_HEAD_DIM = 64
IDX_HEADS = 8
IDX_DIM = 64
DSA_TOPK = 256
ROPE_THETA = 500000.0
N_EXPERTS = 32
MOE_TOP_K = 4
D_FF = 1024
SWIGLU_ALPHA = 1.702
SWIGLU_LIMIT = 7.0
LN_EPS = 1e-5
GN_EPS = 1e-6
DEPTH = 1
DEEPNORM_ALPHA = (2.0 * DEPTH) ** 0.25
PROJ_WIDTHS = (RET_HEADS * RET_DK, RET_HEADS * RET_DK, RET_HEADS * RET_DV, RET_HEADS * RET_DV,
               DSA_HEADS * DSA_HEAD_DIM, DSA_KV_HEADS * DSA_HEAD_DIM, DSA_KV_HEADS * DSA_HEAD_DIM,
               IDX_HEADS * IDX_DIM, IDX_DIM, IDX_HEADS, D_MODEL, D_MODEL)

LANES = 128
MIB = 1024 * 1024

OFF_RQ, OFF_RK, OFF_RV, OFF_RG = 0, 512, 1024, 2048
OFF_AQ, OFF_AK, OFF_AV, OFF_IQ, OFF_IKW = 3072, 3584, 3712, 3840, 4352
OFF_GR, OFF_GA, PACKED_COLS = 4480, 5504, 6528
TAB_COLS = 8 * LANES

PROJ_TM = 512
RET_CHUNK = 128
RET_CHUNKS_PER_STEP = 8
DSA_TQ = 256
DSA_ROWS = 512
DSA_GROUP_MAX = 4
DSA_CELLS = 1024 * 1024
SEARCH_UNROLL = 4
MERGE_TM = 512
FFN_BLOCK = 512
SC_WINDOW = 128
QUARTER = D_MODEL // 4
UP_BLOCK = 2 * LANES


def _nt(a, b):
    return lax.dot_general(a, b, (((1,), (1,)), ((), ())), preferred_element_type=F32)


def _mm(a, b):
    return jnp.dot(a, b, preferred_element_type=F32)


def _proj_kernel(x_ref, w_ref, tab_ref, rq_ref, rk_ref, rv_ref, rg_ref, aq_ref, ak_ref, av_ref,
                 iq_ref, ikw_ref, gr_ref, ga_ref, od0_ref):
    xb = x_ref[...].astype(BF16)
    od0_ref[...] = jnp.zeros(od0_ref.shape, od0_ref.dtype)

    def mm(c0, n):
        return _mm(xb, w_ref[:, c0:c0 + n])

    def tab(i):
        return tab_ref[:, i * LANES:(i + 1) * LANES]

    def rot_full(z):
        return z * tab(0) + pltpu.roll(z, 64, 1) * tab(1)

    def rot_part(z, c):
        return z * tab(c) + pltpu.roll(z, LANES - 8, 1) * tab(c + 1) + pltpu.roll(z, 8, 1) * tab(c + 2)

    z = mm(OFF_RQ, 512)
    for h in range(4):
        sl = slice(h * LANES, (h + 1) * LANES)
        rq_ref[:, sl] = rot_full(z[:, sl]).astype(BF16)
    z = mm(OFF_RK, 512)
    for h in range(4):
        sl = slice(h * LANES, (h + 1) * LANES)
        rk_ref[:, sl] = (rot_full(z[:, sl]) * (RET_DK ** -0.5)).astype(BF16)
    for c in range(2):
        rv_ref[:, c * 512:(c + 1) * 512] = mm(OFF_RV + c * 512, 512).astype(BF16)
    for c in range(2):
        rg_ref[:, c * 512:(c + 1) * 512] = mm(OFF_RG + c * 512, 512)
    z = mm(OFF_AQ, 512)
    for h in range(4):
        sl = slice(h * LANES, (h + 1) * LANES)
        aq_ref[:, sl] = rot_part(z[:, sl], 2).astype(BF16)
    z = mm(OFF_AK, 256)
    ak_ref[...] = rot_part(z[:, :LANES], 2)
    av_ref[...] = z[:, LANES:]
    z = mm(OFF_IQ, 512)
    for h in range(4):
        sl = slice(h * LANES, (h + 1) * LANES)
        iq_ref[:, sl] = rot_part(z[:, sl], 2).astype(BF16)
    ikw_ref[...] = rot_part(mm(OFF_IKW, LANES), 5)
    for c in range(2):
        gr_ref[:, c * 512:(c + 1) * 512] = mm(OFF_GR + c * 512, 512)
    for c in range(2):
        ga_ref[:, c * 512:(c + 1) * 512] = mm(OFF_GA + c * 512, 512)


def _rot_tables(pos):
    p = pos.shape[0]
    posf = pos.astype(F32)[:, None]
    ret_f = RET_ROPE_BASE ** (-jnp.linspace(0.0, 1.0, RET_DK // 2, dtype=F32))
    ang = posf * ret_f[None, :]
    c, s = jnp.cos(ang), jnp.sin(ang)
    cos_r = jnp.concatenate([c, c], 1)
    sin_r = jnp.concatenate([-s, s], 1)
    n_rot = DSA_HEAD_DIM // 4
    att_f = ROPE_THETA ** (-jnp.arange(0, n_rot, 2, dtype=F32) / n_rot)
    ang2 = posf * att_f[None, :]
    c2, s2 = jnp.cos(ang2), jnp.sin(ang2)
    half = n_rot // 2
    rest = DSA_HEAD_DIM - 2 * half
    c64 = jnp.concatenate([c2, c2, jnp.ones((p, rest), F32)], 1)
    s1_64 = jnp.concatenate([-s2, jnp.zeros((p, DSA_HEAD_DIM - half), F32)], 1)
    s2_64 = jnp.concatenate([jnp.zeros((p, half), F32), s2, jnp.zeros((p, rest), F32)], 1)
    z64 = jnp.zeros((p, DSA_HEAD_DIM), F32)
    ci = jnp.concatenate([c64, jnp.full((p, IDX_HEADS), IDX_HEADS ** -0.5, F32),
                          jnp.zeros((p, DSA_HEAD_DIM - IDX_HEADS), F32)], 1)
    return jnp.concatenate([cos_r, sin_r,
                            jnp.concatenate([c64, c64], 1), jnp.concatenate([s1_64, s1_64], 1),
                            jnp.concatenate([s2_64, s2_64], 1),
                            ci, jnp.concatenate([s1_64, z64], 1), jnp.concatenate([s2_64, z64], 1)], 1)


def _pack_w_in(w):
    cuts = np.cumsum(PROJ_WIDTHS)[:-1].tolist()
    rq, rk, rv, rg, aq, ak, av, iq, ik, iw, gr, ga = jnp.split(w, cuts, axis=1)
    pad = jnp.zeros((w.shape[0], LANES - IDX_DIM - IDX_HEADS), w.dtype)
    return jnp.concatenate([rq, rk, rv, rg, aq, ak, av, iq, ik, iw, pad, gr, ga], axis=1).astype(BF16)


def _project(x, wp, tab, tab_period):
    n = x.shape[0]
    tm = PROJ_TM
    row = lambda w: pl.BlockSpec((tm, w), lambda i: (i, 0))
    out_shapes = [((n, 512), BF16), ((n, 512), BF16), ((n, 1024), BF16), ((n, 1024), F32),
                  ((n, 512), BF16), ((n, LANES), F32), ((n, LANES), F32), ((n, 512), BF16),
                  ((n, LANES), F32), ((n, 1024), F32), ((n, 1024), F32), ((n, DSA_HEADS * DSA_HEAD_DIM), BF16)]
    return pl.pallas_call(
        _proj_kernel,
        grid=(n // tm,),
        in_specs=[row(D_MODEL),
                  pl.BlockSpec((D_MODEL, PACKED_COLS), lambda i: (0, 0), pipeline_mode=pl.Buffered(1)),
                  pl.BlockSpec((tm, TAB_COLS), lambda i: (i % tab_period, 0))],
        out_specs=[row(s[1]) for s, _ in out_shapes],
        out_shape=[jax.ShapeDtypeStruct(s, d) for s, d in out_shapes],
        compiler_params=pltpu.CompilerParams(dimension_semantics=("parallel",), vmem_limit_bytes=52 * MIB),
        name="proj",
    )(x, wp, tab)


def _ret_kernel(dec_ref, xi_ref, zeta_ref, rq_ref, rk_ref, rv_ref, rg_ref, s0_ref, ret_ref, sout_ref, st_ref,
                *, rows, n_chunk, g_pow):
    j = pl.program_id(1)
    cpad = RET_CHUNK

    @pl.when(j == 0)
    def _():
        st_ref[...] = s0_ref[0]

    def padded(v):
        if rows == cpad:
            return v
        return jnp.concatenate([v, jnp.zeros((cpad - rows, v.shape[1]), v.dtype)], axis=0)

    for c in range(n_chunk):
        rs = slice(c * rows, (c + 1) * rows)
        for h in range(RET_HEADS):
            ks = slice(h * RET_DK, (h + 1) * RET_DK)
            vs = slice(h * RET_DV, (h + 1) * RET_DV)
            q = padded(rq_ref[rs, ks])
            kt = padded(rk_ref[rs, ks].astype(F32)).T
            v = padded(rv_ref[rs, vs])
            s = st_ref[h]
            sc = _mm(q, kt.astype(BF16)) * dec_ref[h]
            o = _mm(sc.astype(BF16), v) + _mm(q, s.astype(BF16)) * xi_ref[h]
            st_ref[h] = g_pow[h] * s + _mm((kt * zeta_ref[h]).astype(BF16), v)
            o = o[:rows]
            mu = jnp.mean(o, axis=-1, keepdims=True)
            d = o - mu
            var = jnp.mean(d * d, axis=-1, keepdims=True)
            gn = d * lax.rsqrt(var + GN_EPS)
            g = rg_ref[rs, vs]
            ret_ref[rs, vs] = (gn * (g * jax.nn.sigmoid(g))).astype(BF16)

    @pl.when(j == pl.num_programs(1) - 1)
    def _():
        sout_ref[0] = st_ref[...]


def _retention(rq, rk, rv, rg, s0, rows_per_stream, rows):
    n = rq.shape[0]
    n_streams = n // rows_per_stream
    n_chunk = min(rows_per_stream // rows, RET_CHUNKS_PER_STEP)
    blk = rows * n_chunk
    nb = rows_per_stream // blk
    gam = 1.0 - 2.0 ** (-5.0 - np.arange(RET_HEADS, dtype=np.float64))
    i = np.arange(RET_CHUNK, dtype=np.float64)
    diff = i[:, None] - i[None, :]
    dec = np.where(diff >= 0, gam[:, None, None] ** np.maximum(diff, 0.0)[None], 0.0)
    xi = gam[:, None, None] ** (i + 1.0)[None, :, None]
    zeta = np.where(i < rows, gam[:, None, None] ** (rows - 1.0 - i)[None, None, :], 0.0)
    g_pow = tuple(float(g ** rows) for g in gam)
    const = lambda shape: pl.BlockSpec(shape, lambda s, j: (0,) * len(shape))
    row = lambda w: pl.BlockSpec((blk, w), lambda s, j: (s * nb + j, 0))
    st = pl.BlockSpec((1, RET_HEADS, RET_DK, RET_DV), lambda s, j: (s, 0, 0, 0))
    return pl.pallas_call(
        functools.partial(_ret_kernel, rows=rows, n_chunk=n_chunk, g_pow=g_pow),
        grid=(n_streams, nb),
        in_specs=[const((RET_HEADS, RET_CHUNK, RET_CHUNK)), const((RET_HEADS, RET_CHUNK, 1)),
                  const((RET_HEADS, 1, RET_CHUNK)), row(512), row(512), row(1024), row(1024), st],
        out_specs=[row(1024), st],
        out_shape=[jax.ShapeDtypeStruct((n, RET_HEADS * RET_DV), BF16),
                   jax.ShapeDtypeStruct((n_streams, RET_HEADS, RET_DK, RET_DV), F32)],
        scratch_shapes=[pltpu.VMEM((RET_HEADS, RET_DK, RET_DV), F32)],
        compiler_params=pltpu.CompilerParams(dimension_semantics=("parallel", "arbitrary"),
                                             vmem_limit_bytes=32 * MIB),
        name="retention",
    )(jnp.asarray(dec, F32), jnp.asarray(xi, F32), jnp.asarray(zeta, F32), rq, rk, rv, rg, s0)


def _dsa_kernel(aq_ref, iq_ref, ikwq_ref, kk_ref, vv_ref, ikk_ref, *rest,
                group, tq, n_keys, row0, limit_const, n_sel):
    o_ref, w_ref, bias_ref = rest[-3:]
    rows = group * tq
    nsel_f = float(n_sel)
    neg_inf = -jnp.inf

    col = lax.broadcasted_iota(I32, (tq, n_keys), 1)
    if limit_const is None:
        limit = (jnp.right_shift(lax.broadcasted_iota(I32, (tq, 1), 0) + row0, 6) + 1) * CHUNK
    else:
        limit = limit_const

    def score_stream(s, carry):
        ikb = ikk_ref[s][:, :IDX_DIM].astype(BF16)
        iww = ikwq_ref[s][:, IDX_DIM:IDX_DIM + IDX_HEADS] * (IDX_DIM ** -0.5)
        acc = jnp.zeros((tq, n_keys), F32)
        for p in range(IDX_HEADS // 2):
            slab = iq_ref[s, :, p * LANES:(p + 1) * LANES]
            for hh in range(2):
                h = 2 * p + hh
                acc = acc + jnp.maximum(_nt(slab[:, hh * IDX_DIM:(hh + 1) * IDX_DIM], ikb), 0.0) * iww[:, h:h + 1]
        w_ref[pl.ds(pl.multiple_of(s * tq, tq), tq), :] = jnp.where(col < limit, acc, neg_inf)
        return carry

    lax.fori_loop(0, group, score_stream, 0)

    sc = w_ref[...]
    pos = jnp.sum(jnp.where(sc >= 0.0, 1.0, 0.0), axis=1, keepdims=True) >= nsel_f
    kk = jnp.where(pos, nsel_f, float(n_keys - n_sel + 1))
    w_ref[...] = jnp.where(pos, sc, -sc)

    def bit_step(i, u):
        cand_u = u | jnp.left_shift(jnp.int32(1), 30 - i)
        cand = pltpu.bitcast(cand_u, F32)
        cnt = jnp.sum(jnp.where(w_ref[...] >= cand, 1.0, 0.0), axis=1, keepdims=True)
        return jnp.where(cnt >= kk, cand_u, u)

    mag_u = lax.fori_loop(0, 31, bit_step, jnp.zeros((rows, 1), I32), unroll=SEARCH_UNROLL)
    mag = pltpu.bitcast(mag_u, F32)
    thr = jnp.where(pos, mag, -mag)

    sc = jnp.where(pos, w_ref[...], -w_ref[...])
    short = jnp.sum(jnp.where(sc >= thr, 1.0, 0.0), axis=1, keepdims=True) < nsel_f
    thr = jnp.where(jnp.logical_and(short, jnp.logical_not(pos)), -pltpu.bitcast(mag_u + 1, F32), thr)
    ge = sc >= thr
    cnt_gt = jnp.sum(jnp.where(sc > thr, 1.0, 0.0), axis=1, keepdims=True)
    cnt_ge = jnp.sum(jnp.where(ge, 1.0, 0.0), axis=1, keepdims=True)
    bias_ref[...] = jnp.where(jnp.logical_and(ge, sc > neg_inf), 0.0, neg_inf)
    excess = jnp.logical_and(cnt_ge > nsel_f, thr > neg_inf)

    @pl.when(jnp.max(jnp.where(excess, 1.0, 0.0)) > 0.0)
    def _():
        need = nsel_f - cnt_gt
        tri = jnp.where(lax.broadcasted_iota(I32, (LANES, LANES), 0) < lax.broadcasted_iota(I32, (LANES, LANES), 1),
                        1.0, 0.0).astype(BF16)
        before = jnp.zeros((rows, 1), F32)
        for b in range(n_keys // LANES):
            sl = slice(b * LANES, (b + 1) * LANES)
            sblk = jnp.where(pos, w_ref[:, sl], -w_ref[:, sl])
            eq = jnp.where(sblk == thr, 1.0, 0.0)
            rank = _mm(eq.astype(BF16), tri) + before
            keep = jnp.logical_or(sblk > thr, jnp.logical_and(sblk == thr, rank < need))
            bias_ref[:, sl] = jnp.where(jnp.logical_and(keep, sblk > neg_inf), 0.0, neg_inf)
            before = before + jnp.sum(eq, axis=1, keepdims=True)

    def attend_stream(s, carry):
        bias = bias_ref[pl.ds(pl.multiple_of(s * tq, tq), tq), :]
        kfull = kk_ref[s]
        vfull = vv_ref[s]
        for g in range(DSA_KV_HEADS):
            gs = slice(g * DSA_HEAD_DIM, (g + 1) * DSA_HEAD_DIM)
            kg = kfull[:, gs].astype(BF16)
            vg = vfull[:, gs].astype(BF16)
            for pp in range(2):
                p = 2 * g + pp
                slab = (aq_ref[s, :, p * LANES:(p + 1) * LANES].astype(F32) * (DSA_HEAD_DIM ** -0.5)).astype(BF16)
                outs = []
                for hh in range(2):
                    lg = _nt(slab[:, hh * DSA_HEAD_DIM:(hh + 1) * DSA_HEAD_DIM], kg) + bias
                    m = jnp.max(lg, axis=1, keepdims=True)
                    pr = jnp.exp(lg - m)
                    den = jnp.sum(pr, axis=1, keepdims=True)
                    outs.append(_mm(pr.astype(BF16), vg) / den)
                o_ref[s, :, p * LANES:(p + 1) * LANES] = jnp.concatenate(outs, axis=1).astype(BF16)
        return carry

    lax.fori_loop(0, group, attend_stream, 0)


def _dsa_call(aq, iq, ikw, kk, vv, ikk, group, tq, jq, n_keys, limit_const, n_sel, name, prev=None):
    n_streams, rows_per_stream = aq.shape[0], aq.shape[1]
    extra_specs, extra_args, aliases = [], (), {}
    if prev is not None:
        extra_specs, extra_args, aliases = [pl.BlockSpec(memory_space=pl.ANY)], (prev,), {6: 0}
    qrow = lambda a: pl.BlockSpec((group, tq, a.shape[2]), lambda s: (s, jq, 0))
    krow = lambda a: pl.BlockSpec((group, n_keys, a.shape[2]), lambda s: (s, 0, 0))
    width = DSA_HEADS * DSA_HEAD_DIM
    return pl.pallas_call(
        functools.partial(_dsa_kernel, group=group, tq=tq, n_keys=n_keys, row0=jq * tq, limit_const=limit_const,
                          n_sel=n_sel),
        grid=(n_streams // group,),
        in_specs=[qrow(aq), qrow(iq), qrow(ikw), krow(kk), krow(vv), krow(ikk)] + extra_specs,
        out_specs=pl.BlockSpec((group, tq, width), lambda s: (s, jq, 0)),
        out_shape=jax.ShapeDtypeStruct((n_streams, rows_per_stream, width), BF16),
        input_output_aliases=aliases,
        scratch_shapes=[pltpu.VMEM((group * tq, n_keys), F32), pltpu.VMEM((group * tq, n_keys), F32)],
        compiler_params=pltpu.CompilerParams(dimension_semantics=("parallel",), vmem_limit_bytes=56 * MIB),
        name=name,
    )(aq, iq, ikw, kk, vv, ikk, *extra_args)


def _dsa_cached_kernel(aq_ref, iq_ref, ikwq_ref, ck_ref, cv_ref, cik_ref, nk_ref, nv_ref, o_ref,
                       w_ref, bias_ref, kk_ref, vv_ref, ikk_ref, **static):
    past, new = ck_ref.shape[1], nk_ref.shape[1]
    for dst, cache, fresh in ((kk_ref, ck_ref, nk_ref), (vv_ref, cv_ref, nv_ref), (ikk_ref, cik_ref, ikwq_ref)):
        dst[:, :past, :cache.shape[2]] = cache[...]
        dst[:, past:past + new, :] = fresh[...]
        dst[:, past + new:, :] = jnp.zeros((dst.shape[0], dst.shape[1] - past - new, dst.shape[2]), F32)
    _dsa_kernel(aq_ref, iq_ref, ikwq_ref, kk_ref, vv_ref, ikk_ref, o_ref, w_ref, bias_ref, **static)


def _dsa_cached(aq, iq, ikw, cache_k, cache_v, cache_ik, new_k, new_v, tq):
    n_streams, past, _ = cache_k.shape
    n = aq.shape[0]
    per_stream = lambda a: a.reshape(n_streams, tq, a.shape[1])
    aq, iq, ikw, new_k, new_v = per_stream(aq), per_stream(iq), per_stream(ikw), per_stream(new_k), per_stream(new_v)
    limit = past + tq
    n_keys = -(-limit // LANES) * LANES
    group = max(1, min(n_streams, DSA_ROWS // tq, DSA_GROUP_MAX))
    blk = lambda a: pl.BlockSpec((group,) + a.shape[1:], lambda s: (s, 0, 0))
    width = DSA_HEADS * DSA_HEAD_DIM
    return pl.pallas_call(
        functools.partial(_dsa_cached_kernel, group=group, tq=tq, n_keys=n_keys, row0=0, limit_const=limit,
                          n_sel=min(DSA_TOPK, limit // 4)),
        grid=(n_streams // group,),
        in_specs=[blk(a) for a in (aq, iq, ikw, cache_k, cache_v, cache_ik, new_k, new_v)],
        out_specs=pl.BlockSpec((group, tq, width), lambda s: (s, 0, 0)),
        out_shape=jax.ShapeDtypeStruct((n_streams, tq, width), BF16),
        scratch_shapes=[pltpu.VMEM((group * tq, n_keys), F32), pltpu.VMEM((group * tq, n_keys), F32)] +
                       [pltpu.VMEM((group, n_keys, LANES), F32)] * 3,
        compiler_params=pltpu.CompilerParams(dimension_semantics=("parallel",), vmem_limit_bytes=48 * MIB),
        name="dsa_s",
    )(aq, iq, ikw, cache_k, cache_v, cache_ik, new_k, new_v).reshape(n, width)


def _pow2_floor(v):
    return 1 << (max(int(v), 1).bit_length() - 1)


def _dsa(aq, iq, ikw, kk, vv, ikk, tq, out):
    n_streams, n_keys, _ = kk.shape
    n = aq.shape[0]
    per_stream = lambda a: a.reshape(n_streams, n // n_streams, a.shape[1])
    aq, iq, ikw = per_stream(aq), per_stream(iq), per_stream(ikw)
    nq = n // n_streams // tq
    n_sel = min(DSA_TOPK, n_keys // 4)
    out = per_stream(out)
    for jq in range(nq):
        extent = (jq + 1) * tq
        group = _pow2_floor(min(n_streams, DSA_GROUP_MAX, max(DSA_ROWS // tq, DSA_CELLS // (tq * extent))))
        out = _dsa_call(aq, iq, ikw, kk, vv, ikk, group, tq, jq, extent, None, n_sel, f"dsa_p{jq}", prev=out)
    return out.reshape(n, -1)


def _pack_rows(v):
    q = QUARTER
    bits = lambda x: pltpu.bitcast(x.astype(BF16).astype(F32), I32)
    pair = lambda c: lax.shift_right_logical(bits(v[:, c * q:(c + 1) * q]), 16) | bits(v[:, (c + 1) * q:(c + 2) * q])
    return pair(0), pair(2)


def _unpack_rows(lo, hi):
    parts = []
    for w in (lo, hi):
        parts.append(pltpu.bitcast(lax.shift_left(w, 16), F32))
        parts.append(pltpu.bitcast(w & jnp.int32(-65536), F32))
    return jnp.concatenate(parts, axis=1)


def _layer_norm(v, g, b):
    mu = jnp.mean(v, axis=-1, keepdims=True)
    d = v - mu
    var = jnp.mean(d * d, axis=-1, keepdims=True)
    return d * lax.rsqrt(var + LN_EPS) * g + b


def _merge_kernel(x_ref, ret_ref, od_ref, gr_ref, ga_ref, wr_ref, wd_ref, wo_ref, g1_ref, b1_ref, rwt_ref, rb_ref,
                  tri_ref, cnt0_ref, h1_ref, hlo_ref, hhi_ref, gtm_ref, ek_ref, rk_ref, tot_ref, cnt_ref):
    @pl.when(pl.program_id(0) == 0)
    def _():
        cnt_ref[...] = cnt0_ref[:, 0:1]

    y_ret = _mm(ret_ref[...], wr_ref[...])
    y_dsa = _mm(od_ref[...], wd_ref[...])
    merged = jax.nn.sigmoid(gr_ref[...]) * y_ret + jax.nn.sigmoid(ga_ref[...]) * y_dsa
    mix = _mm(merged.astype(BF16), wo_ref[...])
    h1 = _layer_norm(DEEPNORM_ALPHA * x_ref[...] + mix, g1_ref[...], b1_ref[...])
    h1_ref[...] = h1
    hlo_ref[...], hhi_ref[...] = _pack_rows(h1)

    logits = lax.dot_general(rwt_ref[...], h1, (((1,), (1,)), ((), ())), preferred_element_type=F32,
                             precision=lax.Precision.HIGHEST) + rb_ref[...]
    tm = logits.shape[1]
    e_iota = lax.broadcasted_iota(I32, (N_EXPERTS, tm), 0)
    tops, hots, firsts = [], [], []
    for _ in range(MOE_TOP_K):
        m = jnp.max(logits, axis=0, keepdims=True)
        first = jnp.min(jnp.where(logits == m, e_iota, N_EXPERTS), axis=0, keepdims=True)
        hot = e_iota == first
        tops.append(m)
        hots.append(hot)
        firsts.append(first)
        logits = jnp.where(hot, -jnp.inf, logits)
    exps = [jnp.exp(m - tops[0]) for m in tops]
    den = exps[0] + exps[1] + exps[2] + exps[3]
    sel = jnp.zeros((N_EXPERTS, tm), F32)
    for hot in hots:
        sel = sel + jnp.where(hot, 1.0, 0.0)
    rank = _mm(sel.astype(BF16), tri_ref[...]) + cnt_ref[...]
    ranks = [jnp.sum(jnp.where(hot, rank, 0.0), axis=0, keepdims=True).astype(I32) for hot in hots]
    pad_i = jnp.zeros((8 - MOE_TOP_K, tm), I32)
    ek_ref[...] = jnp.concatenate(firsts + [pad_i], axis=0)
    rk_ref[...] = jnp.concatenate(ranks + [pad_i], axis=0)
    gates = jnp.concatenate([e / den for e in exps] + [jnp.zeros((LANES - MOE_TOP_K, tm), F32)], axis=0)
    gtm_ref[...] = gates.T
    cnt_ref[...] = cnt_ref[...] + jnp.sum(sel, axis=1, keepdims=True)
    tot_ref[...] = jnp.broadcast_to(cnt_ref[...], tot_ref.shape)


def _merge(x, ret, od, gr, ga, wr, wd, wo, g1, b1, rwt, rb, cnt0):
    n = x.shape[0]
    tm = MERGE_TM
    row = lambda w: pl.BlockSpec((tm, w), lambda i: (i, 0))
    const = lambda a: pl.BlockSpec(a.shape, lambda i: (0,) * a.ndim)
    col = pl.BlockSpec((8, tm), lambda i: (0, i))
    tri = jnp.asarray(np.triu(np.ones((tm, tm), np.float32), 1), BF16)
    return pl.pallas_call(
        _merge_kernel,
        grid=(n // tm,),
        in_specs=[row(D_MODEL), row(1024), row(512), row(1024), row(1024), const(wr), const(wd), const(wo),
                  const(g1), const(b1), const(rwt), const(rb), const(tri), const(cnt0)],
        out_specs=[row(D_MODEL), row(QUARTER), row(QUARTER), row(LANES), col, col,
                   pl.BlockSpec((N_EXPERTS, LANES), lambda i: (0, 0))],
        out_shape=[jax.ShapeDtypeStruct((n, D_MODEL), F32), jax.ShapeDtypeStruct((n, QUARTER), I32),
                   jax.ShapeDtypeStruct((n, QUARTER), I32), jax.ShapeDtypeStruct((n, LANES), F32),
                   jax.ShapeDtypeStruct((8, n), I32), jax.ShapeDtypeStruct((8, n), I32),
                   jax.ShapeDtypeStruct((N_EXPERTS, LANES), F32)],
        scratch_shapes=[pltpu.VMEM((N_EXPERTS, 1), F32)],
        compiler_params=pltpu.CompilerParams(dimension_semantics=("arbitrary",), vmem_limit_bytes=48 * MIB),
        name="merge",
    )(x, ret, od, gr, ga, wr, wd, wo, g1, b1, rwt, rb, tri, cnt0)


def _deinterleave_kernel(w_ref, o_ref):
    r = lax.broadcasted_iota(I32, (UP_BLOCK, UP_BLOCK), 0)
    c = lax.broadcasted_iota(I32, (UP_BLOCK, UP_BLOCK), 1)
    src = jnp.where(c < LANES, 2 * c, 2 * (c - LANES) + 1)
    perm = jnp.where(r == src, 1.0, 0.0).astype(BF16)
    for b in range(w_ref.shape[2] // UP_BLOCK):
        sl = slice(b * UP_BLOCK, (b + 1) * UP_BLOCK)
        o_ref[0, :, sl] = _mm(w_ref[0, :, sl].astype(BF16), perm).astype(BF16)


def _deinterleave_w_up(w_up):
    n_e, d_in, d_out = w_up.shape
    cols = d_out
    spec = pl.BlockSpec((1, d_in, cols), lambda e, c: (e, 0, c))
    return pl.pallas_call(
        _deinterleave_kernel,
        grid=(n_e, d_out // cols),
        in_specs=[spec],
        out_specs=spec,
        out_shape=jax.ShapeDtypeStruct(w_up.shape, BF16),
        compiler_params=pltpu.CompilerParams(dimension_semantics=("parallel", "parallel"),
                                             vmem_limit_bytes=40 * MIB),
        name="w_up_prep",
    )(w_up)


def _pos_kernel(off_ref, ek_ref, rk_ref, pos_ref):
    ek = ek_ref[...]
    pos = rk_ref[...]
    for e in range(N_EXPERTS):
        pos = pos + jnp.where(ek == e, off_ref[e], 0)
    pos_ref[...] = pos


def _positions(off, ek, rk):
    n = ek.shape[1]
    tn = min(n, 2048)
    spec = pl.BlockSpec((8, tn), lambda i, off: (0, i))
    return pl.pallas_call(
        _pos_kernel,
        grid_spec=pltpu.PrefetchScalarGridSpec(num_scalar_prefetch=1, grid=(n // tn,), in_specs=[spec, spec],
                                               out_specs=spec),
        out_shape=jax.ShapeDtypeStruct((8, n), I32),
        name="moe_pos",
    )(off, ek, rk)


def _sc_mesh():
    return plsc.VectorSubcoreMesh(core_axis_name="core", subcore_axis_name="subcore")


def _sc_scatter(xs, pos_flat, n_rows):
    width, dtype = xs[0].shape[1], xs[0].dtype
    windows = [x.shape[0] // SC_WINDOW for x in xs]
    nw = sum(windows)

    @functools.partial(pl.kernel, out_type=jax.ShapeDtypeStruct((n_rows, width), dtype), mesh=_sc_mesh())
    def scatter(*refs):
        p_hbm, o_hbm = refs[len(xs)], refs[len(xs) + 1]

        def body(x_vmem, *p_vmem):
            for p in p_vmem:
                pltpu.sync_copy(x_vmem, o_hbm.at[p.at[0]])

        first = 0
        for x_hbm, nw_g in zip(refs[:len(xs)], windows):
            pltpu.emit_pipeline(
                body, grid=(nw_g,),
                in_specs=[pl.BlockSpec((SC_WINDOW, width), lambda i: (i, 0))] +
                         [pl.BlockSpec((1, SC_WINDOW), functools.partial(lambda w0, i: (0, w0 + i), k * nw + first))
                          for k in range(MOE_TOP_K)],
                out_specs=[], core_axis_name=("core", "subcore"), dimension_semantics=(pltpu.PARALLEL,),
            )(x_hbm, *([p_hbm] * MOE_TOP_K))
            first += nw_g

    return scatter(*xs, pos_flat)


def _sc_gather(y, pos_flat):
    m = pos_flat.shape[1]

    @functools.partial(pl.kernel, out_type=jax.ShapeDtypeStruct((m, y.shape[1]), y.dtype), mesh=_sc_mesh())
    def gather(y_hbm, p_hbm, o_hbm):
        def body(p_vmem, o_vmem):
            pltpu.sync_copy(y_hbm.at[p_vmem.at[0]], o_vmem)

        pltpu.emit_pipeline(
            body, grid=(m // SC_WINDOW,),
            in_specs=[pl.BlockSpec((1, SC_WINDOW), lambda i: (0, i))],
            out_specs=[pl.BlockSpec((SC_WINDOW, y.shape[1]), lambda i: (i, 0))],
            core_axis_name=("core", "subcore"), dimension_semantics=(pltpu.PARALLEL,),
        )(p_hbm, o_hbm)

    return gather(y, pos_flat)


def _ffn_kernel(be_ref, nu_ref, xlo_ref, xhi_ref, wup_ref, bup_ref, wdn_ref, bdn_ref, ylo_ref, yhi_ref):
    @pl.when(pl.program_id(0) < nu_ref[0])
    def _():
        x = _unpack_rows(xlo_ref[...], xhi_ref[...]).astype(BF16)
        h = _mm(x, wup_ref[0]) + bup_ref[0]
        acts = []
        for b in range(2 * D_FF // UP_BLOCK):
            glu = jnp.minimum(h[:, b * UP_BLOCK:b * UP_BLOCK + LANES], SWIGLU_LIMIT)
            lin = jnp.clip(h[:, b * UP_BLOCK + LANES:(b + 1) * UP_BLOCK], -SWIGLU_LIMIT, SWIGLU_LIMIT)
            acts.append(glu * jax.nn.sigmoid(SWIGLU_ALPHA * glu) * (lin + 1.0))
        act = jnp.concatenate(acts, axis=1)
        ylo_ref[...], yhi_ref[...] = _pack_rows(_mm(act.astype(BF16), wdn_ref[0]) + bdn_ref[0])


def _ffn(block_expert, n_used, xs_lo, xs_hi, wup, bup, wdn, bdn):
    rows = xs_lo.shape[0]
    blk = FFN_BLOCK
    row = pl.BlockSpec((blk, QUARTER), lambda i, be, nu: (i, 0))
    per_expert = lambda a: pl.BlockSpec((1,) + a.shape[1:], lambda i, be, nu: (be[i], 0, 0))
    return pl.pallas_call(
        _ffn_kernel,
        grid_spec=pltpu.PrefetchScalarGridSpec(
            num_scalar_prefetch=2, grid=(rows // blk,),
            in_specs=[row, row, per_expert(wup), per_expert(bup), per_expert(wdn), per_expert(bdn)],
            out_specs=[row, row]),
        out_shape=[jax.ShapeDtypeStruct((rows, QUARTER), I32), jax.ShapeDtypeStruct((rows, QUARTER), I32)],
        compiler_params=pltpu.CompilerParams(dimension_semantics=("arbitrary",), vmem_limit_bytes=40 * MIB),
        name="moe_ffn",
    )(block_expert, n_used, xs_lo, xs_hi, wup, bup, wdn, bdn)


def _combine_kernel(h1_ref, olo_ref, ohi_ref, gtm_ref, g2_ref, b2_ref, o_ref):
    g = gtm_ref[...]
    y = jnp.zeros(h1_ref.shape, F32)
    for k in range(MOE_TOP_K):
        y = y + g[:, k:k + 1] * _unpack_rows(olo_ref[k], ohi_ref[k])
    o_ref[...] = _layer_norm(DEEPNORM_ALPHA * h1_ref[...] + y, g2_ref[...], b2_ref[...])


def _combine(h1, og_lo, og_hi, gtm, g2, b2, row0):
    n = h1.shape[0]
    n_all = og_lo.shape[0] // MOE_TOP_K
    tm = MERGE_TM
    row = lambda w: pl.BlockSpec((tm, w), lambda i: (i, 0))
    const = lambda a: pl.BlockSpec(a.shape, lambda i: (0,) * a.ndim)
    picked = pl.BlockSpec((MOE_TOP_K, tm, QUARTER), lambda i: (0, i + row0 // tm, 0))
    return pl.pallas_call(
        _combine_kernel,
        grid=(n // tm,),
        in_specs=[row(D_MODEL), picked, picked, row(LANES), const(g2), const(b2)],
        out_specs=row(D_MODEL),
        out_shape=jax.ShapeDtypeStruct((n, D_MODEL), F32),
        compiler_params=pltpu.CompilerParams(dimension_semantics=("parallel",), vmem_limit_bytes=40 * MIB),
        name="moe_combine",
    )(h1, og_lo.reshape(MOE_TOP_K, n_all, QUARTER), og_hi.reshape(MOE_TOP_K, n_all, QUARTER), gtm, g2, b2)


def _moe(groups, totals, wup, bup, wdn, bdn, g2, b2):
    blk = FFN_BLOCK
    sizes = [g[0].shape[0] for g in groups]
    n_all = sum(sizes)
    n_rows = MOE_TOP_K * n_all + N_EXPERTS * blk
    counts = totals[:, 0].astype(I32)
    padded = (counts + blk - 1) // blk * blk
    ends = jnp.cumsum(padded)
    block_start = jnp.arange(n_rows // blk, dtype=I32) * blk
    block_expert = jnp.minimum(jnp.sum((ends[None, :] <= block_start[:, None]).astype(I32), axis=1), N_EXPERTS - 1)
    n_used = (ends[-1:] // blk).astype(I32)
    cat = lambda i, axis: jnp.concatenate([g[i] for g in groups], axis=axis)
    pos = _positions((ends - padded).astype(I32), cat(4, 1), cat(5, 1))
    pos_flat = pos[:MOE_TOP_K].reshape(1, MOE_TOP_K * n_all)
    xs_lo = _sc_scatter([g[1] for g in groups], pos_flat, n_rows)
    xs_hi = _sc_scatter([g[2] for g in groups], pos_flat, n_rows)
    ys_lo, ys_hi = _ffn(block_expert, n_used, xs_lo, xs_hi, wup, bup, wdn, bdn)
    og_lo, og_hi = _sc_gather(ys_lo, pos_flat), _sc_gather(ys_hi, pos_flat)
    starts = np.cumsum([0] + sizes[:-1]).tolist()
    return [_combine(g[0], og_lo, og_hi, g[3], g2, b2, r0) for g, r0 in zip(groups, starts)]


def _mixer(x, pos_tab, tab_period, s0, rows_per_stream, ret_rows, caches, weights, cnt0):
    wp, wr, wd, wo, g1, b1, rwt, rb = weights
    n = x.shape[0]
    n_streams = n // rows_per_stream
    rq, rk, rv, rg, aq, ak, av, iq, ikw, gr, ga, od0 = _project(x, wp, pos_tab, tab_period)
    ret, s_new = _retention(rq, rk, rv, rg, s0, rows_per_stream, ret_rows)
    if caches is None:
        per_stream = lambda a: a.reshape(n_streams, rows_per_stream, a.shape[1])
        od = _dsa(aq, iq, ikw, per_stream(ak), per_stream(av), per_stream(ikw), DSA_TQ, od0)
    else:
        od = _dsa_cached(aq, iq, ikw, *caches, ak, av, rows_per_stream)
    h1, h_lo, h_hi, gtm, ek, rk_, totals = _merge(x, ret, od, gr, ga, wr, wd, wo, g1, b1, rwt, rb, cnt0)
    return (h1, h_lo, h_hi, gtm, ek, rk_), totals, (s_new, ak, av, ikw[:, :IDX_DIM])


def kernel(x_prompt, x_sample, state_ret, cache_k, cache_v, cache_idx_k, w_in, w_ret_o, w_dsa_o, w_o,
           ln1_g, ln1_b, router_w, router_b, w_up, b_up, w_down, b_down, ln2_g, ln2_b):
    assert w_in.shape[0] == DEPTH
    batch, seq, _ = x_prompt.shape
    dec_batch, dec_seq, _ = x_sample.shape
    past = cache_k.shape[2]
    assert seq % PROJ_TM == 0 and seq % DSA_TQ == 0 and PROJ_TM % dec_seq == 0

    l = 0
    mixer_w = (_pack_w_in(w_in[l]), w_ret_o[l].astype(BF16), w_dsa_o[l].astype(BF16), w_o[l].astype(BF16),
               ln1_g[l][None, :], ln1_b[l][None, :], router_w[l].T, router_b[l][:, None])
    moe_w = (
        _deinterleave_w_up(w_up[l]),
        b_up[l].reshape(N_EXPERTS, 2 * D_FF // UP_BLOCK, LANES, 2).transpose(0, 1, 3, 2).reshape(N_EXPERTS, 1, 2 * D_FF),
        w_down[l].astype(BF16), b_down[l][:, None, :], ln2_g[l][None, :], ln2_b[l][None, :])

    tab_p = _rot_tables(jnp.arange(seq))
    zeros_state = jnp.zeros((batch, RET_HEADS, RET_DK, RET_DV), F32)
    moe_p, totals_p, (s_p, k_p, v_p, ik_p) = _mixer(
        x_prompt.reshape(batch * seq, D_MODEL), tab_p, seq // PROJ_TM, zeros_state, seq, RET_CHUNK, None, mixer_w,
        jnp.zeros((N_EXPERTS, LANES), F32))

    tab_s = jnp.tile(_rot_tables(past + jnp.arange(dec_seq)), (PROJ_TM // dec_seq, 1))
    caches = (cache_k[l].reshape(dec_batch, past, LANES), cache_v[l].reshape(dec_batch, past, LANES), cache_idx_k[l])
    moe_s, totals, (s_s, k_s, v_s, ik_s) = _mixer(
        x_sample.reshape(dec_batch * dec_seq, D_MODEL), tab_s, 1, state_ret[l], dec_seq, dec_seq, caches, mixer_w,
        totals_p)

    y_p, y_s = _moe([moe_p, moe_s], totals, *moe_w)

    kv = (DSA_KV_HEADS, DSA_HEAD_DIM)
    return (y_p.reshape(batch, seq, D_MODEL), y_s.reshape(dec_batch, dec_seq, D_MODEL),
            s_p[None], k_p.reshape(1, batch, seq, *kv), v_p.reshape(1, batch, seq, *kv),
            ik_p.reshape(1, batch, seq, IDX_DIM),
            s_s[None], k_s.reshape(1, dec_batch, dec_seq, *kv), v_s.reshape(1, dec_batch, dec_seq, *kv),
            ik_s.reshape(1, dec_batch, dec_seq, IDX_DIM))
```

```python
import functools

import numpy as np
import jax
import jax.numpy as jnp
from jax import lax
from jax.experimental import pallas as pl
from jax.experimental.pallas import tpu as pltpu
from jax.experimental.pallas import tpu_sc as plsc

F32 = jnp.float32
BF16 = jnp.bfloat16
I32 = jnp.int32

D_MODEL = 1024
CHUNK = 64
RET_HEADS = 4
RET_DK = 128
RET_DV = 256
RET_ROPE_BASE = 10000.0
DSA_HEADS = 8
DSA_KV_HEADS = 2
DSA_HEAD_DIM = 64
IDX_HEADS = 8
IDX_DIM = 64
DSA_TOPK = 256
ROPE_THETA = 500000.0
N_EXPERTS = 32
MOE_TOP_K = 4
D_FF = 1024
SWIGLU_ALPHA = 1.702
SWIGLU_LIMIT = 7.0
LN_EPS = 1e-5
GN_EPS = 1e-6
DEPTH = 1
DEEPNORM_ALPHA = (2.0 * DEPTH) ** 0.25
PROJ_WIDTHS = (RET_HEADS * RET_DK, RET_HEADS * RET_DK, RET_HEADS * RET_DV, RET_HEADS * RET_DV,
               DSA_HEADS * DSA_HEAD_DIM, DSA_KV_HEADS * DSA_HEAD_DIM, DSA_KV_HEADS * DSA_HEAD_DIM,
               IDX_HEADS * IDX_DIM, IDX_DIM, IDX_HEADS, D_MODEL, D_MODEL)

LANES = 128
MIB = 1024 * 1024

OFF_RQ, OFF_RK, OFF_RV, OFF_RG = 0, 512, 1024, 2048
OFF_AQ, OFF_AK, OFF_AV, OFF_IQ, OFF_IKW = 3072, 3584, 3712, 3840, 4352
OFF_GR, OFF_GA, PACKED_COLS = 4480, 5504, 6528
TAB_COLS = 8 * LANES

PROJ_TM = 512
RET_CHUNK = 128
RET_CHUNKS_PER_STEP = 8
DSA_TQ = 256
DSA_ROWS = 512
DSA_GROUP_MAX = 4
SEARCH_UNROLL = 4
MERGE_TM = 512
FFN_BLOCK = 512
SC_WINDOW = 128
QUARTER = D_MODEL // 4
UP_BLOCK = 2 * LANES


def _nt(a, b):
    return lax.dot_general(a, b, (((1,), (1,)), ((), ())), preferred_element_type=F32)


def _mm(a, b):
    return jnp.dot(a, b, preferred_element_type=F32)


def _proj_kernel(x_ref, w_ref, tab_ref, rq_ref, rk_ref, rv_ref, rg_ref, aq_ref, ak_ref, av_ref,
                 iq_ref, ikw_ref, gr_ref, ga_ref, od0_ref):
    xb = x_ref[...].astype(BF16)
    od0_ref[...] = jnp.zeros(od0_ref.shape, od0_ref.dtype)

    def mm(c0, n):
        return _mm(xb, w_ref[:, c0:c0 + n])

    def tab(i):
        return tab_ref[:, i * LANES:(i + 1) * LANES]

    def rot_full(z):
        return z * tab(0) + pltpu.roll(z, 64, 1) * tab(1)

    def rot_part(z, c):
        return z * tab(c) + pltpu.roll(z, LANES - 8, 1) * tab(c + 1) + pltpu.roll(z, 8, 1) * tab(c + 2)

    z = mm(OFF_RQ, 512)
    for h in range(4):
        sl = slice(h * LANES, (h + 1) * LANES)
        rq_ref[:, sl] = rot_full(z[:, sl]).astype(BF16)
    z = mm(OFF_RK, 512)
    for h in range(4):
        sl = slice(h * LANES, (h + 1) * LANES)
        rk_ref[:, sl] = (rot_full(z[:, sl]) * (RET_DK ** -0.5)).astype(BF16)
    for c in range(2):
        rv_ref[:, c * 512:(c + 1) * 512] = mm(OFF_RV + c * 512, 512).astype(BF16)
    for c in range(2):
        rg_ref[:, c * 512:(c + 1) * 512] = mm(OFF_RG + c * 512, 512)
    z = mm(OFF_AQ, 512)
    for h in range(4):
        sl = slice(h * LANES, (h + 1) * LANES)
        aq_ref[:, sl] = rot_part(z[:, sl], 2).astype(BF16)
    z = mm(OFF_AK, 256)
    ak_ref[...] = rot_part(z[:, :LANES], 2)
    av_ref[...] = z[:, LANES:]
    z = mm(OFF_IQ, 512)
    for h in range(4):
        sl = slice(h * LANES, (h + 1) * LANES)
        iq_ref[:, sl] = rot_part(z[:, sl], 2).astype(BF16)
    ikw_ref[...] = rot_part(mm(OFF_IKW, LANES), 5)
    for c in range(2):
        gr_ref[:, c * 512:(c + 1) * 512] = mm(OFF_GR + c * 512, 512)
    for c in range(2):
        ga_ref[:, c * 512:(c + 1) * 512] = mm(OFF_GA + c * 512, 512)


def _rot_tables(pos):
    p = pos.shape[0]
    posf = pos.astype(F32)[:, None]
    ret_f = RET_ROPE_BASE ** (-jnp.linspace(0.0, 1.0, RET_DK // 2, dtype=F32))
    ang = posf * ret_f[None, :]
    c, s = jnp.cos(ang), jnp.sin(ang)
    cos_r = jnp.concatenate([c, c], 1)
    sin_r = jnp.concatenate([-s, s], 1)
    n_rot = DSA_HEAD_DIM // 4
    att_f = ROPE_THETA ** (-jnp.arange(0, n_rot, 2, dtype=F32) / n_rot)
    ang2 = posf * att_f[None, :]
    c2, s2 = jnp.cos(ang2), jnp.sin(ang2)
    half = n_rot // 2
    rest = DSA_HEAD_DIM - 2 * half
    c64 = jnp.concatenate([c2, c2, jnp.ones((p, rest), F32)], 1)
    s1_64 = jnp.concatenate([-s2, jnp.zeros((p, DSA_HEAD_DIM - half), F32)], 1)
    s2_64 = jnp.concatenate([jnp.zeros((p, half), F32), s2, jnp.zeros((p, rest), F32)], 1)
    z64 = jnp.zeros((p, DSA_HEAD_DIM), F32)
    ci = jnp.concatenate([c64, jnp.full((p, IDX_HEADS), IDX_HEADS ** -0.5, F32),
                          jnp.zeros((p, DSA_HEAD_DIM - IDX_HEADS), F32)], 1)
    return jnp.concatenate([cos_r, sin_r,
                            jnp.concatenate([c64, c64], 1), jnp.concatenate([s1_64, s1_64], 1),
                            jnp.concatenate([s2_64, s2_64], 1),
                            ci, jnp.concatenate([s1_64, z64], 1), jnp.concatenate([s2_64, z64], 1)], 1)


def _pack_w_in(w):
    cuts = np.cumsum(PROJ_WIDTHS)[:-1].tolist()
    rq, rk, rv, rg, aq, ak, av, iq, ik, iw, gr, ga = jnp.split(w, cuts, axis=1)
    pad = jnp.zeros((w.shape[0], LANES - IDX_DIM - IDX_HEADS), w.dtype)
    return jnp.concatenate([rq, rk, rv, rg, aq, ak, av, iq, ik, iw, pad, gr, ga], axis=1).astype(BF16)


def _project(x, wp, tab, tab_period):
    n = x.shape[0]
    tm = PROJ_TM
    row = lambda w: pl.BlockSpec((tm, w), lambda i: (i, 0))
    out_shapes = [((n, 512), BF16), ((n, 512), BF16), ((n, 1024), BF16), ((n, 1024), F32),
                  ((n, 512), BF16), ((n, LANES), F32), ((n, LANES), F32), ((n, 512), BF16),
                  ((n, LANES), F32), ((n, 1024), F32), ((n, 1024), F32), ((n, DSA_HEADS * DSA_HEAD_DIM), BF16)]
    return pl.pallas_call(
        _proj_kernel,
        grid=(n // tm,),
        in_specs=[row(D_MODEL),
                  pl.BlockSpec((D_MODEL, PACKED_COLS), lambda i: (0, 0), pipeline_mode=pl.Buffered(1)),
                  pl.BlockSpec((tm, TAB_COLS), lambda i: (i % tab_period, 0))],
        out_specs=[row(s[1]) for s, _ in out_shapes],
        out_shape=[jax.ShapeDtypeStruct(s, d) for s, d in out_shapes],
        compiler_params=pltpu.CompilerParams(dimension_semantics=("parallel",), vmem_limit_bytes=52 * MIB),
        name="proj",
    )(x, wp, tab)


def _ret_kernel(dec_ref, xi_ref, zeta_ref, rq_ref, rk_ref, rv_ref, rg_ref, s0_ref, ret_ref, sout_ref, st_ref,
                *, rows, n_chunk, g_pow):
    j = pl.program_id(1)
    cpad = RET_CHUNK

    @pl.when(j == 0)
    def _():
        st_ref[...] = s0_ref[0]

    def padded(v):
        if rows == cpad:
            return v
        return jnp.concatenate([v, jnp.zeros((cpad - rows, v.shape[1]), v.dtype)], axis=0)

    for c in range(n_chunk):
        rs = slice(c * rows, (c + 1) * rows)
        for h in range(RET_HEADS):
            ks = slice(h * RET_DK, (h + 1) * RET_DK)
            vs = slice(h * RET_DV, (h + 1) * RET_DV)
            q = padded(rq_ref[rs, ks])
            kt = padded(rk_ref[rs, ks].astype(F32)).T
            v = padded(rv_ref[rs, vs])
            s = st_ref[h]
            sc = _mm(q, kt.astype(BF16)) * dec_ref[h]
            o = _mm(sc.astype(BF16), v) + _mm(q, s.astype(BF16)) * xi_ref[h]
            st_ref[h] = g_pow[h] * s + _mm((kt * zeta_ref[h]).astype(BF16), v)
            o = o[:rows]
            mu = jnp.mean(o, axis=-1, keepdims=True)
            d = o - mu
            var = jnp.mean(d * d, axis=-1, keepdims=True)
            gn = d * lax.rsqrt(var + GN_EPS)
            g = rg_ref[rs, vs]
            ret_ref[rs, vs] = (gn * (g * jax.nn.sigmoid(g))).astype(BF16)

    @pl.when(j == pl.num_programs(1) - 1)
    def _():
        sout_ref[0] = st_ref[...]


def _retention(rq, rk, rv, rg, s0, rows_per_stream, rows):
    n = rq.shape[0]
    n_streams = n // rows_per_stream
    n_chunk = min(rows_per_stream // rows, RET_CHUNKS_PER_STEP)
    blk = rows * n_chunk
    nb = rows_per_stream // blk
    gam = 1.0 - 2.0 ** (-5.0 - np.arange(RET_HEADS, dtype=np.float64))
    i = np.arange(RET_CHUNK, dtype=np.float64)
    diff = i[:, None] - i[None, :]
    dec = np.where(diff >= 0, gam[:, None, None] ** np.maximum(diff, 0.0)[None], 0.0)
    xi = gam[:, None, None] ** (i + 1.0)[None, :, None]
    zeta = np.where(i < rows, gam[:, None, None] ** (rows - 1.0 - i)[None, None, :], 0.0)
    g_pow = tuple(float(g ** rows) for g in gam)
    const = lambda shape: pl.BlockSpec(shape, lambda s, j: (0,) * len(shape))
    row = lambda w: pl.BlockSpec((blk, w), lambda s, j: (s * nb + j, 0))
    st = pl.BlockSpec((1, RET_HEADS, RET_DK, RET_DV), lambda s, j: (s, 0, 0, 0))
    return pl.pallas_call(
        functools.partial(_ret_kernel, rows=rows, n_chunk=n_chunk, g_pow=g_pow),
        grid=(n_streams, nb),
        in_specs=[const((RET_HEADS, RET_CHUNK, RET_CHUNK)), const((RET_HEADS, RET_CHUNK, 1)),
                  const((RET_HEADS, 1, RET_CHUNK)), row(512), row(512), row(1024), row(1024), st],
        out_specs=[row(1024), st],
        out_shape=[jax.ShapeDtypeStruct((n, RET_HEADS * RET_DV), BF16),
                   jax.ShapeDtypeStruct((n_streams, RET_HEADS, RET_DK, RET_DV), F32)],
        scratch_shapes=[pltpu.VMEM((RET_HEADS, RET_DK, RET_DV), F32)],
        compiler_params=pltpu.CompilerParams(dimension_semantics=("parallel", "arbitrary"),
                                             vmem_limit_bytes=32 * MIB),
        name="retention",
    )(jnp.asarray(dec, F32), jnp.asarray(xi, F32), jnp.asarray(zeta, F32), rq, rk, rv, rg, s0)


def _dsa_kernel(aq_ref, iq_ref, ikwq_ref, kk_ref, vv_ref, ikk_ref, *rest,
                group, tq, n_keys, row0, limit_const, n_sel):
    o_ref, w_ref, bias_ref = rest[-3:]
    rows = group * tq
    nsel_f = float(n_sel)
    neg_inf = -jnp.inf

    col = lax.broadcasted_iota(I32, (tq, n_keys), 1)
    if limit_const is None:
        limit = (jnp.right_shift(lax.broadcasted_iota(I32, (tq, 1), 0) + row0, 6) + 1) * CHUNK
    else:
        limit = limit_const

    def attend_stream(s, carry):
        bias = bias_ref[pl.ds(pl.multiple_of(s * tq, tq), tq), :]
        kfull = kk_ref[s]
        vfull = vv_ref[s]
        for g in range(DSA_KV_HEADS):
            gs = slice(g * DSA_HEAD_DIM, (g + 1) * DSA_HEAD_DIM)
            kg = kfull[:, gs].astype(BF16)
            vg = vfull[:, gs].astype(BF16)
            for pp in range(2):
                p = 2 * g + pp
                slab = (aq_ref[s, :, p * LANES:(p + 1) * LANES].astype(F32) * (DSA_HEAD_DIM ** -0.5)).astype(BF16)
                outs = []
                for hh in range(2):
                    lg = _nt(slab[:, hh * DSA_HEAD_DIM:(hh + 1) * DSA_HEAD_DIM], kg) + bias
                    m = jnp.max(lg, axis=1, keepdims=True)
                    pr = jnp.exp(lg - m)
                    den = jnp.sum(pr, axis=1, keepdims=True)
                    outs.append(_mm(pr.astype(BF16), vg) / den)
                o_ref[s, :, p * LANES:(p + 1) * LANES] = jnp.concatenate(outs, axis=1).astype(BF16)
        return carry

    if n_keys <= n_sel:
        for s in range(group):
            bias_ref[s * tq:(s + 1) * tq, :] = jnp.where(col < limit, 0.0, neg_inf)
        lax.fori_loop(0, group, attend_stream, 0)
        return

    def score_stream(s, carry):
        ikb = ikk_ref[s][:, :IDX_DIM].astype(BF16)
        iww = ikwq_ref[s][:, IDX_DIM:IDX_DIM + IDX_HEADS] * (IDX_DIM ** -0.5)
        acc = jnp.zeros((tq, n_keys), F32)
        for p in range(IDX_HEADS // 2):
            slab = iq_ref[s, :, p * LANES:(p + 1) * LANES]
            for hh in range(2):
                h = 2 * p + hh
                acc = acc + jnp.maximum(_nt(slab[:, hh * IDX_DIM:(hh + 1) * IDX_DIM], ikb), 0.0) * iww[:, h:h + 1]
        w_ref[pl.ds(pl.multiple_of(s * tq, tq), tq), :] = jnp.where(col < limit, acc, neg_inf)
        return carry

    lax.fori_loop(0, group, score_stream, 0)

    sc = w_ref[...]
    pos = jnp.sum(jnp.where(sc >= 0.0, 1.0, 0.0), axis=1, keepdims=True) >= nsel_f
    kk = jnp.where(pos, nsel_f, float(n_keys - n_sel + 1))
    w_ref[...] = jnp.where(pos, sc, -sc)

    def bit_step(i, u):
        cand_u = u | jnp.left_shift(jnp.int32(1), 30 - i)
        cand = pltpu.bitcast(cand_u, F32)
        cnt = jnp.sum(jnp.where(w_ref[...] >= cand, 1.0, 0.0), axis=1, keepdims=True)
        return jnp.where(cnt >= kk, cand_u, u)

    mag_u = lax.fori_loop(0, 31, bit_step, jnp.zeros((rows, 1), I32), unroll=SEARCH_UNROLL)
    mag = pltpu.bitcast(mag_u, F32)
    thr = jnp.where(pos, mag, -mag)

    sc = jnp.where(pos, w_ref[...], -w_ref[...])
    short = jnp.sum(jnp.where(sc >= thr, 1.0, 0.0), axis=1, keepdims=True) < nsel_f
    thr = jnp.where(jnp.logical_and(short, jnp.logical_not(pos)), -pltpu.bitcast(mag_u + 1, F32), thr)
    ge = sc >= thr
    cnt_gt = jnp.sum(jnp.where(sc > thr, 1.0, 0.0), axis=1, keepdims=True)
    cnt_ge = jnp.sum(jnp.where(ge, 1.0, 0.0), axis=1, keepdims=True)
    bias_ref[...] = jnp.where(jnp.logical_and(ge, sc > neg_inf), 0.0, neg_inf)
    excess = jnp.logical_and(cnt_ge > nsel_f, thr > neg_inf)

    @pl.when(jnp.max(jnp.where(excess, 1.0, 0.0)) > 0.0)
    def _():
        need = nsel_f - cnt_gt
        tri = jnp.where(lax.broadcasted_iota(I32, (LANES, LANES), 0) < lax.broadcasted_iota(I32, (LANES, LANES), 1),
                        1.0, 0.0).astype(BF16)
        before = jnp.zeros((rows, 1), F32)
        for b in range(n_keys // LANES):
            sl = slice(b * LANES, (b + 1) * LANES)
            sblk = jnp.where(pos, w_ref[:, sl], -w_ref[:, sl])
            eq = jnp.where(sblk == thr, 1.0, 0.0)
            rank = _mm(eq.astype(BF16), tri) + before
            keep = jnp.logical_or(sblk > thr, jnp.logical_and(sblk == thr, rank < need))
            bias_ref[:, sl] = jnp.where(jnp.logical_and(keep, sblk > neg_inf), 0.0, neg_inf)
            before = before + jnp.sum(eq, axis=1, keepdims=True)

    lax.fori_loop(0, group, attend_stream, 0)


def _dsa_call(aq, iq, ikw, kk, vv, ikk, group, tq, jq, n_keys, limit_const, n_sel, name, prev=None):
    n_streams, rows_per_stream = aq.shape[0], aq.shape[1]
    extra_specs, extra_args, aliases = [], (), {}
    if prev is not None:
        extra_specs, extra_args, aliases = [pl.BlockSpec(memory_space=pl.ANY)], (prev,), {6: 0}
    qrow = lambda a: pl.BlockSpec((group, tq, a.shape[2]), lambda s: (s, jq, 0))
    krow = lambda a: pl.BlockSpec((group, n_keys, a.shape[2]), lambda s: (s, 0, 0))
    width = DSA_HEADS * DSA_HEAD_DIM
    return pl.pallas_call(
        functools.partial(_dsa_kernel, group=group, tq=tq, n_keys=n_keys, row0=jq * tq, limit_const=limit_const,
                          n_sel=n_sel),
        grid=(n_streams // group,),
        in_specs=[qrow(aq), qrow(iq), qrow(ikw), krow(kk), krow(vv), krow(ikk)] + extra_specs,
        out_specs=pl.BlockSpec((group, tq, width), lambda s: (s, jq, 0)),
        out_shape=jax.ShapeDtypeStruct((n_streams, rows_per_stream, width), BF16),
        input_output_aliases=aliases,
        scratch_shapes=[pltpu.VMEM((group * tq, n_keys), F32), pltpu.VMEM((group * tq, n_keys), F32)],
        compiler_params=pltpu.CompilerParams(dimension_semantics=("parallel",), vmem_limit_bytes=56 * MIB),
        name=name,
    )(aq, iq, ikw, kk, vv, ikk, *extra_args)


def _dsa_cached_kernel(aq_ref, iq_ref, ikwq_ref, ck_ref, cv_ref, cik_ref, nk_ref, nv_ref, o_ref,
                       w_ref, bias_ref, kk_ref, vv_ref, ikk_ref, **static):
    past, new = ck_ref.shape[1], nk_ref.shape[1]
    for dst, cache, fresh in ((kk_ref, ck_ref, nk_ref), (vv_ref, cv_ref, nv_ref), (ikk_ref, cik_ref, ikwq_ref)):
        dst[:, :past, :cache.shape[2]] = cache[...]
        dst[:, past:past + new, :] = fresh[...]
        dst[:, past + new:, :] = jnp.zeros((dst.shape[0], dst.shape[1] - past - new, dst.shape[2]), F32)
    _dsa_kernel(aq_ref, iq_ref, ikwq_ref, kk_ref, vv_ref, ikk_ref, o_ref, w_ref, bias_ref, **static)


def _dsa_cached(aq, iq, ikw, cache_k, cache_v, cache_ik, new_k, new_v, tq):
    n_streams, past, _ = cache_k.shape
    n = aq.shape[0]
    per_stream = lambda a: a.reshape(n_streams, tq, a.shape[1])
    aq, iq, ikw, new_k, new_v = per_stream(aq), per_stream(iq), per_stream(ikw), per_stream(new_k), per_stream(new_v)
    limit = past + tq
    n_keys = -(-limit // LANES) * LANES
    group = max(1, min(n_streams, DSA_ROWS // tq, DSA_GROUP_MAX))
    blk = lambda a: pl.BlockSpec((group,) + a.shape[1:], lambda s: (s, 0, 0))
    width = DSA_HEADS * DSA_HEAD_DIM
    return pl.pallas_call(
        functools.partial(_dsa_cached_kernel, group=group, tq=tq, n_keys=n_keys, row0=0, limit_const=limit,
                          n_sel=min(DSA_TOPK, limit // 4)),
        grid=(n_streams // group,),
        in_specs=[blk(a) for a in (aq, iq, ikw, cache_k, cache_v, cache_ik, new_k, new_v)],
        out_specs=pl.BlockSpec((group, tq, width), lambda s: (s, 0, 0)),
        out_shape=jax.ShapeDtypeStruct((n_streams, tq, width), BF16),
        scratch_shapes=[pltpu.VMEM((group * tq, n_keys), F32), pltpu.VMEM((group * tq, n_keys), F32)] +
                       [pltpu.VMEM((group, n_keys, LANES), F32)] * 3,
        compiler_params=pltpu.CompilerParams(dimension_semantics=("parallel",), vmem_limit_bytes=48 * MIB),
        name="dsa_s",
    )(aq, iq, ikw, cache_k, cache_v, cache_ik, new_k, new_v).reshape(n, width)


def _dsa(aq, iq, ikw, kk, vv, ikk, tq, out):
    n_streams, n_keys, _ = kk.shape
    n = aq.shape[0]
    per_stream = lambda a: a.reshape(n_streams, n // n_streams, a.shape[1])
    aq, iq, ikw = per_stream(aq), per_stream(iq), per_stream(ikw)
    nq = n // n_streams // tq
    n_sel = min(DSA_TOPK, n_keys // 4)
    group = max(1, min(n_streams, DSA_ROWS // tq, DSA_GROUP_MAX))
    out = per_stream(out)
    for jq in range(nq):
        out = _dsa_call(aq, iq, ikw, kk, vv, ikk, group, tq, jq, (jq + 1) * tq, None, n_sel, f"dsa_p{jq}", prev=out)
    return out.reshape(n, -1)


def _pack_rows(v):
    q = QUARTER
    bits = lambda x: pltpu.bitcast(x.astype(BF16).astype(F32), I32)
    pair = lambda c: lax.shift_right_logical(bits(v[:, c * q:(c + 1) * q]), 16) | bits(v[:, (c + 1) * q:(c + 2) * q])
    return pair(0), pair(2)


def _unpack_rows(lo, hi):
    parts = []
    for w in (lo, hi):
        parts.append(pltpu.bitcast(lax.shift_left(w, 16), F32))
        parts.append(pltpu.bitcast(w & jnp.int32(-65536), F32))
    return jnp.concatenate(parts, axis=1)


def _layer_norm(v, g, b):
    mu = jnp.mean(v, axis=-1, keepdims=True)
    d = v - mu
    var = jnp.mean(d * d, axis=-1, keepdims=True)
    return d * lax.rsqrt(var + LN_EPS) * g + b


def _merge_kernel(x_ref, ret_ref, od_ref, gr_ref, ga_ref, wr_ref, wd_ref, wo_ref, g1_ref, b1_ref, rwt_ref, rb_ref,
                  tri_ref, cnt0_ref, h1_ref, hlo_ref, hhi_ref, gtm_ref, ek_ref, rk_ref, tot_ref, cnt_ref):
    @pl.when(pl.program_id(0) == 0)
    def _():
        cnt_ref[...] = cnt0_ref[:, 0:1]

    y_ret = _mm(ret_ref[...], wr_ref[...])
    y_dsa = _mm(od_ref[...], wd_ref[...])
    merged = jax.nn.sigmoid(gr_ref[...]) * y_ret + jax.nn.sigmoid(ga_ref[...]) * y_dsa
    mix = _mm(merged.astype(BF16), wo_ref[...])
    h1 = _layer_norm(DEEPNORM_ALPHA * x_ref[...] + mix, g1_ref[...], b1_ref[...])
    h1_ref[...] = h1
    hlo_ref[...], hhi_ref[...] = _pack_rows(h1)

    logits = lax.dot_general(rwt_ref[...], h1, (((1,), (1,)), ((), ())), preferred_element_type=F32,
                             precision=lax.Precision.HIGHEST) + rb_ref[...]
    tm = logits.shape[1]
    e_iota = lax.broadcasted_iota(I32, (N_EXPERTS, tm), 0)
    tops, hots, firsts = [], [], []
    for _ in range(MOE_TOP_K):
        m = jnp.max(logits, axis=0, keepdims=True)
        first = jnp.min(jnp.where(logits == m, e_iota, N_EXPERTS), axis=0, keepdims=True)
        hot = e_iota == first
        tops.append(m)
        hots.append(hot)
        firsts.append(first)
        logits = jnp.where(hot, -jnp.inf, logits)
    exps = [jnp.exp(m - tops[0]) for m in tops]
    den = exps[0] + exps[1] + exps[2] + exps[3]
    sel = jnp.zeros((N_EXPERTS, tm), F32)
    for hot in hots:
        sel = sel + jnp.where(hot, 1.0, 0.0)
    rank = _mm(sel.astype(BF16), tri_ref[...]) + cnt_ref[...]
    ranks = [jnp.sum(jnp.where(hot, rank, 0.0), axis=0, keepdims=True).astype(I32) for hot in hots]
    pad_i = jnp.zeros((8 - MOE_TOP_K, tm), I32)
    ek_ref[...] = jnp.concatenate(firsts + [pad_i], axis=0)
    rk_ref[...] = jnp.concatenate(ranks + [pad_i], axis=0)
    gates = jnp.concatenate([e / den for e in exps] + [jnp.zeros((LANES - MOE_TOP_K, tm), F32)], axis=0)
    gtm_ref[...] = gates.T
    cnt_ref[...] = cnt_ref[...] + jnp.sum(sel, axis=1, keepdims=True)
    tot_ref[...] = jnp.broadcast_to(cnt_ref[...], tot_ref.shape)


def _merge(x, ret, od, gr, ga, wr, wd, wo, g1, b1, rwt, rb, cnt0):
    n = x.shape[0]
    tm = MERGE_TM
    row = lambda w: pl.BlockSpec((tm, w), lambda i: (i, 0))
    const = lambda a: pl.BlockSpec(a.shape, lambda i: (0,) * a.ndim)
    col = pl.BlockSpec((8, tm), lambda i: (0, i))
    tri = jnp.asarray(np.triu(np.ones((tm, tm), np.float32), 1), BF16)
    return pl.pallas_call(
        _merge_kernel,
        grid=(n // tm,),
        in_specs=[row(D_MODEL), row(1024), row(512), row(1024), row(1024), const(wr), const(wd), const(wo),
                  const(g1), const(b1), const(rwt), const(rb), const(tri), const(cnt0)],
        out_specs=[row(D_MODEL), row(QUARTER), row(QUARTER), row(LANES), col, col,
                   pl.BlockSpec((N_EXPERTS, LANES), lambda i: (0, 0))],
        out_shape=[jax.ShapeDtypeStruct((n, D_MODEL), F32), jax.ShapeDtypeStruct((n, QUARTER), I32),
                   jax.ShapeDtypeStruct((n, QUARTER), I32), jax.ShapeDtypeStruct((n, LANES), F32),
                   jax.ShapeDtypeStruct((8, n), I32), jax.ShapeDtypeStruct((8, n), I32),
                   jax.ShapeDtypeStruct((N_EXPERTS, LANES), F32)],
        scratch_shapes=[pltpu.VMEM((N_EXPERTS, 1), F32)],
        compiler_params=pltpu.CompilerParams(dimension_semantics=("arbitrary",), vmem_limit_bytes=48 * MIB),
        name="merge",
    )(x, ret, od, gr, ga, wr, wd, wo, g1, b1, rwt, rb, tri, cnt0)


def _deinterleave_kernel(w_ref, o_ref):
    r = lax.broadcasted_iota(I32, (UP_BLOCK, UP_BLOCK), 0)
    c = lax.broadcasted_iota(I32, (UP_BLOCK, UP_BLOCK), 1)
    src = jnp.where(c < LANES, 2 * c, 2 * (c - LANES) + 1)
    perm = jnp.where(r == src, 1.0, 0.0).astype(BF16)
    for b in range(w_ref.shape[2] // UP_BLOCK):
        sl = slice(b * UP_BLOCK, (b + 1) * UP_BLOCK)
        o_ref[0, :, sl] = _mm(w_ref[0, :, sl].astype(BF16), perm).astype(BF16)


def _deinterleave_w_up(w_up):
    n_e, d_in, d_out = w_up.shape
    cols = d_out
    spec = pl.BlockSpec((1, d_in, cols), lambda e, c: (e, 0, c))
    return pl.pallas_call(
        _deinterleave_kernel,
        grid=(n_e, d_out // cols),
        in_specs=[spec],
        out_specs=spec,
        out_shape=jax.ShapeDtypeStruct(w_up.shape, BF16),
        compiler_params=pltpu.CompilerParams(dimension_semantics=("parallel", "parallel"),
                                             vmem_limit_bytes=40 * MIB),
        name="w_up_prep",
    )(w_up)


def _pos_kernel(off_ref, ek_ref, rk_ref, pos_ref):
    ek = ek_ref[...]
    pos = rk_ref[...]
    for e in range(N_EXPERTS):
        pos = pos + jnp.where(ek == e, off_ref[e], 0)
    pos_ref[...] = pos


def _positions(off, ek, rk):
    n = ek.shape[1]
    tn = min(n, 2048)
    spec = pl.BlockSpec((8, tn), lambda i, off: (0, i))
    return pl.pallas_call(
        _pos_kernel,
        grid_spec=pltpu.PrefetchScalarGridSpec(num_scalar_prefetch=1, grid=(n // tn,), in_specs=[spec, spec],
                                               out_specs=spec),
        out_shape=jax.ShapeDtypeStruct((8, n), I32),
        name="moe_pos",
    )(off, ek, rk)


def _sc_mesh():
    return plsc.VectorSubcoreMesh(core_axis_name="core", subcore_axis_name="subcore")


def _sc_scatter(xs, pos_flat, n_rows):
    width, dtype = xs[0].shape[1], xs[0].dtype
    windows = [x.shape[0] // SC_WINDOW for x in xs]
    nw = sum(windows)

    @functools.partial(pl.kernel, out_type=jax.ShapeDtypeStruct((n_rows, width), dtype), mesh=_sc_mesh())
    def scatter(*refs):
        p_hbm, o_hbm = refs[len(xs)], refs[len(xs) + 1]

        def body(x_vmem, *p_vmem):
            for p in p_vmem:
                pltpu.sync_copy(x_vmem, o_hbm.at[p.at[0]])

        first = 0
        for x_hbm, nw_g in zip(refs[:len(xs)], windows):
            pltpu.emit_pipeline(
                body, grid=(nw_g,),
                in_specs=[pl.BlockSpec((SC_WINDOW, width), lambda i: (i, 0))] +
                         [pl.BlockSpec((1, SC_WINDOW), functools.partial(lambda w0, i: (0, w0 + i), k * nw + first))
                          for k in range(MOE_TOP_K)],
                out_specs=[], core_axis_name=("core", "subcore"), dimension_semantics=(pltpu.PARALLEL,),
            )(x_hbm, *([p_hbm] * MOE_TOP_K))
            first += nw_g

    return scatter(*xs, pos_flat)


def _sc_gather(y, pos_flat):
    m = pos_flat.shape[1]

    @functools.partial(pl.kernel, out_type=jax.ShapeDtypeStruct((m, y.shape[1]), y.dtype), mesh=_sc_mesh())
    def gather(y_hbm, p_hbm, o_hbm):
        def body(p_vmem, o_vmem):
            pltpu.sync_copy(y_hbm.at[p_vmem.at[0]], o_vmem)

        pltpu.emit_pipeline(
            body, grid=(m // SC_WINDOW,),
            in_specs=[pl.BlockSpec((1, SC_WINDOW), lambda i: (0, i))],
            out_specs=[pl.BlockSpec((SC_WINDOW, y.shape[1]), lambda i: (i, 0))],
            core_axis_name=("core", "subcore"), dimension_semantics=(pltpu.PARALLEL,),
        )(p_hbm, o_hbm)

    return gather(y, pos_flat)


def _ffn_kernel(be_ref, nu_ref, xlo_ref, xhi_ref, wup_ref, bup_ref, wdn_ref, bdn_ref, ylo_ref, yhi_ref):
    @pl.when(pl.program_id(0) < nu_ref[0])
    def _():
        x = _unpack_rows(xlo_ref[...], xhi_ref[...]).astype(BF16)
        h = _mm(x, wup_ref[0]) + bup_ref[0]
        acts = []
        for b in range(2 * D_FF // UP_BLOCK):
            glu = jnp.minimum(h[:, b * UP_BLOCK:b * UP_BLOCK + LANES], SWIGLU_LIMIT)
            lin = jnp.clip(h[:, b * UP_BLOCK + LANES:(b + 1) * UP_BLOCK], -SWIGLU_LIMIT, SWIGLU_LIMIT)
            acts.append(glu * jax.nn.sigmoid(SWIGLU_ALPHA * glu) * (lin + 1.0))
        act = jnp.concatenate(acts, axis=1)
        ylo_ref[...], yhi_ref[...] = _pack_rows(_mm(act.astype(BF16), wdn_ref[0]) + bdn_ref[0])


def _ffn(block_expert, n_used, xs_lo, xs_hi, wup, bup, wdn, bdn):
    rows = xs_lo.shape[0]
    blk = FFN_BLOCK
    row = pl.BlockSpec((blk, QUARTER), lambda i, be, nu: (i, 0))
    per_expert = lambda a: pl.BlockSpec((1,) + a.shape[1:], lambda i, be, nu: (be[i], 0, 0))
    return pl.pallas_call(
        _ffn_kernel,
        grid_spec=pltpu.PrefetchScalarGridSpec(
            num_scalar_prefetch=2, grid=(rows // blk,),
            in_specs=[row, row, per_expert(wup), per_expert(bup), per_expert(wdn), per_expert(bdn)],
            out_specs=[row, row]),
        out_shape=[jax.ShapeDtypeStruct((rows, QUARTER), I32), jax.ShapeDtypeStruct((rows, QUARTER), I32)],
        compiler_params=pltpu.CompilerParams(dimension_semantics=("arbitrary",), vmem_limit_bytes=40 * MIB),
        name="moe_ffn",
    )(block_expert, n_used, xs_lo, xs_hi, wup, bup, wdn, bdn)


def _combine_kernel(h1_ref, olo_ref, ohi_ref, gtm_ref, g2_ref, b2_ref, o_ref):
    g = gtm_ref[...]
    y = jnp.zeros(h1_ref.shape, F32)
    for k in range(MOE_TOP_K):
        y = y + g[:, k:k + 1] * _unpack_rows(olo_ref[k], ohi_ref[k])
    o_ref[...] = _layer_norm(DEEPNORM_ALPHA * h1_ref[...] + y, g2_ref[...], b2_ref[...])


def _combine(h1, og_lo, og_hi, gtm, g2, b2, row0):
    n = h1.shape[0]
    n_all = og_lo.shape[0] // MOE_TOP_K
    tm = MERGE_TM
    row = lambda w: pl.BlockSpec((tm, w), lambda i: (i, 0))
    const = lambda a: pl.BlockSpec(a.shape, lambda i: (0,) * a.ndim)
    picked = pl.BlockSpec((MOE_TOP_K, tm, QUARTER), lambda i: (0, i + row0 // tm, 0))
    return pl.pallas_call(
        _combine_kernel,
        grid=(n // tm,),
        in_specs=[row(D_MODEL), picked, picked, row(LANES), const(g2), const(b2)],
        out_specs=row(D_MODEL),
        out_shape=jax.ShapeDtypeStruct((n, D_MODEL), F32),
        compiler_params=pltpu.CompilerParams(dimension_semantics=("parallel",), vmem_limit_bytes=40 * MIB),
        name="moe_combine",
    )(h1, og_lo.reshape(MOE_TOP_K, n_all, QUARTER), og_hi.reshape(MOE_TOP_K, n_all, QUARTER), gtm, g2, b2)


def _moe(groups, totals, wup, bup, wdn, bdn, g2, b2):
    blk = FFN_BLOCK
    sizes = [g[0].shape[0] for g in groups]
    n_all = sum(sizes)
    n_rows = MOE_TOP_K * n_all + N_EXPERTS * blk
    counts = totals[:, 0].astype(I32)
    padded = (counts + blk - 1) // blk * blk
    ends = jnp.cumsum(padded)
    block_start = jnp.arange(n_rows // blk, dtype=I32) * blk
    block_expert = jnp.minimum(jnp.sum((ends[None, :] <= block_start[:, None]).astype(I32), axis=1), N_EXPERTS - 1)
    n_used = (ends[-1:] // blk).astype(I32)
    cat = lambda i, axis: jnp.concatenate([g[i] for g in groups], axis=axis)
    pos = _positions((ends - padded).astype(I32), cat(4, 1), cat(5, 1))
    pos_flat = pos[:MOE_TOP_K].reshape(1, MOE_TOP_K * n_all)
    xs_lo = _sc_scatter([g[1] for g in groups], pos_flat, n_rows)
    xs_hi = _sc_scatter([g[2] for g in groups], pos_flat, n_rows)
    ys_lo, ys_hi = _ffn(block_expert, n_used, xs_lo, xs_hi, wup, bup, wdn, bdn)
    og_lo, og_hi = _sc_gather(ys_lo, pos_flat), _sc_gather(ys_hi, pos_flat)
    starts = np.cumsum([0] + sizes[:-1]).tolist()
    return [_combine(g[0], og_lo, og_hi, g[3], g2, b2, r0) for g, r0 in zip(groups, starts)]


def _mixer(x, pos_tab, tab_period, s0, rows_per_stream, ret_rows, caches, weights, cnt0):
    wp, wr, wd, wo, g1, b1, rwt, rb = weights
    n = x.shape[0]
    n_streams = n // rows_per_stream
    rq, rk, rv, rg, aq, ak, av, iq, ikw, gr, ga, od0 = _project(x, wp, pos_tab, tab_period)
    ret, s_new = _retention(rq, rk, rv, rg, s0, rows_per_stream, ret_rows)
    if caches is None:
        per_stream = lambda a: a.reshape(n_streams, rows_per_stream, a.shape[1])
        od = _dsa(aq, iq, ikw, per_stream(ak), per_stream(av), per_stream(ikw), DSA_TQ, od0)
    else:
        od = _dsa_cached(aq, iq, ikw, *caches, ak, av, rows_per_stream)
    h1, h_lo, h_hi, gtm, ek, rk_, totals = _merge(x, ret, od, gr, ga, wr, wd, wo, g1, b1, rwt, rb, cnt0)
    return (h1, h_lo, h_hi, gtm, ek, rk_), totals, (s_new, ak, av, ikw[:, :IDX_DIM])


def kernel(x_prompt, x_sample, state_ret, cache_k, cache_v, cache_idx_k, w_in, w_ret_o, w_dsa_o, w_o,
           ln1_g, ln1_b, router_w, router_b, w_up, b_up, w_down, b_down, ln2_g, ln2_b):
    assert w_in.shape[0] == DEPTH
    batch, seq, _ = x_prompt.shape
    dec_batch, dec_seq, _ = x_sample.shape
    past = cache_k.shape[2]
    assert seq % PROJ_TM == 0 and seq % DSA_TQ == 0 and PROJ_TM % dec_seq == 0

    l = 0
    mixer_w = (_pack_w_in(w_in[l]), w_ret_o[l].astype(BF16), w_dsa_o[l].astype(BF16), w_o[l].astype(BF16),
               ln1_g[l][None, :], ln1_b[l][None, :], router_w[l].T, router_b[l][:, None])
    moe_w = (
        _deinterleave_w_up(w_up[l]),
        b_up[l].reshape(N_EXPERTS, 2 * D_FF // UP_BLOCK, LANES, 2).transpose(0, 1, 3, 2).reshape(N_EXPERTS, 1, 2 * D_FF),
        w_down[l].astype(BF16), b_down[l][:, None, :], ln2_g[l][None, :], ln2_b[l][None, :])

    tab_p = _rot_tables(jnp.arange(seq))
    zeros_state = jnp.zeros((batch, RET_HEADS, RET_DK, RET_DV), F32)
    moe_p, totals_p, (s_p, k_p, v_p, ik_p) = _mixer(
        x_prompt.reshape(batch * seq, D_MODEL), tab_p, seq // PROJ_TM, zeros_state, seq, RET_CHUNK, None, mixer_w,
        jnp.zeros((N_EXPERTS, LANES), F32))

    tab_s = jnp.tile(_rot_tables(past + jnp.arange(dec_seq)), (PROJ_TM // dec_seq, 1))
    caches = (cache_k[l].reshape(dec_batch, past, LANES), cache_v[l].reshape(dec_batch, past, LANES), cache_idx_k[l])
    moe_s, totals, (s_s, k_s, v_s, ik_s) = _mixer(
        x_sample.reshape(dec_batch * dec_seq, D_MODEL), tab_s, 1, state_ret[l], dec_seq, dec_seq, caches, mixer_w,
        totals_p)

    y_p, y_s = _moe([moe_p, moe_s], totals, *moe_w)

    kv = (DSA_KV_HEADS, DSA_HEAD_DIM)
    return (y_p.reshape(batch, seq, D_MODEL), y_s.reshape(dec_batch, dec_seq, D_MODEL),
            s_p[None], k_p.reshape(1, batch, seq, *kv), v_p.reshape(1, batch, seq, *kv),
            ik_p.reshape(1, batch, seq, IDX_DIM),
            s_s[None], k_s.reshape(1, dec_batch, dec_seq, *kv), v_s.reshape(1, dec_batch, dec_seq, *kv),
            ik_s.reshape(1, dec_batch, dec_seq, IDX_DIM))
```

```python
import functools

import numpy as np
import jax
import jax.numpy as jnp
from jax import lax
from jax.experimental import pallas as pl
from jax.experimental.pallas import tpu as pltpu
from jax.experimental.pallas import tpu_sc as plsc

F32 = jnp.float32
BF16 = jnp.bfloat16
I32 = jnp.int32

D_MODEL = 1024
CHUNK = 64
RET_HEADS = 4
RET_DK = 128
RET_DV = 256
RET_ROPE_BASE = 10000.0
DSA_HEADS = 8
DSA_KV_HEADS = 2
DSA_HEAD_DIM = 64
IDX_HEADS = 8
IDX_DIM = 64
DSA_TOPK = 256
ROPE_THETA = 500000.0
N_EXPERTS = 32
MOE_TOP_K = 4
D_FF = 1024
SWIGLU_ALPHA = 1.702
SWIGLU_LIMIT = 7.0
LN_EPS = 1e-5
GN_EPS = 1e-6
DEPTH = 1
DEEPNORM_ALPHA = (2.0 * DEPTH) ** 0.25
PROJ_WIDTHS = (RET_HEADS * RET_DK, RET_HEADS * RET_DK, RET_HEADS * RET_DV, RET_HEADS * RET_DV,
               DSA_HEADS * DSA_HEAD_DIM, DSA_KV_HEADS * DSA_HEAD_DIM, DSA_KV_HEADS * DSA_HEAD_DIM,
               IDX_HEADS * IDX_DIM, IDX_DIM, IDX_HEADS, D_MODEL, D_MODEL)

assert CHUNK & (CHUNK - 1) == 0

LANES = 128
MIB = 1024 * 1024

OFF_RQ, OFF_RK, OFF_RV, OFF_RG = 0, 512, 1024, 2048
OFF_AQ, OFF_AK, OFF_AV, OFF_IQ, OFF_IKW = 3072, 3584, 3712, 3840, 4352
OFF_GR, OFF_GA, PACKED_COLS = 4480, 5504, 6528
TAB_COLS = 8 * LANES

PROJ_TM = 512
RET_CHUNK = 128
RET_CHUNKS_PER_STEP = 8
DSA_TQ = 256
DSA_ROWS = 512
DSA_GROUP_MAX = 4
SEARCH_UNROLL = 4
MERGE_TM = 512
FFN_BLOCK = 512
SC_WINDOW = 128
QUARTER = D_MODEL // 4
UP_BLOCK = 2 * LANES


def _nt(a, b):
    return lax.dot_general(a, b, (((1,), (1,)), ((), ())), preferred_element_type=F32)


def _mm(a, b):
    return jnp.dot(a, b, preferred_element_type=F32)


def _proj_kernel(x_ref, w_ref, tab_ref, rq_ref, rk_ref, rv_ref, rg_ref, aq_ref, ak_ref, av_ref,
                 iq_ref, ikw_ref, gr_ref, ga_ref, od0_ref):
    xb = x_ref[...].astype(BF16)
    od0_ref[...] = jnp.zeros(od0_ref.shape, od0_ref.dtype)

    def mm(c0, n):
        return _mm(xb, w_ref[:, c0:c0 + n])

    def tab(i):
        return tab_ref[:, i * LANES:(i + 1) * LANES]

    def rot_full(z):
        return z * tab(0) + pltpu.roll(z, 64, 1) * tab(1)

    def rot_part(z, c):
        return z * tab(c) + pltpu.roll(z, LANES - 8, 1) * tab(c + 1) + pltpu.roll(z, 8, 1) * tab(c + 2)

    z = mm(OFF_RQ, 512)
    for h in range(4):
        sl = slice(h * LANES, (h + 1) * LANES)
        rq_ref[:, sl] = rot_full(z[:, sl]).astype(BF16)
    z = mm(OFF_RK, 512)
    for h in range(4):
        sl = slice(h * LANES, (h + 1) * LANES)
        rk_ref[:, sl] = (rot_full(z[:, sl]) * (RET_DK ** -0.5)).astype(BF16)
    for c in range(2):
        rv_ref[:, c * 512:(c + 1) * 512] = mm(OFF_RV + c * 512, 512).astype(BF16)
    for c in range(2):
        rg_ref[:, c * 512:(c + 1) * 512] = mm(OFF_RG + c * 512, 512)
    z = mm(OFF_AQ, 512)
    for h in range(4):
        sl = slice(h * LANES, (h + 1) * LANES)
        aq_ref[:, sl] = rot_part(z[:, sl], 2).astype(BF16)
    z = mm(OFF_AK, 256)
    ak_ref[...] = rot_part(z[:, :LANES], 2)
    av_ref[...] = z[:, LANES:]
    z = mm(OFF_IQ, 512)
    for h in range(4):
        sl = slice(h * LANES, (h + 1) * LANES)
        iq_ref[:, sl] = rot_part(z[:, sl], 2).astype(BF16)
    ikw_ref[...] = rot_part(mm(OFF_IKW, LANES), 5)
    for c in range(2):
        gr_ref[:, c * 512:(c + 1) * 512] = mm(OFF_GR + c * 512, 512)
    for c in range(2):
        ga_ref[:, c * 512:(c + 1) * 512] = mm(OFF_GA + c * 512, 512)


def _rot_tables(pos):
    p = pos.shape[0]
    posf = pos.astype(F32)[:, None]
    ret_f = RET_ROPE_BASE ** (-jnp.linspace(0.0, 1.0, RET_DK // 2, dtype=F32))
    ang = posf * ret_f[None, :]
    c, s = jnp.cos(ang), jnp.sin(ang)
    cos_r = jnp.concatenate([c, c], 1)
    sin_r = jnp.concatenate([-s, s], 1)
    n_rot = DSA_HEAD_DIM // 4
    att_f = ROPE_THETA ** (-jnp.arange(0, n_rot, 2, dtype=F32) / n_rot)
    ang2 = posf * att_f[None, :]
    c2, s2 = jnp.cos(ang2), jnp.sin(ang2)
    half = n_rot // 2
    rest = DSA_HEAD_DIM - 2 * half
    c64 = jnp.concatenate([c2, c2, jnp.ones((p, rest), F32)], 1)
    s1_64 = jnp.concatenate([-s2, jnp.zeros((p, DSA_HEAD_DIM - half), F32)], 1)
    s2_64 = jnp.concatenate([jnp.zeros((p, half), F32), s2, jnp.zeros((p, rest), F32)], 1)
    z64 = jnp.zeros((p, DSA_HEAD_DIM), F32)
    ci = jnp.concatenate([c64, jnp.full((p, IDX_HEADS), IDX_HEADS ** -0.5, F32),
                          jnp.zeros((p, DSA_HEAD_DIM - IDX_HEADS), F32)], 1)
    return jnp.concatenate([cos_r, sin_r,
                            jnp.concatenate([c64, c64], 1), jnp.concatenate([s1_64, s1_64], 1),
                            jnp.concatenate([s2_64, s2_64], 1),
                            ci, jnp.concatenate([s1_64, z64], 1), jnp.concatenate([s2_64, z64], 1)], 1)


def _pack_w_in(w):
    cuts = np.cumsum(PROJ_WIDTHS)[:-1].tolist()
    rq, rk, rv, rg, aq, ak, av, iq, ik, iw, gr, ga = jnp.split(w, cuts, axis=1)
    pad = jnp.zeros((w.shape[0], LANES - IDX_DIM - IDX_HEADS), w.dtype)
    return jnp.concatenate([rq, rk, rv, rg, aq, ak, av, iq, ik, iw, pad, gr, ga], axis=1).astype(BF16)


def _project(x, wp, tab, tab_period):
    n = x.shape[0]
    tm = PROJ_TM
    row = lambda w: pl.BlockSpec((tm, w), lambda i: (i, 0))
    out_shapes = [((n, 512), BF16), ((n, 512), BF16), ((n, 1024), BF16), ((n, 1024), F32),
                  ((n, 512), BF16), ((n, LANES), F32), ((n, LANES), F32), ((n, 512), BF16),
                  ((n, LANES), F32), ((n, 1024), F32), ((n, 1024), F32), ((n, DSA_HEADS * DSA_HEAD_DIM), BF16)]
    return pl.pallas_call(
        _proj_kernel,
        grid=(n // tm,),
        in_specs=[row(D_MODEL),
                  pl.BlockSpec((D_MODEL, PACKED_COLS), lambda i: (0, 0), pipeline_mode=pl.Buffered(1)),
                  pl.BlockSpec((tm, TAB_COLS), lambda i: (i % tab_period, 0))],
        out_specs=[row(s[1]) for s, _ in out_shapes],
        out_shape=[jax.ShapeDtypeStruct(s, d) for s, d in out_shapes],
        compiler_params=pltpu.CompilerParams(dimension_semantics=("parallel",), vmem_limit_bytes=52 * MIB),
        name="proj",
    )(x, wp, tab)


def _ret_kernel(dec_ref, xi_ref, zeta_ref, rq_ref, rk_ref, rv_ref, rg_ref, s0_ref, ret_ref, sout_ref, st_ref,
                *, rows, n_chunk, g_pow):
    j = pl.program_id(1)
    cpad = RET_CHUNK

    @pl.when(j == 0)
    def _():
        st_ref[...] = s0_ref[0]

    def padded(v):
        if rows == cpad:
            return v
        return jnp.concatenate([v, jnp.zeros((cpad - rows, v.shape[1]), v.dtype)], axis=0)

    for c in range(n_chunk):
        rs = slice(c * rows, (c + 1) * rows)
        for h in range(RET_HEADS):
            ks = slice(h * RET_DK, (h + 1) * RET_DK)
            vs = slice(h * RET_DV, (h + 1) * RET_DV)
            q = padded(rq_ref[rs, ks])
            kt = padded(rk_ref[rs, ks].astype(F32)).T
            v = padded(rv_ref[rs, vs])
            s = st_ref[h]
            sc = _mm(q, kt.astype(BF16)) * dec_ref[h]
            o = _mm(sc.astype(BF16), v) + _mm(q, s.astype(BF16)) * xi_ref[h]
            st_ref[h] = g_pow[h] * s + _mm((kt * zeta_ref[h]).astype(BF16), v)
            o = o[:rows]
            mu = jnp.mean(o, axis=-1, keepdims=True)
            d = o - mu
            var = jnp.mean(d * d, axis=-1, keepdims=True)
            gn = d * lax.rsqrt(var + GN_EPS)
            g = rg_ref[rs, vs]
            ret_ref[rs, vs] = (gn * (g * jax.nn.sigmoid(g))).astype(BF16)

    @pl.when(j == pl.num_programs(1) - 1)
    def _():
        sout_ref[0] = st_ref[...]


def _retention(rq, rk, rv, rg, s0, rows_per_stream, rows):
    n = rq.shape[0]
    n_streams = n // rows_per_stream
    n_chunk = min(rows_per_stream // rows, RET_CHUNKS_PER_STEP)
    blk = rows * n_chunk
    nb = rows_per_stream // blk
    gam = 1.0 - 2.0 ** (-5.0 - np.arange(RET_HEADS, dtype=np.float64))
    i = np.arange(RET_CHUNK, dtype=np.float64)
    diff = i[:, None] - i[None, :]
    dec = np.where(diff >= 0, gam[:, None, None] ** np.maximum(diff, 0.0)[None], 0.0)
    xi = gam[:, None, None] ** (i + 1.0)[None, :, None]
    zeta = np.where(i < rows, gam[:, None, None] ** (rows - 1.0 - i)[None, None, :], 0.0)
    g_pow = tuple(float(g ** rows) for g in gam)
    const = lambda shape: pl.BlockSpec(shape, lambda s, j: (0,) * len(shape))
    row = lambda w: pl.BlockSpec((blk, w), lambda s, j: (s * nb + j, 0))
    st = pl.BlockSpec((1, RET_HEADS, RET_DK, RET_DV), lambda s, j: (s, 0, 0, 0))
    return pl.pallas_call(
        functools.partial(_ret_kernel, rows=rows, n_chunk=n_chunk, g_pow=g_pow),
        grid=(n_streams, nb),
        in_specs=[const((RET_HEADS, RET_CHUNK, RET_CHUNK)), const((RET_HEADS, RET_CHUNK, 1)),
                  const((RET_HEADS, 1, RET_CHUNK)), row(512), row(512), row(1024), row(1024), st],
        out_specs=[row(1024), st],
        out_shape=[jax.ShapeDtypeStruct((n, RET_HEADS * RET_DV), BF16),
                   jax.ShapeDtypeStruct((n_streams, RET_HEADS, RET_DK, RET_DV), F32)],
        scratch_shapes=[pltpu.VMEM((RET_HEADS, RET_DK, RET_DV), F32)],
        compiler_params=pltpu.CompilerParams(dimension_semantics=("parallel", "arbitrary"),
                                             vmem_limit_bytes=32 * MIB),
        name="retention",
    )(jnp.asarray(dec, F32), jnp.asarray(xi, F32), jnp.asarray(zeta, F32), rq, rk, rv, rg, s0)


def _dsa_kernel(aq_ref, iq_ref, ikwq_ref, kk_ref, vv_ref, ikk_ref, *rest,
                group, tq, n_keys, row0, limit_const, n_sel):
    o_ref, w_ref, bias_ref = rest[-3:]
    rows = group * tq
    nsel_f = float(n_sel)
    neg_inf = -jnp.inf

    col = lax.broadcasted_iota(I32, (tq, n_keys), 1)
    if limit_const is None:
        chunk_of_row = jnp.right_shift(lax.broadcasted_iota(I32, (tq, 1), 0) + row0, CHUNK.bit_length() - 1)
        limit = (chunk_of_row + 1) * CHUNK
    else:
        limit = limit_const

    def attend_stream(s, carry):
        bias = bias_ref[pl.ds(pl.multiple_of(s * tq, tq), tq), :]
        kfull = kk_ref[s]
        vfull = vv_ref[s]
        for g in range(DSA_KV_HEADS):
            gs = slice(g * DSA_HEAD_DIM, (g + 1) * DSA_HEAD_DIM)
            kg = kfull[:, gs].astype(BF16)
            vg = vfull[:, gs].astype(BF16)
            for pp in range(2):
                p = 2 * g + pp
                slab = (aq_ref[s, :, p * LANES:(p + 1) * LANES].astype(F32) * (DSA_HEAD_DIM ** -0.5)).astype(BF16)
                outs = []
                for hh in range(2):
                    lg = _nt(slab[:, hh * DSA_HEAD_DIM:(hh + 1) * DSA_HEAD_DIM], kg) + bias
                    m = jnp.max(lg, axis=1, keepdims=True)
                    pr = jnp.exp(lg - m)
                    den = jnp.sum(pr, axis=1, keepdims=True)
                    outs.append(_mm(pr.astype(BF16), vg) / den)
                o_ref[s, :, p * LANES:(p + 1) * LANES] = jnp.concatenate(outs, axis=1).astype(BF16)
        return carry

    if n_keys <= n_sel:
        for s in range(group):
            bias_ref[s * tq:(s + 1) * tq, :] = jnp.where(col < limit, 0.0, neg_inf)
        lax.fori_loop(0, group, attend_stream, 0)
        return

    def score_stream(s, carry):
        ikb = ikk_ref[s][:, :IDX_DIM].astype(BF16)
        iww = ikwq_ref[s][:, IDX_DIM:IDX_DIM + IDX_HEADS] * (IDX_DIM ** -0.5)
        acc = jnp.zeros((tq, n_keys), F32)
        for p in range(IDX_HEADS // 2):
            slab = iq_ref[s, :, p * LANES:(p + 1) * LANES]
            for hh in range(2):
                h = 2 * p + hh
                acc = acc + jnp.maximum(_nt(slab[:, hh * IDX_DIM:(hh + 1) * IDX_DIM], ikb), 0.0) * iww[:, h:h + 1]
        w_ref[pl.ds(pl.multiple_of(s * tq, tq), tq), :] = jnp.where(col < limit, acc, neg_inf)
        return carry

    lax.fori_loop(0, group, score_stream, 0)

    sc = w_ref[...]
    pos = jnp.sum(jnp.where(sc >= 0.0, 1.0, 0.0), axis=1, keepdims=True) >= nsel_f
    kk = jnp.where(pos, nsel_f, float(n_keys - n_sel + 1))
    w_ref[...] = jnp.where(pos, sc, -sc)

    def bit_step(i, u):
        cand_u = u | jnp.left_shift(jnp.int32(1), 30 - i)
        cand = pltpu.bitcast(cand_u, F32)
        cnt = jnp.sum(jnp.where(w_ref[...] >= cand, 1.0, 0.0), axis=1, keepdims=True)
        return jnp.where(cnt >= kk, cand_u, u)

    mag_u = lax.fori_loop(0, 31, bit_step, jnp.zeros((rows, 1), I32), unroll=SEARCH_UNROLL)
    mag = pltpu.bitcast(mag_u, F32)
    thr = jnp.where(pos, mag, -mag)

    sc = jnp.where(pos, w_ref[...], -w_ref[...])
    short = jnp.sum(jnp.where(sc >= thr, 1.0, 0.0), axis=1, keepdims=True) < nsel_f
    thr = jnp.where(jnp.logical_and(short, jnp.logical_not(pos)), -pltpu.bitcast(mag_u + 1, F32), thr)
    ge = sc >= thr
    cnt_gt = jnp.sum(jnp.where(sc > thr, 1.0, 0.0), axis=1, keepdims=True)
    cnt_ge = jnp.sum(jnp.where(ge, 1.0, 0.0), axis=1, keepdims=True)
    bias_ref[...] = jnp.where(jnp.logical_and(ge, sc > neg_inf), 0.0, neg_inf)
    excess = jnp.logical_and(cnt_ge > nsel_f, thr > neg_inf)

    @pl.when(jnp.max(jnp.where(excess, 1.0, 0.0)) > 0.0)
    def _():
        need = nsel_f - cnt_gt
        tri = jnp.where(lax.broadcasted_iota(I32, (LANES, LANES), 0) < lax.broadcasted_iota(I32, (LANES, LANES), 1),
                        1.0, 0.0).astype(BF16)
        before = jnp.zeros((rows, 1), F32)
        for b in range(n_keys // LANES):
            sl = slice(b * LANES, (b + 1) * LANES)
            sblk = jnp.where(pos, w_ref[:, sl], -w_ref[:, sl])
            eq = jnp.where(sblk == thr, 1.0, 0.0)
            rank = _mm(eq.astype(BF16), tri) + before
            keep = jnp.logical_or(sblk > thr, jnp.logical_and(sblk == thr, rank < need))
            bias_ref[:, sl] = jnp.where(jnp.logical_and(keep, sblk > neg_inf), 0.0, neg_inf)
            before = before + jnp.sum(eq, axis=1, keepdims=True)

    lax.fori_loop(0, group, attend_stream, 0)


def _dsa_call(aq, iq, ikw, kk, vv, ikk, group, tq, jq, n_keys, limit_const, n_sel, name, prev=None):
    n_streams, rows_per_stream = aq.shape[0], aq.shape[1]
    extra_specs, extra_args, aliases = [], (), {}
    if prev is not None:
        extra_specs, extra_args, aliases = [pl.BlockSpec(memory_space=pl.ANY)], (prev,), {6: 0}
    qrow = lambda a: pl.BlockSpec((group, tq, a.shape[2]), lambda s: (s, jq, 0))
    krow = lambda a: pl.BlockSpec((group, n_keys, a.shape[2]), lambda s: (s, 0, 0))
    width = DSA_HEADS * DSA_HEAD_DIM
    return pl.pallas_call(
        functools.partial(_dsa_kernel, group=group, tq=tq, n_keys=n_keys, row0=jq * tq, limit_const=limit_const,
                          n_sel=n_sel),
        grid=(n_streams // group,),
        in_specs=[qrow(aq), qrow(iq), qrow(ikw), krow(kk), krow(vv), krow(ikk)] + extra_specs,
        out_specs=pl.BlockSpec((group, tq, width), lambda s: (s, jq, 0)),
        out_shape=jax.ShapeDtypeStruct((n_streams, rows_per_stream, width), BF16),
        input_output_aliases=aliases,
        scratch_shapes=[pltpu.VMEM((group * tq, n_keys), F32), pltpu.VMEM((group * tq, n_keys), F32)],
        compiler_params=pltpu.CompilerParams(dimension_semantics=("parallel",), vmem_limit_bytes=56 * MIB),
        name=name,
    )(aq, iq, ikw, kk, vv, ikk, *extra_args)


def _dsa_cached_kernel(aq_ref, iq_ref, ikwq_ref, ck_ref, cv_ref, cik_ref, nk_ref, nv_ref, o_ref,
                       w_ref, bias_ref, kk_ref, vv_ref, ikk_ref, **static):
    past, new = ck_ref.shape[1], nk_ref.shape[1]
    for dst, cache, fresh in ((kk_ref, ck_ref, nk_ref), (vv_ref, cv_ref, nv_ref), (ikk_ref, cik_ref, ikwq_ref)):
        dst[:, :past, :cache.shape[2]] = cache[...]
        dst[:, past:past + new, :] = fresh[...]
        dst[:, past + new:, :] = jnp.zeros((dst.shape[0], dst.shape[1] - past - new, dst.shape[2]), F32)
    _dsa_kernel(aq_ref, iq_ref, ikwq_ref, kk_ref, vv_ref, ikk_ref, o_ref, w_ref, bias_ref, **static)


def _dsa_cached(aq, iq, ikw, cache_k, cache_v, cache_ik, new_k, new_v, tq):
    n_streams, past, _ = cache_k.shape
    n = aq.shape[0]
    per_stream = lambda a: a.reshape(n_streams, tq, a.shape[1])
    aq, iq, ikw, new_k, new_v = per_stream(aq), per_stream(iq), per_stream(ikw), per_stream(new_k), per_stream(new_v)
    limit = past + tq
    n_keys = -(-limit // LANES) * LANES
    group = max(1, min(n_streams, DSA_ROWS // tq, DSA_GROUP_MAX))
    blk = lambda a: pl.BlockSpec((group,) + a.shape[1:], lambda s: (s, 0, 0))
    width = DSA_HEADS * DSA_HEAD_DIM
    return pl.pallas_call(
        functools.partial(_dsa_cached_kernel, group=group, tq=tq, n_keys=n_keys, row0=0, limit_const=limit,
                          n_sel=min(DSA_TOPK, limit // 4)),
        grid=(n_streams // group,),
        in_specs=[blk(a) for a in (aq, iq, ikw, cache_k, cache_v, cache_ik, new_k, new_v)],
        out_specs=pl.BlockSpec((group, tq, width), lambda s: (s, 0, 0)),
        out_shape=jax.ShapeDtypeStruct((n_streams, tq, width), BF16),
        scratch_shapes=[pltpu.VMEM((group * tq, n_keys), F32), pltpu.VMEM((group * tq, n_keys), F32)] +
                       [pltpu.VMEM((group, n_keys, LANES), F32)] * 3,
        compiler_params=pltpu.CompilerParams(dimension_semantics=("parallel",), vmem_limit_bytes=48 * MIB),
        name="dsa_s",
    )(aq, iq, ikw, cache_k, cache_v, cache_ik, new_k, new_v).reshape(n, width)


def _dsa(aq, iq, ikw, kk, vv, ikk, tq, out):
    n_streams, n_keys, _ = kk.shape
    n = aq.shape[0]
    per_stream = lambda a: a.reshape(n_streams, n // n_streams, a.shape[1])
    aq, iq, ikw = per_stream(aq), per_stream(iq), per_stream(ikw)
    nq = n // n_streams // tq
    n_sel = min(DSA_TOPK, n_keys // 4)
    group = max(1, min(n_streams, DSA_ROWS // tq, DSA_GROUP_MAX))
    out = per_stream(out)
    for jq in range(nq):
        out = _dsa_call(aq, iq, ikw, kk, vv, ikk, group, tq, jq, (jq + 1) * tq, None, n_sel, f"dsa_p{jq}", prev=out)
    return out.reshape(n, -1)


def _pack_rows(v):
    q = QUARTER
    bits = lambda x: pltpu.bitcast(x.astype(BF16).astype(F32), I32)
    pair = lambda c: lax.shift_right_logical(bits(v[:, c * q:(c + 1) * q]), 16) | bits(v[:, (c + 1) * q:(c + 2) * q])
    return pair(0), pair(2)


def _unpack_rows(lo, hi):
    parts = []
    for w in (lo, hi):
        parts.append(pltpu.bitcast(lax.shift_left(w, 16), F32))
        parts.append(pltpu.bitcast(w & jnp.int32(-65536), F32))
    return jnp.concatenate(parts, axis=1)


def _layer_norm(v, g, b):
    mu = jnp.mean(v, axis=-1, keepdims=True)
    d = v - mu
    var = jnp.mean(d * d, axis=-1, keepdims=True)
    return d * lax.rsqrt(var + LN_EPS) * g + b


def _merge_kernel(x_ref, ret_ref, od_ref, gr_ref, ga_ref, wr_ref, wd_ref, wo_ref, g1_ref, b1_ref, rwt_ref, rb_ref,
                  tri_ref, cnt0_ref, h1_ref, hlo_ref, hhi_ref, gtm_ref, ek_ref, rk_ref, tot_ref, cnt_ref):
    @pl.when(pl.program_id(0) == 0)
    def _():
        cnt_ref[...] = cnt0_ref[:, 0:1]

    y_ret = _mm(ret_ref[...], wr_ref[...])
    y_dsa = _mm(od_ref[...], wd_ref[...])
    merged = jax.nn.sigmoid(gr_ref[...]) * y_ret + jax.nn.sigmoid(ga_ref[...]) * y_dsa
    mix = _mm(merged.astype(BF16), wo_ref[...])
    h1 = _layer_norm(DEEPNORM_ALPHA * x_ref[...] + mix, g1_ref[...], b1_ref[...])
    h1_ref[...] = h1
    hlo_ref[...], hhi_ref[...] = _pack_rows(h1)

    logits = lax.dot_general(rwt_ref[...], h1, (((1,), (1,)), ((), ())), preferred_element_type=F32,
                             precision=lax.Precision.HIGHEST) + rb_ref[...]
    tm = logits.shape[1]
    e_iota = lax.broadcasted_iota(I32, (N_EXPERTS, tm), 0)
    tops, hots, firsts = [], [], []
    for _ in range(MOE_TOP_K):
        m = jnp.max(logits, axis=0, keepdims=True)
        first = jnp.min(jnp.where(logits == m, e_iota, N_EXPERTS), axis=0, keepdims=True)
        hot = e_iota == first
        tops.append(m)
        hots.append(hot)
        firsts.append(first)
        logits = jnp.where(hot, -jnp.inf, logits)
    exps = [jnp.exp(m - tops[0]) for m in tops]
    den = exps[0] + exps[1] + exps[2] + exps[3]
    sel = jnp.zeros((N_EXPERTS, tm), F32)
    for hot in hots:
        sel = sel + jnp.where(hot, 1.0, 0.0)
    rank = _mm(sel.astype(BF16), tri_ref[...]) + cnt_ref[...]
    ranks = [jnp.sum(jnp.where(hot, rank, 0.0), axis=0, keepdims=True).astype(I32) for hot in hots]
    pad_i = jnp.zeros((8 - MOE_TOP_K, tm), I32)
    ek_ref[...] = jnp.concatenate(firsts + [pad_i], axis=0)
    rk_ref[...] = jnp.concatenate(ranks + [pad_i], axis=0)
    gates = jnp.concatenate([e / den for e in exps] + [jnp.zeros((LANES - MOE_TOP_K, tm), F32)], axis=0)
    gtm_ref[...] = gates.T
    cnt_ref[...] = cnt_ref[...] + jnp.sum(sel, axis=1, keepdims=True)
    tot_ref[...] = jnp.broadcast_to(cnt_ref[...], tot_ref.shape)


def _merge(x, ret, od, gr, ga, wr, wd, wo, g1, b1, rwt, rb, cnt0):
    n = x.shape[0]
    tm = MERGE_TM
    row = lambda w: pl.BlockSpec((tm, w), lambda i: (i, 0))
    const = lambda a: pl.BlockSpec(a.shape, lambda i: (0,) * a.ndim)
    col = pl.BlockSpec((8, tm), lambda i: (0, i))
    tri = jnp.asarray(np.triu(np.ones((tm, tm), np.float32), 1), BF16)
    return pl.pallas_call(
        _merge_kernel,
        grid=(n // tm,),
        in_specs=[row(D_MODEL), row(1024), row(512), row(1024), row(1024), const(wr), const(wd), const(wo),
                  const(g1), const(b1), const(rwt), const(rb), const(tri), const(cnt0)],
        out_specs=[row(D_MODEL), row(QUARTER), row(QUARTER), row(LANES), col, col,
                   pl.BlockSpec((N_EXPERTS, LANES), lambda i: (0, 0))],
        out_shape=[jax.ShapeDtypeStruct((n, D_MODEL), F32), jax.ShapeDtypeStruct((n, QUARTER), I32),
                   jax.ShapeDtypeStruct((n, QUARTER), I32), jax.ShapeDtypeStruct((n, LANES), F32),
                   jax.ShapeDtypeStruct((8, n), I32), jax.ShapeDtypeStruct((8, n), I32),
                   jax.ShapeDtypeStruct((N_EXPERTS, LANES), F32)],
        scratch_shapes=[pltpu.VMEM((N_EXPERTS, 1), F32)],
        compiler_params=pltpu.CompilerParams(dimension_semantics=("arbitrary",), vmem_limit_bytes=48 * MIB),
        name="merge",
    )(x, ret, od, gr, ga, wr, wd, wo, g1, b1, rwt, rb, tri, cnt0)


def _deinterleave_kernel(w_ref, o_ref):
    r = lax.broadcasted_iota(I32, (UP_BLOCK, UP_BLOCK), 0)
    c = lax.broadcasted_iota(I32, (UP_BLOCK, UP_BLOCK), 1)
    src = jnp.where(c < LANES, 2 * c, 2 * (c - LANES) + 1)
    perm = jnp.where(r == src, 1.0, 0.0).astype(BF16)
    for b in range(w_ref.shape[2] // UP_BLOCK):
        sl = slice(b * UP_BLOCK, (b + 1) * UP_BLOCK)
        o_ref[0, :, sl] = _mm(w_ref[0, :, sl].astype(BF16), perm).astype(BF16)


def _deinterleave_w_up(w_up):
    n_e, d_in, d_out = w_up.shape
    cols = d_out
    spec = pl.BlockSpec((1, d_in, cols), lambda e, c: (e, 0, c))
    return pl.pallas_call(
        _deinterleave_kernel,
        grid=(n_e, d_out // cols),
        in_specs=[spec],
        out_specs=spec,
        out_shape=jax.ShapeDtypeStruct(w_up.shape, BF16),
        compiler_params=pltpu.CompilerParams(dimension_semantics=("parallel", "parallel"),
                                             vmem_limit_bytes=40 * MIB),
        name="w_up_prep",
    )(w_up)


def _pos_kernel(off_ref, ek_ref, rk_ref, pos_ref):
    ek = ek_ref[...]
    pos = rk_ref[...]
    for e in range(N_EXPERTS):
        pos = pos + jnp.where(ek == e, off_ref[e], 0)
    pos_ref[...] = pos


def _positions(off, ek, rk):
    n = ek.shape[1]
    tn = min(n, 2048)
    spec = pl.BlockSpec((8, tn), lambda i, off: (0, i))
    return pl.pallas_call(
        _pos_kernel,
        grid_spec=pltpu.PrefetchScalarGridSpec(num_scalar_prefetch=1, grid=(n // tn,), in_specs=[spec, spec],
                                               out_specs=spec),
        out_shape=jax.ShapeDtypeStruct((8, n), I32),
        name="moe_pos",
    )(off, ek, rk)


def _sc_mesh():
    return plsc.VectorSubcoreMesh(core_axis_name="core", subcore_axis_name="subcore")


def _sc_scatter(xs, pos_flat, n_rows):
    width, dtype = xs[0].shape[1], xs[0].dtype
    windows = [x.shape[0] // SC_WINDOW for x in xs]
    nw = sum(windows)

    @functools.partial(pl.kernel, out_type=jax.ShapeDtypeStruct((n_rows, width), dtype), mesh=_sc_mesh())
    def scatter(*refs):
        p_hbm, o_hbm = refs[len(xs)], refs[len(xs) + 1]

        def body(x_vmem, *p_vmem):
            for p in p_vmem:
                pltpu.sync_copy(x_vmem, o_hbm.at[p.at[0]])

        first = 0
        for x_hbm, nw_g in zip(refs[:len(xs)], windows):
            pltpu.emit_pipeline(
                body, grid=(nw_g,),
                in_specs=[pl.BlockSpec((SC_WINDOW, width), lambda i: (i, 0))] +
                         [pl.BlockSpec((1, SC_WINDOW), functools.partial(lambda w0, i: (0, w0 + i), k * nw + first))
                          for k in range(MOE_TOP_K)],
                out_specs=[], core_axis_name=("core", "subcore"), dimension_semantics=(pltpu.PARALLEL,),
            )(x_hbm, *([p_hbm] * MOE_TOP_K))
            first += nw_g

    return scatter(*xs, pos_flat)


def _sc_gather(y, pos_flat):
    m = pos_flat.shape[1]

    @functools.partial(pl.kernel, out_type=jax.ShapeDtypeStruct((m, y.shape[1]), y.dtype), mesh=_sc_mesh())
    def gather(y_hbm, p_hbm, o_hbm):
        def body(p_vmem, o_vmem):
            pltpu.sync_copy(y_hbm.at[p_vmem.at[0]], o_vmem)

        pltpu.emit_pipeline(
            body, grid=(m // SC_WINDOW,),
            in_specs=[pl.BlockSpec((1, SC_WINDOW), lambda i: (0, i))],
            out_specs=[pl.BlockSpec((SC_WINDOW, y.shape[1]), lambda i: (i, 0))],
            core_axis_name=("core", "subcore"), dimension_semantics=(pltpu.PARALLEL,),
        )(p_hbm, o_hbm)

    return gather(y, pos_flat)


def _ffn_kernel(be_ref, nu_ref, xlo_ref, xhi_ref, wup_ref, bup_ref, wdn_ref, bdn_ref, ylo_ref, yhi_ref):
    @pl.when(pl.program_id(0) < nu_ref[0])
    def _():
        x = _unpack_rows(xlo_ref[...], xhi_ref[...]).astype(BF16)
        h = _mm(x, wup_ref[0]) + bup_ref[0]
        acts = []
        for b in range(2 * D_FF // UP_BLOCK):
            glu = jnp.minimum(h[:, b * UP_BLOCK:b * UP_BLOCK + LANES], SWIGLU_LIMIT)
            lin = jnp.clip(h[:, b * UP_BLOCK + LANES:(b + 1) * UP_BLOCK], -SWIGLU_LIMIT, SWIGLU_LIMIT)
            acts.append(glu * jax.nn.sigmoid(SWIGLU_ALPHA * glu) * (lin + 1.0))
        act = jnp.concatenate(acts, axis=1)
        ylo_ref[...], yhi_ref[...] = _pack_rows(_mm(act.astype(BF16), wdn_ref[0]) + bdn_ref[0])


def _ffn(block_expert, n_used, xs_lo, xs_hi, wup, bup, wdn, bdn):
    rows = xs_lo.shape[0]
    blk = FFN_BLOCK
    row = pl.BlockSpec((blk, QUARTER), lambda i, be, nu: (i, 0))
    per_expert = lambda a: pl.BlockSpec((1,) + a.shape[1:], lambda i, be, nu: (be[i], 0, 0))
    return pl.pallas_call(
        _ffn_kernel,
        grid_spec=pltpu.PrefetchScalarGridSpec(
            num_scalar_prefetch=2, grid=(rows // blk,),
            in_specs=[row, row, per_expert(wup), per_expert(bup), per_expert(wdn), per_expert(bdn)],
            out_specs=[row, row]),
        out_shape=[jax.ShapeDtypeStruct((rows, QUARTER), I32), jax.ShapeDtypeStruct((rows, QUARTER), I32)],
        compiler_params=pltpu.CompilerParams(dimension_semantics=("arbitrary",), vmem_limit_bytes=40 * MIB),
        name="moe_ffn",
    )(block_expert, n_used, xs_lo, xs_hi, wup, bup, wdn, bdn)


def _combine_kernel(h1_ref, olo_ref, ohi_ref, gtm_ref, g2_ref, b2_ref, o_ref):
    g = gtm_ref[...]
    y = jnp.zeros(h1_ref.shape, F32)
    for k in range(MOE_TOP_K):
        y = y + g[:, k:k + 1] * _unpack_rows(olo_ref[k], ohi_ref[k])
    o_ref[...] = _layer_norm(DEEPNORM_ALPHA * h1_ref[...] + y, g2_ref[...], b2_ref[...])


def _combine(h1, og_lo, og_hi, gtm, g2, b2, row0):
    n = h1.shape[0]
    n_all = og_lo.shape[0] // MOE_TOP_K
    tm = MERGE_TM
    row = lambda w: pl.BlockSpec((tm, w), lambda i: (i, 0))
    const = lambda a: pl.BlockSpec(a.shape, lambda i: (0,) * a.ndim)
    picked = pl.BlockSpec((MOE_TOP_K, tm, QUARTER), lambda i: (0, i + row0 // tm, 0))
    return pl.pallas_call(
        _combine_kernel,
        grid=(n // tm,),
        in_specs=[row(D_MODEL), picked, picked, row(LANES), const(g2), const(b2)],
        out_specs=row(D_MODEL),
        out_shape=jax.ShapeDtypeStruct((n, D_MODEL), F32),
        compiler_params=pltpu.CompilerParams(dimension_semantics=("parallel",), vmem_limit_bytes=40 * MIB),
        name="moe_combine",
    )(h1, og_lo.reshape(MOE_TOP_K, n_all, QUARTER), og_hi.reshape(MOE_TOP_K, n_all, QUARTER), gtm, g2, b2)


def _moe(groups, totals, wup, bup, wdn, bdn, g2, b2):
    blk = FFN_BLOCK
    sizes = [g[0].shape[0] for g in groups]
    n_all = sum(sizes)
    n_rows = MOE_TOP_K * n_all + N_EXPERTS * blk
    counts = totals[:, 0].astype(I32)
    padded = (counts + blk - 1) // blk * blk
    ends = jnp.cumsum(padded)
    block_start = jnp.arange(n_rows // blk, dtype=I32) * blk
    block_expert = jnp.minimum(jnp.sum((ends[None, :] <= block_start[:, None]).astype(I32), axis=1), N_EXPERTS - 1)
    n_used = (ends[-1:] // blk).astype(I32)
    cat = lambda i, axis: jnp.concatenate([g[i] for g in groups], axis=axis)
    pos = _positions((ends - padded).astype(I32), cat(4, 1), cat(5, 1))
    pos_flat = pos[:MOE_TOP_K].reshape(1, MOE_TOP_K * n_all)
    xs_lo = _sc_scatter([g[1] for g in groups], pos_flat, n_rows)
    xs_hi = _sc_scatter([g[2] for g in groups], pos_flat, n_rows)
    ys_lo, ys_hi = _ffn(block_expert, n_used, xs_lo, xs_hi, wup, bup, wdn, bdn)
    og_lo, og_hi = _sc_gather(ys_lo, pos_flat), _sc_gather(ys_hi, pos_flat)
    starts = np.cumsum([0] + sizes[:-1]).tolist()
    return [_combine(g[0], og_lo, og_hi, g[3], g2, b2, r0) for g, r0 in zip(groups, starts)]


def _mixer(x, pos_tab, tab_period, s0, rows_per_stream, ret_rows, caches, weights, cnt0):
    wp, wr, wd, wo, g1, b1, rwt, rb = weights
    n = x.shape[0]
    n_streams = n // rows_per_stream
    rq, rk, rv, rg, aq, ak, av, iq, ikw, gr, ga, od0 = _project(x, wp, pos_tab, tab_period)
    ret, s_new = _retention(rq, rk, rv, rg, s0, rows_per_stream, ret_rows)
    if caches is None:
        per_stream = lambda a: a.reshape(n_streams, rows_per_stream, a.shape[1])
        od = _dsa(aq, iq, ikw, per_stream(ak), per_stream(av), per_stream(ikw), DSA_TQ, od0)
    else:
        od = _dsa_cached(aq, iq, ikw, *caches, ak, av, rows_per_stream)
    h1, h_lo, h_hi, gtm, ek, rk_, totals = _merge(x, ret, od, gr, ga, wr, wd, wo, g1, b1, rwt, rb, cnt0)
    return (h1, h_lo, h_hi, gtm, ek, rk_), totals, (s_new, ak, av, ikw[:, :IDX_DIM])


def kernel(x_prompt, x_sample, state_ret, cache_k, cache_v, cache_idx_k, w_in, w_ret_o, w_dsa_o, w_o,
           ln1_g, ln1_b, router_w, router_b, w_up, b_up, w_down, b_down, ln2_g, ln2_b):
    assert w_in.shape[0] == DEPTH
    batch, seq, _ = x_prompt.shape
    dec_batch, dec_seq, _ = x_sample.shape
    past = cache_k.shape[2]
    assert seq % PROJ_TM == 0 and seq % DSA_TQ == 0 and PROJ_TM % dec_seq == 0

    l = 0
    mixer_w = (_pack_w_in(w_in[l]), w_ret_o[l].astype(BF16), w_dsa_o[l].astype(BF16), w_o[l].astype(BF16),
               ln1_g[l][None, :], ln1_b[l][None, :], router_w[l].T, router_b[l][:, None])
    moe_w = (
        _deinterleave_w_up(w_up[l]),
        b_up[l].reshape(N_EXPERTS, 2 * D_FF // UP_BLOCK, LANES, 2).transpose(0, 1, 3, 2).reshape(N_EXPERTS, 1, 2 * D_FF),
        w_down[l].astype(BF16), b_down[l][:, None, :], ln2_g[l][None, :], ln2_b[l][None, :])

    tab_p = _rot_tables(jnp.arange(seq))
    zeros_state = jnp.zeros((batch, RET_HEADS, RET_DK, RET_DV), F32)
    moe_p, totals_p, (s_p, k_p, v_p, ik_p) = _mixer(
        x_prompt.reshape(batch * seq, D_MODEL), tab_p, seq // PROJ_TM, zeros_state, seq, RET_CHUNK, None, mixer_w,
        jnp.zeros((N_EXPERTS, LANES), F32))

    tab_s = jnp.tile(_rot_tables(past + jnp.arange(dec_seq)), (PROJ_TM // dec_seq, 1))
    caches = (cache_k[l].reshape(dec_batch, past, LANES), cache_v[l].reshape(dec_batch, past, LANES), cache_idx_k[l])
    moe_s, totals, (s_s, k_s, v_s, ik_s) = _mixer(
        x_sample.reshape(dec_batch * dec_seq, D_MODEL), tab_s, 1, state_ret[l], dec_seq, dec_seq, caches, mixer_w,
        totals_p)

    y_p, y_s = _moe([moe_p, moe_s], totals, *moe_w)

    kv = (DSA_KV_HEADS, DSA_HEAD_DIM)
    return (y_p.reshape(batch, seq, D_MODEL), y_s.reshape(dec_batch, dec_seq, D_MODEL),
            s_p[None], k_p.reshape(1, batch, seq, *kv), v_p.reshape(1, batch, seq, *kv),
            ik_p.reshape(1, batch, seq, IDX_DIM),
            s_s[None], k_s.reshape(1, dec_batch, dec_seq, *kv), v_s.reshape(1, dec_batch, dec_seq, *kv),
            ik_s.reshape(1, dec_batch, dec_seq, IDX_DIM))
```

```python
import functools

import numpy as np
import jax
import jax.numpy as jnp
from jax import lax
from jax.experimental import pallas as pl
from jax.experimental.pallas import tpu as pltpu
from jax.experimental.pallas import tpu_sc as plsc

F32 = jnp.float32
BF16 = jnp.bfloat16
I32 = jnp.int32

D_MODEL = 1024
CHUNK = 64
RET_HEADS = 4
RET_DK = 128
RET_DV = 256
RET_ROPE_BASE = 10000.0
DSA_HEADS = 8
DSA_KV_HEADS = 2
DSA_HEAD_DIM = 64
IDX_HEADS = 8
IDX_DIM = 64
DSA_TOPK = 256
ROPE_THETA = 500000.0
N_EXPERTS = 32
MOE_TOP_K = 4
D_FF = 1024
SWIGLU_ALPHA = 1.702
SWIGLU_LIMIT = 7.0
LN_EPS = 1e-5
GN_EPS = 1e-6
DEPTH = 1
DEEPNORM_ALPHA = (2.0 * DEPTH) ** 0.25
PROJ_WIDTHS = (RET_HEADS * RET_DK, RET_HEADS * RET_DK, RET_HEADS * RET_DV, RET_HEADS * RET_DV,
               DSA_HEADS * DSA_HEAD_DIM, DSA_KV_HEADS * DSA_HEAD_DIM, DSA_KV_HEADS * DSA_HEAD_DIM,
               IDX_HEADS * IDX_DIM, IDX_DIM, IDX_HEADS, D_MODEL, D_MODEL)

assert CHUNK & (CHUNK - 1) == 0

LANES = 128
MIB = 1024 * 1024

OFF_RQ, OFF_RK, OFF_RV, OFF_RG = 0, 512, 1024, 2048
OFF_AQ, OFF_AK, OFF_AV, OFF_IQ, OFF_IKW = 3072, 3584, 3712, 3840, 4352
OFF_GR, OFF_GA, PACKED_COLS = 4480, 5504, 6528
TAB_COLS = 8 * LANES

PROJ_TM = 512
RET_CHUNK = 128
RET_CHUNKS_PER_STEP = 8
DSA_TQ = 256
DSA_ROWS = 512
DSA_GROUP_MAX = 4
SEARCH_UNROLL = 4
DSA_SHORT_EXTENT = 1280
MERGE_TM = 512
FFN_BLOCK = 512
SC_WINDOW = 128
QUARTER = D_MODEL // 4
UP_BLOCK = 2 * LANES


def _nt(a, b):
    return lax.dot_general(a, b, (((1,), (1,)), ((), ())), preferred_element_type=F32)


def _mm(a, b):
    return jnp.dot(a, b, preferred_element_type=F32)


def _proj_kernel(x_ref, w_ref, tab_ref, rq_ref, rk_ref, rv_ref, rg_ref, aq_ref, ak_ref, av_ref,
                 iq_ref, ikw_ref, gr_ref, ga_ref, od0_ref):
    xb = x_ref[...].astype(BF16)
    od0_ref[...] = jnp.zeros(od0_ref.shape, od0_ref.dtype)

    def mm(c0, n):
        return _mm(xb, w_ref[:, c0:c0 + n])

    def tab(i):
        return tab_ref[:, i * LANES:(i + 1) * LANES]

    def rot_full(z):
        return z * tab(0) + pltpu.roll(z, 64, 1) * tab(1)

    def rot_part(z, c):
        return z * tab(c) + pltpu.roll(z, LANES - 8, 1) * tab(c + 1) + pltpu.roll(z, 8, 1) * tab(c + 2)

    z = mm(OFF_RQ, 512)
    for h in range(4):
        sl = slice(h * LANES, (h + 1) * LANES)
        rq_ref[:, sl] = rot_full(z[:, sl]).astype(BF16)
    z = mm(OFF_RK, 512)
    for h in range(4):
        sl = slice(h * LANES, (h + 1) * LANES)
        rk_ref[:, sl] = (rot_full(z[:, sl]) * (RET_DK ** -0.5)).astype(BF16)
    for c in range(2):
        rv_ref[:, c * 512:(c + 1) * 512] = mm(OFF_RV + c * 512, 512).astype(BF16)
    for c in range(2):
        rg_ref[:, c * 512:(c + 1) * 512] = mm(OFF_RG + c * 512, 512)
    z = mm(OFF_AQ, 512)
    for h in range(4):
        sl = slice(h * LANES, (h + 1) * LANES)
        aq_ref[:, sl] = rot_part(z[:, sl], 2).astype(BF16)
    z = mm(OFF_AK, 256)
    ak_ref[...] = rot_part(z[:, :LANES], 2)
    av_ref[...] = z[:, LANES:]
    z = mm(OFF_IQ, 512)
    for h in range(4):
        sl = slice(h * LANES, (h + 1) * LANES)
        iq_ref[:, sl] = rot_part(z[:, sl], 2).astype(BF16)
    ikw_ref[...] = rot_part(mm(OFF_IKW, LANES), 5)
    for c in range(2):
        gr_ref[:, c * 512:(c + 1) * 512] = mm(OFF_GR + c * 512, 512)
    for c in range(2):
        ga_ref[:, c * 512:(c + 1) * 512] = mm(OFF_GA + c * 512, 512)


def _rot_tables(pos):
    p = pos.shape[0]
    posf = pos.astype(F32)[:, None]
    ret_f = RET_ROPE_BASE ** (-jnp.linspace(0.0, 1.0, RET_DK // 2, dtype=F32))
    ang = posf * ret_f[None, :]
    c, s = jnp.cos(ang), jnp.sin(ang)
    cos_r = jnp.concatenate([c, c], 1)
    sin_r = jnp.concatenate([-s, s], 1)
    n_rot = DSA_HEAD_DIM // 4
    att_f = ROPE_THETA ** (-jnp.arange(0, n_rot, 2, dtype=F32) / n_rot)
    ang2 = posf * att_f[None, :]
    c2, s2 = jnp.cos(ang2), jnp.sin(ang2)
    half = n_rot // 2
    rest = DSA_HEAD_DIM - 2 * half
    c64 = jnp.concatenate([c2, c2, jnp.ones((p, rest), F32)], 1)
    s1_64 = jnp.concatenate([-s2, jnp.zeros((p, DSA_HEAD_DIM - half), F32)], 1)
    s2_64 = jnp.concatenate([jnp.zeros((p, half), F32), s2, jnp.zeros((p, rest), F32)], 1)
    z64 = jnp.zeros((p, DSA_HEAD_DIM), F32)
    ci = jnp.concatenate([c64, jnp.full((p, IDX_HEADS), IDX_HEADS ** -0.5, F32),
                          jnp.zeros((p, DSA_HEAD_DIM - IDX_HEADS), F32)], 1)
    return jnp.concatenate([cos_r, sin_r,
                            jnp.concatenate([c64, c64], 1), jnp.concatenate([s1_64, s1_64], 1),
                            jnp.concatenate([s2_64, s2_64], 1),
                            ci, jnp.concatenate([s1_64, z64], 1), jnp.concatenate([s2_64, z64], 1)], 1)


def _pack_w_in(w):
    cuts = np.cumsum(PROJ_WIDTHS)[:-1].tolist()
    rq, rk, rv, rg, aq, ak, av, iq, ik, iw, gr, ga = jnp.split(w, cuts, axis=1)
    pad = jnp.zeros((w.shape[0], LANES - IDX_DIM - IDX_HEADS), w.dtype)
    return jnp.concatenate([rq, rk, rv, rg, aq, ak, av, iq, ik, iw, pad, gr, ga], axis=1).astype(BF16)


def _project(x, wp, tab, tab_period):
    n = x.shape[0]
    tm = PROJ_TM
    row = lambda w: pl.BlockSpec((tm, w), lambda i: (i, 0))
    out_shapes = [((n, 512), BF16), ((n, 512), BF16), ((n, 1024), BF16), ((n, 1024), F32),
                  ((n, 512), BF16), ((n, LANES), F32), ((n, LANES), F32), ((n, 512), BF16),
                  ((n, LANES), F32), ((n, 1024), F32), ((n, 1024), F32), ((n, DSA_HEADS * DSA_HEAD_DIM), BF16)]
    return pl.pallas_call(
        _proj_kernel,
        grid=(n // tm,),
        in_specs=[row(D_MODEL),
                  pl.BlockSpec((D_MODEL, PACKED_COLS), lambda i: (0, 0), pipeline_mode=pl.Buffered(1)),
                  pl.BlockSpec((tm, TAB_COLS), lambda i: (i % tab_period, 0))],
        out_specs=[row(s[1]) for s, _ in out_shapes],
        out_shape=[jax.ShapeDtypeStruct(s, d) for s, d in out_shapes],
        compiler_params=pltpu.CompilerParams(dimension_semantics=("parallel",), vmem_limit_bytes=52 * MIB),
        name="proj",
    )(x, wp, tab)


def _ret_kernel(dec_ref, xi_ref, zeta_ref, rq_ref, rk_ref, rv_ref, rg_ref, s0_ref, ret_ref, sout_ref, st_ref,
                *, rows, n_chunk, g_pow):
    j = pl.program_id(1)
    cpad = RET_CHUNK

    @pl.when(j == 0)
    def _():
        st_ref[...] = s0_ref[0]

    def padded(v):
        if rows == cpad:
            return v
        return jnp.concatenate([v, jnp.zeros((cpad - rows, v.shape[1]), v.dtype)], axis=0)

    for c in range(n_chunk):
        rs = slice(c * rows, (c + 1) * rows)
        for h in range(RET_HEADS):
            ks = slice(h * RET_DK, (h + 1) * RET_DK)
            vs = slice(h * RET_DV, (h + 1) * RET_DV)
            q = padded(rq_ref[rs, ks])
            kt = padded(rk_ref[rs, ks].astype(F32)).T
            v = padded(rv_ref[rs, vs])
            s = st_ref[h]
            sc = _mm(q, kt.astype(BF16)) * dec_ref[h]
            o = _mm(sc.astype(BF16), v) + _mm(q, s.astype(BF16)) * xi_ref[h]
            st_ref[h] = g_pow[h] * s + _mm((kt * zeta_ref[h]).astype(BF16), v)
            o = o[:rows]
            mu = jnp.mean(o, axis=-1, keepdims=True)
            d = o - mu
            var = jnp.mean(d * d, axis=-1, keepdims=True)
            gn = d * lax.rsqrt(var + GN_EPS)
            g = rg_ref[rs, vs]
            ret_ref[rs, vs] = (gn * (g * jax.nn.sigmoid(g))).astype(BF16)

    @pl.when(j == pl.num_programs(1) - 1)
    def _():
        sout_ref[0] = st_ref[...]


def _retention(rq, rk, rv, rg, s0, rows_per_stream, rows):
    n = rq.shape[0]
    n_streams = n // rows_per_stream
    n_chunk = min(rows_per_stream // rows, RET_CHUNKS_PER_STEP)
    blk = rows * n_chunk
    nb = rows_per_stream // blk
    gam = 1.0 - 2.0 ** (-5.0 - np.arange(RET_HEADS, dtype=np.float64))
    i = np.arange(RET_CHUNK, dtype=np.float64)
    diff = i[:, None] - i[None, :]
    dec = np.where(diff >= 0, gam[:, None, None] ** np.maximum(diff, 0.0)[None], 0.0)
    xi = gam[:, None, None] ** (i + 1.0)[None, :, None]
    zeta = np.where(i < rows, gam[:, None, None] ** (rows - 1.0 - i)[None, None, :], 0.0)
    g_pow = tuple(float(g ** rows) for g in gam)
    const = lambda shape: pl.BlockSpec(shape, lambda s, j: (0,) * len(shape))
    row = lambda w: pl.BlockSpec((blk, w), lambda s, j: (s * nb + j, 0))
    st = pl.BlockSpec((1, RET_HEADS, RET_DK, RET_DV), lambda s, j: (s, 0, 0, 0))
    return pl.pallas_call(
        functools.partial(_ret_kernel, rows=rows, n_chunk=n_chunk, g_pow=g_pow),
        grid=(n_streams, nb),
        in_specs=[const((RET_HEADS, RET_CHUNK, RET_CHUNK)), const((RET_HEADS, RET_CHUNK, 1)),
                  const((RET_HEADS, 1, RET_CHUNK)), row(512), row(512), row(1024), row(1024), st],
        out_specs=[row(1024), st],
        out_shape=[jax.ShapeDtypeStruct((n, RET_HEADS * RET_DV), BF16),
                   jax.ShapeDtypeStruct((n_streams, RET_HEADS, RET_DK, RET_DV), F32)],
        scratch_shapes=[pltpu.VMEM((RET_HEADS, RET_DK, RET_DV), F32)],
        compiler_params=pltpu.CompilerParams(dimension_semantics=("parallel", "arbitrary"),
                                             vmem_limit_bytes=32 * MIB),
        name="retention",
    )(jnp.asarray(dec, F32), jnp.asarray(xi, F32), jnp.asarray(zeta, F32), rq, rk, rv, rg, s0)


def _dsa_kernel(aq_ref, iq_ref, ikwq_ref, kk_ref, vv_ref, ikk_ref, *rest,
                group, tq, n_keys, row0, limit_const, n_sel):
    o_ref, w_ref, bias_ref = rest[-3:]
    rows = group * tq
    stack_pairs = n_keys <= DSA_SHORT_EXTENT
    search_unroll = 2 * SEARCH_UNROLL if n_keys <= DSA_SHORT_EXTENT else SEARCH_UNROLL
    nsel_f = float(n_sel)
    neg_inf = -jnp.inf

    col = lax.broadcasted_iota(I32, (tq, n_keys), 1)
    if limit_const is None:
        chunk_of_row = jnp.right_shift(lax.broadcasted_iota(I32, (tq, 1), 0) + row0, CHUNK.bit_length() - 1)
        limit = (chunk_of_row + 1) * CHUNK
    else:
        limit = limit_const

    def attend_stream(s, carry):
        bias = bias_ref[pl.ds(pl.multiple_of(s * tq, tq), tq), :]
        kfull = kk_ref[s]
        vfull = vv_ref[s]
        for g in range(DSA_KV_HEADS):
            gs = slice(g * DSA_HEAD_DIM, (g + 1) * DSA_HEAD_DIM)
            kg = kfull[:, gs].astype(BF16)
            vg = vfull[:, gs].astype(BF16)
            for pp in range(2):
                p = 2 * g + pp
                slab = (aq_ref[s, :, p * LANES:(p + 1) * LANES].astype(F32) * (DSA_HEAD_DIM ** -0.5)).astype(BF16)
                heads = [slab[:, hh * DSA_HEAD_DIM:(hh + 1) * DSA_HEAD_DIM] for hh in range(2)]
                if stack_pairs:
                    heads = [jnp.concatenate(heads, axis=0)]
                outs = []
                for qh in heads:
                    lg = _nt(qh, kg) + (jnp.concatenate([bias, bias], axis=0) if stack_pairs else bias)
                    m = jnp.max(lg, axis=1, keepdims=True)
                    pr = jnp.exp(lg - m)
                    den = jnp.sum(pr, axis=1, keepdims=True)
                    outs.append(_mm(pr.astype(BF16), vg) / den)
                if stack_pairs:
                    outs = [outs[0][:tq], outs[0][tq:]]
                o_ref[s, :, p * LANES:(p + 1) * LANES] = jnp.concatenate(outs, axis=1).astype(BF16)
        return carry

    if n_keys <= n_sel:
        for s in range(group):
            bias_ref[s * tq:(s + 1) * tq, :] = jnp.where(col < limit, 0.0, neg_inf)
        lax.fori_loop(0, group, attend_stream, 0)
        return

    def score_stream(s, carry):
        ikb = ikk_ref[s][:, :IDX_DIM].astype(BF16)
        iww = ikwq_ref[s][:, IDX_DIM:IDX_DIM + IDX_HEADS] * (IDX_DIM ** -0.5)
        acc = jnp.zeros((tq, n_keys), F32)
        for p in range(IDX_HEADS // 2):
            slab = iq_ref[s, :, p * LANES:(p + 1) * LANES]
            for hh in range(2):
                h = 2 * p + hh
                acc = acc + jnp.maximum(_nt(slab[:, hh * IDX_DIM:(hh + 1) * IDX_DIM], ikb), 0.0) * iww[:, h:h + 1]
        w_ref[pl.ds(pl.multiple_of(s * tq, tq), tq), :] = jnp.where(col < limit, acc, neg_inf)
        return carry

    lax.fori_loop(0, group, score_stream, 0)

    sc = w_ref[...]
    pos = jnp.sum(jnp.where(sc >= 0.0, 1.0, 0.0), axis=1, keepdims=True) >= nsel_f
    kk = jnp.where(pos, nsel_f, float(n_keys - n_sel + 1))
    w_ref[...] = jnp.where(pos, sc, -sc)

    def bit_step(i, u):
        cand_u = u | jnp.left_shift(jnp.int32(1), 30 - i)
        cand = pltpu.bitcast(cand_u, F32)
        cnt = jnp.sum(jnp.where(w_ref[...] >= cand, 1.0, 0.0), axis=1, keepdims=True)
        return jnp.where(cnt >= kk, cand_u, u)

    mag_u = lax.fori_loop(0, 31, bit_step, jnp.zeros((rows, 1), I32), unroll=search_unroll)
    mag = pltpu.bitcast(mag_u, F32)
    thr = jnp.where(pos, mag, -mag)

    sc = jnp.where(pos, w_ref[...], -w_ref[...])
    short = jnp.sum(jnp.where(sc >= thr, 1.0, 0.0), axis=1, keepdims=True) < nsel_f
    thr = jnp.where(jnp.logical_and(short, jnp.logical_not(pos)), -pltpu.bitcast(mag_u + 1, F32), thr)
    ge = sc >= thr
    cnt_gt = jnp.sum(jnp.where(sc > thr, 1.0, 0.0), axis=1, keepdims=True)
    cnt_ge = jnp.sum(jnp.where(ge, 1.0, 0.0), axis=1, keepdims=True)
    bias_ref[...] = jnp.where(jnp.logical_and(ge, sc > neg_inf), 0.0, neg_inf)
    excess = jnp.logical_and(cnt_ge > nsel_f, thr > neg_inf)

    @pl.when(jnp.max(jnp.where(excess, 1.0, 0.0)) > 0.0)
    def _():
        need = nsel_f - cnt_gt
        tri = jnp.where(lax.broadcasted_iota(I32, (LANES, LANES), 0) < lax.broadcasted_iota(I32, (LANES, LANES), 1),
                        1.0, 0.0).astype(BF16)
        before = jnp.zeros((rows, 1), F32)
        for b in range(n_keys // LANES):
            sl = slice(b * LANES, (b + 1) * LANES)
            sblk = jnp.where(pos, w_ref[:, sl], -w_ref[:, sl])
            eq = jnp.where(sblk == thr, 1.0, 0.0)
            rank = _mm(eq.astype(BF16), tri) + before
            keep = jnp.logical_or(sblk > thr, jnp.logical_and(sblk == thr, rank < need))
            bias_ref[:, sl] = jnp.where(jnp.logical_and(keep, sblk > neg_inf), 0.0, neg_inf)
            before = before + jnp.sum(eq, axis=1, keepdims=True)

    lax.fori_loop(0, group, attend_stream, 0)


def _dsa_call(aq, iq, ikw, kk, vv, ikk, group, tq, jq, n_keys, limit_const, n_sel, name, prev=None):
    n_streams, rows_per_stream = aq.shape[0], aq.shape[1]
    extra_specs, extra_args, aliases = [], (), {}
    if prev is not None:
        extra_specs, extra_args, aliases = [pl.BlockSpec(memory_space=pl.ANY)], (prev,), {6: 0}
    qrow = lambda a: pl.BlockSpec((group, tq, a.shape[2]), lambda s: (s, jq, 0))
    krow = lambda a: pl.BlockSpec((group, n_keys, a.shape[2]), lambda s: (s, 0, 0))
    width = DSA_HEADS * DSA_HEAD_DIM
    return pl.pallas_call(
        functools.partial(_dsa_kernel, group=group, tq=tq, n_keys=n_keys, row0=jq * tq, limit_const=limit_const,
                          n_sel=n_sel),
        grid=(n_streams // group,),
        in_specs=[qrow(aq), qrow(iq), qrow(ikw), krow(kk), krow(vv), krow(ikk)] + extra_specs,
        out_specs=pl.BlockSpec((group, tq, width), lambda s: (s, jq, 0)),
        out_shape=jax.ShapeDtypeStruct((n_streams, rows_per_stream, width), BF16),
        input_output_aliases=aliases,
        scratch_shapes=[pltpu.VMEM((group * tq, n_keys), F32), pltpu.VMEM((group * tq, n_keys), F32)],
        compiler_params=pltpu.CompilerParams(dimension_semantics=("parallel",), vmem_limit_bytes=56 * MIB),
        name=name,
    )(aq, iq, ikw, kk, vv, ikk, *extra_args)


def _dsa_cached_kernel(aq_ref, iq_ref, ikwq_ref, ck_ref, cv_ref, cik_ref, nk_ref, nv_ref, o_ref,
                       w_ref, bias_ref, kk_ref, vv_ref, ikk_ref, **static):
    past, new = ck_ref.shape[1], nk_ref.shape[1]
    for dst, cache, fresh in ((kk_ref, ck_ref, nk_ref), (vv_ref, cv_ref, nv_ref), (ikk_ref, cik_ref, ikwq_ref)):
        dst[:, :past, :cache.shape[2]] = cache[...]
        dst[:, past:past + new, :] = fresh[...]
        dst[:, past + new:, :] = jnp.zeros((dst.shape[0], dst.shape[1] - past - new, dst.shape[2]), F32)
    _dsa_kernel(aq_ref, iq_ref, ikwq_ref, kk_ref, vv_ref, ikk_ref, o_ref, w_ref, bias_ref, **static)


def _dsa_cached(aq, iq, ikw, cache_k, cache_v, cache_ik, new_k, new_v, tq):
    n_streams, past, _ = cache_k.shape
    n = aq.shape[0]
    per_stream = lambda a: a.reshape(n_streams, tq, a.shape[1])
    aq, iq, ikw, new_k, new_v = per_stream(aq), per_stream(iq), per_stream(ikw), per_stream(new_k), per_stream(new_v)
    limit = past + tq
    n_keys = -(-limit // LANES) * LANES
    group = max(1, min(n_streams, DSA_ROWS // tq, DSA_GROUP_MAX))
    blk = lambda a: pl.BlockSpec((group,) + a.shape[1:], lambda s: (s, 0, 0))
    width = DSA_HEADS * DSA_HEAD_DIM
    return pl.pallas_call(
        functools.partial(_dsa_cached_kernel, group=group, tq=tq, n_keys=n_keys, row0=0, limit_const=limit,
                          n_sel=min(DSA_TOPK, limit // 4)),
        grid=(n_streams // group,),
        in_specs=[blk(a) for a in (aq, iq, ikw, cache_k, cache_v, cache_ik, new_k, new_v)],
        out_specs=pl.BlockSpec((group, tq, width), lambda s: (s, 0, 0)),
        out_shape=jax.ShapeDtypeStruct((n_streams, tq, width), BF16),
        scratch_shapes=[pltpu.VMEM((group * tq, n_keys), F32), pltpu.VMEM((group * tq, n_keys), F32)] +
                       [pltpu.VMEM((group, n_keys, LANES), F32)] * 3,
        compiler_params=pltpu.CompilerParams(dimension_semantics=("parallel",), vmem_limit_bytes=48 * MIB),
        name="dsa_s",
    )(aq, iq, ikw, cache_k, cache_v, cache_ik, new_k, new_v).reshape(n, width)


def _dsa(aq, iq, ikw, kk, vv, ikk, tq, out):
    n_streams, n_keys, _ = kk.shape
    n = aq.shape[0]
    per_stream = lambda a: a.reshape(n_streams, n // n_streams, a.shape[1])
    aq, iq, ikw = per_stream(aq), per_stream(iq), per_stream(ikw)
    nq = n // n_streams // tq
    n_sel = min(DSA_TOPK, n_keys // 4)
    group = max(1, min(n_streams, DSA_ROWS // tq, DSA_GROUP_MAX))
    out = per_stream(out)
    for jq in range(nq):
        out = _dsa_call(aq, iq, ikw, kk, vv, ikk, group, tq, jq, (jq + 1) * tq, None, n_sel, f"dsa_p{jq}", prev=out)
    return out.reshape(n, -1)


def _pack_rows(v):
    q = QUARTER
    bits = lambda x: pltpu.bitcast(x.astype(BF16).astype(F32), I32)
    pair = lambda c: lax.shift_right_logical(bits(v[:, c * q:(c + 1) * q]), 16) | bits(v[:, (c + 1) * q:(c + 2) * q])
    return pair(0), pair(2)


def _unpack_rows(lo, hi):
    parts = []
    for w in (lo, hi):
        parts.append(pltpu.bitcast(lax.shift_left(w, 16), F32))
        parts.append(pltpu.bitcast(w & jnp.int32(-65536), F32))
    return jnp.concatenate(parts, axis=1)


def _layer_norm(v, g, b):
    mu = jnp.mean(v, axis=-1, keepdims=True)
    d = v - mu
    var = jnp.mean(d * d, axis=-1, keepdims=True)
    return d * lax.rsqrt(var + LN_EPS) * g + b


def _merge_kernel(x_ref, ret_ref, od_ref, gr_ref, ga_ref, wr_ref, wd_ref, wo_ref, g1_ref, b1_ref, rwt_ref, rb_ref,
                  tri_ref, cnt0_ref, h1_ref, hlo_ref, hhi_ref, gtm_ref, ek_ref, rk_ref, tot_ref, cnt_ref):
    @pl.when(pl.program_id(0) == 0)
    def _():
        cnt_ref[...] = cnt0_ref[:, 0:1]

    y_ret = _mm(ret_ref[...], wr_ref[...])
    y_dsa = _mm(od_ref[...], wd_ref[...])
    merged = jax.nn.sigmoid(gr_ref[...]) * y_ret + jax.nn.sigmoid(ga_ref[...]) * y_dsa
    mix = _mm(merged.astype(BF16), wo_ref[...])
    h1 = _layer_norm(DEEPNORM_ALPHA * x_ref[...] + mix, g1_ref[...], b1_ref[...])
    h1_ref[...] = h1
    hlo_ref[...], hhi_ref[...] = _pack_rows(h1)

    logits = lax.dot_general(rwt_ref[...], h1, (((1,), (1,)), ((), ())), preferred_element_type=F32,
                             precision=lax.Precision.HIGHEST) + rb_ref[...]
    tm = logits.shape[1]
    e_iota = lax.broadcasted_iota(I32, (N_EXPERTS, tm), 0)
    tops, hots, firsts = [], [], []
    for _ in range(MOE_TOP_K):
        m = jnp.max(logits, axis=0, keepdims=True)
        first = jnp.min(jnp.where(logits == m, e_iota, N_EXPERTS), axis=0, keepdims=True)
        hot = e_iota == first
        tops.append(m)
        hots.append(hot)
        firsts.append(first)
        logits = jnp.where(hot, -jnp.inf, logits)
    exps = [jnp.exp(m - tops[0]) for m in tops]
    den = exps[0] + exps[1] + exps[2] + exps[3]
    sel = jnp.zeros((N_EXPERTS, tm), F32)
    for hot in hots:
        sel = sel + jnp.where(hot, 1.0, 0.0)
    rank = _mm(sel.astype(BF16), tri_ref[...]) + cnt_ref[...]
    ranks = [jnp.sum(jnp.where(hot, rank, 0.0), axis=0, keepdims=True).astype(I32) for hot in hots]
    pad_i = jnp.zeros((8 - MOE_TOP_K, tm), I32)
    ek_ref[...] = jnp.concatenate(firsts + [pad_i], axis=0)
    rk_ref[...] = jnp.concatenate(ranks + [pad_i], axis=0)
    gates = jnp.concatenate([e / den for e in exps] + [jnp.zeros((LANES - MOE_TOP_K, tm), F32)], axis=0)
    gtm_ref[...] = gates.T
    cnt_ref[...] = cnt_ref[...] + jnp.sum(sel, axis=1, keepdims=True)
    tot_ref[...] = jnp.broadcast_to(cnt_ref[...], tot_ref.shape)


def _merge(x, ret, od, gr, ga, wr, wd, wo, g1, b1, rwt, rb, cnt0):
    n = x.shape[0]
    tm = MERGE_TM
    row = lambda w: pl.BlockSpec((tm, w), lambda i: (i, 0))
    const = lambda a: pl.BlockSpec(a.shape, lambda i: (0,) * a.ndim)
    col = pl.BlockSpec((8, tm), lambda i: (0, i))
    tri = jnp.asarray(np.triu(np.ones((tm, tm), np.float32), 1), BF16)
    return pl.pallas_call(
        _merge_kernel,
        grid=(n // tm,),
        in_specs=[row(D_MODEL), row(1024), row(512), row(1024), row(1024), const(wr), const(wd), const(wo),
                  const(g1), const(b1), const(rwt), const(rb), const(tri), const(cnt0)],
        out_specs=[row(D_MODEL), row(QUARTER), row(QUARTER), row(LANES), col, col,
                   pl.BlockSpec((N_EXPERTS, LANES), lambda i: (0, 0))],
        out_shape=[jax.ShapeDtypeStruct((n, D_MODEL), F32), jax.ShapeDtypeStruct((n, QUARTER), I32),
                   jax.ShapeDtypeStruct((n, QUARTER), I32), jax.ShapeDtypeStruct((n, LANES), F32),
                   jax.ShapeDtypeStruct((8, n), I32), jax.ShapeDtypeStruct((8, n), I32),
                   jax.ShapeDtypeStruct((N_EXPERTS, LANES), F32)],
        scratch_shapes=[pltpu.VMEM((N_EXPERTS, 1), F32)],
        compiler_params=pltpu.CompilerParams(dimension_semantics=("arbitrary",), vmem_limit_bytes=48 * MIB),
        name="merge",
    )(x, ret, od, gr, ga, wr, wd, wo, g1, b1, rwt, rb, tri, cnt0)


def _deinterleave_kernel(w_ref, o_ref):
    r = lax.broadcasted_iota(I32, (UP_BLOCK, UP_BLOCK), 0)
    c = lax.broadcasted_iota(I32, (UP_BLOCK, UP_BLOCK), 1)
    src = jnp.where(c < LANES, 2 * c, 2 * (c - LANES) + 1)
    perm = jnp.where(r == src, 1.0, 0.0).astype(BF16)
    for b in range(w_ref.shape[2] // UP_BLOCK):
        sl = slice(b * UP_BLOCK, (b + 1) * UP_BLOCK)
        o_ref[0, :, sl] = _mm(w_ref[0, :, sl].astype(BF16), perm).astype(BF16)


def _deinterleave_w_up(w_up):
    n_e, d_in, d_out = w_up.shape
    cols = d_out
    spec = pl.BlockSpec((1, d_in, cols), lambda e, c: (e, 0, c))
    return pl.pallas_call(
        _deinterleave_kernel,
        grid=(n_e, d_out // cols),
        in_specs=[spec],
        out_specs=spec,
        out_shape=jax.ShapeDtypeStruct(w_up.shape, BF16),
        compiler_params=pltpu.CompilerParams(dimension_semantics=("parallel", "parallel"),
                                             vmem_limit_bytes=40 * MIB),
        name="w_up_prep",
    )(w_up)


def _pos_kernel(off_ref, ek_ref, rk_ref, pos_ref):
    ek = ek_ref[...]
    pos = rk_ref[...]
    for e in range(N_EXPERTS):
        pos = pos + jnp.where(ek == e, off_ref[e], 0)
    pos_ref[...] = pos


def _positions(off, ek, rk):
    n = ek.shape[1]
    tn = min(n, 2048)
    spec = pl.BlockSpec((8, tn), lambda i, off: (0, i))
    return pl.pallas_call(
        _pos_kernel,
        grid_spec=pltpu.PrefetchScalarGridSpec(num_scalar_prefetch=1, grid=(n // tn,), in_specs=[spec, spec],
                                               out_specs=spec),
        out_shape=jax.ShapeDtypeStruct((8, n), I32),
        name="moe_pos",
    )(off, ek, rk)


def _sc_mesh():
    return plsc.VectorSubcoreMesh(core_axis_name="core", subcore_axis_name="subcore")


def _sc_scatter(xs, pos_flat, n_rows):
    width, dtype = xs[0].shape[1], xs[0].dtype
    windows = [x.shape[0] // SC_WINDOW for x in xs]
    nw = sum(windows)

    @functools.partial(pl.kernel, out_type=jax.ShapeDtypeStruct((n_rows, width), dtype), mesh=_sc_mesh())
    def scatter(*refs):
        p_hbm, o_hbm = refs[len(xs)], refs[len(xs) + 1]

        def body(x_vmem, *p_vmem):
            for p in p_vmem:
                pltpu.sync_copy(x_vmem, o_hbm.at[p.at[0]])

        first = 0
        for x_hbm, nw_g in zip(refs[:len(xs)], windows):
            pltpu.emit_pipeline(
                body, grid=(nw_g,),
                in_specs=[pl.BlockSpec((SC_WINDOW, width), lambda i: (i, 0))] +
                         [pl.BlockSpec((1, SC_WINDOW), functools.partial(lambda w0, i: (0, w0 + i), k * nw + first))
                          for k in range(MOE_TOP_K)],
                out_specs=[], core_axis_name=("core", "subcore"), dimension_semantics=(pltpu.PARALLEL,),
            )(x_hbm, *([p_hbm] * MOE_TOP_K))
            first += nw_g

    return scatter(*xs, pos_flat)


def _sc_gather(y, pos_flat):
    m = pos_flat.shape[1]

    @functools.partial(pl.kernel, out_type=jax.ShapeDtypeStruct((m, y.shape[1]), y.dtype), mesh=_sc_mesh())
    def gather(y_hbm, p_hbm, o_hbm):
        def body(p_vmem, o_vmem):
            pltpu.sync_copy(y_hbm.at[p_vmem.at[0]], o_vmem)

        pltpu.emit_pipeline(
            body, grid=(m // SC_WINDOW,),
            in_specs=[pl.BlockSpec((1, SC_WINDOW), lambda i: (0, i))],
            out_specs=[pl.BlockSpec((SC_WINDOW, y.shape[1]), lambda i: (i, 0))],
            core_axis_name=("core", "subcore"), dimension_semantics=(pltpu.PARALLEL,),
        )(p_hbm, o_hbm)

    return gather(y, pos_flat)


def _ffn_kernel(be_ref, nu_ref, xlo_ref, xhi_ref, wup_ref, bup_ref, wdn_ref, bdn_ref, ylo_ref, yhi_ref):
    @pl.when(pl.program_id(0) < nu_ref[0])
    def _():
        x = _unpack_rows(xlo_ref[...], xhi_ref[...]).astype(BF16)
        h = _mm(x, wup_ref[0]) + bup_ref[0]
        acts = []
        for b in range(2 * D_FF // UP_BLOCK):
            glu = jnp.minimum(h[:, b * UP_BLOCK:b * UP_BLOCK + LANES], SWIGLU_LIMIT)
            lin = jnp.clip(h[:, b * UP_BLOCK + LANES:(b + 1) * UP_BLOCK], -SWIGLU_LIMIT, SWIGLU_LIMIT)
            acts.append(glu * jax.nn.sigmoid(SWIGLU_ALPHA * glu) * (lin + 1.0))
        act = jnp.concatenate(acts, axis=1)
        ylo_ref[...], yhi_ref[...] = _pack_rows(_mm(act.astype(BF16), wdn_ref[0]) + bdn_ref[0])


def _ffn(block_expert, n_used, xs_lo, xs_hi, wup, bup, wdn, bdn):
    rows = xs_lo.shape[0]
    blk = FFN_BLOCK
    row = pl.BlockSpec((blk, QUARTER), lambda i, be, nu: (i, 0))
    per_expert = lambda a: pl.BlockSpec((1,) + a.shape[1:], lambda i, be, nu: (be[i], 0, 0))
    return pl.pallas_call(
        _ffn_kernel,
        grid_spec=pltpu.PrefetchScalarGridSpec(
            num_scalar_prefetch=2, grid=(rows // blk,),
            in_specs=[row, row, per_expert(wup), per_expert(bup), per_expert(wdn), per_expert(bdn)],
            out_specs=[row, row]),
        out_shape=[jax.ShapeDtypeStruct((rows, QUARTER), I32), jax.ShapeDtypeStruct((rows, QUARTER), I32)],
        compiler_params=pltpu.CompilerParams(dimension_semantics=("arbitrary",), vmem_limit_bytes=40 * MIB),
        name="moe_ffn",
    )(block_expert, n_used, xs_lo, xs_hi, wup, bup, wdn, bdn)


def _combine_kernel(h1_ref, olo_ref, ohi_ref, gtm_ref, g2_ref, b2_ref, o_ref):
    g = gtm_ref[...]
    y = jnp.zeros(h1_ref.shape, F32)
    for k in range(MOE_TOP_K):
        y = y + g[:, k:k + 1] * _unpack_rows(olo_ref[k], ohi_ref[k])
    o_ref[...] = _layer_norm(DEEPNORM_ALPHA * h1_ref[...] + y, g2_ref[...], b2_ref[...])


def _combine(h1, og_lo, og_hi, gtm, g2, b2, row0):
    n = h1.shape[0]
    n_all = og_lo.shape[0] // MOE_TOP_K
    tm = MERGE_TM
    row = lambda w: pl.BlockSpec((tm, w), lambda i: (i, 0))
    const = lambda a: pl.BlockSpec(a.shape, lambda i: (0,) * a.ndim)
    picked = pl.BlockSpec((MOE_TOP_K, tm, QUARTER), lambda i: (0, i + row0 // tm, 0))
    return pl.pallas_call(
        _combine_kernel,
        grid=(n // tm,),
        in_specs=[row(D_MODEL), picked, picked, row(LANES), const(g2), const(b2)],
        out_specs=row(D_MODEL),
        out_shape=jax.ShapeDtypeStruct((n, D_MODEL), F32),
        compiler_params=pltpu.CompilerParams(dimension_semantics=("parallel",), vmem_limit_bytes=40 * MIB),
        name="moe_combine",
    )(h1, og_lo.reshape(MOE_TOP_K, n_all, QUARTER), og_hi.reshape(MOE_TOP_K, n_all, QUARTER), gtm, g2, b2)


def _moe(groups, totals, wup, bup, wdn, bdn, g2, b2):
    blk = FFN_BLOCK
    sizes = [g[0].shape[0] for g in groups]
    n_all = sum(sizes)
    n_rows = MOE_TOP_K * n_all + N_EXPERTS * blk
    counts = totals[:, 0].astype(I32)
    padded = (counts + blk - 1) // blk * blk
    ends = jnp.cumsum(padded)
    block_start = jnp.arange(n_rows // blk, dtype=I32) * blk
    block_expert = jnp.minimum(jnp.sum((ends[None, :] <= block_start[:, None]).astype(I32), axis=1), N_EXPERTS - 1)
    n_used = (ends[-1:] // blk).astype(I32)
    cat = lambda i, axis: jnp.concatenate([g[i] for g in groups], axis=axis)
    pos = _positions((ends - padded).astype(I32), cat(4, 1), cat(5, 1))
    pos_flat = pos[:MOE_TOP_K].reshape(1, MOE_TOP_K * n_all)
    xs_lo = _sc_scatter([g[1] for g in groups], pos_flat, n_rows)
    xs_hi = _sc_scatter([g[2] for g in groups], pos_flat, n_rows)
    ys_lo, ys_hi = _ffn(block_expert, n_used, xs_lo, xs_hi, wup, bup, wdn, bdn)
    og_lo, og_hi = _sc_gather(ys_lo, pos_flat), _sc_gather(ys_hi, pos_flat)
    starts = np.cumsum([0] + sizes[:-1]).tolist()
    return [_combine(g[0], og_lo, og_hi, g[3], g2, b2, r0) for g, r0 in zip(groups, starts)]


def _mixer(x, pos_tab, tab_period, s0, rows_per_stream, ret_rows, caches, weights, cnt0):
    wp, wr, wd, wo, g1, b1, rwt, rb = weights
    n = x.shape[0]
    n_streams = n // rows_per_stream
    rq, rk, rv, rg, aq, ak, av, iq, ikw, gr, ga, od0 = _project(x, wp, pos_tab, tab_period)
    ret, s_new = _retention(rq, rk, rv, rg, s0, rows_per_stream, ret_rows)
    if caches is None:
        per_stream = lambda a: a.reshape(n_streams, rows_per_stream, a.shape[1])
        od = _dsa(aq, iq, ikw, per_stream(ak), per_stream(av), per_stream(ikw), DSA_TQ, od0)
    else:
        od = _dsa_cached(aq, iq, ikw, *caches, ak, av, rows_per_stream)
    h1, h_lo, h_hi, gtm, ek, rk_, totals = _merge(x, ret, od, gr, ga, wr, wd, wo, g1, b1, rwt, rb, cnt0)
    return (h1, h_lo, h_hi, gtm, ek, rk_), totals, (s_new, ak, av, ikw[:, :IDX_DIM])


def kernel(x_prompt, x_sample, state_ret, cache_k, cache_v, cache_idx_k, w_in, w_ret_o, w_dsa_o, w_o,
           ln1_g, ln1_b, router_w, router_b, w_up, b_up, w_down, b_down, ln2_g, ln2_b):
    assert w_in.shape[0] == DEPTH
    batch, seq, _ = x_prompt.shape
    dec_batch, dec_seq, _ = x_sample.shape
    past = cache_k.shape[2]
    assert seq % PROJ_TM == 0 and seq % DSA_TQ == 0 and PROJ_TM % dec_seq == 0

    l = 0
    mixer_w = (_pack_w_in(w_in[l]), w_ret_o[l].astype(BF16), w_dsa_o[l].astype(BF16), w_o[l].astype(BF16),
               ln1_g[l][None, :], ln1_b[l][None, :], router_w[l].T, router_b[l][:, None])
    moe_w = (
        _deinterleave_w_up(w_up[l]),
        b_up[l].reshape(N_EXPERTS, 2 * D_FF // UP_BLOCK, LANES, 2).transpose(0, 1, 3, 2).reshape(N_EXPERTS, 1, 2 * D_FF),
        w_down[l].astype(BF16), b_down[l][:, None, :], ln2_g[l][None, :], ln2_b[l][None, :])

    tab_p = _rot_tables(jnp.arange(seq))
    zeros_state = jnp.zeros((batch, RET_HEADS, RET_DK, RET_DV), F32)
    moe_p, totals_p, (s_p, k_p, v_p, ik_p) = _mixer(
        x_prompt.reshape(batch * seq, D_MODEL), tab_p, seq // PROJ_TM, zeros_state, seq, RET_CHUNK, None, mixer_w,
        jnp.zeros((N_EXPERTS, LANES), F32))

    tab_s = jnp.tile(_rot_tables(past + jnp.arange(dec_seq)), (PROJ_TM // dec_seq, 1))
    caches = (cache_k[l].reshape(dec_batch, past, LANES), cache_v[l].reshape(dec_batch, past, LANES), cache_idx_k[l])
    moe_s, totals, (s_s, k_s, v_s, ik_s) = _mixer(
        x_sample.reshape(dec_batch * dec_seq, D_MODEL), tab_s, 1, state_ret[l], dec_seq, dec_seq, caches, mixer_w,
        totals_p)

    y_p, y_s = _moe([moe_p, moe_s], totals, *moe_w)

    kv = (DSA_KV_HEADS, DSA_HEAD_DIM)
    return (y_p.reshape(batch, seq, D_MODEL), y_s.reshape(dec_batch, dec_seq, D_MODEL),
            s_p[None], k_p.reshape(1, batch, seq, *kv), v_p.reshape(1, batch, seq, *kv),
            ik_p.reshape(1, batch, seq, IDX_DIM),
            s_s[None], k_s.reshape(1, dec_batch, dec_seq, *kv), v_s.reshape(1, dec_batch, dec_seq, *kv),
            ik_s.reshape(1, dec_batch, dec_seq, IDX_DIM))
```

```python
import functools

import numpy as np
import jax
import jax.numpy as jnp
from jax import lax
from jax.experimental import pallas as pl
from jax.experimental.pallas import tpu as pltpu
from jax.experimental.pallas import tpu_sc as plsc

F32 = jnp.float32
BF16 = jnp.bfloat16
I32 = jnp.int32

D_MODEL = 1024
CHUNK = 64
RET_HEADS = 4
RET_DK = 128
RET_DV = 256
RET_ROPE_BASE = 10000.0
DSA_HEADS = 8
DSA_KV_HEADS = 2
DSA_HEAD_DIM = 64
IDX_HEADS = 8
IDX_DIM = 64
DSA_TOPK = 256
ROPE_THETA = 500000.0
N_EXPERTS = 32
MOE_TOP_K = 4
D_FF = 1024
SWIGLU_ALPHA = 1.702
SWIGLU_LIMIT = 7.0
LN_EPS = 1e-5
GN_EPS = 1e-6
DEPTH = 1
DEEPNORM_ALPHA = (2.0 * DEPTH) ** 0.25
PROJ_WIDTHS = (RET_HEADS * RET_DK, RET_HEADS * RET_DK, RET_HEADS * RET_DV, RET_HEADS * RET_DV,
               DSA_HEADS * DSA_HEAD_DIM, DSA_KV_HEADS * DSA_HEAD_DIM, DSA_KV_HEADS * DSA_HEAD_DIM,
               IDX_HEADS * IDX_DIM, IDX_DIM, IDX_HEADS, D_MODEL, D_MODEL)

assert CHUNK & (CHUNK - 1) == 0

LANES = 128
MIB = 1024 * 1024

OFF_RQ, OFF_RK, OFF_RV, OFF_RG = 0, 512, 1024, 2048
OFF_AQ, OFF_AK, OFF_AV, OFF_IQ, OFF_IKW = 3072, 3584, 3712, 3840, 4352
OFF_GR, OFF_GA, PACKED_COLS = 4480, 5504, 6528
TAB_COLS = 8 * LANES

PROJ_TM = 512
RET_CHUNK = 128
RET_CHUNKS_PER_STEP = 8
DSA_TQ = 256
DSA_ROWS = 512
DSA_GROUP_MAX = 4
SEARCH_UNROLL = 4
DSA_SHORT_EXTENT = 2048
MERGE_TM = 512
FFN_BLOCK = 512
SC_WINDOW = 128
QUARTER = D_MODEL // 4
UP_BLOCK = 2 * LANES


def _nt(a, b):
    return lax.dot_general(a, b, (((1,), (1,)), ((), ())), preferred_element_type=F32)


def _mm(a, b):
    return jnp.dot(a, b, preferred_element_type=F32)


def _proj_kernel(x_ref, w_ref, tab_ref, rq_ref, rk_ref, rv_ref, rg_ref, aq_ref, ak_ref, av_ref,
                 iq_ref, ikw_ref, gr_ref, ga_ref, od0_ref):
    xb = x_ref[...].astype(BF16)
    od0_ref[...] = jnp.zeros(od0_ref.shape, od0_ref.dtype)

    def mm(c0, n):
        return _mm(xb, w_ref[:, c0:c0 + n])

    def tab(i):
        return tab_ref[:, i * LANES:(i + 1) * LANES]

    def rot_full(z):
        return z * tab(0) + pltpu.roll(z, 64, 1) * tab(1)

    def rot_part(z, c):
        return z * tab(c) + pltpu.roll(z, LANES - 8, 1) * tab(c + 1) + pltpu.roll(z, 8, 1) * tab(c + 2)

    z = mm(OFF_RQ, 512)
    for h in range(4):
        sl = slice(h * LANES, (h + 1) * LANES)
        rq_ref[:, sl] = rot_full(z[:, sl]).astype(BF16)
    z = mm(OFF_RK, 512)
    for h in range(4):
        sl = slice(h * LANES, (h + 1) * LANES)
        rk_ref[:, sl] = (rot_full(z[:, sl]) * (RET_DK ** -0.5)).astype(BF16)
    for c in range(2):
        rv_ref[:, c * 512:(c + 1) * 512] = mm(OFF_RV + c * 512, 512).astype(BF16)
    for c in range(2):
        rg_ref[:, c * 512:(c + 1) * 512] = mm(OFF_RG + c * 512, 512)
    z = mm(OFF_AQ, 512)
    for h in range(4):
        sl = slice(h * LANES, (h + 1) * LANES)
        aq_ref[:, sl] = rot_part(z[:, sl], 2).astype(BF16)
    z = mm(OFF_AK, 256)
    ak_ref[...] = rot_part(z[:, :LANES], 2)
    av_ref[...] = z[:, LANES:]
    z = mm(OFF_IQ, 512)
    for h in range(4):
        sl = slice(h * LANES, (h + 1) * LANES)
        iq_ref[:, sl] = rot_part(z[:, sl], 2).astype(BF16)
    ikw_ref[...] = rot_part(mm(OFF_IKW, LANES), 5)
    for c in range(2):
        gr_ref[:, c * 512:(c + 1) * 512] = mm(OFF_GR + c * 512, 512)
    for c in range(2):
        ga_ref[:, c * 512:(c + 1) * 512] = mm(OFF_GA + c * 512, 512)


def _rot_tables(pos):
    p = pos.shape[0]
    posf = pos.astype(F32)[:, None]
    ret_f = RET_ROPE_BASE ** (-jnp.linspace(0.0, 1.0, RET_DK // 2, dtype=F32))
    ang = posf * ret_f[None, :]
    c, s = jnp.cos(ang), jnp.sin(ang)
    cos_r = jnp.concatenate([c, c], 1)
    sin_r = jnp.concatenate([-s, s], 1)
    n_rot = DSA_HEAD_DIM // 4
    att_f = ROPE_THETA ** (-jnp.arange(0, n_rot, 2, dtype=F32) / n_rot)
    ang2 = posf * att_f[None, :]
    c2, s2 = jnp.cos(ang2), jnp.sin(ang2)
    half = n_rot // 2
    rest = DSA_HEAD_DIM - 2 * half
    c64 = jnp.concatenate([c2, c2, jnp.ones((p, rest), F32)], 1)
    s1_64 = jnp.concatenate([-s2, jnp.zeros((p, DSA_HEAD_DIM - half), F32)], 1)
    s2_64 = jnp.concatenate([jnp.zeros((p, half), F32), s2, jnp.zeros((p, rest), F32)], 1)
    z64 = jnp.zeros((p, DSA_HEAD_DIM), F32)
    ci = jnp.concatenate([c64, jnp.full((p, IDX_HEADS), IDX_HEADS ** -0.5, F32),
                          jnp.zeros((p, DSA_HEAD_DIM - IDX_HEADS), F32)], 1)
    return jnp.concatenate([cos_r, sin_r,
                            jnp.concatenate([c64, c64], 1), jnp.concatenate([s1_64, s1_64], 1),
                            jnp.concatenate([s2_64, s2_64], 1),
                            ci, jnp.concatenate([s1_64, z64], 1), jnp.concatenate([s2_64, z64], 1)], 1)


def _pack_w_in(w):
    cuts = np.cumsum(PROJ_WIDTHS)[:-1].tolist()
    rq, rk, rv, rg, aq, ak, av, iq, ik, iw, gr, ga = jnp.split(w, cuts, axis=1)
    pad = jnp.zeros((w.shape[0], LANES - IDX_DIM - IDX_HEADS), w.dtype)
    return jnp.concatenate([rq, rk, rv, rg, aq, ak, av, iq, ik, iw, pad, gr, ga], axis=1).astype(BF16)


def _project(x, wp, tab, tab_period):
    n = x.shape[0]
    tm = PROJ_TM
    row = lambda w: pl.BlockSpec((tm, w), lambda i: (i, 0))
    out_shapes = [((n, 512), BF16), ((n, 512), BF16), ((n, 1024), BF16), ((n, 1024), F32),
                  ((n, 512), BF16), ((n, LANES), F32), ((n, LANES), F32), ((n, 512), BF16),
                  ((n, LANES), F32), ((n, 1024), F32), ((n, 1024), F32), ((n, DSA_HEADS * DSA_HEAD_DIM), BF16)]
    return pl.pallas_call(
        _proj_kernel,
        grid=(n // tm,),
        in_specs=[row(D_MODEL),
                  pl.BlockSpec((D_MODEL, PACKED_COLS), lambda i: (0, 0), pipeline_mode=pl.Buffered(1)),
                  pl.BlockSpec((tm, TAB_COLS), lambda i: (i % tab_period, 0))],
        out_specs=[row(s[1]) for s, _ in out_shapes],
        out_shape=[jax.ShapeDtypeStruct(s, d) for s, d in out_shapes],
        compiler_params=pltpu.CompilerParams(dimension_semantics=("parallel",), vmem_limit_bytes=52 * MIB),
        name="proj",
    )(x, wp, tab)


def _ret_kernel(dec_ref, xi_ref, zeta_ref, rq_ref, rk_ref, rv_ref, rg_ref, s0_ref, ret_ref, sout_ref, st_ref,
                *, rows, n_chunk, g_pow):
    j = pl.program_id(1)
    cpad = RET_CHUNK

    @pl.when(j == 0)
    def _():
        st_ref[...] = s0_ref[0]

    def padded(v):
        if rows == cpad:
            return v
        return jnp.concatenate([v, jnp.zeros((cpad - rows, v.shape[1]), v.dtype)], axis=0)

    for c in range(n_chunk):
        rs = slice(c * rows, (c + 1) * rows)
        for h in range(RET_HEADS):
            ks = slice(h * RET_DK, (h + 1) * RET_DK)
            vs = slice(h * RET_DV, (h + 1) * RET_DV)
            q = padded(rq_ref[rs, ks])
            kt = padded(rk_ref[rs, ks].astype(F32)).T
            v = padded(rv_ref[rs, vs])
            s = st_ref[h]
            sc = _mm(q, kt.astype(BF16)) * dec_ref[h]
            o = _mm(sc.astype(BF16), v) + _mm(q, s.astype(BF16)) * xi_ref[h]
            st_ref[h] = g_pow[h] * s + _mm((kt * zeta_ref[h]).astype(BF16), v)
            o = o[:rows]
            mu = jnp.mean(o, axis=-1, keepdims=True)
            d = o - mu
            var = jnp.mean(d * d, axis=-1, keepdims=True)
            gn = d * lax.rsqrt(var + GN_EPS)
            g = rg_ref[rs, vs]
            ret_ref[rs, vs] = (gn * (g * jax.nn.sigmoid(g))).astype(BF16)

    @pl.when(j == pl.num_programs(1) - 1)
    def _():
        sout_ref[0] = st_ref[...]


def _retention(rq, rk, rv, rg, s0, rows_per_stream, rows):
    n = rq.shape[0]
    n_streams = n // rows_per_stream
    n_chunk = min(rows_per_stream // rows, RET_CHUNKS_PER_STEP)
    blk = rows * n_chunk
    nb = rows_per_stream // blk
    gam = 1.0 - 2.0 ** (-5.0 - np.arange(RET_HEADS, dtype=np.float64))
    i = np.arange(RET_CHUNK, dtype=np.float64)
    diff = i[:, None] - i[None, :]
    dec = np.where(diff >= 0, gam[:, None, None] ** np.maximum(diff, 0.0)[None], 0.0)
    xi = gam[:, None, None] ** (i + 1.0)[None, :, None]
    zeta = np.where(i < rows, gam[:, None, None] ** (rows - 1.0 - i)[None, None, :], 0.0)
    g_pow = tuple(float(g ** rows) for g in gam)
    const = lambda shape: pl.BlockSpec(shape, lambda s, j: (0,) * len(shape))
    row = lambda w: pl.BlockSpec((blk, w), lambda s, j: (s * nb + j, 0))
    st = pl.BlockSpec((1, RET_HEADS, RET_DK, RET_DV), lambda s, j: (s, 0, 0, 0))
    return pl.pallas_call(
        functools.partial(_ret_kernel, rows=rows, n_chunk=n_chunk, g_pow=g_pow),
        grid=(n_streams, nb),
        in_specs=[const((RET_HEADS, RET_CHUNK, RET_CHUNK)), const((RET_HEADS, RET_CHUNK, 1)),
                  const((RET_HEADS, 1, RET_CHUNK)), row(512), row(512), row(1024), row(1024), st],
        out_specs=[row(1024), st],
        out_shape=[jax.ShapeDtypeStruct((n, RET_HEADS * RET_DV), BF16),
                   jax.ShapeDtypeStruct((n_streams, RET_HEADS, RET_DK, RET_DV), F32)],
        scratch_shapes=[pltpu.VMEM((RET_HEADS, RET_DK, RET_DV), F32)],
        compiler_params=pltpu.CompilerParams(dimension_semantics=("parallel", "arbitrary"),
                                             vmem_limit_bytes=32 * MIB),
        name="retention",
    )(jnp.asarray(dec, F32), jnp.asarray(xi, F32), jnp.asarray(zeta, F32), rq, rk, rv, rg, s0)


def _dsa_kernel(aq_ref, iq_ref, ikwq_ref, kk_ref, vv_ref, ikk_ref, *rest,
                group, tq, n_keys, row0, limit_const, n_sel):
    o_ref, w_ref, bias_ref = rest[-3:]
    rows = group * tq
    stack_pairs = n_keys <= DSA_SHORT_EXTENT
    search_unroll = 2 * SEARCH_UNROLL if n_keys <= DSA_SHORT_EXTENT else SEARCH_UNROLL
    nsel_f = float(n_sel)
    neg_inf = -jnp.inf

    col = lax.broadcasted_iota(I32, (tq, n_keys), 1)
    if limit_const is None:
        chunk_of_row = jnp.right_shift(lax.broadcasted_iota(I32, (tq, 1), 0) + row0, CHUNK.bit_length() - 1)
        limit = (chunk_of_row + 1) * CHUNK
    else:
        limit = limit_const

    def attend_stream(s, carry):
        bias = bias_ref[pl.ds(pl.multiple_of(s * tq, tq), tq), :]
        kfull = kk_ref[s]
        vfull = vv_ref[s]
        for g in range(DSA_KV_HEADS):
            gs = slice(g * DSA_HEAD_DIM, (g + 1) * DSA_HEAD_DIM)
            kg = kfull[:, gs].astype(BF16)
            vg = vfull[:, gs].astype(BF16)
            for pp in range(2):
                p = 2 * g + pp
                slab = (aq_ref[s, :, p * LANES:(p + 1) * LANES].astype(F32) * (DSA_HEAD_DIM ** -0.5)).astype(BF16)
                heads = [slab[:, hh * DSA_HEAD_DIM:(hh + 1) * DSA_HEAD_DIM] for hh in range(2)]
                if stack_pairs:
                    heads = [jnp.concatenate(heads, axis=0)]
                outs = []
                for qh in heads:
                    lg = _nt(qh, kg) + (jnp.concatenate([bias, bias], axis=0) if stack_pairs else bias)
                    m = jnp.max(lg, axis=1, keepdims=True)
                    pr = jnp.exp(lg - m)
                    den = jnp.sum(pr, axis=1, keepdims=True)
                    outs.append(_mm(pr.astype(BF16), vg) / den)
                if stack_pairs:
                    outs = [outs[0][:tq], outs[0][tq:]]
                o_ref[s, :, p * LANES:(p + 1) * LANES] = jnp.concatenate(outs, axis=1).astype(BF16)
        return carry

    if n_keys <= n_sel:
        for s in range(group):
            bias_ref[s * tq:(s + 1) * tq, :] = jnp.where(col < limit, 0.0, neg_inf)
        lax.fori_loop(0, group, attend_stream, 0)
        return

    def score_stream(s, carry):
        ikb = ikk_ref[s][:, :IDX_DIM].astype(BF16)
        iww = ikwq_ref[s][:, IDX_DIM:IDX_DIM + IDX_HEADS] * (IDX_DIM ** -0.5)
        acc = jnp.zeros((tq, n_keys), F32)
        for p in range(IDX_HEADS // 2):
            slab = iq_ref[s, :, p * LANES:(p + 1) * LANES]
            for hh in range(2):
                h = 2 * p + hh
                acc = acc + jnp.maximum(_nt(slab[:, hh * IDX_DIM:(hh + 1) * IDX_DIM], ikb), 0.0) * iww[:, h:h + 1]
        w_ref[pl.ds(pl.multiple_of(s * tq, tq), tq), :] = jnp.where(col < limit, acc, neg_inf)
        return carry

    lax.fori_loop(0, group, score_stream, 0)

    sc = w_ref[...]
    pos = jnp.sum(jnp.where(sc >= 0.0, 1.0, 0.0), axis=1, keepdims=True) >= nsel_f
    kk = jnp.where(pos, nsel_f, float(n_keys - n_sel + 1))
    w_ref[...] = jnp.where(pos, sc, -sc)

    def bit_step(i, u):
        cand_u = u | jnp.left_shift(jnp.int32(1), 30 - i)
        cand = pltpu.bitcast(cand_u, F32)
        cnt = jnp.sum(jnp.where(w_ref[...] >= cand, 1.0, 0.0), axis=1, keepdims=True)
        return jnp.where(cnt >= kk, cand_u, u)

    mag_u = lax.fori_loop(0, 31, bit_step, jnp.zeros((rows, 1), I32), unroll=search_unroll)
    mag = pltpu.bitcast(mag_u, F32)
    thr = jnp.where(pos, mag, -mag)

    sc = jnp.where(pos, w_ref[...], -w_ref[...])
    short = jnp.sum(jnp.where(sc >= thr, 1.0, 0.0), axis=1, keepdims=True) < nsel_f
    thr = jnp.where(jnp.logical_and(short, jnp.logical_not(pos)), -pltpu.bitcast(mag_u + 1, F32), thr)
    ge = sc >= thr
    cnt_gt = jnp.sum(jnp.where(sc > thr, 1.0, 0.0), axis=1, keepdims=True)
    cnt_ge = jnp.sum(jnp.where(ge, 1.0, 0.0), axis=1, keepdims=True)
    bias_ref[...] = jnp.where(jnp.logical_and(ge, sc > neg_inf), 0.0, neg_inf)
    excess = jnp.logical_and(cnt_ge > nsel_f, thr > neg_inf)

    @pl.when(jnp.max(jnp.where(excess, 1.0, 0.0)) > 0.0)
    def _():
        need = nsel_f - cnt_gt
        tri = jnp.where(lax.broadcasted_iota(I32, (LANES, LANES), 0) < lax.broadcasted_iota(I32, (LANES, LANES), 1),
                        1.0, 0.0).astype(BF16)
        before = jnp.zeros((rows, 1), F32)
        for b in range(n_keys // LANES):
            sl = slice(b * LANES, (b + 1) * LANES)
            sblk = jnp.where(pos, w_ref[:, sl], -w_ref[:, sl])
            eq = jnp.where(sblk == thr, 1.0, 0.0)
            rank = _mm(eq.astype(BF16), tri) + before
            keep = jnp.logical_or(sblk > thr, jnp.logical_and(sblk == thr, rank < need))
            bias_ref[:, sl] = jnp.where(jnp.logical_and(keep, sblk > neg_inf), 0.0, neg_inf)
            before = before + jnp.sum(eq, axis=1, keepdims=True)

    lax.fori_loop(0, group, attend_stream, 0)


def _dsa_call(aq, iq, ikw, kk, vv, ikk, group, tq, jq, n_keys, limit_const, n_sel, name, prev=None):
    n_streams, rows_per_stream = aq.shape[0], aq.shape[1]
    extra_specs, extra_args, aliases = [], (), {}
    if prev is not None:
        extra_specs, extra_args, aliases = [pl.BlockSpec(memory_space=pl.ANY)], (prev,), {6: 0}
    qrow = lambda a: pl.BlockSpec((group, tq, a.shape[2]), lambda s: (s, jq, 0))
    krow = lambda a: pl.BlockSpec((group, n_keys, a.shape[2]), lambda s: (s, 0, 0))
    width = DSA_HEADS * DSA_HEAD_DIM
    return pl.pallas_call(
        functools.partial(_dsa_kernel, group=group, tq=tq, n_keys=n_keys, row0=jq * tq, limit_const=limit_const,
                          n_sel=n_sel),
        grid=(n_streams // group,),
        in_specs=[qrow(aq), qrow(iq), qrow(ikw), krow(kk), krow(vv), krow(ikk)] + extra_specs,
        out_specs=pl.BlockSpec((group, tq, width), lambda s: (s, jq, 0)),
        out_shape=jax.ShapeDtypeStruct((n_streams, rows_per_stream, width), BF16),
        input_output_aliases=aliases,
        scratch_shapes=[pltpu.VMEM((group * tq, n_keys), F32), pltpu.VMEM((group * tq, n_keys), F32)],
        compiler_params=pltpu.CompilerParams(dimension_semantics=("parallel",), vmem_limit_bytes=56 * MIB),
        name=name,
    )(aq, iq, ikw, kk, vv, ikk, *extra_args)


def _dsa_cached_kernel(aq_ref, iq_ref, ikwq_ref, ck_ref, cv_ref, cik_ref, nk_ref, nv_ref, o_ref,
                       w_ref, bias_ref, kk_ref, vv_ref, ikk_ref, **static):
    past, new = ck_ref.shape[1], nk_ref.shape[1]
    for dst, cache, fresh in ((kk_ref, ck_ref, nk_ref), (vv_ref, cv_ref, nv_ref), (ikk_ref, cik_ref, ikwq_ref)):
        dst[:, :past, :cache.shape[2]] = cache[...]
        dst[:, past:past + new, :] = fresh[...]
        dst[:, past + new:, :] = jnp.zeros((dst.shape[0], dst.shape[1] - past - new, dst.shape[2]), F32)
    _dsa_kernel(aq_ref, iq_ref, ikwq_ref, kk_ref, vv_ref, ikk_ref, o_ref, w_ref, bias_ref, **static)


def _dsa_cached(aq, iq, ikw, cache_k, cache_v, cache_ik, new_k, new_v, tq):
    n_streams, past, _ = cache_k.shape
    n = aq.shape[0]
    per_stream = lambda a: a.reshape(n_streams, tq, a.shape[1])
    aq, iq, ikw, new_k, new_v = per_stream(aq), per_stream(iq), per_stream(ikw), per_stream(new_k), per_stream(new_v)
    limit = past + tq
    n_keys = -(-limit // LANES) * LANES
    group = max(1, min(n_streams, DSA_ROWS // tq, DSA_GROUP_MAX))
    blk = lambda a: pl.BlockSpec((group,) + a.shape[1:], lambda s: (s, 0, 0))
    width = DSA_HEADS * DSA_HEAD_DIM
    return pl.pallas_call(
        functools.partial(_dsa_cached_kernel, group=group, tq=tq, n_keys=n_keys, row0=0, limit_const=limit,
                          n_sel=min(DSA_TOPK, limit // 4)),
        grid=(n_streams // group,),
        in_specs=[blk(a) for a in (aq, iq, ikw, cache_k, cache_v, cache_ik, new_k, new_v)],
        out_specs=pl.BlockSpec((group, tq, width), lambda s: (s, 0, 0)),
        out_shape=jax.ShapeDtypeStruct((n_streams, tq, width), BF16),
        scratch_shapes=[pltpu.VMEM((group * tq, n_keys), F32), pltpu.VMEM((group * tq, n_keys), F32)] +
                       [pltpu.VMEM((group, n_keys, LANES), F32)] * 3,
        compiler_params=pltpu.CompilerParams(dimension_semantics=("parallel",), vmem_limit_bytes=48 * MIB),
        name="dsa_s",
    )(aq, iq, ikw, cache_k, cache_v, cache_ik, new_k, new_v).reshape(n, width)


def _dsa(aq, iq, ikw, kk, vv, ikk, tq, out):
    n_streams, n_keys, _ = kk.shape
    n = aq.shape[0]
    per_stream = lambda a: a.reshape(n_streams, n // n_streams, a.shape[1])
    aq, iq, ikw = per_stream(aq), per_stream(iq), per_stream(ikw)
    nq = n // n_streams // tq
    n_sel = min(DSA_TOPK, n_keys // 4)
    group = max(1, min(n_streams, DSA_ROWS // tq, DSA_GROUP_MAX))
    out = per_stream(out)
    for jq in range(nq):
        out = _dsa_call(aq, iq, ikw, kk, vv, ikk, group, tq, jq, (jq + 1) * tq, None, n_sel, f"dsa_p{jq}", prev=out)
    return out.reshape(n, -1)


def _pack_rows(v):
    q = QUARTER
    bits = lambda x: pltpu.bitcast(x.astype(BF16).astype(F32), I32)
    pair = lambda c: lax.shift_right_logical(bits(v[:, c * q:(c + 1) * q]), 16) | bits(v[:, (c + 1) * q:(c + 2) * q])
    return pair(0), pair(2)


def _unpack_rows(lo, hi):
    parts = []
    for w in (lo, hi):
        parts.append(pltpu.bitcast(lax.shift_left(w, 16), F32))
        parts.append(pltpu.bitcast(w & jnp.int32(-65536), F32))
    return jnp.concatenate(parts, axis=1)


def _layer_norm(v, g, b):
    mu = jnp.mean(v, axis=-1, keepdims=True)
    d = v - mu
    var = jnp.mean(d * d, axis=-1, keepdims=True)
    return d * lax.rsqrt(var + LN_EPS) * g + b


def _merge_kernel(x_ref, ret_ref, od_ref, gr_ref, ga_ref, wr_ref, wd_ref, wo_ref, g1_ref, b1_ref, rwt_ref, rb_ref,
                  tri_ref, cnt0_ref, h1_ref, hlo_ref, hhi_ref, gtm_ref, ek_ref, rk_ref, tot_ref, cnt_ref):
    @pl.when(pl.program_id(0) == 0)
    def _():
        cnt_ref[...] = cnt0_ref[:, 0:1]

    y_ret = _mm(ret_ref[...], wr_ref[...])
    y_dsa = _mm(od_ref[...], wd_ref[...])
    merged = jax.nn.sigmoid(gr_ref[...]) * y_ret + jax.nn.sigmoid(ga_ref[...]) * y_dsa
    mix = _mm(merged.astype(BF16), wo_ref[...])
    h1 = _layer_norm(DEEPNORM_ALPHA * x_ref[...] + mix, g1_ref[...], b1_ref[...])
    h1_ref[...] = h1
    hlo_ref[...], hhi_ref[...] = _pack_rows(h1)

    logits = lax.dot_general(rwt_ref[...], h1, (((1,), (1,)), ((), ())), preferred_element_type=F32,
                             precision=lax.Precision.HIGHEST) + rb_ref[...]
    tm = logits.shape[1]
    e_iota = lax.broadcasted_iota(I32, (N_EXPERTS, tm), 0)
    tops, hots, firsts = [], [], []
    for _ in range(MOE_TOP_K):
        m = jnp.max(logits, axis=0, keepdims=True)
        first = jnp.min(jnp.where(logits == m, e_iota, N_EXPERTS), axis=0, keepdims=True)
        hot = e_iota == first
        tops.append(m)
        hots.append(hot)
        firsts.append(first)
        logits = jnp.where(hot, -jnp.inf, logits)
    exps = [jnp.exp(m - tops[0]) for m in tops]
    den = exps[0] + exps[1] + exps[2] + exps[3]
    sel = jnp.zeros((N_EXPERTS, tm), F32)
    for hot in hots:
        sel = sel + jnp.where(hot, 1.0, 0.0)
    rank = _mm(sel.astype(BF16), tri_ref[...]) + cnt_ref[...]
    ranks = [jnp.sum(jnp.where(hot, rank, 0.0), axis=0, keepdims=True).astype(I32) for hot in hots]
    pad_i = jnp.zeros((8 - MOE_TOP_K, tm), I32)
    ek_ref[...] = jnp.concatenate(firsts + [pad_i], axis=0)
    rk_ref[...] = jnp.concatenate(ranks + [pad_i], axis=0)
    gates = jnp.concatenate([e / den for e in exps] + [jnp.zeros((LANES - MOE_TOP_K, tm), F32)], axis=0)
    gtm_ref[...] = gates.T
    cnt_ref[...] = cnt_ref[...] + jnp.sum(sel, axis=1, keepdims=True)
    tot_ref[...] = jnp.broadcast_to(cnt_ref[...], tot_ref.shape)


def _merge(x, ret, od, gr, ga, wr, wd, wo, g1, b1, rwt, rb, cnt0):
    n = x.shape[0]
    tm = MERGE_TM
    row = lambda w: pl.BlockSpec((tm, w), lambda i: (i, 0))
    const = lambda a: pl.BlockSpec(a.shape, lambda i: (0,) * a.ndim)
    col = pl.BlockSpec((8, tm), lambda i: (0, i))
    tri = jnp.asarray(np.triu(np.ones((tm, tm), np.float32), 1), BF16)
    return pl.pallas_call(
        _merge_kernel,
        grid=(n // tm,),
        in_specs=[row(D_MODEL), row(1024), row(512), row(1024), row(1024), const(wr), const(wd), const(wo),
                  const(g1), const(b1), const(rwt), const(rb), const(tri), const(cnt0)],
        out_specs=[row(D_MODEL), row(QUARTER), row(QUARTER), row(LANES), col, col,
                   pl.BlockSpec((N_EXPERTS, LANES), lambda i: (0, 0))],
        out_shape=[jax.ShapeDtypeStruct((n, D_MODEL), F32), jax.ShapeDtypeStruct((n, QUARTER), I32),
                   jax.ShapeDtypeStruct((n, QUARTER), I32), jax.ShapeDtypeStruct((n, LANES), F32),
                   jax.ShapeDtypeStruct((8, n), I32), jax.ShapeDtypeStruct((8, n), I32),
                   jax.ShapeDtypeStruct((N_EXPERTS, LANES), F32)],
        scratch_shapes=[pltpu.VMEM((N_EXPERTS, 1), F32)],
        compiler_params=pltpu.CompilerParams(dimension_semantics=("arbitrary",), vmem_limit_bytes=48 * MIB),
        name="merge",
    )(x, ret, od, gr, ga, wr, wd, wo, g1, b1, rwt, rb, tri, cnt0)


def _deinterleave_kernel(w_ref, o_ref):
    r = lax.broadcasted_iota(I32, (UP_BLOCK, UP_BLOCK), 0)
    c = lax.broadcasted_iota(I32, (UP_BLOCK, UP_BLOCK), 1)
    src = jnp.where(c < LANES, 2 * c, 2 * (c - LANES) + 1)
    perm = jnp.where(r == src, 1.0, 0.0).astype(BF16)
    for b in range(w_ref.shape[2] // UP_BLOCK):
        sl = slice(b * UP_BLOCK, (b + 1) * UP_BLOCK)
        o_ref[0, :, sl] = _mm(w_ref[0, :, sl].astype(BF16), perm).astype(BF16)


def _deinterleave_w_up(w_up):
    n_e, d_in, d_out = w_up.shape
    cols = d_out
    spec = pl.BlockSpec((1, d_in, cols), lambda e, c: (e, 0, c))
    return pl.pallas_call(
        _deinterleave_kernel,
        grid=(n_e, d_out // cols),
        in_specs=[spec],
        out_specs=spec,
        out_shape=jax.ShapeDtypeStruct(w_up.shape, BF16),
        compiler_params=pltpu.CompilerParams(dimension_semantics=("parallel", "parallel"),
                                             vmem_limit_bytes=40 * MIB),
        name="w_up_prep",
    )(w_up)


def _pos_kernel(off_ref, ek_ref, rk_ref, pos_ref):
    ek = ek_ref[...]
    pos = rk_ref[...]
    for e in range(N_EXPERTS):
        pos = pos + jnp.where(ek == e, off_ref[e], 0)
    pos_ref[...] = pos


def _positions(off, ek, rk):
    n = ek.shape[1]
    tn = min(n, 2048)
    spec = pl.BlockSpec((8, tn), lambda i, off: (0, i))
    return pl.pallas_call(
        _pos_kernel,
        grid_spec=pltpu.PrefetchScalarGridSpec(num_scalar_prefetch=1, grid=(n // tn,), in_specs=[spec, spec],
                                               out_specs=spec),
        out_shape=jax.ShapeDtypeStruct((8, n), I32),
        name="moe_pos",
    )(off, ek, rk)


def _sc_mesh():
    return plsc.VectorSubcoreMesh(core_axis_name="core", subcore_axis_name="subcore")


def _sc_scatter(xs, pos_flat, n_rows):
    width, dtype = xs[0].shape[1], xs[0].dtype
    windows = [x.shape[0] // SC_WINDOW for x in xs]
    nw = sum(windows)

    @functools.partial(pl.kernel, out_type=jax.ShapeDtypeStruct((n_rows, width), dtype), mesh=_sc_mesh())
    def scatter(*refs):
        p_hbm, o_hbm = refs[len(xs)], refs[len(xs) + 1]

        def body(x_vmem, *p_vmem):
            for p in p_vmem:
                pltpu.sync_copy(x_vmem, o_hbm.at[p.at[0]])

        first = 0
        for x_hbm, nw_g in zip(refs[:len(xs)], windows):
            pltpu.emit_pipeline(
                body, grid=(nw_g,),
                in_specs=[pl.BlockSpec((SC_WINDOW, width), lambda i: (i, 0))] +
                         [pl.BlockSpec((1, SC_WINDOW), functools.partial(lambda w0, i: (0, w0 + i), k * nw + first))
                          for k in range(MOE_TOP_K)],
                out_specs=[], core_axis_name=("core", "subcore"), dimension_semantics=(pltpu.PARALLEL,),
            )(x_hbm, *([p_hbm] * MOE_TOP_K))
            first += nw_g

    return scatter(*xs, pos_flat)


def _sc_gather(y, pos_flat):
    m = pos_flat.shape[1]

    @functools.partial(pl.kernel, out_type=jax.ShapeDtypeStruct((m, y.shape[1]), y.dtype), mesh=_sc_mesh())
    def gather(y_hbm, p_hbm, o_hbm):
        def body(p_vmem, o_vmem):
            pltpu.sync_copy(y_hbm.at[p_vmem.at[0]], o_vmem)

        pltpu.emit_pipeline(
            body, grid=(m // SC_WINDOW,),
            in_specs=[pl.BlockSpec((1, SC_WINDOW), lambda i: (0, i))],
            out_specs=[pl.BlockSpec((SC_WINDOW, y.shape[1]), lambda i: (i, 0))],
            core_axis_name=("core", "subcore"), dimension_semantics=(pltpu.PARALLEL,),
        )(p_hbm, o_hbm)

    return gather(y, pos_flat)


def _ffn_kernel(be_ref, nu_ref, xlo_ref, xhi_ref, wup_ref, bup_ref, wdn_ref, bdn_ref, ylo_ref, yhi_ref):
    @pl.when(pl.program_id(0) < nu_ref[0])
    def _():
        x = _unpack_rows(xlo_ref[...], xhi_ref[...]).astype(BF16)
        h = _mm(x, wup_ref[0]) + bup_ref[0]
        acts = []
        for b in range(2 * D_FF // UP_BLOCK):
            glu = jnp.minimum(h[:, b * UP_BLOCK:b * UP_BLOCK + LANES], SWIGLU_LIMIT)
            lin = jnp.clip(h[:, b * UP_BLOCK + LANES:(b + 1) * UP_BLOCK], -SWIGLU_LIMIT, SWIGLU_LIMIT)
            acts.append(glu * jax.nn.sigmoid(SWIGLU_ALPHA * glu) * (lin + 1.0))
        act = jnp.concatenate(acts, axis=1)
        ylo_ref[...], yhi_ref[...] = _pack_rows(_mm(act.astype(BF16), wdn_ref[0]) + bdn_ref[0])


def _ffn(block_expert, n_used, xs_lo, xs_hi, wup, bup, wdn, bdn):
    rows = xs_lo.shape[0]
    blk = FFN_BLOCK
    row = pl.BlockSpec((blk, QUARTER), lambda i, be, nu: (i, 0))
    per_expert = lambda a: pl.BlockSpec((1,) + a.shape[1:], lambda i, be, nu: (be[i], 0, 0))
    return pl.pallas_call(
        _ffn_kernel,
        grid_spec=pltpu.PrefetchScalarGridSpec(
            num_scalar_prefetch=2, grid=(rows // blk,),
            in_specs=[row, row, per_expert(wup), per_expert(bup), per_expert(wdn), per_expert(bdn)],
            out_specs=[row, row]),
        out_shape=[jax.ShapeDtypeStruct((rows, QUARTER), I32), jax.ShapeDtypeStruct((rows, QUARTER), I32)],
        compiler_params=pltpu.CompilerParams(dimension_semantics=("arbitrary",), vmem_limit_bytes=40 * MIB),
        name="moe_ffn",
    )(block_expert, n_used, xs_lo, xs_hi, wup, bup, wdn, bdn)


def _combine_kernel(h1_ref, olo_ref, ohi_ref, gtm_ref, g2_ref, b2_ref, o_ref):
    g = gtm_ref[...]
    y = jnp.zeros(h1_ref.shape, F32)
    for k in range(MOE_TOP_K):
        y = y + g[:, k:k + 1] * _unpack_rows(olo_ref[k], ohi_ref[k])
    o_ref[...] = _layer_norm(DEEPNORM_ALPHA * h1_ref[...] + y, g2_ref[...], b2_ref[...])


def _combine(h1, og_lo, og_hi, gtm, g2, b2, row0):
    n = h1.shape[0]
    n_all = og_lo.shape[0] // MOE_TOP_K
    tm = MERGE_TM
    row = lambda w: pl.BlockSpec((tm, w), lambda i: (i, 0))
    const = lambda a: pl.BlockSpec(a.shape, lambda i: (0,) * a.ndim)
    picked = pl.BlockSpec((MOE_TOP_K, tm, QUARTER), lambda i: (0, i + row0 // tm, 0))
    return pl.pallas_call(
        _combine_kernel,
        grid=(n // tm,),
        in_specs=[row(D_MODEL), picked, picked, row(LANES), const(g2), const(b2)],
        out_specs=row(D_MODEL),
        out_shape=jax.ShapeDtypeStruct((n, D_MODEL), F32),
        compiler_params=pltpu.CompilerParams(dimension_semantics=("parallel",), vmem_limit_bytes=40 * MIB),
        name="moe_combine",
    )(h1, og_lo.reshape(MOE_TOP_K, n_all, QUARTER), og_hi.reshape(MOE_TOP_K, n_all, QUARTER), gtm, g2, b2)


def _moe(groups, totals, wup, bup, wdn, bdn, g2, b2):
    blk = FFN_BLOCK
    sizes = [g[0].shape[0] for g in groups]
    n_all = sum(sizes)
    n_rows = MOE_TOP_K * n_all + N_EXPERTS * blk
    counts = totals[:, 0].astype(I32)
    padded = (counts + blk - 1) // blk * blk
    ends = jnp.cumsum(padded)
    block_start = jnp.arange(n_rows // blk, dtype=I32) * blk
    block_expert = jnp.minimum(jnp.sum((ends[None, :] <= block_start[:, None]).astype(I32), axis=1), N_EXPERTS - 1)
    n_used = (ends[-1:] // blk).astype(I32)
    cat = lambda i, axis: jnp.concatenate([g[i] for g in groups], axis=axis)
    pos = _positions((ends - padded).astype(I32), cat(4, 1), cat(5, 1))
    pos_flat = pos[:MOE_TOP_K].reshape(1, MOE_TOP_K * n_all)
    xs_lo = _sc_scatter([g[1] for g in groups], pos_flat, n_rows)
    xs_hi = _sc_scatter([g[2] for g in groups], pos_flat, n_rows)
    ys_lo, ys_hi = _ffn(block_expert, n_used, xs_lo, xs_hi, wup, bup, wdn, bdn)
    og_lo, og_hi = _sc_gather(ys_lo, pos_flat), _sc_gather(ys_hi, pos_flat)
    starts = np.cumsum([0] + sizes[:-1]).tolist()
    return [_combine(g[0], og_lo, og_hi, g[3], g2, b2, r0) for g, r0 in zip(groups, starts)]


def _mixer(x, pos_tab, tab_period, s0, rows_per_stream, ret_rows, caches, weights, cnt0):
    wp, wr, wd, wo, g1, b1, rwt, rb = weights
    n = x.shape[0]
    n_streams = n // rows_per_stream
    rq, rk, rv, rg, aq, ak, av, iq, ikw, gr, ga, od0 = _project(x, wp, pos_tab, tab_period)
    ret, s_new = _retention(rq, rk, rv, rg, s0, rows_per_stream, ret_rows)
    if caches is None:
        per_stream = lambda a: a.reshape(n_streams, rows_per_stream, a.shape[1])
        od = _dsa(aq, iq, ikw, per_stream(ak), per_stream(av), per_stream(ikw), DSA_TQ, od0)
    else:
        od = _dsa_cached(aq, iq, ikw, *caches, ak, av, rows_per_stream)
    h1, h_lo, h_hi, gtm, ek, rk_, totals = _merge(x, ret, od, gr, ga, wr, wd, wo, g1, b1, rwt, rb, cnt0)
    return (h1, h_lo, h_hi, gtm, ek, rk_), totals, (s_new, ak, av, ikw[:, :IDX_DIM])


def kernel(x_prompt, x_sample, state_ret, cache_k, cache_v, cache_idx_k, w_in, w_ret_o, w_dsa_o, w_o,
           ln1_g, ln1_b, router_w, router_b, w_up, b_up, w_down, b_down, ln2_g, ln2_b):
    assert w_in.shape[0] == DEPTH
    batch, seq, _ = x_prompt.shape
    dec_batch, dec_seq, _ = x_sample.shape
    past = cache_k.shape[2]
    assert seq % PROJ_TM == 0 and seq % DSA_TQ == 0 and PROJ_TM % dec_seq == 0

    l = 0
    mixer_w = (_pack_w_in(w_in[l]), w_ret_o[l].astype(BF16), w_dsa_o[l].astype(BF16), w_o[l].astype(BF16),
               ln1_g[l][None, :], ln1_b[l][None, :], router_w[l].T, router_b[l][:, None])
    moe_w = (
        _deinterleave_w_up(w_up[l]),
        b_up[l].reshape(N_EXPERTS, 2 * D_FF // UP_BLOCK, LANES, 2).transpose(0, 1, 3, 2).reshape(N_EXPERTS, 1, 2 * D_FF),
        w_down[l].astype(BF16), b_down[l][:, None, :], ln2_g[l][None, :], ln2_b[l][None, :])

    tab_p = _rot_tables(jnp.arange(seq))
    zeros_state = jnp.zeros((batch, RET_HEADS, RET_DK, RET_DV), F32)
    moe_p, totals_p, (s_p, k_p, v_p, ik_p) = _mixer(
        x_prompt.reshape(batch * seq, D_MODEL), tab_p, seq // PROJ_TM, zeros_state, seq, RET_CHUNK, None, mixer_w,
        jnp.zeros((N_EXPERTS, LANES), F32))

    tab_s = jnp.tile(_rot_tables(past + jnp.arange(dec_seq)), (PROJ_TM // dec_seq, 1))
    caches = (cache_k[l].reshape(dec_batch, past, LANES), cache_v[l].reshape(dec_batch, past, LANES), cache_idx_k[l])
    moe_s, totals, (s_s, k_s, v_s, ik_s) = _mixer(
        x_sample.reshape(dec_batch * dec_seq, D_MODEL), tab_s, 1, state_ret[l], dec_seq, dec_seq, caches, mixer_w,
        totals_p)

    y_p, y_s = _moe([moe_p, moe_s], totals, *moe_w)

    kv = (DSA_KV_HEADS, DSA_HEAD_DIM)
    return (y_p.reshape(batch, seq, D_MODEL), y_s.reshape(dec_batch, dec_seq, D_MODEL),
            s_p[None], k_p.reshape(1, batch, seq, *kv), v_p.reshape(1, batch, seq, *kv),
            ik_p.reshape(1, batch, seq, IDX_DIM),
            s_s[None], k_s.reshape(1, dec_batch, dec_seq, *kv), v_s.reshape(1, dec_batch, dec_seq, *kv),
            ik_s.reshape(1, dec_batch, dec_seq, IDX_DIM))
```

```python
import functools

import numpy as np
import jax
import jax.numpy as jnp
from jax import lax
from jax.experimental import pallas as pl
from jax.experimental.pallas import tpu as pltpu
from jax.experimental.pallas import tpu_sc as plsc

F32 = jnp.float32
BF16 = jnp.bfloat16
I32 = jnp.int32

D_MODEL = 1024
CHUNK = 64
RET_HEADS = 4
RET_DK = 128
RET_DV = 256
RET_ROPE_BASE = 10000.0
DSA_HEADS = 8
DSA_KV_HEADS = 2
DSA_HEAD_DIM = 64
IDX_HEADS = 8
IDX_DIM = 64
DSA_TOPK = 256
ROPE_THETA = 500000.0
N_EXPERTS = 32
MOE_TOP_K = 4
D_FF = 1024
SWIGLU_ALPHA = 1.702
SWIGLU_LIMIT = 7.0
LN_EPS = 1e-5
GN_EPS = 1e-6
DEPTH = 1
DEEPNORM_ALPHA = (2.0 * DEPTH) ** 0.25
PROJ_WIDTHS = (RET_HEADS * RET_DK, RET_HEADS * RET_DK, RET_HEADS * RET_DV, RET_HEADS * RET_DV,
               DSA_HEADS * DSA_HEAD_DIM, DSA_KV_HEADS * DSA_HEAD_DIM, DSA_KV_HEADS * DSA_HEAD_DIM,
               IDX_HEADS * IDX_DIM, IDX_DIM, IDX_HEADS, D_MODEL, D_MODEL)

assert CHUNK & (CHUNK - 1) == 0

LANES = 128
MIB = 1024 * 1024

OFF_RQ, OFF_RK, OFF_RV, OFF_RG = 0, 512, 1024, 2048
OFF_AQ, OFF_AK, OFF_AV, OFF_IQ, OFF_IKW = 3072, 3584, 3712, 3840, 4352
OFF_GR, OFF_GA, PACKED_COLS = 4480, 5504, 6528
TAB_COLS = 8 * LANES

PROJ_TM = 512
RET_CHUNK = 128
RET_CHUNKS_PER_STEP = 8
DSA_TQ = 256
DSA_ROWS = 512
DSA_GROUP_MAX = 4
SEARCH_UNROLL = 4
DSA_SHORT_EXTENT = 1536
MERGE_TM = 512
FFN_BLOCK = 512
SC_WINDOW = 128
QUARTER = D_MODEL // 4
UP_BLOCK = 2 * LANES


def _nt(a, b):
    return lax.dot_general(a, b, (((1,), (1,)), ((), ())), preferred_element_type=F32)


def _mm(a, b):
    return jnp.dot(a, b, preferred_element_type=F32)


def _proj_kernel(x_ref, w_ref, tab_ref, rq_ref, rk_ref, rv_ref, rg_ref, aq_ref, ak_ref, av_ref,
                 iq_ref, ikw_ref, gr_ref, ga_ref, od0_ref):
    xb = x_ref[...].astype(BF16)
    od0_ref[...] = jnp.zeros(od0_ref.shape, od0_ref.dtype)

    def mm(c0, n):
        return _mm(xb, w_ref[:, c0:c0 + n])

    def tab(i):
        return tab_ref[:, i * LANES:(i + 1) * LANES]

    def rot_full(z):
        return z * tab(0) + pltpu.roll(z, 64, 1) * tab(1)

    def rot_part(z, c):
        return z * tab(c) + pltpu.roll(z, LANES - 8, 1) * tab(c + 1) + pltpu.roll(z, 8, 1) * tab(c + 2)

    z = mm(OFF_RQ, 512)
    for h in range(4):
        sl = slice(h * LANES, (h + 1) * LANES)
        rq_ref[:, sl] = rot_full(z[:, sl]).astype(BF16)
    z = mm(OFF_RK, 512)
    for h in range(4):
        sl = slice(h * LANES, (h + 1) * LANES)
        rk_ref[:, sl] = (rot_full(z[:, sl]) * (RET_DK ** -0.5)).astype(BF16)
    for c in range(2):
        rv_ref[:, c * 512:(c + 1) * 512] = mm(OFF_RV + c * 512, 512).astype(BF16)
    for c in range(2):
        rg_ref[:, c * 512:(c + 1) * 512] = mm(OFF_RG + c * 512, 512)
    z = mm(OFF_AQ, 512)
    for h in range(4):
        sl = slice(h * LANES, (h + 1) * LANES)
        aq_ref[:, sl] = rot_part(z[:, sl], 2).astype(BF16)
    z = mm(OFF_AK, 256)
    ak_ref[...] = rot_part(z[:, :LANES], 2)
    av_ref[...] = z[:, LANES:]
    z = mm(OFF_IQ, 512)
    for h in range(4):
        sl = slice(h * LANES, (h + 1) * LANES)
        iq_ref[:, sl] = rot_part(z[:, sl], 2).astype(BF16)
    ikw_ref[...] = rot_part(mm(OFF_IKW, LANES), 5)
    for c in range(2):
        gr_ref[:, c * 512:(c + 1) * 512] = mm(OFF_GR + c * 512, 512)
    for c in range(2):
        ga_ref[:, c * 512:(c + 1) * 512] = mm(OFF_GA + c * 512, 512)


def _rot_tables(pos):
    p = pos.shape[0]
    posf = pos.astype(F32)[:, None]
    ret_f = RET_ROPE_BASE ** (-jnp.linspace(0.0, 1.0, RET_DK // 2, dtype=F32))
    ang = posf * ret_f[None, :]
    c, s = jnp.cos(ang), jnp.sin(ang)
    cos_r = jnp.concatenate([c, c], 1)
    sin_r = jnp.concatenate([-s, s], 1)
    n_rot = DSA_HEAD_DIM // 4
    att_f = ROPE_THETA ** (-jnp.arange(0, n_rot, 2, dtype=F32) / n_rot)
    ang2 = posf * att_f[None, :]
    c2, s2 = jnp.cos(ang2), jnp.sin(ang2)
    half = n_rot // 2
    rest = DSA_HEAD_DIM - 2 * half
    c64 = jnp.concatenate([c2, c2, jnp.ones((p, rest), F32)], 1)
    s1_64 = jnp.concatenate([-s2, jnp.zeros((p, DSA_HEAD_DIM - half), F32)], 1)
    s2_64 = jnp.concatenate([jnp.zeros((p, half), F32), s2, jnp.zeros((p, rest), F32)], 1)
    z64 = jnp.zeros((p, DSA_HEAD_DIM), F32)
    ci = jnp.concatenate([c64, jnp.full((p, IDX_HEADS), IDX_HEADS ** -0.5, F32),
                          jnp.zeros((p, DSA_HEAD_DIM - IDX_HEADS), F32)], 1)
    return jnp.concatenate([cos_r, sin_r,
                            jnp.concatenate([c64, c64], 1), jnp.concatenate([s1_64, s1_64], 1),
                            jnp.concatenate([s2_64, s2_64], 1),
                            ci, jnp.concatenate([s1_64, z64], 1), jnp.concatenate([s2_64, z64], 1)], 1)


def _pack_w_in(w):
    cuts = np.cumsum(PROJ_WIDTHS)[:-1].tolist()
    rq, rk, rv, rg, aq, ak, av, iq, ik, iw, gr, ga = jnp.split(w, cuts, axis=1)
    pad = jnp.zeros((w.shape[0], LANES - IDX_DIM - IDX_HEADS), w.dtype)
    return jnp.concatenate([rq, rk, rv, rg, aq, ak, av, iq, ik, iw, pad, gr, ga], axis=1).astype(BF16)


def _project(x, wp, tab, tab_period):
    n = x.shape[0]
    tm = PROJ_TM
    row = lambda w: pl.BlockSpec((tm, w), lambda i: (i, 0))
    out_shapes = [((n, 512), BF16), ((n, 512), BF16), ((n, 1024), BF16), ((n, 1024), F32),
                  ((n, 512), BF16), ((n, LANES), F32), ((n, LANES), F32), ((n, 512), BF16),
                  ((n, LANES), F32), ((n, 1024), F32), ((n, 1024), F32), ((n, DSA_HEADS * DSA_HEAD_DIM), BF16)]
    return pl.pallas_call(
        _proj_kernel,
        grid=(n // tm,),
        in_specs=[row(D_MODEL),
                  pl.BlockSpec((D_MODEL, PACKED_COLS), lambda i: (0, 0), pipeline_mode=pl.Buffered(1)),
                  pl.BlockSpec((tm, TAB_COLS), lambda i: (i % tab_period, 0))],
        out_specs=[row(s[1]) for s, _ in out_shapes],
        out_shape=[jax.ShapeDtypeStruct(s, d) for s, d in out_shapes],
        compiler_params=pltpu.CompilerParams(dimension_semantics=("parallel",), vmem_limit_bytes=52 * MIB),
        name="proj",
    )(x, wp, tab)


def _ret_kernel(dec_ref, xi_ref, zeta_ref, rq_ref, rk_ref, rv_ref, rg_ref, s0_ref, ret_ref, sout_ref, st_ref,
                *, rows, n_chunk, g_pow):
    j = pl.program_id(1)
    cpad = RET_CHUNK

    @pl.when(j == 0)
    def _():
        st_ref[...] = s0_ref[0]

    def padded(v):
        if rows == cpad:
            return v
        return jnp.concatenate([v, jnp.zeros((cpad - rows, v.shape[1]), v.dtype)], axis=0)

    for c in range(n_chunk):
        rs = slice(c * rows, (c + 1) * rows)
        for h in range(RET_HEADS):
            ks = slice(h * RET_DK, (h + 1) * RET_DK)
            vs = slice(h * RET_DV, (h + 1) * RET_DV)
            q = padded(rq_ref[rs, ks])
            kt = padded(rk_ref[rs, ks].astype(F32)).T
            v = padded(rv_ref[rs, vs])
            s = st_ref[h]
            sc = _mm(q, kt.astype(BF16)) * dec_ref[h]
            o = _mm(sc.astype(BF16), v) + _mm(q, s.astype(BF16)) * xi_ref[h]
            st_ref[h] = g_pow[h] * s + _mm((kt * zeta_ref[h]).astype(BF16), v)
            o = o[:rows]
            mu = jnp.mean(o, axis=-1, keepdims=True)
            d = o - mu
            var = jnp.mean(d * d, axis=-1, keepdims=True)
            gn = d * lax.rsqrt(var + GN_EPS)
            g = rg_ref[rs, vs]
            ret_ref[rs, vs] = (gn * (g * jax.nn.sigmoid(g))).astype(BF16)

    @pl.when(j == pl.num_programs(1) - 1)
    def _():
        sout_ref[0] = st_ref[...]


def _retention(rq, rk, rv, rg, s0, rows_per_stream, rows):
    n = rq.shape[0]
    n_streams = n // rows_per_stream
    n_chunk = min(rows_per_stream // rows, RET_CHUNKS_PER_STEP)
    blk = rows * n_chunk
    nb = rows_per_stream // blk
    gam = 1.0 - 2.0 ** (-5.0 - np.arange(RET_HEADS, dtype=np.float64))
    i = np.arange(RET_CHUNK, dtype=np.float64)
    diff = i[:, None] - i[None, :]
    dec = np.where(diff >= 0, gam[:, None, None] ** np.maximum(diff, 0.0)[None], 0.0)
    xi = gam[:, None, None] ** (i + 1.0)[None, :, None]
    zeta = np.where(i < rows, gam[:, None, None] ** (rows - 1.0 - i)[None, None, :], 0.0)
    g_pow = tuple(float(g ** rows) for g in gam)
    const = lambda shape: pl.BlockSpec(shape, lambda s, j: (0,) * len(shape))
    row = lambda w: pl.BlockSpec((blk, w), lambda s, j: (s * nb + j, 0))
    st = pl.BlockSpec((1, RET_HEADS, RET_DK, RET_DV), lambda s, j: (s, 0, 0, 0))
    return pl.pallas_call(
        functools.partial(_ret_kernel, rows=rows, n_chunk=n_chunk, g_pow=g_pow),
        grid=(n_streams, nb),
        in_specs=[const((RET_HEADS, RET_CHUNK, RET_CHUNK)), const((RET_HEADS, RET_CHUNK, 1)),
                  const((RET_HEADS, 1, RET_CHUNK)), row(512), row(512), row(1024), row(1024), st],
        out_specs=[row(1024), st],
        out_shape=[jax.ShapeDtypeStruct((n, RET_HEADS * RET_DV), BF16),
                   jax.ShapeDtypeStruct((n_streams, RET_HEADS, RET_DK, RET_DV), F32)],
        scratch_shapes=[pltpu.VMEM((RET_HEADS, RET_DK, RET_DV), F32)],
        compiler_params=pltpu.CompilerParams(dimension_semantics=("parallel", "arbitrary"),
                                             vmem_limit_bytes=32 * MIB),
        name="retention",
    )(jnp.asarray(dec, F32), jnp.asarray(xi, F32), jnp.asarray(zeta, F32), rq, rk, rv, rg, s0)


def _dsa_kernel(aq_ref, iq_ref, ikwq_ref, kk_ref, vv_ref, ikk_ref, *rest,
                group, tq, n_keys, row0, limit_const, n_sel):
    o_ref, w_ref, bias_ref = rest[-3:]
    rows = group * tq
    stack_pairs = n_keys <= DSA_SHORT_EXTENT
    search_unroll = 2 * SEARCH_UNROLL if n_keys <= DSA_SHORT_EXTENT else SEARCH_UNROLL
    nsel_f = float(n_sel)
    neg_inf = -jnp.inf

    col = lax.broadcasted_iota(I32, (tq, n_keys), 1)
    if limit_const is None:
        chunk_of_row = jnp.right_shift(lax.broadcasted_iota(I32, (tq, 1), 0) + row0, CHUNK.bit_length() - 1)
        limit = (chunk_of_row + 1) * CHUNK
    else:
        limit = limit_const

    def attend_stream(s, carry):
        bias = bias_ref[pl.ds(pl.multiple_of(s * tq, tq), tq), :]
        kfull = kk_ref[s]
        vfull = vv_ref[s]
        for g in range(DSA_KV_HEADS):
            gs = slice(g * DSA_HEAD_DIM, (g + 1) * DSA_HEAD_DIM)
            kg = kfull[:, gs].astype(BF16)
            vg = vfull[:, gs].astype(BF16)
            for pp in range(2):
                p = 2 * g + pp
                slab = (aq_ref[s, :, p * LANES:(p + 1) * LANES].astype(F32) * (DSA_HEAD_DIM ** -0.5)).astype(BF16)
                heads = [slab[:, hh * DSA_HEAD_DIM:(hh + 1) * DSA_HEAD_DIM] for hh in range(2)]
                if stack_pairs:
                    heads = [jnp.concatenate(heads, axis=0)]
                outs = []
                for qh in heads:
                    lg = _nt(qh, kg) + (jnp.concatenate([bias, bias], axis=0) if stack_pairs else bias)
                    m = jnp.max(lg, axis=1, keepdims=True)
                    pr = jnp.exp(lg - m)
                    den = jnp.sum(pr, axis=1, keepdims=True)
                    outs.append(_mm(pr.astype(BF16), vg) / den)
                if stack_pairs:
                    outs = [outs[0][:tq], outs[0][tq:]]
                o_ref[s, :, p * LANES:(p + 1) * LANES] = jnp.concatenate(outs, axis=1).astype(BF16)
        return carry

    if n_keys <= n_sel:
        for s in range(group):
            bias_ref[s * tq:(s + 1) * tq, :] = jnp.where(col < limit, 0.0, neg_inf)
        lax.fori_loop(0, group, attend_stream, 0)
        return

    def score_stream(s, carry):
        ikb = ikk_ref[s][:, :IDX_DIM].astype(BF16)
        iww = ikwq_ref[s][:, IDX_DIM:IDX_DIM + IDX_HEADS] * (IDX_DIM ** -0.5)
        acc = jnp.zeros((tq, n_keys), F32)
        for p in range(IDX_HEADS // 2):
            slab = iq_ref[s, :, p * LANES:(p + 1) * LANES]
            for hh in range(2):
                h = 2 * p + hh
                acc = acc + jnp.maximum(_nt(slab[:, hh * IDX_DIM:(hh + 1) * IDX_DIM], ikb), 0.0) * iww[:, h:h + 1]
        w_ref[pl.ds(pl.multiple_of(s * tq, tq), tq), :] = jnp.where(col < limit, acc, neg_inf)
        return carry

    lax.fori_loop(0, group, score_stream, 0)

    sc = w_ref[...]
    pos = jnp.sum(jnp.where(sc >= 0.0, 1.0, 0.0), axis=1, keepdims=True) >= nsel_f
    kk = jnp.where(pos, nsel_f, float(n_keys - n_sel + 1))
    w_ref[...] = jnp.where(pos, sc, -sc)

    def bit_step(i, u):
        cand_u = u | jnp.left_shift(jnp.int32(1), 30 - i)
        cand = pltpu.bitcast(cand_u, F32)
        cnt = jnp.sum(jnp.where(w_ref[...] >= cand, 1.0, 0.0), axis=1, keepdims=True)
        return jnp.where(cnt >= kk, cand_u, u)

    mag_u = lax.fori_loop(0, 31, bit_step, jnp.zeros((rows, 1), I32), unroll=search_unroll)
    mag = pltpu.bitcast(mag_u, F32)
    thr = jnp.where(pos, mag, -mag)

    sc = jnp.where(pos, w_ref[...], -w_ref[...])
    short = jnp.sum(jnp.where(sc >= thr, 1.0, 0.0), axis=1, keepdims=True) < nsel_f
    thr = jnp.where(jnp.logical_and(short, jnp.logical_not(pos)), -pltpu.bitcast(mag_u + 1, F32), thr)
    ge = sc >= thr
    cnt_gt = jnp.sum(jnp.where(sc > thr, 1.0, 0.0), axis=1, keepdims=True)
    cnt_ge = jnp.sum(jnp.where(ge, 1.0, 0.0), axis=1, keepdims=True)
    bias_ref[...] = jnp.where(jnp.logical_and(ge, sc > neg_inf), 0.0, neg_inf)
    excess = jnp.logical_and(cnt_ge > nsel_f, thr > neg_inf)

    @pl.when(jnp.max(jnp.where(excess, 1.0, 0.0)) > 0.0)
    def _():
        need = nsel_f - cnt_gt
        tri = jnp.where(lax.broadcasted_iota(I32, (LANES, LANES), 0) < lax.broadcasted_iota(I32, (LANES, LANES), 1),
                        1.0, 0.0).astype(BF16)
        before = jnp.zeros((rows, 1), F32)
        for b in range(n_keys // LANES):
            sl = slice(b * LANES, (b + 1) * LANES)
            sblk = jnp.where(pos, w_ref[:, sl], -w_ref[:, sl])
            eq = jnp.where(sblk == thr, 1.0, 0.0)
            rank = _mm(eq.astype(BF16), tri) + before
            keep = jnp.logical_or(sblk > thr, jnp.logical_and(sblk == thr, rank < need))
            bias_ref[:, sl] = jnp.where(jnp.logical_and(keep, sblk > neg_inf), 0.0, neg_inf)
            before = before + jnp.sum(eq, axis=1, keepdims=True)

    lax.fori_loop(0, group, attend_stream, 0)


def _dsa_call(aq, iq, ikw, kk, vv, ikk, group, tq, jq, n_keys, limit_const, n_sel, name, prev=None):
    n_streams, rows_per_stream = aq.shape[0], aq.shape[1]
    extra_specs, extra_args, aliases = [], (), {}
    if prev is not None:
        extra_specs, extra_args, aliases = [pl.BlockSpec(memory_space=pl.ANY)], (prev,), {6: 0}
    qrow = lambda a: pl.BlockSpec((group, tq, a.shape[2]), lambda s: (s, jq, 0))
    krow = lambda a: pl.BlockSpec((group, n_keys, a.shape[2]), lambda s: (s, 0, 0))
    width = DSA_HEADS * DSA_HEAD_DIM
    return pl.pallas_call(
        functools.partial(_dsa_kernel, group=group, tq=tq, n_keys=n_keys, row0=jq * tq, limit_const=limit_const,
                          n_sel=n_sel),
        grid=(n_streams // group,),
        in_specs=[qrow(aq), qrow(iq), qrow(ikw), krow(kk), krow(vv), krow(ikk)] + extra_specs,
        out_specs=pl.BlockSpec((group, tq, width), lambda s: (s, jq, 0)),
        out_shape=jax.ShapeDtypeStruct((n_streams, rows_per_stream, width), BF16),
        input_output_aliases=aliases,
        scratch_shapes=[pltpu.VMEM((group * tq, n_keys), F32), pltpu.VMEM((group * tq, n_keys), F32)],
        compiler_params=pltpu.CompilerParams(dimension_semantics=("parallel",), vmem_limit_bytes=56 * MIB),
        name=name,
    )(aq, iq, ikw, kk, vv, ikk, *extra_args)


def _dsa_cached_kernel(aq_ref, iq_ref, ikwq_ref, ck_ref, cv_ref, cik_ref, nk_ref, nv_ref, o_ref,
                       w_ref, bias_ref, kk_ref, vv_ref, ikk_ref, **static):
    past, new = ck_ref.shape[1], nk_ref.shape[1]
    for dst, cache, fresh in ((kk_ref, ck_ref, nk_ref), (vv_ref, cv_ref, nv_ref), (ikk_ref, cik_ref, ikwq_ref)):
        dst[:, :past, :cache.shape[2]] = cache[...]
        dst[:, past:past + new, :] = fresh[...]
        dst[:, past + new:, :] = jnp.zeros((dst.shape[0], dst.shape[1] - past - new, dst.shape[2]), F32)
    _dsa_kernel(aq_ref, iq_ref, ikwq_ref, kk_ref, vv_ref, ikk_ref, o_ref, w_ref, bias_ref, **static)


def _dsa_cached(aq, iq, ikw, cache_k, cache_v, cache_ik, new_k, new_v, tq):
    n_streams, past, _ = cache_k.shape
    n = aq.shape[0]
    per_stream = lambda a: a.reshape(n_streams, tq, a.shape[1])
    aq, iq, ikw, new_k, new_v = per_stream(aq), per_stream(iq), per_stream(ikw), per_stream(new_k), per_stream(new_v)
    limit = past + tq
    n_keys = -(-limit // LANES) * LANES
    group = max(1, min(n_streams, DSA_ROWS // tq, DSA_GROUP_MAX))
    blk = lambda a: pl.BlockSpec((group,) + a.shape[1:], lambda s: (s, 0, 0))
    width = DSA_HEADS * DSA_HEAD_DIM
    return pl.pallas_call(
        functools.partial(_dsa_cached_kernel, group=group, tq=tq, n_keys=n_keys, row0=0, limit_const=limit,
                          n_sel=min(DSA_TOPK, limit // 4)),
        grid=(n_streams // group,),
        in_specs=[blk(a) for a in (aq, iq, ikw, cache_k, cache_v, cache_ik, new_k, new_v)],
        out_specs=pl.BlockSpec((group, tq, width), lambda s: (s, 0, 0)),
        out_shape=jax.ShapeDtypeStruct((n_streams, tq, width), BF16),
        scratch_shapes=[pltpu.VMEM((group * tq, n_keys), F32), pltpu.VMEM((group * tq, n_keys), F32)] +
                       [pltpu.VMEM((group, n_keys, LANES), F32)] * 3,
        compiler_params=pltpu.CompilerParams(dimension_semantics=("parallel",), vmem_limit_bytes=48 * MIB),
        name="dsa_s",
    )(aq, iq, ikw, cache_k, cache_v, cache_ik, new_k, new_v).reshape(n, width)


def _dsa(aq, iq, ikw, kk, vv, ikk, tq, out):
    n_streams, n_keys, _ = kk.shape
    n = aq.shape[0]
    per_stream = lambda a: a.reshape(n_streams, n // n_streams, a.shape[1])
    aq, iq, ikw = per_stream(aq), per_stream(iq), per_stream(ikw)
    nq = n // n_streams // tq
    n_sel = min(DSA_TOPK, n_keys // 4)
    group = max(1, min(n_streams, DSA_ROWS // tq, DSA_GROUP_MAX))
    out = per_stream(out)
    for jq in range(nq):
        out = _dsa_call(aq, iq, ikw, kk, vv, ikk, group, tq, jq, (jq + 1) * tq, None, n_sel, f"dsa_p{jq}", prev=out)
    return out.reshape(n, -1)


def _pack_rows(v):
    q = QUARTER
    bits = lambda x: pltpu.bitcast(x.astype(BF16).astype(F32), I32)
    pair = lambda c: lax.shift_right_logical(bits(v[:, c * q:(c + 1) * q]), 16) | bits(v[:, (c + 1) * q:(c + 2) * q])
    return pair(0), pair(2)


def _unpack_rows(lo, hi):
    parts = []
    for w in (lo, hi):
        parts.append(pltpu.bitcast(lax.shift_left(w, 16), F32))
        parts.append(pltpu.bitcast(w & jnp.int32(-65536), F32))
    return jnp.concatenate(parts, axis=1)


def _layer_norm(v, g, b):
    mu = jnp.mean(v, axis=-1, keepdims=True)
    d = v - mu
    var = jnp.mean(d * d, axis=-1, keepdims=True)
    return d * lax.rsqrt(var + LN_EPS) * g + b


def _merge_kernel(x_ref, ret_ref, od_ref, gr_ref, ga_ref, wr_ref, wd_ref, wo_ref, g1_ref, b1_ref, rwt_ref, rb_ref,
                  tri_ref, cnt0_ref, h1_ref, hlo_ref, hhi_ref, gtm_ref, ek_ref, rk_ref, tot_ref, cnt_ref):
    @pl.when(pl.program_id(0) == 0)
    def _():
        cnt_ref[...] = cnt0_ref[:, 0:1]

    y_ret = _mm(ret_ref[...], wr_ref[...])
    y_dsa = _mm(od_ref[...], wd_ref[...])
    merged = jax.nn.sigmoid(gr_ref[...]) * y_ret + jax.nn.sigmoid(ga_ref[...]) * y_dsa
    mix = _mm(merged.astype(BF16), wo_ref[...])
    h1 = _layer_norm(DEEPNORM_ALPHA * x_ref[...] + mix, g1_ref[...], b1_ref[...])
    h1_ref[...] = h1
    hlo_ref[...], hhi_ref[...] = _pack_rows(h1)

    logits = lax.dot_general(rwt_ref[...], h1, (((1,), (1,)), ((), ())), preferred_element_type=F32,
                             precision=lax.Precision.HIGHEST) + rb_ref[...]
    tm = logits.shape[1]
    e_iota = lax.broadcasted_iota(I32, (N_EXPERTS, tm), 0)
    tops, hots, firsts = [], [], []
    for _ in range(MOE_TOP_K):
        m = jnp.max(logits, axis=0, keepdims=True)
        first = jnp.min(jnp.where(logits == m, e_iota, N_EXPERTS), axis=0, keepdims=True)
        hot = e_iota == first
        tops.append(m)
        hots.append(hot)
        firsts.append(first)
        logits = jnp.where(hot, -jnp.inf, logits)
    exps = [jnp.exp(m - tops[0]) for m in tops]
    den = exps[0] + exps[1] + exps[2] + exps[3]
    sel = jnp.zeros((N_EXPERTS, tm), F32)
    for hot in hots:
        sel = sel + jnp.where(hot, 1.0, 0.0)
    rank = _mm(sel.astype(BF16), tri_ref[...]) + cnt_ref[...]
    ranks = [jnp.sum(jnp.where(hot, rank, 0.0), axis=0, keepdims=True).astype(I32) for hot in hots]
    pad_i = jnp.zeros((8 - MOE_TOP_K, tm), I32)
    ek_ref[...] = jnp.concatenate(firsts + [pad_i], axis=0)
    rk_ref[...] = jnp.concatenate(ranks + [pad_i], axis=0)
    gates = jnp.concatenate([e / den for e in exps] + [jnp.zeros((LANES - MOE_TOP_K, tm), F32)], axis=0)
    gtm_ref[...] = gates.T
    cnt_ref[...] = cnt_ref[...] + jnp.sum(sel, axis=1, keepdims=True)
    tot_ref[...] = jnp.broadcast_to(cnt_ref[...], tot_ref.shape)


def _merge(x, ret, od, gr, ga, wr, wd, wo, g1, b1, rwt, rb, cnt0):
    n = x.shape[0]
    tm = MERGE_TM
    row = lambda w: pl.BlockSpec((tm, w), lambda i: (i, 0))
    const = lambda a: pl.BlockSpec(a.shape, lambda i: (0,) * a.ndim)
    col = pl.BlockSpec((8, tm), lambda i: (0, i))
    tri = jnp.asarray(np.triu(np.ones((tm, tm), np.float32), 1), BF16)
    return pl.pallas_call(
        _merge_kernel,
        grid=(n // tm,),
        in_specs=[row(D_MODEL), row(1024), row(512), row(1024), row(1024), const(wr), const(wd), const(wo),
                  const(g1), const(b1), const(rwt), const(rb), const(tri), const(cnt0)],
        out_specs=[row(D_MODEL), row(QUARTER), row(QUARTER), row(LANES), col, col,
                   pl.BlockSpec((N_EXPERTS, LANES), lambda i: (0, 0))],
        out_shape=[jax.ShapeDtypeStruct((n, D_MODEL), F32), jax.ShapeDtypeStruct((n, QUARTER), I32),
                   jax.ShapeDtypeStruct((n, QUARTER), I32), jax.ShapeDtypeStruct((n, LANES), F32),
                   jax.ShapeDtypeStruct((8, n), I32), jax.ShapeDtypeStruct((8, n), I32),
                   jax.ShapeDtypeStruct((N_EXPERTS, LANES), F32)],
        scratch_shapes=[pltpu.VMEM((N_EXPERTS, 1), F32)],
        compiler_params=pltpu.CompilerParams(dimension_semantics=("arbitrary",), vmem_limit_bytes=48 * MIB),
        name="merge",
    )(x, ret, od, gr, ga, wr, wd, wo, g1, b1, rwt, rb, tri, cnt0)


def _deinterleave_kernel(w_ref, o_ref):
    r = lax.broadcasted_iota(I32, (UP_BLOCK, UP_BLOCK), 0)
    c = lax.broadcasted_iota(I32, (UP_BLOCK, UP_BLOCK), 1)
    src = jnp.where(c < LANES, 2 * c, 2 * (c - LANES) + 1)
    perm = jnp.where(r == src, 1.0, 0.0).astype(BF16)
    for b in range(w_ref.shape[2] // UP_BLOCK):
        sl = slice(b * UP_BLOCK, (b + 1) * UP_BLOCK)
        o_ref[0, :, sl] = _mm(w_ref[0, :, sl].astype(BF16), perm).astype(BF16)


def _deinterleave_w_up(w_up):
    n_e, d_in, d_out = w_up.shape
    cols = d_out
    spec = pl.BlockSpec((1, d_in, cols), lambda e, c: (e, 0, c))
    return pl.pallas_call(
        _deinterleave_kernel,
        grid=(n_e, d_out // cols),
        in_specs=[spec],
        out_specs=spec,
        out_shape=jax.ShapeDtypeStruct(w_up.shape, BF16),
        compiler_params=pltpu.CompilerParams(dimension_semantics=("parallel", "parallel"),
                                             vmem_limit_bytes=40 * MIB),
        name="w_up_prep",
    )(w_up)


def _pos_kernel(off_ref, ek_ref, rk_ref, pos_ref):
    ek = ek_ref[...]
    pos = rk_ref[...]
    for e in range(N_EXPERTS):
        pos = pos + jnp.where(ek == e, off_ref[e], 0)
    pos_ref[...] = pos


def _positions(off, ek, rk):
    n = ek.shape[1]
    tn = min(n, 2048)
    spec = pl.BlockSpec((8, tn), lambda i, off: (0, i))
    return pl.pallas_call(
        _pos_kernel,
        grid_spec=pltpu.PrefetchScalarGridSpec(num_scalar_prefetch=1, grid=(n // tn,), in_specs=[spec, spec],
                                               out_specs=spec),
        out_shape=jax.ShapeDtypeStruct((8, n), I32),
        name="moe_pos",
    )(off, ek, rk)


def _sc_mesh():
    return plsc.VectorSubcoreMesh(core_axis_name="core", subcore_axis_name="subcore")


def _sc_scatter(xs, pos_flat, n_rows):
    width, dtype = xs[0].shape[1], xs[0].dtype
    windows = [x.shape[0] // SC_WINDOW for x in xs]
    nw = sum(windows)

    @functools.partial(pl.kernel, out_type=jax.ShapeDtypeStruct((n_rows, width), dtype), mesh=_sc_mesh())
    def scatter(*refs):
        p_hbm, o_hbm = refs[len(xs)], refs[len(xs) + 1]

        def body(x_vmem, *p_vmem):
            for p in p_vmem:
                pltpu.sync_copy(x_vmem, o_hbm.at[p.at[0]])

        first = 0
        for x_hbm, nw_g in zip(refs[:len(xs)], windows):
            pltpu.emit_pipeline(
                body, grid=(nw_g,),
                in_specs=[pl.BlockSpec((SC_WINDOW, width), lambda i: (i, 0))] +
                         [pl.BlockSpec((1, SC_WINDOW), functools.partial(lambda w0, i: (0, w0 + i), k * nw + first))
                          for k in range(MOE_TOP_K)],
                out_specs=[], core_axis_name=("core", "subcore"), dimension_semantics=(pltpu.PARALLEL,),
            )(x_hbm, *([p_hbm] * MOE_TOP_K))
            first += nw_g

    return scatter(*xs, pos_flat)


def _sc_gather(y, pos_flat):
    m = pos_flat.shape[1]

    @functools.partial(pl.kernel, out_type=jax.ShapeDtypeStruct((m, y.shape[1]), y.dtype), mesh=_sc_mesh())
    def gather(y_hbm, p_hbm, o_hbm):
        def body(p_vmem, o_vmem):
            pltpu.sync_copy(y_hbm.at[p_vmem.at[0]], o_vmem)

        pltpu.emit_pipeline(
            body, grid=(m // SC_WINDOW,),
            in_specs=[pl.BlockSpec((1, SC_WINDOW), lambda i: (0, i))],
            out_specs=[pl.BlockSpec((SC_WINDOW, y.shape[1]), lambda i: (i, 0))],
            core_axis_name=("core", "subcore"), dimension_semantics=(pltpu.PARALLEL,),
        )(p_hbm, o_hbm)

    return gather(y, pos_flat)


def _ffn_kernel(be_ref, nu_ref, xlo_ref, xhi_ref, wup_ref, bup_ref, wdn_ref, bdn_ref, ylo_ref, yhi_ref):
    @pl.when(pl.program_id(0) < nu_ref[0])
    def _():
        x = _unpack_rows(xlo_ref[...], xhi_ref[...]).astype(BF16)
        h = _mm(x, wup_ref[0]) + bup_ref[0]
        acts = []
        for b in range(2 * D_FF // UP_BLOCK):
            glu = jnp.minimum(h[:, b * UP_BLOCK:b * UP_BLOCK + LANES], SWIGLU_LIMIT)
            lin = jnp.clip(h[:, b * UP_BLOCK + LANES:(b + 1) * UP_BLOCK], -SWIGLU_LIMIT, SWIGLU_LIMIT)
            acts.append(glu * jax.nn.sigmoid(SWIGLU_ALPHA * glu) * (lin + 1.0))
        act = jnp.concatenate(acts, axis=1)
        ylo_ref[...], yhi_ref[...] = _pack_rows(_mm(act.astype(BF16), wdn_ref[0]) + bdn_ref[0])


def _ffn(block_expert, n_used, xs_lo, xs_hi, wup, bup, wdn, bdn):
    rows = xs_lo.shape[0]
    blk = FFN_BLOCK
    row = pl.BlockSpec((blk, QUARTER), lambda i, be, nu: (i, 0))
    per_expert = lambda a: pl.BlockSpec((1,) + a.shape[1:], lambda i, be, nu: (be[i], 0, 0))
    return pl.pallas_call(
        _ffn_kernel,
        grid_spec=pltpu.PrefetchScalarGridSpec(
            num_scalar_prefetch=2, grid=(rows // blk,),
            in_specs=[row, row, per_expert(wup), per_expert(bup), per_expert(wdn), per_expert(bdn)],
            out_specs=[row, row]),
        out_shape=[jax.ShapeDtypeStruct((rows, QUARTER), I32), jax.ShapeDtypeStruct((rows, QUARTER), I32)],
        compiler_params=pltpu.CompilerParams(dimension_semantics=("arbitrary",), vmem_limit_bytes=40 * MIB),
        name="moe_ffn",
    )(block_expert, n_used, xs_lo, xs_hi, wup, bup, wdn, bdn)


def _combine_kernel(h1_ref, olo_ref, ohi_ref, gtm_ref, g2_ref, b2_ref, o_ref):
    g = gtm_ref[...]
    y = jnp.zeros(h1_ref.shape, F32)
    for k in range(MOE_TOP_K):
        y = y + g[:, k:k + 1] * _unpack_rows(olo_ref[k], ohi_ref[k])
    o_ref[...] = _layer_norm(DEEPNORM_ALPHA * h1_ref[...] + y, g2_ref[...], b2_ref[...])


def _combine(h1, og_lo, og_hi, gtm, g2, b2, row0):
    n = h1.shape[0]
    n_all = og_lo.shape[0] // MOE_TOP_K
    tm = MERGE_TM
    row = lambda w: pl.BlockSpec((tm, w), lambda i: (i, 0))
    const = lambda a: pl.BlockSpec(a.shape, lambda i: (0,) * a.ndim)
    picked = pl.BlockSpec((MOE_TOP_K, tm, QUARTER), lambda i: (0, i + row0 // tm, 0))
    return pl.pallas_call(
        _combine_kernel,
        grid=(n // tm,),
        in_specs=[row(D_MODEL), picked, picked, row(LANES), const(g2), const(b2)],
        out_specs=row(D_MODEL),
        out_shape=jax.ShapeDtypeStruct((n, D_MODEL), F32),
        compiler_params=pltpu.CompilerParams(dimension_semantics=("parallel",), vmem_limit_bytes=40 * MIB),
        name="moe_combine",
    )(h1, og_lo.reshape(MOE_TOP_K, n_all, QUARTER), og_hi.reshape(MOE_TOP_K, n_all, QUARTER), gtm, g2, b2)


def _moe(groups, totals, wup, bup, wdn, bdn, g2, b2):
    blk = FFN_BLOCK
    sizes = [g[0].shape[0] for g in groups]
    n_all = sum(sizes)
    n_rows = MOE_TOP_K * n_all + N_EXPERTS * blk
    counts = totals[:, 0].astype(I32)
    padded = (counts + blk - 1) // blk * blk
    ends = jnp.cumsum(padded)
    block_start = jnp.arange(n_rows // blk, dtype=I32) * blk
    block_expert = jnp.minimum(jnp.sum((ends[None, :] <= block_start[:, None]).astype(I32), axis=1), N_EXPERTS - 1)
    n_used = (ends[-1:] // blk).astype(I32)
    cat = lambda i, axis: jnp.concatenate([g[i] for g in groups], axis=axis)
    pos = _positions((ends - padded).astype(I32), cat(4, 1), cat(5, 1))
    pos_flat = pos[:MOE_TOP_K].reshape(1, MOE_TOP_K * n_all)
    xs_lo = _sc_scatter([g[1] for g in groups], pos_flat, n_rows)
    xs_hi = _sc_scatter([g[2] for g in groups], pos_flat, n_rows)
    ys_lo, ys_hi = _ffn(block_expert, n_used, xs_lo, xs_hi, wup, bup, wdn, bdn)
    og_lo, og_hi = _sc_gather(ys_lo, pos_flat), _sc_gather(ys_hi, pos_flat)
    starts = np.cumsum([0] + sizes[:-1]).tolist()
    return [_combine(g[0], og_lo, og_hi, g[3], g2, b2, r0) for g, r0 in zip(groups, starts)]


def _mixer(x, pos_tab, tab_period, s0, rows_per_stream, ret_rows, caches, weights, cnt0):
    wp, wr, wd, wo, g1, b1, rwt, rb = weights
    n = x.shape[0]
    n_streams = n // rows_per_stream
    rq, rk, rv, rg, aq, ak, av, iq, ikw, gr, ga, od0 = _project(x, wp, pos_tab, tab_period)
    ret, s_new = _retention(rq, rk, rv, rg, s0, rows_per_stream, ret_rows)
    if caches is None:
        per_stream = lambda a: a.reshape(n_streams, rows_per_stream, a.shape[1])
        od = _dsa(aq, iq, ikw, per_stream(ak), per_stream(av), per_stream(ikw), DSA_TQ, od0)
    else:
        od = _dsa_cached(aq, iq, ikw, *caches, ak, av, rows_per_stream)
    h1, h_lo, h_hi, gtm, ek, rk_, totals = _merge(x, ret, od, gr, ga, wr, wd, wo, g1, b1, rwt, rb, cnt0)
    return (h1, h_lo, h_hi, gtm, ek, rk_), totals, (s_new, ak, av, ikw[:, :IDX_DIM])


def kernel(x_prompt, x_sample, state_ret, cache_k, cache_v, cache_idx_k, w_in, w_ret_o, w_dsa_o, w_o,
           ln1_g, ln1_b, router_w, router_b, w_up, b_up, w_down, b_down, ln2_g, ln2_b):
    assert w_in.shape[0] == DEPTH
    batch, seq, _ = x_prompt.shape
    dec_batch, dec_seq, _ = x_sample.shape
    past = cache_k.shape[2]
    assert seq % PROJ_TM == 0 and seq % DSA_TQ == 0 and PROJ_TM % dec_seq == 0

    l = 0
    mixer_w = (_pack_w_in(w_in[l]), w_ret_o[l].astype(BF16), w_dsa_o[l].astype(BF16), w_o[l].astype(BF16),
               ln1_g[l][None, :], ln1_b[l][None, :], router_w[l].T, router_b[l][:, None])
    moe_w = (
        _deinterleave_w_up(w_up[l]),
        b_up[l].reshape(N_EXPERTS, 2 * D_FF // UP_BLOCK, LANES, 2).transpose(0, 1, 3, 2).reshape(N_EXPERTS, 1, 2 * D_FF),
        w_down[l].astype(BF16), b_down[l][:, None, :], ln2_g[l][None, :], ln2_b[l][None, :])

    tab_p = _rot_tables(jnp.arange(seq))
    zeros_state = jnp.zeros((batch, RET_HEADS, RET_DK, RET_DV), F32)
    moe_p, totals_p, (s_p, k_p, v_p, ik_p) = _mixer(
        x_prompt.reshape(batch * seq, D_MODEL), tab_p, seq // PROJ_TM, zeros_state, seq, RET_CHUNK, None, mixer_w,
        jnp.zeros((N_EXPERTS, LANES), F32))

    tab_s = jnp.tile(_rot_tables(past + jnp.arange(dec_seq)), (PROJ_TM // dec_seq, 1))
    caches = (cache_k[l].reshape(dec_batch, past, LANES), cache_v[l].reshape(dec_batch, past, LANES), cache_idx_k[l])
    moe_s, totals, (s_s, k_s, v_s, ik_s) = _mixer(
        x_sample.reshape(dec_batch * dec_seq, D_MODEL), tab_s, 1, state_ret[l], dec_seq, dec_seq, caches, mixer_w,
        totals_p)

    y_p, y_s = _moe([moe_p, moe_s], totals, *moe_w)

    kv = (DSA_KV_HEADS, DSA_HEAD_DIM)
    return (y_p.reshape(batch, seq, D_MODEL), y_s.reshape(dec_batch, dec_seq, D_MODEL),
            s_p[None], k_p.reshape(1, batch, seq, *kv), v_p.reshape(1, batch, seq, *kv),
            ik_p.reshape(1, batch, seq, IDX_DIM),
            s_s[None], k_s.reshape(1, dec_batch, dec_seq, *kv), v_s.reshape(1, dec_batch, dec_seq, *kv),
            ik_s.reshape(1, dec_batch, dec_seq, IDX_DIM))
```

```python
import functools

import numpy as np
import jax
import jax.numpy as jnp
from jax import lax
from jax.experimental import pallas as pl
from jax.experimental.pallas import tpu as pltpu
from jax.experimental.pallas import tpu_sc as plsc

F32 = jnp.float32
BF16 = jnp.bfloat16
I32 = jnp.int32

D_MODEL = 1024
CHUNK = 64
RET_HEADS = 4
RET_DK = 128
RET_DV = 256
RET_ROPE_BASE = 10000.0
DSA_HEADS = 8
DSA_KV_HEADS = 2
DSA_HEAD_DIM = 64
IDX_HEADS = 8
IDX_DIM = 64
DSA_TOPK = 256
ROPE_THETA = 500000.0
N_EXPERTS = 32
MOE_TOP_K = 4
D_FF = 1024
SWIGLU_ALPHA = 1.702
SWIGLU_LIMIT = 7.0
LN_EPS = 1e-5
GN_EPS = 1e-6
DEPTH = 1
DEEPNORM_ALPHA = (2.0 * DEPTH) ** 0.25
PROJ_WIDTHS = (RET_HEADS * RET_DK, RET_HEADS * RET_DK, RET_HEADS * RET_DV, RET_HEADS * RET_DV,
               DSA_HEADS * DSA_HEAD_DIM, DSA_KV_HEADS * DSA_HEAD_DIM, DSA_KV_HEADS * DSA_HEAD_DIM,
               IDX_HEADS * IDX_DIM, IDX_DIM, IDX_HEADS, D_MODEL, D_MODEL)

assert CHUNK & (CHUNK - 1) == 0

LANES = 128
MIB = 1024 * 1024

OFF_RQ, OFF_RK, OFF_RV, OFF_RG = 0, 512, 1024, 2048
OFF_AQ, OFF_AK, OFF_AV, OFF_IQ, OFF_IKW = 3072, 3584, 3712, 3840, 4352
OFF_GR, OFF_GA, PACKED_COLS = 4480, 5504, 6528
TAB_COLS = 8 * LANES

PROJ_TM = 512
RET_CHUNK = 128
RET_CHUNKS_PER_STEP = 8
DSA_TQ = 256
DSA_ROWS = 512
DSA_GROUP_MAX = 4
SEARCH_UNROLL = 4
DSA_SHORT_EXTENT = 1280
DSA_CHAIN_CELLS = 640 * 1024
MERGE_TM = 512
FFN_BLOCK = 512
SC_WINDOW = 128
QUARTER = D_MODEL // 4
UP_BLOCK = 2 * LANES


def _nt(a, b):
    return lax.dot_general(a, b, (((1,), (1,)), ((), ())), preferred_element_type=F32)


def _mm(a, b):
    return jnp.dot(a, b, preferred_element_type=F32)


def _proj_kernel(x_ref, w_ref, tab_ref, rq_ref, rk_ref, rv_ref, rg_ref, aq_ref, ak_ref, av_ref,
                 iq_ref, ikw_ref, gr_ref, ga_ref, od0_ref):
    xb = x_ref[...].astype(BF16)
    od0_ref[...] = jnp.zeros(od0_ref.shape, od0_ref.dtype)

    def mm(c0, n):
        return _mm(xb, w_ref[:, c0:c0 + n])

    def tab(i):
        return tab_ref[:, i * LANES:(i + 1) * LANES]

    def rot_full(z):
        return z * tab(0) + pltpu.roll(z, 64, 1) * tab(1)

    def rot_part(z, c):
        return z * tab(c) + pltpu.roll(z, LANES - 8, 1) * tab(c + 1) + pltpu.roll(z, 8, 1) * tab(c + 2)

    z = mm(OFF_RQ, 512)
    for h in range(4):
        sl = slice(h * LANES, (h + 1) * LANES)
        rq_ref[:, sl] = rot_full(z[:, sl]).astype(BF16)
    z = mm(OFF_RK, 512)
    for h in range(4):
        sl = slice(h * LANES, (h + 1) * LANES)
        rk_ref[:, sl] = (rot_full(z[:, sl]) * (RET_DK ** -0.5)).astype(BF16)
    for c in range(2):
        rv_ref[:, c * 512:(c + 1) * 512] = mm(OFF_RV + c * 512, 512).astype(BF16)
    for c in range(2):
        rg_ref[:, c * 512:(c + 1) * 512] = mm(OFF_RG + c * 512, 512)
    z = mm(OFF_AQ, 512)
    for h in range(4):
        sl = slice(h * LANES, (h + 1) * LANES)
        aq_ref[:, sl] = rot_part(z[:, sl], 2).astype(BF16)
    z = mm(OFF_AK, 256)
    ak_ref[...] = rot_part(z[:, :LANES], 2)
    av_ref[...] = z[:, LANES:]
    z = mm(OFF_IQ, 512)
    for h in range(4):
        sl = slice(h * LANES, (h + 1) * LANES)
        iq_ref[:, sl] = rot_part(z[:, sl], 2).astype(BF16)
    ikw_ref[...] = rot_part(mm(OFF_IKW, LANES), 5)
    for c in range(2):
        gr_ref[:, c * 512:(c + 1) * 512] = mm(OFF_GR + c * 512, 512)
    for c in range(2):
        ga_ref[:, c * 512:(c + 1) * 512] = mm(OFF_GA + c * 512, 512)


def _rot_tables(pos):
    p = pos.shape[0]
    posf = pos.astype(F32)[:, None]
    ret_f = RET_ROPE_BASE ** (-jnp.linspace(0.0, 1.0, RET_DK // 2, dtype=F32))
    ang = posf * ret_f[None, :]
    c, s = jnp.cos(ang), jnp.sin(ang)
    cos_r = jnp.concatenate([c, c], 1)
    sin_r = jnp.concatenate([-s, s], 1)
    n_rot = DSA_HEAD_DIM // 4
    att_f = ROPE_THETA ** (-jnp.arange(0, n_rot, 2, dtype=F32) / n_rot)
    ang2 = posf * att_f[None, :]
    c2, s2 = jnp.cos(ang2), jnp.sin(ang2)
    half = n_rot // 2
    rest = DSA_HEAD_DIM - 2 * half
    c64 = jnp.concatenate([c2, c2, jnp.ones((p, rest), F32)], 1)
    s1_64 = jnp.concatenate([-s2, jnp.zeros((p, DSA_HEAD_DIM - half), F32)], 1)
    s2_64 = jnp.concatenate([jnp.zeros((p, half), F32), s2, jnp.zeros((p, rest), F32)], 1)
    z64 = jnp.zeros((p, DSA_HEAD_DIM), F32)
    ci = jnp.concatenate([c64, jnp.full((p, IDX_HEADS), IDX_HEADS ** -0.5, F32),
                          jnp.zeros((p, DSA_HEAD_DIM - IDX_HEADS), F32)], 1)
    return jnp.concatenate([cos_r, sin_r,
                            jnp.concatenate([c64, c64], 1), jnp.concatenate([s1_64, s1_64], 1),
                            jnp.concatenate([s2_64, s2_64], 1),
                            ci, jnp.concatenate([s1_64, z64], 1), jnp.concatenate([s2_64, z64], 1)], 1)


def _pack_w_in(w):
    cuts = np.cumsum(PROJ_WIDTHS)[:-1].tolist()
    rq, rk, rv, rg, aq, ak, av, iq, ik, iw, gr, ga = jnp.split(w, cuts, axis=1)
    pad = jnp.zeros((w.shape[0], LANES - IDX_DIM - IDX_HEADS), w.dtype)
    return jnp.concatenate([rq, rk, rv, rg, aq, ak, av, iq, ik, iw, pad, gr, ga], axis=1).astype(BF16)


def _project(x, wp, tab, tab_period):
    n = x.shape[0]
    tm = PROJ_TM
    row = lambda w: pl.BlockSpec((tm, w), lambda i: (i, 0))
    out_shapes = [((n, 512), BF16), ((n, 512), BF16), ((n, 1024), BF16), ((n, 1024), F32),
                  ((n, 512), BF16), ((n, LANES), F32), ((n, LANES), F32), ((n, 512), BF16),
                  ((n, LANES), F32), ((n, 1024), F32), ((n, 1024), F32), ((n, DSA_HEADS * DSA_HEAD_DIM), BF16)]
    return pl.pallas_call(
        _proj_kernel,
        grid=(n // tm,),
        in_specs=[row(D_MODEL),
                  pl.BlockSpec((D_MODEL, PACKED_COLS), lambda i: (0, 0), pipeline_mode=pl.Buffered(1)),
                  pl.BlockSpec((tm, TAB_COLS), lambda i: (i % tab_period, 0))],
        out_specs=[row(s[1]) for s, _ in out_shapes],
        out_shape=[jax.ShapeDtypeStruct(s, d) for s, d in out_shapes],
        compiler_params=pltpu.CompilerParams(dimension_semantics=("parallel",), vmem_limit_bytes=52 * MIB),
        name="proj",
    )(x, wp, tab)


def _ret_kernel(dec_ref, xi_ref, zeta_ref, rq_ref, rk_ref, rv_ref, rg_ref, s0_ref, ret_ref, sout_ref, st_ref,
                *, rows, n_chunk, g_pow):
    j = pl.program_id(1)
    cpad = RET_CHUNK

    @pl.when(j == 0)
    def _():
        st_ref[...] = s0_ref[0]

    def padded(v):
        if rows == cpad:
            return v
        return jnp.concatenate([v, jnp.zeros((cpad - rows, v.shape[1]), v.dtype)], axis=0)

    for c in range(n_chunk):
        rs = slice(c * rows, (c + 1) * rows)
        for h in range(RET_HEADS):
            ks = slice(h * RET_DK, (h + 1) * RET_DK)
            vs = slice(h * RET_DV, (h + 1) * RET_DV)
            q = padded(rq_ref[rs, ks])
            kt = padded(rk_ref[rs, ks].astype(F32)).T
            v = padded(rv_ref[rs, vs])
            s = st_ref[h]
            sc = _mm(q, kt.astype(BF16)) * dec_ref[h]
            o = _mm(sc.astype(BF16), v) + _mm(q, s.astype(BF16)) * xi_ref[h]
            st_ref[h] = g_pow[h] * s + _mm((kt * zeta_ref[h]).astype(BF16), v)
            o = o[:rows]
            mu = jnp.mean(o, axis=-1, keepdims=True)
            d = o - mu
            var = jnp.mean(d * d, axis=-1, keepdims=True)
            gn = d * lax.rsqrt(var + GN_EPS)
            g = rg_ref[rs, vs]
            ret_ref[rs, vs] = (gn * (g * jax.nn.sigmoid(g))).astype(BF16)

    @pl.when(j == pl.num_programs(1) - 1)
    def _():
        sout_ref[0] = st_ref[...]


def _retention(rq, rk, rv, rg, s0, rows_per_stream, rows):
    n = rq.shape[0]
    n_streams = n // rows_per_stream
    n_chunk = min(rows_per_stream // rows, RET_CHUNKS_PER_STEP)
    blk = rows * n_chunk
    nb = rows_per_stream // blk
    gam = 1.0 - 2.0 ** (-5.0 - np.arange(RET_HEADS, dtype=np.float64))
    i = np.arange(RET_CHUNK, dtype=np.float64)
    diff = i[:, None] - i[None, :]
    dec = np.where(diff >= 0, gam[:, None, None] ** np.maximum(diff, 0.0)[None], 0.0)
    xi = gam[:, None, None] ** (i + 1.0)[None, :, None]
    zeta = np.where(i < rows, gam[:, None, None] ** (rows - 1.0 - i)[None, None, :], 0.0)
    g_pow = tuple(float(g ** rows) for g in gam)
    const = lambda shape: pl.BlockSpec(shape, lambda s, j: (0,) * len(shape))
    row = lambda w: pl.BlockSpec((blk, w), lambda s, j: (s * nb + j, 0))
    st = pl.BlockSpec((1, RET_HEADS, RET_DK, RET_DV), lambda s, j: (s, 0, 0, 0))
    return pl.pallas_call(
        functools.partial(_ret_kernel, rows=rows, n_chunk=n_chunk, g_pow=g_pow),
        grid=(n_streams, nb),
        in_specs=[const((RET_HEADS, RET_CHUNK, RET_CHUNK)), const((RET_HEADS, RET_CHUNK, 1)),
                  const((RET_HEADS, 1, RET_CHUNK)), row(512), row(512), row(1024), row(1024), st],
        out_specs=[row(1024), st],
        out_shape=[jax.ShapeDtypeStruct((n, RET_HEADS * RET_DV), BF16),
                   jax.ShapeDtypeStruct((n_streams, RET_HEADS, RET_DK, RET_DV), F32)],
        scratch_shapes=[pltpu.VMEM((RET_HEADS, RET_DK, RET_DV), F32)],
        compiler_params=pltpu.CompilerParams(dimension_semantics=("parallel", "arbitrary"),
                                             vmem_limit_bytes=32 * MIB),
        name="retention",
    )(jnp.asarray(dec, F32), jnp.asarray(xi, F32), jnp.asarray(zeta, F32), rq, rk, rv, rg, s0)


def _dsa_kernel(aq_ref, iq_ref, ikwq_ref, kk_ref, vv_ref, ikk_ref, *rest,
                group, tq, n_keys, row0, limit_const, n_sel):
    o_ref, w_ref, bias_ref = rest[-3:]
    rows = group * tq
    stack = max(c for c in (1, 2, 4) if c == 1 or c * tq * n_keys <= DSA_CHAIN_CELLS)
    search_unroll = 2 * SEARCH_UNROLL if n_keys <= DSA_SHORT_EXTENT else SEARCH_UNROLL
    nsel_f = float(n_sel)
    neg_inf = -jnp.inf

    col = lax.broadcasted_iota(I32, (tq, n_keys), 1)
    if limit_const is None:
        chunk_of_row = jnp.right_shift(lax.broadcasted_iota(I32, (tq, 1), 0) + row0, CHUNK.bit_length() - 1)
        limit = (chunk_of_row + 1) * CHUNK
    else:
        limit = limit_const

    def attend_stream(s, carry):
        bias = bias_ref[pl.ds(pl.multiple_of(s * tq, tq), tq), :]
        kfull = kk_ref[s]
        vfull = vv_ref[s]
        bias_st = jnp.concatenate([bias] * stack, axis=0) if stack > 1 else bias
        for g in range(DSA_KV_HEADS):
            gs = slice(g * DSA_HEAD_DIM, (g + 1) * DSA_HEAD_DIM)
            kg = kfull[:, gs].astype(BF16)
            vg = vfull[:, gs].astype(BF16)
            qs = []
            for p in range(2 * g, 2 * g + 2):
                slab = (aq_ref[s, :, p * LANES:(p + 1) * LANES].astype(F32) * (DSA_HEAD_DIM ** -0.5)).astype(BF16)
                qs += [slab[:, :DSA_HEAD_DIM], slab[:, DSA_HEAD_DIM:]]
            outs = []
            for c in range(len(qs) // stack):
                qh = jnp.concatenate(qs[c * stack:(c + 1) * stack], axis=0) if stack > 1 else qs[c]
                lg = _nt(qh, kg) + bias_st
                m = jnp.max(lg, axis=1, keepdims=True)
                pr = jnp.exp(lg - m)
                den = jnp.sum(pr, axis=1, keepdims=True)
                out = _mm(pr.astype(BF16), vg) / den
                outs += [out[i * tq:(i + 1) * tq] for i in range(stack)]
            for pp in range(2):
                p = 2 * g + pp
                o_ref[s, :, p * LANES:(p + 1) * LANES] = jnp.concatenate(outs[2 * pp:2 * pp + 2], axis=1).astype(BF16)
        return carry

    if n_keys <= n_sel:
        for s in range(group):
            bias_ref[s * tq:(s + 1) * tq, :] = jnp.where(col < limit, 0.0, neg_inf)
        lax.fori_loop(0, group, attend_stream, 0)
        return

    def score_stream(s, carry):
        ikb = ikk_ref[s][:, :IDX_DIM].astype(BF16)
        iww = ikwq_ref[s][:, IDX_DIM:IDX_DIM + IDX_HEADS] * (IDX_DIM ** -0.5)
        acc = jnp.zeros((tq, n_keys), F32)
        for p in range(IDX_HEADS // 2):
            slab = iq_ref[s, :, p * LANES:(p + 1) * LANES]
            for hh in range(2):
                h = 2 * p + hh
                acc = acc + jnp.maximum(_nt(slab[:, hh * IDX_DIM:(hh + 1) * IDX_DIM], ikb), 0.0) * iww[:, h:h + 1]
        w_ref[pl.ds(pl.multiple_of(s * tq, tq), tq), :] = jnp.where(col < limit, acc, neg_inf)
        return carry

    lax.fori_loop(0, group, score_stream, 0)

    sc = w_ref[...]
    pos = jnp.sum(jnp.where(sc >= 0.0, 1.0, 0.0), axis=1, keepdims=True) >= nsel_f
    kk = jnp.where(pos, nsel_f, float(n_keys - n_sel + 1))
    w_ref[...] = jnp.where(pos, sc, -sc)

    def bit_step(i, u):
        cand_u = u | jnp.left_shift(jnp.int32(1), 30 - i)
        cand = pltpu.bitcast(cand_u, F32)
        cnt = jnp.sum(jnp.where(w_ref[...] >= cand, 1.0, 0.0), axis=1, keepdims=True)
        return jnp.where(cnt >= kk, cand_u, u)

    mag_u = lax.fori_loop(0, 31, bit_step, jnp.zeros((rows, 1), I32), unroll=search_unroll)
    mag = pltpu.bitcast(mag_u, F32)
    thr = jnp.where(pos, mag, -mag)

    sc = jnp.where(pos, w_ref[...], -w_ref[...])
    short = jnp.sum(jnp.where(sc >= thr, 1.0, 0.0), axis=1, keepdims=True) < nsel_f
    thr = jnp.where(jnp.logical_and(short, jnp.logical_not(pos)), -pltpu.bitcast(mag_u + 1, F32), thr)
    ge = sc >= thr
    cnt_gt = jnp.sum(jnp.where(sc > thr, 1.0, 0.0), axis=1, keepdims=True)
    cnt_ge = jnp.sum(jnp.where(ge, 1.0, 0.0), axis=1, keepdims=True)
    bias_ref[...] = jnp.where(jnp.logical_and(ge, sc > neg_inf), 0.0, neg_inf)
    excess = jnp.logical_and(cnt_ge > nsel_f, thr > neg_inf)

    @pl.when(jnp.max(jnp.where(excess, 1.0, 0.0)) > 0.0)
    def _():
        need = nsel_f - cnt_gt
        tri = jnp.where(lax.broadcasted_iota(I32, (LANES, LANES), 0) < lax.broadcasted_iota(I32, (LANES, LANES), 1),
                        1.0, 0.0).astype(BF16)
        before = jnp.zeros((rows, 1), F32)
        for b in range(n_keys // LANES):
            sl = slice(b * LANES, (b + 1) * LANES)
            sblk = jnp.where(pos, w_ref[:, sl], -w_ref[:, sl])
            eq = jnp.where(sblk == thr, 1.0, 0.0)
            rank = _mm(eq.astype(BF16), tri) + before
            keep = jnp.logical_or(sblk > thr, jnp.logical_and(sblk == thr, rank < need))
            bias_ref[:, sl] = jnp.where(jnp.logical_and(keep, sblk > neg_inf), 0.0, neg_inf)
            before = before + jnp.sum(eq, axis=1, keepdims=True)

    lax.fori_loop(0, group, attend_stream, 0)


def _dsa_call(aq, iq, ikw, kk, vv, ikk, group, tq, jq, n_keys, limit_const, n_sel, name, prev=None):
    n_streams, rows_per_stream = aq.shape[0], aq.shape[1]
    extra_specs, extra_args, aliases = [], (), {}
    if prev is not None:
        extra_specs, extra_args, aliases = [pl.BlockSpec(memory_space=pl.ANY)], (prev,), {6: 0}
    qrow = lambda a: pl.BlockSpec((group, tq, a.shape[2]), lambda s: (s, jq, 0))
    krow = lambda a: pl.BlockSpec((group, n_keys, a.shape[2]), lambda s: (s, 0, 0))
    width = DSA_HEADS * DSA_HEAD_DIM
    return pl.pallas_call(
        functools.partial(_dsa_kernel, group=group, tq=tq, n_keys=n_keys, row0=jq * tq, limit_const=limit_const,
                          n_sel=n_sel),
        grid=(n_streams // group,),
        in_specs=[qrow(aq), qrow(iq), qrow(ikw), krow(kk), krow(vv), krow(ikk)] + extra_specs,
        out_specs=pl.BlockSpec((group, tq, width), lambda s: (s, jq, 0)),
        out_shape=jax.ShapeDtypeStruct((n_streams, rows_per_stream, width), BF16),
        input_output_aliases=aliases,
        scratch_shapes=[pltpu.VMEM((group * tq, n_keys), F32), pltpu.VMEM((group * tq, n_keys), F32)],
        compiler_params=pltpu.CompilerParams(dimension_semantics=("parallel",), vmem_limit_bytes=56 * MIB),
        name=name,
    )(aq, iq, ikw, kk, vv, ikk, *extra_args)


def _dsa_cached_kernel(aq_ref, iq_ref, ikwq_ref, ck_ref, cv_ref, cik_ref, nk_ref, nv_ref, o_ref,
                       w_ref, bias_ref, kk_ref, vv_ref, ikk_ref, **static):
    past, new = ck_ref.shape[1], nk_ref.shape[1]
    for dst, cache, fresh in ((kk_ref, ck_ref, nk_ref), (vv_ref, cv_ref, nv_ref), (ikk_ref, cik_ref, ikwq_ref)):
        dst[:, :past, :cache.shape[2]] = cache[...]
        dst[:, past:past + new, :] = fresh[...]
        dst[:, past + new:, :] = jnp.zeros((dst.shape[0], dst.shape[1] - past - new, dst.shape[2]), F32)
    _dsa_kernel(aq_ref, iq_ref, ikwq_ref, kk_ref, vv_ref, ikk_ref, o_ref, w_ref, bias_ref, **static)


def _dsa_cached(aq, iq, ikw, cache_k, cache_v, cache_ik, new_k, new_v, tq):
    n_streams, past, _ = cache_k.shape
    n = aq.shape[0]
    per_stream = lambda a: a.reshape(n_streams, tq, a.shape[1])
    aq, iq, ikw, new_k, new_v = per_stream(aq), per_stream(iq), per_stream(ikw), per_stream(new_k), per_stream(new_v)
    limit = past + tq
    n_keys = -(-limit // LANES) * LANES
    group = max(1, min(n_streams, DSA_ROWS // tq, DSA_GROUP_MAX))
    blk = lambda a: pl.BlockSpec((group,) + a.shape[1:], lambda s: (s, 0, 0))
    width = DSA_HEADS * DSA_HEAD_DIM
    return pl.pallas_call(
        functools.partial(_dsa_cached_kernel, group=group, tq=tq, n_keys=n_keys, row0=0, limit_const=limit,
                          n_sel=min(DSA_TOPK, limit // 4)),
        grid=(n_streams // group,),
        in_specs=[blk(a) for a in (aq, iq, ikw, cache_k, cache_v, cache_ik, new_k, new_v)],
        out_specs=pl.BlockSpec((group, tq, width), lambda s: (s, 0, 0)),
        out_shape=jax.ShapeDtypeStruct((n_streams, tq, width), BF16),
        scratch_shapes=[pltpu.VMEM((group * tq, n_keys), F32), pltpu.VMEM((group * tq, n_keys), F32)] +
                       [pltpu.VMEM((group, n_keys, LANES), F32)] * 3,
        compiler_params=pltpu.CompilerParams(dimension_semantics=("parallel",), vmem_limit_bytes=48 * MIB),
        name="dsa_s",
    )(aq, iq, ikw, cache_k, cache_v, cache_ik, new_k, new_v).reshape(n, width)


def _dsa(aq, iq, ikw, kk, vv, ikk, tq, out):
    n_streams, n_keys, _ = kk.shape
    n = aq.shape[0]
    per_stream = lambda a: a.reshape(n_streams, n // n_streams, a.shape[1])
    aq, iq, ikw = per_stream(aq), per_stream(iq), per_stream(ikw)
    nq = n // n_streams // tq
    n_sel = min(DSA_TOPK, n_keys // 4)
    group = max(1, min(n_streams, DSA_ROWS // tq, DSA_GROUP_MAX))
    out = per_stream(out)
    for jq in range(nq):
        out = _dsa_call(aq, iq, ikw, kk, vv, ikk, group, tq, jq, (jq + 1) * tq, None, n_sel, f"dsa_p{jq}", prev=out)
    return out.reshape(n, -1)


def _pack_rows(v):
    q = QUARTER
    bits = lambda x: pltpu.bitcast(x.astype(BF16).astype(F32), I32)
    pair = lambda c: lax.shift_right_logical(bits(v[:, c * q:(c + 1) * q]), 16) | bits(v[:, (c + 1) * q:(c + 2) * q])
    return pair(0), pair(2)


def _unpack_rows(lo, hi):
    parts = []
    for w in (lo, hi):
        parts.append(pltpu.bitcast(lax.shift_left(w, 16), F32))
        parts.append(pltpu.bitcast(w & jnp.int32(-65536), F32))
    return jnp.concatenate(parts, axis=1)


def _layer_norm(v, g, b):
    mu = jnp.mean(v, axis=-1, keepdims=True)
    d = v - mu
    var = jnp.mean(d * d, axis=-1, keepdims=True)
    return d * lax.rsqrt(var + LN_EPS) * g + b


def _merge_kernel(x_ref, ret_ref, od_ref, gr_ref, ga_ref, wr_ref, wd_ref, wo_ref, g1_ref, b1_ref, rwt_ref, rb_ref,
                  tri_ref, cnt0_ref, h1_ref, hlo_ref, hhi_ref, gtm_ref, ek_ref, rk_ref, tot_ref, cnt_ref):
    @pl.when(pl.program_id(0) == 0)
    def _():
        cnt_ref[...] = cnt0_ref[:, 0:1]

    y_ret = _mm(ret_ref[...], wr_ref[...])
    y_dsa = _mm(od_ref[...], wd_ref[...])
    merged = jax.nn.sigmoid(gr_ref[...]) * y_ret + jax.nn.sigmoid(ga_ref[...]) * y_dsa
    mix = _mm(merged.astype(BF16), wo_ref[...])
    h1 = _layer_norm(DEEPNORM_ALPHA * x_ref[...] + mix, g1_ref[...], b1_ref[...])
    h1_ref[...] = h1
    hlo_ref[...], hhi_ref[...] = _pack_rows(h1)

    logits = lax.dot_general(rwt_ref[...], h1, (((1,), (1,)), ((), ())), preferred_element_type=F32,
                             precision=lax.Precision.HIGHEST) + rb_ref[...]
    tm = logits.shape[1]
    e_iota = lax.broadcasted_iota(I32, (N_EXPERTS, tm), 0)
    tops, hots, firsts = [], [], []
    for _ in range(MOE_TOP_K):
        m = jnp.max(logits, axis=0, keepdims=True)
        first = jnp.min(jnp.where(logits == m, e_iota, N_EXPERTS), axis=0, keepdims=True)
        hot = e_iota == first
        tops.append(m)
        hots.append(hot)
        firsts.append(first)
        logits = jnp.where(hot, -jnp.inf, logits)
    exps = [jnp.exp(m - tops[0]) for m in tops]
    den = exps[0] + exps[1] + exps[2] + exps[3]
    sel = jnp.zeros((N_EXPERTS, tm), F32)
    for hot in hots:
        sel = sel + jnp.where(hot, 1.0, 0.0)
    rank = _mm(sel.astype(BF16), tri_ref[...]) + cnt_ref[...]
    ranks = [jnp.sum(jnp.where(hot, rank, 0.0), axis=0, keepdims=True).astype(I32) for hot in hots]
    pad_i = jnp.zeros((8 - MOE_TOP_K, tm), I32)
    ek_ref[...] = jnp.concatenate(firsts + [pad_i], axis=0)
    rk_ref[...] = jnp.concatenate(ranks + [pad_i], axis=0)
    gates = jnp.concatenate([e / den for e in exps] + [jnp.zeros((LANES - MOE_TOP_K, tm), F32)], axis=0)
    gtm_ref[...] = gates.T
    cnt_ref[...] = cnt_ref[...] + jnp.sum(sel, axis=1, keepdims=True)
    tot_ref[...] = jnp.broadcast_to(cnt_ref[...], tot_ref.shape)


def _merge(x, ret, od, gr, ga, wr, wd, wo, g1, b1, rwt, rb, cnt0):
    n = x.shape[0]
    tm = MERGE_TM
    row = lambda w: pl.BlockSpec((tm, w), lambda i: (i, 0))
    const = lambda a: pl.BlockSpec(a.shape, lambda i: (0,) * a.ndim)
    col = pl.BlockSpec((8, tm), lambda i: (0, i))
    tri = jnp.asarray(np.triu(np.ones((tm, tm), np.float32), 1), BF16)
    return pl.pallas_call(
        _merge_kernel,
        grid=(n // tm,),
        in_specs=[row(D_MODEL), row(1024), row(512), row(1024), row(1024), const(wr), const(wd), const(wo),
                  const(g1), const(b1), const(rwt), const(rb), const(tri), const(cnt0)],
        out_specs=[row(D_MODEL), row(QUARTER), row(QUARTER), row(LANES), col, col,
                   pl.BlockSpec((N_EXPERTS, LANES), lambda i: (0, 0))],
        out_shape=[jax.ShapeDtypeStruct((n, D_MODEL), F32), jax.ShapeDtypeStruct((n, QUARTER), I32),
                   jax.ShapeDtypeStruct((n, QUARTER), I32), jax.ShapeDtypeStruct((n, LANES), F32),
                   jax.ShapeDtypeStruct((8, n), I32), jax.ShapeDtypeStruct((8, n), I32),
                   jax.ShapeDtypeStruct((N_EXPERTS, LANES), F32)],
        scratch_shapes=[pltpu.VMEM((N_EXPERTS, 1), F32)],
        compiler_params=pltpu.CompilerParams(dimension_semantics=("arbitrary",), vmem_limit_bytes=48 * MIB),
        name="merge",
    )(x, ret, od, gr, ga, wr, wd, wo, g1, b1, rwt, rb, tri, cnt0)


def _deinterleave_kernel(w_ref, o_ref):
    r = lax.broadcasted_iota(I32, (UP_BLOCK, UP_BLOCK), 0)
    c = lax.broadcasted_iota(I32, (UP_BLOCK, UP_BLOCK), 1)
    src = jnp.where(c < LANES, 2 * c, 2 * (c - LANES) + 1)
    perm = jnp.where(r == src, 1.0, 0.0).astype(BF16)
    for b in range(w_ref.shape[2] // UP_BLOCK):
        sl = slice(b * UP_BLOCK, (b + 1) * UP_BLOCK)
        o_ref[0, :, sl] = _mm(w_ref[0, :, sl].astype(BF16), perm).astype(BF16)


def _deinterleave_w_up(w_up):
    n_e, d_in, d_out = w_up.shape
    cols = d_out
    spec = pl.BlockSpec((1, d_in, cols), lambda e, c: (e, 0, c))
    return pl.pallas_call(
        _deinterleave_kernel,
        grid=(n_e, d_out // cols),
        in_specs=[spec],
        out_specs=spec,
        out_shape=jax.ShapeDtypeStruct(w_up.shape, BF16),
        compiler_params=pltpu.CompilerParams(dimension_semantics=("parallel", "parallel"),
                                             vmem_limit_bytes=40 * MIB),
        name="w_up_prep",
    )(w_up)


def _pos_kernel(off_ref, ek_ref, rk_ref, pos_ref):
    ek = ek_ref[...]
    pos = rk_ref[...]
    for e in range(N_EXPERTS):
        pos = pos + jnp.where(ek == e, off_ref[e], 0)
    pos_ref[...] = pos


def _positions(off, ek, rk):
    n = ek.shape[1]
    tn = min(n, 2048)
    spec = pl.BlockSpec((8, tn), lambda i, off: (0, i))
    return pl.pallas_call(
        _pos_kernel,
        grid_spec=pltpu.PrefetchScalarGridSpec(num_scalar_prefetch=1, grid=(n // tn,), in_specs=[spec, spec],
                                               out_specs=spec),
        out_shape=jax.ShapeDtypeStruct((8, n), I32),
        name="moe_pos",
    )(off, ek, rk)


def _sc_mesh():
    return plsc.VectorSubcoreMesh(core_axis_name="core", subcore_axis_name="subcore")


def _sc_scatter(xs, pos_flat, n_rows):
    width, dtype = xs[0].shape[1], xs[0].dtype
    windows = [x.shape[0] // SC_WINDOW for x in xs]
    nw = sum(windows)

    @functools.partial(pl.kernel, out_type=jax.ShapeDtypeStruct((n_rows, width), dtype), mesh=_sc_mesh())
    def scatter(*refs):
        p_hbm, o_hbm = refs[len(xs)], refs[len(xs) + 1]

        def body(x_vmem, *p_vmem):
            for p in p_vmem:
                pltpu.sync_copy(x_vmem, o_hbm.at[p.at[0]])

        first = 0
        for x_hbm, nw_g in zip(refs[:len(xs)], windows):
            pltpu.emit_pipeline(
                body, grid=(nw_g,),
                in_specs=[pl.BlockSpec((SC_WINDOW, width), lambda i: (i, 0))] +
                         [pl.BlockSpec((1, SC_WINDOW), functools.partial(lambda w0, i: (0, w0 + i), k * nw + first))
                          for k in range(MOE_TOP_K)],
                out_specs=[], core_axis_name=("core", "subcore"), dimension_semantics=(pltpu.PARALLEL,),
            )(x_hbm, *([p_hbm] * MOE_TOP_K))
            first += nw_g

    return scatter(*xs, pos_flat)


def _sc_gather(y, pos_flat):
    m = pos_flat.shape[1]

    @functools.partial(pl.kernel, out_type=jax.ShapeDtypeStruct((m, y.shape[1]), y.dtype), mesh=_sc_mesh())
    def gather(y_hbm, p_hbm, o_hbm):
        def body(p_vmem, o_vmem):
            pltpu.sync_copy(y_hbm.at[p_vmem.at[0]], o_vmem)

        pltpu.emit_pipeline(
            body, grid=(m // SC_WINDOW,),
            in_specs=[pl.BlockSpec((1, SC_WINDOW), lambda i: (0, i))],
            out_specs=[pl.BlockSpec((SC_WINDOW, y.shape[1]), lambda i: (i, 0))],
            core_axis_name=("core", "subcore"), dimension_semantics=(pltpu.PARALLEL,),
        )(p_hbm, o_hbm)

    return gather(y, pos_flat)


def _ffn_kernel(be_ref, nu_ref, xlo_ref, xhi_ref, wup_ref, bup_ref, wdn_ref, bdn_ref, ylo_ref, yhi_ref):
    @pl.when(pl.program_id(0) < nu_ref[0])
    def _():
        x = _unpack_rows(xlo_ref[...], xhi_ref[...]).astype(BF16)
        h = _mm(x, wup_ref[0]) + bup_ref[0]
        acts = []
        for b in range(2 * D_FF // UP_BLOCK):
            glu = jnp.minimum(h[:, b * UP_BLOCK:b * UP_BLOCK + LANES], SWIGLU_LIMIT)
            lin = jnp.clip(h[:, b * UP_BLOCK + LANES:(b + 1) * UP_BLOCK], -SWIGLU_LIMIT, SWIGLU_LIMIT)
            acts.append(glu * jax.nn.sigmoid(SWIGLU_ALPHA * glu) * (lin + 1.0))
        act = jnp.concatenate(acts, axis=1)
        ylo_ref[...], yhi_ref[...] = _pack_rows(_mm(act.astype(BF16), wdn_ref[0]) + bdn_ref[0])


def _ffn(block_expert, n_used, xs_lo, xs_hi, wup, bup, wdn, bdn):
    rows = xs_lo.shape[0]
    blk = FFN_BLOCK
    row = pl.BlockSpec((blk, QUARTER), lambda i, be, nu: (i, 0))
    per_expert = lambda a: pl.BlockSpec((1,) + a.shape[1:], lambda i, be, nu: (be[i], 0, 0))
    return pl.pallas_call(
        _ffn_kernel,
        grid_spec=pltpu.PrefetchScalarGridSpec(
            num_scalar_prefetch=2, grid=(rows // blk,),
            in_specs=[row, row, per_expert(wup), per_expert(bup), per_expert(wdn), per_expert(bdn)],
            out_specs=[row, row]),
        out_shape=[jax.ShapeDtypeStruct((rows, QUARTER), I32), jax.ShapeDtypeStruct((rows, QUARTER), I32)],
        compiler_params=pltpu.CompilerParams(dimension_semantics=("arbitrary",), vmem_limit_bytes=40 * MIB),
        name="moe_ffn",
    )(block_expert, n_used, xs_lo, xs_hi, wup, bup, wdn, bdn)


def _combine_kernel(h1_ref, olo_ref, ohi_ref, gtm_ref, g2_ref, b2_ref, o_ref):
    g = gtm_ref[...]
    y = jnp.zeros(h1_ref.shape, F32)
    for k in range(MOE_TOP_K):
        y = y + g[:, k:k + 1] * _unpack_rows(olo_ref[k], ohi_ref[k])
    o_ref[...] = _layer_norm(DEEPNORM_ALPHA * h1_ref[...] + y, g2_ref[...], b2_ref[...])


def _combine(h1, og_lo, og_hi, gtm, g2, b2, row0):
    n = h1.shape[0]
    n_all = og_lo.shape[0] // MOE_TOP_K
    tm = MERGE_TM
    row = lambda w: pl.BlockSpec((tm, w), lambda i: (i, 0))
    const = lambda a: pl.BlockSpec(a.shape, lambda i: (0,) * a.ndim)
    picked = pl.BlockSpec((MOE_TOP_K, tm, QUARTER), lambda i: (0, i + row0 // tm, 0))
    return pl.pallas_call(
        _combine_kernel,
        grid=(n // tm,),
        in_specs=[row(D_MODEL), picked, picked, row(LANES), const(g2), const(b2)],
        out_specs=row(D_MODEL),
        out_shape=jax.ShapeDtypeStruct((n, D_MODEL), F32),
        compiler_params=pltpu.CompilerParams(dimension_semantics=("parallel",), vmem_limit_bytes=40 * MIB),
        name="moe_combine",
    )(h1, og_lo.reshape(MOE_TOP_K, n_all, QUARTER), og_hi.reshape(MOE_TOP_K, n_all, QUARTER), gtm, g2, b2)


def _moe(groups, totals, wup, bup, wdn, bdn, g2, b2):
    blk = FFN_BLOCK
    sizes = [g[0].shape[0] for g in groups]
    n_all = sum(sizes)
    n_rows = MOE_TOP_K * n_all + N_EXPERTS * blk
    counts = totals[:, 0].astype(I32)
    padded = (counts + blk - 1) // blk * blk
    ends = jnp.cumsum(padded)
    block_start = jnp.arange(n_rows // blk, dtype=I32) * blk
    block_expert = jnp.minimum(jnp.sum((ends[None, :] <= block_start[:, None]).astype(I32), axis=1), N_EXPERTS - 1)
    n_used = (ends[-1:] // blk).astype(I32)
    cat = lambda i, axis: jnp.concatenate([g[i] for g in groups], axis=axis)
    pos = _positions((ends - padded).astype(I32), cat(4, 1), cat(5, 1))
    pos_flat = pos[:MOE_TOP_K].reshape(1, MOE_TOP_K * n_all)
    xs_lo = _sc_scatter([g[1] for g in groups], pos_flat, n_rows)
    xs_hi = _sc_scatter([g[2] for g in groups], pos_flat, n_rows)
    ys_lo, ys_hi = _ffn(block_expert, n_used, xs_lo, xs_hi, wup, bup, wdn, bdn)
    og_lo, og_hi = _sc_gather(ys_lo, pos_flat), _sc_gather(ys_hi, pos_flat)
    starts = np.cumsum([0] + sizes[:-1]).tolist()
    return [_combine(g[0], og_lo, og_hi, g[3], g2, b2, r0) for g, r0 in zip(groups, starts)]


def _mixer(x, pos_tab, tab_period, s0, rows_per_stream, ret_rows, caches, weights, cnt0):
    wp, wr, wd, wo, g1, b1, rwt, rb = weights
    n = x.shape[0]
    n_streams = n // rows_per_stream
    rq, rk, rv, rg, aq, ak, av, iq, ikw, gr, ga, od0 = _project(x, wp, pos_tab, tab_period)
    ret, s_new = _retention(rq, rk, rv, rg, s0, rows_per_stream, ret_rows)
    if caches is None:
        per_stream = lambda a: a.reshape(n_streams, rows_per_stream, a.shape[1])
        od = _dsa(aq, iq, ikw, per_stream(ak), per_stream(av), per_stream(ikw), DSA_TQ, od0)
    else:
        od = _dsa_cached(aq, iq, ikw, *caches, ak, av, rows_per_stream)
    h1, h_lo, h_hi, gtm, ek, rk_, totals = _merge(x, ret, od, gr, ga, wr, wd, wo, g1, b1, rwt, rb, cnt0)
    return (h1, h_lo, h_hi, gtm, ek, rk_), totals, (s_new, ak, av, ikw[:, :IDX_DIM])


def kernel(x_prompt, x_sample, state_ret, cache_k, cache_v, cache_idx_k, w_in, w_ret_o, w_dsa_o, w_o,
           ln1_g, ln1_b, router_w, router_b, w_up, b_up, w_down, b_down, ln2_g, ln2_b):
    assert w_in.shape[0] == DEPTH
    batch, seq, _ = x_prompt.shape
    dec_batch, dec_seq, _ = x_sample.shape
    past = cache_k.shape[2]
    assert seq % PROJ_TM == 0 and seq % DSA_TQ == 0 and PROJ_TM % dec_seq == 0

    l = 0
    mixer_w = (_pack_w_in(w_in[l]), w_ret_o[l].astype(BF16), w_dsa_o[l].astype(BF16), w_o[l].astype(BF16),
               ln1_g[l][None, :], ln1_b[l][None, :], router_w[l].T, router_b[l][:, None])
    moe_w = (
        _deinterleave_w_up(w_up[l]),
        b_up[l].reshape(N_EXPERTS, 2 * D_FF // UP_BLOCK, LANES, 2).transpose(0, 1, 3, 2).reshape(N_EXPERTS, 1, 2 * D_FF),
        w_down[l].astype(BF16), b_down[l][:, None, :], ln2_g[l][None, :], ln2_b[l][None, :])

    tab_p = _rot_tables(jnp.arange(seq))
    zeros_state = jnp.zeros((batch, RET_HEADS, RET_DK, RET_DV), F32)
    moe_p, totals_p, (s_p, k_p, v_p, ik_p) = _mixer(
        x_prompt.reshape(batch * seq, D_MODEL), tab_p, seq // PROJ_TM, zeros_state, seq, RET_CHUNK, None, mixer_w,
        jnp.zeros((N_EXPERTS, LANES), F32))

    tab_s = jnp.tile(_rot_tables(past + jnp.arange(dec_seq)), (PROJ_TM // dec_seq, 1))
    caches = (cache_k[l].reshape(dec_batch, past, LANES), cache_v[l].reshape(dec_batch, past, LANES), cache_idx_k[l])
    moe_s, totals, (s_s, k_s, v_s, ik_s) = _mixer(
        x_sample.reshape(dec_batch * dec_seq, D_MODEL), tab_s, 1, state_ret[l], dec_seq, dec_seq, caches, mixer_w,
        totals_p)

    y_p, y_s = _moe([moe_p, moe_s], totals, *moe_w)

    kv = (DSA_KV_HEADS, DSA_HEAD_DIM)
    return (y_p.reshape(batch, seq, D_MODEL), y_s.reshape(dec_batch, dec_seq, D_MODEL),
            s_p[None], k_p.reshape(1, batch, seq, *kv), v_p.reshape(1, batch, seq, *kv),
            ik_p.reshape(1, batch, seq, IDX_DIM),
            s_s[None], k_s.reshape(1, dec_batch, dec_seq, *kv), v_s.reshape(1, dec_batch, dec_seq, *kv),
            ik_s.reshape(1, dec_batch, dec_seq, IDX_DIM))
```

```python
import functools

import numpy as np
import jax
import jax.numpy as jnp
from jax import lax
from jax.experimental import pallas as pl
from jax.experimental.pallas import tpu as pltpu
from jax.experimental.pallas import tpu_sc as plsc

F32 = jnp.float32
BF16 = jnp.bfloat16
I32 = jnp.int32

D_MODEL = 1024
CHUNK = 64
RET_HEADS = 4
RET_DK = 128
RET_DV = 256
RET_ROPE_BASE = 10000.0
DSA_HEADS = 8
DSA_KV_HEADS = 2
DSA_HEAD_DIM = 64
IDX_HEADS = 8
IDX_DIM = 64
DSA_TOPK = 256
ROPE_THETA = 500000.0
N_EXPERTS = 32
MOE_TOP_K = 4
D_FF = 1024
SWIGLU_ALPHA = 1.702
SWIGLU_LIMIT = 7.0
LN_EPS = 1e-5
GN_EPS = 1e-6
DEPTH = 1
DEEPNORM_ALPHA = (2.0 * DEPTH) ** 0.25
PROJ_WIDTHS = (RET_HEADS * RET_DK, RET_HEADS * RET_DK, RET_HEADS * RET_DV, RET_HEADS * RET_DV,
               DSA_HEADS * DSA_HEAD_DIM, DSA_KV_HEADS * DSA_HEAD_DIM, DSA_KV_HEADS * DSA_HEAD_DIM,
               IDX_HEADS * IDX_DIM, IDX_DIM, IDX_HEADS, D_MODEL, D_MODEL)

assert CHUNK & (CHUNK - 1) == 0

LANES = 128
MIB = 1024 * 1024

OFF_RQ, OFF_RK, OFF_RV, OFF_RG = 0, 512, 1024, 2048
OFF_AQ, OFF_AK, OFF_AV, OFF_IQ, OFF_IKW = 3072, 3584, 3712, 3840, 4352
OFF_GR, OFF_GA, PACKED_COLS = 4480, 5504, 6528
TAB_COLS = 8 * LANES

PROJ_TM = 512
RET_CHUNK = 128
RET_CHUNKS_PER_STEP = 8
DSA_TQ = 256
DSA_ROWS = 512
DSA_GROUP_MAX = 4
SEARCH_UNROLL = 4
DSA_SHORT_EXTENT = 1280
DSA_CHAIN_CELLS = 768 * 1024
MERGE_TM = 512
FFN_BLOCK = 512
SC_WINDOW = 128
QUARTER = D_MODEL // 4
UP_BLOCK = 2 * LANES


def _nt(a, b):
    return lax.dot_general(a, b, (((1,), (1,)), ((), ())), preferred_element_type=F32)


def _mm(a, b):
    return jnp.dot(a, b, preferred_element_type=F32)


def _proj_kernel(x_ref, w_ref, tab_ref, rq_ref, rk_ref, rv_ref, rg_ref, aq_ref, ak_ref, av_ref,
                 iq_ref, ikw_ref, gr_ref, ga_ref, od0_ref):
    xb = x_ref[...].astype(BF16)
    od0_ref[...] = jnp.zeros(od0_ref.shape, od0_ref.dtype)

    def mm(c0, n):
        return _mm(xb, w_ref[:, c0:c0 + n])

    def tab(i):
        return tab_ref[:, i * LANES:(i + 1) * LANES]

    def rot_full(z):
        return z * tab(0) + pltpu.roll(z, 64, 1) * tab(1)

    def rot_part(z, c):
        return z * tab(c) + pltpu.roll(z, LANES - 8, 1) * tab(c + 1) + pltpu.roll(z, 8, 1) * tab(c + 2)

    z = mm(OFF_RQ, 512)
    for h in range(4):
        sl = slice(h * LANES, (h + 1) * LANES)
        rq_ref[:, sl] = rot_full(z[:, sl]).astype(BF16)
    z = mm(OFF_RK, 512)
    for h in range(4):
        sl = slice(h * LANES, (h + 1) * LANES)
        rk_ref[:, sl] = (rot_full(z[:, sl]) * (RET_DK ** -0.5)).astype(BF16)
    for c in range(2):
        rv_ref[:, c * 512:(c + 1) * 512] = mm(OFF_RV + c * 512, 512).astype(BF16)
    for c in range(2):
        rg_ref[:, c * 512:(c + 1) * 512] = mm(OFF_RG + c * 512, 512)
    z = mm(OFF_AQ, 512)
    for h in range(4):
        sl = slice(h * LANES, (h + 1) * LANES)
        aq_ref[:, sl] = rot_part(z[:, sl], 2).astype(BF16)
    z = mm(OFF_AK, 256)
    ak_ref[...] = rot_part(z[:, :LANES], 2)
    av_ref[...] = z[:, LANES:]
    z = mm(OFF_IQ, 512)
    for h in range(4):
        sl = slice(h * LANES, (h + 1) * LANES)
        iq_ref[:, sl] = rot_part(z[:, sl], 2).astype(BF16)
    ikw_ref[...] = rot_part(mm(OFF_IKW, LANES), 5)
    for c in range(2):
        gr_ref[:, c * 512:(c + 1) * 512] = mm(OFF_GR + c * 512, 512)
    for c in range(2):
        ga_ref[:, c * 512:(c + 1) * 512] = mm(OFF_GA + c * 512, 512)


def _rot_tables(pos):
    p = pos.shape[0]
    posf = pos.astype(F32)[:, None]
    ret_f = RET_ROPE_BASE ** (-jnp.linspace(0.0, 1.0, RET_DK // 2, dtype=F32))
    ang = posf * ret_f[None, :]
    c, s = jnp.cos(ang), jnp.sin(ang)
    cos_r = jnp.concatenate([c, c], 1)
    sin_r = jnp.concatenate([-s, s], 1)
    n_rot = DSA_HEAD_DIM // 4
    att_f = ROPE_THETA ** (-jnp.arange(0, n_rot, 2, dtype=F32) / n_rot)
    ang2 = posf * att_f[None, :]
    c2, s2 = jnp.cos(ang2), jnp.sin(ang2)
    half = n_rot // 2
    rest = DSA_HEAD_DIM - 2 * half
    c64 = jnp.concatenate([c2, c2, jnp.ones((p, rest), F32)], 1)
    s1_64 = jnp.concatenate([-s2, jnp.zeros((p, DSA_HEAD_DIM - half), F32)], 1)
    s2_64 = jnp.concatenate([jnp.zeros((p, half), F32), s2, jnp.zeros((p, rest), F32)], 1)
    z64 = jnp.zeros((p, DSA_HEAD_DIM), F32)
    ci = jnp.concatenate([c64, jnp.full((p, IDX_HEADS), IDX_HEADS ** -0.5, F32),
                          jnp.zeros((p, DSA_HEAD_DIM - IDX_HEADS), F32)], 1)
    return jnp.concatenate([cos_r, sin_r,
                            jnp.concatenate([c64, c64], 1), jnp.concatenate([s1_64, s1_64], 1),
                            jnp.concatenate([s2_64, s2_64], 1),
                            ci, jnp.concatenate([s1_64, z64], 1), jnp.concatenate([s2_64, z64], 1)], 1)


def _pack_w_in(w):
    cuts = np.cumsum(PROJ_WIDTHS)[:-1].tolist()
    rq, rk, rv, rg, aq, ak, av, iq, ik, iw, gr, ga = jnp.split(w, cuts, axis=1)
    pad = jnp.zeros((w.shape[0], LANES - IDX_DIM - IDX_HEADS), w.dtype)
    return jnp.concatenate([rq, rk, rv, rg, aq, ak, av, iq, ik, iw, pad, gr, ga], axis=1).astype(BF16)


def _project(x, wp, tab, tab_period):
    n = x.shape[0]
    tm = PROJ_TM
    row = lambda w: pl.BlockSpec((tm, w), lambda i: (i, 0))
    out_shapes = [((n, 512), BF16), ((n, 512), BF16), ((n, 1024), BF16), ((n, 1024), F32),
                  ((n, 512), BF16), ((n, LANES), F32), ((n, LANES), F32), ((n, 512), BF16),
                  ((n, LANES), F32), ((n, 1024), F32), ((n, 1024), F32), ((n, DSA_HEADS * DSA_HEAD_DIM), BF16)]
    return pl.pallas_call(
        _proj_kernel,
        grid=(n // tm,),
        in_specs=[row(D_MODEL),
                  pl.BlockSpec((D_MODEL, PACKED_COLS), lambda i: (0, 0), pipeline_mode=pl.Buffered(1)),
                  pl.BlockSpec((tm, TAB_COLS), lambda i: (i % tab_period, 0))],
        out_specs=[row(s[1]) for s, _ in out_shapes],
        out_shape=[jax.ShapeDtypeStruct(s, d) for s, d in out_shapes],
        compiler_params=pltpu.CompilerParams(dimension_semantics=("parallel",), vmem_limit_bytes=52 * MIB),
        name="proj",
    )(x, wp, tab)


def _ret_kernel(dec_ref, xi_ref, zeta_ref, rq_ref, rk_ref, rv_ref, rg_ref, s0_ref, ret_ref, sout_ref, st_ref,
                *, rows, n_chunk, g_pow):
    j = pl.program_id(1)
    cpad = RET_CHUNK

    @pl.when(j == 0)
    def _():
        st_ref[...] = s0_ref[0]

    def padded(v):
        if rows == cpad:
            return v
        return jnp.concatenate([v, jnp.zeros((cpad - rows, v.shape[1]), v.dtype)], axis=0)

    for c in range(n_chunk):
        rs = slice(c * rows, (c + 1) * rows)
        for h in range(RET_HEADS):
            ks = slice(h * RET_DK, (h + 1) * RET_DK)
            vs = slice(h * RET_DV, (h + 1) * RET_DV)
            q = padded(rq_ref[rs, ks])
            kt = padded(rk_ref[rs, ks].astype(F32)).T
            v = padded(rv_ref[rs, vs])
            s = st_ref[h]
            sc = _mm(q, kt.astype(BF16)) * dec_ref[h]
            o = _mm(sc.astype(BF16), v) + _mm(q, s.astype(BF16)) * xi_ref[h]
            st_ref[h] = g_pow[h] * s + _mm((kt * zeta_ref[h]).astype(BF16), v)
            o = o[:rows]
            mu = jnp.mean(o, axis=-1, keepdims=True)
            d = o - mu
            var = jnp.mean(d * d, axis=-1, keepdims=True)
            gn = d * lax.rsqrt(var + GN_EPS)
            g = rg_ref[rs, vs]
            ret_ref[rs, vs] = (gn * (g * jax.nn.sigmoid(g))).astype(BF16)

    @pl.when(j == pl.num_programs(1) - 1)
    def _():
        sout_ref[0] = st_ref[...]


def _retention(rq, rk, rv, rg, s0, rows_per_stream, rows):
    n = rq.shape[0]
    n_streams = n // rows_per_stream
    n_chunk = min(rows_per_stream // rows, RET_CHUNKS_PER_STEP)
    blk = rows * n_chunk
    nb = rows_per_stream // blk
    gam = 1.0 - 2.0 ** (-5.0 - np.arange(RET_HEADS, dtype=np.float64))
    i = np.arange(RET_CHUNK, dtype=np.float64)
    diff = i[:, None] - i[None, :]
    dec = np.where(diff >= 0, gam[:, None, None] ** np.maximum(diff, 0.0)[None], 0.0)
    xi = gam[:, None, None] ** (i + 1.0)[None, :, None]
    zeta = np.where(i < rows, gam[:, None, None] ** (rows - 1.0 - i)[None, None, :], 0.0)
    g_pow = tuple(float(g ** rows) for g in gam)
    const = lambda shape: pl.BlockSpec(shape, lambda s, j: (0,) * len(shape))
    row = lambda w: pl.BlockSpec((blk, w), lambda s, j: (s * nb + j, 0))
    st = pl.BlockSpec((1, RET_HEADS, RET_DK, RET_DV), lambda s, j: (s, 0, 0, 0))
    return pl.pallas_call(
        functools.partial(_ret_kernel, rows=rows, n_chunk=n_chunk, g_pow=g_pow),
        grid=(n_streams, nb),
        in_specs=[const((RET_HEADS, RET_CHUNK, RET_CHUNK)), const((RET_HEADS, RET_CHUNK, 1)),
                  const((RET_HEADS, 1, RET_CHUNK)), row(512), row(512), row(1024), row(1024), st],
        out_specs=[row(1024), st],
        out_shape=[jax.ShapeDtypeStruct((n, RET_HEADS * RET_DV), BF16),
                   jax.ShapeDtypeStruct((n_streams, RET_HEADS, RET_DK, RET_DV), F32)],
        scratch_shapes=[pltpu.VMEM((RET_HEADS, RET_DK, RET_DV), F32)],
        compiler_params=pltpu.CompilerParams(dimension_semantics=("parallel", "arbitrary"),
                                             vmem_limit_bytes=32 * MIB),
        name="retention",
    )(jnp.asarray(dec, F32), jnp.asarray(xi, F32), jnp.asarray(zeta, F32), rq, rk, rv, rg, s0)


def _dsa_kernel(aq_ref, iq_ref, ikwq_ref, kk_ref, vv_ref, ikk_ref, *rest,
                group, tq, n_keys, row0, limit_const, n_sel):
    o_ref, w_ref, bias_ref = rest[-3:]
    rows = group * tq
    stack = max(c for c in (1, 2, 4) if c == 1 or c * tq * n_keys <= DSA_CHAIN_CELLS)
    search_unroll = 2 * SEARCH_UNROLL if n_keys <= DSA_SHORT_EXTENT else SEARCH_UNROLL
    nsel_f = float(n_sel)
    neg_inf = -jnp.inf

    col = lax.broadcasted_iota(I32, (tq, n_keys), 1)
    if limit_const is None:
        chunk_of_row = jnp.right_shift(lax.broadcasted_iota(I32, (tq, 1), 0) + row0, CHUNK.bit_length() - 1)
        limit = (chunk_of_row + 1) * CHUNK
    else:
        limit = limit_const

    def attend_stream(s, carry):
        bias = bias_ref[pl.ds(pl.multiple_of(s * tq, tq), tq), :]
        kfull = kk_ref[s]
        vfull = vv_ref[s]
        bias_st = jnp.concatenate([bias] * stack, axis=0) if stack > 1 else bias
        for g in range(DSA_KV_HEADS):
            gs = slice(g * DSA_HEAD_DIM, (g + 1) * DSA_HEAD_DIM)
            kg = kfull[:, gs].astype(BF16)
            vg = vfull[:, gs].astype(BF16)
            qs = []
            for p in range(2 * g, 2 * g + 2):
                slab = (aq_ref[s, :, p * LANES:(p + 1) * LANES].astype(F32) * (DSA_HEAD_DIM ** -0.5)).astype(BF16)
                qs += [slab[:, :DSA_HEAD_DIM], slab[:, DSA_HEAD_DIM:]]
            outs = []
            for c in range(len(qs) // stack):
                qh = jnp.concatenate(qs[c * stack:(c + 1) * stack], axis=0) if stack > 1 else qs[c]
                lg = _nt(qh, kg) + bias_st
                m = jnp.max(lg, axis=1, keepdims=True)
                pr = jnp.exp(lg - m)
                den = jnp.sum(pr, axis=1, keepdims=True)
                out = _mm(pr.astype(BF16), vg) / den
                outs += [out[i * tq:(i + 1) * tq] for i in range(stack)]
            for pp in range(2):
                p = 2 * g + pp
                o_ref[s, :, p * LANES:(p + 1) * LANES] = jnp.concatenate(outs[2 * pp:2 * pp + 2], axis=1).astype(BF16)
        return carry

    if n_keys <= n_sel:
        for s in range(group):
            bias_ref[s * tq:(s + 1) * tq, :] = jnp.where(col < limit, 0.0, neg_inf)
        lax.fori_loop(0, group, attend_stream, 0)
        return

    def score_stream(s, carry):
        ikb = ikk_ref[s][:, :IDX_DIM].astype(BF16)
        iww = ikwq_ref[s][:, IDX_DIM:IDX_DIM + IDX_HEADS] * (IDX_DIM ** -0.5)
        acc = jnp.zeros((tq, n_keys), F32)
        for p in range(IDX_HEADS // 2):
            slab = iq_ref[s, :, p * LANES:(p + 1) * LANES]
            for hh in range(2):
                h = 2 * p + hh
                acc = acc + jnp.maximum(_nt(slab[:, hh * IDX_DIM:(hh + 1) * IDX_DIM], ikb), 0.0) * iww[:, h:h + 1]
        w_ref[pl.ds(pl.multiple_of(s * tq, tq), tq), :] = jnp.where(col < limit, acc, neg_inf)
        return carry

    lax.fori_loop(0, group, score_stream, 0)

    sc = w_ref[...]
    pos = jnp.sum(jnp.where(sc >= 0.0, 1.0, 0.0), axis=1, keepdims=True) >= nsel_f
    kk = jnp.where(pos, nsel_f, float(n_keys - n_sel + 1))
    w_ref[...] = jnp.where(pos, sc, -sc)

    def bit_step(i, u):
        cand_u = u | jnp.left_shift(jnp.int32(1), 30 - i)
        cand = pltpu.bitcast(cand_u, F32)
        cnt = jnp.sum(jnp.where(w_ref[...] >= cand, 1.0, 0.0), axis=1, keepdims=True)
        return jnp.where(cnt >= kk, cand_u, u)

    mag_u = lax.fori_loop(0, 31, bit_step, jnp.zeros((rows, 1), I32), unroll=search_unroll)
    mag = pltpu.bitcast(mag_u, F32)
    thr = jnp.where(pos, mag, -mag)

    sc = jnp.where(pos, w_ref[...], -w_ref[...])
    short = jnp.sum(jnp.where(sc >= thr, 1.0, 0.0), axis=1, keepdims=True) < nsel_f
    thr = jnp.where(jnp.logical_and(short, jnp.logical_not(pos)), -pltpu.bitcast(mag_u + 1, F32), thr)
    ge = sc >= thr
    cnt_gt = jnp.sum(jnp.where(sc > thr, 1.0, 0.0), axis=1, keepdims=True)
    cnt_ge = jnp.sum(jnp.where(ge, 1.0, 0.0), axis=1, keepdims=True)
    bias_ref[...] = jnp.where(jnp.logical_and(ge, sc > neg_inf), 0.0, neg_inf)
    excess = jnp.logical_and(cnt_ge > nsel_f, thr > neg_inf)

    @pl.when(jnp.max(jnp.where(excess, 1.0, 0.0)) > 0.0)
    def _():
        need = nsel_f - cnt_gt
        tri = jnp.where(lax.broadcasted_iota(I32, (LANES, LANES), 0) < lax.broadcasted_iota(I32, (LANES, LANES), 1),
                        1.0, 0.0).astype(BF16)
        before = jnp.zeros((rows, 1), F32)
        for b in range(n_keys // LANES):
            sl = slice(b * LANES, (b + 1) * LANES)
            sblk = jnp.where(pos, w_ref[:, sl], -w_ref[:, sl])
            eq = jnp.where(sblk == thr, 1.0, 0.0)
            rank = _mm(eq.astype(BF16), tri) + before
            keep = jnp.logical_or(sblk > thr, jnp.logical_and(sblk == thr, rank < need))
            bias_ref[:, sl] = jnp.where(jnp.logical_and(keep, sblk > neg_inf), 0.0, neg_inf)
            before = before + jnp.sum(eq, axis=1, keepdims=True)

    lax.fori_loop(0, group, attend_stream, 0)


def _dsa_call(aq, iq, ikw, kk, vv, ikk, group, tq, jq, n_keys, limit_const, n_sel, name, prev=None):
    n_streams, rows_per_stream = aq.shape[0], aq.shape[1]
    extra_specs, extra_args, aliases = [], (), {}
    if prev is not None:
        extra_specs, extra_args, aliases = [pl.BlockSpec(memory_space=pl.ANY)], (prev,), {6: 0}
    qrow = lambda a: pl.BlockSpec((group, tq, a.shape[2]), lambda s: (s, jq, 0))
    krow = lambda a: pl.BlockSpec((group, n_keys, a.shape[2]), lambda s: (s, 0, 0))
    width = DSA_HEADS * DSA_HEAD_DIM
    return pl.pallas_call(
        functools.partial(_dsa_kernel, group=group, tq=tq, n_keys=n_keys, row0=jq * tq, limit_const=limit_const,
                          n_sel=n_sel),
        grid=(n_streams // group,),
        in_specs=[qrow(aq), qrow(iq), qrow(ikw), krow(kk), krow(vv), krow(ikk)] + extra_specs,
        out_specs=pl.BlockSpec((group, tq, width), lambda s: (s, jq, 0)),
        out_shape=jax.ShapeDtypeStruct((n_streams, rows_per_stream, width), BF16),
        input_output_aliases=aliases,
        scratch_shapes=[pltpu.VMEM((group * tq, n_keys), F32), pltpu.VMEM((group * tq, n_keys), F32)],
        compiler_params=pltpu.CompilerParams(dimension_semantics=("parallel",), vmem_limit_bytes=56 * MIB),
        name=name,
    )(aq, iq, ikw, kk, vv, ikk, *extra_args)


def _dsa_cached_kernel(aq_ref, iq_ref, ikwq_ref, ck_ref, cv_ref, cik_ref, nk_ref, nv_ref, o_ref,
                       w_ref, bias_ref, kk_ref, vv_ref, ikk_ref, **static):
    past, new = ck_ref.shape[1], nk_ref.shape[1]
    for dst, cache, fresh in ((kk_ref, ck_ref, nk_ref), (vv_ref, cv_ref, nv_ref), (ikk_ref, cik_ref, ikwq_ref)):
        dst[:, :past, :cache.shape[2]] = cache[...]
        dst[:, past:past + new, :] = fresh[...]
        dst[:, past + new:, :] = jnp.zeros((dst.shape[0], dst.shape[1] - past - new, dst.shape[2]), F32)
    _dsa_kernel(aq_ref, iq_ref, ikwq_ref, kk_ref, vv_ref, ikk_ref, o_ref, w_ref, bias_ref, **static)


def _dsa_cached(aq, iq, ikw, cache_k, cache_v, cache_ik, new_k, new_v, tq):
    n_streams, past, _ = cache_k.shape
    n = aq.shape[0]
    per_stream = lambda a: a.reshape(n_streams, tq, a.shape[1])
    aq, iq, ikw, new_k, new_v = per_stream(aq), per_stream(iq), per_stream(ikw), per_stream(new_k), per_stream(new_v)
    limit = past + tq
    n_keys = -(-limit // LANES) * LANES
    group = max(1, min(n_streams, DSA_ROWS // tq, DSA_GROUP_MAX))
    blk = lambda a: pl.BlockSpec((group,) + a.shape[1:], lambda s: (s, 0, 0))
    width = DSA_HEADS * DSA_HEAD_DIM
    return pl.pallas_call(
        functools.partial(_dsa_cached_kernel, group=group, tq=tq, n_keys=n_keys, row0=0, limit_const=limit,
                          n_sel=min(DSA_TOPK, limit // 4)),
        grid=(n_streams // group,),
        in_specs=[blk(a) for a in (aq, iq, ikw, cache_k, cache_v, cache_ik, new_k, new_v)],
        out_specs=pl.BlockSpec((group, tq, width), lambda s: (s, 0, 0)),
        out_shape=jax.ShapeDtypeStruct((n_streams, tq, width), BF16),
        scratch_shapes=[pltpu.VMEM((group * tq, n_keys), F32), pltpu.VMEM((group * tq, n_keys), F32)] +
                       [pltpu.VMEM((group, n_keys, LANES), F32)] * 3,
        compiler_params=pltpu.CompilerParams(dimension_semantics=("parallel",), vmem_limit_bytes=48 * MIB),
        name="dsa_s",
    )(aq, iq, ikw, cache_k, cache_v, cache_ik, new_k, new_v).reshape(n, width)


def _dsa(aq, iq, ikw, kk, vv, ikk, tq, out):
    n_streams, n_keys, _ = kk.shape
    n = aq.shape[0]
    per_stream = lambda a: a.reshape(n_streams, n // n_streams, a.shape[1])
    aq, iq, ikw = per_stream(aq), per_stream(iq), per_stream(ikw)
    nq = n // n_streams // tq
    n_sel = min(DSA_TOPK, n_keys // 4)
    group = max(1, min(n_streams, DSA_ROWS // tq, DSA_GROUP_MAX))
    out = per_stream(out)
    for jq in range(nq):
        out = _dsa_call(aq, iq, ikw, kk, vv, ikk, group, tq, jq, (jq + 1) * tq, None, n_sel, f"dsa_p{jq}", prev=out)
    return out.reshape(n, -1)


def _pack_rows(v):
    q = QUARTER
    bits = lambda x: pltpu.bitcast(x.astype(BF16).astype(F32), I32)
    pair = lambda c: lax.shift_right_logical(bits(v[:, c * q:(c + 1) * q]), 16) | bits(v[:, (c + 1) * q:(c + 2) * q])
    return pair(0), pair(2)


def _unpack_rows(lo, hi):
    parts = []
    for w in (lo, hi):
        parts.append(pltpu.bitcast(lax.shift_left(w, 16), F32))
        parts.append(pltpu.bitcast(w & jnp.int32(-65536), F32))
    return jnp.concatenate(parts, axis=1)


def _layer_norm(v, g, b):
    mu = jnp.mean(v, axis=-1, keepdims=True)
    d = v - mu
    var = jnp.mean(d * d, axis=-1, keepdims=True)
    return d * lax.rsqrt(var + LN_EPS) * g + b


def _merge_kernel(x_ref, ret_ref, od_ref, gr_ref, ga_ref, wr_ref, wd_ref, wo_ref, g1_ref, b1_ref, rwt_ref, rb_ref,
                  tri_ref, cnt0_ref, h1_ref, hlo_ref, hhi_ref, gtm_ref, ek_ref, rk_ref, tot_ref, cnt_ref):
    @pl.when(pl.program_id(0) == 0)
    def _():
        cnt_ref[...] = cnt0_ref[:, 0:1]

    y_ret = _mm(ret_ref[...], wr_ref[...])
    y_dsa = _mm(od_ref[...], wd_ref[...])
    merged = jax.nn.sigmoid(gr_ref[...]) * y_ret + jax.nn.sigmoid(ga_ref[...]) * y_dsa
    mix = _mm(merged.astype(BF16), wo_ref[...])
    h1 = _layer_norm(DEEPNORM_ALPHA * x_ref[...] + mix, g1_ref[...], b1_ref[...])
    h1_ref[...] = h1
    hlo_ref[...], hhi_ref[...] = _pack_rows(h1)

    logits = lax.dot_general(rwt_ref[...], h1, (((1,), (1,)), ((), ())), preferred_element_type=F32,
                             precision=lax.Precision.HIGHEST) + rb_ref[...]
    tm = logits.shape[1]
    e_iota = lax.broadcasted_iota(I32, (N_EXPERTS, tm), 0)
    tops, hots, firsts = [], [], []
    for _ in range(MOE_TOP_K):
        m = jnp.max(logits, axis=0, keepdims=True)
        first = jnp.min(jnp.where(logits == m, e_iota, N_EXPERTS), axis=0, keepdims=True)
        hot = e_iota == first
        tops.append(m)
        hots.append(hot)
        firsts.append(first)
        logits = jnp.where(hot, -jnp.inf, logits)
    exps = [jnp.exp(m - tops[0]) for m in tops]
    den = exps[0] + exps[1] + exps[2] + exps[3]
    sel = jnp.zeros((N_EXPERTS, tm), F32)
    for hot in hots:
        sel = sel + jnp.where(hot, 1.0, 0.0)
    rank = _mm(sel.astype(BF16), tri_ref[...]) + cnt_ref[...]
    ranks = [jnp.sum(jnp.where(hot, rank, 0.0), axis=0, keepdims=True).astype(I32) for hot in hots]
    pad_i = jnp.zeros((8 - MOE_TOP_K, tm), I32)
    ek_ref[...] = jnp.concatenate(firsts + [pad_i], axis=0)
    rk_ref[...] = jnp.concatenate(ranks + [pad_i], axis=0)
    gates = jnp.concatenate([e / den for e in exps] + [jnp.zeros((LANES - MOE_TOP_K, tm), F32)], axis=0)
    gtm_ref[...] = gates.T
    cnt_ref[...] = cnt_ref[...] + jnp.sum(sel, axis=1, keepdims=True)
    tot_ref[...] = jnp.broadcast_to(cnt_ref[...], tot_ref.shape)


def _merge(x, ret, od, gr, ga, wr, wd, wo, g1, b1, rwt, rb, cnt0):
    n = x.shape[0]
    tm = MERGE_TM
    row = lambda w: pl.BlockSpec((tm, w), lambda i: (i, 0))
    const = lambda a: pl.BlockSpec(a.shape, lambda i: (0,) * a.ndim)
    col = pl.BlockSpec((8, tm), lambda i: (0, i))
    tri = jnp.asarray(np.triu(np.ones((tm, tm), np.float32), 1), BF16)
    return pl.pallas_call(
        _merge_kernel,
        grid=(n // tm,),
        in_specs=[row(D_MODEL), row(1024), row(512), row(1024), row(1024), const(wr), const(wd), const(wo),
                  const(g1), const(b1), const(rwt), const(rb), const(tri), const(cnt0)],
        out_specs=[row(D_MODEL), row(QUARTER), row(QUARTER), row(LANES), col, col,
                   pl.BlockSpec((N_EXPERTS, LANES), lambda i: (0, 0))],
        out_shape=[jax.ShapeDtypeStruct((n, D_MODEL), F32), jax.ShapeDtypeStruct((n, QUARTER), I32),
                   jax.ShapeDtypeStruct((n, QUARTER), I32), jax.ShapeDtypeStruct((n, LANES), F32),
                   jax.ShapeDtypeStruct((8, n), I32), jax.ShapeDtypeStruct((8, n), I32),
                   jax.ShapeDtypeStruct((N_EXPERTS, LANES), F32)],
        scratch_shapes=[pltpu.VMEM((N_EXPERTS, 1), F32)],
        compiler_params=pltpu.CompilerParams(dimension_semantics=("arbitrary",), vmem_limit_bytes=48 * MIB),
        name="merge",
    )(x, ret, od, gr, ga, wr, wd, wo, g1, b1, rwt, rb, tri, cnt0)


def _deinterleave_kernel(w_ref, o_ref):
    r = lax.broadcasted_iota(I32, (UP_BLOCK, UP_BLOCK), 0)
    c = lax.broadcasted_iota(I32, (UP_BLOCK, UP_BLOCK), 1)
    src = jnp.where(c < LANES, 2 * c, 2 * (c - LANES) + 1)
    perm = jnp.where(r == src, 1.0, 0.0).astype(BF16)
    for b in range(w_ref.shape[2] // UP_BLOCK):
        sl = slice(b * UP_BLOCK, (b + 1) * UP_BLOCK)
        o_ref[0, :, sl] = _mm(w_ref[0, :, sl].astype(BF16), perm).astype(BF16)


def _deinterleave_w_up(w_up):
    n_e, d_in, d_out = w_up.shape
    cols = d_out
    spec = pl.BlockSpec((1, d_in, cols), lambda e, c: (e, 0, c))
    return pl.pallas_call(
        _deinterleave_kernel,
        grid=(n_e, d_out // cols),
        in_specs=[spec],
        out_specs=spec,
        out_shape=jax.ShapeDtypeStruct(w_up.shape, BF16),
        compiler_params=pltpu.CompilerParams(dimension_semantics=("parallel", "parallel"),
                                             vmem_limit_bytes=40 * MIB),
        name="w_up_prep",
    )(w_up)


def _pos_kernel(off_ref, ek_ref, rk_ref, pos_ref):
    ek = ek_ref[...]
    pos = rk_ref[...]
    for e in range(N_EXPERTS):
        pos = pos + jnp.where(ek == e, off_ref[e], 0)
    pos_ref[...] = pos


def _positions(off, ek, rk):
    n = ek.shape[1]
    tn = min(n, 2048)
    spec = pl.BlockSpec((8, tn), lambda i, off: (0, i))
    return pl.pallas_call(
        _pos_kernel,
        grid_spec=pltpu.PrefetchScalarGridSpec(num_scalar_prefetch=1, grid=(n // tn,), in_specs=[spec, spec],
                                               out_specs=spec),
        out_shape=jax.ShapeDtypeStruct((8, n), I32),
        name="moe_pos",
    )(off, ek, rk)


def _sc_mesh():
    return plsc.VectorSubcoreMesh(core_axis_name="core", subcore_axis_name="subcore")


def _sc_scatter(xs, pos_flat, n_rows):
    width, dtype = xs[0].shape[1], xs[0].dtype
    windows = [x.shape[0] // SC_WINDOW for x in xs]
    nw = sum(windows)

    @functools.partial(pl.kernel, out_type=jax.ShapeDtypeStruct((n_rows, width), dtype), mesh=_sc_mesh())
    def scatter(*refs):
        p_hbm, o_hbm = refs[len(xs)], refs[len(xs) + 1]

        def body(x_vmem, *p_vmem):
            for p in p_vmem:
                pltpu.sync_copy(x_vmem, o_hbm.at[p.at[0]])

        first = 0
        for x_hbm, nw_g in zip(refs[:len(xs)], windows):
            pltpu.emit_pipeline(
                body, grid=(nw_g,),
                in_specs=[pl.BlockSpec((SC_WINDOW, width), lambda i: (i, 0))] +
                         [pl.BlockSpec((1, SC_WINDOW), functools.partial(lambda w0, i: (0, w0 + i), k * nw + first))
                          for k in range(MOE_TOP_K)],
                out_specs=[], core_axis_name=("core", "subcore"), dimension_semantics=(pltpu.PARALLEL,),
            )(x_hbm, *([p_hbm] * MOE_TOP_K))
            first += nw_g

    return scatter(*xs, pos_flat)


def _sc_gather(y, pos_flat):
    m = pos_flat.shape[1]

    @functools.partial(pl.kernel, out_type=jax.ShapeDtypeStruct((m, y.shape[1]), y.dtype), mesh=_sc_mesh())
    def gather(y_hbm, p_hbm, o_hbm):
        def body(p_vmem, o_vmem):
            pltpu.sync_copy(y_hbm.at[p_vmem.at[0]], o_vmem)

        pltpu.emit_pipeline(
            body, grid=(m // SC_WINDOW,),
            in_specs=[pl.BlockSpec((1, SC_WINDOW), lambda i: (0, i))],
            out_specs=[pl.BlockSpec((SC_WINDOW, y.shape[1]), lambda i: (i, 0))],
            core_axis_name=("core", "subcore"), dimension_semantics=(pltpu.PARALLEL,),
        )(p_hbm, o_hbm)

    return gather(y, pos_flat)


def _ffn_kernel(be_ref, nu_ref, xlo_ref, xhi_ref, wup_ref, bup_ref, wdn_ref, bdn_ref, ylo_ref, yhi_ref):
    @pl.when(pl.program_id(0) < nu_ref[0])
    def _():
        x = _unpack_rows(xlo_ref[...], xhi_ref[...]).astype(BF16)
        h = _mm(x, wup_ref[0]) + bup_ref[0]
        acts = []
        for b in range(2 * D_FF // UP_BLOCK):
            glu = jnp.minimum(h[:, b * UP_BLOCK:b * UP_BLOCK + LANES], SWIGLU_LIMIT)
            lin = jnp.clip(h[:, b * UP_BLOCK + LANES:(b + 1) * UP_BLOCK], -SWIGLU_LIMIT, SWIGLU_LIMIT)
            acts.append(glu * jax.nn.sigmoid(SWIGLU_ALPHA * glu) * (lin + 1.0))
        act = jnp.concatenate(acts, axis=1)
        ylo_ref[...], yhi_ref[...] = _pack_rows(_mm(act.astype(BF16), wdn_ref[0]) + bdn_ref[0])


def _ffn(block_expert, n_used, xs_lo, xs_hi, wup, bup, wdn, bdn):
    rows = xs_lo.shape[0]
    blk = FFN_BLOCK
    row = pl.BlockSpec((blk, QUARTER), lambda i, be, nu: (i, 0))
    per_expert = lambda a: pl.BlockSpec((1,) + a.shape[1:], lambda i, be, nu: (be[i], 0, 0))
    return pl.pallas_call(
        _ffn_kernel,
        grid_spec=pltpu.PrefetchScalarGridSpec(
            num_scalar_prefetch=2, grid=(rows // blk,),
            in_specs=[row, row, per_expert(wup), per_expert(bup), per_expert(wdn), per_expert(bdn)],
            out_specs=[row, row]),
        out_shape=[jax.ShapeDtypeStruct((rows, QUARTER), I32), jax.ShapeDtypeStruct((rows, QUARTER), I32)],
        compiler_params=pltpu.CompilerParams(dimension_semantics=("arbitrary",), vmem_limit_bytes=40 * MIB),
        name="moe_ffn",
    )(block_expert, n_used, xs_lo, xs_hi, wup, bup, wdn, bdn)


def _combine_kernel(h1_ref, olo_ref, ohi_ref, gtm_ref, g2_ref, b2_ref, o_ref):
    g = gtm_ref[...]
    y = jnp.zeros(h1_ref.shape, F32)
    for k in range(MOE_TOP_K):
        y = y + g[:, k:k + 1] * _unpack_rows(olo_ref[k], ohi_ref[k])
    o_ref[...] = _layer_norm(DEEPNORM_ALPHA * h1_ref[...] + y, g2_ref[...], b2_ref[...])


def _combine(h1, og_lo, og_hi, gtm, g2, b2, row0):
    n = h1.shape[0]
    n_all = og_lo.shape[0] // MOE_TOP_K
    tm = MERGE_TM
    row = lambda w: pl.BlockSpec((tm, w), lambda i: (i, 0))
    const = lambda a: pl.BlockSpec(a.shape, lambda i: (0,) * a.ndim)
    picked = pl.BlockSpec((MOE_TOP_K, tm, QUARTER), lambda i: (0, i + row0 // tm, 0))
    return pl.pallas_call(
        _combine_kernel,
        grid=(n // tm,),
        in_specs=[row(D_MODEL), picked, picked, row(LANES), const(g2), const(b2)],
        out_specs=row(D_MODEL),
        out_shape=jax.ShapeDtypeStruct((n, D_MODEL), F32),
        compiler_params=pltpu.CompilerParams(dimension_semantics=("parallel",), vmem_limit_bytes=40 * MIB),
        name="moe_combine",
    )(h1, og_lo.reshape(MOE_TOP_K, n_all, QUARTER), og_hi.reshape(MOE_TOP_K, n_all, QUARTER), gtm, g2, b2)


def _moe(groups, totals, wup, bup, wdn, bdn, g2, b2):
    blk = FFN_BLOCK
    sizes = [g[0].shape[0] for g in groups]
    n_all = sum(sizes)
    n_rows = MOE_TOP_K * n_all + N_EXPERTS * blk
    counts = totals[:, 0].astype(I32)
    padded = (counts + blk - 1) // blk * blk
    ends = jnp.cumsum(padded)
    block_start = jnp.arange(n_rows // blk, dtype=I32) * blk
    block_expert = jnp.minimum(jnp.sum((ends[None, :] <= block_start[:, None]).astype(I32), axis=1), N_EXPERTS - 1)
    n_used = (ends[-1:] // blk).astype(I32)
    cat = lambda i, axis: jnp.concatenate([g[i] for g in groups], axis=axis)
    pos = _positions((ends - padded).astype(I32), cat(4, 1), cat(5, 1))
    pos_flat = pos[:MOE_TOP_K].reshape(1, MOE_TOP_K * n_all)
    xs_lo = _sc_scatter([g[1] for g in groups], pos_flat, n_rows)
    xs_hi = _sc_scatter([g[2] for g in groups], pos_flat, n_rows)
    ys_lo, ys_hi = _ffn(block_expert, n_used, xs_lo, xs_hi, wup, bup, wdn, bdn)
    og_lo, og_hi = _sc_gather(ys_lo, pos_flat), _sc_gather(ys_hi, pos_flat)
    starts = np.cumsum([0] + sizes[:-1]).tolist()
    return [_combine(g[0], og_lo, og_hi, g[3], g2, b2, r0) for g, r0 in zip(groups, starts)]


def _mixer(x, pos_tab, tab_period, s0, rows_per_stream, ret_rows, caches, weights, cnt0):
    wp, wr, wd, wo, g1, b1, rwt, rb = weights
    n = x.shape[0]
    n_streams = n // rows_per_stream
    rq, rk, rv, rg, aq, ak, av, iq, ikw, gr, ga, od0 = _project(x, wp, pos_tab, tab_period)
    ret, s_new = _retention(rq, rk, rv, rg, s0, rows_per_stream, ret_rows)
    if caches is None:
        per_stream = lambda a: a.reshape(n_streams, rows_per_stream, a.shape[1])
        od = _dsa(aq, iq, ikw, per_stream(ak), per_stream(av), per_stream(ikw), DSA_TQ, od0)
    else:
        od = _dsa_cached(aq, iq, ikw, *caches, ak, av, rows_per_stream)
    h1, h_lo, h_hi, gtm, ek, rk_, totals = _merge(x, ret, od, gr, ga, wr, wd, wo, g1, b1, rwt, rb, cnt0)
    return (h1, h_lo, h_hi, gtm, ek, rk_), totals, (s_new, ak, av, ikw[:, :IDX_DIM])


def kernel(x_prompt, x_sample, state_ret, cache_k, cache_v, cache_idx_k, w_in, w_ret_o, w_dsa_o, w_o,
           ln1_g, ln1_b, router_w, router_b, w_up, b_up, w_down, b_down, ln2_g, ln2_b):
    assert w_in.shape[0] == DEPTH
    batch, seq, _ = x_prompt.shape
    dec_batch, dec_seq, _ = x_sample.shape
    past = cache_k.shape[2]
    assert seq % PROJ_TM == 0 and seq % DSA_TQ == 0 and PROJ_TM % dec_seq == 0

    l = 0
    mixer_w = (_pack_w_in(w_in[l]), w_ret_o[l].astype(BF16), w_dsa_o[l].astype(BF16), w_o[l].astype(BF16),
               ln1_g[l][None, :], ln1_b[l][None, :], router_w[l].T, router_b[l][:, None])
    moe_w = (
        _deinterleave_w_up(w_up[l]),
        b_up[l].reshape(N_EXPERTS, 2 * D_FF // UP_BLOCK, LANES, 2).transpose(0, 1, 3, 2).reshape(N_EXPERTS, 1, 2 * D_FF),
        w_down[l].astype(BF16), b_down[l][:, None, :], ln2_g[l][None, :], ln2_b[l][None, :])

    tab_p = _rot_tables(jnp.arange(seq))
    zeros_state = jnp.zeros((batch, RET_HEADS, RET_DK, RET_DV), F32)
    moe_p, totals_p, (s_p, k_p, v_p, ik_p) = _mixer(
        x_prompt.reshape(batch * seq, D_MODEL), tab_p, seq // PROJ_TM, zeros_state, seq, RET_CHUNK, None, mixer_w,
        jnp.zeros((N_EXPERTS, LANES), F32))

    tab_s = jnp.tile(_rot_tables(past + jnp.arange(dec_seq)), (PROJ_TM // dec_seq, 1))
    caches = (cache_k[l].reshape(dec_batch, past, LANES), cache_v[l].reshape(dec_batch, past, LANES), cache_idx_k[l])
    moe_s, totals, (s_s, k_s, v_s, ik_s) = _mixer(
        x_sample.reshape(dec_batch * dec_seq, D_MODEL), tab_s, 1, state_ret[l], dec_seq, dec_seq, caches, mixer_w,
        totals_p)

    y_p, y_s = _moe([moe_p, moe_s], totals, *moe_w)

    kv = (DSA_KV_HEADS, DSA_HEAD_DIM)
    return (y_p.reshape(batch, seq, D_MODEL), y_s.reshape(dec_batch, dec_seq, D_MODEL),
            s_p[None], k_p.reshape(1, batch, seq, *kv), v_p.reshape(1, batch, seq, *kv),
            ik_p.reshape(1, batch, seq, IDX_DIM),
            s_s[None], k_s.reshape(1, dec_batch, dec_seq, *kv), v_s.reshape(1, dec_batch, dec_seq, *kv),
            ik_s.reshape(1, dec_batch, dec_seq, IDX_DIM))
```

```python
import functools

import numpy as np
import jax
import jax.numpy as jnp
from jax import lax
from jax.experimental import pallas as pl
from jax.experimental.pallas import tpu as pltpu
from jax.experimental.pallas import tpu_sc as plsc

F32 = jnp.float32
BF16 = jnp.bfloat16
I32 = jnp.int32

D_MODEL = 1024
CHUNK = 64
RET_HEADS = 4
RET_DK = 128
RET_DV = 256
RET_ROPE_BASE = 10000.0
DSA_HEADS = 8
DSA_KV_HEADS = 2
DSA_HEAD_DIM = 64
IDX_HEADS = 8
IDX_DIM = 64
DSA_TOPK = 256
ROPE_THETA = 500000.0
N_EXPERTS = 32
MOE_TOP_K = 4
D_FF = 1024
SWIGLU_ALPHA = 1.702
SWIGLU_LIMIT = 7.0
LN_EPS = 1e-5
GN_EPS = 1e-6
DEPTH = 1
DEEPNORM_ALPHA = (2.0 * DEPTH) ** 0.25
PROJ_WIDTHS = (RET_HEADS * RET_DK, RET_HEADS * RET_DK, RET_HEADS * RET_DV, RET_HEADS * RET_DV,
               DSA_HEADS * DSA_HEAD_DIM, DSA_KV_HEADS * DSA_HEAD_DIM, DSA_KV_HEADS * DSA_HEAD_DIM,
               IDX_HEADS * IDX_DIM, IDX_DIM, IDX_HEADS, D_MODEL, D_MODEL)

assert CHUNK & (CHUNK - 1) == 0

LANES = 128
MIB = 1024 * 1024

OFF_RQ, OFF_RK, OFF_RV, OFF_RG = 0, 512, 1024, 2048
OFF_AQ, OFF_AK, OFF_AV, OFF_IQ, OFF_IKW = 3072, 3584, 3712, 3840, 4352
OFF_GR, OFF_GA, PACKED_COLS = 4480, 5504, 6528
TAB_COLS = 8 * LANES

PROJ_TM = 512
RET_CHUNK = 128
RET_CHUNKS_PER_STEP = 8
DSA_TQ = 256
DSA_ROWS = 512
DSA_GROUP_MAX = 4
SEARCH_UNROLL = 4
DSA_SHORT_EXTENT = 1280
DSA_CHAIN_CELLS = 640 * 1024
MERGE_TM = 512
FFN_BLOCK = 512
SC_WINDOW = 128
QUARTER = D_MODEL // 4
UP_BLOCK = 2 * LANES


def _nt(a, b):
    return lax.dot_general(a, b, (((1,), (1,)), ((), ())), preferred_element_type=F32)


def _mm(a, b):
    return jnp.dot(a, b, preferred_element_type=F32)


def _proj_kernel(x_ref, w_ref, tab_ref, rq_ref, rk_ref, rv_ref, rg_ref, aq_ref, ak_ref, av_ref,
                 iq_ref, ikw_ref, gr_ref, ga_ref, od0_ref):
    xb = x_ref[...].astype(BF16)
    od0_ref[...] = jnp.zeros(od0_ref.shape, od0_ref.dtype)

    def mm(c0, n):
        return _mm(xb, w_ref[:, c0:c0 + n])

    def tab(i):
        return tab_ref[:, i * LANES:(i + 1) * LANES]

    def rot_full(z):
        return z * tab(0) + pltpu.roll(z, 64, 1) * tab(1)

    def rot_part(z, c):
        return z * tab(c) + pltpu.roll(z, LANES - 8, 1) * tab(c + 1) + pltpu.roll(z, 8, 1) * tab(c + 2)

    z = mm(OFF_RQ, 512)
    for h in range(4):
        sl = slice(h * LANES, (h + 1) * LANES)
        rq_ref[:, sl] = rot_full(z[:, sl]).astype(BF16)
    z = mm(OFF_RK, 512)
    for h in range(4):
        sl = slice(h * LANES, (h + 1) * LANES)
        rk_ref[:, sl] = (rot_full(z[:, sl]) * (RET_DK ** -0.5)).astype(BF16)
    for c in range(2):
        rv_ref[:, c * 512:(c + 1) * 512] = mm(OFF_RV + c * 512, 512).astype(BF16)
    for c in range(2):
        rg_ref[:, c * 512:(c + 1) * 512] = mm(OFF_RG + c * 512, 512)
    z = mm(OFF_AQ, 512)
    for h in range(4):
        sl = slice(h * LANES, (h + 1) * LANES)
        aq_ref[:, sl] = rot_part(z[:, sl], 2).astype(BF16)
    z = mm(OFF_AK, 256)
    ak_ref[...] = rot_part(z[:, :LANES], 2)
    av_ref[...] = z[:, LANES:]
    z = mm(OFF_IQ, 512)
    for h in range(4):
        sl = slice(h * LANES, (h + 1) * LANES)
        iq_ref[:, sl] = rot_part(z[:, sl], 2).astype(BF16)
    ikw_ref[...] = rot_part(mm(OFF_IKW, LANES), 5)
    for c in range(2):
        gr_ref[:, c * 512:(c + 1) * 512] = mm(OFF_GR + c * 512, 512)
    for c in range(2):
        ga_ref[:, c * 512:(c + 1) * 512] = mm(OFF_GA + c * 512, 512)


def _rot_tables(pos):
    p = pos.shape[0]
    posf = pos.astype(F32)[:, None]
    ret_f = RET_ROPE_BASE ** (-jnp.linspace(0.0, 1.0, RET_DK // 2, dtype=F32))
    ang = posf * ret_f[None, :]
    c, s = jnp.cos(ang), jnp.sin(ang)
    cos_r = jnp.concatenate([c, c], 1)
    sin_r = jnp.concatenate([-s, s], 1)
    n_rot = DSA_HEAD_DIM // 4
    att_f = ROPE_THETA ** (-jnp.arange(0, n_rot, 2, dtype=F32) / n_rot)
    ang2 = posf * att_f[None, :]
    c2, s2 = jnp.cos(ang2), jnp.sin(ang2)
    half = n_rot // 2
    rest = DSA_HEAD_DIM - 2 * half
    c64 = jnp.concatenate([c2, c2, jnp.ones((p, rest), F32)], 1)
    s1_64 = jnp.concatenate([-s2, jnp.zeros((p, DSA_HEAD_DIM - half), F32)], 1)
    s2_64 = jnp.concatenate([jnp.zeros((p, half), F32), s2, jnp.zeros((p, rest), F32)], 1)
    z64 = jnp.zeros((p, DSA_HEAD_DIM), F32)
    ci = jnp.concatenate([c64, jnp.full((p, IDX_HEADS), IDX_HEADS ** -0.5, F32),
                          jnp.zeros((p, DSA_HEAD_DIM - IDX_HEADS), F32)], 1)
    return jnp.concatenate([cos_r, sin_r,
                            jnp.concatenate([c64, c64], 1), jnp.concatenate([s1_64, s1_64], 1),
                            jnp.concatenate([s2_64, s2_64], 1),
                            ci, jnp.concatenate([s1_64, z64], 1), jnp.concatenate([s2_64, z64], 1)], 1)


def _pack_w_in(w):
    cuts = np.cumsum(PROJ_WIDTHS)[:-1].tolist()
    rq, rk, rv, rg, aq, ak, av, iq, ik, iw, gr, ga = jnp.split(w, cuts, axis=1)
    pad = jnp.zeros((w.shape[0], LANES - IDX_DIM - IDX_HEADS), w.dtype)
    return jnp.concatenate([rq, rk, rv, rg, aq, ak, av, iq, ik, iw, pad, gr, ga], axis=1).astype(BF16)


def _project(x, wp, tab, tab_period):
    n = x.shape[0]
    tm = PROJ_TM
    row = lambda w: pl.BlockSpec((tm, w), lambda i: (i, 0))
    out_shapes = [((n, 512), BF16), ((n, 512), BF16), ((n, 1024), BF16), ((n, 1024), F32),
                  ((n, 512), BF16), ((n, LANES), F32), ((n, LANES), F32), ((n, 512), BF16),
                  ((n, LANES), F32), ((n, 1024), F32), ((n, 1024), F32), ((n, DSA_HEADS * DSA_HEAD_DIM), BF16)]
    return pl.pallas_call(
        _proj_kernel,
        grid=(n // tm,),
        in_specs=[row(D_MODEL),
                  pl.BlockSpec((D_MODEL, PACKED_COLS), lambda i: (0, 0), pipeline_mode=pl.Buffered(1)),
                  pl.BlockSpec((tm, TAB_COLS), lambda i: (i % tab_period, 0))],
        out_specs=[row(s[1]) for s, _ in out_shapes],
        out_shape=[jax.ShapeDtypeStruct(s, d) for s, d in out_shapes],
        compiler_params=pltpu.CompilerParams(dimension_semantics=("parallel",), vmem_limit_bytes=52 * MIB),
        name="proj",
    )(x, wp, tab)


def _ret_kernel(dec_ref, xi_ref, zeta_ref, rq_ref, rk_ref, rv_ref, rg_ref, s0_ref, ret_ref, sout_ref, st_ref,
                *, rows, n_chunk, g_pow):
    j = pl.program_id(1)
    cpad = RET_CHUNK

    @pl.when(j == 0)
    def _():
        st_ref[...] = s0_ref[0]

    def padded(v):
        if rows == cpad:
            return v
        return jnp.concatenate([v, jnp.zeros((cpad - rows, v.shape[1]), v.dtype)], axis=0)

    for c in range(n_chunk):
        rs = slice(c * rows, (c + 1) * rows)
        for h in range(RET_HEADS):
            ks = slice(h * RET_DK, (h + 1) * RET_DK)
            vs = slice(h * RET_DV, (h + 1) * RET_DV)
            q = padded(rq_ref[rs, ks])
            kt = padded(rk_ref[rs, ks].astype(F32)).T
            v = padded(rv_ref[rs, vs])
            s = st_ref[h]
            sc = _mm(q, kt.astype(BF16)) * dec_ref[h]
            o = _mm(sc.astype(BF16), v) + _mm(q, s.astype(BF16)) * xi_ref[h]
            st_ref[h] = g_pow[h] * s + _mm((kt * zeta_ref[h]).astype(BF16), v)
            o = o[:rows]
            mu = jnp.mean(o, axis=-1, keepdims=True)
            d = o - mu
            var = jnp.mean(d * d, axis=-1, keepdims=True)
            gn = d * lax.rsqrt(var + GN_EPS)
            g = rg_ref[rs, vs]
            ret_ref[rs, vs] = (gn * (g * jax.nn.sigmoid(g))).astype(BF16)

    @pl.when(j == pl.num_programs(1) - 1)
    def _():
        sout_ref[0] = st_ref[...]


def _retention(rq, rk, rv, rg, s0, rows_per_stream, rows):
    n = rq.shape[0]
    n_streams = n // rows_per_stream
    n_chunk = min(rows_per_stream // rows, RET_CHUNKS_PER_STEP)
    blk = rows * n_chunk
    nb = rows_per_stream // blk
    gam = 1.0 - 2.0 ** (-5.0 - np.arange(RET_HEADS, dtype=np.float64))
    i = np.arange(RET_CHUNK, dtype=np.float64)
    diff = i[:, None] - i[None, :]
    dec = np.where(diff >= 0, gam[:, None, None] ** np.maximum(diff, 0.0)[None], 0.0)
    xi = gam[:, None, None] ** (i + 1.0)[None, :, None]
    zeta = np.where(i < rows, gam[:, None, None] ** (rows - 1.0 - i)[None, None, :], 0.0)
    g_pow = tuple(float(g ** rows) for g in gam)
    const = lambda shape: pl.BlockSpec(shape, lambda s, j: (0,) * len(shape))
    row = lambda w: pl.BlockSpec((blk, w), lambda s, j: (s * nb + j, 0))
    st = pl.BlockSpec((1, RET_HEADS, RET_DK, RET_DV), lambda s, j: (s, 0, 0, 0))
    return pl.pallas_call(
        functools.partial(_ret_kernel, rows=rows, n_chunk=n_chunk, g_pow=g_pow),
        grid=(n_streams, nb),
        in_specs=[const((RET_HEADS, RET_CHUNK, RET_CHUNK)), const((RET_HEADS, RET_CHUNK, 1)),
                  const((RET_HEADS, 1, RET_CHUNK)), row(512), row(512), row(1024), row(1024), st],
        out_specs=[row(1024), st],
        out_shape=[jax.ShapeDtypeStruct((n, RET_HEADS * RET_DV), BF16),
                   jax.ShapeDtypeStruct((n_streams, RET_HEADS, RET_DK, RET_DV), F32)],
        scratch_shapes=[pltpu.VMEM((RET_HEADS, RET_DK, RET_DV), F32)],
        compiler_params=pltpu.CompilerParams(dimension_semantics=("parallel", "arbitrary"),
                                             vmem_limit_bytes=32 * MIB),
        name="retention",
    )(jnp.asarray(dec, F32), jnp.asarray(xi, F32), jnp.asarray(zeta, F32), rq, rk, rv, rg, s0)


def _dsa_kernel(aq_ref, iq_ref, ikwq_ref, kk_ref, vv_ref, ikk_ref, *rest,
                group, tq, n_keys, row0, limit_const, n_sel):
    o_ref, w_ref, bias_ref = rest[-3:]
    rows = group * tq
    stack = max(c for c in (1, 2, 4) if c == 1 or c * tq * n_keys <= DSA_CHAIN_CELLS)
    search_unroll = 2 * SEARCH_UNROLL if n_keys <= DSA_SHORT_EXTENT else SEARCH_UNROLL
    nsel_f = float(n_sel)
    neg_inf = -jnp.inf

    col = lax.broadcasted_iota(I32, (tq, n_keys), 1)
    if limit_const is None:
        chunk_of_row = jnp.right_shift(lax.broadcasted_iota(I32, (tq, 1), 0) + row0, CHUNK.bit_length() - 1)
        limit = (chunk_of_row + 1) * CHUNK
    else:
        limit = limit_const

    def attend_stream(s, carry):
        bias = bias_ref[pl.ds(pl.multiple_of(s * tq, tq), tq), :]
        kfull = kk_ref[s]
        vfull = vv_ref[s]
        bias_st = jnp.concatenate([bias] * stack, axis=0) if stack > 1 else bias
        for g in range(DSA_KV_HEADS):
            gs = slice(g * DSA_HEAD_DIM, (g + 1) * DSA_HEAD_DIM)
            kg = kfull[:, gs].astype(BF16)
            vg = vfull[:, gs].astype(BF16)
            qs = []
            for p in range(2 * g, 2 * g + 2):
                slab = (aq_ref[s, :, p * LANES:(p + 1) * LANES].astype(F32) * (DSA_HEAD_DIM ** -0.5)).astype(BF16)
                qs += [slab[:, :DSA_HEAD_DIM], slab[:, DSA_HEAD_DIM:]]
            outs = []
            for c in range(len(qs) // stack):
                qh = jnp.concatenate(qs[c * stack:(c + 1) * stack], axis=0) if stack > 1 else qs[c]
                lg = _nt(qh, kg) + bias_st
                m = jnp.max(lg, axis=1, keepdims=True)
                pr = jnp.exp(lg - m)
                den = jnp.sum(pr, axis=1, keepdims=True)
                out = _mm(pr.astype(BF16), vg) / den
                outs += [out[i * tq:(i + 1) * tq] for i in range(stack)]
            for pp in range(2):
                p = 2 * g + pp
                o_ref[s, :, p * LANES:(p + 1) * LANES] = jnp.concatenate(outs[2 * pp:2 * pp + 2], axis=1).astype(BF16)
        return carry

    if n_keys <= n_sel:
        for s in range(group):
            bias_ref[s * tq:(s + 1) * tq, :] = jnp.where(col < limit, 0.0, neg_inf)
        lax.fori_loop(0, group, attend_stream, 0)
        return

    def score_stream(s, carry):
        ikb = ikk_ref[s][:, :IDX_DIM].astype(BF16)
        iww = ikwq_ref[s][:, IDX_DIM:IDX_DIM + IDX_HEADS] * (IDX_DIM ** -0.5)
        acc = jnp.zeros((tq, n_keys), F32)
        for p in range(IDX_HEADS // 2):
            slab = iq_ref[s, :, p * LANES:(p + 1) * LANES]
            for hh in range(2):
                h = 2 * p + hh
                acc = acc + jnp.maximum(_nt(slab[:, hh * IDX_DIM:(hh + 1) * IDX_DIM], ikb), 0.0) * iww[:, h:h + 1]
        w_ref[pl.ds(pl.multiple_of(s * tq, tq), tq), :] = jnp.where(col < limit, acc, neg_inf)
        return carry

    lax.fori_loop(0, group, score_stream, 0)

    sc = w_ref[...]
    pos = jnp.sum(jnp.where(sc >= 0.0, 1.0, 0.0), axis=1, keepdims=True) >= nsel_f
    kk = jnp.where(pos, nsel_f, float(n_keys - n_sel + 1))
    w_ref[...] = jnp.where(pos, sc, -sc)

    def bit_step(i, u):
        cand_u = u | jnp.left_shift(jnp.int32(1), 30 - i)
        cand = pltpu.bitcast(cand_u, F32)
        cnt = jnp.sum(jnp.where(w_ref[...] >= cand, 1.0, 0.0), axis=1, keepdims=True)
        return jnp.where(cnt >= kk, cand_u, u)

    wt_ref = bias_ref.bitcast(BF16)
    wt_ref[pl.ds(0, rows), :] = pltpu.bitcast(pltpu.bitcast(w_ref[...], I32) & jnp.int32(-65536), F32).astype(BF16)

    def bit_step_hi(i, u):
        cand_u = u | jnp.left_shift(jnp.int32(1), 30 - i)
        cand = pltpu.bitcast(cand_u, F32).astype(BF16)
        hit = jnp.where(wt_ref[pl.ds(0, rows), :] >= cand, jnp.ones((), BF16), jnp.zeros((), BF16))
        part = hit[:, :LANES]
        for b in range(1, n_keys // LANES):
            part = part + hit[:, b * LANES:(b + 1) * LANES]
        cnt = jnp.sum(part.astype(F32), axis=1, keepdims=True)
        return jnp.where(cnt >= kk, cand_u, u)

    mag_u = lax.fori_loop(0, 15, bit_step_hi, jnp.zeros((rows, 1), I32), unroll=search_unroll)
    mag_u = lax.fori_loop(15, 31, bit_step, mag_u, unroll=search_unroll)
    mag = pltpu.bitcast(mag_u, F32)
    thr = jnp.where(pos, mag, -mag)

    sc = jnp.where(pos, w_ref[...], -w_ref[...])
    short = jnp.sum(jnp.where(sc >= thr, 1.0, 0.0), axis=1, keepdims=True) < nsel_f
    thr = jnp.where(jnp.logical_and(short, jnp.logical_not(pos)), -pltpu.bitcast(mag_u + 1, F32), thr)
    ge = sc >= thr
    cnt_gt = jnp.sum(jnp.where(sc > thr, 1.0, 0.0), axis=1, keepdims=True)
    cnt_ge = jnp.sum(jnp.where(ge, 1.0, 0.0), axis=1, keepdims=True)
    bias_ref[...] = jnp.where(jnp.logical_and(ge, sc > neg_inf), 0.0, neg_inf)
    excess = jnp.logical_and(cnt_ge > nsel_f, thr > neg_inf)

    @pl.when(jnp.max(jnp.where(excess, 1.0, 0.0)) > 0.0)
    def _():
        need = nsel_f - cnt_gt
        tri = jnp.where(lax.broadcasted_iota(I32, (LANES, LANES), 0) < lax.broadcasted_iota(I32, (LANES, LANES), 1),
                        1.0, 0.0).astype(BF16)
        before = jnp.zeros((rows, 1), F32)
        for b in range(n_keys // LANES):
            sl = slice(b * LANES, (b + 1) * LANES)
            sblk = jnp.where(pos, w_ref[:, sl], -w_ref[:, sl])
            eq = jnp.where(sblk == thr, 1.0, 0.0)
            rank = _mm(eq.astype(BF16), tri) + before
            keep = jnp.logical_or(sblk > thr, jnp.logical_and(sblk == thr, rank < need))
            bias_ref[:, sl] = jnp.where(jnp.logical_and(keep, sblk > neg_inf), 0.0, neg_inf)
            before = before + jnp.sum(eq, axis=1, keepdims=True)

    lax.fori_loop(0, group, attend_stream, 0)


def _dsa_call(aq, iq, ikw, kk, vv, ikk, group, tq, jq, n_keys, limit_const, n_sel, name, prev=None):
    n_streams, rows_per_stream = aq.shape[0], aq.shape[1]
    extra_specs, extra_args, aliases = [], (), {}
    if prev is not None:
        extra_specs, extra_args, aliases = [pl.BlockSpec(memory_space=pl.ANY)], (prev,), {6: 0}
    qrow = lambda a: pl.BlockSpec((group, tq, a.shape[2]), lambda s: (s, jq, 0))
    krow = lambda a: pl.BlockSpec((group, n_keys, a.shape[2]), lambda s: (s, 0, 0))
    width = DSA_HEADS * DSA_HEAD_DIM
    return pl.pallas_call(
        functools.partial(_dsa_kernel, group=group, tq=tq, n_keys=n_keys, row0=jq * tq, limit_const=limit_const,
                          n_sel=n_sel),
        grid=(n_streams // group,),
        in_specs=[qrow(aq), qrow(iq), qrow(ikw), krow(kk), krow(vv), krow(ikk)] + extra_specs,
        out_specs=pl.BlockSpec((group, tq, width), lambda s: (s, jq, 0)),
        out_shape=jax.ShapeDtypeStruct((n_streams, rows_per_stream, width), BF16),
        input_output_aliases=aliases,
        scratch_shapes=[pltpu.VMEM((group * tq, n_keys), F32), pltpu.VMEM((group * tq, n_keys), F32)],
        compiler_params=pltpu.CompilerParams(dimension_semantics=("parallel",), vmem_limit_bytes=56 * MIB),
        name=name,
    )(aq, iq, ikw, kk, vv, ikk, *extra_args)


def _dsa_cached_kernel(aq_ref, iq_ref, ikwq_ref, ck_ref, cv_ref, cik_ref, nk_ref, nv_ref, o_ref,
                       w_ref, bias_ref, kk_ref, vv_ref, ikk_ref, **static):
    past, new = ck_ref.shape[1], nk_ref.shape[1]
    for dst, cache, fresh in ((kk_ref, ck_ref, nk_ref), (vv_ref, cv_ref, nv_ref), (ikk_ref, cik_ref, ikwq_ref)):
        dst[:, :past, :cache.shape[2]] = cache[...]
        dst[:, past:past + new, :] = fresh[...]
        dst[:, past + new:, :] = jnp.zeros((dst.shape[0], dst.shape[1] - past - new, dst.shape[2]), F32)
    _dsa_kernel(aq_ref, iq_ref, ikwq_ref, kk_ref, vv_ref, ikk_ref, o_ref, w_ref, bias_ref, **static)


def _dsa_cached(aq, iq, ikw, cache_k, cache_v, cache_ik, new_k, new_v, tq):
    n_streams, past, _ = cache_k.shape
    n = aq.shape[0]
    per_stream = lambda a: a.reshape(n_streams, tq, a.shape[1])
    aq, iq, ikw, new_k, new_v = per_stream(aq), per_stream(iq), per_stream(ikw), per_stream(new_k), per_stream(new_v)
    limit = past + tq
    n_keys = -(-limit // LANES) * LANES
    group = max(1, min(n_streams, DSA_ROWS // tq, DSA_GROUP_MAX))
    blk = lambda a: pl.BlockSpec((group,) + a.shape[1:], lambda s: (s, 0, 0))
    width = DSA_HEADS * DSA_HEAD_DIM
    return pl.pallas_call(
        functools.partial(_dsa_cached_kernel, group=group, tq=tq, n_keys=n_keys, row0=0, limit_const=limit,
                          n_sel=min(DSA_TOPK, limit // 4)),
        grid=(n_streams // group,),
        in_specs=[blk(a) for a in (aq, iq, ikw, cache_k, cache_v, cache_ik, new_k, new_v)],
        out_specs=pl.BlockSpec((group, tq, width), lambda s: (s, 0, 0)),
        out_shape=jax.ShapeDtypeStruct((n_streams, tq, width), BF16),
        scratch_shapes=[pltpu.VMEM((group * tq, n_keys), F32), pltpu.VMEM((group * tq, n_keys), F32)] +
                       [pltpu.VMEM((group, n_keys, LANES), F32)] * 3,
        compiler_params=pltpu.CompilerParams(dimension_semantics=("parallel",), vmem_limit_bytes=48 * MIB),
        name="dsa_s",
    )(aq, iq, ikw, cache_k, cache_v, cache_ik, new_k, new_v).reshape(n, width)


def _dsa(aq, iq, ikw, kk, vv, ikk, tq, out):
    n_streams, n_keys, _ = kk.shape
    n = aq.shape[0]
    per_stream = lambda a: a.reshape(n_streams, n // n_streams, a.shape[1])
    aq, iq, ikw = per_stream(aq), per_stream(iq), per_stream(ikw)
    nq = n // n_streams // tq
    n_sel = min(DSA_TOPK, n_keys // 4)
    group = max(1, min(n_streams, DSA_ROWS // tq, DSA_GROUP_MAX))
    out = per_stream(out)
    for jq in range(nq):
        out = _dsa_call(aq, iq, ikw, kk, vv, ikk, group, tq, jq, (jq + 1) * tq, None, n_sel, f"dsa_p{jq}", prev=out)
    return out.reshape(n, -1)


def _pack_rows(v):
    q = QUARTER
    bits = lambda x: pltpu.bitcast(x.astype(BF16).astype(F32), I32)
    pair = lambda c: lax.shift_right_logical(bits(v[:, c * q:(c + 1) * q]), 16) | bits(v[:, (c + 1) * q:(c + 2) * q])
    return pair(0), pair(2)


def _unpack_rows(lo, hi):
    parts = []
    for w in (lo, hi):
        parts.append(pltpu.bitcast(lax.shift_left(w, 16), F32))
        parts.append(pltpu.bitcast(w & jnp.int32(-65536), F32))
    return jnp.concatenate(parts, axis=1)


def _layer_norm(v, g, b):
    mu = jnp.mean(v, axis=-1, keepdims=True)
    d = v - mu
    var = jnp.mean(d * d, axis=-1, keepdims=True)
    return d * lax.rsqrt(var + LN_EPS) * g + b


def _merge_kernel(x_ref, ret_ref, od_ref, gr_ref, ga_ref, wr_ref, wd_ref, wo_ref, g1_ref, b1_ref, rwt_ref, rb_ref,
                  tri_ref, cnt0_ref, h1_ref, hlo_ref, hhi_ref, gtm_ref, ek_ref, rk_ref, tot_ref, cnt_ref):
    @pl.when(pl.program_id(0) == 0)
    def _():
        cnt_ref[...] = cnt0_ref[:, 0:1]

    y_ret = _mm(ret_ref[...], wr_ref[...])
    y_dsa = _mm(od_ref[...], wd_ref[...])
    merged = jax.nn.sigmoid(gr_ref[...]) * y_ret + jax.nn.sigmoid(ga_ref[...]) * y_dsa
    mix = _mm(merged.astype(BF16), wo_ref[...])
    h1 = _layer_norm(DEEPNORM_ALPHA * x_ref[...] + mix, g1_ref[...], b1_ref[...])
    h1_ref[...] = h1
    hlo_ref[...], hhi_ref[...] = _pack_rows(h1)

    logits = lax.dot_general(rwt_ref[...], h1, (((1,), (1,)), ((), ())), preferred_element_type=F32,
                             precision=lax.Precision.HIGHEST) + rb_ref[...]
    tm = logits.shape[1]
    e_iota = lax.broadcasted_iota(I32, (N_EXPERTS, tm), 0)
    tops, hots, firsts = [], [], []
    for _ in range(MOE_TOP_K):
        m = jnp.max(logits, axis=0, keepdims=True)
        first = jnp.min(jnp.where(logits == m, e_iota, N_EXPERTS), axis=0, keepdims=True)
        hot = e_iota == first
        tops.append(m)
        hots.append(hot)
        firsts.append(first)
        logits = jnp.where(hot, -jnp.inf, logits)
    exps = [jnp.exp(m - tops[0]) for m in tops]
    den = exps[0] + exps[1] + exps[2] + exps[3]
    sel = jnp.zeros((N_EXPERTS, tm), F32)
    for hot in hots:
        sel = sel + jnp.where(hot, 1.0, 0.0)
    rank = _mm(sel.astype(BF16), tri_ref[...]) + cnt_ref[...]
    ranks = [jnp.sum(jnp.where(hot, rank, 0.0), axis=0, keepdims=True).astype(I32) for hot in hots]
    pad_i = jnp.zeros((8 - MOE_TOP_K, tm), I32)
    ek_ref[...] = jnp.concatenate(firsts + [pad_i], axis=0)
    rk_ref[...] = jnp.concatenate(ranks + [pad_i], axis=0)
    gates = jnp.concatenate([e / den for e in exps] + [jnp.zeros((LANES - MOE_TOP_K, tm), F32)], axis=0)
    gtm_ref[...] = gates.T
    cnt_ref[...] = cnt_ref[...] + jnp.sum(sel, axis=1, keepdims=True)
    tot_ref[...] = jnp.broadcast_to(cnt_ref[...], tot_ref.shape)


def _merge(x, ret, od, gr, ga, wr, wd, wo, g1, b1, rwt, rb, cnt0):
    n = x.shape[0]
    tm = MERGE_TM
    row = lambda w: pl.BlockSpec((tm, w), lambda i: (i, 0))
    const = lambda a: pl.BlockSpec(a.shape, lambda i: (0,) * a.ndim)
    col = pl.BlockSpec((8, tm), lambda i: (0, i))
    tri = jnp.asarray(np.triu(np.ones((tm, tm), np.float32), 1), BF16)
    return pl.pallas_call(
        _merge_kernel,
        grid=(n // tm,),
        in_specs=[row(D_MODEL), row(1024), row(512), row(1024), row(1024), const(wr), const(wd), const(wo),
                  const(g1), const(b1), const(rwt), const(rb), const(tri), const(cnt0)],
        out_specs=[row(D_MODEL), row(QUARTER), row(QUARTER), row(LANES), col, col,
                   pl.BlockSpec((N_EXPERTS, LANES), lambda i: (0, 0))],
        out_shape=[jax.ShapeDtypeStruct((n, D_MODEL), F32), jax.ShapeDtypeStruct((n, QUARTER), I32),
                   jax.ShapeDtypeStruct((n, QUARTER), I32), jax.ShapeDtypeStruct((n, LANES), F32),
                   jax.ShapeDtypeStruct((8, n), I32), jax.ShapeDtypeStruct((8, n), I32),
                   jax.ShapeDtypeStruct((N_EXPERTS, LANES), F32)],
        scratch_shapes=[pltpu.VMEM((N_EXPERTS, 1), F32)],
        compiler_params=pltpu.CompilerParams(dimension_semantics=("arbitrary",), vmem_limit_bytes=48 * MIB),
        name="merge",
    )(x, ret, od, gr, ga, wr, wd, wo, g1, b1, rwt, rb, tri, cnt0)


def _deinterleave_kernel(w_ref, o_ref):
    r = lax.broadcasted_iota(I32, (UP_BLOCK, UP_BLOCK), 0)
    c = lax.broadcasted_iota(I32, (UP_BLOCK, UP_BLOCK), 1)
    src = jnp.where(c < LANES, 2 * c, 2 * (c - LANES) + 1)
    perm = jnp.where(r == src, 1.0, 0.0).astype(BF16)
    for b in range(w_ref.shape[2] // UP_BLOCK):
        sl = slice(b * UP_BLOCK, (b + 1) * UP_BLOCK)
        o_ref[0, :, sl] = _mm(w_ref[0, :, sl].astype(BF16), perm).astype(BF16)


def _deinterleave_w_up(w_up):
    n_e, d_in, d_out = w_up.shape
    cols = d_out
    spec = pl.BlockSpec((1, d_in, cols), lambda e, c: (e, 0, c))
    return pl.pallas_call(
        _deinterleave_kernel,
        grid=(n_e, d_out // cols),
        in_specs=[spec],
        out_specs=spec,
        out_shape=jax.ShapeDtypeStruct(w_up.shape, BF16),
        compiler_params=pltpu.CompilerParams(dimension_semantics=("parallel", "parallel"),
                                             vmem_limit_bytes=40 * MIB),
        name="w_up_prep",
    )(w_up)


def _pos_kernel(off_ref, ek_ref, rk_ref, pos_ref):
    ek = ek_ref[...]
    pos = rk_ref[...]
    for e in range(N_EXPERTS):
        pos = pos + jnp.where(ek == e, off_ref[e], 0)
    pos_ref[...] = pos


def _positions(off, ek, rk):
    n = ek.shape[1]
    tn = min(n, 2048)
    spec = pl.BlockSpec((8, tn), lambda i, off: (0, i))
    return pl.pallas_call(
        _pos_kernel,
        grid_spec=pltpu.PrefetchScalarGridSpec(num_scalar_prefetch=1, grid=(n // tn,), in_specs=[spec, spec],
                                               out_specs=spec),
        out_shape=jax.ShapeDtypeStruct((8, n), I32),
        name="moe_pos",
    )(off, ek, rk)


def _sc_mesh():
    return plsc.VectorSubcoreMesh(core_axis_name="core", subcore_axis_name="subcore")


def _sc_scatter(xs, pos_flat, n_rows):
    width, dtype = xs[0].shape[1], xs[0].dtype
    windows = [x.shape[0] // SC_WINDOW for x in xs]
    nw = sum(windows)

    @functools.partial(pl.kernel, out_type=jax.ShapeDtypeStruct((n_rows, width), dtype), mesh=_sc_mesh())
    def scatter(*refs):
        p_hbm, o_hbm = refs[len(xs)], refs[len(xs) + 1]

        def body(x_vmem, *p_vmem):
            for p in p_vmem:
                pltpu.sync_copy(x_vmem, o_hbm.at[p.at[0]])

        first = 0
        for x_hbm, nw_g in zip(refs[:len(xs)], windows):
            pltpu.emit_pipeline(
                body, grid=(nw_g,),
                in_specs=[pl.BlockSpec((SC_WINDOW, width), lambda i: (i, 0))] +
                         [pl.BlockSpec((1, SC_WINDOW), functools.partial(lambda w0, i: (0, w0 + i), k * nw + first))
                          for k in range(MOE_TOP_K)],
                out_specs=[], core_axis_name=("core", "subcore"), dimension_semantics=(pltpu.PARALLEL,),
            )(x_hbm, *([p_hbm] * MOE_TOP_K))
            first += nw_g

    return scatter(*xs, pos_flat)


def _sc_gather(y, pos_flat):
    m = pos_flat.shape[1]

    @functools.partial(pl.kernel, out_type=jax.ShapeDtypeStruct((m, y.shape[1]), y.dtype), mesh=_sc_mesh())
    def gather(y_hbm, p_hbm, o_hbm):
        def body(p_vmem, o_vmem):
            pltpu.sync_copy(y_hbm.at[p_vmem.at[0]], o_vmem)

        pltpu.emit_pipeline(
            body, grid=(m // SC_WINDOW,),
            in_specs=[pl.BlockSpec((1, SC_WINDOW), lambda i: (0, i))],
            out_specs=[pl.BlockSpec((SC_WINDOW, y.shape[1]), lambda i: (i, 0))],
            core_axis_name=("core", "subcore"), dimension_semantics=(pltpu.PARALLEL,),
        )(p_hbm, o_hbm)

    return gather(y, pos_flat)


def _ffn_kernel(be_ref, nu_ref, xlo_ref, xhi_ref, wup_ref, bup_ref, wdn_ref, bdn_ref, ylo_ref, yhi_ref):
    @pl.when(pl.program_id(0) < nu_ref[0])
    def _():
        x = _unpack_rows(xlo_ref[...], xhi_ref[...]).astype(BF16)
        h = _mm(x, wup_ref[0]) + bup_ref[0]
        acts = []
        for b in range(2 * D_FF // UP_BLOCK):
            glu = jnp.minimum(h[:, b * UP_BLOCK:b * UP_BLOCK + LANES], SWIGLU_LIMIT)
            lin = jnp.clip(h[:, b * UP_BLOCK + LANES:(b + 1) * UP_BLOCK], -SWIGLU_LIMIT, SWIGLU_LIMIT)
            acts.append(glu * jax.nn.sigmoid(SWIGLU_ALPHA * glu) * (lin + 1.0))
        act = jnp.concatenate(acts, axis=1)
        ylo_ref[...], yhi_ref[...] = _pack_rows(_mm(act.astype(BF16), wdn_ref[0]) + bdn_ref[0])


def _ffn(block_expert, n_used, xs_lo, xs_hi, wup, bup, wdn, bdn):
    rows = xs_lo.shape[0]
    blk = FFN_BLOCK
    row = pl.BlockSpec((blk, QUARTER), lambda i, be, nu: (i, 0))
    per_expert = lambda a: pl.BlockSpec((1,) + a.shape[1:], lambda i, be, nu: (be[i], 0, 0))
    return pl.pallas_call(
        _ffn_kernel,
        grid_spec=pltpu.PrefetchScalarGridSpec(
            num_scalar_prefetch=2, grid=(rows // blk,),
            in_specs=[row, row, per_expert(wup), per_expert(bup), per_expert(wdn), per_expert(bdn)],
            out_specs=[row, row]),
        out_shape=[jax.ShapeDtypeStruct((rows, QUARTER), I32), jax.ShapeDtypeStruct((rows, QUARTER), I32)],
        compiler_params=pltpu.CompilerParams(dimension_semantics=("arbitrary",), vmem_limit_bytes=40 * MIB),
        name="moe_ffn",
    )(block_expert, n_used, xs_lo, xs_hi, wup, bup, wdn, bdn)


def _combine_kernel(h1_ref, olo_ref, ohi_ref, gtm_ref, g2_ref, b2_ref, o_ref):
    g = gtm_ref[...]
    y = jnp.zeros(h1_ref.shape, F32)
    for k in range(MOE_TOP_K):
        y = y + g[:, k:k + 1] * _unpack_rows(olo_ref[k], ohi_ref[k])
    o_ref[...] = _layer_norm(DEEPNORM_ALPHA * h1_ref[...] + y, g2_ref[...], b2_ref[...])


def _combine(h1, og_lo, og_hi, gtm, g2, b2, row0):
    n = h1.shape[0]
    n_all = og_lo.shape[0] // MOE_TOP_K
    tm = MERGE_TM
    row = lambda w: pl.BlockSpec((tm, w), lambda i: (i, 0))
    const = lambda a: pl.BlockSpec(a.shape, lambda i: (0,) * a.ndim)
    picked = pl.BlockSpec((MOE_TOP_K, tm, QUARTER), lambda i: (0, i + row0 // tm, 0))
    return pl.pallas_call(
        _combine_kernel,
        grid=(n // tm,),
        in_specs=[row(D_MODEL), picked, picked, row(LANES), const(g2), const(b2)],
        out_specs=row(D_MODEL),
        out_shape=jax.ShapeDtypeStruct((n, D_MODEL), F32),
        compiler_params=pltpu.CompilerParams(dimension_semantics=("parallel",), vmem_limit_bytes=40 * MIB),
        name="moe_combine",
    )(h1, og_lo.reshape(MOE_TOP_K, n_all, QUARTER), og_hi.reshape(MOE_TOP_K, n_all, QUARTER), gtm, g2, b2)


def _moe(groups, totals, wup, bup, wdn, bdn, g2, b2):
    blk = FFN_BLOCK
    sizes = [g[0].shape[0] for g in groups]
    n_all = sum(sizes)
    n_rows = MOE_TOP_K * n_all + N_EXPERTS * blk
    counts = totals[:, 0].astype(I32)
    padded = (counts + blk - 1) // blk * blk
    ends = jnp.cumsum(padded)
    block_start = jnp.arange(n_rows // blk, dtype=I32) * blk
    block_expert = jnp.minimum(jnp.sum((ends[None, :] <= block_start[:, None]).astype(I32), axis=1), N_EXPERTS - 1)
    n_used = (ends[-1:] // blk).astype(I32)
    cat = lambda i, axis: jnp.concatenate([g[i] for g in groups], axis=axis)
    pos = _positions((ends - padded).astype(I32), cat(4, 1), cat(5, 1))
    pos_flat = pos[:MOE_TOP_K].reshape(1, MOE_TOP_K * n_all)
    xs_lo = _sc_scatter([g[1] for g in groups], pos_flat, n_rows)
    xs_hi = _sc_scatter([g[2] for g in groups], pos_flat, n_rows)
    ys_lo, ys_hi = _ffn(block_expert, n_used, xs_lo, xs_hi, wup, bup, wdn, bdn)
    og_lo, og_hi = _sc_gather(ys_lo, pos_flat), _sc_gather(ys_hi, pos_flat)
    starts = np.cumsum([0] + sizes[:-1]).tolist()
    return [_combine(g[0], og_lo, og_hi, g[3], g2, b2, r0) for g, r0 in zip(groups, starts)]


def _mixer(x, pos_tab, tab_period, s0, rows_per_stream, ret_rows, caches, weights, cnt0):
    wp, wr, wd, wo, g1, b1, rwt, rb = weights
    n = x.shape[0]
    n_streams = n // rows_per_stream
    rq, rk, rv, rg, aq, ak, av, iq, ikw, gr, ga, od0 = _project(x, wp, pos_tab, tab_period)
    ret, s_new = _retention(rq, rk, rv, rg, s0, rows_per_stream, ret_rows)
    if caches is None:
        per_stream = lambda a: a.reshape(n_streams, rows_per_stream, a.shape[1])
        od = _dsa(aq, iq, ikw, per_stream(ak), per_stream(av), per_stream(ikw), DSA_TQ, od0)
    else:
        od = _dsa_cached(aq, iq, ikw, *caches, ak, av, rows_per_stream)
    h1, h_lo, h_hi, gtm, ek, rk_, totals = _merge(x, ret, od, gr, ga, wr, wd, wo, g1, b1, rwt, rb, cnt0)
    return (h1, h_lo, h_hi, gtm, ek, rk_), totals, (s_new, ak, av, ikw[:, :IDX_DIM])


def kernel(x_prompt, x_sample, state_ret, cache_k, cache_v, cache_idx_k, w_in, w_ret_o, w_dsa_o, w_o,
           ln1_g, ln1_b, router_w, router_b, w_up, b_up, w_down, b_down, ln2_g, ln2_b):
    assert w_in.shape[0] == DEPTH
    batch, seq, _ = x_prompt.shape
    dec_batch, dec_seq, _ = x_sample.shape
    past = cache_k.shape[2]
    assert seq % PROJ_TM == 0 and seq % DSA_TQ == 0 and PROJ_TM % dec_seq == 0

    l = 0
    mixer_w = (_pack_w_in(w_in[l]), w_ret_o[l].astype(BF16), w_dsa_o[l].astype(BF16), w_o[l].astype(BF16),
               ln1_g[l][None, :], ln1_b[l][None, :], router_w[l].T, router_b[l][:, None])
    moe_w = (
        _deinterleave_w_up(w_up[l]),
        b_up[l].reshape(N_EXPERTS, 2 * D_FF // UP_BLOCK, LANES, 2).transpose(0, 1, 3, 2).reshape(N_EXPERTS, 1, 2 * D_FF),
        w_down[l].astype(BF16), b_down[l][:, None, :], ln2_g[l][None, :], ln2_b[l][None, :])

    tab_p = _rot_tables(jnp.arange(seq))
    zeros_state = jnp.zeros((batch, RET_HEADS, RET_DK, RET_DV), F32)
    moe_p, totals_p, (s_p, k_p, v_p, ik_p) = _mixer(
        x_prompt.reshape(batch * seq, D_MODEL), tab_p, seq // PROJ_TM, zeros_state, seq, RET_CHUNK, None, mixer_w,
        jnp.zeros((N_EXPERTS, LANES), F32))

    tab_s = jnp.tile(_rot_tables(past + jnp.arange(dec_seq)), (PROJ_TM // dec_seq, 1))
    caches = (cache_k[l].reshape(dec_batch, past, LANES), cache_v[l].reshape(dec_batch, past, LANES), cache_idx_k[l])
    moe_s, totals, (s_s, k_s, v_s, ik_s) = _mixer(
        x_sample.reshape(dec_batch * dec_seq, D_MODEL), tab_s, 1, state_ret[l], dec_seq, dec_seq, caches, mixer_w,
        totals_p)

    y_p, y_s = _moe([moe_p, moe_s], totals, *moe_w)

    kv = (DSA_KV_HEADS, DSA_HEAD_DIM)
    return (y_p.reshape(batch, seq, D_MODEL), y_s.reshape(dec_batch, dec_seq, D_MODEL),
            s_p[None], k_p.reshape(1, batch, seq, *kv), v_p.reshape(1, batch, seq, *kv),
            ik_p.reshape(1, batch, seq, IDX_DIM),
            s_s[None], k_s.reshape(1, dec_batch, dec_seq, *kv), v_s.reshape(1, dec_batch, dec_seq, *kv),
            ik_s.reshape(1, dec_batch, dec_seq, IDX_DIM))
```
